```python
import math
import jax, jax.numpy as jnp
from jax import lax
import numpy as np

D_MODEL = 1024
BATCH = 8
SEQ = 2048
DEPTH = 4
DEC_BATCH = 128
DEC_SEQ = 4
PAST_LEN = 16384
PAGE_SIZE = 128

N_META = 16
N_EVEN = (DEPTH + 1) // 2
N_ODD = DEPTH // 2
EPS = 1e-6
D_A = D_MODEL // 2
H_A = 8
HD_A = D_A // H_A
CONV_W = 4
RG_C = 8.0
H_B = 4
D_B = D_MODEL // 2
DK_B = D_B // H_B
DV_B = D_B // H_B
HGRN_CHUNK = 64
D_IN_EVEN = 2 * D_A + 4 * D_B
SPLITS_EVEN = (D_A, 2 * D_A, 2 * D_A + D_B, 2 * D_A + 2 * D_B, 2 * D_A + 3 * D_B)
S5_GC = 16
S5_G = D_MODEL // S5_GC
S5_P = 64
D_FF = 2816
N_EXPERTS = 8
TOP_K = 2
D_FF_EXPERT = 1408
MOE_BLOCK = 128

kernel_name = 'hybrid_rglru_hgrn2_s5_moe_step'


def rmsnorm(x, g):
    x32 = x.astype(jnp.float32)
    y = x32 * lax.rsqrt(jnp.mean(x32 * x32, axis=-1, keepdims=True) + EPS)
    return y.astype(x.dtype) * g


def swiglu(x, w_gu, w_down):
    g, u = jnp.split(x @ w_gu, 2, axis=-1)
    return (jax.nn.silu(g) * u) @ w_down


def _lin_combine(e1, e2):
    a1, b1 = e1
    a2, b2 = e2
    return a1 * a2, a2 * b1 + b2


def _cplx_combine(e1, e2):
    a1r, a1i, b1r, b1i = e1
    a2r, a2i, b2r, b2i = e2
    return (a2r * a1r - a2i * a1i, a2r * a1i + a2i * a1r,
            a2r * b1r - a2i * b1i + b2r, a2r * b1i + a2i * b1r + b2i)


def causal_conv(x, buf, w, b):
    t = x.shape[1]
    xp = jnp.concatenate([buf.astype(x.dtype), x], axis=1)
    y = sum(w[k] * xp[:, k:k + t] for k in range(CONV_W)) + b
    return y, xp[:, t:]


def rg_lru(x, h0, w_a, b_a, w_x, b_x, lam):
    bsz, t, _ = x.shape
    x32 = x.astype(jnp.float32)
    xh = x32.reshape(bsz, t, H_A, HD_A)
    r = jax.nn.sigmoid(jnp.einsum('bthi,hij->bthj', xh, w_a.astype(jnp.float32)).reshape(bsz, t, D_A) + b_a.astype(jnp.float32))
    ig = jax.nn.sigmoid(jnp.einsum('bthi,hij->bthj', xh, w_x.astype(jnp.float32)).reshape(bsz, t, D_A) + b_x.astype(jnp.float32))
    log_a = -RG_C * r * jax.nn.softplus(-lam.astype(jnp.float32))
    a = jnp.exp(log_a)
    u = jnp.sqrt(-jnp.expm1(2.0 * log_a)) * ig * x32
    u = u.at[:, 0].add(a[:, 0] * h0.astype(jnp.float32))
    _, h = lax.associative_scan(_lin_combine, (a, u), axis=1)
    return h, h[:, -1]


def gla_chunk(q, logf, k, v, s0):
    n = q.shape[2]
    g = jnp.cumsum(logf, axis=2)
    causal = jnp.tril(jnp.ones((n, n), dtype=bool))[:, :, None]
    diff = g[:, :, :, None, :] - g[:, :, None, :, :]
    decay = jnp.exp(jnp.where(causal, diff, -jnp.inf))
    att = jnp.einsum('bhik,bhjk,bhijk->bhij', q, k, decay)
    o = jnp.einsum('bhij,bhjv->bhiv', att, v) + jnp.einsum('bhik,bhkv->bhiv', q * jnp.exp(g), s0)
    g_last = g[:, :, -1:, :]
    s_new = jnp.exp(g_last[:, :, 0, :])[..., None] * s0 + jnp.einsum('bhjk,bhjv->bhkv', k * jnp.exp(g_last - g), v)
    return o, s_new


def hgrn2(q, logf, k, v, s0, n_lead):
    bsz, nh, t, _ = q.shape
    outs = []
    s = s0
    if n_lead > 0:
        o_lead, s = gla_chunk(q[:, :, :n_lead], logf[:, :, :n_lead], k[:, :, :n_lead], v[:, :, :n_lead], s)
        outs.append(o_lead)
    rest = t - n_lead
    c = min(HGRN_CHUNK, rest)
    n = -(-rest // c)
    pad = n * c - rest

    def prep(z):
        z = jnp.pad(z[:, :, n_lead:], ((0, 0), (0, 0), (0, pad), (0, 0)))
        return z.reshape(bsz, nh, n, c, z.shape[-1]).transpose(2, 0, 1, 3, 4)

    def step(carry, blk):
        o_blk, carry = gla_chunk(blk[0], blk[1], blk[2], blk[3], carry)
        return carry, o_blk

    s, o = lax.scan(step, s, (prep(q), prep(logf), prep(k), prep(v)))
    o = o.transpose(1, 2, 0, 3, 4).reshape(bsz, nh, n * c, DV_B)[:, :, :rest]
    outs.append(o)
    return jnp.concatenate(outs, axis=2), s


def s5(u, h0r, h0i, lam_re, lam_im, log_dt, b_re, b_im, c_re, c_im, d_skip):
    bsz, t, _ = u.shape
    u32 = u.astype(jnp.float32).reshape(bsz, t, S5_G, S5_GC)
    lr = lam_re.astype(jnp.float32)
    li = lam_im.astype(jnp.float32)
    dt = jnp.exp(log_dt.astype(jnp.float32))[:, None]
    mag = jnp.exp(lr * dt)
    ar = mag * jnp.cos(li * dt)
    ai = mag * jnp.sin(li * dt)
    den = lr * lr + li * li
    cr = ((ar - 1.0) * lr + ai * li) / den
    ci = (ai * lr - (ar - 1.0) * li) / den
    bur = jnp.einsum('btgc,gpc->btgp', u32, b_re.astype(jnp.float32))
    bui = jnp.einsum('btgc,gpc->btgp', u32, b_im.astype(jnp.float32))
    xr = cr * bur - ci * bui
    xi = cr * bui + ci * bur
    h0r = h0r.astype(jnp.float32)
    h0i = h0i.astype(jnp.float32)
    xr = xr.at[:, 0].add(ar * h0r - ai * h0i)
    xi = xi.at[:, 0].add(ar * h0i + ai * h0r)
    a_r = jnp.broadcast_to(ar, xr.shape)
    a_i = jnp.broadcast_to(ai, xr.shape)
    _, _, hr, hi = lax.associative_scan(_cplx_combine, (a_r, a_i, xr, xi), axis=1)
    y = (jnp.einsum('btgp,gcp->btgc', hr, c_re.astype(jnp.float32))
         - jnp.einsum('btgp,gcp->btgc', hi, c_im.astype(jnp.float32))
         + d_skip.astype(jnp.float32).reshape(S5_G, S5_GC) * u32)
    return y.reshape(bsz, t, D_MODEL), hr[:, -1], hi[:, -1]


def moe_ffn(h, w_router, w_gu, w_down):
    bsz, t, d = h.shape
    n = bsz * t
    m = n * TOP_K
    x = h.reshape(n, d)
    probs = jax.nn.softmax(x.astype(jnp.float32) @ w_router.astype(jnp.float32), axis=-1)
    gate, idx = lax.top_k(probs, TOP_K)
    gate = gate / jnp.sum(gate, axis=-1, keepdims=True)
    e = idx.reshape(m).astype(jnp.int32)
    tok = jnp.repeat(jnp.arange(n, dtype=jnp.int32), TOP_K)
    wgt = gate.reshape(m)
    order = jnp.argsort(e, stable=True)
    e_s, tok_s, w_s = e[order], tok[order], wgt[order]
    counts = jax.ops.segment_sum(jnp.ones((m,), jnp.int32), e, num_segments=N_EXPERTS)
    padded = (counts + MOE_BLOCK - 1) // MOE_BLOCK * MOE_BLOCK
    pad_end = jnp.cumsum(padded)
    pad_start = pad_end - padded
    start = jnp.cumsum(counts) - counts
    dest = pad_start[e_s] + jnp.arange(m, dtype=jnp.int32) - start[e_s]
    rows = (-(-m // MOE_BLOCK) + N_EXPERTS) * MOE_BLOCK
    n_blocks = rows // MOE_BLOCK
    buf_tok = jnp.full((rows,), n, jnp.int32).at[dest].set(tok_s)
    buf_w = jnp.zeros((rows,), jnp.float32).at[dest].set(w_s)
    blk_e = jnp.minimum(jnp.searchsorted(pad_end, jnp.arange(n_blocks, dtype=jnp.int32) * MOE_BLOCK, side='right'), N_EXPERTS - 1)
    xp = jnp.concatenate([x, jnp.zeros((1, d), x.dtype)], axis=0)
    xb = xp[buf_tok].reshape(n_blocks, MOE_BLOCK, d)

    def expert_block(args):
        xblk, ei = args
        return swiglu(xblk, w_gu[ei], w_down[ei])

    yb = lax.map(expert_block, (xb, blk_e)).reshape(rows, d)
    y = jax.ops.segment_sum(yb * buf_w[:, None].astype(yb.dtype), buf_tok, num_segments=n + 1)[:n]
    return y.reshape(bsz, t, d)


def even_mixer(h, w, i, lb, h0, conv0, s0, n_lead):
    bsz, t, _ = h.shape
    proj = h @ w['even_w_in'][i]
    xa, ga, q, fr, v, gb = jnp.split(proj, SPLITS_EVEN, axis=-1)
    xc, conv_new = causal_conv(xa, conv0, w['rglru_conv_w'][i], w['rglru_conv_b'][i])
    ha, h_last = rg_lru(xc, h0, w['rglru_w_a'][i], w['rglru_b_a'][i], w['rglru_w_x'][i], w['rglru_b_x'][i], w['rglru_lambda'][i])
    ya = ha.astype(h.dtype) * jax.nn.gelu(ga)
    lbf = lb.astype(jnp.float32)
    logf = jnp.logaddexp(jnp.log(lbf), jnp.log1p(-lbf) + jax.nn.log_sigmoid(fr.astype(jnp.float32)))
    k = -jnp.expm1(logf)

    def heads(z):
        return z.reshape(bsz, t, H_B, -1).transpose(0, 2, 1, 3)

    ob, s_new = hgrn2(heads(jax.nn.silu(q).astype(jnp.float32)), heads(logf), heads(k),
                      heads(v.astype(jnp.float32)), s0.astype(jnp.float32), n_lead)
    ob = rmsnorm(ob.transpose(0, 2, 1, 3), w['hgrn2_gnorm'][i].reshape(H_B, DV_B).astype(jnp.float32))
    ob = ob.reshape(bsz, t, D_B).astype(h.dtype) * jax.nn.silu(gb)
    out = jnp.concatenate([ya, ob], axis=-1) @ w['even_w_out'][i]
    return out, h_last, conv_new, s_new


def odd_mixer(h, w, i, re0, im0):
    y, hr, hi = s5(h, re0, im0, w['s5_lam_re'][i], w['s5_lam_im'][i], w['s5_log_dt'][i],
                   w['s5_b_re'][i], w['s5_b_im'][i], w['s5_c_re'][i], w['s5_c_im'][i], w['s5_d'][i])
    z = jax.nn.gelu(y).astype(h.dtype)
    val, gate = jnp.split(z @ w['s5_w_glu'][i], 2, axis=-1)
    return val * jax.nn.sigmoid(gate), hr, hi


def trunk(x, n_lead, h0, conv0, s0, re0, im0, w):
    lb_cum = jnp.cumsum(jax.nn.softmax(w['hgrn2_lb_raw'].astype(jnp.float32), axis=0), axis=0)
    lb_all = lb_cum - lb_cum[:1]
    new_h, new_conv, new_s, new_re, new_im = [], [], [], [], []
    for l in range(DEPTH):
        i = l // 2
        hn = rmsnorm(x, w['norm_mix'][l])
        if l % 2 == 0:
            mix, hl, cl, sl = even_mixer(hn, w, i, lb_all[i], h0[i], conv0[i], s0[i], n_lead)
            new_h.append(hl)
            new_conv.append(cl)
            new_s.append(sl)
            x = x + mix
            x = x + swiglu(rmsnorm(x, w['norm_ffn'][l]), w['ffn_w_gu'][i], w['ffn_w_down'][i])
        else:
            mix, rl, il = odd_mixer(hn, w, i, re0[i], im0[i])
            new_re.append(rl)
            new_im.append(il)
            x = x + mix
            x = x + moe_ffn(rmsnorm(x, w['norm_ffn'][l]), w['moe_w_router'][i], w['moe_w_gu'][i], w['moe_w_down'][i])
    y = rmsnorm(x, w['norm_final'])
    return (y, jnp.stack(new_h).astype(h0.dtype), jnp.stack(new_conv).astype(conv0.dtype),
            jnp.stack(new_s).astype(s0.dtype), jnp.stack(new_re).astype(re0.dtype), jnp.stack(new_im).astype(im0.dtype))


def setup_inputs(seed: int = 0) -> dict:
    key = jax.random.key(seed)
    keys = iter(jax.random.split(key, 48))

    def nrm(shape, scale):
        return scale * jax.random.normal(next(keys), shape, jnp.float32)

    def unif(shape, lo, hi):
        return jax.random.uniform(next(keys), shape, jnp.float32, lo, hi)

    a0 = unif((N_EVEN, D_A), 0.9, 0.999)
    s5_lam_im = jnp.pi * jnp.arange(S5_P, dtype=jnp.float32) + nrm((N_ODD, S5_G, S5_P), 0.01)
    return {
        'x_prompt': nrm((BATCH, SEQ, D_MODEL), 1.0),
        'x_sample': nrm((DEC_BATCH, DEC_SEQ, D_MODEL), 1.0),
        'state_rglru_h': nrm((N_EVEN, DEC_BATCH, D_A), 0.5),
        'state_rglru_conv': nrm((N_EVEN, DEC_BATCH, CONV_W - 1, D_A), 1.0),
        'state_hgrn2': nrm((N_EVEN, DEC_BATCH, H_B, DK_B, DV_B), 0.5),
        'state_s5_re': nrm((N_ODD, DEC_BATCH, S5_G, S5_P), 0.1),
        'state_s5_im': nrm((N_ODD, DEC_BATCH, S5_G, S5_P), 0.1),
        'meta_tokens': nrm((N_META, D_MODEL), 1.0),
        'norm_mix': 1.0 + nrm((DEPTH, D_MODEL), 0.05),
        'norm_ffn': 1.0 + nrm((DEPTH, D_MODEL), 0.05),
        'norm_final': 1.0 + nrm((D_MODEL,), 0.05),
        'even_w_in': nrm((N_EVEN, D_MODEL, D_IN_EVEN), D_MODEL ** -0.5),
        'even_w_out': nrm((N_EVEN, D_A + D_B, D_MODEL), (D_A + D_B) ** -0.5),
        'rglru_conv_w': nrm((N_EVEN, CONV_W, D_A), CONV_W ** -0.5),
        'rglru_conv_b': nrm((N_EVEN, D_A), 0.01),
        'rglru_w_a': nrm((N_EVEN, H_A, HD_A, HD_A), HD_A ** -0.5),
        'rglru_b_a': nrm((N_EVEN, D_A), 0.01),
        'rglru_w_x': nrm((N_EVEN, H_A, HD_A, HD_A), HD_A ** -0.5),
        'rglru_b_x': nrm((N_EVEN, D_A), 0.01),
        'rglru_lambda': jnp.log(a0) - jnp.log1p(-a0),
        'hgrn2_lb_raw': nrm((N_EVEN, D_B), 1.0),
        'hgrn2_gnorm': 1.0 + nrm((N_EVEN, D_B), 0.05),
        's5_lam_re': -0.5 + nrm((N_ODD, S5_G, S5_P), 0.01),
        's5_lam_im': s5_lam_im,
        's5_log_dt': unif((N_ODD, S5_G), math.log(1e-3), math.log(1e-1)),
        's5_b_re': nrm((N_ODD, S5_G, S5_P, S5_GC), (2 * S5_GC) ** -0.5),
        's5_b_im': nrm((N_ODD, S5_G, S5_P, S5_GC), (2 * S5_GC) ** -0.5),
        's5_c_re': nrm((N_ODD, S5_G, S5_GC, S5_P), (2 * S5_P) ** -0.5),
        's5_c_im': nrm((N_ODD, S5_G, S5_GC, S5_P), (2 * S5_P) ** -0.5),
        's5_d': nrm((N_ODD, D_MODEL), 1.0),
        's5_w_glu': nrm((N_ODD, D_MODEL, 2 * D_MODEL), D_MODEL ** -0.5),
        'ffn_w_gu': nrm((N_EVEN, D_MODEL, 2 * D_FF), D_MODEL ** -0.5),
        'ffn_w_down': nrm((N_EVEN, D_FF, D_MODEL), D_FF ** -0.5),
        'moe_w_router': nrm((N_ODD, D_MODEL, N_EXPERTS), D_MODEL ** -0.5),
        'moe_w_gu': nrm((N_ODD, N_EXPERTS, D_MODEL, 2 * D_FF_EXPERT), D_MODEL ** -0.5),
        'moe_w_down': nrm((N_ODD, N_EXPERTS, D_FF_EXPERT, D_MODEL), D_FF_EXPERT ** -0.5),
    }


def reference(x_prompt, x_sample, state_rglru_h, state_rglru_conv, state_hgrn2, state_s5_re, state_s5_im,
              meta_tokens, norm_mix, norm_ffn, norm_final, even_w_in, even_w_out,
              rglru_conv_w, rglru_conv_b, rglru_w_a, rglru_b_a, rglru_w_x, rglru_b_x, rglru_lambda,
              hgrn2_lb_raw, hgrn2_gnorm, s5_lam_re, s5_lam_im, s5_log_dt, s5_b_re, s5_b_im, s5_c_re, s5_c_im,
              s5_d, s5_w_glu, ffn_w_gu, ffn_w_down, moe_w_router, moe_w_gu, moe_w_down):
    w = {
        'norm_mix': norm_mix, 'norm_ffn': norm_ffn, 'norm_final': norm_final,
        'even_w_in': even_w_in, 'even_w_out': even_w_out,
        'rglru_conv_w': rglru_conv_w, 'rglru_conv_b': rglru_conv_b,
        'rglru_w_a': rglru_w_a, 'rglru_b_a': rglru_b_a, 'rglru_w_x': rglru_w_x, 'rglru_b_x': rglru_b_x,
        'rglru_lambda': rglru_lambda, 'hgrn2_lb_raw': hgrn2_lb_raw, 'hgrn2_gnorm': hgrn2_gnorm,
        's5_lam_re': s5_lam_re, 's5_lam_im': s5_lam_im, 's5_log_dt': s5_log_dt,
        's5_b_re': s5_b_re, 's5_b_im': s5_b_im, 's5_c_re': s5_c_re, 's5_c_im': s5_c_im,
        's5_d': s5_d, 's5_w_glu': s5_w_glu, 'ffn_w_gu': ffn_w_gu, 'ffn_w_down': ffn_w_down,
        'moe_w_router': moe_w_router, 'moe_w_gu': moe_w_gu, 'moe_w_down': moe_w_down,
    }
    bp = x_prompt.shape[0]
    meta = jnp.broadcast_to(meta_tokens.astype(x_prompt.dtype)[None], (bp, N_META, D_MODEL))
    xm = jnp.concatenate([meta, x_prompt], axis=1)
    z_h = jnp.zeros((N_EVEN, bp) + state_rglru_h.shape[2:], state_rglru_h.dtype)
    z_conv = jnp.zeros((N_EVEN, bp) + state_rglru_conv.shape[2:], state_rglru_conv.dtype)
    z_s = jnp.zeros((N_EVEN, bp) + state_hgrn2.shape[2:], state_hgrn2.dtype)
    z_re = jnp.zeros((N_ODD, bp) + state_s5_re.shape[2:], state_s5_re.dtype)
    z_im = jnp.zeros((N_ODD, bp) + state_s5_im.shape[2:], state_s5_im.dtype)
    yp, p_h, p_conv, p_s, p_re, p_im = trunk(xm, N_META, z_h, z_conv, z_s, z_re, z_im, w)
    y_prompt = yp[:, N_META:]
    y_sample, s_h, s_conv, s_s, s_re, s_im = trunk(x_sample, 0, state_rglru_h, state_rglru_conv, state_hgrn2,
                                                   state_s5_re, state_s5_im, w)
    return (y_prompt, y_sample, p_h, p_conv, p_s, p_re, p_im, s_h, s_conv, s_s, s_re, s_im)
```

```python
import functools

import jax
import jax.numpy as jnp
from jax import lax
from jax.experimental import pallas as pl
from jax.experimental.pallas import tpu as pltpu

F32 = jnp.float32
BF16 = jnp.bfloat16

EPS = 1e-6
N_META = 16
CONV_W = 4
RG_C = 8.0
H_A = 8
H_B = 4
S5_GC = 16
S5_P = 64
TOP_K = 2

V7X_VMEM_LIMIT_BYTES = 56 * 1024 * 1024
SUBLANES = 8
LANES = 128
HGRN_ROWS = 128
S5_CH_CHUNK = 256
HGRN_EXP_CLAMP = 80.0


def _params(*sem):
    return pltpu.CompilerParams(dimension_semantics=sem, vmem_limit_bytes=V7X_VMEM_LIMIT_BYTES)


def _pick_tile(n, target, mult):
    best = None
    for d in range(mult, min(n, target) + 1, mult):
        if n % d == 0:
            best = d
    assert best is not None, (n, target, mult)
    return best


def _rms(x, g):
    ms = jnp.mean(x * x, axis=-1, keepdims=True)
    return x * lax.rsqrt(ms + EPS) * g


def _dot(a, b):
    return jnp.dot(a, b, preferred_element_type=F32)


def _norm_matmul_kernel(x_ref, g_ref, w_ref, o_ref, xn_ref):
    @pl.when(pl.program_id(1) == 0)
    def _():
        xn_ref[...] = _rms(x_ref[...], g_ref[...]).astype(BF16)

    o_ref[...] = _dot(xn_ref[...], w_ref[...])


def norm_matmul(x, g, w, tm, tn):
    n, d = x.shape
    nout = w.shape[1]
    return pl.pallas_call(
        _norm_matmul_kernel,
        out_shape=jax.ShapeDtypeStruct((n, nout), F32),
        grid=(n // tm, nout // tn),
        in_specs=[
            pl.BlockSpec((tm, d), lambda i, j: (i, 0)),
            pl.BlockSpec((1, d), lambda i, j: (0, 0)),
            pl.BlockSpec((d, tn), lambda i, j: (0, j)),
        ],
        out_specs=pl.BlockSpec((tm, tn), lambda i, j: (i, j)),
        scratch_shapes=[pltpu.VMEM((tm, d), BF16)],
        compiler_params=_params("parallel", "arbitrary"),
        name="norm_matmul",
    )(x, g.reshape(1, d), w)


def _norm_swiglu_kernel(x_ref, g_ref, wg_ref, wu_ref, o_ref, xn_ref):
    @pl.when(pl.program_id(1) == 0)
    def _():
        xn_ref[...] = _rms(x_ref[...], g_ref[...]).astype(BF16)

    xn = xn_ref[...]
    gate = _dot(xn, wg_ref[...])
    up = _dot(xn, wu_ref[...])
    o_ref[...] = (jax.nn.silu(gate) * up).astype(o_ref.dtype)


def norm_swiglu(x, g, w_gu, tm, tn):
    n, d = x.shape
    dff = w_gu.shape[1] // 2
    nj = dff // tn
    return pl.pallas_call(
        _norm_swiglu_kernel,
        out_shape=jax.ShapeDtypeStruct((n, dff), BF16),
        grid=(n // tm, nj),
        in_specs=[
            pl.BlockSpec((tm, d), lambda i, j: (i, 0)),
            pl.BlockSpec((1, d), lambda i, j: (0, 0)),
            pl.BlockSpec((d, tn), lambda i, j: (0, j)),
            pl.BlockSpec((d, tn), lambda i, j: (0, j + nj)),
        ],
        out_specs=pl.BlockSpec((tm, tn), lambda i, j: (i, j)),
        scratch_shapes=[pltpu.VMEM((tm, d), BF16)],
        compiler_params=_params("parallel", "arbitrary"),
        name="norm_swiglu",
    )(x, g.reshape(1, d), w_gu, w_gu)


def _matmul_residual_kernel(*refs, n_in):
    a_refs = refs[:n_in]
    w_refs = refs[n_in:2 * n_in]
    r_ref = refs[2 * n_in]
    o_ref = refs[2 * n_in + 1]
    acc = r_ref[...]
    for a_ref, w_ref in zip(a_refs, w_refs):
        acc = acc + _dot(a_ref[...], w_ref[...])
    o_ref[...] = acc


def matmul_residual(a_list, w, res, tm):
    n, dout = res.shape
    n_in = len(a_list)
    in_specs = []
    for a in a_list:
        in_specs.append(pl.BlockSpec((tm, a.shape[1]), lambda i: (i, 0)))
    for k, a in enumerate(a_list):
        assert a.shape[1] * n_in == w.shape[0]
        in_specs.append(pl.BlockSpec((a.shape[1], dout), lambda i, k=k: (k, 0)))
    in_specs.append(pl.BlockSpec((tm, dout), lambda i: (i, 0)))
    return pl.pallas_call(
        functools.partial(_matmul_residual_kernel, n_in=n_in),
        out_shape=jax.ShapeDtypeStruct((n, dout), F32),
        grid=(n // tm,),
        in_specs=in_specs,
        out_specs=pl.BlockSpec((tm, dout), lambda i: (i, 0)),
        compiler_params=_params("parallel"),
        name="matmul_residual",
    )(*a_list, *([w] * n_in), res)


def _glu_residual_kernel(z_ref, wv_ref, wg_ref, r_ref, o_ref):
    z = z_ref[...]
    val = _dot(z, wv_ref[...])
    gate = _dot(z, wg_ref[...])
    o_ref[...] = r_ref[...] + val * jax.nn.sigmoid(gate)


def glu_residual(z, w_glu, res, tm, tn):
    n, d = z.shape
    dout = w_glu.shape[1] // 2
    nj = dout // tn
    return pl.pallas_call(
        _glu_residual_kernel,
        out_shape=jax.ShapeDtypeStruct((n, dout), F32),
        grid=(n // tm, nj),
        in_specs=[
            pl.BlockSpec((tm, d), lambda i, j: (i, 0)),
            pl.BlockSpec((d, tn), lambda i, j: (0, j)),
            pl.BlockSpec((d, tn), lambda i, j: (0, j + nj)),
            pl.BlockSpec((tm, tn), lambda i, j: (i, j)),
        ],
        out_specs=pl.BlockSpec((tm, tn), lambda i, j: (i, j)),
        compiler_params=_params("parallel", "arbitrary"),
        name="glu_residual",
    )(z, w_glu, w_glu, res)


def _rmsnorm_kernel(x_ref, g_ref, o_ref):
    o_ref[...] = _rms(x_ref[...], g_ref[...])


def rmsnorm_rows(x, g, tm):
    n, d = x.shape
    return pl.pallas_call(
        _rmsnorm_kernel,
        out_shape=jax.ShapeDtypeStruct((n, d), F32),
        grid=(n // tm,),
        in_specs=[pl.BlockSpec((tm, d), lambda i: (i, 0)), pl.BlockSpec((1, d), lambda i: (0, 0))],
        out_specs=pl.BlockSpec((tm, d), lambda i: (i, 0)),
        compiler_params=_params("parallel"),
        name="final_rmsnorm",
    )(x, g.reshape(1, d))


def _rglru_kernel(xa_ref, ga_ref, conv0_ref, h0_ref, cw_ref, cb_ref, wg_ref, bg_ref, lam_ref,
                  ya_ref, hlast_ref, convnew_ref, xpad_ref, a_ref, u_ref, *, tb, rb):
    i = pl.program_id(0)
    rows = tb * rb
    tail = (CONV_W - 1) * rb
    c = xa_ref.shape[-1]

    @pl.when(i == 0)
    def _():
        xpad_ref[0:tail, :] = conv0_ref[...]
        hlast_ref[...] = h0_ref[...]

    @pl.when(i > 0)
    def _():
        xpad_ref[0:tail, :] = xpad_ref[rows:rows + tail, :]

    xpad_ref[tail:tail + rows, :] = xa_ref[...]
    xc = cb_ref[...]
    for k in range(CONV_W):
        xc = xc + cw_ref[k:k + 1, :] * xpad_ref[k * rb:k * rb + rows, :]

    gates = _dot(xc.astype(BF16), wg_ref[...]) + bg_ref[...]
    r = jax.nn.sigmoid(gates[:, :c])
    ig = jax.nn.sigmoid(gates[:, c:])
    log_a = (-RG_C) * r * jax.nn.softplus(-lam_ref[...])
    a = jnp.exp(log_a)
    mult = jnp.sqrt(-jnp.tanh(log_a) * (a * a + 1.0))
    a_ref[...] = a
    u_ref[...] = mult * ig * xc

    lc = max(LANES, (2 * SUBLANES * LANES * SUBLANES) // rb // LANES * LANES)
    lc = min(lc, c)
    for c0 in range(0, c, lc):
        def body(t, h, c0=c0):
            sl = pl.ds(pl.multiple_of(t * rb, rb), rb)
            h = a_ref[sl, c0:c0 + lc] * h + u_ref[sl, c0:c0 + lc]
            u_ref[sl, c0:c0 + lc] = h
            return h

        hlast_ref[:, c0:c0 + lc] = lax.fori_loop(0, tb, body, hlast_ref[:, c0:c0 + lc])

    ya_ref[...] = (u_ref[...] * jax.nn.gelu(ga_ref[...])).astype(ya_ref.dtype)
    convnew_ref[...] = xpad_ref[rows:rows + tail, :]


def rglru(proj, n_rows, conv0, h0, cw, cb, wg, bg, lam, tb, rb):
    c = h0.shape[1]
    rows = tb * rb
    tail = (CONV_W - 1) * rb
    full = lambda shape: pl.BlockSpec(shape, lambda i: (0,) * len(shape))
    return pl.pallas_call(
        functools.partial(_rglru_kernel, tb=tb, rb=rb),
        out_shape=(
            jax.ShapeDtypeStruct((n_rows, c), BF16),
            jax.ShapeDtypeStruct((rb, c), F32),
            jax.ShapeDtypeStruct((tail, c), F32),
        ),
        grid=(n_rows // rows,),
        in_specs=[
            pl.BlockSpec((rows, c), lambda i: (i, 0)),
            pl.BlockSpec((rows, c), lambda i: (i, 1)),
            full((tail, c)), full((rb, c)), full((CONV_W, c)), full((1, c)),
            full((c, 2 * c)), full((1, 2 * c)), full((1, c)),
        ],
        out_specs=(
            pl.BlockSpec((rows, c), lambda i: (i, 0)),
            full((rb, c)),
            full((tail, c)),
        ),
        scratch_shapes=[
            pltpu.VMEM((rows + tail, c), F32),
            pltpu.VMEM((rows, c), F32),
            pltpu.VMEM((rows, c), F32),
        ],
        compiler_params=_params("arbitrary"),
        name="rglru",
    )(proj, proj, conv0, h0, cw, cb.reshape(1, c), wg, bg.reshape(1, 2 * c), lam.reshape(1, c))


def _hgrn2_kernel(q_ref, f_ref, v_ref, gb_ref, s0_ref, lbraw_ref, gn_ref, ob_ref, s_ref,
                  *, layer, tb, nb, nh):
    i = pl.program_id(1)
    m = HGRN_ROWS
    tc = m // nb
    d = q_ref.shape[-1]
    dk = d // nh

    @pl.when(i == 0)
    def _():
        s_ref[...] = s0_ref[...]

    p = jax.nn.softmax(lbraw_ref[...], axis=0)
    cum = p[0:1, :]
    for r in range(1, layer + 1):
        cum = cum + p[r:r + 1, :]
    lb = cum - p[0:1, :]
    log_lb = jnp.log(lb)
    log_1mlb = jnp.log1p(-lb)

    row = lax.broadcasted_iota(jnp.int32, (m, m), 0)
    col = lax.broadcasted_iota(jnp.int32, (m, m), 1)
    mask = ((row & (nb - 1)) == (col & (nb - 1))) & (col <= row)
    bid = lax.broadcasted_iota(jnp.int32, (m, dk), 0) & (nb - 1)

    for c in range(tb // tc):
        tsl = slice(c * tc, (c + 1) * tc)
        q = jax.nn.silu(q_ref[tsl].reshape(m, d))
        fr = f_ref[tsl].reshape(m, d)
        v = v_ref[tsl].reshape(m, d).astype(BF16)
        gb = gb_ref[tsl].reshape(m, d)
        logf = jnp.logaddexp(log_lb, log_1mlb + jax.nn.log_sigmoid(fr))
        k = -jnp.tanh(0.5 * logf) * (jnp.exp(logf) + 1.0)
        slabs = [logf[0:nb]]
        for t in range(1, tc):
            slabs.append(slabs[-1] + logf[t * nb:(t + 1) * nb])
        g = jnp.concatenate(slabs, axis=0)
        g_last = jnp.concatenate([slabs[-1]] * tc, axis=0)
        qt = (q * jnp.exp(g)).astype(BF16)
        kt = (k * jnp.exp(jnp.minimum(-g, HGRN_EXP_CLAMP))).astype(BF16)
        ks = (k * jnp.exp(g_last - g)).astype(BF16)
        dec = jnp.exp(g_last)

        outs = []
        for h in range(nh):
            hs = slice(h * dk, (h + 1) * dk)
            qt_h, kt_h, ks_h, v_h = qt[:, hs], kt[:, hs], ks[:, hs], v[:, hs]
            att = lax.dot_general(qt_h, kt_h, (((1,), (1,)), ((), ())), preferred_element_type=F32)
            att = jnp.where(mask, att, 0.0).astype(BF16)
            o = _dot(att, v_h)
            s_h = s_ref[:, h].reshape(nb * dk, dk)
            zero = jnp.zeros_like(qt_h)
            q_exp = jnp.concatenate([jnp.where(bid == b, qt_h, zero) for b in range(nb)], axis=1)
            o = o + _dot(q_exp, s_h.astype(BF16))
            k_exp = jnp.concatenate([jnp.where(bid == b, ks_h, zero) for b in range(nb)], axis=1)
            ds = lax.dot_general(k_exp, v_h, (((0,), (0,)), ((), ())), preferred_element_type=F32)
            dec_t = dec[:, hs].T
            for b in range(nb):
                s_ref[b, h] = dec_t[:, b:b + 1] * s_h[b * dk:(b + 1) * dk] + ds[b * dk:(b + 1) * dk]
            outs.append(_rms(o, gn_ref[:, hs]))
        ob = jnp.concatenate(outs, axis=1) * jax.nn.silu(gb)
        ob_ref[c * m:(c + 1) * m, :] = ob.astype(ob_ref.dtype)


def hgrn2(proj3, s0, lb_raw, gnorm, layer, n_t, tb, nb):
    bsz, nh, dk, _ = s0.shape
    d = nh * dk
    off = proj3.shape[-1] // d - 4
    nbb = bsz // nb
    nt = n_t // tb
    col = lambda k: pl.BlockSpec((tb, nb, d), lambda j, i, k=k: (i, j, k + off))
    return pl.pallas_call(
        functools.partial(_hgrn2_kernel, layer=layer, tb=tb, nb=nb, nh=nh),
        out_shape=(
            jax.ShapeDtypeStruct((n_t * bsz, d), BF16),
            jax.ShapeDtypeStruct(s0.shape, F32),
        ),
        grid=(nbb, nt),
        in_specs=[
            col(0), col(1), col(2), col(3),
            pl.BlockSpec((nb, nh, dk, dk), lambda j, i: (j, 0, 0, 0)),
            pl.BlockSpec(lb_raw.shape, lambda j, i: (0, 0)),
            pl.BlockSpec((1, d), lambda j, i: (0, 0)),
        ],
        out_specs=(
            pl.BlockSpec((tb * nb, d), lambda j, i: (j * nt + i, 0)),
            pl.BlockSpec((nb, nh, dk, dk), lambda j, i: (j, 0, 0, 0)),
        ),
        compiler_params=_params("arbitrary", "arbitrary"),
        name="hgrn2",
    )(proj3, proj3, proj3, proj3, s0, lb_raw, gnorm.reshape(1, d))


def _s5_kernel(x_ref, g_ref, wb_ref, wc_ref, lr_ref, li_ref, ldt_ref, d_ref, h0r_ref, h0i_ref,
               z_ref, hr_ref, hi_ref, xn_ref, xr_ref, xi_ref, cst_ref, *, tb, rb):
    i = pl.program_id(0)
    nch = wb_ref.shape[0]
    cw = wb_ref.shape[1]
    sw = wb_ref.shape[2] // 2

    @pl.when(i == 0)
    def _():
        hr_ref[...] = h0r_ref[...]
        hi_ref[...] = h0i_ref[...]
        lr = lr_ref[...]
        li = li_ref[...]
        dt = jnp.exp(ldt_ref[...])
        mag = jnp.exp(lr * dt)
        ar = mag * jnp.cos(li * dt)
        ai = mag * jnp.sin(li * dt)
        den = lr * lr + li * li
        cst_ref[0:1, :] = ar
        cst_ref[1:2, :] = ai
        cst_ref[2:3, :] = ((ar - 1.0) * lr + ai * li) / den
        cst_ref[3:4, :] = (ai * lr - (ar - 1.0) * li) / den

    xn_ref[...] = _rms(x_ref[...], g_ref[...])

    for c in range(nch):
        cs = slice(c * cw, (c + 1) * cw)
        ss = slice(c * sw, (c + 1) * sw)
        u = xn_ref[:, cs]
        bu = _dot(u.astype(BF16), wb_ref[c])
        bur, bui = bu[:, :sw], bu[:, sw:]
        cr = cst_ref[2:3, ss]
        ci = cst_ref[3:4, ss]
        xr_ref[...] = cr * bur - ci * bui
        xi_ref[...] = cr * bui + ci * bur

        ar = jnp.broadcast_to(cst_ref[0:1, ss], (SUBLANES, sw))
        ai = jnp.broadcast_to(cst_ref[1:2, ss], (SUBLANES, sw))

        def sub_rows(s, carry, ss=ss, ar=ar, ai=ai):
            s0 = pl.multiple_of(s * SUBLANES, SUBLANES)

            def step(t, h):
                hr, hi = h
                r0 = pl.multiple_of(t * rb + s0, SUBLANES)
                nhr = ar * hr - ai * hi + xr_ref[pl.ds(r0, SUBLANES), :]
                nhi = ar * hi + ai * hr + xi_ref[pl.ds(r0, SUBLANES), :]
                xr_ref[pl.ds(r0, SUBLANES), :] = nhr
                xi_ref[pl.ds(r0, SUBLANES), :] = nhi
                return nhr, nhi

            hr, hi = lax.fori_loop(0, tb, step, (hr_ref[pl.ds(s0, SUBLANES), ss], hi_ref[pl.ds(s0, SUBLANES), ss]))
            hr_ref[pl.ds(s0, SUBLANES), ss] = hr
            hi_ref[pl.ds(s0, SUBLANES), ss] = hi
            return carry

        lax.fori_loop(0, rb // SUBLANES, sub_rows, 0)

        hcat = jnp.concatenate([xr_ref[...].astype(BF16), xi_ref[...].astype(BF16)], axis=1)
        y = _dot(hcat, wc_ref[c]) + d_ref[:, cs] * u
        z_ref[:, cs] = jax.nn.gelu(y).astype(z_ref.dtype)


def s5(x, n_rows, g, wb, wc, lr, li, ldt, dskip, h0r, h0i, tb, rb):
    d = x.shape[1]
    ns = h0r.shape[1]
    rows = tb * rb
    sw = wb.shape[2] // 2
    full = lambda shape: pl.BlockSpec(shape, lambda i: (0,) * len(shape))
    return pl.pallas_call(
        functools.partial(_s5_kernel, tb=tb, rb=rb),
        out_shape=(
            jax.ShapeDtypeStruct((n_rows, d), BF16),
            jax.ShapeDtypeStruct((rb, ns), F32),
            jax.ShapeDtypeStruct((rb, ns), F32),
        ),
        grid=(n_rows // rows,),
        in_specs=[
            pl.BlockSpec((rows, d), lambda i: (i, 0)),
            full((1, d)), full(wb.shape), full(wc.shape),
            full((1, ns)), full((1, ns)), full((1, ns)), full((1, d)),
            full((rb, ns)), full((rb, ns)),
        ],
        out_specs=(
            pl.BlockSpec((rows, d), lambda i: (i, 0)),
            full((rb, ns)), full((rb, ns)),
        ),
        scratch_shapes=[
            pltpu.VMEM((rows, d), F32),
            pltpu.VMEM((rows, sw), F32),
            pltpu.VMEM((rows, sw), F32),
            pltpu.VMEM((SUBLANES, ns), F32),
        ],
        compiler_params=_params("arbitrary"),
        name="s5",
    )(x, g.reshape(1, d), wb, wc, lr, li, ldt, dskip.reshape(1, d), h0r, h0i)


def _s5_block_diag_in(b):
    g, p, c = b.shape
    gpc = S5_CH_CHUNK // c
    bt = b.transpose(0, 2, 1).reshape(g // gpc, gpc, c, p)
    out = jnp.einsum('ngcp,gh->ngchp', bt, jnp.eye(gpc, dtype=b.dtype))
    return out.reshape(g // gpc, gpc * c, gpc * p)


def _s5_block_diag_out(cm):
    g, c, p = cm.shape
    gpc = S5_CH_CHUNK // c
    ct = cm.transpose(0, 2, 1).reshape(g // gpc, gpc, p, c)
    out = jnp.einsum('ngpc,gh->ngphc', ct, jnp.eye(gpc, dtype=cm.dtype))
    return out.reshape(g // gpc, gpc * p, gpc * c)


def _moe_kernel(x_ref, g_ref, wrt_ref, wgu_ref, wdn_ref, o_ref, xn_ref, pos_ref, gate_ref, *, sub):
    e = pl.program_id(1)
    tm = x_ref.shape[0]
    ne = wrt_ref.shape[0]
    dff = wdn_ref.shape[1]

    @pl.when(e == 0)
    def _():
        x = x_ref[...]
        xn = _rms(x, g_ref[...])
        xn_ref[...] = xn.astype(BF16)
        logits = lax.dot_general(wrt_ref[...], xn, (((1,), (1,)), ((), ())),
                                 precision=lax.Precision.HIGHEST, preferred_element_type=F32)
        ex = jnp.exp(logits - jnp.max(logits, axis=0, keepdims=True))
        probs = ex / jnp.sum(ex, axis=0, keepdims=True)
        eid = lax.broadcasted_iota(jnp.int32, (ne, tm), 0).astype(F32)
        m1 = jnp.max(probs, axis=0, keepdims=True)
        i1 = jnp.min(jnp.where(probs == m1, eid, float(ne)), axis=0, keepdims=True)
        sel1 = eid == i1
        rest = jnp.where(sel1, -1.0, probs)
        m2 = jnp.max(rest, axis=0, keepdims=True)
        i2 = jnp.min(jnp.where(rest == m2, eid, float(ne)), axis=0, keepdims=True)
        sel2 = eid == i2
        den = m1 + m2
        gate_ref[...] = jnp.where(sel1, m1 / den, 0.0) + jnp.where(sel2, m2 / den, 0.0)
        chosen = jnp.where(sel1, 1.0, jnp.where(sel2, 1.0, 0.0))
        r = lax.broadcasted_iota(jnp.int32, (tm, tm), 0)
        c = lax.broadcasted_iota(jnp.int32, (tm, tm), 1)
        before = jnp.where(r < c, 1.0, 0.0).astype(BF16)
        rank = _dot(chosen.astype(BF16), before)
        pos_ref[...] = jnp.where(chosen > 0.0, rank, -1.0)
        o_ref[...] = x

    pos_e = pos_ref[pl.ds(e, 1), :]
    gate_e = gate_ref[pl.ds(e, 1), :]
    cnt = jnp.sum(jnp.where(pos_e >= 0.0, 1.0, 0.0)).astype(jnp.int32)

    def body(s, carry):
        slot = lax.broadcasted_iota(jnp.int32, (sub, tm), 0) + s * sub
        hit = pos_e == slot.astype(F32)
        onehot = jnp.where(hit, 1.0, 0.0).astype(BF16)
        xs = _dot(onehot, xn_ref[...]).astype(BF16)
        gs = jnp.sum(jnp.where(hit, gate_e, 0.0), axis=1, keepdims=True)
        hgu = _dot(xs, wgu_ref[0])
        act = (jax.nn.silu(hgu[:, :dff]) * hgu[:, dff:]).astype(BF16)
        yb = (_dot(act, wdn_ref[0]) * gs).astype(BF16)
        o_ref[...] += lax.dot_general(onehot, yb, (((0,), (0,)), ((), ())), preferred_element_type=F32)
        return carry

    lax.fori_loop(0, (cnt + sub - 1) // sub, body, 0)


def moe_residual(x, g, w_router_t, w_gu, w_down, tm, sub):
    n, d = x.shape
    ne, _, dff2 = w_gu.shape
    return pl.pallas_call(
        functools.partial(_moe_kernel, sub=sub),
        out_shape=jax.ShapeDtypeStruct((n, d), F32),
        grid=(n // tm, ne),
        in_specs=[
            pl.BlockSpec((tm, d), lambda i, e: (i, 0)),
            pl.BlockSpec((1, d), lambda i, e: (0, 0)),
            pl.BlockSpec((ne, d), lambda i, e: (0, 0)),
            pl.BlockSpec((1, d, dff2), lambda i, e: (e, 0, 0)),
            pl.BlockSpec((1, dff2 // 2, d), lambda i, e: (e, 0, 0)),
        ],
        out_specs=pl.BlockSpec((tm, d), lambda i, e: (i, 0)),
        scratch_shapes=[
            pltpu.VMEM((tm, d), BF16),
            pltpu.VMEM((ne, tm), F32),
            pltpu.VMEM((ne, tm), F32),
        ],
        compiler_params=_params("parallel", "arbitrary"),
        name="moe",
    )(x, g.reshape(1, d), w_router_t, w_gu, w_down)


def kernel(x_prompt, x_sample, state_rglru_h, state_rglru_conv, state_hgrn2, state_s5_re, state_s5_im,
           meta_tokens, norm_mix, norm_ffn, norm_final, even_w_in, even_w_out,
           rglru_conv_w, rglru_conv_b, rglru_w_a, rglru_b_a, rglru_w_x, rglru_b_x, rglru_lambda,
           hgrn2_lb_raw, hgrn2_gnorm, s5_lam_re, s5_lam_im, s5_log_dt, s5_b_re, s5_b_im, s5_c_re, s5_c_im,
           s5_d, s5_w_glu, ffn_w_gu, ffn_w_down, moe_w_router, moe_w_gu, moe_w_down):
    bp, tp0, d = x_prompt.shape
    bs, ts, _ = x_sample.shape
    tp = tp0 + N_META
    depth = norm_mix.shape[0]
    d_a = state_rglru_h.shape[-1]
    d_b = d - d_a
    ns = state_s5_re.shape[-2] * state_s5_re.shape[-1]
    assert bp == SUBLANES and bs % SUBLANES == 0 and HGRN_ROWS % bp == 0
    n_p, n_s = tp * bp, ts * bs
    n = n_p + n_s

    meta = jnp.broadcast_to(meta_tokens.astype(x_prompt.dtype)[None], (bp, N_META, d))
    xm = jnp.concatenate([meta, x_prompt], axis=1)
    x = jnp.concatenate([xm.transpose(1, 0, 2).reshape(n_p, d), x_sample.transpose(1, 0, 2).reshape(n_s, d)], axis=0)

    tm = _pick_tile(n, 1024, 128)
    tc_p = HGRN_ROWS // bp
    tb_p = _pick_tile(tp, 64, tc_p)
    nb_s = 32
    assert bs % nb_s == 0 and HGRN_ROWS % nb_s == 0 and ts == HGRN_ROWS // nb_s

    zeros = lambda *shape: jnp.zeros(shape, F32)
    eye_a = jnp.eye(H_A, dtype=F32)
    block_diag = lambda w: jnp.einsum('hij,hg->higj', w, eye_a).reshape(d_a, d_a)

    new = {k: [] for k in ('p_h', 'p_conv', 'p_s', 'p_re', 'p_im', 's_h', 's_conv', 's_s', 's_re', 's_im')}
    for l in range(depth):
        li = l // 2
        if l % 2 == 0:
            proj = norm_matmul(x, norm_mix[l], even_w_in[li].astype(BF16), tm, 1024)
            proj_s = proj[n_p:]
            wg = jnp.concatenate([block_diag(rglru_w_a[li]), block_diag(rglru_w_x[li])], axis=1).astype(BF16)
            bg = jnp.concatenate([rglru_b_a[li], rglru_b_x[li]])
            rg_args = (rglru_conv_w[li], rglru_conv_b[li], wg, bg, rglru_lambda[li])
            ya_p, h_p, conv_p = rglru(proj, n_p, zeros((CONV_W - 1) * bp, d_a), zeros(bp, d_a), *rg_args, tb=tb_p, rb=bp)
            conv0_s = state_rglru_conv[li].transpose(1, 0, 2).reshape((CONV_W - 1) * bs, d_a)
            ya_s, h_s, conv_s = rglru(proj_s, n_s, conv0_s, state_rglru_h[li], *rg_args, tb=ts, rb=bs)
            new['p_h'].append(h_p)
            new['s_h'].append(h_s)
            new['p_conv'].append(conv_p.reshape(CONV_W - 1, bp, d_a).transpose(1, 0, 2))
            new['s_conv'].append(conv_s.reshape(CONV_W - 1, bs, d_a).transpose(1, 0, 2))
            hg_args = (hgrn2_lb_raw, hgrn2_gnorm[li], li)
            ob_p, s_p = hgrn2(proj.reshape(n // bp, bp, proj.shape[1]), zeros(bp, H_B, d_b // H_B, d_b // H_B),
                              *hg_args, n_t=tp, tb=tb_p, nb=bp)
            ob_s, s_s = hgrn2(proj_s[:, 2 * d_a:].reshape(ts, bs, 4 * d_b), state_hgrn2[li],
                              *hg_args, n_t=ts, tb=ts, nb=nb_s)
            ob_s = ob_s.reshape(bs // nb_s, ts, nb_s, d_b).transpose(1, 0, 2, 3).reshape(n_s, d_b)
            new['p_s'].append(s_p)
            new['s_s'].append(s_s)
            ya = jnp.concatenate([ya_p, ya_s], axis=0)
            ob = jnp.concatenate([ob_p, ob_s], axis=0)
            x = matmul_residual([ya, ob], even_w_out[li].astype(BF16), x, tm)
            hff = norm_swiglu(x, norm_ffn[l], ffn_w_gu[li].astype(BF16), tm, ffn_w_gu.shape[2] // 4)
            x = matmul_residual([hff], ffn_w_down[li].astype(BF16), x, tm)
        else:
            wb = jnp.concatenate([_s5_block_diag_in(s5_b_re[li]), _s5_block_diag_in(s5_b_im[li])], axis=2).astype(BF16)
            wc = jnp.concatenate([_s5_block_diag_out(s5_c_re[li]), -_s5_block_diag_out(s5_c_im[li])], axis=1).astype(BF16)
            s5_args = (norm_mix[l], wb, wc, s5_lam_re[li].reshape(1, ns), s5_lam_im[li].reshape(1, ns),
                       jnp.repeat(s5_log_dt[li], S5_P).reshape(1, ns), s5_d[li])
            z_p, re_p, im_p = s5(x, n_p, *s5_args, zeros(bp, ns), zeros(bp, ns), tb=tb_p, rb=bp)
            z_s, re_s, im_s = s5(x[n_p:], n_s, *s5_args, state_s5_re[li].reshape(bs, ns), state_s5_im[li].reshape(bs, ns),
                                 tb=ts, rb=bs)
            new['p_re'].append(re_p.reshape((bp,) + state_s5_re.shape[2:]))
            new['p_im'].append(im_p.reshape((bp,) + state_s5_im.shape[2:]))
            new['s_re'].append(re_s.reshape(state_s5_re.shape[1:]))
            new['s_im'].append(im_s.reshape(state_s5_im.shape[1:]))
            z = jnp.concatenate([z_p, z_s], axis=0)
            x = glu_residual(z, s5_w_glu[li].astype(BF16), x, tm, 512)
            x = moe_residual(x, norm_ffn[l], moe_w_router[li].T, moe_w_gu[li].astype(BF16),
                             moe_w_down[li].astype(BF16), tm, 256)

    y = rmsnorm_rows(x, norm_final, tm)
    y_prompt = y[:n_p].reshape(tp, bp, d)[N_META:].transpose(1, 0, 2)
    y_sample = y[n_p:].reshape(ts, bs, d).transpose(1, 0, 2)
    st = lambda k, ref: jnp.stack(new[k]).astype(ref.dtype)
    return (y_prompt, y_sample,
            st('p_h', state_rglru_h), st('p_conv', state_rglru_conv), st('p_s', state_hgrn2),
            st('p_re', state_s5_re), st('p_im', state_s5_im),
            st('s_h', state_rglru_h), st('s_conv', state_rglru_conv), st('s_s', state_hgrn2),
            st('s_re', state_s5_re), st('s_im', state_s5_im))
```

```python
import functools

import jax
import jax.numpy as jnp
from jax import lax
from jax.experimental import pallas as pl
from jax.experimental.pallas import tpu as pltpu

F32 = jnp.float32
BF16 = jnp.bfloat16

EPS = 1e-6
N_META = 16
CONV_W = 4
RG_C = 8.0
H_A = 8
S5_P = 64

V7X_VMEM_LIMIT_BYTES = 56 * 1024 * 1024
SUBLANES = 8
LANES = 128
V7X_MXU_DIM = 256
HGRN_ROWS = 128
HGRN_EXP_CLAMP = 80.0
PROMPT_BLOCK_ROWS = 896
SCAN_CARRY_ELEMS = 16 * SUBLANES * LANES


def _params(*sem):
    return pltpu.CompilerParams(dimension_semantics=sem, vmem_limit_bytes=V7X_VMEM_LIMIT_BYTES)


def _rms(x, g):
    ms = jnp.mean(x * x, axis=-1, keepdims=True)
    return x * lax.rsqrt(ms + EPS) * g


def _dot(a, b):
    return jnp.dot(a, b, preferred_element_type=F32)


def _full(shape):
    return pl.BlockSpec(shape, lambda *_: (0,) * len(shape))


def _norm_matmul_kernel(x_ref, g_ref, w_ref, o_ref, xn_ref):
    @pl.when(pl.program_id(1) == 0)
    def _():
        xn_ref[...] = _rms(x_ref[...], g_ref[...]).astype(BF16)

    o_ref[...] = _dot(xn_ref[...], w_ref[...])


def norm_matmul(x, g, w, tm, tn):
    n, d = x.shape
    nout = w.shape[1]
    return pl.pallas_call(
        _norm_matmul_kernel,
        out_shape=jax.ShapeDtypeStruct((n, nout), F32),
        grid=(n // tm, nout // tn),
        in_specs=[
            pl.BlockSpec((tm, d), lambda i, j: (i, 0)),
            _full((1, d)),
            pl.BlockSpec((d, tn), lambda i, j: (0, j)),
        ],
        out_specs=pl.BlockSpec((tm, tn), lambda i, j: (i, j)),
        scratch_shapes=[pltpu.VMEM((tm, d), BF16)],
        compiler_params=_params("parallel", "arbitrary"),
        name="norm_matmul",
    )(x, g.reshape(1, d), w)


def _norm_swiglu_kernel(x_ref, g_ref, wg_ref, wu_ref, o_ref, xn_ref):
    @pl.when(pl.program_id(1) == 0)
    def _():
        xn_ref[...] = _rms(x_ref[...], g_ref[...]).astype(BF16)

    xn = xn_ref[...]
    gate = _dot(xn, wg_ref[...])
    up = _dot(xn, wu_ref[...])
    o_ref[...] = (jax.nn.silu(gate) * up).astype(o_ref.dtype)


def norm_swiglu(x, g, w_gu, tm, tn):
    n, d = x.shape
    dff = w_gu.shape[1] // 2
    nj = dff // tn
    return pl.pallas_call(
        _norm_swiglu_kernel,
        out_shape=jax.ShapeDtypeStruct((n, dff), BF16),
        grid=(n // tm, nj),
        in_specs=[
            pl.BlockSpec((tm, d), lambda i, j: (i, 0)),
            _full((1, d)),
            pl.BlockSpec((d, tn), lambda i, j: (0, j)),
            pl.BlockSpec((d, tn), lambda i, j: (0, j + nj)),
        ],
        out_specs=pl.BlockSpec((tm, tn), lambda i, j: (i, j)),
        scratch_shapes=[pltpu.VMEM((tm, d), BF16)],
        compiler_params=_params("parallel", "arbitrary"),
        name="norm_swiglu",
    )(x, g.reshape(1, d), w_gu, w_gu)


def _matmul_residual_kernel(*refs, n_in):
    a_refs = refs[:n_in]
    w_refs = refs[n_in:2 * n_in]
    r_ref = refs[2 * n_in]
    o_ref = refs[2 * n_in + 1]
    acc = r_ref[...]
    for a_ref, w_ref in zip(a_refs, w_refs):
        acc = acc + _dot(a_ref[...], w_ref[...])
    o_ref[...] = acc


def matmul_residual(a_list, w, res, tm):
    n, dout = res.shape
    n_in = len(a_list)
    in_specs = []
    for a in a_list:
        in_specs.append(pl.BlockSpec((tm, a.shape[1]), lambda i: (i, 0)))
    for k, a in enumerate(a_list):
        assert a.shape[1] * n_in == w.shape[0]
        in_specs.append(pl.BlockSpec((a.shape[1], dout), lambda i, k=k: (k, 0)))
    in_specs.append(pl.BlockSpec((tm, dout), lambda i: (i, 0)))
    return pl.pallas_call(
        functools.partial(_matmul_residual_kernel, n_in=n_in),
        out_shape=jax.ShapeDtypeStruct((n, dout), F32),
        grid=(n // tm,),
        in_specs=in_specs,
        out_specs=pl.BlockSpec((tm, dout), lambda i: (i, 0)),
        compiler_params=_params("parallel"),
        name="matmul_residual",
    )(*a_list, *([w] * n_in), res)


def _glu_residual_kernel(z_ref, wv_ref, wg_ref, r_ref, o_ref):
    z = z_ref[...]
    val = _dot(z, wv_ref[...])
    gate = _dot(z, wg_ref[...])
    o_ref[...] = r_ref[...] + val * jax.nn.sigmoid(gate)


def glu_residual(z, w_glu, res, tm, tn):
    n, d = z.shape
    dout = w_glu.shape[1] // 2
    nj = dout // tn
    return pl.pallas_call(
        _glu_residual_kernel,
        out_shape=jax.ShapeDtypeStruct((n, dout), F32),
        grid=(n // tm, nj),
        in_specs=[
            pl.BlockSpec((tm, d), lambda i, j: (i, 0)),
            pl.BlockSpec((d, tn), lambda i, j: (0, j)),
            pl.BlockSpec((d, tn), lambda i, j: (0, j + nj)),
            pl.BlockSpec((tm, tn), lambda i, j: (i, j)),
        ],
        out_specs=pl.BlockSpec((tm, tn), lambda i, j: (i, j)),
        compiler_params=_params("parallel", "arbitrary"),
        name="glu_residual",
    )(z, w_glu, w_glu, res)


def _rmsnorm_kernel(x_ref, g_ref, o_ref):
    o_ref[...] = _rms(x_ref[...], g_ref[...])


def rmsnorm_rows(x, g, tm):
    n, d = x.shape
    return pl.pallas_call(
        _rmsnorm_kernel,
        out_shape=jax.ShapeDtypeStruct((n, d), F32),
        grid=(n // tm,),
        in_specs=[pl.BlockSpec((tm, d), lambda i: (i, 0)), _full((1, d))],
        out_specs=pl.BlockSpec((tm, d), lambda i: (i, 0)),
        compiler_params=_params("parallel"),
        name="final_rmsnorm",
    )(x, g.reshape(1, d))


def _rglru_kernel(xa_ref, ga_ref, conv0_ref, h0_ref, cw_ref, cb_ref, wg_ref, bg_ref, lam_ref,
                  ya_ref, hlast_ref, convnew_ref, xpad_ref, a_ref, u_ref, *, n_t, tb, rb):
    i = pl.program_id(0)
    rows = tb * rb
    tail = (CONV_W - 1) * rb
    c = xa_ref.shape[-1]
    t_valid = jnp.minimum(n_t - i * tb, tb)

    @pl.when(i == 0)
    def _():
        xpad_ref[0:tail, :] = conv0_ref[...]
        hlast_ref[...] = h0_ref[...]

    @pl.when(i > 0)
    def _():
        xpad_ref[0:tail, :] = xpad_ref[rows:rows + tail, :]

    xpad_ref[tail:tail + rows, :] = xa_ref[...]
    xc = cb_ref[...]
    for k in range(CONV_W):
        xc = xc + cw_ref[k:k + 1, :] * xpad_ref[k * rb:k * rb + rows, :]

    gates = _dot(xc.astype(BF16), wg_ref[...]) + bg_ref[...]
    r = jax.nn.sigmoid(gates[:, :c])
    ig = jax.nn.sigmoid(gates[:, c:])
    log_a = (-RG_C) * r * jax.nn.softplus(-lam_ref[...])
    a = jnp.exp(log_a)
    mult = jnp.sqrt(-jnp.tanh(log_a) * (a * a + 1.0))
    a_ref[...] = a
    u_ref[...] = mult * ig * xc

    lc = min(c, max(LANES, SCAN_CARRY_ELEMS // rb // LANES * LANES))
    for c0 in range(0, c, lc):
        def body(t, h, c0=c0):
            sl = pl.ds(pl.multiple_of(t * rb, rb), rb)
            h = a_ref[sl, c0:c0 + lc] * h + u_ref[sl, c0:c0 + lc]
            u_ref[sl, c0:c0 + lc] = h
            return h

        hlast_ref[:, c0:c0 + lc] = lax.fori_loop(0, t_valid, body, hlast_ref[:, c0:c0 + lc])

    live = lax.broadcasted_iota(jnp.int32, (rows, c), 0) < t_valid * rb
    ya_ref[...] = jnp.where(live, u_ref[...] * jax.nn.gelu(ga_ref[...]), 0.0).astype(ya_ref.dtype)
    convnew_ref[...] = xpad_ref[pl.ds(pl.multiple_of(t_valid * rb, rb), tail), :]


def rglru(proj, n_t, conv0, h0, cw, cb, wg, bg, lam, tb, rb):
    c = h0.shape[1]
    n_rows = proj.shape[0]
    rows = tb * rb
    tail = (CONV_W - 1) * rb
    return pl.pallas_call(
        functools.partial(_rglru_kernel, n_t=n_t, tb=tb, rb=rb),
        out_shape=(
            jax.ShapeDtypeStruct((n_rows, c), BF16),
            jax.ShapeDtypeStruct((rb, c), F32),
            jax.ShapeDtypeStruct((tail, c), F32),
        ),
        grid=(n_rows // rows,),
        in_specs=[
            pl.BlockSpec((rows, c), lambda i: (i, 0)),
            pl.BlockSpec((rows, c), lambda i: (i, 1)),
            _full((tail, c)), _full((rb, c)), _full((CONV_W, c)), _full((1, c)),
            _full((c, 2 * c)), _full((1, 2 * c)), _full((1, c)),
        ],
        out_specs=(
            pl.BlockSpec((rows, c), lambda i: (i, 0)),
            _full((rb, c)),
            _full((tail, c)),
        ),
        scratch_shapes=[
            pltpu.VMEM((rows + tail, c), F32),
            pltpu.VMEM((rows, c), F32),
            pltpu.VMEM((rows, c), F32),
        ],
        compiler_params=_params("arbitrary"),
        name="rglru",
    )(proj, proj, conv0, h0, cw, cb.reshape(1, c), wg, bg.reshape(1, 2 * c), lam.reshape(1, c))


def _split3_bf16(x):
    hi = x.astype(BF16)
    r1 = x - hi.astype(F32)
    mid = r1.astype(BF16)
    lo = (r1 - mid.astype(F32)).astype(BF16)
    return hi, mid, lo


def _hgrn2_kernel(q_ref, f_ref, v_ref, gb_ref, s0_ref, lbraw_ref, gn_ref, ob_ref, s_ref,
                  *, layer, n_t, tb, nb, nh):
    i = pl.program_id(1)
    m = HGRN_ROWS
    tc = m // nb
    d = q_ref.shape[-1]
    dk = d // nh
    n_chunks = jnp.minimum(n_t - i * tb, tb) // tc

    @pl.when(i == 0)
    def _():
        s_ref[...] = s0_ref[...]

    @pl.when(n_chunks < tb // tc)
    def _():
        ob_ref[...] = jnp.zeros_like(ob_ref)

    p = jax.nn.softmax(lbraw_ref[...], axis=0)
    cum = p[0:1, :]
    for r in range(1, layer + 1):
        cum = cum + p[r:r + 1, :]
    lb = cum - p[0:1, :]
    log_lb = jnp.log(lb)
    log_1mlb = jnp.log1p(-lb)

    row = lax.broadcasted_iota(jnp.int32, (m, m), 0)
    col = lax.broadcasted_iota(jnp.int32, (m, m), 1)
    same_seq_causal = jnp.where(((row & (nb - 1)) == (col & (nb - 1))) & (col <= row), 1.0, 0.0)
    rid = lax.broadcasted_iota(jnp.int32, (m, dk), 0)
    bid = rid & (nb - 1)
    ones = jnp.ones((m, dk), BF16)

    def chunk(c, carry):
        tsl = pl.ds(c * tc, tc)
        q = jax.nn.silu(q_ref[tsl].reshape(m, d))
        fr = f_ref[tsl].reshape(m, d)
        v = v_ref[tsl].reshape(m, d).astype(BF16)
        gb = gb_ref[tsl].reshape(m, d)
        logf = jnp.logaddexp(log_lb, log_1mlb + jax.nn.log_sigmoid(fr))
        k = 1.0 - jnp.exp(logf)
        slabs = [logf[0:nb]]
        for t in range(1, tc):
            slabs.append(slabs[-1] + logf[t * nb:(t + 1) * nb])
        g = jnp.concatenate(slabs, axis=0)
        g_last = jnp.concatenate([slabs[-1]] * tc, axis=0)
        qt = (q * jnp.exp(g)).astype(BF16)
        kt = (k * jnp.exp(jnp.minimum(-g, HGRN_EXP_CLAMP))).astype(BF16)
        ks = (k * jnp.exp(g_last - g)).astype(BF16)
        dec = jnp.where(lax.broadcasted_iota(jnp.int32, (m, d), 0) < nb, jnp.exp(g_last), 0.0)

        outs = []
        for h in range(nh):
            hs = slice(h * dk, (h + 1) * dk)
            qt_h, kt_h, ks_h, v_h = qt[:, hs], kt[:, hs], ks[:, hs], v[:, hs]
            att = lax.dot_general(qt_h, kt_h, (((1,), (1,)), ((), ())), preferred_element_type=F32)
            att = (att * same_seq_causal).astype(BF16)
            o = _dot(att, v_h)
            s_h = s_ref[:, h].reshape(nb * dk, dk)
            zero = jnp.zeros_like(qt_h)
            expand = lambda x: jnp.concatenate([jnp.where(bid == b, x, zero) for b in range(nb)], axis=1)
            o = o + _dot(expand(qt_h), s_h.astype(BF16))
            ds = lax.dot_general(expand(ks_h), v_h, (((0,), (0,)), ((), ())), preferred_element_type=F32)
            dmat = None
            for part in _split3_bf16(dec[:, hs]):
                term = lax.dot_general(expand(part), ones, (((0,), (0,)), ((), ())), preferred_element_type=F32)
                dmat = term if dmat is None else dmat + term
            s_ref[:, h] = (dmat * s_h + ds).reshape(nb, dk, dk)
            outs.append(_rms(o, gn_ref[:, hs]))
        ob = jnp.concatenate(outs, axis=1) * jax.nn.silu(gb)
        ob_ref[pl.ds(pl.multiple_of(c * m, m), m), :] = ob.astype(ob_ref.dtype)
        return carry

    lax.fori_loop(0, n_chunks, chunk, 0)


def hgrn2(proj3, s0, lb_raw, gnorm, layer, n_t, tb, nb):
    bsz, nh, dk, _ = s0.shape
    d = nh * dk
    off = proj3.shape[-1] // d - 4
    nbb = bsz // nb
    nt = proj3.shape[0] // tb
    col = lambda k: pl.BlockSpec((tb, nb, d), lambda j, i, k=k: (i, j, k + off))
    return pl.pallas_call(
        functools.partial(_hgrn2_kernel, layer=layer, n_t=n_t, tb=tb, nb=nb, nh=nh),
        out_shape=(
            jax.ShapeDtypeStruct((proj3.shape[0] * bsz, d), BF16),
            jax.ShapeDtypeStruct(s0.shape, F32),
        ),
        grid=(nbb, nt),
        in_specs=[
            col(0), col(1), col(2), col(3),
            pl.BlockSpec((nb, nh, dk, dk), lambda j, i: (j, 0, 0, 0)),
            _full(lb_raw.shape),
            _full((1, d)),
        ],
        out_specs=(
            pl.BlockSpec((tb * nb, d), lambda j, i: (j * nt + i, 0)),
            pl.BlockSpec((nb, nh, dk, dk), lambda j, i: (j, 0, 0, 0)),
        ),
        compiler_params=_params("arbitrary", "arbitrary"),
        name="hgrn2",
    )(proj3, proj3, proj3, proj3, s0, lb_raw, gnorm.reshape(1, d))


def _s5_prep_kernel(lr_ref, li_ref, ldt_ref, bre_ref, bim_ref, ar_ref, ai_ref, ore_ref, oim_ref):
    lr = lr_ref[...]
    li = li_ref[...]
    dt = jnp.exp(ldt_ref[...])
    mag = jnp.exp(lr * dt)
    ar = mag * jnp.cos(li * dt)
    ai = mag * jnp.sin(li * dt)
    den = lr * lr + li * li
    cr = ((ar - 1.0) * lr + ai * li) / den
    ci = (ai * lr - (ar - 1.0) * li) / den
    ar_ref[...] = ar
    ai_ref[...] = ai
    ore_ref[...] = cr * bre_ref[...] - ci * bim_ref[...]
    oim_ref[...] = cr * bim_ref[...] + ci * bre_ref[...]


def s5_prep(lam_re, lam_im, log_dt, b_re, b_im):
    g, p, c = b_re.shape
    ns = g * p
    colv = jax.ShapeDtypeStruct((ns, 1), F32)
    mat = jax.ShapeDtypeStruct((ns, c), F32)
    ldt = jnp.broadcast_to(log_dt[:, None], (g, p)).reshape(ns, 1)
    return pl.pallas_call(
        _s5_prep_kernel, out_shape=(colv, colv, mat, mat), name="s5_prep",
    )(lam_re.reshape(ns, 1), lam_im.reshape(ns, 1), ldt, b_re.reshape(ns, c), b_im.reshape(ns, c))


def _s5_kernel(x_ref, g_ref, wb_ref, wc_ref, ar_ref, ai_ref, d_ref, h0r_ref, h0i_ref,
               z_ref, hr_ref, hi_ref, xn_ref, xr_ref, xi_ref, *, n_t, tb, rb):
    i = pl.program_id(0)
    nch = wb_ref.shape[0]
    cw = wb_ref.shape[1]
    sw = wb_ref.shape[2] // 2
    rows = tb * rb
    t_valid = jnp.minimum(n_t - i * tb, tb)

    @pl.when(i == 0)
    def _():
        hr_ref[...] = h0r_ref[...]
        hi_ref[...] = h0i_ref[...]

    xn_ref[...] = _rms(x_ref[...], g_ref[...])
    live = lax.broadcasted_iota(jnp.int32, (rows, cw), 0) < t_valid * rb

    for c in range(nch):
        cs = slice(c * cw, (c + 1) * cw)
        ss = slice(c * sw, (c + 1) * sw)
        u = xn_ref[:, cs]
        bu = _dot(u.astype(BF16), wb_ref[c])
        xr_ref[...] = bu[:, :sw]
        xi_ref[...] = bu[:, sw:]

        ar = jnp.broadcast_to(ar_ref[:, ss], (SUBLANES, sw))
        ai = jnp.broadcast_to(ai_ref[:, ss], (SUBLANES, sw))

        def sub_rows(s, carry, ss=ss, ar=ar, ai=ai):
            s0 = pl.multiple_of(s * SUBLANES, SUBLANES)

            def step(t, h):
                hr, hi = h
                r0 = pl.multiple_of(t * rb + s0, SUBLANES)
                nhr = ar * hr - ai * hi + xr_ref[pl.ds(r0, SUBLANES), :]
                nhi = ar * hi + ai * hr + xi_ref[pl.ds(r0, SUBLANES), :]
                xr_ref[pl.ds(r0, SUBLANES), :] = nhr
                xi_ref[pl.ds(r0, SUBLANES), :] = nhi
                return nhr, nhi

            hr, hi = lax.fori_loop(0, t_valid, step, (hr_ref[pl.ds(s0, SUBLANES), ss], hi_ref[pl.ds(s0, SUBLANES), ss]))
            hr_ref[pl.ds(s0, SUBLANES), ss] = hr
            hi_ref[pl.ds(s0, SUBLANES), ss] = hi
            return carry

        lax.fori_loop(0, rb // SUBLANES, sub_rows, 0)

        hcat = jnp.concatenate([xr_ref[...].astype(BF16), xi_ref[...].astype(BF16)], axis=1)
        y = _dot(hcat, wc_ref[c]) + d_ref[:, cs] * u
        z_ref[:, cs] = jnp.where(live, jax.nn.gelu(y), 0.0).astype(z_ref.dtype)


def s5(x, n_t, g, wb, wc, a_re, a_im, dskip, h0r, h0i, tb, rb):
    n_rows, d = x.shape
    ns = h0r.shape[1]
    rows = tb * rb
    sw = wb.shape[2] // 2
    return pl.pallas_call(
        functools.partial(_s5_kernel, n_t=n_t, tb=tb, rb=rb),
        out_shape=(
            jax.ShapeDtypeStruct((n_rows, d), BF16),
            jax.ShapeDtypeStruct((rb, ns), F32),
            jax.ShapeDtypeStruct((rb, ns), F32),
        ),
        grid=(n_rows // rows,),
        in_specs=[
            pl.BlockSpec((rows, d), lambda i: (i, 0)),
            _full((1, d)), _full(wb.shape), _full(wc.shape),
            _full((1, ns)), _full((1, ns)), _full((1, d)),
            _full((rb, ns)), _full((rb, ns)),
        ],
        out_specs=(
            pl.BlockSpec((rows, d), lambda i: (i, 0)),
            _full((rb, ns)), _full((rb, ns)),
        ),
        scratch_shapes=[
            pltpu.VMEM((rows, d), F32),
            pltpu.VMEM((rows, sw), F32),
            pltpu.VMEM((rows, sw), F32),
        ],
        compiler_params=_params("arbitrary"),
        name="s5",
    )(x, g.reshape(1, d), wb, wc, a_re, a_im, dskip.reshape(1, d), h0r, h0i)


def _s5_block_diag_in(b):
    g, p, c = b.shape
    gpc = V7X_MXU_DIM // c
    bt = b.transpose(0, 2, 1).reshape(g // gpc, gpc, c, p)
    out = jnp.einsum('ngcp,gh->ngchp', bt, jnp.eye(gpc, dtype=b.dtype))
    return out.reshape(g // gpc, gpc * c, gpc * p)


def _s5_block_diag_out(cm):
    g, c, p = cm.shape
    gpc = V7X_MXU_DIM // c
    ct = cm.transpose(0, 2, 1).reshape(g // gpc, gpc, p, c)
    out = jnp.einsum('ngpc,gh->ngphc', ct, jnp.eye(gpc, dtype=cm.dtype))
    return out.reshape(g // gpc, gpc * p, gpc * c)


def _moe_kernel(x_ref, g_ref, wrt_ref, wgu_ref, wdn_ref, o_ref, xnt_ref, acc_ref, pos_ref, gate_ref, *, sub):
    e = pl.program_id(1)
    tm = x_ref.shape[0]
    ne = wrt_ref.shape[0]
    dff = wdn_ref.shape[2]

    @pl.when(e == 0)
    def _():
        xnt = _rms(x_ref[...], g_ref[...]).T
        xnt_ref[...] = xnt.astype(BF16)
        acc_ref[...] = jnp.zeros_like(acc_ref)
        logits = jnp.dot(wrt_ref[...], xnt, precision=lax.Precision.HIGHEST, preferred_element_type=F32)
        ex = jnp.exp(logits - jnp.max(logits, axis=0, keepdims=True))
        probs = ex / jnp.sum(ex, axis=0, keepdims=True)
        eid = lax.broadcasted_iota(jnp.int32, (ne, tm), 0).astype(F32)
        m1 = jnp.max(probs, axis=0, keepdims=True)
        i1 = jnp.min(jnp.where(probs == m1, eid, float(ne)), axis=0, keepdims=True)
        sel1 = eid == i1
        rest = jnp.where(sel1, -1.0, probs)
        m2 = jnp.max(rest, axis=0, keepdims=True)
        i2 = jnp.min(jnp.where(rest == m2, eid, float(ne)), axis=0, keepdims=True)
        sel2 = eid == i2
        den = m1 + m2
        gate_ref[...] = jnp.where(sel1, m1 / den, 0.0) + jnp.where(sel2, m2 / den, 0.0)
        chosen = jnp.where(sel1, 1.0, jnp.where(sel2, 1.0, 0.0))
        r = lax.broadcasted_iota(jnp.int32, (tm, tm), 0)
        c = lax.broadcasted_iota(jnp.int32, (tm, tm), 1)
        before = jnp.where(r < c, 1.0, 0.0).astype(BF16)
        rank = _dot(chosen.astype(BF16), before)
        pos_ref[...] = jnp.where(chosen > 0.0, rank, -1.0)

    pos_e = pos_ref[pl.ds(e, 1), :]
    gate_e = gate_ref[pl.ds(e, 1), :]
    cnt = jnp.sum(jnp.where(pos_e >= 0.0, 1.0, 0.0)).astype(jnp.int32)

    def body(s, carry):
        slot = lax.broadcasted_iota(jnp.int32, (sub, tm), 0) + s * sub
        onehot = jnp.where(pos_e == slot.astype(F32), 1.0, 0.0).astype(BF16)
        xs = lax.dot_general(xnt_ref[...], onehot, (((1,), (1,)), ((), ())), preferred_element_type=F32)
        hgu = _dot(wgu_ref[0], xs.astype(BF16))
        act = (jax.nn.silu(hgu[:dff]) * hgu[dff:]).astype(BF16)
        yb = _dot(wdn_ref[0], act).astype(BF16)
        acc_ref[...] += _dot(yb, onehot) * gate_e
        return carry

    lax.fori_loop(0, (cnt + sub - 1) // sub, body, 0)

    @pl.when(e == ne - 1)
    def _():
        o_ref[...] = x_ref[...] + acc_ref[...].T


def moe_residual(x, g, w_router_t, w_gu_t, w_down_t, tm, sub):
    n, d = x.shape
    ne, dff2, _ = w_gu_t.shape
    return pl.pallas_call(
        functools.partial(_moe_kernel, sub=sub),
        out_shape=jax.ShapeDtypeStruct((n, d), F32),
        grid=(n // tm, ne),
        in_specs=[
            pl.BlockSpec((tm, d), lambda i, e: (i, 0)),
            _full((1, d)),
            _full((ne, d)),
            pl.BlockSpec((1, dff2, d), lambda i, e: (e, 0, 0)),
            pl.BlockSpec((1, d, dff2 // 2), lambda i, e: (e, 0, 0)),
        ],
        out_specs=pl.BlockSpec((tm, d), lambda i, e: (i, 0)),
        scratch_shapes=[
            pltpu.VMEM((d, tm), BF16),
            pltpu.VMEM((d, tm), F32),
            pltpu.VMEM((ne, tm), F32),
            pltpu.VMEM((ne, tm), F32),
        ],
        compiler_params=_params("parallel", "arbitrary"),
        name="moe",
    )(x, g.reshape(1, d), w_router_t, w_gu_t, w_down_t)


def _trunk(x, n_t, rb, tb, nb, h0, conv0, s0, re0, im0, w):
    n, d = x.shape
    depth = w['norm_mix'].shape[0]
    d_a = h0.shape[-1]
    nh, dk = s0.shape[2], s0.shape[3]
    d_b = nh * dk
    ns = re0.shape[-2] * re0.shape[-1]
    tm = tb * rb
    new = {k: [] for k in ('h', 'conv', 's', 're', 'im')}
    for l in range(depth):
        li = l // 2
        if l % 2 == 0:
            proj = norm_matmul(x, w['norm_mix'][l], w['even_w_in'][li], tm, 1024)
            conv_tm = conv0[li].transpose(1, 0, 2).reshape((CONV_W - 1) * rb, d_a)
            ya, h_new, conv_new = rglru(proj, n_t, conv_tm, h0[li], w['rglru_conv_w'][li], w['rglru_conv_b'][li],
                                        w['rglru_wg'][li], w['rglru_bg'][li], w['rglru_lambda'][li], tb=tb, rb=rb)
            new['h'].append(h_new)
            new['conv'].append(conv_new.reshape(CONV_W - 1, rb, d_a).transpose(1, 0, 2))
            ob, s_new = hgrn2(proj.reshape(n // rb, rb, proj.shape[1]), s0[li], w['hgrn2_lb_raw'], w['hgrn2_gnorm'][li],
                              li, n_t=n_t, tb=tb, nb=nb)
            if nb != rb:
                ob = ob.reshape(rb // nb, n // rb, nb, d_b).transpose(1, 0, 2, 3).reshape(n, d_b)
            new['s'].append(s_new)
            x = matmul_residual([ya, ob], w['even_w_out'][li], x, tm)
            hff = norm_swiglu(x, w['norm_ffn'][l], w['ffn_w_gu'][li], tm, w['ffn_w_gu'].shape[2] // 4)
            x = matmul_residual([hff], w['ffn_w_down'][li], x, tm)
        else:
            z, re_new, im_new = s5(x, n_t, w['norm_mix'][l], w['s5_wb'][li], w['s5_wc'][li], w['s5_a_re'][li],
                                   w['s5_a_im'][li], w['s5_d'][li], re0[li].reshape(rb, ns), im0[li].reshape(rb, ns),
                                   tb=tb, rb=rb)
            new['re'].append(re_new.reshape(re0.shape[1:]))
            new['im'].append(im_new.reshape(im0.shape[1:]))
            x = glu_residual(z, w['s5_w_glu'][li], x, tm, 512)
            x = moe_residual(x, w['norm_ffn'][l], w['moe_w_router_t'][li], w['moe_w_gu_t'][li], w['moe_w_down_t'][li],
                             tm, V7X_MXU_DIM)
    y = rmsnorm_rows(x, w['norm_final'], tm)
    return y, tuple(jnp.stack(new[k]) for k in ('h', 'conv', 's', 're', 'im'))


def kernel(x_prompt, x_sample, state_rglru_h, state_rglru_conv, state_hgrn2, state_s5_re, state_s5_im,
           meta_tokens, norm_mix, norm_ffn, norm_final, even_w_in, even_w_out,
           rglru_conv_w, rglru_conv_b, rglru_w_a, rglru_b_a, rglru_w_x, rglru_b_x, rglru_lambda,
           hgrn2_lb_raw, hgrn2_gnorm, s5_lam_re, s5_lam_im, s5_log_dt, s5_b_re, s5_b_im, s5_c_re, s5_c_im,
           s5_d, s5_w_glu, ffn_w_gu, ffn_w_down, moe_w_router, moe_w_gu, moe_w_down):
    bp, tp0, d = x_prompt.shape
    bs, ts, _ = x_sample.shape
    tp = tp0 + N_META
    d_a = state_rglru_h.shape[-1]
    n_even, n_odd = state_rglru_h.shape[0], state_s5_re.shape[0]
    assert bp == SUBLANES and bs % SUBLANES == 0
    tb_p = PROMPT_BLOCK_ROWS // bp
    tp_pad = -(-tp // tb_p) * tb_p
    nb_s = HGRN_ROWS // ts
    assert (tp % (HGRN_ROWS // bp) == 0 and tb_p % (HGRN_ROWS // bp) == 0 and bs % nb_s == 0
            and nb_s * ts == HGRN_ROWS and nb_s % SUBLANES == 0)

    eye_a = jnp.eye(H_A, dtype=F32)
    block_diag = lambda m: jnp.einsum('lhij,hg->lhigj', m, eye_a).reshape(n_even, d_a, d_a)
    prep = [s5_prep(s5_lam_re[i], s5_lam_im[i], s5_log_dt[i], s5_b_re[i], s5_b_im[i]) for i in range(n_odd)]
    ns = s5_lam_re.shape[1] * s5_lam_re.shape[2]
    bshape = s5_b_re.shape[1:]
    w = {
        'norm_mix': norm_mix, 'norm_ffn': norm_ffn, 'norm_final': norm_final,
        'even_w_in': even_w_in.astype(BF16), 'even_w_out': even_w_out.astype(BF16),
        'rglru_conv_w': rglru_conv_w, 'rglru_conv_b': rglru_conv_b,
        'rglru_wg': jnp.concatenate([block_diag(rglru_w_a), block_diag(rglru_w_x)], axis=2).astype(BF16),
        'rglru_bg': jnp.concatenate([rglru_b_a, rglru_b_x], axis=1),
        'rglru_lambda': rglru_lambda, 'hgrn2_lb_raw': hgrn2_lb_raw, 'hgrn2_gnorm': hgrn2_gnorm,
        's5_a_re': [p[0].reshape(1, ns) for p in prep], 's5_a_im': [p[1].reshape(1, ns) for p in prep],
        's5_wb': [jnp.concatenate([_s5_block_diag_in(p[2].reshape(bshape)), _s5_block_diag_in(p[3].reshape(bshape))],
                                  axis=2).astype(BF16) for p in prep],
        's5_wc': [jnp.concatenate([_s5_block_diag_out(s5_c_re[i]), -_s5_block_diag_out(s5_c_im[i])],
                                  axis=1).astype(BF16) for i in range(n_odd)],
        's5_d': s5_d, 's5_w_glu': s5_w_glu.astype(BF16),
        'ffn_w_gu': ffn_w_gu.astype(BF16), 'ffn_w_down': ffn_w_down.astype(BF16),
        'moe_w_router_t': moe_w_router.transpose(0, 2, 1),
        'moe_w_gu_t': moe_w_gu.astype(BF16).transpose(0, 1, 3, 2),
        'moe_w_down_t': moe_w_down.astype(BF16).transpose(0, 1, 3, 2),
    }

    meta = jnp.broadcast_to(meta_tokens.astype(x_prompt.dtype)[None], (bp, N_META, d))
    xm = jnp.concatenate([meta, x_prompt], axis=1).transpose(1, 0, 2)
    xm = jnp.pad(xm, ((0, tp_pad - tp), (0, 0), (0, 0))).reshape(tp_pad * bp, d)
    zero = lambda ref, lead: jnp.zeros((lead, bp) + ref.shape[2:], ref.dtype)
    yp, p_new = _trunk(xm, tp, bp, tb_p, bp, zero(state_rglru_h, n_even), zero(state_rglru_conv, n_even),
                       zero(state_hgrn2, n_even), zero(state_s5_re, n_odd), zero(state_s5_im, n_odd), w)
    y_prompt = yp.reshape(tp_pad, bp, d)[N_META:tp].transpose(1, 0, 2)

    xs = x_sample.transpose(1, 0, 2).reshape(ts * bs, d)
    ys, s_new = _trunk(xs, ts, bs, ts, nb_s, state_rglru_h, state_rglru_conv, state_hgrn2, state_s5_re, state_s5_im, w)
    y_sample = ys.reshape(ts, bs, d).transpose(1, 0, 2)

    refs = (state_rglru_h, state_rglru_conv, state_hgrn2, state_s5_re, state_s5_im)
    cast = lambda new: tuple(a.astype(r.dtype) for a, r in zip(new, refs))
    return (y_prompt, y_sample) + cast(p_new) + cast(s_new)
```

```python
import functools

import jax
import jax.numpy as jnp
from jax import lax
from jax.experimental import pallas as pl
from jax.experimental.pallas import tpu as pltpu

F32 = jnp.float32
BF16 = jnp.bfloat16

EPS = 1e-6
N_META = 16
CONV_W = 4
RG_C = 8.0
H_A = 8
S5_P = 64

V7X_VMEM_LIMIT_BYTES = 56 * 1024 * 1024
SUBLANES = 8
LANES = 128
V7X_MXU_DIM = 256
HGRN_ROWS = 128
HGRN_EXP_CLAMP = 80.0
PROMPT_BLOCK_ROWS = 896
SCAN_CARRY_ELEMS = 16 * SUBLANES * LANES


def _params(*sem):
    return pltpu.CompilerParams(dimension_semantics=sem, vmem_limit_bytes=V7X_VMEM_LIMIT_BYTES)


def _rms(x, g):
    ms = jnp.mean(x * x, axis=-1, keepdims=True)
    return x * lax.rsqrt(ms + EPS) * g


def _dot(a, b):
    return jnp.dot(a, b, preferred_element_type=F32)


def _full(shape):
    return pl.BlockSpec(shape, lambda *_: (0,) * len(shape))


def _norm_matmul_kernel(x_ref, g_ref, w_ref, o_ref, xn_ref):
    @pl.when(pl.program_id(1) == 0)
    def _():
        xn_ref[...] = _rms(x_ref[...], g_ref[...]).astype(BF16)

    o_ref[...] = _dot(xn_ref[...], w_ref[...])


def norm_matmul(x, g, w, tm, tn):
    n, d = x.shape
    nout = w.shape[1]
    return pl.pallas_call(
        _norm_matmul_kernel,
        out_shape=jax.ShapeDtypeStruct((n, nout), F32),
        grid=(n // tm, nout // tn),
        in_specs=[
            pl.BlockSpec((tm, d), lambda i, j: (i, 0)),
            _full((1, d)),
            pl.BlockSpec((d, tn), lambda i, j: (0, j)),
        ],
        out_specs=pl.BlockSpec((tm, tn), lambda i, j: (i, j)),
        scratch_shapes=[pltpu.VMEM((tm, d), BF16)],
        compiler_params=_params("parallel", "arbitrary"),
        name="norm_matmul",
    )(x, g.reshape(1, d), w)


def _norm_swiglu_kernel(x_ref, g_ref, wg_ref, wu_ref, o_ref, xn_ref):
    @pl.when(pl.program_id(1) == 0)
    def _():
        xn_ref[...] = _rms(x_ref[...], g_ref[...]).astype(BF16)

    xn = xn_ref[...]
    gate = _dot(xn, wg_ref[...])
    up = _dot(xn, wu_ref[...])
    o_ref[...] = (jax.nn.silu(gate) * up).astype(o_ref.dtype)


def norm_swiglu(x, g, w_gu, tm, tn):
    n, d = x.shape
    dff = w_gu.shape[1] // 2
    nj = dff // tn
    return pl.pallas_call(
        _norm_swiglu_kernel,
        out_shape=jax.ShapeDtypeStruct((n, dff), BF16),
        grid=(n // tm, nj),
        in_specs=[
            pl.BlockSpec((tm, d), lambda i, j: (i, 0)),
            _full((1, d)),
            pl.BlockSpec((d, tn), lambda i, j: (0, j)),
            pl.BlockSpec((d, tn), lambda i, j: (0, j + nj)),
        ],
        out_specs=pl.BlockSpec((tm, tn), lambda i, j: (i, j)),
        scratch_shapes=[pltpu.VMEM((tm, d), BF16)],
        compiler_params=_params("parallel", "arbitrary"),
        name="norm_swiglu",
    )(x, g.reshape(1, d), w_gu, w_gu)


def _matmul_residual_kernel(*refs, n_in):
    a_refs = refs[:n_in]
    w_refs = refs[n_in:2 * n_in]
    r_ref = refs[2 * n_in]
    o_ref = refs[2 * n_in + 1]
    acc = r_ref[...]
    for a_ref, w_ref in zip(a_refs, w_refs):
        acc = acc + _dot(a_ref[...], w_ref[...])
    o_ref[...] = acc


def matmul_residual(a_list, w, res, tm):
    n, dout = res.shape
    n_in = len(a_list)
    in_specs = []
    for a in a_list:
        in_specs.append(pl.BlockSpec((tm, a.shape[1]), lambda i: (i, 0)))
    for k, a in enumerate(a_list):
        assert a.shape[1] * n_in == w.shape[0]
        in_specs.append(pl.BlockSpec((a.shape[1], dout), lambda i, k=k: (k, 0)))
    in_specs.append(pl.BlockSpec((tm, dout), lambda i: (i, 0)))
    return pl.pallas_call(
        functools.partial(_matmul_residual_kernel, n_in=n_in),
        out_shape=jax.ShapeDtypeStruct((n, dout), F32),
        grid=(n // tm,),
        in_specs=in_specs,
        out_specs=pl.BlockSpec((tm, dout), lambda i: (i, 0)),
        compiler_params=_params("parallel"),
        name="matmul_residual",
    )(*a_list, *([w] * n_in), res)


def _glu_residual_kernel(z_ref, wv_ref, wg_ref, r_ref, o_ref):
    z = z_ref[...]
    val = _dot(z, wv_ref[...])
    gate = _dot(z, wg_ref[...])
    o_ref[...] = r_ref[...] + val * jax.nn.sigmoid(gate)


def glu_residual(z, w_glu, res, tm, tn):
    n, d = z.shape
    dout = w_glu.shape[1] // 2
    nj = dout // tn
    return pl.pallas_call(
        _glu_residual_kernel,
        out_shape=jax.ShapeDtypeStruct((n, dout), F32),
        grid=(n // tm, nj),
        in_specs=[
            pl.BlockSpec((tm, d), lambda i, j: (i, 0)),
            pl.BlockSpec((d, tn), lambda i, j: (0, j)),
            pl.BlockSpec((d, tn), lambda i, j: (0, j + nj)),
            pl.BlockSpec((tm, tn), lambda i, j: (i, j)),
        ],
        out_specs=pl.BlockSpec((tm, tn), lambda i, j: (i, j)),
        compiler_params=_params("parallel", "arbitrary"),
        name="glu_residual",
    )(z, w_glu, w_glu, res)


def _rmsnorm_kernel(x_ref, g_ref, o_ref):
    o_ref[...] = _rms(x_ref[...], g_ref[...])


def rmsnorm_rows(x, g, tm):
    n, d = x.shape
    return pl.pallas_call(
        _rmsnorm_kernel,
        out_shape=jax.ShapeDtypeStruct((n, d), F32),
        grid=(n // tm,),
        in_specs=[pl.BlockSpec((tm, d), lambda i: (i, 0)), _full((1, d))],
        out_specs=pl.BlockSpec((tm, d), lambda i: (i, 0)),
        compiler_params=_params("parallel"),
        name="final_rmsnorm",
    )(x, g.reshape(1, d))


def _rglru_kernel(xa_ref, ga_ref, conv0_ref, h0_ref, cw_ref, cb_ref, wg_ref, bg_ref, lam_ref,
                  ya_ref, hlast_ref, convnew_ref, xpad_ref, a_ref, u_ref, *, n_t, tb, rb):
    i = pl.program_id(0)
    rows = tb * rb
    tail = (CONV_W - 1) * rb
    c = xa_ref.shape[-1]
    t_valid = jnp.minimum(n_t - i * tb, tb)

    @pl.when(i == 0)
    def _():
        xpad_ref[0:tail, :] = conv0_ref[...]
        hlast_ref[...] = h0_ref[...]

    @pl.when(i > 0)
    def _():
        xpad_ref[0:tail, :] = xpad_ref[rows:rows + tail, :]

    xpad_ref[tail:tail + rows, :] = xa_ref[...]
    xc = cb_ref[...]
    for k in range(CONV_W):
        xc = xc + cw_ref[k:k + 1, :] * xpad_ref[k * rb:k * rb + rows, :]

    gates = _dot(xc.astype(BF16), wg_ref[...]) + bg_ref[...]
    r = jax.nn.sigmoid(gates[:, :c])
    ig = jax.nn.sigmoid(gates[:, c:])
    log_a = (-RG_C) * r * jax.nn.softplus(-lam_ref[...])
    a = jnp.exp(log_a)
    mult = jnp.sqrt(-jnp.tanh(log_a) * (a * a + 1.0))
    a_ref[...] = a
    u_ref[...] = mult * ig * xc

    lc = min(c, max(LANES, SCAN_CARRY_ELEMS // rb // LANES * LANES))
    for c0 in range(0, c, lc):
        def body(t, h, c0=c0):
            sl = pl.ds(pl.multiple_of(t * rb, rb), rb)
            h = a_ref[sl, c0:c0 + lc] * h + u_ref[sl, c0:c0 + lc]
            u_ref[sl, c0:c0 + lc] = h
            return h

        hlast_ref[:, c0:c0 + lc] = lax.fori_loop(0, t_valid, body, hlast_ref[:, c0:c0 + lc])

    live = lax.broadcasted_iota(jnp.int32, (rows, c), 0) < t_valid * rb
    ya_ref[...] = jnp.where(live, u_ref[...] * jax.nn.gelu(ga_ref[...]), 0.0).astype(ya_ref.dtype)
    convnew_ref[...] = xpad_ref[pl.ds(pl.multiple_of(t_valid * rb, rb), tail), :]


def rglru(proj, n_t, conv0, h0, cw, cb, wg, bg, lam, tb, rb):
    c = h0.shape[1]
    n_rows = proj.shape[0]
    rows = tb * rb
    tail = (CONV_W - 1) * rb
    return pl.pallas_call(
        functools.partial(_rglru_kernel, n_t=n_t, tb=tb, rb=rb),
        out_shape=(
            jax.ShapeDtypeStruct((n_rows, c), BF16),
            jax.ShapeDtypeStruct((rb, c), F32),
            jax.ShapeDtypeStruct((tail, c), F32),
        ),
        grid=(n_rows // rows,),
        in_specs=[
            pl.BlockSpec((rows, c), lambda i: (i, 0)),
            pl.BlockSpec((rows, c), lambda i: (i, 1)),
            _full((tail, c)), _full((rb, c)), _full((CONV_W, c)), _full((1, c)),
            _full((c, 2 * c)), _full((1, 2 * c)), _full((1, c)),
        ],
        out_specs=(
            pl.BlockSpec((rows, c), lambda i: (i, 0)),
            _full((rb, c)),
            _full((tail, c)),
        ),
        scratch_shapes=[
            pltpu.VMEM((rows + tail, c), F32),
            pltpu.VMEM((rows, c), F32),
            pltpu.VMEM((rows, c), F32),
        ],
        compiler_params=_params("arbitrary"),
        name="rglru",
    )(proj, proj, conv0, h0, cw, cb.reshape(1, c), wg, bg.reshape(1, 2 * c), lam.reshape(1, c))


def _hgrn2_kernel(q_ref, f_ref, v_ref, gb_ref, s0_ref, lbraw_ref, gn_ref, ob_ref, snew_ref, st_ref,
                  *, layer, n_t, tb, nb, nh):
    i = pl.program_id(1)
    m = HGRN_ROWS
    tc = m // nb
    d = q_ref.shape[-1]
    dk = d // nh
    n_chunks = jnp.minimum(n_t - i * tb, tb) // tc

    @pl.when(i == 0)
    def _():
        for b in range(nb):
            for h in range(nh):
                st_ref[h, :, b * dk:(b + 1) * dk] = s0_ref[b, h].T

    @pl.when(n_chunks < tb // tc)
    def _():
        ob_ref[...] = jnp.zeros_like(ob_ref)

    p = jax.nn.softmax(lbraw_ref[...], axis=0)
    cum = p[0:1, :]
    for r in range(1, layer + 1):
        cum = cum + p[r:r + 1, :]
    lb = cum - p[0:1, :]
    log_lb = jnp.log(lb)
    log_1mlb = jnp.log1p(-lb)

    row = lax.broadcasted_iota(jnp.int32, (m, m), 0)
    col = lax.broadcasted_iota(jnp.int32, (m, m), 1)
    same_seq_causal = jnp.where(((row & (nb - 1)) == (col & (nb - 1))) & (col <= row), 1.0, 0.0)
    bid = lax.broadcasted_iota(jnp.int32, (m, dk), 0) & (nb - 1)

    def chunk(c, carry):
        tsl = pl.ds(c * tc, tc)
        q = jax.nn.silu(q_ref[tsl].reshape(m, d))
        fr = f_ref[tsl].reshape(m, d)
        v = v_ref[tsl].reshape(m, d).astype(BF16)
        gb = gb_ref[tsl].reshape(m, d)
        logf = jnp.logaddexp(log_lb, log_1mlb + jax.nn.log_sigmoid(fr))
        k = 1.0 - jnp.exp(logf)
        slabs = [logf[0:nb]]
        for t in range(1, tc):
            slabs.append(slabs[-1] + logf[t * nb:(t + 1) * nb])
        g = jnp.concatenate(slabs, axis=0)
        g_last = jnp.concatenate([slabs[-1]] * tc, axis=0)
        qt = (q * jnp.exp(g)).astype(BF16)
        kt = (k * jnp.exp(jnp.minimum(-g, HGRN_EXP_CLAMP))).astype(BF16)
        ks = (k * jnp.exp(g_last - g)).astype(BF16)
        dec = jnp.exp(slabs[-1])

        outs = []
        for h in range(nh):
            hs = slice(h * dk, (h + 1) * dk)
            qt_h, kt_h, ks_h, v_h = qt[:, hs], kt[:, hs], ks[:, hs], v[:, hs]
            att = lax.dot_general(qt_h, kt_h, (((1,), (1,)), ((), ())), preferred_element_type=F32)
            att = (att * same_seq_causal).astype(BF16)
            o = _dot(att, v_h)
            zero = jnp.zeros_like(qt_h)
            expand = lambda x: jnp.concatenate([jnp.where(bid == b, x, zero) for b in range(nb)], axis=1)
            st_h = st_ref[h]
            o = o + lax.dot_general(expand(qt_h), st_h.astype(BF16), (((1,), (1,)), ((), ())),
                                    preferred_element_type=F32)
            dst = lax.dot_general(v_h, expand(ks_h), (((0,), (0,)), ((), ())), preferred_element_type=F32)
            dec_row = jnp.concatenate([dec[b:b + 1, hs] for b in range(nb)], axis=1)
            st_ref[h] = st_h * dec_row + dst
            outs.append(_rms(o, gn_ref[:, hs]))
        ob = jnp.concatenate(outs, axis=1) * jax.nn.silu(gb)
        ob_ref[pl.ds(pl.multiple_of(c * m, m), m), :] = ob.astype(ob_ref.dtype)
        return carry

    lax.fori_loop(0, n_chunks, chunk, 0)

    @pl.when(i == pl.num_programs(1) - 1)
    def _():
        for b in range(nb):
            for h in range(nh):
                snew_ref[b, h] = st_ref[h, :, b * dk:(b + 1) * dk].T


def hgrn2(proj3, s0, lb_raw, gnorm, layer, n_t, tb, nb):
    bsz, nh, dk, _ = s0.shape
    d = nh * dk
    off = proj3.shape[-1] // d - 4
    nbb = bsz // nb
    nt = proj3.shape[0] // tb
    col = lambda k: pl.BlockSpec((tb, nb, d), lambda j, i, k=k: (i, j, k + off))
    return pl.pallas_call(
        functools.partial(_hgrn2_kernel, layer=layer, n_t=n_t, tb=tb, nb=nb, nh=nh),
        out_shape=(
            jax.ShapeDtypeStruct((proj3.shape[0] * bsz, d), BF16),
            jax.ShapeDtypeStruct(s0.shape, F32),
        ),
        grid=(nbb, nt),
        in_specs=[
            col(0), col(1), col(2), col(3),
            pl.BlockSpec((nb, nh, dk, dk), lambda j, i: (j, 0, 0, 0)),
            _full(lb_raw.shape),
            _full((1, d)),
        ],
        out_specs=(
            pl.BlockSpec((tb * nb, d), lambda j, i: (j * nt + i, 0)),
            pl.BlockSpec((nb, nh, dk, dk), lambda j, i: (j, 0, 0, 0)),
        ),
        scratch_shapes=[pltpu.VMEM((nh, dk, nb * dk), F32)],
        compiler_params=_params("arbitrary", "arbitrary"),
        name="hgrn2",
    )(proj3, proj3, proj3, proj3, s0, lb_raw, gnorm.reshape(1, d))


def _s5_prep_kernel(lr_ref, li_ref, ldt_ref, bre_ref, bim_ref, ar_ref, ai_ref, ore_ref, oim_ref):
    lr = lr_ref[...]
    li = li_ref[...]
    dt = jnp.exp(ldt_ref[...])
    mag = jnp.exp(lr * dt)
    ar = mag * jnp.cos(li * dt)
    ai = mag * jnp.sin(li * dt)
    den = lr * lr + li * li
    cr = ((ar - 1.0) * lr + ai * li) / den
    ci = (ai * lr - (ar - 1.0) * li) / den
    ar_ref[...] = ar
    ai_ref[...] = ai
    ore_ref[...] = cr * bre_ref[...] - ci * bim_ref[...]
    oim_ref[...] = cr * bim_ref[...] + ci * bre_ref[...]


def s5_prep(lam_re, lam_im, log_dt, b_re, b_im):
    g, p, c = b_re.shape
    ns = g * p
    colv = jax.ShapeDtypeStruct((ns, 1), F32)
    mat = jax.ShapeDtypeStruct((ns, c), F32)
    ldt = jnp.broadcast_to(log_dt[:, None], (g, p)).reshape(ns, 1)
    return pl.pallas_call(
        _s5_prep_kernel, out_shape=(colv, colv, mat, mat), name="s5_prep",
    )(lam_re.reshape(ns, 1), lam_im.reshape(ns, 1), ldt, b_re.reshape(ns, c), b_im.reshape(ns, c))


def _s5_kernel(x_ref, g_ref, wb_ref, wc_ref, ar_ref, ai_ref, d_ref, h0r_ref, h0i_ref,
               z_ref, hr_ref, hi_ref, xn_ref, xr_ref, xi_ref, *, n_t, tb, rb):
    i = pl.program_id(0)
    nch = wb_ref.shape[0]
    cw = wb_ref.shape[1]
    sw = wb_ref.shape[2] // 2
    rows = tb * rb
    t_valid = jnp.minimum(n_t - i * tb, tb)

    @pl.when(i == 0)
    def _():
        hr_ref[...] = h0r_ref[...]
        hi_ref[...] = h0i_ref[...]

    xn_ref[...] = _rms(x_ref[...], g_ref[...])
    live = lax.broadcasted_iota(jnp.int32, (rows, cw), 0) < t_valid * rb

    for c in range(nch):
        cs = slice(c * cw, (c + 1) * cw)
        ss = slice(c * sw, (c + 1) * sw)
        u = xn_ref[:, cs]
        bu = _dot(u.astype(BF16), wb_ref[c])
        xr_ref[...] = bu[:, :sw]
        xi_ref[...] = bu[:, sw:]

        ar = jnp.broadcast_to(ar_ref[:, ss], (SUBLANES, sw))
        ai = jnp.broadcast_to(ai_ref[:, ss], (SUBLANES, sw))

        def sub_rows(s, carry, ss=ss, ar=ar, ai=ai):
            s0 = pl.multiple_of(s * SUBLANES, SUBLANES)

            def step(t, h):
                hr, hi = h
                r0 = pl.multiple_of(t * rb + s0, SUBLANES)
                nhr = ar * hr - ai * hi + xr_ref[pl.ds(r0, SUBLANES), :]
                nhi = ar * hi + ai * hr + xi_ref[pl.ds(r0, SUBLANES), :]
                xr_ref[pl.ds(r0, SUBLANES), :] = nhr
                xi_ref[pl.ds(r0, SUBLANES), :] = nhi
                return nhr, nhi

            hr, hi = lax.fori_loop(0, t_valid, step, (hr_ref[pl.ds(s0, SUBLANES), ss], hi_ref[pl.ds(s0, SUBLANES), ss]))
            hr_ref[pl.ds(s0, SUBLANES), ss] = hr
            hi_ref[pl.ds(s0, SUBLANES), ss] = hi
            return carry

        lax.fori_loop(0, rb // SUBLANES, sub_rows, 0)

        hcat = jnp.concatenate([xr_ref[...].astype(BF16), xi_ref[...].astype(BF16)], axis=1)
        y = _dot(hcat, wc_ref[c]) + d_ref[:, cs] * u
        z_ref[:, cs] = jnp.where(live, jax.nn.gelu(y), 0.0).astype(z_ref.dtype)


def s5(x, n_t, g, wb, wc, a_re, a_im, dskip, h0r, h0i, tb, rb):
    n_rows, d = x.shape
    ns = h0r.shape[1]
    rows = tb * rb
    sw = wb.shape[2] // 2
    return pl.pallas_call(
        functools.partial(_s5_kernel, n_t=n_t, tb=tb, rb=rb),
        out_shape=(
            jax.ShapeDtypeStruct((n_rows, d), BF16),
            jax.ShapeDtypeStruct((rb, ns), F32),
            jax.ShapeDtypeStruct((rb, ns), F32),
        ),
        grid=(n_rows // rows,),
        in_specs=[
            pl.BlockSpec((rows, d), lambda i: (i, 0)),
            _full((1, d)), _full(wb.shape), _full(wc.shape),
            _full((1, ns)), _full((1, ns)), _full((1, d)),
            _full((rb, ns)), _full((rb, ns)),
        ],
        out_specs=(
            pl.BlockSpec((rows, d), lambda i: (i, 0)),
            _full((rb, ns)), _full((rb, ns)),
        ),
        scratch_shapes=[
            pltpu.VMEM((rows, d), F32),
            pltpu.VMEM((rows, sw), F32),
            pltpu.VMEM((rows, sw), F32),
        ],
        compiler_params=_params("arbitrary"),
        name="s5",
    )(x, g.reshape(1, d), wb, wc, a_re, a_im, dskip.reshape(1, d), h0r, h0i)


def _s5_block_diag_in(b):
    g, p, c = b.shape
    gpc = V7X_MXU_DIM // c
    bt = b.transpose(0, 2, 1).reshape(g // gpc, gpc, c, p)
    out = jnp.einsum('ngcp,gh->ngchp', bt, jnp.eye(gpc, dtype=b.dtype))
    return out.reshape(g // gpc, gpc * c, gpc * p)


def _s5_block_diag_out(cm):
    g, c, p = cm.shape
    gpc = V7X_MXU_DIM // c
    ct = cm.transpose(0, 2, 1).reshape(g // gpc, gpc, p, c)
    out = jnp.einsum('ngpc,gh->ngphc', ct, jnp.eye(gpc, dtype=cm.dtype))
    return out.reshape(g // gpc, gpc * p, gpc * c)


def _moe_kernel(x_ref, g_ref, wrt_ref, wgu_ref, wdn_ref, o_ref, xn_ref, pos_ref, gate_ref, *, sub):
    e = pl.program_id(1)
    tm = x_ref.shape[0]
    ne = wrt_ref.shape[0]
    dff = wdn_ref.shape[1]

    @pl.when(e == 0)
    def _():
        x = x_ref[...]
        xn = _rms(x, g_ref[...])
        xn_ref[...] = xn.astype(BF16)
        logits = lax.dot_general(wrt_ref[...], xn, (((1,), (1,)), ((), ())),
                                 precision=lax.Precision.HIGHEST, preferred_element_type=F32)
        ex = jnp.exp(logits - jnp.max(logits, axis=0, keepdims=True))
        probs = ex / jnp.sum(ex, axis=0, keepdims=True)
        eid = lax.broadcasted_iota(jnp.int32, (ne, tm), 0).astype(F32)
        m1 = jnp.max(probs, axis=0, keepdims=True)
        i1 = jnp.min(jnp.where(probs == m1, eid, float(ne)), axis=0, keepdims=True)
        sel1 = eid == i1
        rest = jnp.where(sel1, -1.0, probs)
        m2 = jnp.max(rest, axis=0, keepdims=True)
        i2 = jnp.min(jnp.where(rest == m2, eid, float(ne)), axis=0, keepdims=True)
        sel2 = eid == i2
        den = m1 + m2
        gate_ref[...] = jnp.where(sel1, m1 / den, 0.0) + jnp.where(sel2, m2 / den, 0.0)
        chosen = jnp.where(sel1, 1.0, jnp.where(sel2, 1.0, 0.0))
        r = lax.broadcasted_iota(jnp.int32, (tm, tm), 0)
        c = lax.broadcasted_iota(jnp.int32, (tm, tm), 1)
        before = jnp.where(r < c, 1.0, 0.0).astype(BF16)
        rank = _dot(chosen.astype(BF16), before)
        pos_ref[...] = jnp.where(chosen > 0.0, rank, -1.0)
        o_ref[...] = x

    pos_e = pos_ref[pl.ds(e, 1), :]
    gate_e = gate_ref[pl.ds(e, 1), :]
    cnt = jnp.sum(jnp.where(pos_e >= 0.0, 1.0, 0.0)).astype(jnp.int32)

    def body(s, carry):
        slot = lax.broadcasted_iota(jnp.int32, (sub, tm), 0) + s * sub
        hit = pos_e == slot.astype(F32)
        onehot = jnp.where(hit, 1.0, 0.0).astype(BF16)
        xs = _dot(onehot, xn_ref[...]).astype(BF16)
        gs = jnp.sum(jnp.where(hit, gate_e, 0.0), axis=1, keepdims=True)
        hgu = _dot(xs, wgu_ref[0])
        act = (jax.nn.silu(hgu[:, :dff]) * hgu[:, dff:]).astype(BF16)
        yb = (_dot(act, wdn_ref[0]) * gs).astype(BF16)
        o_ref[...] += lax.dot_general(onehot, yb, (((0,), (0,)), ((), ())), preferred_element_type=F32)
        return carry

    lax.fori_loop(0, (cnt + sub - 1) // sub, body, 0)


def moe_residual(x, g, w_router_t, w_gu, w_down, tm, sub):
    n, d = x.shape
    ne, _, dff2 = w_gu.shape
    return pl.pallas_call(
        functools.partial(_moe_kernel, sub=sub),
        out_shape=jax.ShapeDtypeStruct((n, d), F32),
        grid=(n // tm, ne),
        in_specs=[
            pl.BlockSpec((tm, d), lambda i, e: (i, 0)),
            _full((1, d)),
            _full((ne, d)),
            pl.BlockSpec((1, d, dff2), lambda i, e: (e, 0, 0)),
            pl.BlockSpec((1, dff2 // 2, d), lambda i, e: (e, 0, 0)),
        ],
        out_specs=pl.BlockSpec((tm, d), lambda i, e: (i, 0)),
        scratch_shapes=[
            pltpu.VMEM((tm, d), BF16),
            pltpu.VMEM((ne, tm), F32),
            pltpu.VMEM((ne, tm), F32),
        ],
        compiler_params=_params("parallel", "arbitrary"),
        name="moe",
    )(x, g.reshape(1, d), w_router_t, w_gu, w_down)


def _trunk(x, n_t, rb, tb, nb, h0, conv0, s0, re0, im0, w):
    n, d = x.shape
    depth = w['norm_mix'].shape[0]
    d_a = h0.shape[-1]
    nh, dk = s0.shape[2], s0.shape[3]
    d_b = nh * dk
    ns = re0.shape[-2] * re0.shape[-1]
    tm = tb * rb
    new = {k: [] for k in ('h', 'conv', 's', 're', 'im')}
    for l in range(depth):
        li = l // 2
        if l % 2 == 0:
            proj = norm_matmul(x, w['norm_mix'][l], w['even_w_in'][li], tm, 1024)
            conv_tm = conv0[li].transpose(1, 0, 2).reshape((CONV_W - 1) * rb, d_a)
            ya, h_new, conv_new = rglru(proj, n_t, conv_tm, h0[li], w['rglru_conv_w'][li], w['rglru_conv_b'][li],
                                        w['rglru_wg'][li], w['rglru_bg'][li], w['rglru_lambda'][li], tb=tb, rb=rb)
            new['h'].append(h_new)
            new['conv'].append(conv_new.reshape(CONV_W - 1, rb, d_a).transpose(1, 0, 2))
            ob, s_new = hgrn2(proj.reshape(n // rb, rb, proj.shape[1]), s0[li], w['hgrn2_lb_raw'], w['hgrn2_gnorm'][li],
                              li, n_t=n_t, tb=tb, nb=nb)
            if nb != rb:
                ob = ob.reshape(rb // nb, n // rb, nb, d_b).transpose(1, 0, 2, 3).reshape(n, d_b)
            new['s'].append(s_new)
            x = matmul_residual([ya, ob], w['even_w_out'][li], x, tm)
            hff = norm_swiglu(x, w['norm_ffn'][l], w['ffn_w_gu'][li], tm, w['ffn_w_gu'].shape[2] // 4)
            x = matmul_residual([hff], w['ffn_w_down'][li], x, tm)
        else:
            z, re_new, im_new = s5(x, n_t, w['norm_mix'][l], w['s5_wb'][li], w['s5_wc'][li], w['s5_a_re'][li],
                                   w['s5_a_im'][li], w['s5_d'][li], re0[li].reshape(rb, ns), im0[li].reshape(rb, ns),
                                   tb=tb, rb=rb)
            new['re'].append(re_new.reshape(re0.shape[1:]))
            new['im'].append(im_new.reshape(im0.shape[1:]))
            x = glu_residual(z, w['s5_w_glu'][li], x, tm, 512)
            x = moe_residual(x, w['norm_ffn'][l], w['moe_w_router_t'][li], w['moe_w_gu'][li], w['moe_w_down'][li],
                             tm, V7X_MXU_DIM)
    y = rmsnorm_rows(x, w['norm_final'], tm)
    return y, tuple(jnp.stack(new[k]) for k in ('h', 'conv', 's', 're', 'im'))


def kernel(x_prompt, x_sample, state_rglru_h, state_rglru_conv, state_hgrn2, state_s5_re, state_s5_im,
           meta_tokens, norm_mix, norm_ffn, norm_final, even_w_in, even_w_out,
           rglru_conv_w, rglru_conv_b, rglru_w_a, rglru_b_a, rglru_w_x, rglru_b_x, rglru_lambda,
           hgrn2_lb_raw, hgrn2_gnorm, s5_lam_re, s5_lam_im, s5_log_dt, s5_b_re, s5_b_im, s5_c_re, s5_c_im,
           s5_d, s5_w_glu, ffn_w_gu, ffn_w_down, moe_w_router, moe_w_gu, moe_w_down):
    bp, tp0, d = x_prompt.shape
    bs, ts, _ = x_sample.shape
    tp = tp0 + N_META
    d_a = state_rglru_h.shape[-1]
    n_even, n_odd = state_rglru_h.shape[0], state_s5_re.shape[0]
    assert bp == SUBLANES and bs % SUBLANES == 0
    tb_p = PROMPT_BLOCK_ROWS // bp
    tp_pad = -(-tp // tb_p) * tb_p
    nb_s = HGRN_ROWS // ts
    assert (tp % (HGRN_ROWS // bp) == 0 and tb_p % (HGRN_ROWS // bp) == 0 and bs % nb_s == 0
            and nb_s * ts == HGRN_ROWS and nb_s % SUBLANES == 0)

    eye_a = jnp.eye(H_A, dtype=F32)
    block_diag = lambda m: jnp.einsum('lhij,hg->lhigj', m, eye_a).reshape(n_even, d_a, d_a)
    prep = [s5_prep(s5_lam_re[i], s5_lam_im[i], s5_log_dt[i], s5_b_re[i], s5_b_im[i]) for i in range(n_odd)]
    ns = s5_lam_re.shape[1] * s5_lam_re.shape[2]
    bshape = s5_b_re.shape[1:]
    w = {
        'norm_mix': norm_mix, 'norm_ffn': norm_ffn, 'norm_final': norm_final,
        'even_w_in': even_w_in.astype(BF16), 'even_w_out': even_w_out.astype(BF16),
        'rglru_conv_w': rglru_conv_w, 'rglru_conv_b': rglru_conv_b,
        'rglru_wg': jnp.concatenate([block_diag(rglru_w_a), block_diag(rglru_w_x)], axis=2).astype(BF16),
        'rglru_bg': jnp.concatenate([rglru_b_a, rglru_b_x], axis=1),
        'rglru_lambda': rglru_lambda, 'hgrn2_lb_raw': hgrn2_lb_raw, 'hgrn2_gnorm': hgrn2_gnorm,
        's5_a_re': [p[0].reshape(1, ns) for p in prep], 's5_a_im': [p[1].reshape(1, ns) for p in prep],
        's5_wb': [jnp.concatenate([_s5_block_diag_in(p[2].reshape(bshape)), _s5_block_diag_in(p[3].reshape(bshape))],
                                  axis=2).astype(BF16) for p in prep],
        's5_wc': [jnp.concatenate([_s5_block_diag_out(s5_c_re[i]), -_s5_block_diag_out(s5_c_im[i])],
                                  axis=1).astype(BF16) for i in range(n_odd)],
        's5_d': s5_d, 's5_w_glu': s5_w_glu.astype(BF16),
        'ffn_w_gu': ffn_w_gu.astype(BF16), 'ffn_w_down': ffn_w_down.astype(BF16),
        'moe_w_router_t': moe_w_router.transpose(0, 2, 1),
        'moe_w_gu': moe_w_gu.astype(BF16), 'moe_w_down': moe_w_down.astype(BF16),
    }

    meta = jnp.broadcast_to(meta_tokens.astype(x_prompt.dtype)[None], (bp, N_META, d))
    xm = jnp.concatenate([meta, x_prompt], axis=1).transpose(1, 0, 2)
    xm = jnp.pad(xm, ((0, tp_pad - tp), (0, 0), (0, 0))).reshape(tp_pad * bp, d)
    zero = lambda ref, lead: jnp.zeros((lead, bp) + ref.shape[2:], ref.dtype)
    yp, p_new = _trunk(xm, tp, bp, tb_p, bp, zero(state_rglru_h, n_even), zero(state_rglru_conv, n_even),
                       zero(state_hgrn2, n_even), zero(state_s5_re, n_odd), zero(state_s5_im, n_odd), w)
    y_prompt = yp.reshape(tp_pad, bp, d)[N_META:tp].transpose(1, 0, 2)

    xs = x_sample.transpose(1, 0, 2).reshape(ts * bs, d)
    ys, s_new = _trunk(xs, ts, bs, ts, nb_s, state_rglru_h, state_rglru_conv, state_hgrn2, state_s5_re, state_s5_im, w)
    y_sample = ys.reshape(ts, bs, d).transpose(1, 0, 2)

    refs = (state_rglru_h, state_rglru_conv, state_hgrn2, state_s5_re, state_s5_im)
    cast = lambda new: tuple(a.astype(r.dtype) for a, r in zip(new, refs))
    return (y_prompt, y_sample) + cast(p_new) + cast(s_new)
```

```python
import functools

import jax
import jax.numpy as jnp
from jax import lax
from jax.experimental import pallas as pl
from jax.experimental.pallas import tpu as pltpu

F32 = jnp.float32
BF16 = jnp.bfloat16

EPS = 1e-6
N_META = 16
CONV_W = 4
RG_C = 8.0
H_A = 8
S5_P = 64

V7X_VMEM_LIMIT_BYTES = 56 * 1024 * 1024
SUBLANES = 8
LANES = 128
V7X_MXU_DIM = 256
HGRN_ROWS = 128
HGRN_EXP_CLAMP = 80.0
PROMPT_BLOCK_ROWS = 896
SCAN_CARRY_ELEMS = 16 * SUBLANES * LANES


def _params(*sem):
    return pltpu.CompilerParams(dimension_semantics=sem, vmem_limit_bytes=V7X_VMEM_LIMIT_BYTES)


def _rms(x, g):
    ms = jnp.mean(x * x, axis=-1, keepdims=True)
    return x * lax.rsqrt(ms + EPS) * g


def _dot(a, b):
    return jnp.dot(a, b, preferred_element_type=F32)


def _full(shape):
    return pl.BlockSpec(shape, lambda *_: (0,) * len(shape))


def _norm_matmul_kernel(x_ref, g_ref, w_ref, o_ref, xn_ref):
    @pl.when(pl.program_id(1) == 0)
    def _():
        xn_ref[...] = _rms(x_ref[...], g_ref[...]).astype(BF16)

    o_ref[...] = _dot(xn_ref[...], w_ref[...])


def norm_matmul(x, g, w, li, tm, tn):
    n, d = x.shape
    nout = w.shape[2]
    return pl.pallas_call(
        _norm_matmul_kernel,
        out_shape=jax.ShapeDtypeStruct((n, nout), F32),
        grid=(n // tm, nout // tn),
        in_specs=[
            pl.BlockSpec((tm, d), lambda i, j: (i, 0)),
            _full((1, d)),
            pl.BlockSpec((None, d, tn), lambda i, j: (li, 0, j)),
        ],
        out_specs=pl.BlockSpec((tm, tn), lambda i, j: (i, j)),
        scratch_shapes=[pltpu.VMEM((tm, d), BF16)],
        compiler_params=_params("parallel", "arbitrary"),
        name="norm_matmul",
    )(x, g.reshape(1, d), w)


def _norm_swiglu_kernel(x_ref, g_ref, wg_ref, wu_ref, o_ref, xn_ref):
    @pl.when(pl.program_id(1) == 0)
    def _():
        xn_ref[...] = _rms(x_ref[...], g_ref[...]).astype(BF16)

    xn = xn_ref[...]
    gate = _dot(xn, wg_ref[...])
    up = _dot(xn, wu_ref[...])
    o_ref[...] = (jax.nn.silu(gate) * up).astype(o_ref.dtype)


def norm_swiglu(x, g, w_gu, li, tm, tn):
    n, d = x.shape
    dff = w_gu.shape[2] // 2
    nj = dff // tn
    return pl.pallas_call(
        _norm_swiglu_kernel,
        out_shape=jax.ShapeDtypeStruct((n, dff), BF16),
        grid=(n // tm, nj),
        in_specs=[
            pl.BlockSpec((tm, d), lambda i, j: (i, 0)),
            _full((1, d)),
            pl.BlockSpec((None, d, tn), lambda i, j: (li, 0, j)),
            pl.BlockSpec((None, d, tn), lambda i, j: (li, 0, j + nj)),
        ],
        out_specs=pl.BlockSpec((tm, tn), lambda i, j: (i, j)),
        scratch_shapes=[pltpu.VMEM((tm, d), BF16)],
        compiler_params=_params("parallel", "arbitrary"),
        name="norm_swiglu",
    )(x, g.reshape(1, d), w_gu, w_gu)


def _matmul_residual_kernel(*refs, n_in):
    a_refs = refs[:n_in]
    w_refs = refs[n_in:2 * n_in]
    r_ref = refs[2 * n_in]
    o_ref = refs[2 * n_in + 1]
    acc = r_ref[...]
    for a_ref, w_ref in zip(a_refs, w_refs):
        acc = acc + _dot(a_ref[...], w_ref[...])
    o_ref[...] = acc


def matmul_residual(a_list, w, li, res, tm):
    n, dout = res.shape
    n_in = len(a_list)
    in_specs = []
    for a in a_list:
        in_specs.append(pl.BlockSpec((tm, a.shape[1]), lambda i: (i, 0)))
    for k, a in enumerate(a_list):
        assert a.shape[1] * n_in == w.shape[1]
        in_specs.append(pl.BlockSpec((None, a.shape[1], dout), lambda i, k=k: (li, k, 0)))
    in_specs.append(pl.BlockSpec((tm, dout), lambda i: (i, 0)))
    return pl.pallas_call(
        functools.partial(_matmul_residual_kernel, n_in=n_in),
        out_shape=jax.ShapeDtypeStruct((n, dout), F32),
        grid=(n // tm,),
        in_specs=in_specs,
        out_specs=pl.BlockSpec((tm, dout), lambda i: (i, 0)),
        compiler_params=_params("parallel"),
        name="matmul_residual",
    )(*a_list, *([w] * n_in), res)


def _glu_residual_kernel(z_ref, wv_ref, wg_ref, r_ref, o_ref):
    z = z_ref[...]
    val = _dot(z, wv_ref[...])
    gate = _dot(z, wg_ref[...])
    o_ref[...] = r_ref[...] + val * jax.nn.sigmoid(gate)


def glu_residual(z, w_glu, li, res, tm, tn):
    n, d = z.shape
    dout = w_glu.shape[2] // 2
    nj = dout // tn
    return pl.pallas_call(
        _glu_residual_kernel,
        out_shape=jax.ShapeDtypeStruct((n, dout), F32),
        grid=(n // tm, nj),
        in_specs=[
            pl.BlockSpec((tm, d), lambda i, j: (i, 0)),
            pl.BlockSpec((None, d, tn), lambda i, j: (li, 0, j)),
            pl.BlockSpec((None, d, tn), lambda i, j: (li, 0, j + nj)),
            pl.BlockSpec((tm, tn), lambda i, j: (i, j)),
        ],
        out_specs=pl.BlockSpec((tm, tn), lambda i, j: (i, j)),
        compiler_params=_params("parallel", "arbitrary"),
        name="glu_residual",
    )(z, w_glu, w_glu, res)


def _rmsnorm_kernel(x_ref, g_ref, o_ref):
    o_ref[...] = _rms(x_ref[...], g_ref[...])


def rmsnorm_rows(x, g, tm):
    n, d = x.shape
    return pl.pallas_call(
        _rmsnorm_kernel,
        out_shape=jax.ShapeDtypeStruct((n, d), F32),
        grid=(n // tm,),
        in_specs=[pl.BlockSpec((tm, d), lambda i: (i, 0)), _full((1, d))],
        out_specs=pl.BlockSpec((tm, d), lambda i: (i, 0)),
        compiler_params=_params("parallel"),
        name="final_rmsnorm",
    )(x, g.reshape(1, d))


def _rmsnorm_batch_major_kernel(*refs, nk, tc, nb):
    x_refs, g_ref, o_ref = refs[:nk], refs[nk], refs[nk + 1]
    d = o_ref.shape[-1]
    for k in range(nk):
        y = _rms(x_refs[k][...], g_ref[...])
        o_ref[:, k * tc:(k + 1) * tc, :] = jnp.swapaxes(y.reshape(tc, nb, d), 0, 1)


def rmsnorm_batch_major(x, g, nb, t_skip, t_out, tc, nk):
    n, d = x.shape
    rows = tc * nb
    assert t_skip % tc == 0 and t_out % (nk * tc) == 0
    specs = [pl.BlockSpec((rows, d), lambda j, k=k: (nk * j + t_skip // tc + k, 0)) for k in range(nk)]
    return pl.pallas_call(
        functools.partial(_rmsnorm_batch_major_kernel, nk=nk, tc=tc, nb=nb),
        out_shape=jax.ShapeDtypeStruct((nb, t_out, d), F32),
        grid=(t_out // (nk * tc),),
        in_specs=specs + [_full((1, d))],
        out_specs=pl.BlockSpec((nb, nk * tc, d), lambda j: (0, j, 0)),
        compiler_params=_params("parallel"),
        name="final_rmsnorm_batch_major",
    )(*([x] * nk), g.reshape(1, d))


def _to_time_major_kernel(*refs, nk, n_chunks):
    x_refs, lead_ref, o_ref = refs[:nk], refs[nk], refs[nk + 1]
    i = pl.program_id(0)
    nb, tc, d = x_refs[0].shape
    rows = tc * nb
    for k in range(nk):
        chunk = i * nk + k - 1
        val = jnp.swapaxes(x_refs[k][...], 0, 1).reshape(rows, d)
        if k == 0:
            lead = jnp.broadcast_to(lead_ref[...][:, None, :], (tc, nb, d)).reshape(rows, d)
            val = jnp.where(i == 0, lead, val)
        o_ref[k * rows:(k + 1) * rows, :] = jnp.where(chunk < n_chunks, val, 0.0)


def to_time_major(x, lead, t_pad, nk):
    nb, t, d = x.shape
    tc = lead.shape[0]
    assert t % tc == 0 and t_pad % (nk * tc) == 0
    n_chunks = t // tc
    specs = [pl.BlockSpec((nb, tc, d), lambda i, k=k: (0, jnp.clip(i * nk + k - 1, 0, n_chunks - 1), 0))
             for k in range(nk)]
    return pl.pallas_call(
        functools.partial(_to_time_major_kernel, nk=nk, n_chunks=n_chunks),
        out_shape=jax.ShapeDtypeStruct((t_pad * nb, d), F32),
        grid=(t_pad // (nk * tc),),
        in_specs=specs + [_full((tc, d))],
        out_specs=pl.BlockSpec((nk * tc * nb, d), lambda i: (i, 0)),
        compiler_params=_params("parallel"),
        name="to_time_major",
    )(*([x] * nk), lead)


def _rglru_kernel(xa_ref, ga_ref, conv0_ref, h0_ref, cw_ref, cb_ref, wg_ref, bg_ref, lam_ref,
                  ya_ref, hlast_ref, convnew_ref, xpad_ref, a_ref, u_ref, *, n_t, tb, rb):
    i = pl.program_id(0)
    rows = tb * rb
    tail = (CONV_W - 1) * rb
    c = xa_ref.shape[-1]
    t_valid = jnp.minimum(n_t - i * tb, tb)

    @pl.when(i == 0)
    def _():
        xpad_ref[0:tail, :] = conv0_ref[...]
        hlast_ref[...] = h0_ref[...]

    @pl.when(i > 0)
    def _():
        xpad_ref[0:tail, :] = xpad_ref[rows:rows + tail, :]

    xpad_ref[tail:tail + rows, :] = xa_ref[...]
    xc = cb_ref[...]
    for k in range(CONV_W):
        xc = xc + cw_ref[k:k + 1, :] * xpad_ref[k * rb:k * rb + rows, :]

    gates = _dot(xc.astype(BF16), wg_ref[...]) + bg_ref[...]
    r = jax.nn.sigmoid(gates[:, :c])
    ig = jax.nn.sigmoid(gates[:, c:])
    log_a = (-RG_C) * r * jax.nn.softplus(-lam_ref[...])
    a = jnp.exp(log_a)
    mult = jnp.sqrt(-jnp.tanh(log_a) * (a * a + 1.0))
    a_ref[...] = a
    u_ref[...] = mult * ig * xc

    lc = min(c, max(LANES, SCAN_CARRY_ELEMS // rb // LANES * LANES))
    for c0 in range(0, c, lc):
        def body(t, h, c0=c0):
            sl = pl.ds(pl.multiple_of(t * rb, rb), rb)
            h = a_ref[sl, c0:c0 + lc] * h + u_ref[sl, c0:c0 + lc]
            u_ref[sl, c0:c0 + lc] = h
            return h

        hlast_ref[:, c0:c0 + lc] = lax.fori_loop(0, t_valid, body, hlast_ref[:, c0:c0 + lc])

    live = lax.broadcasted_iota(jnp.int32, (rows, c), 0) < t_valid * rb
    ya_ref[...] = jnp.where(live, u_ref[...] * jax.nn.gelu(ga_ref[...]), 0.0).astype(ya_ref.dtype)
    convnew_ref[...] = xpad_ref[pl.ds(pl.multiple_of(t_valid * rb, rb), tail), :]


def rglru(proj, n_t, conv0, h0, cw, cb, wg, bg, lam, tb, rb):
    c = h0.shape[1]
    n_rows = proj.shape[0]
    rows = tb * rb
    tail = (CONV_W - 1) * rb
    return pl.pallas_call(
        functools.partial(_rglru_kernel, n_t=n_t, tb=tb, rb=rb),
        out_shape=(
            jax.ShapeDtypeStruct((n_rows, c), BF16),
            jax.ShapeDtypeStruct((rb, c), F32),
            jax.ShapeDtypeStruct((tail, c), F32),
        ),
        grid=(n_rows // rows,),
        in_specs=[
            pl.BlockSpec((rows, c), lambda i: (i, 0)),
            pl.BlockSpec((rows, c), lambda i: (i, 1)),
            _full((tail, c)), _full((rb, c)), _full((CONV_W, c)), _full((1, c)),
            _full((c, 2 * c)), _full((1, 2 * c)), _full((1, c)),
        ],
        out_specs=(
            pl.BlockSpec((rows, c), lambda i: (i, 0)),
            _full((rb, c)),
            _full((tail, c)),
        ),
        scratch_shapes=[
            pltpu.VMEM((rows + tail, c), F32),
            pltpu.VMEM((rows, c), F32),
            pltpu.VMEM((rows, c), F32),
        ],
        compiler_params=_params("arbitrary"),
        name="rglru",
    )(proj, proj, conv0, h0, cw, cb.reshape(1, c), wg, bg.reshape(1, 2 * c), lam.reshape(1, c))


def _hgrn2_kernel(q_ref, f_ref, v_ref, gb_ref, s0_ref, lbraw_ref, gn_ref, *rest, layer, n_t, tb, nb, nh):
    ob_ref, snew_ref, st_ref = rest[-3:]
    i = pl.program_id(1)
    m = HGRN_ROWS
    tc = m // nb
    d = q_ref.shape[-1]
    dk = d // nh
    n_chunks = jnp.minimum(n_t - i * tb, tb) // tc

    @pl.when(i == 0)
    def _():
        for b in range(nb):
            for h in range(nh):
                st_ref[h, :, b * dk:(b + 1) * dk] = s0_ref[b, h].T

    @pl.when(n_chunks < tb // tc)
    def _():
        ob_ref[...] = jnp.zeros_like(ob_ref)

    p = jax.nn.softmax(lbraw_ref[...], axis=0)
    cum = p[0:1, :]
    for r in range(1, layer + 1):
        cum = cum + p[r:r + 1, :]
    lb = cum - p[0:1, :]
    log_lb = jnp.log(lb)
    log_1mlb = jnp.log1p(-lb)

    row = lax.broadcasted_iota(jnp.int32, (m, m), 0)
    col = lax.broadcasted_iota(jnp.int32, (m, m), 1)
    same_seq_causal = jnp.where(((row & (nb - 1)) == (col & (nb - 1))) & (col <= row), 1.0, 0.0)
    bid = lax.broadcasted_iota(jnp.int32, (m, dk), 0) & (nb - 1)

    def chunk(c, carry):
        tsl = pl.ds(c * tc, tc)
        q = jax.nn.silu(q_ref[tsl].reshape(m, d))
        fr = f_ref[tsl].reshape(m, d)
        v = v_ref[tsl].reshape(m, d).astype(BF16)
        gb = gb_ref[tsl].reshape(m, d)
        logf = jnp.logaddexp(log_lb, log_1mlb + jax.nn.log_sigmoid(fr))
        k = 1.0 - jnp.exp(logf)
        slabs = [logf[0:nb]]
        for t in range(1, tc):
            slabs.append(slabs[-1] + logf[t * nb:(t + 1) * nb])
        g = jnp.concatenate(slabs, axis=0)
        g_last = jnp.concatenate([slabs[-1]] * tc, axis=0)
        qt = (q * jnp.exp(g)).astype(BF16)
        kt = (k * jnp.exp(jnp.minimum(-g, HGRN_EXP_CLAMP))).astype(BF16)
        ks = (k * jnp.exp(g_last - g)).astype(BF16)
        dec = jnp.exp(slabs[-1])

        outs = []
        for h in range(nh):
            hs = slice(h * dk, (h + 1) * dk)
            qt_h, kt_h, ks_h, v_h = qt[:, hs], kt[:, hs], ks[:, hs], v[:, hs]
            att = lax.dot_general(qt_h, kt_h, (((1,), (1,)), ((), ())), preferred_element_type=F32)
            att = (att * same_seq_causal).astype(BF16)
            o = _dot(att, v_h)
            zero = jnp.zeros_like(qt_h)
            expand = lambda x: jnp.concatenate([jnp.where(bid == b, x, zero) for b in range(nb)], axis=1)
            st_h = st_ref[h]
            o = o + lax.dot_general(expand(qt_h), st_h.astype(BF16), (((1,), (1,)), ((), ())),
                                    preferred_element_type=F32)
            dst = lax.dot_general(v_h, expand(ks_h), (((0,), (0,)), ((), ())), preferred_element_type=F32)
            dec_row = jnp.concatenate([dec[b:b + 1, hs] for b in range(nb)], axis=1)
            st_ref[h] = st_h * dec_row + dst
            outs.append(_rms(o, gn_ref[:, hs]))
        ob = jnp.concatenate(outs, axis=1) * jax.nn.silu(gb)
        ob_ref[pl.ds(pl.multiple_of(c * m, m), m), :] = ob.astype(ob_ref.dtype)
        return carry

    lax.fori_loop(0, n_chunks, chunk, 0)

    @pl.when(i == pl.num_programs(1) - 1)
    def _():
        for b in range(nb):
            for h in range(nh):
                snew_ref[b, h] = st_ref[h, :, b * dk:(b + 1) * dk].T


def hgrn2(proj3, s0, s0_layer, s_stack, n_layers, lb_raw, gnorm, layer, n_t, tb, nb):
    _, bsz, nh, dk, _ = s0.shape
    d = nh * dk
    off = proj3.shape[-1] // d - 4
    nbb = bsz // nb
    nt = proj3.shape[0] // tb
    col = lambda k: pl.BlockSpec((tb, nb, d), lambda j, i, k=k: (i, j, k + off))
    in_specs = [
        col(0), col(1), col(2), col(3),
        pl.BlockSpec((None, nb, nh, dk, dk), lambda j, i: (s0_layer, j, 0, 0, 0)),
        _full(lb_raw.shape),
        _full((1, d)),
    ]
    args = [proj3, proj3, proj3, proj3, s0, lb_raw, gnorm.reshape(1, d)]
    aliases = {}
    if s_stack is not None:
        in_specs.append(pl.BlockSpec(memory_space=pl.ANY))
        args.append(s_stack)
        aliases = {len(args) - 1: 1}
    return pl.pallas_call(
        functools.partial(_hgrn2_kernel, layer=layer, n_t=n_t, tb=tb, nb=nb, nh=nh),
        out_shape=(
            jax.ShapeDtypeStruct((proj3.shape[0] * bsz, d), BF16),
            jax.ShapeDtypeStruct((n_layers, bsz, nh, dk, dk), F32),
        ),
        grid=(nbb, nt),
        in_specs=in_specs,
        out_specs=(
            pl.BlockSpec((tb * nb, d), lambda j, i: (j * nt + i, 0)),
            pl.BlockSpec((None, nb, nh, dk, dk), lambda j, i: (layer, j, 0, 0, 0)),
        ),
        scratch_shapes=[pltpu.VMEM((nh, dk, nb * dk), F32)],
        input_output_aliases=aliases,
        compiler_params=_params("arbitrary", "arbitrary"),
        name="hgrn2",
    )(*args)


def _s5_prep_kernel(lr_ref, li_ref, ldt_ref, bre_ref, bim_ref, ar_ref, ai_ref, ore_ref, oim_ref):
    lr = lr_ref[...]
    li = li_ref[...]
    dt = jnp.exp(ldt_ref[...])
    mag = jnp.exp(lr * dt)
    ar = mag * jnp.cos(li * dt)
    ai = mag * jnp.sin(li * dt)
    den = lr * lr + li * li
    cr = ((ar - 1.0) * lr + ai * li) / den
    ci = (ai * lr - (ar - 1.0) * li) / den
    ar_ref[...] = ar
    ai_ref[...] = ai
    ore_ref[...] = cr * bre_ref[...] - ci * bim_ref[...]
    oim_ref[...] = cr * bim_ref[...] + ci * bre_ref[...]


def s5_prep(lam_re, lam_im, log_dt, b_re, b_im):
    g, p, c = b_re.shape
    ns = g * p
    colv = jax.ShapeDtypeStruct((ns, 1), F32)
    mat = jax.ShapeDtypeStruct((ns, c), F32)
    ldt = jnp.broadcast_to(log_dt[:, None], (g, p)).reshape(ns, 1)
    return pl.pallas_call(
        _s5_prep_kernel, out_shape=(colv, colv, mat, mat), name="s5_prep",
    )(lam_re.reshape(ns, 1), lam_im.reshape(ns, 1), ldt, b_re.reshape(ns, c), b_im.reshape(ns, c))


def _s5_kernel(x_ref, g_ref, wb_ref, wc_ref, ar_ref, ai_ref, d_ref, h0r_ref, h0i_ref,
               z_ref, hr_ref, hi_ref, xn_ref, xr_ref, xi_ref, *, n_t, tb, rb):
    i = pl.program_id(0)
    nch = wb_ref.shape[0]
    cw = wb_ref.shape[1]
    sw = wb_ref.shape[2] // 2
    rows = tb * rb
    t_valid = jnp.minimum(n_t - i * tb, tb)

    @pl.when(i == 0)
    def _():
        hr_ref[...] = h0r_ref[...]
        hi_ref[...] = h0i_ref[...]

    xn_ref[...] = _rms(x_ref[...], g_ref[...])
    live = lax.broadcasted_iota(jnp.int32, (rows, cw), 0) < t_valid * rb

    for c in range(nch):
        cs = slice(c * cw, (c + 1) * cw)
        ss = slice(c * sw, (c + 1) * sw)
        u = xn_ref[:, cs]
        bu = _dot(u.astype(BF16), wb_ref[c])
        xr_ref[...] = bu[:, :sw]
        xi_ref[...] = bu[:, sw:]

        ar = jnp.broadcast_to(ar_ref[:, ss], (SUBLANES, sw))
        ai = jnp.broadcast_to(ai_ref[:, ss], (SUBLANES, sw))

        def sub_rows(s, carry, ss=ss, ar=ar, ai=ai):
            s0 = pl.multiple_of(s * SUBLANES, SUBLANES)

            def step(t, h):
                hr, hi = h
                r0 = pl.multiple_of(t * rb + s0, SUBLANES)
                nhr = ar * hr - ai * hi + xr_ref[pl.ds(r0, SUBLANES), :]
                nhi = ar * hi + ai * hr + xi_ref[pl.ds(r0, SUBLANES), :]
                xr_ref[pl.ds(r0, SUBLANES), :] = nhr
                xi_ref[pl.ds(r0, SUBLANES), :] = nhi
                return nhr, nhi

            hr, hi = lax.fori_loop(0, t_valid, step, (hr_ref[pl.ds(s0, SUBLANES), ss], hi_ref[pl.ds(s0, SUBLANES), ss]))
            hr_ref[pl.ds(s0, SUBLANES), ss] = hr
            hi_ref[pl.ds(s0, SUBLANES), ss] = hi
            return carry

        lax.fori_loop(0, rb // SUBLANES, sub_rows, 0)

        hcat = jnp.concatenate([xr_ref[...].astype(BF16), xi_ref[...].astype(BF16)], axis=1)
        y = _dot(hcat, wc_ref[c]) + d_ref[:, cs] * u
        z_ref[:, cs] = jnp.where(live, jax.nn.gelu(y), 0.0).astype(z_ref.dtype)


def s5(x, n_t, g, wb, wc, a_re, a_im, dskip, h0r, h0i, tb, rb):
    n_rows, d = x.shape
    ns = h0r.shape[1]
    rows = tb * rb
    sw = wb.shape[2] // 2
    return pl.pallas_call(
        functools.partial(_s5_kernel, n_t=n_t, tb=tb, rb=rb),
        out_shape=(
            jax.ShapeDtypeStruct((n_rows, d), BF16),
            jax.ShapeDtypeStruct((rb, ns), F32),
            jax.ShapeDtypeStruct((rb, ns), F32),
        ),
        grid=(n_rows // rows,),
        in_specs=[
            pl.BlockSpec((rows, d), lambda i: (i, 0)),
            _full((1, d)), _full(wb.shape), _full(wc.shape),
            _full((1, ns)), _full((1, ns)), _full((1, d)),
            _full((rb, ns)), _full((rb, ns)),
        ],
        out_specs=(
            pl.BlockSpec((rows, d), lambda i: (i, 0)),
            _full((rb, ns)), _full((rb, ns)),
        ),
        scratch_shapes=[
            pltpu.VMEM((rows, d), F32),
            pltpu.VMEM((rows, sw), F32),
            pltpu.VMEM((rows, sw), F32),
        ],
        compiler_params=_params("arbitrary"),
        name="s5",
    )(x, g.reshape(1, d), wb, wc, a_re, a_im, dskip.reshape(1, d), h0r, h0i)


def _s5_block_diag_in(b):
    g, p, c = b.shape
    gpc = V7X_MXU_DIM // c
    bt = b.transpose(0, 2, 1).reshape(g // gpc, gpc, c, p)
    out = jnp.einsum('ngcp,gh->ngchp', bt, jnp.eye(gpc, dtype=b.dtype))
    return out.reshape(g // gpc, gpc * c, gpc * p)


def _s5_block_diag_out(cm):
    g, c, p = cm.shape
    gpc = V7X_MXU_DIM // c
    ct = cm.transpose(0, 2, 1).reshape(g // gpc, gpc, p, c)
    out = jnp.einsum('ngpc,gh->ngphc', ct, jnp.eye(gpc, dtype=cm.dtype))
    return out.reshape(g // gpc, gpc * p, gpc * c)


def _moe_kernel(x_ref, g_ref, wrt_ref, wgu_ref, wdn_ref, o_ref, xn_ref, pos_ref, gate_ref, *, sub):
    e = pl.program_id(1)
    tm = x_ref.shape[0]
    ne = wrt_ref.shape[0]
    dff = wdn_ref.shape[1]

    @pl.when(e == 0)
    def _():
        x = x_ref[...]
        xn = _rms(x, g_ref[...])
        xn_ref[...] = xn.astype(BF16)
        logits = lax.dot_general(wrt_ref[...], xn, (((1,), (1,)), ((), ())),
                                 precision=lax.Precision.HIGHEST, preferred_element_type=F32)
        ex = jnp.exp(logits - jnp.max(logits, axis=0, keepdims=True))
        probs = ex / jnp.sum(ex, axis=0, keepdims=True)
        eid = lax.broadcasted_iota(jnp.int32, (ne, tm), 0).astype(F32)
        m1 = jnp.max(probs, axis=0, keepdims=True)
        i1 = jnp.min(jnp.where(probs == m1, eid, float(ne)), axis=0, keepdims=True)
        sel1 = eid == i1
        rest = jnp.where(sel1, -1.0, probs)
        m2 = jnp.max(rest, axis=0, keepdims=True)
        i2 = jnp.min(jnp.where(rest == m2, eid, float(ne)), axis=0, keepdims=True)
        sel2 = eid == i2
        den = m1 + m2
        gate_ref[...] = jnp.where(sel1, m1 / den, 0.0) + jnp.where(sel2, m2 / den, 0.0)
        chosen = jnp.where(sel1, 1.0, jnp.where(sel2, 1.0, 0.0))
        r = lax.broadcasted_iota(jnp.int32, (tm, tm), 0)
        c = lax.broadcasted_iota(jnp.int32, (tm, tm), 1)
        before = jnp.where(r < c, 1.0, 0.0).astype(BF16)
        rank = _dot(chosen.astype(BF16), before)
        pos_ref[...] = jnp.where(chosen > 0.0, rank, -1.0)
        o_ref[...] = x

    pos_e = pos_ref[pl.ds(e, 1), :]
    gate_e = gate_ref[pl.ds(e, 1), :]
    cnt = jnp.sum(jnp.where(pos_e >= 0.0, 1.0, 0.0)).astype(jnp.int32)

    def body(s, carry):
        slot = lax.broadcasted_iota(jnp.int32, (sub, tm), 0) + s * sub
        hit = pos_e == slot.astype(F32)
        onehot = jnp.where(hit, 1.0, 0.0).astype(BF16)
        xs = _dot(onehot, xn_ref[...]).astype(BF16)
        gs = jnp.sum(jnp.where(hit, gate_e, 0.0), axis=1, keepdims=True)
        hgu = _dot(xs, wgu_ref[0])
        act = (jax.nn.silu(hgu[:, :dff]) * hgu[:, dff:]).astype(BF16)
        yb = (_dot(act, wdn_ref[0]) * gs).astype(BF16)
        o_ref[...] += lax.dot_general(onehot, yb, (((0,), (0,)), ((), ())), preferred_element_type=F32)
        return carry

    lax.fori_loop(0, (cnt + sub - 1) // sub, body, 0)


def moe_residual(x, g, w_router_t, w_gu, w_down, li, tm, sub):
    n, d = x.shape
    _, ne, _, dff2 = w_gu.shape
    return pl.pallas_call(
        functools.partial(_moe_kernel, sub=sub),
        out_shape=jax.ShapeDtypeStruct((n, d), F32),
        grid=(n // tm, ne),
        in_specs=[
            pl.BlockSpec((tm, d), lambda i, e: (i, 0)),
            _full((1, d)),
            _full((ne, d)),
            pl.BlockSpec((None, 1, d, dff2), lambda i, e: (li, e, 0, 0)),
            pl.BlockSpec((None, 1, dff2 // 2, d), lambda i, e: (li, e, 0, 0)),
        ],
        out_specs=pl.BlockSpec((tm, d), lambda i, e: (i, 0)),
        scratch_shapes=[
            pltpu.VMEM((tm, d), BF16),
            pltpu.VMEM((ne, tm), F32),
            pltpu.VMEM((ne, tm), F32),
        ],
        compiler_params=_params("parallel", "arbitrary"),
        name="moe",
    )(x, g.reshape(1, d), w_router_t, w_gu, w_down)


def _trunk(x, n_t, rb, tb, nb, h0, conv0, s0, re0, im0, w):
    n, d = x.shape
    depth = w['norm_mix'].shape[0]
    d_a = h0.shape[-1]
    nh, dk = s0.shape[2], s0.shape[3]
    d_b = nh * dk
    ns = re0.shape[-2] * re0.shape[-1]
    tm = tb * rb
    n_even = (depth + 1) // 2
    new = {k: [] for k in ('h', 'conv', 're', 'im')}
    s_stack = None
    for l in range(depth):
        li = l // 2
        if l % 2 == 0:
            proj = norm_matmul(x, w['norm_mix'][l], w['even_w_in'], li, tm, 1024)
            conv_tm = conv0[li].transpose(1, 0, 2).reshape((CONV_W - 1) * rb, d_a)
            ya, h_new, conv_new = rglru(proj, n_t, conv_tm, h0[li], w['rglru_conv_w'][li], w['rglru_conv_b'][li],
                                        w['rglru_wg'][li], w['rglru_bg'][li], w['rglru_lambda'][li], tb=tb, rb=rb)
            new['h'].append(h_new)
            new['conv'].append(conv_new.reshape(CONV_W - 1, rb, d_a).transpose(1, 0, 2))
            ob, s_stack = hgrn2(proj.reshape(n // rb, rb, proj.shape[1]), s0, li % s0.shape[0], s_stack, n_even,
                                w['hgrn2_lb_raw'], w['hgrn2_gnorm'][li], li, n_t=n_t, tb=tb, nb=nb)
            if nb != rb:
                ob = ob.reshape(rb // nb, n // rb, nb, d_b).transpose(1, 0, 2, 3).reshape(n, d_b)
            x = matmul_residual([ya, ob], w['even_w_out'], li, x, tm)
            hff = norm_swiglu(x, w['norm_ffn'][l], w['ffn_w_gu'], li, tm, w['ffn_w_gu'].shape[2] // 4)
            x = matmul_residual([hff], w['ffn_w_down'], li, x, tm)
        else:
            z, re_new, im_new = s5(x, n_t, w['norm_mix'][l], w['s5_wb'][li], w['s5_wc'][li], w['s5_a_re'][li],
                                   w['s5_a_im'][li], w['s5_d'][li], re0[li].reshape(rb, ns), im0[li].reshape(rb, ns),
                                   tb=tb, rb=rb)
            new['re'].append(re_new.reshape(re0.shape[1:]))
            new['im'].append(im_new.reshape(im0.shape[1:]))
            x = glu_residual(z, w['s5_w_glu'], li, x, tm, 512)
            x = moe_residual(x, w['norm_ffn'][l], w['moe_w_router_t'][li], w['moe_w_gu'], w['moe_w_down'], li,
                             tm, V7X_MXU_DIM)
    stack = lambda k: jnp.stack(new[k])
    return x, (stack('h'), stack('conv'), s_stack, stack('re'), stack('im'))


def kernel(x_prompt, x_sample, state_rglru_h, state_rglru_conv, state_hgrn2, state_s5_re, state_s5_im,
           meta_tokens, norm_mix, norm_ffn, norm_final, even_w_in, even_w_out,
           rglru_conv_w, rglru_conv_b, rglru_w_a, rglru_b_a, rglru_w_x, rglru_b_x, rglru_lambda,
           hgrn2_lb_raw, hgrn2_gnorm, s5_lam_re, s5_lam_im, s5_log_dt, s5_b_re, s5_b_im, s5_c_re, s5_c_im,
           s5_d, s5_w_glu, ffn_w_gu, ffn_w_down, moe_w_router, moe_w_gu, moe_w_down):
    bp, tp0, d = x_prompt.shape
    bs, ts, _ = x_sample.shape
    tp = tp0 + N_META
    d_a = state_rglru_h.shape[-1]
    n_even, n_odd = state_rglru_h.shape[0], state_s5_re.shape[0]
    assert bp == SUBLANES and bs % SUBLANES == 0
    tb_p = PROMPT_BLOCK_ROWS // bp
    tp_pad = -(-tp // tb_p) * tb_p
    nb_s = HGRN_ROWS // ts
    assert (tp % (HGRN_ROWS // bp) == 0 and tb_p % (HGRN_ROWS // bp) == 0 and bs % nb_s == 0
            and nb_s * ts == HGRN_ROWS and nb_s % SUBLANES == 0)

    eye_a = jnp.eye(H_A, dtype=F32)
    block_diag = lambda m: jnp.einsum('lhij,hg->lhigj', m, eye_a).reshape(n_even, d_a, d_a)
    prep = [s5_prep(s5_lam_re[i], s5_lam_im[i], s5_log_dt[i], s5_b_re[i], s5_b_im[i]) for i in range(n_odd)]
    ns = s5_lam_re.shape[1] * s5_lam_re.shape[2]
    bshape = s5_b_re.shape[1:]
    w = {
        'norm_mix': norm_mix, 'norm_ffn': norm_ffn, 'norm_final': norm_final,
        'even_w_in': even_w_in.astype(BF16), 'even_w_out': even_w_out.astype(BF16),
        'rglru_conv_w': rglru_conv_w, 'rglru_conv_b': rglru_conv_b,
        'rglru_wg': jnp.concatenate([block_diag(rglru_w_a), block_diag(rglru_w_x)], axis=2).astype(BF16),
        'rglru_bg': jnp.concatenate([rglru_b_a, rglru_b_x], axis=1),
        'rglru_lambda': rglru_lambda, 'hgrn2_lb_raw': hgrn2_lb_raw, 'hgrn2_gnorm': hgrn2_gnorm,
        's5_a_re': [p[0].reshape(1, ns) for p in prep], 's5_a_im': [p[1].reshape(1, ns) for p in prep],
        's5_wb': [jnp.concatenate([_s5_block_diag_in(p[2].reshape(bshape)), _s5_block_diag_in(p[3].reshape(bshape))],
                                  axis=2).astype(BF16) for p in prep],
        's5_wc': [jnp.concatenate([_s5_block_diag_out(s5_c_re[i]), -_s5_block_diag_out(s5_c_im[i])],
                                  axis=1).astype(BF16) for i in range(n_odd)],
        's5_d': s5_d, 's5_w_glu': s5_w_glu.astype(BF16),
        'ffn_w_gu': ffn_w_gu.astype(BF16), 'ffn_w_down': ffn_w_down.astype(BF16),
        'moe_w_router_t': moe_w_router.transpose(0, 2, 1),
        'moe_w_gu': moe_w_gu.astype(BF16), 'moe_w_down': moe_w_down.astype(BF16),
    }

    tc_p = HGRN_ROWS // bp
    assert N_META == tc_p
    xm = to_time_major(x_prompt, meta_tokens.astype(x_prompt.dtype), tp_pad, tb_p // tc_p)
    zero = lambda ref, lead: jnp.zeros((lead, bp) + ref.shape[2:], ref.dtype)
    xp, p_new = _trunk(xm, tp, bp, tb_p, bp, zero(state_rglru_h, n_even), zero(state_rglru_conv, n_even),
                       zero(state_hgrn2, 1), zero(state_s5_re, n_odd), zero(state_s5_im, n_odd), w)
    nk_out = max(k for k in range(1, HGRN_ROWS // tc_p + 1) if tp0 % (k * tc_p) == 0)
    y_prompt = rmsnorm_batch_major(xp, norm_final, bp, N_META, tp0, tc_p, nk_out)

    xs = x_sample.transpose(1, 0, 2).reshape(ts * bs, d)
    xs, s_new = _trunk(xs, ts, bs, ts, nb_s, state_rglru_h, state_rglru_conv, state_hgrn2, state_s5_re, state_s5_im, w)
    y_sample = rmsnorm_rows(xs, norm_final, ts * bs).reshape(ts, bs, d).transpose(1, 0, 2)

    refs = (state_rglru_h, state_rglru_conv, state_hgrn2, state_s5_re, state_s5_im)
    cast = lambda new: tuple(a.astype(r.dtype) for a, r in zip(new, refs))
    return (y_prompt, y_sample) + cast(p_new) + cast(s_new)
```

```python
import functools

import jax
import jax.numpy as jnp
from jax import lax
from jax.experimental import pallas as pl
from jax.experimental.pallas import tpu as pltpu

F32 = jnp.float32
BF16 = jnp.bfloat16

EPS = 1e-6
N_META = 16
CONV_W = 4
RG_C = 8.0
H_A = 8
S5_P = 64

V7X_VMEM_LIMIT_BYTES = 56 * 1024 * 1024
SUBLANES = 8
LANES = 128
V7X_MXU_DIM = 256
HGRN_ROWS = 128
S5_ROWS = 128
HGRN_EXP_CLAMP = 80.0
PROMPT_BLOCK_ROWS = 896
SCAN_CARRY_ELEMS = 16 * SUBLANES * LANES


def _params(*sem):
    return pltpu.CompilerParams(dimension_semantics=sem, vmem_limit_bytes=V7X_VMEM_LIMIT_BYTES)


def _rms(x, g):
    ms = jnp.mean(x * x, axis=-1, keepdims=True)
    return x * lax.rsqrt(ms + EPS) * g


def _dot(a, b):
    return jnp.dot(a, b, preferred_element_type=F32)


def _full(shape):
    return pl.BlockSpec(shape, lambda *_: (0,) * len(shape))


def _norm_matmul_kernel(x_ref, g_ref, w_ref, o_ref, xn_ref):
    @pl.when(pl.program_id(1) == 0)
    def _():
        xn_ref[...] = _rms(x_ref[...], g_ref[...]).astype(BF16)

    o_ref[...] = _dot(xn_ref[...], w_ref[...]).astype(o_ref.dtype)


def norm_matmul(x, g, w, li, tm, tn):
    n, d = x.shape
    nout = w.shape[2]
    return pl.pallas_call(
        _norm_matmul_kernel,
        out_shape=jax.ShapeDtypeStruct((n, nout), BF16),
        grid=(n // tm, nout // tn),
        in_specs=[
            pl.BlockSpec((tm, d), lambda i, j: (i, 0)),
            _full((1, d)),
            pl.BlockSpec((None, d, tn), lambda i, j: (li, 0, j)),
        ],
        out_specs=pl.BlockSpec((tm, tn), lambda i, j: (i, j)),
        scratch_shapes=[pltpu.VMEM((tm, d), BF16)],
        compiler_params=_params("parallel", "arbitrary"),
        name="norm_matmul",
    )(x, g.reshape(1, d), w)


def _norm_swiglu_kernel(x_ref, g_ref, wg_ref, wu_ref, o_ref, xn_ref):
    @pl.when(pl.program_id(1) == 0)
    def _():
        xn_ref[...] = _rms(x_ref[...], g_ref[...]).astype(BF16)

    xn = xn_ref[...]
    gate = _dot(xn, wg_ref[...])
    up = _dot(xn, wu_ref[...])
    o_ref[...] = (jax.nn.silu(gate) * up).astype(o_ref.dtype)


def norm_swiglu(x, g, w_gu, li, tm, tn):
    n, d = x.shape
    dff = w_gu.shape[2] // 2
    nj = dff // tn
    return pl.pallas_call(
        _norm_swiglu_kernel,
        out_shape=jax.ShapeDtypeStruct((n, dff), BF16),
        grid=(n // tm, nj),
        in_specs=[
            pl.BlockSpec((tm, d), lambda i, j: (i, 0)),
            _full((1, d)),
            pl.BlockSpec((None, d, tn), lambda i, j: (li, 0, j)),
            pl.BlockSpec((None, d, tn), lambda i, j: (li, 0, j + nj)),
        ],
        out_specs=pl.BlockSpec((tm, tn), lambda i, j: (i, j)),
        scratch_shapes=[pltpu.VMEM((tm, d), BF16)],
        compiler_params=_params("parallel", "arbitrary"),
        name="norm_swiglu",
    )(x, g.reshape(1, d), w_gu, w_gu)


def _matmul_residual_kernel(*refs, n_in):
    a_refs = refs[:n_in]
    w_refs = refs[n_in:2 * n_in]
    r_ref = refs[2 * n_in]
    o_ref = refs[2 * n_in + 1]
    acc = r_ref[...]
    for a_ref, w_ref in zip(a_refs, w_refs):
        acc = acc + _dot(a_ref[...], w_ref[...])
    o_ref[...] = acc


def matmul_residual(a_list, w, li, res, tm):
    n, dout = res.shape
    n_in = len(a_list)
    in_specs = []
    for a in a_list:
        in_specs.append(pl.BlockSpec((tm, a.shape[1]), lambda i: (i, 0)))
    for k, a in enumerate(a_list):
        assert a.shape[1] * n_in == w.shape[1]
        in_specs.append(pl.BlockSpec((None, a.shape[1], dout), lambda i, k=k: (li, k, 0)))
    in_specs.append(pl.BlockSpec((tm, dout), lambda i: (i, 0)))
    return pl.pallas_call(
        functools.partial(_matmul_residual_kernel, n_in=n_in),
        out_shape=jax.ShapeDtypeStruct((n, dout), F32),
        grid=(n // tm,),
        in_specs=in_specs,
        out_specs=pl.BlockSpec((tm, dout), lambda i: (i, 0)),
        compiler_params=_params("parallel"),
        name="matmul_residual",
    )(*a_list, *([w] * n_in), res)


def _glu_residual_kernel(z_ref, wv_ref, wg_ref, r_ref, o_ref):
    z = z_ref[...]
    val = _dot(z, wv_ref[...])
    gate = _dot(z, wg_ref[...])
    o_ref[...] = r_ref[...] + val * jax.nn.sigmoid(gate)


def glu_residual(z, w_glu, li, res, tm, tn):
    n, d = z.shape
    dout = w_glu.shape[2] // 2
    nj = dout // tn
    return pl.pallas_call(
        _glu_residual_kernel,
        out_shape=jax.ShapeDtypeStruct((n, dout), F32),
        grid=(n // tm, nj),
        in_specs=[
            pl.BlockSpec((tm, d), lambda i, j: (i, 0)),
            pl.BlockSpec((None, d, tn), lambda i, j: (li, 0, j)),
            pl.BlockSpec((None, d, tn), lambda i, j: (li, 0, j + nj)),
            pl.BlockSpec((tm, tn), lambda i, j: (i, j)),
        ],
        out_specs=pl.BlockSpec((tm, tn), lambda i, j: (i, j)),
        compiler_params=_params("parallel", "arbitrary"),
        name="glu_residual",
    )(z, w_glu, w_glu, res)


def _rmsnorm_kernel(x_ref, g_ref, o_ref):
    o_ref[...] = _rms(x_ref[...], g_ref[...])


def rmsnorm_rows(x, g, tm):
    n, d = x.shape
    return pl.pallas_call(
        _rmsnorm_kernel,
        out_shape=jax.ShapeDtypeStruct((n, d), F32),
        grid=(n // tm,),
        in_specs=[pl.BlockSpec((tm, d), lambda i: (i, 0)), _full((1, d))],
        out_specs=pl.BlockSpec((tm, d), lambda i: (i, 0)),
        compiler_params=_params("parallel"),
        name="final_rmsnorm",
    )(x, g.reshape(1, d))


def _rmsnorm_batch_major_kernel(*refs, nk, tc, nb):
    x_refs, g_ref, o_ref = refs[:nk], refs[nk], refs[nk + 1]
    d = o_ref.shape[-1]
    for k in range(nk):
        y = _rms(x_refs[k][...], g_ref[...])
        o_ref[:, k * tc:(k + 1) * tc, :] = jnp.swapaxes(y.reshape(tc, nb, d), 0, 1)


def rmsnorm_batch_major(x, g, nb, t_skip, t_out, tc, nk):
    n, d = x.shape
    rows = tc * nb
    assert t_skip % tc == 0 and t_out % (nk * tc) == 0
    specs = [pl.BlockSpec((rows, d), lambda j, k=k: (nk * j + t_skip // tc + k, 0)) for k in range(nk)]
    return pl.pallas_call(
        functools.partial(_rmsnorm_batch_major_kernel, nk=nk, tc=tc, nb=nb),
        out_shape=jax.ShapeDtypeStruct((nb, t_out, d), F32),
        grid=(t_out // (nk * tc),),
        in_specs=specs + [_full((1, d))],
        out_specs=pl.BlockSpec((nb, nk * tc, d), lambda j: (0, j, 0)),
        compiler_params=_params("parallel"),
        name="final_rmsnorm_batch_major",
    )(*([x] * nk), g.reshape(1, d))


def _to_time_major_kernel(*refs, nk, n_chunks):
    x_refs, lead_ref, o_ref = refs[:nk], refs[nk], refs[nk + 1]
    i = pl.program_id(0)
    nb, tc, d = x_refs[0].shape
    rows = tc * nb
    for k in range(nk):
        chunk = i * nk + k - 1
        val = jnp.swapaxes(x_refs[k][...], 0, 1).reshape(rows, d)
        if k == 0:
            lead = jnp.broadcast_to(lead_ref[...][:, None, :], (tc, nb, d)).reshape(rows, d)
            val = jnp.where(i == 0, lead, val)
        o_ref[k * rows:(k + 1) * rows, :] = jnp.where(chunk < n_chunks, val, 0.0)


def to_time_major(x, lead, t_pad, nk):
    nb, t, d = x.shape
    tc = lead.shape[0]
    assert t % tc == 0 and t_pad % (nk * tc) == 0
    n_chunks = t // tc
    specs = [pl.BlockSpec((nb, tc, d), lambda i, k=k: (0, jnp.clip(i * nk + k - 1, 0, n_chunks - 1), 0))
             for k in range(nk)]
    return pl.pallas_call(
        functools.partial(_to_time_major_kernel, nk=nk, n_chunks=n_chunks),
        out_shape=jax.ShapeDtypeStruct((t_pad * nb, d), F32),
        grid=(t_pad // (nk * tc),),
        in_specs=specs + [_full((tc, d))],
        out_specs=pl.BlockSpec((nk * tc * nb, d), lambda i: (i, 0)),
        compiler_params=_params("parallel"),
        name="to_time_major",
    )(*([x] * nk), lead)


def _rglru_kernel(xa_ref, ga_ref, conv0_ref, h0_ref, cw_ref, cb_ref, wg_ref, bg_ref, lam_ref,
                  ya_ref, hlast_ref, convnew_ref, xpad_ref, a_ref, u_ref, *, n_t, tb, rb):
    i = pl.program_id(0)
    rows = tb * rb
    tail = (CONV_W - 1) * rb
    c = xa_ref.shape[-1]
    t_valid = jnp.minimum(n_t - i * tb, tb)

    @pl.when(i == 0)
    def _():
        xpad_ref[0:tail, :] = conv0_ref[...]
        hlast_ref[...] = h0_ref[...]

    @pl.when(i > 0)
    def _():
        xpad_ref[0:tail, :] = xpad_ref[rows:rows + tail, :]

    xpad_ref[tail:tail + rows, :] = xa_ref[...].astype(F32)
    xc = cb_ref[...]
    for k in range(CONV_W):
        xc = xc + cw_ref[k:k + 1, :] * xpad_ref[k * rb:k * rb + rows, :]

    gates = _dot(xc.astype(BF16), wg_ref[...]) + bg_ref[...]
    r = jax.nn.sigmoid(gates[:, :c])
    ig = jax.nn.sigmoid(gates[:, c:])
    log_a = (-RG_C) * r * jax.nn.softplus(-lam_ref[...])
    a = jnp.exp(log_a)
    mult = jnp.sqrt(-jnp.tanh(log_a) * (a * a + 1.0))
    a_ref[...] = a
    u_ref[...] = mult * ig * xc

    lc = min(c, max(LANES, SCAN_CARRY_ELEMS // rb // LANES * LANES))
    for c0 in range(0, c, lc):
        def body(t, h, c0=c0):
            sl = pl.ds(pl.multiple_of(t * rb, rb), rb)
            h = a_ref[sl, c0:c0 + lc] * h + u_ref[sl, c0:c0 + lc]
            u_ref[sl, c0:c0 + lc] = h
            return h

        hlast_ref[:, c0:c0 + lc] = lax.fori_loop(0, t_valid, body, hlast_ref[:, c0:c0 + lc])

    live = lax.broadcasted_iota(jnp.int32, (rows, c), 0) < t_valid * rb
    ya_ref[...] = jnp.where(live, u_ref[...] * jax.nn.gelu(ga_ref[...].astype(F32)), 0.0).astype(ya_ref.dtype)
    convnew_ref[...] = xpad_ref[pl.ds(pl.multiple_of(t_valid * rb, rb), tail), :]


def rglru(proj, n_t, conv0, h0, cw, cb, wg, bg, lam, tb, rb):
    c = h0.shape[1]
    n_rows = proj.shape[0]
    rows = tb * rb
    tail = (CONV_W - 1) * rb
    return pl.pallas_call(
        functools.partial(_rglru_kernel, n_t=n_t, tb=tb, rb=rb),
        out_shape=(
            jax.ShapeDtypeStruct((n_rows, c), BF16),
            jax.ShapeDtypeStruct((rb, c), F32),
            jax.ShapeDtypeStruct((tail, c), F32),
        ),
        grid=(n_rows // rows,),
        in_specs=[
            pl.BlockSpec((rows, c), lambda i: (i, 0)),
            pl.BlockSpec((rows, c), lambda i: (i, 1)),
            _full((tail, c)), _full((rb, c)), _full((CONV_W, c)), _full((1, c)),
            _full((c, 2 * c)), _full((1, 2 * c)), _full((1, c)),
        ],
        out_specs=(
            pl.BlockSpec((rows, c), lambda i: (i, 0)),
            _full((rb, c)),
            _full((tail, c)),
        ),
        scratch_shapes=[
            pltpu.VMEM((rows + tail, c), F32),
            pltpu.VMEM((rows, c), F32),
            pltpu.VMEM((rows, c), F32),
        ],
        compiler_params=_params("arbitrary"),
        name="rglru",
    )(proj, proj, conv0, h0, cw, cb.reshape(1, c), wg, bg.reshape(1, 2 * c), lam.reshape(1, c))


def _hgrn2_kernel(q_ref, f_ref, v_ref, gb_ref, s0_ref, lbraw_ref, gn_ref, *rest, layer, n_t, tb, nb, nh):
    ob_ref, snew_ref, st_ref = rest[-3:]
    i = pl.program_id(1)
    m = HGRN_ROWS
    tc = m // nb
    d = q_ref.shape[-1]
    dk = d // nh
    n_chunks = jnp.minimum(n_t - i * tb, tb) // tc

    @pl.when(i == 0)
    def _():
        for b in range(nb):
            for h in range(nh):
                st_ref[h, :, b * dk:(b + 1) * dk] = s0_ref[b, h].T

    @pl.when(n_chunks < tb // tc)
    def _():
        ob_ref[...] = jnp.zeros_like(ob_ref)

    p = jax.nn.softmax(lbraw_ref[...], axis=0)
    cum = p[0:1, :]
    for r in range(1, layer + 1):
        cum = cum + p[r:r + 1, :]
    lb = cum - p[0:1, :]
    log_lb = jnp.log(lb)
    log_1mlb = jnp.log1p(-lb)

    row = lax.broadcasted_iota(jnp.int32, (m, m), 0)
    col = lax.broadcasted_iota(jnp.int32, (m, m), 1)
    same_seq_causal = jnp.where(((row & (nb - 1)) == (col & (nb - 1))) & (col <= row), 1.0, 0.0)
    bid = lax.broadcasted_iota(jnp.int32, (m, dk), 0) & (nb - 1)

    def chunk(c, carry):
        def rows_of(ref):
            if len(ref.shape) == 2:
                return ref[pl.ds(pl.multiple_of(c * m, m), m), :]
            return ref[pl.ds(c * tc, tc)].reshape(m, d)

        q = jax.nn.silu(rows_of(q_ref).astype(F32))
        fr = rows_of(f_ref).astype(F32)
        v = rows_of(v_ref).astype(BF16)
        gb = rows_of(gb_ref).astype(F32)
        logf = jnp.logaddexp(log_lb, log_1mlb + jax.nn.log_sigmoid(fr))
        k = 1.0 - jnp.exp(logf)
        slabs = [logf[0:nb]]
        for t in range(1, tc):
            slabs.append(slabs[-1] + logf[t * nb:(t + 1) * nb])
        g = jnp.concatenate(slabs, axis=0)
        g_last = jnp.concatenate([slabs[-1]] * tc, axis=0)
        qt = (q * jnp.exp(g)).astype(BF16)
        kt = (k * jnp.exp(jnp.minimum(-g, HGRN_EXP_CLAMP))).astype(BF16)
        ks = (k * jnp.exp(g_last - g)).astype(BF16)
        dec = jnp.exp(slabs[-1])

        outs = []
        for h in range(nh):
            hs = slice(h * dk, (h + 1) * dk)
            qt_h, kt_h, ks_h, v_h = qt[:, hs], kt[:, hs], ks[:, hs], v[:, hs]
            att = lax.dot_general(qt_h, kt_h, (((1,), (1,)), ((), ())), preferred_element_type=F32)
            att = (att * same_seq_causal).astype(BF16)
            o = _dot(att, v_h)
            zero = jnp.zeros_like(qt_h)
            expand = lambda x: jnp.concatenate([jnp.where(bid == b, x, zero) for b in range(nb)], axis=1)
            st_h = st_ref[h]
            o = o + lax.dot_general(expand(qt_h), st_h.astype(BF16), (((1,), (1,)), ((), ())),
                                    preferred_element_type=F32)
            dst = lax.dot_general(v_h, expand(ks_h), (((0,), (0,)), ((), ())), preferred_element_type=F32)
            dec_row = jnp.concatenate([dec[b:b + 1, hs] for b in range(nb)], axis=1)
            st_ref[h] = st_h * dec_row + dst
            outs.append(_rms(o, gn_ref[:, hs]))
        ob = jnp.concatenate(outs, axis=1) * jax.nn.silu(gb)
        ob_ref[pl.ds(pl.multiple_of(c * m, m), m), :] = ob.astype(ob_ref.dtype)
        return carry

    lax.fori_loop(0, n_chunks, chunk, 0)

    @pl.when(i == pl.num_programs(1) - 1)
    def _():
        for b in range(nb):
            for h in range(nh):
                snew_ref[b, h] = st_ref[h, :, b * dk:(b + 1) * dk].T


def hgrn2(proj, s0, s0_layer, s_stack, n_layers, lb_raw, gnorm, layer, n_t, tb, nb):
    _, bsz, nh, dk, _ = s0.shape
    d = nh * dk
    off = proj.shape[-1] // d - 4
    nbb = bsz // nb
    n_tpad = proj.shape[0] // bsz
    nt = n_tpad // tb
    if nb == bsz:
        col = lambda k: pl.BlockSpec((tb * nb, d), lambda j, i, k=k: (i, k + off))
    else:
        proj = proj.reshape(n_tpad, bsz, proj.shape[1])
        col = lambda k: pl.BlockSpec((tb, nb, d), lambda j, i, k=k: (i, j, k + off))
    in_specs = [
        col(0), col(1), col(2), col(3),
        pl.BlockSpec((None, nb, nh, dk, dk), lambda j, i: (s0_layer, j, 0, 0, 0)),
        _full(lb_raw.shape),
        _full((1, d)),
    ]
    args = [proj, proj, proj, proj, s0, lb_raw, gnorm.reshape(1, d)]
    aliases = {}
    if s_stack is not None:
        in_specs.append(pl.BlockSpec(memory_space=pl.ANY))
        args.append(s_stack)
        aliases = {len(args) - 1: 1}
    return pl.pallas_call(
        functools.partial(_hgrn2_kernel, layer=layer, n_t=n_t, tb=tb, nb=nb, nh=nh),
        out_shape=(
            jax.ShapeDtypeStruct((n_tpad * bsz, d), BF16),
            jax.ShapeDtypeStruct((n_layers, bsz, nh, dk, dk), F32),
        ),
        grid=(nbb, nt),
        in_specs=in_specs,
        out_specs=(
            pl.BlockSpec((tb * nb, d), lambda j, i: (j * nt + i, 0)),
            pl.BlockSpec((None, nb, nh, dk, dk), lambda j, i: (layer, j, 0, 0, 0)),
        ),
        scratch_shapes=[pltpu.VMEM((nh, dk, nb * dk), F32)],
        input_output_aliases=aliases,
        compiler_params=_params("arbitrary", "arbitrary"),
        name="hgrn2",
    )(*args)


def _s5_prep_kernel(lr_ref, li_ref, ldt_ref, bre_ref, bim_ref, ar_ref, ai_ref, ore_ref, oim_ref):
    lr = lr_ref[...]
    li = li_ref[...]
    dt = jnp.exp(ldt_ref[...])
    mag = jnp.exp(lr * dt)
    ar = mag * jnp.cos(li * dt)
    ai = mag * jnp.sin(li * dt)
    den = lr * lr + li * li
    cr = ((ar - 1.0) * lr + ai * li) / den
    ci = (ai * lr - (ar - 1.0) * li) / den
    ar_ref[...] = ar
    ai_ref[...] = ai
    ore_ref[...] = cr * bre_ref[...] - ci * bim_ref[...]
    oim_ref[...] = cr * bim_ref[...] + ci * bre_ref[...]


def s5_prep(lam_re, lam_im, log_dt, b_re, b_im):
    g, p, c = b_re.shape
    ns = g * p
    colv = jax.ShapeDtypeStruct((ns, 1), F32)
    mat = jax.ShapeDtypeStruct((ns, c), F32)
    ldt = jnp.broadcast_to(log_dt[:, None], (g, p)).reshape(ns, 1)
    return pl.pallas_call(
        _s5_prep_kernel, out_shape=(colv, colv, mat, mat), name="s5_prep",
    )(lam_re.reshape(ns, 1), lam_im.reshape(ns, 1), ldt, b_re.reshape(ns, c), b_im.reshape(ns, c))


def _s5_kernel(x_ref, g_ref, wb_ref, wc_ref, ar_ref, ai_ref, d_ref, h0r_ref, h0i_ref,
               z_ref, hr_ref, hi_ref, *, n_t, tb, rb):
    i = pl.program_id(0)
    nch = wb_ref.shape[0]
    cw = wb_ref.shape[1]
    sw = wb_ref.shape[2] // 2
    m = S5_ROWS
    tsb = m // rb
    nsl = rb // SUBLANES
    n_sub = jnp.minimum(n_t - i * tb, tb) // tsb

    @pl.when(i == 0)
    def _():
        hr_ref[...] = h0r_ref[...]
        hi_ref[...] = h0i_ref[...]

    @pl.when(n_sub < tb // tsb)
    def _():
        z_ref[...] = jnp.zeros_like(z_ref)

    def sub_block(j, carry):
        r0 = pl.multiple_of(j * m, m)
        xn = _rms(x_ref[pl.ds(r0, m), :], g_ref[...])
        for c in range(nch):
            cs = slice(c * cw, (c + 1) * cw)
            ss = slice(c * sw, (c + 1) * sw)
            u = xn[:, cs]
            bu = _dot(u.astype(BF16), wb_ref[c])
            ar = jnp.broadcast_to(ar_ref[:, ss], (SUBLANES, sw))
            ai = jnp.broadcast_to(ai_ref[:, ss], (SUBLANES, sw))
            out_r = [None] * (tsb * nsl)
            out_i = [None] * (tsb * nsl)
            for s in range(nsl):
                srow = slice(s * SUBLANES, (s + 1) * SUBLANES)
                hr, hi = hr_ref[srow, ss], hi_ref[srow, ss]
                for t in range(tsb):
                    lo = t * rb + s * SUBLANES
                    hr, hi = (ar * hr - ai * hi + bu[lo:lo + SUBLANES, :sw],
                              ar * hi + ai * hr + bu[lo:lo + SUBLANES, sw:])
                    out_r[t * nsl + s] = hr
                    out_i[t * nsl + s] = hi
                hr_ref[srow, ss] = hr
                hi_ref[srow, ss] = hi
            hcat = jnp.concatenate([jnp.concatenate(out_r, axis=0).astype(BF16),
                                    jnp.concatenate(out_i, axis=0).astype(BF16)], axis=1)
            y = _dot(hcat, wc_ref[c]) + d_ref[:, cs] * u
            z_ref[pl.ds(r0, m), cs] = jax.nn.gelu(y).astype(z_ref.dtype)
        return carry

    lax.fori_loop(0, n_sub, sub_block, 0)


def s5(x, n_t, g, wb, wc, a_re, a_im, dskip, h0r, h0i, tb, rb):
    n_rows, d = x.shape
    ns = h0r.shape[1]
    rows = tb * rb
    assert S5_ROWS % rb == 0 and rows % S5_ROWS == 0 and n_t % (S5_ROWS // rb) == 0
    return pl.pallas_call(
        functools.partial(_s5_kernel, n_t=n_t, tb=tb, rb=rb),
        out_shape=(
            jax.ShapeDtypeStruct((n_rows, d), BF16),
            jax.ShapeDtypeStruct((rb, ns), F32),
            jax.ShapeDtypeStruct((rb, ns), F32),
        ),
        grid=(n_rows // rows,),
        in_specs=[
            pl.BlockSpec((rows, d), lambda i: (i, 0)),
            _full((1, d)), _full(wb.shape), _full(wc.shape),
            _full((1, ns)), _full((1, ns)), _full((1, d)),
            _full((rb, ns)), _full((rb, ns)),
        ],
        out_specs=(
            pl.BlockSpec((rows, d), lambda i: (i, 0)),
            _full((rb, ns)), _full((rb, ns)),
        ),
        compiler_params=_params("arbitrary"),
        name="s5",
    )(x, g.reshape(1, d), wb, wc, a_re, a_im, dskip.reshape(1, d), h0r, h0i)


def _s5_block_diag_in(b):
    g, p, c = b.shape
    gpc = V7X_MXU_DIM // c
    bt = b.transpose(0, 2, 1).reshape(g // gpc, gpc, c, p)
    out = jnp.einsum('ngcp,gh->ngchp', bt, jnp.eye(gpc, dtype=b.dtype))
    return out.reshape(g // gpc, gpc * c, gpc * p)


def _s5_block_diag_out(cm):
    g, c, p = cm.shape
    gpc = V7X_MXU_DIM // c
    ct = cm.transpose(0, 2, 1).reshape(g // gpc, gpc, p, c)
    out = jnp.einsum('ngpc,gh->ngphc', ct, jnp.eye(gpc, dtype=cm.dtype))
    return out.reshape(g // gpc, gpc * p, gpc * c)


def _moe_kernel(x_ref, g_ref, wrt_ref, wgu_ref, wdn_ref, o_ref, xn_ref, pos_ref, gate_ref, *, sub):
    e = pl.program_id(1)
    tm = x_ref.shape[0]
    ne = wrt_ref.shape[0]
    dff = wdn_ref.shape[1]

    @pl.when(e == 0)
    def _():
        x = x_ref[...]
        xn = _rms(x, g_ref[...])
        xn_ref[...] = xn.astype(BF16)
        logits = lax.dot_general(wrt_ref[...], xn, (((1,), (1,)), ((), ())),
                                 precision=lax.Precision.HIGHEST, preferred_element_type=F32)
        ex = jnp.exp(logits - jnp.max(logits, axis=0, keepdims=True))
        probs = ex / jnp.sum(ex, axis=0, keepdims=True)
        eid = lax.broadcasted_iota(jnp.int32, (ne, tm), 0).astype(F32)
        m1 = jnp.max(probs, axis=0, keepdims=True)
        i1 = jnp.min(jnp.where(probs == m1, eid, float(ne)), axis=0, keepdims=True)
        sel1 = eid == i1
        rest = jnp.where(sel1, -1.0, probs)
        m2 = jnp.max(rest, axis=0, keepdims=True)
        i2 = jnp.min(jnp.where(rest == m2, eid, float(ne)), axis=0, keepdims=True)
        sel2 = eid == i2
        den = m1 + m2
        gate_ref[...] = jnp.where(sel1, m1 / den, 0.0) + jnp.where(sel2, m2 / den, 0.0)
        chosen = jnp.where(sel1, 1.0, jnp.where(sel2, 1.0, 0.0))
        r = lax.broadcasted_iota(jnp.int32, (tm, tm), 0)
        c = lax.broadcasted_iota(jnp.int32, (tm, tm), 1)
        before = jnp.where(r < c, 1.0, 0.0).astype(BF16)
        rank = _dot(chosen.astype(BF16), before)
        pos_ref[...] = jnp.where(chosen > 0.0, rank, -1.0)
        o_ref[...] = x

    pos_e = pos_ref[pl.ds(e, 1), :]
    gate_e = gate_ref[pl.ds(e, 1), :]
    cnt = jnp.sum(jnp.where(pos_e >= 0.0, 1.0, 0.0)).astype(jnp.int32)

    def body(s, carry):
        slot = lax.broadcasted_iota(jnp.int32, (sub, tm), 0) + s * sub
        hit = pos_e == slot.astype(F32)
        onehot = jnp.where(hit, 1.0, 0.0).astype(BF16)
        xs = _dot(onehot, xn_ref[...]).astype(BF16)
        gs = jnp.sum(jnp.where(hit, gate_e, 0.0), axis=1, keepdims=True)
        hgu = _dot(xs, wgu_ref[0])
        act = (jax.nn.silu(hgu[:, :dff]) * hgu[:, dff:]).astype(BF16)
        yb = (_dot(act, wdn_ref[0]) * gs).astype(BF16)
        o_ref[...] += lax.dot_general(onehot, yb, (((0,), (0,)), ((), ())), preferred_element_type=F32)
        return carry

    lax.fori_loop(0, (cnt + sub - 1) // sub, body, 0)


def moe_residual(x, g, w_router_t, w_gu, w_down, li, tm, sub):
    n, d = x.shape
    _, ne, _, dff2 = w_gu.shape
    return pl.pallas_call(
        functools.partial(_moe_kernel, sub=sub),
        out_shape=jax.ShapeDtypeStruct((n, d), F32),
        grid=(n // tm, ne),
        in_specs=[
            pl.BlockSpec((tm, d), lambda i, e: (i, 0)),
            _full((1, d)),
            _full((ne, d)),
            pl.BlockSpec((None, 1, d, dff2), lambda i, e: (li, e, 0, 0)),
            pl.BlockSpec((None, 1, dff2 // 2, d), lambda i, e: (li, e, 0, 0)),
        ],
        out_specs=pl.BlockSpec((tm, d), lambda i, e: (i, 0)),
        scratch_shapes=[
            pltpu.VMEM((tm, d), BF16),
            pltpu.VMEM((ne, tm), F32),
            pltpu.VMEM((ne, tm), F32),
        ],
        compiler_params=_params("parallel", "arbitrary"),
        name="moe",
    )(x, g.reshape(1, d), w_router_t, w_gu, w_down)


def _trunk(x, n_t, rb, tb, nb, h0, conv0, s0, re0, im0, w):
    n, d = x.shape
    depth = w['norm_mix'].shape[0]
    d_a = h0.shape[-1]
    nh, dk = s0.shape[2], s0.shape[3]
    d_b = nh * dk
    ns = re0.shape[-2] * re0.shape[-1]
    tm = tb * rb
    n_even = (depth + 1) // 2
    new = {k: [] for k in ('h', 'conv', 're', 'im')}
    s_stack = None
    for l in range(depth):
        li = l // 2
        if l % 2 == 0:
            proj = norm_matmul(x, w['norm_mix'][l], w['even_w_in'], li, tm, 1024)
            conv_tm = conv0[li].transpose(1, 0, 2).reshape((CONV_W - 1) * rb, d_a)
            ya, h_new, conv_new = rglru(proj, n_t, conv_tm, h0[li], w['rglru_conv_w'][li], w['rglru_conv_b'][li],
                                        w['rglru_wg'][li], w['rglru_bg'][li], w['rglru_lambda'][li], tb=tb, rb=rb)
            new['h'].append(h_new)
            new['conv'].append(conv_new.reshape(CONV_W - 1, rb, d_a).transpose(1, 0, 2))
            ob, s_stack = hgrn2(proj, s0, li % s0.shape[0], s_stack, n_even,
                                w['hgrn2_lb_raw'], w['hgrn2_gnorm'][li], li, n_t=n_t, tb=tb, nb=nb)
            if nb != rb:
                ob = ob.reshape(rb // nb, n // rb, nb, d_b).transpose(1, 0, 2, 3).reshape(n, d_b)
            x = matmul_residual([ya, ob], w['even_w_out'], li, x, tm)
            hff = norm_swiglu(x, w['norm_ffn'][l], w['ffn_w_gu'], li, tm, w['ffn_w_gu'].shape[2] // 4)
            x = matmul_residual([hff], w['ffn_w_down'], li, x, tm)
        else:
            z, re_new, im_new = s5(x, n_t, w['norm_mix'][l], w['s5_wb'][li], w['s5_wc'][li], w['s5_a_re'][li],
                                   w['s5_a_im'][li], w['s5_d'][li], re0[li].reshape(rb, ns), im0[li].reshape(rb, ns),
                                   tb=tb, rb=rb)
            new['re'].append(re_new.reshape(re0.shape[1:]))
            new['im'].append(im_new.reshape(im0.shape[1:]))
            x = glu_residual(z, w['s5_w_glu'], li, x, tm, 512)
            x = moe_residual(x, w['norm_ffn'][l], w['moe_w_router_t'][li], w['moe_w_gu'], w['moe_w_down'], li,
                             tm, V7X_MXU_DIM)
    stack = lambda k: jnp.stack(new[k])
    return x, (stack('h'), stack('conv'), s_stack, stack('re'), stack('im'))


def kernel(x_prompt, x_sample, state_rglru_h, state_rglru_conv, state_hgrn2, state_s5_re, state_s5_im,
           meta_tokens, norm_mix, norm_ffn, norm_final, even_w_in, even_w_out,
           rglru_conv_w, rglru_conv_b, rglru_w_a, rglru_b_a, rglru_w_x, rglru_b_x, rglru_lambda,
           hgrn2_lb_raw, hgrn2_gnorm, s5_lam_re, s5_lam_im, s5_log_dt, s5_b_re, s5_b_im, s5_c_re, s5_c_im,
           s5_d, s5_w_glu, ffn_w_gu, ffn_w_down, moe_w_router, moe_w_gu, moe_w_down):
    bp, tp0, d = x_prompt.shape
    bs, ts, _ = x_sample.shape
    tp = tp0 + N_META
    d_a = state_rglru_h.shape[-1]
    n_even, n_odd = state_rglru_h.shape[0], state_s5_re.shape[0]
    assert bp == SUBLANES and bs % SUBLANES == 0
    tb_p = PROMPT_BLOCK_ROWS // bp
    tp_pad = -(-tp // tb_p) * tb_p
    nb_s = HGRN_ROWS // ts
    assert (tp % (HGRN_ROWS // bp) == 0 and tb_p % (HGRN_ROWS // bp) == 0 and bs % nb_s == 0
            and nb_s * ts == HGRN_ROWS and nb_s % SUBLANES == 0)

    eye_a = jnp.eye(H_A, dtype=F32)
    block_diag = lambda m: jnp.einsum('lhij,hg->lhigj', m, eye_a).reshape(n_even, d_a, d_a)
    prep = [s5_prep(s5_lam_re[i], s5_lam_im[i], s5_log_dt[i], s5_b_re[i], s5_b_im[i]) for i in range(n_odd)]
    ns = s5_lam_re.shape[1] * s5_lam_re.shape[2]
    bshape = s5_b_re.shape[1:]
    w = {
        'norm_mix': norm_mix, 'norm_ffn': norm_ffn, 'norm_final': norm_final,
        'even_w_in': even_w_in.astype(BF16), 'even_w_out': even_w_out.astype(BF16),
        'rglru_conv_w': rglru_conv_w, 'rglru_conv_b': rglru_conv_b,
        'rglru_wg': jnp.concatenate([block_diag(rglru_w_a), block_diag(rglru_w_x)], axis=2).astype(BF16),
        'rglru_bg': jnp.concatenate([rglru_b_a, rglru_b_x], axis=1),
        'rglru_lambda': rglru_lambda, 'hgrn2_lb_raw': hgrn2_lb_raw, 'hgrn2_gnorm': hgrn2_gnorm,
        's5_a_re': [p[0].reshape(1, ns) for p in prep], 's5_a_im': [p[1].reshape(1, ns) for p in prep],
        's5_wb': [jnp.concatenate([_s5_block_diag_in(p[2].reshape(bshape)), _s5_block_diag_in(p[3].reshape(bshape))],
                                  axis=2).astype(BF16) for p in prep],
        's5_wc': [jnp.concatenate([_s5_block_diag_out(s5_c_re[i]), -_s5_block_diag_out(s5_c_im[i])],
                                  axis=1).astype(BF16) for i in range(n_odd)],
        's5_d': s5_d, 's5_w_glu': s5_w_glu.astype(BF16),
        'ffn_w_gu': ffn_w_gu.astype(BF16), 'ffn_w_down': ffn_w_down.astype(BF16),
        'moe_w_router_t': moe_w_router.transpose(0, 2, 1),
        'moe_w_gu': moe_w_gu.astype(BF16), 'moe_w_down': moe_w_down.astype(BF16),
    }

    tc_p = HGRN_ROWS // bp
    assert N_META == tc_p
    xm = to_time_major(x_prompt, meta_tokens.astype(x_prompt.dtype), tp_pad, tb_p // tc_p)
    zero = lambda ref, lead: jnp.zeros((lead, bp) + ref.shape[2:], ref.dtype)
    xp, p_new = _trunk(xm, tp, bp, tb_p, bp, zero(state_rglru_h, n_even), zero(state_rglru_conv, n_even),
                       zero(state_hgrn2, 1), zero(state_s5_re, n_odd), zero(state_s5_im, n_odd), w)
    nk_out = max(k for k in range(1, HGRN_ROWS // tc_p + 1) if tp0 % (k * tc_p) == 0)
    y_prompt = rmsnorm_batch_major(xp, norm_final, bp, N_META, tp0, tc_p, nk_out)

    xs = x_sample.transpose(1, 0, 2).reshape(ts * bs, d)
    xs, s_new = _trunk(xs, ts, bs, ts, nb_s, state_rglru_h, state_rglru_conv, state_hgrn2, state_s5_re, state_s5_im, w)
    y_sample = rmsnorm_rows(xs, norm_final, ts * bs).reshape(ts, bs, d).transpose(1, 0, 2)

    refs = (state_rglru_h, state_rglru_conv, state_hgrn2, state_s5_re, state_s5_im)
    cast = lambda new: tuple(a.astype(r.dtype) for a, r in zip(new, refs))
    return (y_prompt, y_sample) + cast(p_new) + cast(s_new)
```

```python
import functools

import jax
import jax.numpy as jnp
from jax import lax
from jax.experimental import pallas as pl
from jax.experimental.pallas import tpu as pltpu

F32 = jnp.float32
BF16 = jnp.bfloat16

EPS = 1e-6
N_META = 16
CONV_W = 4
RG_C = 8.0
H_A = 8
S5_P = 64

V7X_VMEM_LIMIT_BYTES = 56 * 1024 * 1024
SUBLANES = 8
LANES = 128
V7X_MXU_DIM = 256
HGRN_ROWS = 128
S5_ROWS = 128
HGRN_EXP_CLAMP = 80.0
PROMPT_BLOCK_ROWS = 896
SCAN_CARRY_ELEMS = 16 * SUBLANES * LANES


def _params(*sem):
    return pltpu.CompilerParams(dimension_semantics=sem, vmem_limit_bytes=V7X_VMEM_LIMIT_BYTES)


def _rms(x, g):
    ms = jnp.mean(x * x, axis=-1, keepdims=True)
    return x * lax.rsqrt(ms + EPS) * g


def _dot(a, b):
    return jnp.dot(a, b, preferred_element_type=F32)


def _full(shape):
    return pl.BlockSpec(shape, lambda *_: (0,) * len(shape))


def _norm_matmul_kernel(x_ref, g_ref, w_ref, o_ref, xn_ref):
    @pl.when(pl.program_id(1) == 0)
    def _():
        xn_ref[...] = _rms(x_ref[...], g_ref[...]).astype(BF16)

    o_ref[...] = _dot(xn_ref[...], w_ref[...]).astype(o_ref.dtype)


def norm_matmul(x, g, w, li, tm, tn):
    n, d = x.shape
    nout = w.shape[2]
    return pl.pallas_call(
        _norm_matmul_kernel,
        out_shape=jax.ShapeDtypeStruct((n, nout), BF16),
        grid=(n // tm, nout // tn),
        in_specs=[
            pl.BlockSpec((tm, d), lambda i, j: (i, 0)),
            _full((1, d)),
            pl.BlockSpec((None, d, tn), lambda i, j: (li, 0, j)),
        ],
        out_specs=pl.BlockSpec((tm, tn), lambda i, j: (i, j)),
        scratch_shapes=[pltpu.VMEM((tm, d), BF16)],
        compiler_params=_params("parallel", "arbitrary"),
        name="norm_matmul",
    )(x, g.reshape(1, d), w)


def _norm_swiglu_kernel(x_ref, g_ref, wg_ref, wu_ref, o_ref, xn_ref):
    @pl.when(pl.program_id(1) == 0)
    def _():
        xn_ref[...] = _rms(x_ref[...], g_ref[...]).astype(BF16)

    xn = xn_ref[...]
    gate = _dot(xn, wg_ref[...])
    up = _dot(xn, wu_ref[...])
    o_ref[...] = (jax.nn.silu(gate) * up).astype(o_ref.dtype)


def norm_swiglu(x, g, w_gu, li, tm, tn):
    n, d = x.shape
    dff = w_gu.shape[2] // 2
    nj = dff // tn
    return pl.pallas_call(
        _norm_swiglu_kernel,
        out_shape=jax.ShapeDtypeStruct((n, dff), BF16),
        grid=(n // tm, nj),
        in_specs=[
            pl.BlockSpec((tm, d), lambda i, j: (i, 0)),
            _full((1, d)),
            pl.BlockSpec((None, d, tn), lambda i, j: (li, 0, j)),
            pl.BlockSpec((None, d, tn), lambda i, j: (li, 0, j + nj)),
        ],
        out_specs=pl.BlockSpec((tm, tn), lambda i, j: (i, j)),
        scratch_shapes=[pltpu.VMEM((tm, d), BF16)],
        compiler_params=_params("parallel", "arbitrary"),
        name="norm_swiglu",
    )(x, g.reshape(1, d), w_gu, w_gu)


def _matmul_residual_kernel(*refs, n_in):
    a_refs = refs[:n_in]
    w_refs = refs[n_in:2 * n_in]
    r_ref = refs[2 * n_in]
    o_ref = refs[2 * n_in + 1]
    acc = r_ref[...]
    for a_ref, w_ref in zip(a_refs, w_refs):
        acc = acc + _dot(a_ref[...], w_ref[...])
    o_ref[...] = acc


def matmul_residual(a_list, w, li, res, tm):
    n, dout = res.shape
    n_in = len(a_list)
    in_specs = []
    for a in a_list:
        in_specs.append(pl.BlockSpec((tm, a.shape[1]), lambda i: (i, 0)))
    for k, a in enumerate(a_list):
        assert a.shape[1] * n_in == w.shape[1]
        in_specs.append(pl.BlockSpec((None, a.shape[1], dout), lambda i, k=k: (li, k, 0)))
    in_specs.append(pl.BlockSpec((tm, dout), lambda i: (i, 0)))
    return pl.pallas_call(
        functools.partial(_matmul_residual_kernel, n_in=n_in),
        out_shape=jax.ShapeDtypeStruct((n, dout), F32),
        grid=(n // tm,),
        in_specs=in_specs,
        out_specs=pl.BlockSpec((tm, dout), lambda i: (i, 0)),
        compiler_params=_params("parallel"),
        name="matmul_residual",
    )(*a_list, *([w] * n_in), res)


def _glu_residual_kernel(z_ref, wv_ref, wg_ref, r_ref, o_ref):
    z = z_ref[...]
    val = _dot(z, wv_ref[...])
    gate = _dot(z, wg_ref[...])
    o_ref[...] = r_ref[...] + val * jax.nn.sigmoid(gate)


def glu_residual(z, w_glu, li, res, tm, tn):
    n, d = z.shape
    dout = w_glu.shape[2] // 2
    nj = dout // tn
    return pl.pallas_call(
        _glu_residual_kernel,
        out_shape=jax.ShapeDtypeStruct((n, dout), F32),
        grid=(n // tm, nj),
        in_specs=[
            pl.BlockSpec((tm, d), lambda i, j: (i, 0)),
            pl.BlockSpec((None, d, tn), lambda i, j: (li, 0, j)),
            pl.BlockSpec((None, d, tn), lambda i, j: (li, 0, j + nj)),
            pl.BlockSpec((tm, tn), lambda i, j: (i, j)),
        ],
        out_specs=pl.BlockSpec((tm, tn), lambda i, j: (i, j)),
        compiler_params=_params("parallel", "arbitrary"),
        name="glu_residual",
    )(z, w_glu, w_glu, res)


def _rmsnorm_kernel(x_ref, g_ref, o_ref):
    o_ref[...] = _rms(x_ref[...], g_ref[...])


def rmsnorm_rows(x, g, tm):
    n, d = x.shape
    return pl.pallas_call(
        _rmsnorm_kernel,
        out_shape=jax.ShapeDtypeStruct((n, d), F32),
        grid=(n // tm,),
        in_specs=[pl.BlockSpec((tm, d), lambda i: (i, 0)), _full((1, d))],
        out_specs=pl.BlockSpec((tm, d), lambda i: (i, 0)),
        compiler_params=_params("parallel"),
        name="final_rmsnorm",
    )(x, g.reshape(1, d))


def _rmsnorm_batch_major_kernel(*refs, nk, tc, nb):
    x_refs, g_ref, o_ref = refs[:nk], refs[nk], refs[nk + 1]
    d = o_ref.shape[-1]
    for k in range(nk):
        y = _rms(x_refs[k][...], g_ref[...])
        o_ref[:, k * tc:(k + 1) * tc, :] = jnp.swapaxes(y.reshape(tc, nb, d), 0, 1)


def rmsnorm_batch_major(x, g, nb, t_skip, t_out, tc, nk):
    n, d = x.shape
    rows = tc * nb
    assert t_skip % tc == 0 and t_out % (nk * tc) == 0
    specs = [pl.BlockSpec((rows, d), lambda j, k=k: (nk * j + t_skip // tc + k, 0)) for k in range(nk)]
    return pl.pallas_call(
        functools.partial(_rmsnorm_batch_major_kernel, nk=nk, tc=tc, nb=nb),
        out_shape=jax.ShapeDtypeStruct((nb, t_out, d), F32),
        grid=(t_out // (nk * tc),),
        in_specs=specs + [_full((1, d))],
        out_specs=pl.BlockSpec((nb, nk * tc, d), lambda j: (0, j, 0)),
        compiler_params=_params("parallel"),
        name="final_rmsnorm_batch_major",
    )(*([x] * nk), g.reshape(1, d))


def _to_time_major_kernel(*refs, nk, n_chunks):
    x_refs, lead_ref, o_ref = refs[:nk], refs[nk], refs[nk + 1]
    i = pl.program_id(0)
    nb, tc, d = x_refs[0].shape
    rows = tc * nb
    for k in range(nk):
        chunk = i * nk + k - 1
        val = jnp.swapaxes(x_refs[k][...], 0, 1).reshape(rows, d)
        if k == 0:
            lead = jnp.broadcast_to(lead_ref[...][:, None, :], (tc, nb, d)).reshape(rows, d)
            val = jnp.where(i == 0, lead, val)
        o_ref[k * rows:(k + 1) * rows, :] = jnp.where(chunk < n_chunks, val, 0.0)


def to_time_major(x, lead, t_pad, nk):
    nb, t, d = x.shape
    tc = lead.shape[0]
    assert t % tc == 0 and t_pad % (nk * tc) == 0
    n_chunks = t // tc
    specs = [pl.BlockSpec((nb, tc, d), lambda i, k=k: (0, jnp.clip(i * nk + k - 1, 0, n_chunks - 1), 0))
             for k in range(nk)]
    return pl.pallas_call(
        functools.partial(_to_time_major_kernel, nk=nk, n_chunks=n_chunks),
        out_shape=jax.ShapeDtypeStruct((t_pad * nb, d), F32),
        grid=(t_pad // (nk * tc),),
        in_specs=specs + [_full((tc, d))],
        out_specs=pl.BlockSpec((nk * tc * nb, d), lambda i: (i, 0)),
        compiler_params=_params("parallel"),
        name="to_time_major",
    )(*([x] * nk), lead)


def _rglru_kernel(xa_ref, ga_ref, conv0_ref, h0_ref, cw_ref, cb_ref, wg_ref, bg_ref, lam_ref,
                  ya_ref, hlast_ref, convnew_ref, xpad_ref, a_ref, u_ref, *, n_t, tb, rb):
    i = pl.program_id(0)
    rows = tb * rb
    tail = (CONV_W - 1) * rb
    c = xa_ref.shape[-1]
    t_valid = jnp.minimum(n_t - i * tb, tb)

    @pl.when(i == 0)
    def _():
        xpad_ref[0:tail, :] = conv0_ref[...]
        hlast_ref[...] = h0_ref[...]

    @pl.when(i > 0)
    def _():
        xpad_ref[0:tail, :] = xpad_ref[rows:rows + tail, :]

    xpad_ref[tail:tail + rows, :] = xa_ref[...].astype(F32)
    xc = cb_ref[...]
    for k in range(CONV_W):
        xc = xc + cw_ref[k:k + 1, :] * xpad_ref[k * rb:k * rb + rows, :]

    gates = _dot(xc.astype(BF16), wg_ref[...]) + bg_ref[...]
    r = jax.nn.sigmoid(gates[:, :c])
    ig = jax.nn.sigmoid(gates[:, c:])
    log_a = (-RG_C) * r * jax.nn.softplus(-lam_ref[...])
    a = jnp.exp(log_a)
    mult = jnp.sqrt(1.0 - a * a)
    a_ref[...] = a
    u_ref[...] = mult * ig * xc

    lc = min(c, max(LANES, SCAN_CARRY_ELEMS // rb // LANES * LANES))
    for c0 in range(0, c, lc):
        def body(t, h, c0=c0):
            sl = pl.ds(pl.multiple_of(t * rb, rb), rb)
            h = a_ref[sl, c0:c0 + lc] * h + u_ref[sl, c0:c0 + lc]
            u_ref[sl, c0:c0 + lc] = h
            return h

        hlast_ref[:, c0:c0 + lc] = lax.fori_loop(0, t_valid, body, hlast_ref[:, c0:c0 + lc])

    live = lax.broadcasted_iota(jnp.int32, (rows, c), 0) < t_valid * rb
    ya_ref[...] = jnp.where(live, u_ref[...] * jax.nn.gelu(ga_ref[...].astype(F32)), 0.0).astype(ya_ref.dtype)
    convnew_ref[...] = xpad_ref[pl.ds(pl.multiple_of(t_valid * rb, rb), tail), :]


def rglru(proj, n_t, conv0, h0, cw, cb, wg, bg, lam, tb, rb):
    c = h0.shape[1]
    n_rows = proj.shape[0]
    rows = tb * rb
    tail = (CONV_W - 1) * rb
    return pl.pallas_call(
        functools.partial(_rglru_kernel, n_t=n_t, tb=tb, rb=rb),
        out_shape=(
            jax.ShapeDtypeStruct((n_rows, c), BF16),
            jax.ShapeDtypeStruct((rb, c), F32),
            jax.ShapeDtypeStruct((tail, c), F32),
        ),
        grid=(n_rows // rows,),
        in_specs=[
            pl.BlockSpec((rows, c), lambda i: (i, 0)),
            pl.BlockSpec((rows, c), lambda i: (i, 1)),
            _full((tail, c)), _full((rb, c)), _full((CONV_W, c)), _full((1, c)),
            _full((c, 2 * c)), _full((1, 2 * c)), _full((1, c)),
        ],
        out_specs=(
            pl.BlockSpec((rows, c), lambda i: (i, 0)),
            _full((rb, c)),
            _full((tail, c)),
        ),
        scratch_shapes=[
            pltpu.VMEM((rows + tail, c), F32),
            pltpu.VMEM((rows, c), F32),
            pltpu.VMEM((rows, c), F32),
        ],
        compiler_params=_params("arbitrary"),
        name="rglru",
    )(proj, proj, conv0, h0, cw, cb.reshape(1, c), wg, bg.reshape(1, 2 * c), lam.reshape(1, c))


def _hgrn2_kernel(q_ref, f_ref, v_ref, gb_ref, s0_ref, lbraw_ref, gn_ref, *rest, layer, n_t, tb, nb, nh):
    ob_ref, snew_ref, st_ref = rest[-3:]
    i = pl.program_id(1)
    m = HGRN_ROWS
    tc = m // nb
    d = q_ref.shape[-1]
    dk = d // nh
    n_chunks = jnp.minimum(n_t - i * tb, tb) // tc

    @pl.when(i == 0)
    def _():
        for b in range(nb):
            for h in range(nh):
                st_ref[h, :, b * dk:(b + 1) * dk] = s0_ref[b, h].T

    @pl.when(n_chunks < tb // tc)
    def _():
        ob_ref[...] = jnp.zeros_like(ob_ref)

    p = jax.nn.softmax(lbraw_ref[...], axis=0)
    cum = p[0:1, :]
    for r in range(1, layer + 1):
        cum = cum + p[r:r + 1, :]
    lb = cum - p[0:1, :]
    log_lb = jnp.log(lb)
    log_1mlb = jnp.log1p(-lb)

    row = lax.broadcasted_iota(jnp.int32, (m, m), 0)
    col = lax.broadcasted_iota(jnp.int32, (m, m), 1)
    same_seq_causal = jnp.where(((row & (nb - 1)) == (col & (nb - 1))) & (col <= row), 1.0, 0.0)
    bid = lax.broadcasted_iota(jnp.int32, (m, dk), 0) & (nb - 1)

    def chunk(c, carry):
        def rows_of(ref):
            if len(ref.shape) == 2:
                return ref[pl.ds(pl.multiple_of(c * m, m), m), :]
            return ref[pl.ds(c * tc, tc)].reshape(m, d)

        q = jax.nn.silu(rows_of(q_ref).astype(F32))
        fr = rows_of(f_ref).astype(F32)
        v = rows_of(v_ref).astype(BF16)
        gb = rows_of(gb_ref).astype(F32)
        logf = jnp.logaddexp(log_lb, log_1mlb + jax.nn.log_sigmoid(fr))
        k = 1.0 - jnp.exp(logf)
        slabs = [logf[0:nb]]
        for t in range(1, tc):
            slabs.append(slabs[-1] + logf[t * nb:(t + 1) * nb])
        g = jnp.concatenate(slabs, axis=0)
        g_last = jnp.concatenate([slabs[-1]] * tc, axis=0)
        qt = (q * jnp.exp(g)).astype(BF16)
        kt = (k * jnp.exp(jnp.minimum(-g, HGRN_EXP_CLAMP))).astype(BF16)
        ks = (k * jnp.exp(g_last - g)).astype(BF16)
        dec = jnp.exp(slabs[-1])

        outs = []
        for h in range(nh):
            hs = slice(h * dk, (h + 1) * dk)
            qt_h, kt_h, ks_h, v_h = qt[:, hs], kt[:, hs], ks[:, hs], v[:, hs]
            att = lax.dot_general(qt_h, kt_h, (((1,), (1,)), ((), ())), preferred_element_type=F32)
            att = (att * same_seq_causal).astype(BF16)
            o = _dot(att, v_h)
            zero = jnp.zeros_like(qt_h)
            expand = lambda x: jnp.concatenate([jnp.where(bid == b, x, zero) for b in range(nb)], axis=1)
            st_h = st_ref[h]
            o = o + lax.dot_general(expand(qt_h), st_h.astype(BF16), (((1,), (1,)), ((), ())),
                                    preferred_element_type=F32)
            dst = lax.dot_general(v_h, expand(ks_h), (((0,), (0,)), ((), ())), preferred_element_type=F32)
            dec_row = jnp.concatenate([dec[b:b + 1, hs] for b in range(nb)], axis=1)
            st_ref[h] = st_h * dec_row + dst
            outs.append(_rms(o, gn_ref[:, hs]))
        ob = jnp.concatenate(outs, axis=1) * jax.nn.silu(gb)
        ob_ref[pl.ds(pl.multiple_of(c * m, m), m), :] = ob.astype(ob_ref.dtype)
        return carry

    lax.fori_loop(0, n_chunks, chunk, 0)

    @pl.when(i == pl.num_programs(1) - 1)
    def _():
        for b in range(nb):
            for h in range(nh):
                snew_ref[b, h] = st_ref[h, :, b * dk:(b + 1) * dk].T


def hgrn2(proj, s0, s0_layer, s_stack, n_layers, lb_raw, gnorm, layer, n_t, tb, nb):
    _, bsz, nh, dk, _ = s0.shape
    d = nh * dk
    off = proj.shape[-1] // d - 4
    nbb = bsz // nb
    n_tpad = proj.shape[0] // bsz
    nt = n_tpad // tb
    if nb == bsz:
        col = lambda k: pl.BlockSpec((tb * nb, d), lambda j, i, k=k: (i, k + off))
    else:
        proj = proj.reshape(n_tpad, bsz, proj.shape[1])
        col = lambda k: pl.BlockSpec((tb, nb, d), lambda j, i, k=k: (i, j, k + off))
    in_specs = [
        col(0), col(1), col(2), col(3),
        pl.BlockSpec((None, nb, nh, dk, dk), lambda j, i: (s0_layer, j, 0, 0, 0)),
        _full(lb_raw.shape),
        _full((1, d)),
    ]
    args = [proj, proj, proj, proj, s0, lb_raw, gnorm.reshape(1, d)]
    aliases = {}
    if s_stack is not None:
        in_specs.append(pl.BlockSpec(memory_space=pl.ANY))
        args.append(s_stack)
        aliases = {len(args) - 1: 1}
    return pl.pallas_call(
        functools.partial(_hgrn2_kernel, layer=layer, n_t=n_t, tb=tb, nb=nb, nh=nh),
        out_shape=(
            jax.ShapeDtypeStruct((n_tpad * bsz, d), BF16),
            jax.ShapeDtypeStruct((n_layers, bsz, nh, dk, dk), F32),
        ),
        grid=(nbb, nt),
        in_specs=in_specs,
        out_specs=(
            pl.BlockSpec((tb * nb, d), lambda j, i: (j * nt + i, 0)),
            pl.BlockSpec((None, nb, nh, dk, dk), lambda j, i: (layer, j, 0, 0, 0)),
        ),
        scratch_shapes=[pltpu.VMEM((nh, dk, nb * dk), F32)],
        input_output_aliases=aliases,
        compiler_params=_params("arbitrary", "arbitrary"),
        name="hgrn2",
    )(*args)


def _s5_prep_kernel(lr_ref, li_ref, ldt_ref, bre_ref, bim_ref, ar_ref, ai_ref, ore_ref, oim_ref):
    lr = lr_ref[...]
    li = li_ref[...]
    dt = jnp.exp(ldt_ref[...])
    mag = jnp.exp(lr * dt)
    ar = mag * jnp.cos(li * dt)
    ai = mag * jnp.sin(li * dt)
    den = lr * lr + li * li
    cr = ((ar - 1.0) * lr + ai * li) / den
    ci = (ai * lr - (ar - 1.0) * li) / den
    ar_ref[...] = ar
    ai_ref[...] = ai
    ore_ref[...] = cr * bre_ref[...] - ci * bim_ref[...]
    oim_ref[...] = cr * bim_ref[...] + ci * bre_ref[...]


def s5_prep(lam_re, lam_im, log_dt, b_re, b_im):
    g, p, c = b_re.shape
    ns = g * p
    colv = jax.ShapeDtypeStruct((ns, 1), F32)
    mat = jax.ShapeDtypeStruct((ns, c), F32)
    ldt = jnp.broadcast_to(log_dt[:, None], (g, p)).reshape(ns, 1)
    return pl.pallas_call(
        _s5_prep_kernel, out_shape=(colv, colv, mat, mat), name="s5_prep",
    )(lam_re.reshape(ns, 1), lam_im.reshape(ns, 1), ldt, b_re.reshape(ns, c), b_im.reshape(ns, c))


def _s5_kernel(x_ref, g_ref, wb_ref, wc_ref, ar_ref, ai_ref, d_ref, h0r_ref, h0i_ref,
               z_ref, hr_ref, hi_ref, *, n_t, tb, rb):
    i = pl.program_id(0)
    nch = wb_ref.shape[0]
    cw = wb_ref.shape[1]
    sw = wb_ref.shape[2] // 2
    m = S5_ROWS
    tsb = m // rb
    nsl = rb // SUBLANES
    n_sub = jnp.minimum(n_t - i * tb, tb) // tsb

    @pl.when(i == 0)
    def _():
        hr_ref[...] = h0r_ref[...]
        hi_ref[...] = h0i_ref[...]

    @pl.when(n_sub < tb // tsb)
    def _():
        z_ref[...] = jnp.zeros_like(z_ref)

    def sub_block(j, carry):
        r0 = pl.multiple_of(j * m, m)
        xn = _rms(x_ref[pl.ds(r0, m), :], g_ref[...])
        xnb = xn.astype(BF16)
        drive = [_dot(xnb[:, c * cw:(c + 1) * cw], wb_ref[c]) for c in range(nch)]
        for c in range(nch):
            cs = slice(c * cw, (c + 1) * cw)
            ss = slice(c * sw, (c + 1) * sw)
            u = xn[:, cs]
            bu = drive[c]
            ar = jnp.broadcast_to(ar_ref[:, ss], (SUBLANES, sw))
            ai = jnp.broadcast_to(ai_ref[:, ss], (SUBLANES, sw))
            out_r = [None] * (tsb * nsl)
            out_i = [None] * (tsb * nsl)
            for s in range(nsl):
                srow = slice(s * SUBLANES, (s + 1) * SUBLANES)
                hr, hi = hr_ref[srow, ss], hi_ref[srow, ss]
                for t in range(tsb):
                    lo = t * rb + s * SUBLANES
                    hr, hi = (ar * hr - ai * hi + bu[lo:lo + SUBLANES, :sw],
                              ar * hi + ai * hr + bu[lo:lo + SUBLANES, sw:])
                    out_r[t * nsl + s] = hr
                    out_i[t * nsl + s] = hi
                hr_ref[srow, ss] = hr
                hi_ref[srow, ss] = hi
            hcat = jnp.concatenate([jnp.concatenate(out_r, axis=0).astype(BF16),
                                    jnp.concatenate(out_i, axis=0).astype(BF16)], axis=1)
            y = _dot(hcat, wc_ref[c]) + d_ref[:, cs] * u
            z_ref[pl.ds(r0, m), cs] = jax.nn.gelu(y).astype(z_ref.dtype)
        return carry

    lax.fori_loop(0, n_sub, sub_block, 0)


def s5(x, n_t, g, wb, wc, a_re, a_im, dskip, h0r, h0i, tb, rb):
    n_rows, d = x.shape
    ns = h0r.shape[1]
    rows = tb * rb
    assert S5_ROWS % rb == 0 and rows % S5_ROWS == 0 and n_t % (S5_ROWS // rb) == 0
    return pl.pallas_call(
        functools.partial(_s5_kernel, n_t=n_t, tb=tb, rb=rb),
        out_shape=(
            jax.ShapeDtypeStruct((n_rows, d), BF16),
            jax.ShapeDtypeStruct((rb, ns), F32),
            jax.ShapeDtypeStruct((rb, ns), F32),
        ),
        grid=(n_rows // rows,),
        in_specs=[
            pl.BlockSpec((rows, d), lambda i: (i, 0)),
            _full((1, d)), _full(wb.shape), _full(wc.shape),
            _full((1, ns)), _full((1, ns)), _full((1, d)),
            _full((rb, ns)), _full((rb, ns)),
        ],
        out_specs=(
            pl.BlockSpec((rows, d), lambda i: (i, 0)),
            _full((rb, ns)), _full((rb, ns)),
        ),
        compiler_params=_params("arbitrary"),
        name="s5",
    )(x, g.reshape(1, d), wb, wc, a_re, a_im, dskip.reshape(1, d), h0r, h0i)


def _s5_block_diag_in(b):
    g, p, c = b.shape
    gpc = V7X_MXU_DIM // c
    bt = b.transpose(0, 2, 1).reshape(g // gpc, gpc, c, p)
    out = jnp.einsum('ngcp,gh->ngchp', bt, jnp.eye(gpc, dtype=b.dtype))
    return out.reshape(g // gpc, gpc * c, gpc * p)


def _s5_block_diag_out(cm):
    g, c, p = cm.shape
    gpc = V7X_MXU_DIM // c
    ct = cm.transpose(0, 2, 1).reshape(g // gpc, gpc, p, c)
    out = jnp.einsum('ngpc,gh->ngphc', ct, jnp.eye(gpc, dtype=cm.dtype))
    return out.reshape(g // gpc, gpc * p, gpc * c)


def _moe_kernel(x_ref, g_ref, wrt_ref, wgu_ref, wdn_ref, o_ref, xn_ref, pos_ref, gate_ref, *, sub):
    e = pl.program_id(1)
    tm = x_ref.shape[0]
    ne = wrt_ref.shape[0]
    dff = wdn_ref.shape[1]

    @pl.when(e == 0)
    def _():
        x = x_ref[...]
        xn = _rms(x, g_ref[...])
        xn_ref[...] = xn.astype(BF16)
        logits = lax.dot_general(wrt_ref[...], xn, (((1,), (1,)), ((), ())),
                                 precision=lax.Precision.HIGHEST, preferred_element_type=F32)
        ex = jnp.exp(logits - jnp.max(logits, axis=0, keepdims=True))
        probs = ex / jnp.sum(ex, axis=0, keepdims=True)
        eid = lax.broadcasted_iota(jnp.int32, (ne, tm), 0).astype(F32)
        m1 = jnp.max(probs, axis=0, keepdims=True)
        i1 = jnp.min(jnp.where(probs == m1, eid, float(ne)), axis=0, keepdims=True)
        sel1 = eid == i1
        rest = jnp.where(sel1, -1.0, probs)
        m2 = jnp.max(rest, axis=0, keepdims=True)
        i2 = jnp.min(jnp.where(rest == m2, eid, float(ne)), axis=0, keepdims=True)
        sel2 = eid == i2
        den = m1 + m2
        gate_ref[...] = jnp.where(sel1, m1 / den, 0.0) + jnp.where(sel2, m2 / den, 0.0)
        chosen = jnp.where(sel1, 1.0, jnp.where(sel2, 1.0, 0.0))
        r = lax.broadcasted_iota(jnp.int32, (tm, tm), 0)
        c = lax.broadcasted_iota(jnp.int32, (tm, tm), 1)
        before = jnp.where(r < c, 1.0, 0.0).astype(BF16)
        rank = _dot(chosen.astype(BF16), before)
        pos_ref[...] = jnp.where(chosen > 0.0, rank, -1.0)
        o_ref[...] = x

    pos_e = pos_ref[pl.ds(e, 1), :]
    gate_e = gate_ref[pl.ds(e, 1), :]
    cnt = jnp.sum(jnp.where(pos_e >= 0.0, 1.0, 0.0)).astype(jnp.int32)

    def body(s, carry):
        slot = lax.broadcasted_iota(jnp.int32, (sub, tm), 0) + s * sub
        hit = pos_e == slot.astype(F32)
        onehot = jnp.where(hit, 1.0, 0.0).astype(BF16)
        xs = _dot(onehot, xn_ref[...]).astype(BF16)
        gs = jnp.sum(jnp.where(hit, gate_e, 0.0), axis=1, keepdims=True)
        hgu = _dot(xs, wgu_ref[0])
        act = (jax.nn.silu(hgu[:, :dff]) * hgu[:, dff:]).astype(BF16)
        yb = (_dot(act, wdn_ref[0]) * gs).astype(BF16)
        o_ref[...] += lax.dot_general(onehot, yb, (((0,), (0,)), ((), ())), preferred_element_type=F32)
        return carry

    lax.fori_loop(0, (cnt + sub - 1) // sub, body, 0)


def moe_residual(x, g, w_router_t, w_gu, w_down, li, tm, sub):
    n, d = x.shape
    _, ne, _, dff2 = w_gu.shape
    return pl.pallas_call(
        functools.partial(_moe_kernel, sub=sub),
        out_shape=jax.ShapeDtypeStruct((n, d), F32),
        grid=(n // tm, ne),
        in_specs=[
            pl.BlockSpec((tm, d), lambda i, e: (i, 0)),
            _full((1, d)),
            _full((ne, d)),
            pl.BlockSpec((None, 1, d, dff2), lambda i, e: (li, e, 0, 0)),
            pl.BlockSpec((None, 1, dff2 // 2, d), lambda i, e: (li, e, 0, 0)),
        ],
        out_specs=pl.BlockSpec((tm, d), lambda i, e: (i, 0)),
        scratch_shapes=[
            pltpu.VMEM((tm, d), BF16),
            pltpu.VMEM((ne, tm), F32),
            pltpu.VMEM((ne, tm), F32),
        ],
        compiler_params=_params("parallel", "arbitrary"),
        name="moe",
    )(x, g.reshape(1, d), w_router_t, w_gu, w_down)


def _trunk(x, n_t, rb, tb, nb, h0, conv0, s0, re0, im0, w):
    n, d = x.shape
    depth = w['norm_mix'].shape[0]
    d_a = h0.shape[-1]
    nh, dk = s0.shape[2], s0.shape[3]
    d_b = nh * dk
    ns = re0.shape[-2] * re0.shape[-1]
    tm = tb * rb
    n_even = (depth + 1) // 2
    new = {k: [] for k in ('h', 'conv', 're', 'im')}
    s_stack = None
    for l in range(depth):
        li = l // 2
        if l % 2 == 0:
            proj = norm_matmul(x, w['norm_mix'][l], w['even_w_in'], li, tm, w['even_w_in'].shape[2])
            conv_tm = conv0[li].transpose(1, 0, 2).reshape((CONV_W - 1) * rb, d_a)
            ya, h_new, conv_new = rglru(proj, n_t, conv_tm, h0[li], w['rglru_conv_w'][li], w['rglru_conv_b'][li],
                                        w['rglru_wg'][li], w['rglru_bg'][li], w['rglru_lambda'][li], tb=tb, rb=rb)
            new['h'].append(h_new)
            new['conv'].append(conv_new.reshape(CONV_W - 1, rb, d_a).transpose(1, 0, 2))
            ob, s_stack = hgrn2(proj, s0, li % s0.shape[0], s_stack, n_even,
                                w['hgrn2_lb_raw'], w['hgrn2_gnorm'][li], li, n_t=n_t, tb=tb, nb=nb)
            if nb != rb:
                ob = ob.reshape(rb // nb, n // rb, nb, d_b).transpose(1, 0, 2, 3).reshape(n, d_b)
            x = matmul_residual([ya, ob], w['even_w_out'], li, x, tm)
            hff = norm_swiglu(x, w['norm_ffn'][l], w['ffn_w_gu'], li, tm, w['ffn_w_gu'].shape[2] // 4)
            x = matmul_residual([hff], w['ffn_w_down'], li, x, tm)
        else:
            z, re_new, im_new = s5(x, n_t, w['norm_mix'][l], w['s5_wb'][li], w['s5_wc'][li], w['s5_a_re'][li],
                                   w['s5_a_im'][li], w['s5_d'][li], re0[li].reshape(rb, ns), im0[li].reshape(rb, ns),
                                   tb=tb, rb=rb)
            new['re'].append(re_new.reshape(re0.shape[1:]))
            new['im'].append(im_new.reshape(im0.shape[1:]))
            x = glu_residual(z, w['s5_w_glu'], li, x, tm, w['s5_w_glu'].shape[2] // 2)
            x = moe_residual(x, w['norm_ffn'][l], w['moe_w_router_t'][li], w['moe_w_gu'], w['moe_w_down'], li,
                             tm, V7X_MXU_DIM)
    stack = lambda k: jnp.stack(new[k])
    return x, (stack('h'), stack('conv'), s_stack, stack('re'), stack('im'))


def kernel(x_prompt, x_sample, state_rglru_h, state_rglru_conv, state_hgrn2, state_s5_re, state_s5_im,
           meta_tokens, norm_mix, norm_ffn, norm_final, even_w_in, even_w_out,
           rglru_conv_w, rglru_conv_b, rglru_w_a, rglru_b_a, rglru_w_x, rglru_b_x, rglru_lambda,
           hgrn2_lb_raw, hgrn2_gnorm, s5_lam_re, s5_lam_im, s5_log_dt, s5_b_re, s5_b_im, s5_c_re, s5_c_im,
           s5_d, s5_w_glu, ffn_w_gu, ffn_w_down, moe_w_router, moe_w_gu, moe_w_down):
    bp, tp0, d = x_prompt.shape
    bs, ts, _ = x_sample.shape
    tp = tp0 + N_META
    d_a = state_rglru_h.shape[-1]
    n_even, n_odd = state_rglru_h.shape[0], state_s5_re.shape[0]
    assert bp == SUBLANES and bs % SUBLANES == 0
    tb_p = PROMPT_BLOCK_ROWS // bp
    tp_pad = -(-tp // tb_p) * tb_p
    nb_s = HGRN_ROWS // ts
    assert (tp % (HGRN_ROWS // bp) == 0 and tb_p % (HGRN_ROWS // bp) == 0 and bs % nb_s == 0
            and nb_s * ts == HGRN_ROWS and nb_s % SUBLANES == 0)

    eye_a = jnp.eye(H_A, dtype=F32)
    block_diag = lambda m: jnp.einsum('lhij,hg->lhigj', m, eye_a).reshape(n_even, d_a, d_a)
    prep = [s5_prep(s5_lam_re[i], s5_lam_im[i], s5_log_dt[i], s5_b_re[i], s5_b_im[i]) for i in range(n_odd)]
    ns = s5_lam_re.shape[1] * s5_lam_re.shape[2]
    bshape = s5_b_re.shape[1:]
    w = {
        'norm_mix': norm_mix, 'norm_ffn': norm_ffn, 'norm_final': norm_final,
        'even_w_in': even_w_in.astype(BF16), 'even_w_out': even_w_out.astype(BF16),
        'rglru_conv_w': rglru_conv_w, 'rglru_conv_b': rglru_conv_b,
        'rglru_wg': jnp.concatenate([block_diag(rglru_w_a), block_diag(rglru_w_x)], axis=2).astype(BF16),
        'rglru_bg': jnp.concatenate([rglru_b_a, rglru_b_x], axis=1),
        'rglru_lambda': rglru_lambda, 'hgrn2_lb_raw': hgrn2_lb_raw, 'hgrn2_gnorm': hgrn2_gnorm,
        's5_a_re': [p[0].reshape(1, ns) for p in prep], 's5_a_im': [p[1].reshape(1, ns) for p in prep],
        's5_wb': [jnp.concatenate([_s5_block_diag_in(p[2].reshape(bshape)), _s5_block_diag_in(p[3].reshape(bshape))],
                                  axis=2).astype(BF16) for p in prep],
        's5_wc': [jnp.concatenate([_s5_block_diag_out(s5_c_re[i]), -_s5_block_diag_out(s5_c_im[i])],
                                  axis=1).astype(BF16) for i in range(n_odd)],
        's5_d': s5_d, 's5_w_glu': s5_w_glu.astype(BF16),
        'ffn_w_gu': ffn_w_gu.astype(BF16), 'ffn_w_down': ffn_w_down.astype(BF16),
        'moe_w_router_t': moe_w_router.transpose(0, 2, 1),
        'moe_w_gu': moe_w_gu.astype(BF16), 'moe_w_down': moe_w_down.astype(BF16),
    }

    tc_p = HGRN_ROWS // bp
    assert N_META == tc_p
    xm = to_time_major(x_prompt, meta_tokens.astype(x_prompt.dtype), tp_pad, tb_p // tc_p)
    zero = lambda ref, lead: jnp.zeros((lead, bp) + ref.shape[2:], ref.dtype)
    xp, p_new = _trunk(xm, tp, bp, tb_p, bp, zero(state_rglru_h, n_even), zero(state_rglru_conv, n_even),
                       zero(state_hgrn2, 1), zero(state_s5_re, n_odd), zero(state_s5_im, n_odd), w)
    nk_out = max(k for k in range(1, HGRN_ROWS // tc_p + 1) if tp0 % (k * tc_p) == 0)
    y_prompt = rmsnorm_batch_major(xp, norm_final, bp, N_META, tp0, tc_p, nk_out)

    xs = x_sample.transpose(1, 0, 2).reshape(ts * bs, d)
    xs, s_new = _trunk(xs, ts, bs, ts, nb_s, state_rglru_h, state_rglru_conv, state_hgrn2, state_s5_re, state_s5_im, w)
    y_sample = rmsnorm_rows(xs, norm_final, ts * bs).reshape(ts, bs, d).transpose(1, 0, 2)

    refs = (state_rglru_h, state_rglru_conv, state_hgrn2, state_s5_re, state_s5_im)
    cast = lambda new: tuple(a.astype(r.dtype) for a, r in zip(new, refs))
    return (y_prompt, y_sample) + cast(p_new) + cast(s_new)
```

```python
import functools

import jax
import jax.numpy as jnp
from jax import lax
from jax.experimental import pallas as pl
from jax.experimental.pallas import tpu as pltpu

F32 = jnp.float32
BF16 = jnp.bfloat16

EPS = 1e-6
N_META = 16
CONV_W = 4
RG_C = 8.0
H_A = 8
S5_P = 64

V7X_VMEM_LIMIT_BYTES = 56 * 1024 * 1024
SUBLANES = 8
LANES = 128
V7X_MXU_DIM = 256
HGRN_ROWS = 128
S5_ROWS = 128
HGRN_EXP_CLAMP = 80.0
PROMPT_BLOCK_ROWS = 896
SCAN_CARRY_ELEMS = 16 * SUBLANES * LANES


def _params(*sem):
    return pltpu.CompilerParams(dimension_semantics=sem, vmem_limit_bytes=V7X_VMEM_LIMIT_BYTES)


def _rms(x, g):
    ms = jnp.mean(x * x, axis=-1, keepdims=True)
    return x * lax.rsqrt(ms + EPS) * g


def _dot(a, b):
    return jnp.dot(a, b, preferred_element_type=F32)


def _full(shape):
    return pl.BlockSpec(shape, lambda *_: (0,) * len(shape))


def _norm_matmul_kernel(x_ref, g_ref, w_ref, o_ref, xn_ref):
    @pl.when(pl.program_id(1) == 0)
    def _():
        xn_ref[...] = _rms(x_ref[...], g_ref[...]).astype(BF16)

    o_ref[...] = _dot(xn_ref[...], w_ref[...]).astype(o_ref.dtype)


def norm_matmul(x, g, w, li, tm, tn):
    n, d = x.shape
    nout = w.shape[2]
    return pl.pallas_call(
        _norm_matmul_kernel,
        out_shape=jax.ShapeDtypeStruct((n, nout), BF16),
        grid=(n // tm, nout // tn),
        in_specs=[
            pl.BlockSpec((tm, d), lambda i, j: (i, 0)),
            _full((1, d)),
            pl.BlockSpec((None, d, tn), lambda i, j: (li, 0, j)),
        ],
        out_specs=pl.BlockSpec((tm, tn), lambda i, j: (i, j)),
        scratch_shapes=[pltpu.VMEM((tm, d), BF16)],
        compiler_params=_params("parallel", "arbitrary"),
        name="norm_matmul",
    )(x, g.reshape(1, d), w)


def _mix_ffn_kernel(ya_ref, ob_ref, x_ref, wo_ref, g_ref, wg_ref, wu_ref, wdn_ref, o_ref, xn_ref):
    j = pl.program_id(1)
    d_a = ya_ref.shape[1]

    @pl.when(j == 0)
    def _():
        x1 = x_ref[...] + _dot(ya_ref[...], wo_ref[:d_a, :]) + _dot(ob_ref[...], wo_ref[d_a:, :])
        xn_ref[...] = _rms(x1, g_ref[...]).astype(BF16)
        o_ref[...] = x1

    xn = xn_ref[...]
    h = (jax.nn.silu(_dot(xn, wg_ref[...])) * _dot(xn, wu_ref[...])).astype(BF16)
    o_ref[...] += _dot(h, wdn_ref[...])


def mix_ffn(ya, ob, x, w_out, g, w_gu, w_down, li, tm, chunk):
    n, d = x.shape
    dff = w_down.shape[1]
    assert dff % chunk == 0 and chunk % LANES == 0
    nj = dff // chunk
    rows = lambda a: pl.BlockSpec((tm, a.shape[1]), lambda i, j: (i, 0))
    return pl.pallas_call(
        _mix_ffn_kernel,
        out_shape=jax.ShapeDtypeStruct((n, d), F32),
        grid=(n // tm, nj),
        in_specs=[
            rows(ya), rows(ob), rows(x),
            pl.BlockSpec((None,) + w_out.shape[1:], lambda i, j: (li, 0, 0)),
            _full((1, d)),
            pl.BlockSpec((None, d, chunk), lambda i, j: (li, 0, j)),
            pl.BlockSpec((None, d, chunk), lambda i, j: (li, 0, j + nj)),
            pl.BlockSpec((None, chunk, d), lambda i, j: (li, j, 0)),
        ],
        out_specs=pl.BlockSpec((tm, d), lambda i, j: (i, 0)),
        scratch_shapes=[pltpu.VMEM((tm, d), BF16)],
        compiler_params=_params("parallel", "arbitrary"),
        name="mix_ffn",
    )(ya, ob, x, w_out, g.reshape(1, d), w_gu, w_gu, w_down)


def _glu_residual_kernel(z_ref, wv_ref, wg_ref, r_ref, o_ref):
    z = z_ref[...]
    val = _dot(z, wv_ref[...])
    gate = _dot(z, wg_ref[...])
    o_ref[...] = r_ref[...] + val * jax.nn.sigmoid(gate)


def glu_residual(z, w_glu, li, res, tm, tn):
    n, d = z.shape
    dout = w_glu.shape[2] // 2
    nj = dout // tn
    return pl.pallas_call(
        _glu_residual_kernel,
        out_shape=jax.ShapeDtypeStruct((n, dout), F32),
        grid=(n // tm, nj),
        in_specs=[
            pl.BlockSpec((tm, d), lambda i, j: (i, 0)),
            pl.BlockSpec((None, d, tn), lambda i, j: (li, 0, j)),
            pl.BlockSpec((None, d, tn), lambda i, j: (li, 0, j + nj)),
            pl.BlockSpec((tm, tn), lambda i, j: (i, j)),
        ],
        out_specs=pl.BlockSpec((tm, tn), lambda i, j: (i, j)),
        compiler_params=_params("parallel", "arbitrary"),
        name="glu_residual",
    )(z, w_glu, w_glu, res)


def _rmsnorm_kernel(x_ref, g_ref, o_ref):
    o_ref[...] = _rms(x_ref[...], g_ref[...])


def rmsnorm_rows(x, g, tm):
    n, d = x.shape
    return pl.pallas_call(
        _rmsnorm_kernel,
        out_shape=jax.ShapeDtypeStruct((n, d), F32),
        grid=(n // tm,),
        in_specs=[pl.BlockSpec((tm, d), lambda i: (i, 0)), _full((1, d))],
        out_specs=pl.BlockSpec((tm, d), lambda i: (i, 0)),
        compiler_params=_params("parallel"),
        name="final_rmsnorm",
    )(x, g.reshape(1, d))


def _rmsnorm_batch_major_kernel(*refs, nk, tc, nb):
    x_refs, g_ref, o_ref = refs[:nk], refs[nk], refs[nk + 1]
    d = o_ref.shape[-1]
    for k in range(nk):
        y = _rms(x_refs[k][...], g_ref[...])
        o_ref[:, k * tc:(k + 1) * tc, :] = jnp.swapaxes(y.reshape(tc, nb, d), 0, 1)


def rmsnorm_batch_major(x, g, nb, t_skip, t_out, tc, nk):
    n, d = x.shape
    rows = tc * nb
    assert t_skip % tc == 0 and t_out % (nk * tc) == 0
    specs = [pl.BlockSpec((rows, d), lambda j, k=k: (nk * j + t_skip // tc + k, 0)) for k in range(nk)]
    return pl.pallas_call(
        functools.partial(_rmsnorm_batch_major_kernel, nk=nk, tc=tc, nb=nb),
        out_shape=jax.ShapeDtypeStruct((nb, t_out, d), F32),
        grid=(t_out // (nk * tc),),
        in_specs=specs + [_full((1, d))],
        out_specs=pl.BlockSpec((nb, nk * tc, d), lambda j: (0, j, 0)),
        compiler_params=_params("parallel"),
        name="final_rmsnorm_batch_major",
    )(*([x] * nk), g.reshape(1, d))


def _to_time_major_kernel(*refs, nk, n_chunks):
    x_refs, lead_ref, o_ref = refs[:nk], refs[nk], refs[nk + 1]
    i = pl.program_id(0)
    nb, tc, d = x_refs[0].shape
    rows = tc * nb
    for k in range(nk):
        chunk = i * nk + k - 1
        val = jnp.swapaxes(x_refs[k][...], 0, 1).reshape(rows, d)
        if k == 0:
            lead = jnp.broadcast_to(lead_ref[...][:, None, :], (tc, nb, d)).reshape(rows, d)
            val = jnp.where(i == 0, lead, val)
        o_ref[k * rows:(k + 1) * rows, :] = jnp.where(chunk < n_chunks, val, 0.0)


def to_time_major(x, lead, t_pad, nk):
    nb, t, d = x.shape
    tc = lead.shape[0]
    assert t % tc == 0 and t_pad % (nk * tc) == 0
    n_chunks = t // tc
    specs = [pl.BlockSpec((nb, tc, d), lambda i, k=k: (0, jnp.clip(i * nk + k - 1, 0, n_chunks - 1), 0))
             for k in range(nk)]
    return pl.pallas_call(
        functools.partial(_to_time_major_kernel, nk=nk, n_chunks=n_chunks),
        out_shape=jax.ShapeDtypeStruct((t_pad * nb, d), F32),
        grid=(t_pad // (nk * tc),),
        in_specs=specs + [_full((tc, d))],
        out_specs=pl.BlockSpec((nk * tc * nb, d), lambda i: (i, 0)),
        compiler_params=_params("parallel"),
        name="to_time_major",
    )(*([x] * nk), lead)


def _rglru_kernel(xa_ref, ga_ref, conv0_ref, h0_ref, cw_ref, cb_ref, wg_ref, bg_ref, lam_ref,
                  ya_ref, hlast_ref, convnew_ref, xpad_ref, a_ref, u_ref, *, n_t, tb, rb):
    i = pl.program_id(0)
    rows = tb * rb
    tail = (CONV_W - 1) * rb
    c = xa_ref.shape[-1]
    t_valid = jnp.minimum(n_t - i * tb, tb)

    @pl.when(i == 0)
    def _():
        xpad_ref[0:tail, :] = conv0_ref[...]
        hlast_ref[...] = h0_ref[...]

    @pl.when(i > 0)
    def _():
        xpad_ref[0:tail, :] = xpad_ref[rows:rows + tail, :]

    xpad_ref[tail:tail + rows, :] = xa_ref[...].astype(F32)
    xc = cb_ref[...]
    for k in range(CONV_W):
        xc = xc + cw_ref[k:k + 1, :] * xpad_ref[k * rb:k * rb + rows, :]

    gates = _dot(xc.astype(BF16), wg_ref[...]) + bg_ref[...]
    r = jax.nn.sigmoid(gates[:, :c])
    ig = jax.nn.sigmoid(gates[:, c:])
    log_a = (-RG_C) * r * jax.nn.softplus(-lam_ref[...])
    a = jnp.exp(log_a)
    mult = jnp.sqrt(1.0 - a * a)
    a_ref[...] = a
    u_ref[...] = mult * ig * xc

    lc = min(c, max(LANES, SCAN_CARRY_ELEMS // rb // LANES * LANES))
    for c0 in range(0, c, lc):
        def body(t, h, c0=c0):
            sl = pl.ds(pl.multiple_of(t * rb, rb), rb)
            h = a_ref[sl, c0:c0 + lc] * h + u_ref[sl, c0:c0 + lc]
            u_ref[sl, c0:c0 + lc] = h
            return h

        hlast_ref[:, c0:c0 + lc] = lax.fori_loop(0, t_valid, body, hlast_ref[:, c0:c0 + lc])

    live = lax.broadcasted_iota(jnp.int32, (rows, c), 0) < t_valid * rb
    ya_ref[...] = jnp.where(live, u_ref[...] * jax.nn.gelu(ga_ref[...].astype(F32)), 0.0).astype(ya_ref.dtype)
    convnew_ref[...] = xpad_ref[pl.ds(pl.multiple_of(t_valid * rb, rb), tail), :]


def rglru(proj, n_t, conv0, h0, cw, cb, wg, bg, lam, tb, rb):
    c = h0.shape[1]
    n_rows = proj.shape[0]
    rows = tb * rb
    tail = (CONV_W - 1) * rb
    return pl.pallas_call(
        functools.partial(_rglru_kernel, n_t=n_t, tb=tb, rb=rb),
        out_shape=(
            jax.ShapeDtypeStruct((n_rows, c), BF16),
            jax.ShapeDtypeStruct((rb, c), F32),
            jax.ShapeDtypeStruct((tail, c), F32),
        ),
        grid=(n_rows // rows,),
        in_specs=[
            pl.BlockSpec((rows, c), lambda i: (i, 0)),
            pl.BlockSpec((rows, c), lambda i: (i, 1)),
            _full((tail, c)), _full((rb, c)), _full((CONV_W, c)), _full((1, c)),
            _full((c, 2 * c)), _full((1, 2 * c)), _full((1, c)),
        ],
        out_specs=(
            pl.BlockSpec((rows, c), lambda i: (i, 0)),
            _full((rb, c)),
            _full((tail, c)),
        ),
        scratch_shapes=[
            pltpu.VMEM((rows + tail, c), F32),
            pltpu.VMEM((rows, c), F32),
            pltpu.VMEM((rows, c), F32),
        ],
        compiler_params=_params("arbitrary"),
        name="rglru",
    )(proj, proj, conv0, h0, cw, cb.reshape(1, c), wg, bg.reshape(1, 2 * c), lam.reshape(1, c))


def _hgrn2_kernel(q_ref, f_ref, v_ref, gb_ref, s0_ref, lbraw_ref, gn_ref, *rest, layer, n_t, tb, nb, nh):
    ob_ref, snew_ref, st_ref = rest[-3:]
    i = pl.program_id(1)
    m = HGRN_ROWS
    tc = m // nb
    d = q_ref.shape[-1]
    dk = d // nh
    n_chunks = jnp.minimum(n_t - i * tb, tb) // tc

    @pl.when(i == 0)
    def _():
        for b in range(nb):
            for h in range(nh):
                st_ref[h, :, b * dk:(b + 1) * dk] = s0_ref[b, h].T

    @pl.when(n_chunks < tb // tc)
    def _():
        ob_ref[...] = jnp.zeros_like(ob_ref)

    p = jax.nn.softmax(lbraw_ref[...], axis=0)
    cum = p[0:1, :]
    for r in range(1, layer + 1):
        cum = cum + p[r:r + 1, :]
    lb = cum - p[0:1, :]
    log_lb = jnp.log(lb)
    log_1mlb = jnp.log1p(-lb)

    row = lax.broadcasted_iota(jnp.int32, (m, m), 0)
    col = lax.broadcasted_iota(jnp.int32, (m, m), 1)
    same_seq_causal = jnp.where(((row & (nb - 1)) == (col & (nb - 1))) & (col <= row), 1.0, 0.0)
    bid = lax.broadcasted_iota(jnp.int32, (m, dk), 0) & (nb - 1)

    def chunk(c, carry):
        def rows_of(ref):
            if len(ref.shape) == 2:
                return ref[pl.ds(pl.multiple_of(c * m, m), m), :]
            return ref[pl.ds(c * tc, tc)].reshape(m, d)

        q = jax.nn.silu(rows_of(q_ref).astype(F32))
        fr = rows_of(f_ref).astype(F32)
        v = rows_of(v_ref).astype(BF16)
        gb = rows_of(gb_ref).astype(F32)
        logf = jnp.logaddexp(log_lb, log_1mlb + jax.nn.log_sigmoid(fr))
        k = 1.0 - jnp.exp(logf)
        slabs = [logf[0:nb]]
        for t in range(1, tc):
            slabs.append(slabs[-1] + logf[t * nb:(t + 1) * nb])
        g = jnp.concatenate(slabs, axis=0)
        g_last = jnp.concatenate([slabs[-1]] * tc, axis=0)
        qt = (q * jnp.exp(g)).astype(BF16)
        kt = (k * jnp.exp(jnp.minimum(-g, HGRN_EXP_CLAMP))).astype(BF16)
        ks = (k * jnp.exp(g_last - g)).astype(BF16)
        dec = jnp.exp(slabs[-1])

        outs = []
        for h in range(nh):
            hs = slice(h * dk, (h + 1) * dk)
            qt_h, kt_h, ks_h, v_h = qt[:, hs], kt[:, hs], ks[:, hs], v[:, hs]
            att = lax.dot_general(qt_h, kt_h, (((1,), (1,)), ((), ())), preferred_element_type=F32)
            att = (att * same_seq_causal).astype(BF16)
            o = _dot(att, v_h)
            zero = jnp.zeros_like(qt_h)
            expand = lambda x: jnp.concatenate([jnp.where(bid == b, x, zero) for b in range(nb)], axis=1)
            st_h = st_ref[h]
            o = o + lax.dot_general(expand(qt_h), st_h.astype(BF16), (((1,), (1,)), ((), ())),
                                    preferred_element_type=F32)
            dst = lax.dot_general(v_h, expand(ks_h), (((0,), (0,)), ((), ())), preferred_element_type=F32)
            dec_row = jnp.concatenate([dec[b:b + 1, hs] for b in range(nb)], axis=1)
            st_ref[h] = st_h * dec_row + dst
            outs.append(_rms(o, gn_ref[:, hs]))
        ob = jnp.concatenate(outs, axis=1) * jax.nn.silu(gb)
        ob_ref[pl.ds(pl.multiple_of(c * m, m), m), :] = ob.astype(ob_ref.dtype)
        return carry

    lax.fori_loop(0, n_chunks, chunk, 0)

    @pl.when(i == pl.num_programs(1) - 1)
    def _():
        for b in range(nb):
            for h in range(nh):
                snew_ref[b, h] = st_ref[h, :, b * dk:(b + 1) * dk].T


def hgrn2(proj, s0, s0_layer, s_stack, n_layers, lb_raw, gnorm, layer, n_t, tb, nb):
    _, bsz, nh, dk, _ = s0.shape
    d = nh * dk
    off = proj.shape[-1] // d - 4
    nbb = bsz // nb
    n_tpad = proj.shape[0] // bsz
    nt = n_tpad // tb
    if nb == bsz:
        col = lambda k: pl.BlockSpec((tb * nb, d), lambda j, i, k=k: (i, k + off))
    else:
        proj = proj.reshape(n_tpad, bsz, proj.shape[1])
        col = lambda k: pl.BlockSpec((tb, nb, d), lambda j, i, k=k: (i, j, k + off))
    in_specs = [
        col(0), col(1), col(2), col(3),
        pl.BlockSpec((None, nb, nh, dk, dk), lambda j, i: (s0_layer, j, 0, 0, 0)),
        _full(lb_raw.shape),
        _full((1, d)),
    ]
    args = [proj, proj, proj, proj, s0, lb_raw, gnorm.reshape(1, d)]
    aliases = {}
    if s_stack is not None:
        in_specs.append(pl.BlockSpec(memory_space=pl.ANY))
        args.append(s_stack)
        aliases = {len(args) - 1: 1}
    return pl.pallas_call(
        functools.partial(_hgrn2_kernel, layer=layer, n_t=n_t, tb=tb, nb=nb, nh=nh),
        out_shape=(
            jax.ShapeDtypeStruct((n_tpad * bsz, d), BF16),
            jax.ShapeDtypeStruct((n_layers, bsz, nh, dk, dk), F32),
        ),
        grid=(nbb, nt),
        in_specs=in_specs,
        out_specs=(
            pl.BlockSpec((tb * nb, d), lambda j, i: (j * nt + i, 0)),
            pl.BlockSpec((None, nb, nh, dk, dk), lambda j, i: (layer, j, 0, 0, 0)),
        ),
        scratch_shapes=[pltpu.VMEM((nh, dk, nb * dk), F32)],
        input_output_aliases=aliases,
        compiler_params=_params("arbitrary", "arbitrary"),
        name="hgrn2",
    )(*args)


def _s5_prep_kernel(lr_ref, li_ref, ldt_ref, bre_ref, bim_ref, ar_ref, ai_ref, ore_ref, oim_ref):
    lr = lr_ref[...]
    li = li_ref[...]
    dt = jnp.exp(ldt_ref[...])
    mag = jnp.exp(lr * dt)
    ar = mag * jnp.cos(li * dt)
    ai = mag * jnp.sin(li * dt)
    den = lr * lr + li * li
    cr = ((ar - 1.0) * lr + ai * li) / den
    ci = (ai * lr - (ar - 1.0) * li) / den
    ar_ref[...] = ar
    ai_ref[...] = ai
    ore_ref[...] = cr * bre_ref[...] - ci * bim_ref[...]
    oim_ref[...] = cr * bim_ref[...] + ci * bre_ref[...]


def s5_prep(lam_re, lam_im, log_dt, b_re, b_im):
    g, p, c = b_re.shape
    ns = g * p
    colv = jax.ShapeDtypeStruct((ns, 1), F32)
    mat = jax.ShapeDtypeStruct((ns, c), F32)
    ldt = jnp.broadcast_to(log_dt[:, None], (g, p)).reshape(ns, 1)
    return pl.pallas_call(
        _s5_prep_kernel, out_shape=(colv, colv, mat, mat), name="s5_prep",
    )(lam_re.reshape(ns, 1), lam_im.reshape(ns, 1), ldt, b_re.reshape(ns, c), b_im.reshape(ns, c))


def _s5_kernel(x_ref, g_ref, wb_ref, wc_ref, ar_ref, ai_ref, d_ref, h0r_ref, h0i_ref,
               z_ref, hr_ref, hi_ref, *, n_t, tb, rb):
    i = pl.program_id(0)
    nch = wb_ref.shape[0]
    cw = wb_ref.shape[1]
    sw = wb_ref.shape[2] // 2
    m = S5_ROWS
    tsb = m // rb
    nsl = rb // SUBLANES
    n_sub = jnp.minimum(n_t - i * tb, tb) // tsb

    @pl.when(i == 0)
    def _():
        hr_ref[...] = h0r_ref[...]
        hi_ref[...] = h0i_ref[...]

    @pl.when(n_sub < tb // tsb)
    def _():
        z_ref[...] = jnp.zeros_like(z_ref)

    def sub_block(j, carry):
        r0 = pl.multiple_of(j * m, m)
        xn = _rms(x_ref[pl.ds(r0, m), :], g_ref[...])
        xnb = xn.astype(BF16)
        drive = [_dot(xnb[:, c * cw:(c + 1) * cw], wb_ref[c]) for c in range(nch)]
        for c in range(nch):
            cs = slice(c * cw, (c + 1) * cw)
            ss = slice(c * sw, (c + 1) * sw)
            u = xn[:, cs]
            bu = drive[c]
            ar = jnp.broadcast_to(ar_ref[:, ss], (SUBLANES, sw))
            ai = jnp.broadcast_to(ai_ref[:, ss], (SUBLANES, sw))
            out_r = [None] * (tsb * nsl)
            out_i = [None] * (tsb * nsl)
            for s in range(nsl):
                srow = slice(s * SUBLANES, (s + 1) * SUBLANES)
                hr, hi = hr_ref[srow, ss], hi_ref[srow, ss]
                for t in range(tsb):
                    lo = t * rb + s * SUBLANES
                    hr, hi = (ar * hr - ai * hi + bu[lo:lo + SUBLANES, :sw],
                              ar * hi + ai * hr + bu[lo:lo + SUBLANES, sw:])
                    out_r[t * nsl + s] = hr
                    out_i[t * nsl + s] = hi
                hr_ref[srow, ss] = hr
                hi_ref[srow, ss] = hi
            hcat = jnp.concatenate([jnp.concatenate(out_r, axis=0).astype(BF16),
                                    jnp.concatenate(out_i, axis=0).astype(BF16)], axis=1)
            y = _dot(hcat, wc_ref[c]) + d_ref[:, cs] * u
            z_ref[pl.ds(r0, m), cs] = jax.nn.gelu(y).astype(z_ref.dtype)
        return carry

    lax.fori_loop(0, n_sub, sub_block, 0)


def s5(x, n_t, g, wb, wc, a_re, a_im, dskip, h0r, h0i, tb, rb):
    n_rows, d = x.shape
    ns = h0r.shape[1]
    rows = tb * rb
    assert S5_ROWS % rb == 0 and rows % S5_ROWS == 0 and n_t % (S5_ROWS // rb) == 0
    return pl.pallas_call(
        functools.partial(_s5_kernel, n_t=n_t, tb=tb, rb=rb),
        out_shape=(
            jax.ShapeDtypeStruct((n_rows, d), BF16),
            jax.ShapeDtypeStruct((rb, ns), F32),
            jax.ShapeDtypeStruct((rb, ns), F32),
        ),
        grid=(n_rows // rows,),
        in_specs=[
            pl.BlockSpec((rows, d), lambda i: (i, 0)),
            _full((1, d)), _full(wb.shape), _full(wc.shape),
            _full((1, ns)), _full((1, ns)), _full((1, d)),
            _full((rb, ns)), _full((rb, ns)),
        ],
        out_specs=(
            pl.BlockSpec((rows, d), lambda i: (i, 0)),
            _full((rb, ns)), _full((rb, ns)),
        ),
        compiler_params=_params("arbitrary"),
        name="s5",
    )(x, g.reshape(1, d), wb, wc, a_re, a_im, dskip.reshape(1, d), h0r, h0i)


def _s5_block_diag_in(b):
    g, p, c = b.shape
    gpc = V7X_MXU_DIM // c
    bt = b.transpose(0, 2, 1).reshape(g // gpc, gpc, c, p)
    out = jnp.einsum('ngcp,gh->ngchp', bt, jnp.eye(gpc, dtype=b.dtype))
    return out.reshape(g // gpc, gpc * c, gpc * p)


def _s5_block_diag_out(cm):
    g, c, p = cm.shape
    gpc = V7X_MXU_DIM // c
    ct = cm.transpose(0, 2, 1).reshape(g // gpc, gpc, p, c)
    out = jnp.einsum('ngpc,gh->ngphc', ct, jnp.eye(gpc, dtype=cm.dtype))
    return out.reshape(g // gpc, gpc * p, gpc * c)


def _moe_kernel(x_ref, g_ref, wrt_ref, wgu_ref, wdn_ref, o_ref, xn_ref, pos_ref, gate_ref, *, sub):
    e = pl.program_id(1)
    tm = x_ref.shape[0]
    ne = wrt_ref.shape[0]
    dff = wdn_ref.shape[1]

    @pl.when(e == 0)
    def _():
        x = x_ref[...]
        xn = _rms(x, g_ref[...])
        xn_ref[...] = xn.astype(BF16)
        logits = lax.dot_general(wrt_ref[...], xn, (((1,), (1,)), ((), ())),
                                 precision=lax.Precision.HIGHEST, preferred_element_type=F32)
        ex = jnp.exp(logits - jnp.max(logits, axis=0, keepdims=True))
        probs = ex / jnp.sum(ex, axis=0, keepdims=True)
        eid = lax.broadcasted_iota(jnp.int32, (ne, tm), 0).astype(F32)
        m1 = jnp.max(probs, axis=0, keepdims=True)
        i1 = jnp.min(jnp.where(probs == m1, eid, float(ne)), axis=0, keepdims=True)
        sel1 = eid == i1
        rest = jnp.where(sel1, -1.0, probs)
        m2 = jnp.max(rest, axis=0, keepdims=True)
        i2 = jnp.min(jnp.where(rest == m2, eid, float(ne)), axis=0, keepdims=True)
        sel2 = eid == i2
        den = m1 + m2
        gate_ref[...] = jnp.where(sel1, m1 / den, 0.0) + jnp.where(sel2, m2 / den, 0.0)
        chosen = jnp.where(sel1, 1.0, jnp.where(sel2, 1.0, 0.0))
        r = lax.broadcasted_iota(jnp.int32, (tm, tm), 0)
        c = lax.broadcasted_iota(jnp.int32, (tm, tm), 1)
        before = jnp.where(r < c, 1.0, 0.0).astype(BF16)
        rank = _dot(chosen.astype(BF16), before)
        pos_ref[...] = jnp.where(chosen > 0.0, rank, -1.0)
        o_ref[...] = x

    pos_e = pos_ref[pl.ds(e, 1), :]
    gate_e = gate_ref[pl.ds(e, 1), :]
    cnt = jnp.sum(jnp.where(pos_e >= 0.0, 1.0, 0.0)).astype(jnp.int32)

    def body(s, carry):
        slot = lax.broadcasted_iota(jnp.int32, (sub, tm), 0) + s * sub
        hit = pos_e == slot.astype(F32)
        onehot = jnp.where(hit, 1.0, 0.0).astype(BF16)
        xs = _dot(onehot, xn_ref[...]).astype(BF16)
        gs = jnp.sum(jnp.where(hit, gate_e, 0.0), axis=1, keepdims=True)
        hgu = _dot(xs, wgu_ref[0])
        act = (jax.nn.silu(hgu[:, :dff]) * hgu[:, dff:]).astype(BF16)
        yb = (_dot(act, wdn_ref[0]) * gs).astype(BF16)
        o_ref[...] += lax.dot_general(onehot, yb, (((0,), (0,)), ((), ())), preferred_element_type=F32)
        return carry

    lax.fori_loop(0, (cnt + sub - 1) // sub, body, 0)


def moe_residual(x, g, w_router_t, w_gu, w_down, li, tm, sub):
    n, d = x.shape
    _, ne, _, dff2 = w_gu.shape
    return pl.pallas_call(
        functools.partial(_moe_kernel, sub=sub),
        out_shape=jax.ShapeDtypeStruct((n, d), F32),
        grid=(n // tm, ne),
        in_specs=[
            pl.BlockSpec((tm, d), lambda i, e: (i, 0)),
            _full((1, d)),
            _full((ne, d)),
            pl.BlockSpec((None, 1, d, dff2), lambda i, e: (li, e, 0, 0)),
            pl.BlockSpec((None, 1, dff2 // 2, d), lambda i, e: (li, e, 0, 0)),
        ],
        out_specs=pl.BlockSpec((tm, d), lambda i, e: (i, 0)),
        scratch_shapes=[
            pltpu.VMEM((tm, d), BF16),
            pltpu.VMEM((ne, tm), F32),
            pltpu.VMEM((ne, tm), F32),
        ],
        compiler_params=_params("parallel", "arbitrary"),
        name="moe",
    )(x, g.reshape(1, d), w_router_t, w_gu, w_down)


def _trunk(x, n_t, rb, tb, nb, h0, conv0, s0, re0, im0, w):
    n, d = x.shape
    depth = w['norm_mix'].shape[0]
    d_a = h0.shape[-1]
    nh, dk = s0.shape[2], s0.shape[3]
    d_b = nh * dk
    ns = re0.shape[-2] * re0.shape[-1]
    tm = tb * rb
    n_even = (depth + 1) // 2
    new = {k: [] for k in ('h', 'conv', 're', 'im')}
    s_stack = None
    for l in range(depth):
        li = l // 2
        if l % 2 == 0:
            proj = norm_matmul(x, w['norm_mix'][l], w['even_w_in'], li, tm, w['even_w_in'].shape[2])
            conv_tm = conv0[li].transpose(1, 0, 2).reshape((CONV_W - 1) * rb, d_a)
            ya, h_new, conv_new = rglru(proj, n_t, conv_tm, h0[li], w['rglru_conv_w'][li], w['rglru_conv_b'][li],
                                        w['rglru_wg'][li], w['rglru_bg'][li], w['rglru_lambda'][li], tb=tb, rb=rb)
            new['h'].append(h_new)
            new['conv'].append(conv_new.reshape(CONV_W - 1, rb, d_a).transpose(1, 0, 2))
            ob, s_stack = hgrn2(proj, s0, li % s0.shape[0], s_stack, n_even,
                                w['hgrn2_lb_raw'], w['hgrn2_gnorm'][li], li, n_t=n_t, tb=tb, nb=nb)
            if nb != rb:
                ob = ob.reshape(rb // nb, n // rb, nb, d_b).transpose(1, 0, 2, 3).reshape(n, d_b)
            x = mix_ffn(ya, ob, x, w['even_w_out'], w['norm_ffn'][l], w['ffn_w_gu'], w['ffn_w_down'], li, tm,
                        w['ffn_w_down'].shape[1] // 2)
        else:
            z, re_new, im_new = s5(x, n_t, w['norm_mix'][l], w['s5_wb'][li], w['s5_wc'][li], w['s5_a_re'][li],
                                   w['s5_a_im'][li], w['s5_d'][li], re0[li].reshape(rb, ns), im0[li].reshape(rb, ns),
                                   tb=tb, rb=rb)
            new['re'].append(re_new.reshape(re0.shape[1:]))
            new['im'].append(im_new.reshape(im0.shape[1:]))
            x = glu_residual(z, w['s5_w_glu'], li, x, tm, w['s5_w_glu'].shape[2] // 2)
            x = moe_residual(x, w['norm_ffn'][l], w['moe_w_router_t'][li], w['moe_w_gu'], w['moe_w_down'], li,
                             tm, V7X_MXU_DIM)
    stack = lambda k: jnp.stack(new[k])
    return x, (stack('h'), stack('conv'), s_stack, stack('re'), stack('im'))


def kernel(x_prompt, x_sample, state_rglru_h, state_rglru_conv, state_hgrn2, state_s5_re, state_s5_im,
           meta_tokens, norm_mix, norm_ffn, norm_final, even_w_in, even_w_out,
           rglru_conv_w, rglru_conv_b, rglru_w_a, rglru_b_a, rglru_w_x, rglru_b_x, rglru_lambda,
           hgrn2_lb_raw, hgrn2_gnorm, s5_lam_re, s5_lam_im, s5_log_dt, s5_b_re, s5_b_im, s5_c_re, s5_c_im,
           s5_d, s5_w_glu, ffn_w_gu, ffn_w_down, moe_w_router, moe_w_gu, moe_w_down):
    bp, tp0, d = x_prompt.shape
    bs, ts, _ = x_sample.shape
    tp = tp0 + N_META
    d_a = state_rglru_h.shape[-1]
    n_even, n_odd = state_rglru_h.shape[0], state_s5_re.shape[0]
    assert bp == SUBLANES and bs % SUBLANES == 0
    tb_p = PROMPT_BLOCK_ROWS // bp
    tp_pad = -(-tp // tb_p) * tb_p
    nb_s = HGRN_ROWS // ts
    assert (tp % (HGRN_ROWS // bp) == 0 and tb_p % (HGRN_ROWS // bp) == 0 and bs % nb_s == 0
            and nb_s * ts == HGRN_ROWS and nb_s % SUBLANES == 0)

    eye_a = jnp.eye(H_A, dtype=F32)
    block_diag = lambda m: jnp.einsum('lhij,hg->lhigj', m, eye_a).reshape(n_even, d_a, d_a)
    prep = [s5_prep(s5_lam_re[i], s5_lam_im[i], s5_log_dt[i], s5_b_re[i], s5_b_im[i]) for i in range(n_odd)]
    ns = s5_lam_re.shape[1] * s5_lam_re.shape[2]
    bshape = s5_b_re.shape[1:]
    w = {
        'norm_mix': norm_mix, 'norm_ffn': norm_ffn, 'norm_final': norm_final,
        'even_w_in': even_w_in.astype(BF16), 'even_w_out': even_w_out.astype(BF16),
        'rglru_conv_w': rglru_conv_w, 'rglru_conv_b': rglru_conv_b,
        'rglru_wg': jnp.concatenate([block_diag(rglru_w_a), block_diag(rglru_w_x)], axis=2).astype(BF16),
        'rglru_bg': jnp.concatenate([rglru_b_a, rglru_b_x], axis=1),
        'rglru_lambda': rglru_lambda, 'hgrn2_lb_raw': hgrn2_lb_raw, 'hgrn2_gnorm': hgrn2_gnorm,
        's5_a_re': [p[0].reshape(1, ns) for p in prep], 's5_a_im': [p[1].reshape(1, ns) for p in prep],
        's5_wb': [jnp.concatenate([_s5_block_diag_in(p[2].reshape(bshape)), _s5_block_diag_in(p[3].reshape(bshape))],
                                  axis=2).astype(BF16) for p in prep],
        's5_wc': [jnp.concatenate([_s5_block_diag_out(s5_c_re[i]), -_s5_block_diag_out(s5_c_im[i])],
                                  axis=1).astype(BF16) for i in range(n_odd)],
        's5_d': s5_d, 's5_w_glu': s5_w_glu.astype(BF16),
        'ffn_w_gu': ffn_w_gu.astype(BF16), 'ffn_w_down': ffn_w_down.astype(BF16),
        'moe_w_router_t': moe_w_router.transpose(0, 2, 1),
        'moe_w_gu': moe_w_gu.astype(BF16), 'moe_w_down': moe_w_down.astype(BF16),
    }

    tc_p = HGRN_ROWS // bp
    assert N_META == tc_p
    xm = to_time_major(x_prompt, meta_tokens.astype(x_prompt.dtype), tp_pad, tb_p // tc_p)
    zero = lambda ref, lead: jnp.zeros((lead, bp) + ref.shape[2:], ref.dtype)
    xp, p_new = _trunk(xm, tp, bp, tb_p, bp, zero(state_rglru_h, n_even), zero(state_rglru_conv, n_even),
                       zero(state_hgrn2, 1), zero(state_s5_re, n_odd), zero(state_s5_im, n_odd), w)
    nk_out = max(k for k in range(1, HGRN_ROWS // tc_p + 1) if tp0 % (k * tc_p) == 0)
    y_prompt = rmsnorm_batch_major(xp, norm_final, bp, N_META, tp0, tc_p, nk_out)

    xs = x_sample.transpose(1, 0, 2).reshape(ts * bs, d)
    xs, s_new = _trunk(xs, ts, bs, ts, nb_s, state_rglru_h, state_rglru_conv, state_hgrn2, state_s5_re, state_s5_im, w)
    y_sample = rmsnorm_rows(xs, norm_final, ts * bs).reshape(ts, bs, d).transpose(1, 0, 2)

    refs = (state_rglru_h, state_rglru_conv, state_hgrn2, state_s5_re, state_s5_im)
    cast = lambda new: tuple(a.astype(r.dtype) for a, r in zip(new, refs))
    return (y_prompt, y_sample) + cast(p_new) + cast(s_new)
```

```python
import functools

import jax
import jax.numpy as jnp
from jax import lax
from jax.experimental import pallas as pl
from jax.experimental.pallas import tpu as pltpu

F32 = jnp.float32
BF16 = jnp.bfloat16

EPS = 1e-6
N_META = 16
CONV_W = 4
RG_C = 8.0
H_A = 8
S5_P = 64

V7X_VMEM_LIMIT_BYTES = 56 * 1024 * 1024
SUBLANES = 8
LANES = 128
V7X_MXU_DIM = 256
HGRN_ROWS = 128
S5_ROWS = 128
HGRN_EXP_CLAMP = 80.0
PROMPT_BLOCK_ROWS = 768
SCAN_CARRY_ELEMS = 16 * SUBLANES * LANES


def _params(*sem):
    return pltpu.CompilerParams(dimension_semantics=sem, vmem_limit_bytes=V7X_VMEM_LIMIT_BYTES)


def _rms(x, g):
    ms = jnp.mean(x * x, axis=-1, keepdims=True)
    return x * lax.rsqrt(ms + EPS) * g


def _dot(a, b):
    return jnp.dot(a, b, preferred_element_type=F32)


def _full(shape):
    return pl.BlockSpec(shape, lambda *_: (0,) * len(shape))


def _norm_matmul_kernel(x_ref, g_ref, w_ref, o_ref, xn_ref):
    @pl.when(pl.program_id(1) == 0)
    def _():
        xn_ref[...] = _rms(x_ref[...], g_ref[...]).astype(BF16)

    o_ref[...] = _dot(xn_ref[...], w_ref[...]).astype(o_ref.dtype)


def norm_matmul(x, g, w, li, tm, tn):
    n, d = x.shape
    nout = w.shape[2]
    return pl.pallas_call(
        _norm_matmul_kernel,
        out_shape=jax.ShapeDtypeStruct((n, nout), BF16),
        grid=(n // tm, nout // tn),
        in_specs=[
            pl.BlockSpec((tm, d), lambda i, j: (i, 0)),
            _full((1, d)),
            pl.BlockSpec((None, d, tn), lambda i, j: (li, 0, j)),
        ],
        out_specs=pl.BlockSpec((tm, tn), lambda i, j: (i, j)),
        scratch_shapes=[pltpu.VMEM((tm, d), BF16)],
        compiler_params=_params("parallel", "arbitrary"),
        name="norm_matmul",
    )(x, g.reshape(1, d), w)


def _mix_ffn_kernel(ya_ref, ob_ref, x_ref, wo_ref, g_ref, wg_ref, wu_ref, wdn_ref, o_ref, xn_ref):
    j = pl.program_id(1)
    d_a = ya_ref.shape[1]

    @pl.when(j == 0)
    def _():
        x1 = x_ref[...] + _dot(ya_ref[...], wo_ref[:d_a, :]) + _dot(ob_ref[...], wo_ref[d_a:, :])
        xn_ref[...] = _rms(x1, g_ref[...]).astype(BF16)
        o_ref[...] = x1

    xn = xn_ref[...]
    h = (jax.nn.silu(_dot(xn, wg_ref[...])) * _dot(xn, wu_ref[...])).astype(BF16)
    o_ref[...] += _dot(h, wdn_ref[...])


def mix_ffn(ya, ob, x, w_out, g, w_gu, w_down, li, tm, chunk):
    n, d = x.shape
    dff = w_down.shape[1]
    assert dff % chunk == 0 and chunk % LANES == 0
    nj = dff // chunk
    rows = lambda a: pl.BlockSpec((tm, a.shape[1]), lambda i, j: (i, 0))
    return pl.pallas_call(
        _mix_ffn_kernel,
        out_shape=jax.ShapeDtypeStruct((n, d), F32),
        grid=(n // tm, nj),
        in_specs=[
            rows(ya), rows(ob), rows(x),
            pl.BlockSpec((None,) + w_out.shape[1:], lambda i, j: (li, 0, 0)),
            _full((1, d)),
            pl.BlockSpec((None, d, chunk), lambda i, j: (li, 0, j)),
            pl.BlockSpec((None, d, chunk), lambda i, j: (li, 0, j + nj)),
            pl.BlockSpec((None, chunk, d), lambda i, j: (li, j, 0)),
        ],
        out_specs=pl.BlockSpec((tm, d), lambda i, j: (i, 0)),
        scratch_shapes=[pltpu.VMEM((tm, d), BF16)],
        compiler_params=_params("parallel", "arbitrary"),
        name="mix_ffn",
    )(ya, ob, x, w_out, g.reshape(1, d), w_gu, w_gu, w_down)


def _glu_residual_kernel(z_ref, wv_ref, wg_ref, r_ref, o_ref):
    z = z_ref[...]
    val = _dot(z, wv_ref[...])
    gate = _dot(z, wg_ref[...])
    o_ref[...] = r_ref[...] + val * jax.nn.sigmoid(gate)


def glu_residual(z, w_glu, li, res, tm, tn):
    n, d = z.shape
    dout = w_glu.shape[2] // 2
    nj = dout // tn
    return pl.pallas_call(
        _glu_residual_kernel,
        out_shape=jax.ShapeDtypeStruct((n, dout), F32),
        grid=(n // tm, nj),
        in_specs=[
            pl.BlockSpec((tm, d), lambda i, j: (i, 0)),
            pl.BlockSpec((None, d, tn), lambda i, j: (li, 0, j)),
            pl.BlockSpec((None, d, tn), lambda i, j: (li, 0, j + nj)),
            pl.BlockSpec((tm, tn), lambda i, j: (i, j)),
        ],
        out_specs=pl.BlockSpec((tm, tn), lambda i, j: (i, j)),
        compiler_params=_params("parallel", "arbitrary"),
        name="glu_residual",
    )(z, w_glu, w_glu, res)


def _rmsnorm_kernel(x_ref, g_ref, o_ref):
    o_ref[...] = _rms(x_ref[...], g_ref[...])


def rmsnorm_rows(x, g, tm):
    n, d = x.shape
    return pl.pallas_call(
        _rmsnorm_kernel,
        out_shape=jax.ShapeDtypeStruct((n, d), F32),
        grid=(n // tm,),
        in_specs=[pl.BlockSpec((tm, d), lambda i: (i, 0)), _full((1, d))],
        out_specs=pl.BlockSpec((tm, d), lambda i: (i, 0)),
        compiler_params=_params("parallel"),
        name="final_rmsnorm",
    )(x, g.reshape(1, d))


def _rmsnorm_batch_major_kernel(*refs, nk, tc, nb):
    x_refs, g_ref, o_ref = refs[:nk], refs[nk], refs[nk + 1]
    d = o_ref.shape[-1]
    for k in range(nk):
        y = _rms(x_refs[k][...], g_ref[...])
        o_ref[:, k * tc:(k + 1) * tc, :] = jnp.swapaxes(y.reshape(tc, nb, d), 0, 1)


def rmsnorm_batch_major(x, g, nb, t_skip, t_out, tc, nk):
    n, d = x.shape
    rows = tc * nb
    assert t_skip % tc == 0 and t_out % (nk * tc) == 0
    specs = [pl.BlockSpec((rows, d), lambda j, k=k: (nk * j + t_skip // tc + k, 0)) for k in range(nk)]
    return pl.pallas_call(
        functools.partial(_rmsnorm_batch_major_kernel, nk=nk, tc=tc, nb=nb),
        out_shape=jax.ShapeDtypeStruct((nb, t_out, d), F32),
        grid=(t_out // (nk * tc),),
        in_specs=specs + [_full((1, d))],
        out_specs=pl.BlockSpec((nb, nk * tc, d), lambda j: (0, j, 0)),
        compiler_params=_params("parallel"),
        name="final_rmsnorm_batch_major",
    )(*([x] * nk), g.reshape(1, d))


def _to_time_major_kernel(*refs, nk, n_chunks):
    x_refs, lead_ref, o_ref = refs[:nk], refs[nk], refs[nk + 1]
    i = pl.program_id(0)
    nb, tc, d = x_refs[0].shape
    rows = tc * nb
    for k in range(nk):
        chunk = i * nk + k - 1
        val = jnp.swapaxes(x_refs[k][...], 0, 1).reshape(rows, d)
        if k == 0:
            lead = jnp.broadcast_to(lead_ref[...][:, None, :], (tc, nb, d)).reshape(rows, d)
            val = jnp.where(i == 0, lead, val)
        o_ref[k * rows:(k + 1) * rows, :] = jnp.where(chunk < n_chunks, val, 0.0)


def to_time_major(x, lead, t_pad, nk):
    nb, t, d = x.shape
    tc = lead.shape[0]
    assert t % tc == 0 and t_pad % (nk * tc) == 0
    n_chunks = t // tc
    specs = [pl.BlockSpec((nb, tc, d), lambda i, k=k: (0, jnp.clip(i * nk + k - 1, 0, n_chunks - 1), 0))
             for k in range(nk)]
    return pl.pallas_call(
        functools.partial(_to_time_major_kernel, nk=nk, n_chunks=n_chunks),
        out_shape=jax.ShapeDtypeStruct((t_pad * nb, d), F32),
        grid=(t_pad // (nk * tc),),
        in_specs=specs + [_full((tc, d))],
        out_specs=pl.BlockSpec((nk * tc * nb, d), lambda i: (i, 0)),
        compiler_params=_params("parallel"),
        name="to_time_major",
    )(*([x] * nk), lead)


def _rglru_kernel(xa_ref, ga_ref, conv0_ref, h0_ref, cw_ref, cb_ref, wg_ref, bg_ref, lam_ref,
                  ya_ref, hlast_ref, convnew_ref, xpad_ref, a_ref, u_ref, *, n_t, tb, rb):
    i = pl.program_id(0)
    rows = tb * rb
    tail = (CONV_W - 1) * rb
    c = xa_ref.shape[-1]
    t_valid = jnp.minimum(n_t - i * tb, tb)

    @pl.when(i == 0)
    def _():
        xpad_ref[0:tail, :] = conv0_ref[...]
        hlast_ref[...] = h0_ref[...]

    @pl.when(i > 0)
    def _():
        xpad_ref[0:tail, :] = xpad_ref[rows:rows + tail, :]

    xpad_ref[tail:tail + rows, :] = xa_ref[...].astype(F32)
    xc = cb_ref[...]
    for k in range(CONV_W):
        xc = xc + cw_ref[k:k + 1, :] * xpad_ref[k * rb:k * rb + rows, :]

    gates = _dot(xc.astype(BF16), wg_ref[...]) + bg_ref[...]
    r = jax.nn.sigmoid(gates[:, :c])
    ig = jax.nn.sigmoid(gates[:, c:])
    log_a = (-RG_C) * r * jax.nn.softplus(-lam_ref[...])
    a = jnp.exp(log_a)
    mult = jnp.sqrt(1.0 - a * a)
    a_ref[...] = a
    u_ref[...] = mult * ig * xc

    lc = min(c, max(LANES, SCAN_CARRY_ELEMS // rb // LANES * LANES))
    for c0 in range(0, c, lc):
        def body(t, h, c0=c0):
            sl = pl.ds(pl.multiple_of(t * rb, rb), rb)
            h = a_ref[sl, c0:c0 + lc] * h + u_ref[sl, c0:c0 + lc]
            u_ref[sl, c0:c0 + lc] = h
            return h

        hlast_ref[:, c0:c0 + lc] = lax.fori_loop(0, t_valid, body, hlast_ref[:, c0:c0 + lc])

    live = lax.broadcasted_iota(jnp.int32, (rows, c), 0) < t_valid * rb
    ya_ref[...] = jnp.where(live, u_ref[...] * jax.nn.gelu(ga_ref[...].astype(F32)), 0.0).astype(ya_ref.dtype)
    convnew_ref[...] = xpad_ref[pl.ds(pl.multiple_of(t_valid * rb, rb), tail), :]


def rglru(proj, n_t, conv0, h0, cw, cb, wg, bg, lam, tb, rb):
    c = h0.shape[1]
    n_rows = proj.shape[0]
    rows = tb * rb
    tail = (CONV_W - 1) * rb
    return pl.pallas_call(
        functools.partial(_rglru_kernel, n_t=n_t, tb=tb, rb=rb),
        out_shape=(
            jax.ShapeDtypeStruct((n_rows, c), BF16),
            jax.ShapeDtypeStruct((rb, c), F32),
            jax.ShapeDtypeStruct((tail, c), F32),
        ),
        grid=(n_rows // rows,),
        in_specs=[
            pl.BlockSpec((rows, c), lambda i: (i, 0)),
            pl.BlockSpec((rows, c), lambda i: (i, 1)),
            _full((tail, c)), _full((rb, c)), _full((CONV_W, c)), _full((1, c)),
            _full((c, 2 * c)), _full((1, 2 * c)), _full((1, c)),
        ],
        out_specs=(
            pl.BlockSpec((rows, c), lambda i: (i, 0)),
            _full((rb, c)),
            _full((tail, c)),
        ),
        scratch_shapes=[
            pltpu.VMEM((rows + tail, c), F32),
            pltpu.VMEM((rows, c), F32),
            pltpu.VMEM((rows, c), F32),
        ],
        compiler_params=_params("arbitrary"),
        name="rglru",
    )(proj, proj, conv0, h0, cw, cb.reshape(1, c), wg, bg.reshape(1, 2 * c), lam.reshape(1, c))


def _hgrn2_kernel(q_ref, f_ref, v_ref, gb_ref, s0_ref, lbraw_ref, gn_ref, *rest, layer, n_t, tb, nb, nh):
    ob_ref, snew_ref, st_ref = rest[-3:]
    i = pl.program_id(1)
    m = HGRN_ROWS
    tc = m // nb
    d = q_ref.shape[-1]
    dk = d // nh
    n_chunks = jnp.minimum(n_t - i * tb, tb) // tc

    @pl.when(i == 0)
    def _():
        for b in range(nb):
            for h in range(nh):
                st_ref[h, :, b * dk:(b + 1) * dk] = s0_ref[b, h].T

    @pl.when(n_chunks < tb // tc)
    def _():
        ob_ref[...] = jnp.zeros_like(ob_ref)

    p = jax.nn.softmax(lbraw_ref[...], axis=0)
    cum = p[0:1, :]
    for r in range(1, layer + 1):
        cum = cum + p[r:r + 1, :]
    lb = cum - p[0:1, :]
    log_lb = jnp.log(lb)
    log_1mlb = jnp.log1p(-lb)

    row = lax.broadcasted_iota(jnp.int32, (m, m), 0)
    col = lax.broadcasted_iota(jnp.int32, (m, m), 1)
    same_seq_causal = jnp.where(((row & (nb - 1)) == (col & (nb - 1))) & (col <= row), 1.0, 0.0)
    bid = lax.broadcasted_iota(jnp.int32, (m, dk), 0) & (nb - 1)

    def chunk(c, carry):
        def rows_of(ref):
            if len(ref.shape) == 2:
                return ref[pl.ds(pl.multiple_of(c * m, m), m), :]
            return ref[pl.ds(c * tc, tc)].reshape(m, d)

        q = jax.nn.silu(rows_of(q_ref).astype(F32))
        fr = rows_of(f_ref).astype(F32)
        v = rows_of(v_ref).astype(BF16)
        gb = rows_of(gb_ref).astype(F32)
        logf = jnp.logaddexp(log_lb, log_1mlb + jax.nn.log_sigmoid(fr))
        k = 1.0 - jnp.exp(logf)
        slabs = [logf[0:nb]]
        for t in range(1, tc):
            slabs.append(slabs[-1] + logf[t * nb:(t + 1) * nb])
        g = jnp.concatenate(slabs, axis=0)
        g_last = jnp.concatenate([slabs[-1]] * tc, axis=0)
        qt = (q * jnp.exp(g)).astype(BF16)
        kt = (k * jnp.exp(jnp.minimum(-g, HGRN_EXP_CLAMP))).astype(BF16)
        ks = (k * jnp.exp(g_last - g)).astype(BF16)
        dec = jnp.exp(slabs[-1])

        outs = []
        for h in range(nh):
            hs = slice(h * dk, (h + 1) * dk)
            qt_h, kt_h, ks_h, v_h = qt[:, hs], kt[:, hs], ks[:, hs], v[:, hs]
            att = lax.dot_general(qt_h, kt_h, (((1,), (1,)), ((), ())), preferred_element_type=F32)
            att = (att * same_seq_causal).astype(BF16)
            o = _dot(att, v_h)
            zero = jnp.zeros_like(qt_h)
            expand = lambda x: jnp.concatenate([jnp.where(bid == b, x, zero) for b in range(nb)], axis=1)
            st_h = st_ref[h]
            o = o + lax.dot_general(expand(qt_h), st_h.astype(BF16), (((1,), (1,)), ((), ())),
                                    preferred_element_type=F32)
            dst = lax.dot_general(v_h, expand(ks_h), (((0,), (0,)), ((), ())), preferred_element_type=F32)
            dec_row = jnp.concatenate([dec[b:b + 1, hs] for b in range(nb)], axis=1)
            st_ref[h] = st_h * dec_row + dst
            outs.append(_rms(o, gn_ref[:, hs]))
        ob = jnp.concatenate(outs, axis=1) * jax.nn.silu(gb)
        ob_ref[pl.ds(pl.multiple_of(c * m, m), m), :] = ob.astype(ob_ref.dtype)
        return carry

    lax.fori_loop(0, n_chunks, chunk, 0)

    @pl.when(i == pl.num_programs(1) - 1)
    def _():
        for b in range(nb):
            for h in range(nh):
                snew_ref[b, h] = st_ref[h, :, b * dk:(b + 1) * dk].T


def hgrn2(proj, s0, s0_layer, s_stack, n_layers, lb_raw, gnorm, layer, n_t, tb, nb):
    _, bsz, nh, dk, _ = s0.shape
    d = nh * dk
    off = proj.shape[-1] // d - 4
    nbb = bsz // nb
    n_tpad = proj.shape[0] // bsz
    nt = n_tpad // tb
    if nb == bsz:
        col = lambda k: pl.BlockSpec((tb * nb, d), lambda j, i, k=k: (i, k + off))
    else:
        proj = proj.reshape(n_tpad, bsz, proj.shape[1])
        col = lambda k: pl.BlockSpec((tb, nb, d), lambda j, i, k=k: (i, j, k + off))
    in_specs = [
        col(0), col(1), col(2), col(3),
        pl.BlockSpec((None, nb, nh, dk, dk), lambda j, i: (s0_layer, j, 0, 0, 0)),
        _full(lb_raw.shape),
        _full((1, d)),
    ]
    args = [proj, proj, proj, proj, s0, lb_raw, gnorm.reshape(1, d)]
    aliases = {}
    if s_stack is not None:
        in_specs.append(pl.BlockSpec(memory_space=pl.ANY))
        args.append(s_stack)
        aliases = {len(args) - 1: 1}
    return pl.pallas_call(
        functools.partial(_hgrn2_kernel, layer=layer, n_t=n_t, tb=tb, nb=nb, nh=nh),
        out_shape=(
            jax.ShapeDtypeStruct((n_tpad * bsz, d), BF16),
            jax.ShapeDtypeStruct((n_layers, bsz, nh, dk, dk), F32),
        ),
        grid=(nbb, nt),
        in_specs=in_specs,
        out_specs=(
            pl.BlockSpec((tb * nb, d), lambda j, i: (j * nt + i, 0)),
            pl.BlockSpec((None, nb, nh, dk, dk), lambda j, i: (layer, j, 0, 0, 0)),
        ),
        scratch_shapes=[pltpu.VMEM((nh, dk, nb * dk), F32)],
        input_output_aliases=aliases,
        compiler_params=_params("arbitrary", "arbitrary"),
        name="hgrn2",
    )(*args)


def _s5_prep_kernel(lr_ref, li_ref, ldt_ref, bre_ref, bim_ref, ar_ref, ai_ref, ore_ref, oim_ref):
    lr = lr_ref[...]
    li = li_ref[...]
    dt = jnp.exp(ldt_ref[...])
    mag = jnp.exp(lr * dt)
    ar = mag * jnp.cos(li * dt)
    ai = mag * jnp.sin(li * dt)
    den = lr * lr + li * li
    cr = ((ar - 1.0) * lr + ai * li) / den
    ci = (ai * lr - (ar - 1.0) * li) / den
    ar_ref[...] = ar
    ai_ref[...] = ai
    ore_ref[...] = cr * bre_ref[...] - ci * bim_ref[...]
    oim_ref[...] = cr * bim_ref[...] + ci * bre_ref[...]


def s5_prep(lam_re, lam_im, log_dt, b_re, b_im):
    g, p, c = b_re.shape
    ns = g * p
    colv = jax.ShapeDtypeStruct((ns, 1), F32)
    mat = jax.ShapeDtypeStruct((ns, c), F32)
    ldt = jnp.broadcast_to(log_dt[:, None], (g, p)).reshape(ns, 1)
    return pl.pallas_call(
        _s5_prep_kernel, out_shape=(colv, colv, mat, mat), name="s5_prep",
    )(lam_re.reshape(ns, 1), lam_im.reshape(ns, 1), ldt, b_re.reshape(ns, c), b_im.reshape(ns, c))


def _s5_kernel(x_ref, g_ref, wb_ref, wc_ref, ar_ref, ai_ref, d_ref, h0r_ref, h0i_ref,
               z_ref, hr_ref, hi_ref, *, n_t, tb, rb):
    i = pl.program_id(0)
    nch = wb_ref.shape[0]
    cw = wb_ref.shape[1]
    sw = wb_ref.shape[2] // 2
    m = S5_ROWS
    tsb = m // rb
    nsl = rb // SUBLANES
    n_sub = jnp.minimum(n_t - i * tb, tb) // tsb

    @pl.when(i == 0)
    def _():
        hr_ref[...] = h0r_ref[...]
        hi_ref[...] = h0i_ref[...]

    @pl.when(n_sub < tb // tsb)
    def _():
        z_ref[...] = jnp.zeros_like(z_ref)

    def sub_block(j, carry):
        r0 = pl.multiple_of(j * m, m)
        xn = _rms(x_ref[pl.ds(r0, m), :], g_ref[...])
        xnb = xn.astype(BF16)
        drive = [_dot(xnb[:, c * cw:(c + 1) * cw], wb_ref[c]) for c in range(nch)]
        for c in range(nch):
            cs = slice(c * cw, (c + 1) * cw)
            ss = slice(c * sw, (c + 1) * sw)
            u = xn[:, cs]
            bu = drive[c]
            ar = jnp.broadcast_to(ar_ref[:, ss], (SUBLANES, sw))
            ai = jnp.broadcast_to(ai_ref[:, ss], (SUBLANES, sw))
            out_r = [None] * (tsb * nsl)
            out_i = [None] * (tsb * nsl)
            for s in range(nsl):
                srow = slice(s * SUBLANES, (s + 1) * SUBLANES)
                hr, hi = hr_ref[srow, ss], hi_ref[srow, ss]
                for t in range(tsb):
                    lo = t * rb + s * SUBLANES
                    hr, hi = (ar * hr - ai * hi + bu[lo:lo + SUBLANES, :sw],
                              ar * hi + ai * hr + bu[lo:lo + SUBLANES, sw:])
                    out_r[t * nsl + s] = hr
                    out_i[t * nsl + s] = hi
                hr_ref[srow, ss] = hr
                hi_ref[srow, ss] = hi
            hcat = jnp.concatenate([jnp.concatenate(out_r, axis=0).astype(BF16),
                                    jnp.concatenate(out_i, axis=0).astype(BF16)], axis=1)
            y = _dot(hcat, wc_ref[c]) + d_ref[:, cs] * u
            z_ref[pl.ds(r0, m), cs] = jax.nn.gelu(y).astype(z_ref.dtype)
        return carry

    lax.fori_loop(0, n_sub, sub_block, 0)


def s5(x, n_t, g, wb, wc, a_re, a_im, dskip, h0r, h0i, tb, rb):
    n_rows, d = x.shape
    ns = h0r.shape[1]
    rows = tb * rb
    assert S5_ROWS % rb == 0 and rows % S5_ROWS == 0 and n_t % (S5_ROWS // rb) == 0
    return pl.pallas_call(
        functools.partial(_s5_kernel, n_t=n_t, tb=tb, rb=rb),
        out_shape=(
            jax.ShapeDtypeStruct((n_rows, d), BF16),
            jax.ShapeDtypeStruct((rb, ns), F32),
            jax.ShapeDtypeStruct((rb, ns), F32),
        ),
        grid=(n_rows // rows,),
        in_specs=[
            pl.BlockSpec((rows, d), lambda i: (i, 0)),
            _full((1, d)), _full(wb.shape), _full(wc.shape),
            _full((1, ns)), _full((1, ns)), _full((1, d)),
            _full((rb, ns)), _full((rb, ns)),
        ],
        out_specs=(
            pl.BlockSpec((rows, d), lambda i: (i, 0)),
            _full((rb, ns)), _full((rb, ns)),
        ),
        compiler_params=_params("arbitrary"),
        name="s5",
    )(x, g.reshape(1, d), wb, wc, a_re, a_im, dskip.reshape(1, d), h0r, h0i)


def _s5_block_diag_in(b):
    g, p, c = b.shape
    gpc = V7X_MXU_DIM // c
    bt = b.transpose(0, 2, 1).reshape(g // gpc, gpc, c, p)
    out = jnp.einsum('ngcp,gh->ngchp', bt, jnp.eye(gpc, dtype=b.dtype))
    return out.reshape(g // gpc, gpc * c, gpc * p)


def _s5_block_diag_out(cm):
    g, c, p = cm.shape
    gpc = V7X_MXU_DIM // c
    ct = cm.transpose(0, 2, 1).reshape(g // gpc, gpc, p, c)
    out = jnp.einsum('ngpc,gh->ngphc', ct, jnp.eye(gpc, dtype=cm.dtype))
    return out.reshape(g // gpc, gpc * p, gpc * c)


def _moe_kernel(x_ref, g_ref, wrt_ref, wgu_ref, wdn_ref, o_ref, xn_ref, pos_ref, gate_ref, *, sub):
    e = pl.program_id(1)
    tm = x_ref.shape[0]
    ne = wrt_ref.shape[0]
    dff = wdn_ref.shape[1]

    @pl.when(e == 0)
    def _():
        x = x_ref[...]
        xn = _rms(x, g_ref[...])
        xn_ref[...] = xn.astype(BF16)
        logits = lax.dot_general(wrt_ref[...], xn, (((1,), (1,)), ((), ())),
                                 precision=lax.Precision.HIGHEST, preferred_element_type=F32)
        ex = jnp.exp(logits - jnp.max(logits, axis=0, keepdims=True))
        probs = ex / jnp.sum(ex, axis=0, keepdims=True)
        eid = lax.broadcasted_iota(jnp.int32, (ne, tm), 0).astype(F32)
        m1 = jnp.max(probs, axis=0, keepdims=True)
        i1 = jnp.min(jnp.where(probs == m1, eid, float(ne)), axis=0, keepdims=True)
        sel1 = eid == i1
        rest = jnp.where(sel1, -1.0, probs)
        m2 = jnp.max(rest, axis=0, keepdims=True)
        i2 = jnp.min(jnp.where(rest == m2, eid, float(ne)), axis=0, keepdims=True)
        sel2 = eid == i2
        den = m1 + m2
        gate_ref[...] = jnp.where(sel1, m1 / den, 0.0) + jnp.where(sel2, m2 / den, 0.0)
        chosen = jnp.where(sel1, 1.0, jnp.where(sel2, 1.0, 0.0))
        r = lax.broadcasted_iota(jnp.int32, (tm, tm), 0)
        c = lax.broadcasted_iota(jnp.int32, (tm, tm), 1)
        before = jnp.where(r < c, 1.0, 0.0).astype(BF16)
        rank = _dot(chosen.astype(BF16), before)
        pos_ref[...] = jnp.where(chosen > 0.0, rank, -1.0)
        o_ref[...] = x

    pos_e = pos_ref[pl.ds(e, 1), :]
    gate_e = gate_ref[pl.ds(e, 1), :]
    cnt = jnp.sum(jnp.where(pos_e >= 0.0, 1.0, 0.0)).astype(jnp.int32)

    def body(s, carry):
        slot = lax.broadcasted_iota(jnp.int32, (sub, tm), 0) + s * sub
        hit = pos_e == slot.astype(F32)
        onehot = jnp.where(hit, 1.0, 0.0).astype(BF16)
        xs = _dot(onehot, xn_ref[...]).astype(BF16)
        gs = jnp.sum(jnp.where(hit, gate_e, 0.0), axis=1, keepdims=True)
        hgu = _dot(xs, wgu_ref[0])
        act = (jax.nn.silu(hgu[:, :dff]) * hgu[:, dff:]).astype(BF16)
        yb = (_dot(act, wdn_ref[0]) * gs).astype(BF16)
        o_ref[...] += lax.dot_general(onehot, yb, (((0,), (0,)), ((), ())), preferred_element_type=F32)
        return carry

    lax.fori_loop(0, (cnt + sub - 1) // sub, body, 0)


def moe_residual(x, g, w_router_t, w_gu, w_down, li, tm, sub):
    n, d = x.shape
    _, ne, _, dff2 = w_gu.shape
    return pl.pallas_call(
        functools.partial(_moe_kernel, sub=sub),
        out_shape=jax.ShapeDtypeStruct((n, d), F32),
        grid=(n // tm, ne),
        in_specs=[
            pl.BlockSpec((tm, d), lambda i, e: (i, 0)),
            _full((1, d)),
            _full((ne, d)),
            pl.BlockSpec((None, 1, d, dff2), lambda i, e: (li, e, 0, 0)),
            pl.BlockSpec((None, 1, dff2 // 2, d), lambda i, e: (li, e, 0, 0)),
        ],
        out_specs=pl.BlockSpec((tm, d), lambda i, e: (i, 0)),
        scratch_shapes=[
            pltpu.VMEM((tm, d), BF16),
            pltpu.VMEM((ne, tm), F32),
            pltpu.VMEM((ne, tm), F32),
        ],
        compiler_params=_params("parallel", "arbitrary"),
        name="moe",
    )(x, g.reshape(1, d), w_router_t, w_gu, w_down)


def _trunk(x, n_t, rb, tb, nb, h0, conv0, s0, re0, im0, w):
    n, d = x.shape
    depth = w['norm_mix'].shape[0]
    d_a = h0.shape[-1]
    nh, dk = s0.shape[2], s0.shape[3]
    d_b = nh * dk
    ns = re0.shape[-2] * re0.shape[-1]
    tm = tb * rb
    n_even = (depth + 1) // 2
    new = {k: [] for k in ('h', 'conv', 're', 'im')}
    s_stack = None
    for l in range(depth):
        li = l // 2
        if l % 2 == 0:
            proj = norm_matmul(x, w['norm_mix'][l], w['even_w_in'], li, tm, w['even_w_in'].shape[2])
            conv_tm = conv0[li].transpose(1, 0, 2).reshape((CONV_W - 1) * rb, d_a)
            ya, h_new, conv_new = rglru(proj, n_t, conv_tm, h0[li], w['rglru_conv_w'][li], w['rglru_conv_b'][li],
                                        w['rglru_wg'][li], w['rglru_bg'][li], w['rglru_lambda'][li], tb=tb, rb=rb)
            new['h'].append(h_new)
            new['conv'].append(conv_new.reshape(CONV_W - 1, rb, d_a).transpose(1, 0, 2))
            ob, s_stack = hgrn2(proj, s0, li % s0.shape[0], s_stack, n_even,
                                w['hgrn2_lb_raw'], w['hgrn2_gnorm'][li], li, n_t=n_t, tb=tb, nb=nb)
            if nb != rb:
                ob = ob.reshape(rb // nb, n // rb, nb, d_b).transpose(1, 0, 2, 3).reshape(n, d_b)
            x = mix_ffn(ya, ob, x, w['even_w_out'], w['norm_ffn'][l], w['ffn_w_gu'], w['ffn_w_down'], li, tm,
                        w['ffn_w_down'].shape[1] // 2)
        else:
            z, re_new, im_new = s5(x, n_t, w['norm_mix'][l], w['s5_wb'][li], w['s5_wc'][li], w['s5_a_re'][li],
                                   w['s5_a_im'][li], w['s5_d'][li], re0[li].reshape(rb, ns), im0[li].reshape(rb, ns),
                                   tb=tb, rb=rb)
            new['re'].append(re_new.reshape(re0.shape[1:]))
            new['im'].append(im_new.reshape(im0.shape[1:]))
            x = glu_residual(z, w['s5_w_glu'], li, x, tm, w['s5_w_glu'].shape[2] // 2)
            x = moe_residual(x, w['norm_ffn'][l], w['moe_w_router_t'][li], w['moe_w_gu'], w['moe_w_down'], li,
                             tm, V7X_MXU_DIM)
    stack = lambda k: jnp.stack(new[k])
    return x, (stack('h'), stack('conv'), s_stack, stack('re'), stack('im'))


def kernel(x_prompt, x_sample, state_rglru_h, state_rglru_conv, state_hgrn2, state_s5_re, state_s5_im,
           meta_tokens, norm_mix, norm_ffn, norm_final, even_w_in, even_w_out,
           rglru_conv_w, rglru_conv_b, rglru_w_a, rglru_b_a, rglru_w_x, rglru_b_x, rglru_lambda,
           hgrn2_lb_raw, hgrn2_gnorm, s5_lam_re, s5_lam_im, s5_log_dt, s5_b_re, s5_b_im, s5_c_re, s5_c_im,
           s5_d, s5_w_glu, ffn_w_gu, ffn_w_down, moe_w_router, moe_w_gu, moe_w_down):
    bp, tp0, d = x_prompt.shape
    bs, ts, _ = x_sample.shape
    tp = tp0 + N_META
    d_a = state_rglru_h.shape[-1]
    n_even, n_odd = state_rglru_h.shape[0], state_s5_re.shape[0]
    assert bp == SUBLANES and bs % SUBLANES == 0
    tb_p = PROMPT_BLOCK_ROWS // bp
    tp_pad = -(-tp // tb_p) * tb_p
    nb_s = HGRN_ROWS // ts
    assert (tp % (HGRN_ROWS // bp) == 0 and tb_p % (HGRN_ROWS // bp) == 0 and bs % nb_s == 0
            and nb_s * ts == HGRN_ROWS and nb_s % SUBLANES == 0)

    eye_a = jnp.eye(H_A, dtype=F32)
    block_diag = lambda m: jnp.einsum('lhij,hg->lhigj', m, eye_a).reshape(n_even, d_a, d_a)
    prep = [s5_prep(s5_lam_re[i], s5_lam_im[i], s5_log_dt[i], s5_b_re[i], s5_b_im[i]) for i in range(n_odd)]
    ns = s5_lam_re.shape[1] * s5_lam_re.shape[2]
    bshape = s5_b_re.shape[1:]
    w = {
        'norm_mix': norm_mix, 'norm_ffn': norm_ffn, 'norm_final': norm_final,
        'even_w_in': even_w_in.astype(BF16), 'even_w_out': even_w_out.astype(BF16),
        'rglru_conv_w': rglru_conv_w, 'rglru_conv_b': rglru_conv_b,
        'rglru_wg': jnp.concatenate([block_diag(rglru_w_a), block_diag(rglru_w_x)], axis=2).astype(BF16),
        'rglru_bg': jnp.concatenate([rglru_b_a, rglru_b_x], axis=1),
        'rglru_lambda': rglru_lambda, 'hgrn2_lb_raw': hgrn2_lb_raw, 'hgrn2_gnorm': hgrn2_gnorm,
        's5_a_re': [p[0].reshape(1, ns) for p in prep], 's5_a_im': [p[1].reshape(1, ns) for p in prep],
        's5_wb': [jnp.concatenate([_s5_block_diag_in(p[2].reshape(bshape)), _s5_block_diag_in(p[3].reshape(bshape))],
                                  axis=2).astype(BF16) for p in prep],
        's5_wc': [jnp.concatenate([_s5_block_diag_out(s5_c_re[i]), -_s5_block_diag_out(s5_c_im[i])],
                                  axis=1).astype(BF16) for i in range(n_odd)],
        's5_d': s5_d, 's5_w_glu': s5_w_glu.astype(BF16),
        'ffn_w_gu': ffn_w_gu.astype(BF16), 'ffn_w_down': ffn_w_down.astype(BF16),
        'moe_w_router_t': moe_w_router.transpose(0, 2, 1),
        'moe_w_gu': moe_w_gu.astype(BF16), 'moe_w_down': moe_w_down.astype(BF16),
    }

    tc_p = HGRN_ROWS // bp
    assert N_META == tc_p
    xm = to_time_major(x_prompt, meta_tokens.astype(x_prompt.dtype), tp_pad, tb_p // tc_p)
    zero = lambda ref, lead: jnp.zeros((lead, bp) + ref.shape[2:], ref.dtype)
    xp, p_new = _trunk(xm, tp, bp, tb_p, bp, zero(state_rglru_h, n_even), zero(state_rglru_conv, n_even),
                       zero(state_hgrn2, 1), zero(state_s5_re, n_odd), zero(state_s5_im, n_odd), w)
    nk_out = max(k for k in range(1, HGRN_ROWS // tc_p + 1) if tp0 % (k * tc_p) == 0)
    y_prompt = rmsnorm_batch_major(xp, norm_final, bp, N_META, tp0, tc_p, nk_out)

    xs = x_sample.transpose(1, 0, 2).reshape(ts * bs, d)
    xs, s_new = _trunk(xs, ts, bs, ts, nb_s, state_rglru_h, state_rglru_conv, state_hgrn2, state_s5_re, state_s5_im, w)
    y_sample = rmsnorm_rows(xs, norm_final, ts * bs).reshape(ts, bs, d).transpose(1, 0, 2)

    refs = (state_rglru_h, state_rglru_conv, state_hgrn2, state_s5_re, state_s5_im)
    cast = lambda new: tuple(a.astype(r.dtype) for a, r in zip(new, refs))
    return (y_prompt, y_sample) + cast(p_new) + cast(s_new)
```

```python
import functools

import jax
import jax.numpy as jnp
from jax import lax
from jax.experimental import pallas as pl
from jax.experimental.pallas import tpu as pltpu

F32 = jnp.float32
BF16 = jnp.bfloat16

EPS = 1e-6
N_META = 16
CONV_W = 4
RG_C = 8.0
H_A = 8
S5_P = 64

V7X_VMEM_LIMIT_BYTES = 56 * 1024 * 1024
SUBLANES = 8
LANES = 128
V7X_MXU_DIM = 256
MOE_TAIL_ROWS = (128, 192, V7X_MXU_DIM)
HGRN_ROWS = 128
S5_MAX_SUB_ROWS = 384
HGRN_EXP_CLAMP = 80.0
PROMPT_BLOCK_ROWS = 768
SCAN_CARRY_ELEMS = 16 * SUBLANES * LANES


def _params(*sem):
    return pltpu.CompilerParams(dimension_semantics=sem, vmem_limit_bytes=V7X_VMEM_LIMIT_BYTES)


def _rms(x, g):
    ms = jnp.mean(x * x, axis=-1, keepdims=True)
    return x * lax.rsqrt(ms + EPS) * g


def _dot(a, b):
    return jnp.dot(a, b, preferred_element_type=F32)


def _full(shape):
    return pl.BlockSpec(shape, lambda *_: (0,) * len(shape))


def _norm_matmul_kernel(x_ref, g_ref, w_ref, o_ref, xn_ref):
    @pl.when(pl.program_id(1) == 0)
    def _():
        xn_ref[...] = _rms(x_ref[...], g_ref[...]).astype(BF16)

    o_ref[...] = _dot(xn_ref[...], w_ref[...]).astype(o_ref.dtype)


def norm_matmul(x, g, w, li, tm, tn):
    n, d = x.shape
    nout = w.shape[2]
    return pl.pallas_call(
        _norm_matmul_kernel,
        out_shape=jax.ShapeDtypeStruct((n, nout), BF16),
        grid=(n // tm, nout // tn),
        in_specs=[
            pl.BlockSpec((tm, d), lambda i, j: (i, 0)),
            _full((1, d)),
            pl.BlockSpec((None, d, tn), lambda i, j: (li, 0, j)),
        ],
        out_specs=pl.BlockSpec((tm, tn), lambda i, j: (i, j)),
        scratch_shapes=[pltpu.VMEM((tm, d), BF16)],
        compiler_params=_params("parallel", "arbitrary"),
        name="norm_matmul",
    )(x, g.reshape(1, d), w)


def _mix_ffn_kernel(ya_ref, ob_ref, x_ref, wo_ref, g_ref, wg_ref, wu_ref, wdn_ref, o_ref, xn_ref):
    j = pl.program_id(1)
    d_a = ya_ref.shape[1]

    @pl.when(j == 0)
    def _():
        x1 = x_ref[...] + _dot(ya_ref[...], wo_ref[:d_a, :]) + _dot(ob_ref[...], wo_ref[d_a:, :])
        xn_ref[...] = _rms(x1, g_ref[...]).astype(BF16)
        o_ref[...] = x1

    xn = xn_ref[...]
    h = (jax.nn.silu(_dot(xn, wg_ref[...])) * _dot(xn, wu_ref[...])).astype(BF16)
    o_ref[...] += _dot(h, wdn_ref[...])


def mix_ffn(ya, ob, x, w_out, g, w_gu, w_down, li, tm, chunk):
    n, d = x.shape
    dff = w_down.shape[1]
    assert dff % chunk == 0 and chunk % LANES == 0
    nj = dff // chunk
    rows = lambda a: pl.BlockSpec((tm, a.shape[1]), lambda i, j: (i, 0))
    return pl.pallas_call(
        _mix_ffn_kernel,
        out_shape=jax.ShapeDtypeStruct((n, d), F32),
        grid=(n // tm, nj),
        in_specs=[
            rows(ya), rows(ob), rows(x),
            pl.BlockSpec((None,) + w_out.shape[1:], lambda i, j: (li, 0, 0)),
            _full((1, d)),
            pl.BlockSpec((None, d, chunk), lambda i, j: (li, 0, j)),
            pl.BlockSpec((None, d, chunk), lambda i, j: (li, 0, j + nj)),
            pl.BlockSpec((None, chunk, d), lambda i, j: (li, j, 0)),
        ],
        out_specs=pl.BlockSpec((tm, d), lambda i, j: (i, 0)),
        scratch_shapes=[pltpu.VMEM((tm, d), BF16)],
        compiler_params=_params("parallel", "arbitrary"),
        name="mix_ffn",
    )(ya, ob, x, w_out, g.reshape(1, d), w_gu, w_gu, w_down)


def _glu_residual_kernel(z_ref, wv_ref, wg_ref, r_ref, o_ref):
    z = z_ref[...]
    val = _dot(z, wv_ref[...])
    gate = _dot(z, wg_ref[...])
    o_ref[...] = r_ref[...] + val * jax.nn.sigmoid(gate)


def glu_residual(z, w_glu, li, res, tm, tn):
    n, d = z.shape
    dout = w_glu.shape[2] // 2
    nj = dout // tn
    return pl.pallas_call(
        _glu_residual_kernel,
        out_shape=jax.ShapeDtypeStruct((n, dout), F32),
        grid=(n // tm, nj),
        in_specs=[
            pl.BlockSpec((tm, d), lambda i, j: (i, 0)),
            pl.BlockSpec((None, d, tn), lambda i, j: (li, 0, j)),
            pl.BlockSpec((None, d, tn), lambda i, j: (li, 0, j + nj)),
            pl.BlockSpec((tm, tn), lambda i, j: (i, j)),
        ],
        out_specs=pl.BlockSpec((tm, tn), lambda i, j: (i, j)),
        compiler_params=_params("parallel", "arbitrary"),
        name="glu_residual",
    )(z, w_glu, w_glu, res)


def _rmsnorm_kernel(x_ref, g_ref, o_ref):
    o_ref[...] = _rms(x_ref[...], g_ref[...])


def rmsnorm_rows(x, g, tm):
    n, d = x.shape
    return pl.pallas_call(
        _rmsnorm_kernel,
        out_shape=jax.ShapeDtypeStruct((n, d), F32),
        grid=(n // tm,),
        in_specs=[pl.BlockSpec((tm, d), lambda i: (i, 0)), _full((1, d))],
        out_specs=pl.BlockSpec((tm, d), lambda i: (i, 0)),
        compiler_params=_params("parallel"),
        name="final_rmsnorm",
    )(x, g.reshape(1, d))


def _rmsnorm_batch_major_kernel(*refs, nk, tc, nb):
    x_refs, g_ref, o_ref = refs[:nk], refs[nk], refs[nk + 1]
    d = o_ref.shape[-1]
    for k in range(nk):
        y = _rms(x_refs[k][...], g_ref[...])
        o_ref[:, k * tc:(k + 1) * tc, :] = jnp.swapaxes(y.reshape(tc, nb, d), 0, 1)


def rmsnorm_batch_major(x, g, nb, t_skip, t_out, tc, nk):
    n, d = x.shape
    rows = tc * nb
    assert t_skip % tc == 0 and t_out % (nk * tc) == 0
    specs = [pl.BlockSpec((rows, d), lambda j, k=k: (nk * j + t_skip // tc + k, 0)) for k in range(nk)]
    return pl.pallas_call(
        functools.partial(_rmsnorm_batch_major_kernel, nk=nk, tc=tc, nb=nb),
        out_shape=jax.ShapeDtypeStruct((nb, t_out, d), F32),
        grid=(t_out // (nk * tc),),
        in_specs=specs + [_full((1, d))],
        out_specs=pl.BlockSpec((nb, nk * tc, d), lambda j: (0, j, 0)),
        compiler_params=_params("parallel"),
        name="final_rmsnorm_batch_major",
    )(*([x] * nk), g.reshape(1, d))


def _to_time_major_kernel(*refs, nk, n_chunks):
    x_refs, lead_ref, o_ref = refs[:nk], refs[nk], refs[nk + 1]
    i = pl.program_id(0)
    nb, tc, d = x_refs[0].shape
    rows = tc * nb
    for k in range(nk):
        chunk = i * nk + k - 1
        val = jnp.swapaxes(x_refs[k][...], 0, 1).reshape(rows, d)
        if k == 0:
            lead = jnp.broadcast_to(lead_ref[...][:, None, :], (tc, nb, d)).reshape(rows, d)
            val = jnp.where(i == 0, lead, val)
        o_ref[k * rows:(k + 1) * rows, :] = jnp.where(chunk < n_chunks, val, 0.0)


def to_time_major(x, lead, t_pad, nk):
    nb, t, d = x.shape
    tc = lead.shape[0]
    assert t % tc == 0 and t_pad % (nk * tc) == 0
    n_chunks = t // tc
    specs = [pl.BlockSpec((nb, tc, d), lambda i, k=k: (0, jnp.clip(i * nk + k - 1, 0, n_chunks - 1), 0))
             for k in range(nk)]
    return pl.pallas_call(
        functools.partial(_to_time_major_kernel, nk=nk, n_chunks=n_chunks),
        out_shape=jax.ShapeDtypeStruct((t_pad * nb, d), F32),
        grid=(t_pad // (nk * tc),),
        in_specs=specs + [_full((tc, d))],
        out_specs=pl.BlockSpec((nk * tc * nb, d), lambda i: (i, 0)),
        compiler_params=_params("parallel"),
        name="to_time_major",
    )(*([x] * nk), lead)


def _rglru_kernel(xa_ref, ga_ref, conv0_ref, h0_ref, cw_ref, cb_ref, wg_ref, bg_ref, lam_ref,
                  ya_ref, hlast_ref, convnew_ref, xpad_ref, a_ref, u_ref, *, n_t, tb, rb):
    i = pl.program_id(0)
    rows = tb * rb
    tail = (CONV_W - 1) * rb
    c = xa_ref.shape[-1]
    t_valid = jnp.minimum(n_t - i * tb, tb)

    @pl.when(i == 0)
    def _():
        xpad_ref[0:tail, :] = conv0_ref[...]
        hlast_ref[...] = h0_ref[...]

    @pl.when(i > 0)
    def _():
        xpad_ref[0:tail, :] = xpad_ref[rows:rows + tail, :]

    xpad_ref[tail:tail + rows, :] = xa_ref[...].astype(F32)
    xc = cb_ref[...]
    for k in range(CONV_W):
        xc = xc + cw_ref[k:k + 1, :] * xpad_ref[k * rb:k * rb + rows, :]

    gates = _dot(xc.astype(BF16), wg_ref[...]) + bg_ref[...]
    r = jax.nn.sigmoid(gates[:, :c])
    ig = jax.nn.sigmoid(gates[:, c:])
    log_a = (-RG_C) * r * jax.nn.softplus(-lam_ref[...])
    a = jnp.exp(log_a)
    mult = jnp.sqrt(1.0 - a * a)
    a_ref[...] = a
    u_ref[...] = mult * ig * xc

    lc = min(c, max(LANES, SCAN_CARRY_ELEMS // rb // LANES * LANES))
    for c0 in range(0, c, lc):
        def body(t, h, c0=c0):
            sl = pl.ds(pl.multiple_of(t * rb, rb), rb)
            h = a_ref[sl, c0:c0 + lc] * h + u_ref[sl, c0:c0 + lc]
            u_ref[sl, c0:c0 + lc] = h
            return h

        hlast_ref[:, c0:c0 + lc] = lax.fori_loop(0, t_valid, body, hlast_ref[:, c0:c0 + lc])

    live = lax.broadcasted_iota(jnp.int32, (rows, c), 0) < t_valid * rb
    ya_ref[...] = jnp.where(live, u_ref[...] * jax.nn.gelu(ga_ref[...].astype(F32)), 0.0).astype(ya_ref.dtype)
    convnew_ref[...] = xpad_ref[pl.ds(pl.multiple_of(t_valid * rb, rb), tail), :]


def rglru(proj, n_t, conv0, h0, cw, cb, wg, bg, lam, tb, rb):
    c = h0.shape[1]
    n_rows = proj.shape[0]
    rows = tb * rb
    tail = (CONV_W - 1) * rb
    return pl.pallas_call(
        functools.partial(_rglru_kernel, n_t=n_t, tb=tb, rb=rb),
        out_shape=(
            jax.ShapeDtypeStruct((n_rows, c), BF16),
            jax.ShapeDtypeStruct((rb, c), F32),
            jax.ShapeDtypeStruct((tail, c), F32),
        ),
        grid=(n_rows // rows,),
        in_specs=[
            pl.BlockSpec((rows, c), lambda i: (i, 0)),
            pl.BlockSpec((rows, c), lambda i: (i, 1)),
            _full((tail, c)), _full((rb, c)), _full((CONV_W, c)), _full((1, c)),
            _full((c, 2 * c)), _full((1, 2 * c)), _full((1, c)),
        ],
        out_specs=(
            pl.BlockSpec((rows, c), lambda i: (i, 0)),
            _full((rb, c)),
            _full((tail, c)),
        ),
        scratch_shapes=[
            pltpu.VMEM((rows + tail, c), F32),
            pltpu.VMEM((rows, c), F32),
            pltpu.VMEM((rows, c), F32),
        ],
        compiler_params=_params("arbitrary"),
        name="rglru",
    )(proj, proj, conv0, h0, cw, cb.reshape(1, c), wg, bg.reshape(1, 2 * c), lam.reshape(1, c))


def _hgrn2_kernel(q_ref, f_ref, v_ref, gb_ref, s0_ref, lbraw_ref, gn_ref, *rest, layer, n_t, tb, nb, nh):
    ob_ref, snew_ref, st_ref = rest[-3:]
    i = pl.program_id(1)
    m = HGRN_ROWS
    tc = m // nb
    d = q_ref.shape[-1]
    dk = d // nh
    n_chunks = jnp.minimum(n_t - i * tb, tb) // tc

    @pl.when(i == 0)
    def _():
        for b in range(nb):
            for h in range(nh):
                st_ref[h, :, b * dk:(b + 1) * dk] = s0_ref[b, h].T

    @pl.when(n_chunks < tb // tc)
    def _():
        ob_ref[...] = jnp.zeros_like(ob_ref)

    p = jax.nn.softmax(lbraw_ref[...], axis=0)
    cum = p[0:1, :]
    for r in range(1, layer + 1):
        cum = cum + p[r:r + 1, :]
    lb = cum - p[0:1, :]
    log_lb = jnp.log(lb)
    log_1mlb = jnp.log1p(-lb)

    row = lax.broadcasted_iota(jnp.int32, (m, m), 0)
    col = lax.broadcasted_iota(jnp.int32, (m, m), 1)
    same_seq_causal = jnp.where(((row & (nb - 1)) == (col & (nb - 1))) & (col <= row), 1.0, 0.0)
    bid = lax.broadcasted_iota(jnp.int32, (m, dk), 0) & (nb - 1)

    def chunk(c, carry):
        def rows_of(ref):
            if len(ref.shape) == 2:
                return ref[pl.ds(pl.multiple_of(c * m, m), m), :]
            return ref[pl.ds(c * tc, tc)].reshape(m, d)

        q = jax.nn.silu(rows_of(q_ref).astype(F32))
        fr = rows_of(f_ref).astype(F32)
        v = rows_of(v_ref).astype(BF16)
        gb = rows_of(gb_ref).astype(F32)
        logf = jnp.logaddexp(log_lb, log_1mlb + jax.nn.log_sigmoid(fr))
        k = 1.0 - jnp.exp(logf)
        slabs = [logf[0:nb]]
        for t in range(1, tc):
            slabs.append(slabs[-1] + logf[t * nb:(t + 1) * nb])
        g = jnp.concatenate(slabs, axis=0)
        g_last = jnp.concatenate([slabs[-1]] * tc, axis=0)
        qt = (q * jnp.exp(g)).astype(BF16)
        kt = (k * jnp.exp(jnp.minimum(-g, HGRN_EXP_CLAMP))).astype(BF16)
        ks = (k * jnp.exp(g_last - g)).astype(BF16)
        dec = jnp.exp(slabs[-1])

        outs = []
        for h in range(nh):
            hs = slice(h * dk, (h + 1) * dk)
            qt_h, kt_h, ks_h, v_h = qt[:, hs], kt[:, hs], ks[:, hs], v[:, hs]
            att = lax.dot_general(qt_h, kt_h, (((1,), (1,)), ((), ())), preferred_element_type=F32)
            att = (att * same_seq_causal).astype(BF16)
            o = _dot(att, v_h)
            zero = jnp.zeros_like(qt_h)
            expand = lambda x: jnp.concatenate([jnp.where(bid == b, x, zero) for b in range(nb)], axis=1)
            st_h = st_ref[h]
            o = o + lax.dot_general(expand(qt_h), st_h.astype(BF16), (((1,), (1,)), ((), ())),
                                    preferred_element_type=F32)
            dst = lax.dot_general(v_h, expand(ks_h), (((0,), (0,)), ((), ())), preferred_element_type=F32)
            dec_row = jnp.concatenate([dec[b:b + 1, hs] for b in range(nb)], axis=1)
            st_ref[h] = st_h * dec_row + dst
            outs.append(_rms(o, gn_ref[:, hs]))
        ob = jnp.concatenate(outs, axis=1) * jax.nn.silu(gb)
        ob_ref[pl.ds(pl.multiple_of(c * m, m), m), :] = ob.astype(ob_ref.dtype)
        return carry

    lax.fori_loop(0, n_chunks, chunk, 0)

    @pl.when(i == pl.num_programs(1) - 1)
    def _():
        for b in range(nb):
            for h in range(nh):
                snew_ref[b, h] = st_ref[h, :, b * dk:(b + 1) * dk].T


def hgrn2(proj, s0, s0_layer, s_stack, n_layers, lb_raw, gnorm, layer, n_t, tb, nb):
    _, bsz, nh, dk, _ = s0.shape
    d = nh * dk
    off = proj.shape[-1] // d - 4
    nbb = bsz // nb
    n_tpad = proj.shape[0] // bsz
    nt = n_tpad // tb
    if nb == bsz:
        col = lambda k: pl.BlockSpec((tb * nb, d), lambda j, i, k=k: (i, k + off))
    else:
        proj = proj.reshape(n_tpad, bsz, proj.shape[1])
        col = lambda k: pl.BlockSpec((tb, nb, d), lambda j, i, k=k: (i, j, k + off))
    in_specs = [
        col(0), col(1), col(2), col(3),
        pl.BlockSpec((None, nb, nh, dk, dk), lambda j, i: (s0_layer, j, 0, 0, 0)),
        _full(lb_raw.shape),
        _full((1, d)),
    ]
    args = [proj, proj, proj, proj, s0, lb_raw, gnorm.reshape(1, d)]
    aliases = {}
    if s_stack is not None:
        in_specs.append(pl.BlockSpec(memory_space=pl.ANY))
        args.append(s_stack)
        aliases = {len(args) - 1: 1}
    return pl.pallas_call(
        functools.partial(_hgrn2_kernel, layer=layer, n_t=n_t, tb=tb, nb=nb, nh=nh),
        out_shape=(
            jax.ShapeDtypeStruct((n_tpad * bsz, d), BF16),
            jax.ShapeDtypeStruct((n_layers, bsz, nh, dk, dk), F32),
        ),
        grid=(nbb, nt),
        in_specs=in_specs,
        out_specs=(
            pl.BlockSpec((tb * nb, d), lambda j, i: (j * nt + i, 0)),
            pl.BlockSpec((None, nb, nh, dk, dk), lambda j, i: (layer, j, 0, 0, 0)),
        ),
        scratch_shapes=[pltpu.VMEM((nh, dk, nb * dk), F32)],
        input_output_aliases=aliases,
        compiler_params=_params("arbitrary", "arbitrary"),
        name="hgrn2",
    )(*args)


def _s5_prep_kernel(lr_ref, li_ref, ldt_ref, bre_ref, bim_ref, ar_ref, ai_ref, ore_ref, oim_ref):
    lr = lr_ref[...]
    li = li_ref[...]
    dt = jnp.exp(ldt_ref[...])
    mag = jnp.exp(lr * dt)
    ar = mag * jnp.cos(li * dt)
    ai = mag * jnp.sin(li * dt)
    den = lr * lr + li * li
    cr = ((ar - 1.0) * lr + ai * li) / den
    ci = (ai * lr - (ar - 1.0) * li) / den
    ar_ref[...] = ar
    ai_ref[...] = ai
    ore_ref[...] = cr * bre_ref[...] - ci * bim_ref[...]
    oim_ref[...] = cr * bim_ref[...] + ci * bre_ref[...]


def s5_prep(lam_re, lam_im, log_dt, b_re, b_im):
    g, p, c = b_re.shape
    ns = g * p
    colv = jax.ShapeDtypeStruct((ns, 1), F32)
    mat = jax.ShapeDtypeStruct((ns, c), F32)
    ldt = jnp.broadcast_to(log_dt[:, None], (g, p)).reshape(ns, 1)
    return pl.pallas_call(
        _s5_prep_kernel, out_shape=(colv, colv, mat, mat), name="s5_prep",
    )(lam_re.reshape(ns, 1), lam_im.reshape(ns, 1), ldt, b_re.reshape(ns, c), b_im.reshape(ns, c))


def _s5_kernel(x_ref, g_ref, wb_ref, wc_ref, ar_ref, ai_ref, d_ref, h0r_ref, h0i_ref,
               z_ref, hr_ref, hi_ref, *, n_t, tb, rb, m):
    i = pl.program_id(0)
    nch = wb_ref.shape[0]
    cw = wb_ref.shape[1]
    sw = wb_ref.shape[2] // 2
    tsb = m // rb
    nsl = rb // SUBLANES
    n_sub = jnp.minimum(n_t - i * tb, tb) // tsb

    @pl.when(i == 0)
    def _():
        hr_ref[...] = h0r_ref[...]
        hi_ref[...] = h0i_ref[...]

    @pl.when(n_sub < tb // tsb)
    def _():
        z_ref[...] = jnp.zeros_like(z_ref)

    def sub_block(j, carry):
        r0 = pl.multiple_of(j * m, m)
        xn = _rms(x_ref[pl.ds(r0, m), :], g_ref[...])
        xnb = xn.astype(BF16)
        drive = [_dot(xnb[:, c * cw:(c + 1) * cw], wb_ref[c]) for c in range(nch)]
        for c in range(nch):
            cs = slice(c * cw, (c + 1) * cw)
            ss = slice(c * sw, (c + 1) * sw)
            u = xn[:, cs]
            bu = drive[c]
            ar = jnp.broadcast_to(ar_ref[:, ss], (SUBLANES, sw))
            ai = jnp.broadcast_to(ai_ref[:, ss], (SUBLANES, sw))
            out_r = [None] * (tsb * nsl)
            out_i = [None] * (tsb * nsl)
            for s in range(nsl):
                srow = slice(s * SUBLANES, (s + 1) * SUBLANES)
                hr, hi = hr_ref[srow, ss], hi_ref[srow, ss]
                for t in range(tsb):
                    lo = t * rb + s * SUBLANES
                    hr, hi = (ar * hr - ai * hi + bu[lo:lo + SUBLANES, :sw],
                              ar * hi + ai * hr + bu[lo:lo + SUBLANES, sw:])
                    out_r[t * nsl + s] = hr
                    out_i[t * nsl + s] = hi
                hr_ref[srow, ss] = hr
                hi_ref[srow, ss] = hi
            hcat = jnp.concatenate([jnp.concatenate(out_r, axis=0).astype(BF16),
                                    jnp.concatenate(out_i, axis=0).astype(BF16)], axis=1)
            y = _dot(hcat, wc_ref[c]) + d_ref[:, cs] * u
            z_ref[pl.ds(r0, m), cs] = jax.nn.gelu(y).astype(z_ref.dtype)
        return carry

    lax.fori_loop(0, n_sub, sub_block, 0)


def s5(x, n_t, g, wb, wc, a_re, a_im, dskip, h0r, h0i, tb, rb):
    n_rows, d = x.shape
    ns = h0r.shape[1]
    rows = tb * rb
    m = max(k for k in range(rb, S5_MAX_SUB_ROWS + 1, rb)
            if rows % k == 0 and n_t % (k // rb) == 0 and k % (2 * SUBLANES) == 0)
    return pl.pallas_call(
        functools.partial(_s5_kernel, n_t=n_t, tb=tb, rb=rb, m=m),
        out_shape=(
            jax.ShapeDtypeStruct((n_rows, d), BF16),
            jax.ShapeDtypeStruct((rb, ns), F32),
            jax.ShapeDtypeStruct((rb, ns), F32),
        ),
        grid=(n_rows // rows,),
        in_specs=[
            pl.BlockSpec((rows, d), lambda i: (i, 0)),
            _full((1, d)), _full(wb.shape), _full(wc.shape),
            _full((1, ns)), _full((1, ns)), _full((1, d)),
            _full((rb, ns)), _full((rb, ns)),
        ],
        out_specs=(
            pl.BlockSpec((rows, d), lambda i: (i, 0)),
            _full((rb, ns)), _full((rb, ns)),
        ),
        compiler_params=_params("arbitrary"),
        name="s5",
    )(x, g.reshape(1, d), wb, wc, a_re, a_im, dskip.reshape(1, d), h0r, h0i)


def _s5_block_diag_in(b):
    g, p, c = b.shape
    gpc = V7X_MXU_DIM // c
    bt = b.transpose(0, 2, 1).reshape(g // gpc, gpc, c, p)
    out = jnp.einsum('ngcp,gh->ngchp', bt, jnp.eye(gpc, dtype=b.dtype))
    return out.reshape(g // gpc, gpc * c, gpc * p)


def _s5_block_diag_out(cm):
    g, c, p = cm.shape
    gpc = V7X_MXU_DIM // c
    ct = cm.transpose(0, 2, 1).reshape(g // gpc, gpc, p, c)
    out = jnp.einsum('ngpc,gh->ngphc', ct, jnp.eye(gpc, dtype=cm.dtype))
    return out.reshape(g // gpc, gpc * p, gpc * c)


def _moe_kernel(x_ref, g_ref, wrt_ref, wgu_ref, wdn_ref, o_ref, xn_ref, pos_ref, gate_ref, *, sub):
    e = pl.program_id(1)
    tm = x_ref.shape[0]
    ne = wrt_ref.shape[0]
    dff = wdn_ref.shape[1]

    @pl.when(e == 0)
    def _():
        x = x_ref[...]
        xn = _rms(x, g_ref[...])
        xn_ref[...] = xn.astype(BF16)
        logits = lax.dot_general(wrt_ref[...], xn, (((1,), (1,)), ((), ())),
                                 precision=lax.Precision.HIGHEST, preferred_element_type=F32)
        ex = jnp.exp(logits - jnp.max(logits, axis=0, keepdims=True))
        probs = ex / jnp.sum(ex, axis=0, keepdims=True)
        eid = lax.broadcasted_iota(jnp.int32, (ne, tm), 0).astype(F32)
        m1 = jnp.max(probs, axis=0, keepdims=True)
        i1 = jnp.min(jnp.where(probs == m1, eid, float(ne)), axis=0, keepdims=True)
        sel1 = eid == i1
        rest = jnp.where(sel1, -1.0, probs)
        m2 = jnp.max(rest, axis=0, keepdims=True)
        i2 = jnp.min(jnp.where(rest == m2, eid, float(ne)), axis=0, keepdims=True)
        sel2 = eid == i2
        den = m1 + m2
        gate_ref[...] = jnp.where(sel1, m1 / den, 0.0) + jnp.where(sel2, m2 / den, 0.0)
        chosen = jnp.where(sel1, 1.0, jnp.where(sel2, 1.0, 0.0))
        r = lax.broadcasted_iota(jnp.int32, (tm, tm), 0)
        c = lax.broadcasted_iota(jnp.int32, (tm, tm), 1)
        before = jnp.where(r < c, 1.0, 0.0).astype(BF16)
        rank = _dot(chosen.astype(BF16), before)
        pos_ref[...] = jnp.where(chosen > 0.0, rank, -1.0)
        o_ref[...] = x

    pos_e = pos_ref[pl.ds(e, 1), :]
    gate_e = gate_ref[pl.ds(e, 1), :]
    cnt = jnp.sum(jnp.where(pos_e >= 0.0, 1.0, 0.0)).astype(jnp.int32)

    def run_block(base, rows):
        slot = lax.broadcasted_iota(jnp.int32, (rows, tm), 0) + base
        hit = pos_e == slot.astype(F32)
        onehot = jnp.where(hit, 1.0, 0.0).astype(BF16)
        xs = _dot(onehot, xn_ref[...]).astype(BF16)
        gs = jnp.sum(jnp.where(hit, gate_e, 0.0), axis=1, keepdims=True)
        hgu = _dot(xs, wgu_ref[0])
        act = (jax.nn.silu(hgu[:, :dff]) * hgu[:, dff:]).astype(BF16)
        yb = (_dot(act, wdn_ref[0]) * gs).astype(BF16)
        o_ref[...] += lax.dot_general(onehot, yb, (((0,), (0,)), ((), ())), preferred_element_type=F32)

    def body(s, carry):
        run_block(s * sub, sub)
        return carry

    n_full = cnt // sub
    lax.fori_loop(0, n_full, body, 0)
    rem = cnt - n_full * sub
    lo = 0
    for rows in MOE_TAIL_ROWS:
        pl.when((rem > lo) & (rem <= rows))(functools.partial(run_block, n_full * sub, rows))
        lo = rows


def moe_residual(x, g, w_router_t, w_gu, w_down, li, tm, sub):
    n, d = x.shape
    _, ne, _, dff2 = w_gu.shape
    assert MOE_TAIL_ROWS[-1] == sub
    return pl.pallas_call(
        functools.partial(_moe_kernel, sub=sub),
        out_shape=jax.ShapeDtypeStruct((n, d), F32),
        grid=(n // tm, ne),
        in_specs=[
            pl.BlockSpec((tm, d), lambda i, e: (i, 0)),
            _full((1, d)),
            _full((ne, d)),
            pl.BlockSpec((None, 1, d, dff2), lambda i, e: (li, e, 0, 0)),
            pl.BlockSpec((None, 1, dff2 // 2, d), lambda i, e: (li, e, 0, 0)),
        ],
        out_specs=pl.BlockSpec((tm, d), lambda i, e: (i, 0)),
        scratch_shapes=[
            pltpu.VMEM((tm, d), BF16),
            pltpu.VMEM((ne, tm), F32),
            pltpu.VMEM((ne, tm), F32),
        ],
        compiler_params=_params("parallel", "arbitrary"),
        name="moe",
    )(x, g.reshape(1, d), w_router_t, w_gu, w_down)


def _trunk(x, n_t, rb, tb, nb, h0, conv0, s0, re0, im0, w):
    n, d = x.shape
    depth = w['norm_mix'].shape[0]
    d_a = h0.shape[-1]
    nh, dk = s0.shape[2], s0.shape[3]
    d_b = nh * dk
    ns = re0.shape[-2] * re0.shape[-1]
    tm = tb * rb
    n_even = (depth + 1) // 2
    new = {k: [] for k in ('h', 'conv', 're', 'im')}
    s_stack = None
    for l in range(depth):
        li = l // 2
        if l % 2 == 0:
            proj = norm_matmul(x, w['norm_mix'][l], w['even_w_in'], li, tm, w['even_w_in'].shape[2])
            conv_tm = conv0[li].transpose(1, 0, 2).reshape((CONV_W - 1) * rb, d_a)
            ya, h_new, conv_new = rglru(proj, n_t, conv_tm, h0[li], w['rglru_conv_w'][li], w['rglru_conv_b'][li],
                                        w['rglru_wg'][li], w['rglru_bg'][li], w['rglru_lambda'][li], tb=tb, rb=rb)
            new['h'].append(h_new)
            new['conv'].append(conv_new.reshape(CONV_W - 1, rb, d_a).transpose(1, 0, 2))
            ob, s_stack = hgrn2(proj, s0, li % s0.shape[0], s_stack, n_even,
                                w['hgrn2_lb_raw'], w['hgrn2_gnorm'][li], li, n_t=n_t, tb=tb, nb=nb)
            if nb != rb:
                ob = ob.reshape(rb // nb, n // rb, nb, d_b).transpose(1, 0, 2, 3).reshape(n, d_b)
            x = mix_ffn(ya, ob, x, w['even_w_out'], w['norm_ffn'][l], w['ffn_w_gu'], w['ffn_w_down'], li, tm,
                        w['ffn_w_down'].shape[1] // 2)
        else:
            z, re_new, im_new = s5(x, n_t, w['norm_mix'][l], w['s5_wb'][li], w['s5_wc'][li], w['s5_a_re'][li],
                                   w['s5_a_im'][li], w['s5_d'][li], re0[li].reshape(rb, ns), im0[li].reshape(rb, ns),
                                   tb=tb, rb=rb)
            new['re'].append(re_new.reshape(re0.shape[1:]))
            new['im'].append(im_new.reshape(im0.shape[1:]))
            x = glu_residual(z, w['s5_w_glu'], li, x, tm, w['s5_w_glu'].shape[2] // 2)
            x = moe_residual(x, w['norm_ffn'][l], w['moe_w_router_t'][li], w['moe_w_gu'], w['moe_w_down'], li,
                             tm, V7X_MXU_DIM)
    stack = lambda k: jnp.stack(new[k])
    return x, (stack('h'), stack('conv'), s_stack, stack('re'), stack('im'))


def kernel(x_prompt, x_sample, state_rglru_h, state_rglru_conv, state_hgrn2, state_s5_re, state_s5_im,
           meta_tokens, norm_mix, norm_ffn, norm_final, even_w_in, even_w_out,
           rglru_conv_w, rglru_conv_b, rglru_w_a, rglru_b_a, rglru_w_x, rglru_b_x, rglru_lambda,
           hgrn2_lb_raw, hgrn2_gnorm, s5_lam_re, s5_lam_im, s5_log_dt, s5_b_re, s5_b_im, s5_c_re, s5_c_im,
           s5_d, s5_w_glu, ffn_w_gu, ffn_w_down, moe_w_router, moe_w_gu, moe_w_down):
    bp, tp0, d = x_prompt.shape
    bs, ts, _ = x_sample.shape
    tp = tp0 + N_META
    d_a = state_rglru_h.shape[-1]
    n_even, n_odd = state_rglru_h.shape[0], state_s5_re.shape[0]
    assert bp == SUBLANES and bs % SUBLANES == 0
    tb_p = PROMPT_BLOCK_ROWS // bp
    tp_pad = -(-tp // tb_p) * tb_p
    nb_s = HGRN_ROWS // ts
    assert (tp % (HGRN_ROWS // bp) == 0 and tb_p % (HGRN_ROWS // bp) == 0 and bs % nb_s == 0
            and nb_s * ts == HGRN_ROWS and nb_s % SUBLANES == 0)

    eye_a = jnp.eye(H_A, dtype=F32)
    block_diag = lambda m: jnp.einsum('lhij,hg->lhigj', m, eye_a).reshape(n_even, d_a, d_a)
    prep = [s5_prep(s5_lam_re[i], s5_lam_im[i], s5_log_dt[i], s5_b_re[i], s5_b_im[i]) for i in range(n_odd)]
    ns = s5_lam_re.shape[1] * s5_lam_re.shape[2]
    bshape = s5_b_re.shape[1:]
    w = {
        'norm_mix': norm_mix, 'norm_ffn': norm_ffn, 'norm_final': norm_final,
        'even_w_in': even_w_in.astype(BF16), 'even_w_out': even_w_out.astype(BF16),
        'rglru_conv_w': rglru_conv_w, 'rglru_conv_b': rglru_conv_b,
        'rglru_wg': jnp.concatenate([block_diag(rglru_w_a), block_diag(rglru_w_x)], axis=2).astype(BF16),
        'rglru_bg': jnp.concatenate([rglru_b_a, rglru_b_x], axis=1),
        'rglru_lambda': rglru_lambda, 'hgrn2_lb_raw': hgrn2_lb_raw, 'hgrn2_gnorm': hgrn2_gnorm,
        's5_a_re': [p[0].reshape(1, ns) for p in prep], 's5_a_im': [p[1].reshape(1, ns) for p in prep],
        's5_wb': [jnp.concatenate([_s5_block_diag_in(p[2].reshape(bshape)), _s5_block_diag_in(p[3].reshape(bshape))],
                                  axis=2).astype(BF16) for p in prep],
        's5_wc': [jnp.concatenate([_s5_block_diag_out(s5_c_re[i]), -_s5_block_diag_out(s5_c_im[i])],
                                  axis=1).astype(BF16) for i in range(n_odd)],
        's5_d': s5_d, 's5_w_glu': s5_w_glu.astype(BF16),
        'ffn_w_gu': ffn_w_gu.astype(BF16), 'ffn_w_down': ffn_w_down.astype(BF16),
        'moe_w_router_t': moe_w_router.transpose(0, 2, 1),
        'moe_w_gu': moe_w_gu.astype(BF16), 'moe_w_down': moe_w_down.astype(BF16),
    }

    tc_p = HGRN_ROWS // bp
    assert N_META == tc_p
    xm = to_time_major(x_prompt, meta_tokens.astype(x_prompt.dtype), tp_pad, tb_p // tc_p)
    zero = lambda ref, lead: jnp.zeros((lead, bp) + ref.shape[2:], ref.dtype)
    xp, p_new = _trunk(xm, tp, bp, tb_p, bp, zero(state_rglru_h, n_even), zero(state_rglru_conv, n_even),
                       zero(state_hgrn2, 1), zero(state_s5_re, n_odd), zero(state_s5_im, n_odd), w)
    nk_out = max(k for k in range(1, HGRN_ROWS // tc_p + 1) if tp0 % (k * tc_p) == 0)
    y_prompt = rmsnorm_batch_major(xp, norm_final, bp, N_META, tp0, tc_p, nk_out)

    xs = x_sample.transpose(1, 0, 2).reshape(ts * bs, d)
    xs, s_new = _trunk(xs, ts, bs, ts, nb_s, state_rglru_h, state_rglru_conv, state_hgrn2, state_s5_re, state_s5_im, w)
    y_sample = rmsnorm_rows(xs, norm_final, ts * bs).reshape(ts, bs, d).transpose(1, 0, 2)

    refs = (state_rglru_h, state_rglru_conv, state_hgrn2, state_s5_re, state_s5_im)
    cast = lambda new: tuple(a.astype(r.dtype) for a, r in zip(new, refs))
    return (y_prompt, y_sample) + cast(p_new) + cast(s_new)
```

```python
import functools

import jax
import jax.numpy as jnp
from jax import lax
from jax.experimental import pallas as pl
from jax.experimental.pallas import tpu as pltpu

F32 = jnp.float32
BF16 = jnp.bfloat16

EPS = 1e-6
N_META = 16
CONV_W = 4
RG_C = 8.0
H_A = 8
S5_P = 64

V7X_VMEM_LIMIT_BYTES = 56 * 1024 * 1024
SUBLANES = 8
LANES = 128
V7X_MXU_DIM = 256
MOE_TAIL_ROWS = (128, 160, 192, 224, V7X_MXU_DIM)
HGRN_ROWS = 128
S5_MAX_SUB_ROWS = 384
HGRN_EXP_CLAMP = 80.0
PROMPT_BLOCK_ROWS = 768
SCAN_CARRY_ELEMS = 16 * SUBLANES * LANES


def _params(*sem):
    return pltpu.CompilerParams(dimension_semantics=sem, vmem_limit_bytes=V7X_VMEM_LIMIT_BYTES)


def _rms(x, g):
    ms = jnp.mean(x * x, axis=-1, keepdims=True)
    return x * lax.rsqrt(ms + EPS) * g


def _dot(a, b):
    return jnp.dot(a, b, preferred_element_type=F32)


def _full(shape):
    return pl.BlockSpec(shape, lambda *_: (0,) * len(shape))


def _norm_matmul_kernel(x_ref, g_ref, w_ref, o_ref, xn_ref):
    @pl.when(pl.program_id(1) == 0)
    def _():
        xn_ref[...] = _rms(x_ref[...], g_ref[...]).astype(BF16)

    o_ref[...] = _dot(xn_ref[...], w_ref[...]).astype(o_ref.dtype)


def norm_matmul(x, g, w, li, tm, tn):
    n, d = x.shape
    nout = w.shape[2]
    return pl.pallas_call(
        _norm_matmul_kernel,
        out_shape=jax.ShapeDtypeStruct((n, nout), BF16),
        grid=(n // tm, nout // tn),
        in_specs=[
            pl.BlockSpec((tm, d), lambda i, j: (i, 0)),
            _full((1, d)),
            pl.BlockSpec((None, d, tn), lambda i, j: (li, 0, j)),
        ],
        out_specs=pl.BlockSpec((tm, tn), lambda i, j: (i, j)),
        scratch_shapes=[pltpu.VMEM((tm, d), BF16)],
        compiler_params=_params("parallel", "arbitrary"),
        name="norm_matmul",
    )(x, g.reshape(1, d), w)


def _mix_ffn_kernel(ya_ref, ob_ref, x_ref, wo_ref, g_ref, wg_ref, wu_ref, wdn_ref, o_ref, xn_ref):
    j = pl.program_id(1)
    d_a = ya_ref.shape[1]

    @pl.when(j == 0)
    def _():
        x1 = x_ref[...] + _dot(ya_ref[...], wo_ref[:d_a, :]) + _dot(ob_ref[...], wo_ref[d_a:, :])
        xn_ref[...] = _rms(x1, g_ref[...]).astype(BF16)
        o_ref[...] = x1

    xn = xn_ref[...]
    h = (jax.nn.silu(_dot(xn, wg_ref[...])) * _dot(xn, wu_ref[...])).astype(BF16)
    o_ref[...] += _dot(h, wdn_ref[...])


def mix_ffn(ya, ob, x, w_out, g, w_gu, w_down, li, tm, chunk):
    n, d = x.shape
    dff = w_down.shape[1]
    assert dff % chunk == 0 and chunk % LANES == 0
    nj = dff // chunk
    rows = lambda a: pl.BlockSpec((tm, a.shape[1]), lambda i, j: (i, 0))
    return pl.pallas_call(
        _mix_ffn_kernel,
        out_shape=jax.ShapeDtypeStruct((n, d), F32),
        grid=(n // tm, nj),
        in_specs=[
            rows(ya), rows(ob), rows(x),
            pl.BlockSpec((None,) + w_out.shape[1:], lambda i, j: (li, 0, 0)),
            _full((1, d)),
            pl.BlockSpec((None, d, chunk), lambda i, j: (li, 0, j)),
            pl.BlockSpec((None, d, chunk), lambda i, j: (li, 0, j + nj)),
            pl.BlockSpec((None, chunk, d), lambda i, j: (li, j, 0)),
        ],
        out_specs=pl.BlockSpec((tm, d), lambda i, j: (i, 0)),
        scratch_shapes=[pltpu.VMEM((tm, d), BF16)],
        compiler_params=_params("parallel", "arbitrary"),
        name="mix_ffn",
    )(ya, ob, x, w_out, g.reshape(1, d), w_gu, w_gu, w_down)


def _glu_residual_kernel(z_ref, wv_ref, wg_ref, r_ref, o_ref):
    z = z_ref[...]
    val = _dot(z, wv_ref[...])
    gate = _dot(z, wg_ref[...])
    o_ref[...] = r_ref[...] + val * jax.nn.sigmoid(gate)


def glu_residual(z, w_glu, li, res, tm, tn):
    n, d = z.shape
    dout = w_glu.shape[2] // 2
    nj = dout // tn
    return pl.pallas_call(
        _glu_residual_kernel,
        out_shape=jax.ShapeDtypeStruct((n, dout), F32),
        grid=(n // tm, nj),
        in_specs=[
            pl.BlockSpec((tm, d), lambda i, j: (i, 0)),
            pl.BlockSpec((None, d, tn), lambda i, j: (li, 0, j)),
            pl.BlockSpec((None, d, tn), lambda i, j: (li, 0, j + nj)),
            pl.BlockSpec((tm, tn), lambda i, j: (i, j)),
        ],
        out_specs=pl.BlockSpec((tm, tn), lambda i, j: (i, j)),
        compiler_params=_params("parallel", "arbitrary"),
        name="glu_residual",
    )(z, w_glu, w_glu, res)


def _rmsnorm_kernel(x_ref, g_ref, o_ref):
    o_ref[...] = _rms(x_ref[...], g_ref[...])


def rmsnorm_rows(x, g, tm):
    n, d = x.shape
    return pl.pallas_call(
        _rmsnorm_kernel,
        out_shape=jax.ShapeDtypeStruct((n, d), F32),
        grid=(n // tm,),
        in_specs=[pl.BlockSpec((tm, d), lambda i: (i, 0)), _full((1, d))],
        out_specs=pl.BlockSpec((tm, d), lambda i: (i, 0)),
        compiler_params=_params("parallel"),
        name="final_rmsnorm",
    )(x, g.reshape(1, d))


def _rmsnorm_batch_major_kernel(*refs, nk, tc, nb):
    x_refs, g_ref, o_ref = refs[:nk], refs[nk], refs[nk + 1]
    d = o_ref.shape[-1]
    for k in range(nk):
        y = _rms(x_refs[k][...], g_ref[...])
        o_ref[:, k * tc:(k + 1) * tc, :] = jnp.swapaxes(y.reshape(tc, nb, d), 0, 1)


def rmsnorm_batch_major(x, g, nb, t_skip, t_out, tc, nk):
    n, d = x.shape
    rows = tc * nb
    assert t_skip % tc == 0 and t_out % (nk * tc) == 0
    specs = [pl.BlockSpec((rows, d), lambda j, k=k: (nk * j + t_skip // tc + k, 0)) for k in range(nk)]
    return pl.pallas_call(
        functools.partial(_rmsnorm_batch_major_kernel, nk=nk, tc=tc, nb=nb),
        out_shape=jax.ShapeDtypeStruct((nb, t_out, d), F32),
        grid=(t_out // (nk * tc),),
        in_specs=specs + [_full((1, d))],
        out_specs=pl.BlockSpec((nb, nk * tc, d), lambda j: (0, j, 0)),
        compiler_params=_params("parallel"),
        name="final_rmsnorm_batch_major",
    )(*([x] * nk), g.reshape(1, d))


def _to_time_major_kernel(*refs, nk, n_chunks):
    x_refs, lead_ref, o_ref = refs[:nk], refs[nk], refs[nk + 1]
    i = pl.program_id(0)
    nb, tc, d = x_refs[0].shape
    rows = tc * nb
    for k in range(nk):
        chunk = i * nk + k - 1
        val = jnp.swapaxes(x_refs[k][...], 0, 1).reshape(rows, d)
        if k == 0:
            lead = jnp.broadcast_to(lead_ref[...][:, None, :], (tc, nb, d)).reshape(rows, d)
            val = jnp.where(i == 0, lead, val)
        o_ref[k * rows:(k + 1) * rows, :] = jnp.where(chunk < n_chunks, val, 0.0)


def to_time_major(x, lead, t_pad, nk):
    nb, t, d = x.shape
    tc = lead.shape[0]
    assert t % tc == 0 and t_pad % (nk * tc) == 0
    n_chunks = t // tc
    specs = [pl.BlockSpec((nb, tc, d), lambda i, k=k: (0, jnp.clip(i * nk + k - 1, 0, n_chunks - 1), 0))
             for k in range(nk)]
    return pl.pallas_call(
        functools.partial(_to_time_major_kernel, nk=nk, n_chunks=n_chunks),
        out_shape=jax.ShapeDtypeStruct((t_pad * nb, d), F32),
        grid=(t_pad // (nk * tc),),
        in_specs=specs + [_full((tc, d))],
        out_specs=pl.BlockSpec((nk * tc * nb, d), lambda i: (i, 0)),
        compiler_params=_params("parallel"),
        name="to_time_major",
    )(*([x] * nk), lead)


def _rglru_kernel(xa_ref, ga_ref, conv0_ref, h0_ref, cw_ref, cb_ref, wg_ref, bg_ref, lam_ref,
                  ya_ref, hlast_ref, convnew_ref, xpad_ref, a_ref, u_ref, *, n_t, tb, rb):
    i = pl.program_id(0)
    rows = tb * rb
    tail = (CONV_W - 1) * rb
    c = xa_ref.shape[-1]
    t_valid = jnp.minimum(n_t - i * tb, tb)

    @pl.when(i == 0)
    def _():
        xpad_ref[0:tail, :] = conv0_ref[...]
        hlast_ref[...] = h0_ref[...]

    @pl.when(i > 0)
    def _():
        xpad_ref[0:tail, :] = xpad_ref[rows:rows + tail, :]

    xpad_ref[tail:tail + rows, :] = xa_ref[...].astype(F32)
    xc = cb_ref[...]
    for k in range(CONV_W):
        xc = xc + cw_ref[k:k + 1, :] * xpad_ref[k * rb:k * rb + rows, :]

    gates = _dot(xc.astype(BF16), wg_ref[...]) + bg_ref[...]
    r = jax.nn.sigmoid(gates[:, :c])
    ig = jax.nn.sigmoid(gates[:, c:])
    log_a = (-RG_C) * r * jax.nn.softplus(-lam_ref[...])
    a = jnp.exp(log_a)
    mult = jnp.sqrt(1.0 - a * a)
    a_ref[...] = a
    u_ref[...] = mult * ig * xc

    lc = min(c, max(LANES, SCAN_CARRY_ELEMS // rb // LANES * LANES))
    for c0 in range(0, c, lc):
        def body(t, h, c0=c0):
            sl = pl.ds(pl.multiple_of(t * rb, rb), rb)
            h = a_ref[sl, c0:c0 + lc] * h + u_ref[sl, c0:c0 + lc]
            u_ref[sl, c0:c0 + lc] = h
            return h

        hlast_ref[:, c0:c0 + lc] = lax.fori_loop(0, t_valid, body, hlast_ref[:, c0:c0 + lc])

    live = lax.broadcasted_iota(jnp.int32, (rows, c), 0) < t_valid * rb
    ya_ref[...] = jnp.where(live, u_ref[...] * jax.nn.gelu(ga_ref[...].astype(F32)), 0.0).astype(ya_ref.dtype)
    convnew_ref[...] = xpad_ref[pl.ds(pl.multiple_of(t_valid * rb, rb), tail), :]


def rglru(proj, n_t, conv0, h0, cw, cb, wg, bg, lam, tb, rb):
    c = h0.shape[1]
    n_rows = proj.shape[0]
    rows = tb * rb
    tail = (CONV_W - 1) * rb
    return pl.pallas_call(
        functools.partial(_rglru_kernel, n_t=n_t, tb=tb, rb=rb),
        out_shape=(
            jax.ShapeDtypeStruct((n_rows, c), BF16),
            jax.ShapeDtypeStruct((rb, c), F32),
            jax.ShapeDtypeStruct((tail, c), F32),
        ),
        grid=(n_rows // rows,),
        in_specs=[
            pl.BlockSpec((rows, c), lambda i: (i, 0)),
            pl.BlockSpec((rows, c), lambda i: (i, 1)),
            _full((tail, c)), _full((rb, c)), _full((CONV_W, c)), _full((1, c)),
            _full((c, 2 * c)), _full((1, 2 * c)), _full((1, c)),
        ],
        out_specs=(
            pl.BlockSpec((rows, c), lambda i: (i, 0)),
            _full((rb, c)),
            _full((tail, c)),
        ),
        scratch_shapes=[
            pltpu.VMEM((rows + tail, c), F32),
            pltpu.VMEM((rows, c), F32),
            pltpu.VMEM((rows, c), F32),
        ],
        compiler_params=_params("arbitrary"),
        name="rglru",
    )(proj, proj, conv0, h0, cw, cb.reshape(1, c), wg, bg.reshape(1, 2 * c), lam.reshape(1, c))


def _hgrn2_kernel(q_ref, f_ref, v_ref, gb_ref, s0_ref, lbraw_ref, gn_ref, *rest, layer, n_t, tb, nb, nh):
    ob_ref, snew_ref, st_ref = rest[-3:]
    i = pl.program_id(1)
    m = HGRN_ROWS
    tc = m // nb
    d = q_ref.shape[-1]
    dk = d // nh
    n_chunks = jnp.minimum(n_t - i * tb, tb) // tc

    @pl.when(i == 0)
    def _():
        for b in range(nb):
            for h in range(nh):
                st_ref[h, :, b * dk:(b + 1) * dk] = s0_ref[b, h].T

    @pl.when(n_chunks < tb // tc)
    def _():
        ob_ref[...] = jnp.zeros_like(ob_ref)

    p = jax.nn.softmax(lbraw_ref[...], axis=0)
    cum = p[0:1, :]
    for r in range(1, layer + 1):
        cum = cum + p[r:r + 1, :]
    lb = cum - p[0:1, :]
    log_lb = jnp.log(lb)
    log_1mlb = jnp.log1p(-lb)

    row = lax.broadcasted_iota(jnp.int32, (m, m), 0)
    col = lax.broadcasted_iota(jnp.int32, (m, m), 1)
    same_seq_causal = jnp.where(((row & (nb - 1)) == (col & (nb - 1))) & (col <= row), 1.0, 0.0)
    bid = lax.broadcasted_iota(jnp.int32, (m, dk), 0) & (nb - 1)

    def chunk(c, carry):
        def rows_of(ref):
            if len(ref.shape) == 2:
                return ref[pl.ds(pl.multiple_of(c * m, m), m), :]
            return ref[pl.ds(c * tc, tc)].reshape(m, d)

        q = jax.nn.silu(rows_of(q_ref).astype(F32))
        fr = rows_of(f_ref).astype(F32)
        v = rows_of(v_ref).astype(BF16)
        gb = rows_of(gb_ref).astype(F32)
        logf = jnp.logaddexp(log_lb, log_1mlb + jax.nn.log_sigmoid(fr))
        k = 1.0 - jnp.exp(logf)
        slabs = [logf[0:nb]]
        for t in range(1, tc):
            slabs.append(slabs[-1] + logf[t * nb:(t + 1) * nb])
        g = jnp.concatenate(slabs, axis=0)
        g_last = jnp.concatenate([slabs[-1]] * tc, axis=0)
        qt = (q * jnp.exp(g)).astype(BF16)
        kt = (k * jnp.exp(jnp.minimum(-g, HGRN_EXP_CLAMP))).astype(BF16)
        ks = (k * jnp.exp(g_last - g)).astype(BF16)
        dec = jnp.exp(slabs[-1])

        outs = []
        for h in range(nh):
            hs = slice(h * dk, (h + 1) * dk)
            qt_h, kt_h, ks_h, v_h = qt[:, hs], kt[:, hs], ks[:, hs], v[:, hs]
            att = lax.dot_general(qt_h, kt_h, (((1,), (1,)), ((), ())), preferred_element_type=F32)
            att = (att * same_seq_causal).astype(BF16)
            o = _dot(att, v_h)
            zero = jnp.zeros_like(qt_h)
            expand = lambda x: jnp.concatenate([jnp.where(bid == b, x, zero) for b in range(nb)], axis=1)
            st_h = st_ref[h]
            o = o + lax.dot_general(expand(qt_h), st_h.astype(BF16), (((1,), (1,)), ((), ())),
                                    preferred_element_type=F32)
            dst = lax.dot_general(v_h, expand(ks_h), (((0,), (0,)), ((), ())), preferred_element_type=F32)
            dec_row = jnp.concatenate([dec[b:b + 1, hs] for b in range(nb)], axis=1)
            st_ref[h] = st_h * dec_row + dst
            outs.append(_rms(o, gn_ref[:, hs]))
        ob = jnp.concatenate(outs, axis=1) * jax.nn.silu(gb)
        ob_ref[pl.ds(pl.multiple_of(c * m, m), m), :] = ob.astype(ob_ref.dtype)
        return carry

    lax.fori_loop(0, n_chunks, chunk, 0)

    @pl.when(i == pl.num_programs(1) - 1)
    def _():
        for b in range(nb):
            for h in range(nh):
                snew_ref[b, h] = st_ref[h, :, b * dk:(b + 1) * dk].T


def hgrn2(proj, s0, s0_layer, s_stack, n_layers, lb_raw, gnorm, layer, n_t, tb, nb):
    _, bsz, nh, dk, _ = s0.shape
    d = nh * dk
    off = proj.shape[-1] // d - 4
    nbb = bsz // nb
    n_tpad = proj.shape[0] // bsz
    nt = n_tpad // tb
    if nb == bsz:
        col = lambda k: pl.BlockSpec((tb * nb, d), lambda j, i, k=k: (i, k + off))
    else:
        proj = proj.reshape(n_tpad, bsz, proj.shape[1])
        col = lambda k: pl.BlockSpec((tb, nb, d), lambda j, i, k=k: (i, j, k + off))
    in_specs = [
        col(0), col(1), col(2), col(3),
        pl.BlockSpec((None, nb, nh, dk, dk), lambda j, i: (s0_layer, j, 0, 0, 0)),
        _full(lb_raw.shape),
        _full((1, d)),
    ]
    args = [proj, proj, proj, proj, s0, lb_raw, gnorm.reshape(1, d)]
    aliases = {}
    if s_stack is not None:
        in_specs.append(pl.BlockSpec(memory_space=pl.ANY))
        args.append(s_stack)
        aliases = {len(args) - 1: 1}
    return pl.pallas_call(
        functools.partial(_hgrn2_kernel, layer=layer, n_t=n_t, tb=tb, nb=nb, nh=nh),
        out_shape=(
            jax.ShapeDtypeStruct((n_tpad * bsz, d), BF16),
            jax.ShapeDtypeStruct((n_layers, bsz, nh, dk, dk), F32),
        ),
        grid=(nbb, nt),
        in_specs=in_specs,
        out_specs=(
            pl.BlockSpec((tb * nb, d), lambda j, i: (j * nt + i, 0)),
            pl.BlockSpec((None, nb, nh, dk, dk), lambda j, i: (layer, j, 0, 0, 0)),
        ),
        scratch_shapes=[pltpu.VMEM((nh, dk, nb * dk), F32)],
        input_output_aliases=aliases,
        compiler_params=_params("arbitrary", "arbitrary"),
        name="hgrn2",
    )(*args)


def _s5_prep_kernel(lr_ref, li_ref, ldt_ref, bre_ref, bim_ref, ar_ref, ai_ref, ore_ref, oim_ref):
    lr = lr_ref[...]
    li = li_ref[...]
    dt = jnp.exp(ldt_ref[...])
    mag = jnp.exp(lr * dt)
    ar = mag * jnp.cos(li * dt)
    ai = mag * jnp.sin(li * dt)
    den = lr * lr + li * li
    cr = ((ar - 1.0) * lr + ai * li) / den
    ci = (ai * lr - (ar - 1.0) * li) / den
    ar_ref[...] = ar
    ai_ref[...] = ai
    ore_ref[...] = cr * bre_ref[...] - ci * bim_ref[...]
    oim_ref[...] = cr * bim_ref[...] + ci * bre_ref[...]


def s5_prep(lam_re, lam_im, log_dt, b_re, b_im):
    g, p, c = b_re.shape
    ns = g * p
    colv = jax.ShapeDtypeStruct((ns, 1), F32)
    mat = jax.ShapeDtypeStruct((ns, c), F32)
    ldt = jnp.broadcast_to(log_dt[:, None], (g, p)).reshape(ns, 1)
    return pl.pallas_call(
        _s5_prep_kernel, out_shape=(colv, colv, mat, mat), name="s5_prep",
    )(lam_re.reshape(ns, 1), lam_im.reshape(ns, 1), ldt, b_re.reshape(ns, c), b_im.reshape(ns, c))


def _s5_kernel(x_ref, g_ref, wb_ref, wc_ref, ar_ref, ai_ref, d_ref, h0r_ref, h0i_ref,
               z_ref, hr_ref, hi_ref, *, n_t, tb, rb, m):
    i = pl.program_id(0)
    nch = wb_ref.shape[0]
    cw = wb_ref.shape[1]
    sw = wb_ref.shape[2] // 2
    tsb = m // rb
    nsl = rb // SUBLANES
    n_sub = jnp.minimum(n_t - i * tb, tb) // tsb

    @pl.when(i == 0)
    def _():
        hr_ref[...] = h0r_ref[...]
        hi_ref[...] = h0i_ref[...]

    @pl.when(n_sub < tb // tsb)
    def _():
        z_ref[...] = jnp.zeros_like(z_ref)

    def sub_block(j, carry):
        r0 = pl.multiple_of(j * m, m)
        xn = _rms(x_ref[pl.ds(r0, m), :], g_ref[...])
        xnb = xn.astype(BF16)
        drive = [_dot(xnb[:, c * cw:(c + 1) * cw], wb_ref[c]) for c in range(nch)]
        for c in range(nch):
            cs = slice(c * cw, (c + 1) * cw)
            ss = slice(c * sw, (c + 1) * sw)
            u = xn[:, cs]
            bu = drive[c]
            ar = jnp.broadcast_to(ar_ref[:, ss], (SUBLANES, sw))
            ai = jnp.broadcast_to(ai_ref[:, ss], (SUBLANES, sw))
            out_r = [None] * (tsb * nsl)
            out_i = [None] * (tsb * nsl)
            for s in range(nsl):
                srow = slice(s * SUBLANES, (s + 1) * SUBLANES)
                hr, hi = hr_ref[srow, ss], hi_ref[srow, ss]
                for t in range(tsb):
                    lo = t * rb + s * SUBLANES
                    hr, hi = (ar * hr - ai * hi + bu[lo:lo + SUBLANES, :sw],
                              ar * hi + ai * hr + bu[lo:lo + SUBLANES, sw:])
                    out_r[t * nsl + s] = hr
                    out_i[t * nsl + s] = hi
                hr_ref[srow, ss] = hr
                hi_ref[srow, ss] = hi
            hcat = jnp.concatenate([jnp.concatenate(out_r, axis=0).astype(BF16),
                                    jnp.concatenate(out_i, axis=0).astype(BF16)], axis=1)
            y = _dot(hcat, wc_ref[c]) + d_ref[:, cs] * u
            z_ref[pl.ds(r0, m), cs] = jax.nn.gelu(y).astype(z_ref.dtype)
        return carry

    lax.fori_loop(0, n_sub, sub_block, 0)


def s5(x, n_t, g, wb, wc, a_re, a_im, dskip, h0r, h0i, tb, rb):
    n_rows, d = x.shape
    ns = h0r.shape[1]
    rows = tb * rb
    m = max(k for k in range(rb, S5_MAX_SUB_ROWS + 1, rb)
            if rows % k == 0 and n_t % (k // rb) == 0 and k % (2 * SUBLANES) == 0)
    return pl.pallas_call(
        functools.partial(_s5_kernel, n_t=n_t, tb=tb, rb=rb, m=m),
        out_shape=(
            jax.ShapeDtypeStruct((n_rows, d), BF16),
            jax.ShapeDtypeStruct((rb, ns), F32),
            jax.ShapeDtypeStruct((rb, ns), F32),
        ),
        grid=(n_rows // rows,),
        in_specs=[
            pl.BlockSpec((rows, d), lambda i: (i, 0)),
            _full((1, d)), _full(wb.shape), _full(wc.shape),
            _full((1, ns)), _full((1, ns)), _full((1, d)),
            _full((rb, ns)), _full((rb, ns)),
        ],
        out_specs=(
            pl.BlockSpec((rows, d), lambda i: (i, 0)),
            _full((rb, ns)), _full((rb, ns)),
        ),
        compiler_params=_params("arbitrary"),
        name="s5",
    )(x, g.reshape(1, d), wb, wc, a_re, a_im, dskip.reshape(1, d), h0r, h0i)


def _s5_block_diag_in(b):
    g, p, c = b.shape
    gpc = V7X_MXU_DIM // c
    bt = b.transpose(0, 2, 1).reshape(g // gpc, gpc, c, p)
    out = jnp.einsum('ngcp,gh->ngchp', bt, jnp.eye(gpc, dtype=b.dtype))
    return out.reshape(g // gpc, gpc * c, gpc * p)


def _s5_block_diag_out(cm):
    g, c, p = cm.shape
    gpc = V7X_MXU_DIM // c
    ct = cm.transpose(0, 2, 1).reshape(g // gpc, gpc, p, c)
    out = jnp.einsum('ngpc,gh->ngphc', ct, jnp.eye(gpc, dtype=cm.dtype))
    return out.reshape(g // gpc, gpc * p, gpc * c)


def _moe_kernel(x_ref, g_ref, wrt_ref, wgu_ref, wdn_ref, o_ref, xn_ref, pos_ref, gate_ref, *, sub):
    e = pl.program_id(1)
    tm = x_ref.shape[0]
    ne = wrt_ref.shape[0]
    dff = wdn_ref.shape[1]

    @pl.when(e == 0)
    def _():
        x = x_ref[...]
        xn = _rms(x, g_ref[...])
        xn_ref[...] = xn.astype(BF16)
        logits = lax.dot_general(wrt_ref[...], xn, (((1,), (1,)), ((), ())),
                                 precision=lax.Precision.HIGHEST, preferred_element_type=F32)
        ex = jnp.exp(logits - jnp.max(logits, axis=0, keepdims=True))
        probs = ex / jnp.sum(ex, axis=0, keepdims=True)
        eid = lax.broadcasted_iota(jnp.int32, (ne, tm), 0).astype(F32)
        m1 = jnp.max(probs, axis=0, keepdims=True)
        i1 = jnp.min(jnp.where(probs == m1, eid, float(ne)), axis=0, keepdims=True)
        sel1 = eid == i1
        rest = jnp.where(sel1, -1.0, probs)
        m2 = jnp.max(rest, axis=0, keepdims=True)
        i2 = jnp.min(jnp.where(rest == m2, eid, float(ne)), axis=0, keepdims=True)
        sel2 = eid == i2
        den = m1 + m2
        gate_ref[...] = jnp.where(sel1, m1 / den, 0.0) + jnp.where(sel2, m2 / den, 0.0)
        chosen = jnp.where(sel1, 1.0, jnp.where(sel2, 1.0, 0.0))
        r = lax.broadcasted_iota(jnp.int32, (tm, tm), 0)
        c = lax.broadcasted_iota(jnp.int32, (tm, tm), 1)
        before = jnp.where(r < c, 1.0, 0.0).astype(BF16)
        rank = _dot(chosen.astype(BF16), before)
        pos_ref[...] = jnp.where(chosen > 0.0, rank, -1.0)
        o_ref[...] = x

    pos_e = pos_ref[pl.ds(e, 1), :]
    gate_e = gate_ref[pl.ds(e, 1), :]
    cnt = jnp.sum(jnp.where(pos_e >= 0.0, 1.0, 0.0)).astype(jnp.int32)

    def run_block(base, rows):
        slot = lax.broadcasted_iota(jnp.int32, (rows, tm), 0) + base
        hit = pos_e == slot.astype(F32)
        onehot = jnp.where(hit, 1.0, 0.0).astype(BF16)
        xs = _dot(onehot, xn_ref[...]).astype(BF16)
        gs = jnp.sum(jnp.where(hit, gate_e, 0.0), axis=1, keepdims=True)
        hgu = _dot(xs, wgu_ref[0])
        act = (jax.nn.silu(hgu[:, :dff]) * hgu[:, dff:]).astype(BF16)
        yb = (_dot(act, wdn_ref[0]) * gs).astype(BF16)
        o_ref[...] += lax.dot_general(onehot, yb, (((0,), (0,)), ((), ())), preferred_element_type=F32)

    def body(s, carry):
        run_block(s * sub, sub)
        return carry

    n_full = cnt // sub
    lax.fori_loop(0, n_full, body, 0)
    rem = cnt - n_full * sub
    lo = 0
    for rows in MOE_TAIL_ROWS:
        pl.when((rem > lo) & (rem <= rows))(functools.partial(run_block, n_full * sub, rows))
        lo = rows


def moe_residual(x, g, w_router_t, w_gu, w_down, li, tm, sub):
    n, d = x.shape
    _, ne, _, dff2 = w_gu.shape
    assert MOE_TAIL_ROWS[-1] == sub
    return pl.pallas_call(
        functools.partial(_moe_kernel, sub=sub),
        out_shape=jax.ShapeDtypeStruct((n, d), F32),
        grid=(n // tm, ne),
        in_specs=[
            pl.BlockSpec((tm, d), lambda i, e: (i, 0)),
            _full((1, d)),
            _full((ne, d)),
            pl.BlockSpec((None, 1, d, dff2), lambda i, e: (li, e, 0, 0)),
            pl.BlockSpec((None, 1, dff2 // 2, d), lambda i, e: (li, e, 0, 0)),
        ],
        out_specs=pl.BlockSpec((tm, d), lambda i, e: (i, 0)),
        scratch_shapes=[
            pltpu.VMEM((tm, d), BF16),
            pltpu.VMEM((ne, tm), F32),
            pltpu.VMEM((ne, tm), F32),
        ],
        compiler_params=_params("parallel", "arbitrary"),
        name="moe",
    )(x, g.reshape(1, d), w_router_t, w_gu, w_down)


def _trunk(x, n_t, rb, tb, nb, h0, conv0, s0, re0, im0, w):
    n, d = x.shape
    depth = w['norm_mix'].shape[0]
    d_a = h0.shape[-1]
    nh, dk = s0.shape[2], s0.shape[3]
    d_b = nh * dk
    ns = re0.shape[-2] * re0.shape[-1]
    tm = tb * rb
    n_even = (depth + 1) // 2
    new = {k: [] for k in ('h', 'conv', 're', 'im')}
    s_stack = None
    for l in range(depth):
        li = l // 2
        if l % 2 == 0:
            proj = norm_matmul(x, w['norm_mix'][l], w['even_w_in'], li, tm, w['even_w_in'].shape[2])
            conv_tm = conv0[li].transpose(1, 0, 2).reshape((CONV_W - 1) * rb, d_a)
            ya, h_new, conv_new = rglru(proj, n_t, conv_tm, h0[li], w['rglru_conv_w'][li], w['rglru_conv_b'][li],
                                        w['rglru_wg'][li], w['rglru_bg'][li], w['rglru_lambda'][li], tb=tb, rb=rb)
            new['h'].append(h_new)
            new['conv'].append(conv_new.reshape(CONV_W - 1, rb, d_a).transpose(1, 0, 2))
            ob, s_stack = hgrn2(proj, s0, li % s0.shape[0], s_stack, n_even,
                                w['hgrn2_lb_raw'], w['hgrn2_gnorm'][li], li, n_t=n_t, tb=tb, nb=nb)
            if nb != rb:
                ob = ob.reshape(rb // nb, n // rb, nb, d_b).transpose(1, 0, 2, 3).reshape(n, d_b)
            x = mix_ffn(ya, ob, x, w['even_w_out'], w['norm_ffn'][l], w['ffn_w_gu'], w['ffn_w_down'], li, tm,
                        w['ffn_w_down'].shape[1] // 2)
        else:
            z, re_new, im_new = s5(x, n_t, w['norm_mix'][l], w['s5_wb'][li], w['s5_wc'][li], w['s5_a_re'][li],
                                   w['s5_a_im'][li], w['s5_d'][li], re0[li].reshape(rb, ns), im0[li].reshape(rb, ns),
                                   tb=tb, rb=rb)
            new['re'].append(re_new.reshape(re0.shape[1:]))
            new['im'].append(im_new.reshape(im0.shape[1:]))
            x = glu_residual(z, w['s5_w_glu'], li, x, tm, w['s5_w_glu'].shape[2] // 2)
            x = moe_residual(x, w['norm_ffn'][l], w['moe_w_router_t'][li], w['moe_w_gu'], w['moe_w_down'], li,
                             tm, V7X_MXU_DIM)
    stack = lambda k: jnp.stack(new[k])
    return x, (stack('h'), stack('conv'), s_stack, stack('re'), stack('im'))


def kernel(x_prompt, x_sample, state_rglru_h, state_rglru_conv, state_hgrn2, state_s5_re, state_s5_im,
           meta_tokens, norm_mix, norm_ffn, norm_final, even_w_in, even_w_out,
           rglru_conv_w, rglru_conv_b, rglru_w_a, rglru_b_a, rglru_w_x, rglru_b_x, rglru_lambda,
           hgrn2_lb_raw, hgrn2_gnorm, s5_lam_re, s5_lam_im, s5_log_dt, s5_b_re, s5_b_im, s5_c_re, s5_c_im,
           s5_d, s5_w_glu, ffn_w_gu, ffn_w_down, moe_w_router, moe_w_gu, moe_w_down):
    bp, tp0, d = x_prompt.shape
    bs, ts, _ = x_sample.shape
    tp = tp0 + N_META
    d_a = state_rglru_h.shape[-1]
    n_even, n_odd = state_rglru_h.shape[0], state_s5_re.shape[0]
    assert bp == SUBLANES and bs % SUBLANES == 0
    tb_p = PROMPT_BLOCK_ROWS // bp
    tp_pad = -(-tp // tb_p) * tb_p
    nb_s = HGRN_ROWS // ts
    assert (tp % (HGRN_ROWS // bp) == 0 and tb_p % (HGRN_ROWS // bp) == 0 and bs % nb_s == 0
            and nb_s * ts == HGRN_ROWS and nb_s % SUBLANES == 0)

    eye_a = jnp.eye(H_A, dtype=F32)
    block_diag = lambda m: jnp.einsum('lhij,hg->lhigj', m, eye_a).reshape(n_even, d_a, d_a)
    prep = [s5_prep(s5_lam_re[i], s5_lam_im[i], s5_log_dt[i], s5_b_re[i], s5_b_im[i]) for i in range(n_odd)]
    ns = s5_lam_re.shape[1] * s5_lam_re.shape[2]
    bshape = s5_b_re.shape[1:]
    w = {
        'norm_mix': norm_mix, 'norm_ffn': norm_ffn, 'norm_final': norm_final,
        'even_w_in': even_w_in.astype(BF16), 'even_w_out': even_w_out.astype(BF16),
        'rglru_conv_w': rglru_conv_w, 'rglru_conv_b': rglru_conv_b,
        'rglru_wg': jnp.concatenate([block_diag(rglru_w_a), block_diag(rglru_w_x)], axis=2).astype(BF16),
        'rglru_bg': jnp.concatenate([rglru_b_a, rglru_b_x], axis=1),
        'rglru_lambda': rglru_lambda, 'hgrn2_lb_raw': hgrn2_lb_raw, 'hgrn2_gnorm': hgrn2_gnorm,
        's5_a_re': [p[0].reshape(1, ns) for p in prep], 's5_a_im': [p[1].reshape(1, ns) for p in prep],
        's5_wb': [jnp.concatenate([_s5_block_diag_in(p[2].reshape(bshape)), _s5_block_diag_in(p[3].reshape(bshape))],
                                  axis=2).astype(BF16) for p in prep],
        's5_wc': [jnp.concatenate([_s5_block_diag_out(s5_c_re[i]), -_s5_block_diag_out(s5_c_im[i])],
                                  axis=1).astype(BF16) for i in range(n_odd)],
        's5_d': s5_d, 's5_w_glu': s5_w_glu.astype(BF16),
        'ffn_w_gu': ffn_w_gu.astype(BF16), 'ffn_w_down': ffn_w_down.astype(BF16),
        'moe_w_router_t': moe_w_router.transpose(0, 2, 1),
        'moe_w_gu': moe_w_gu.astype(BF16), 'moe_w_down': moe_w_down.astype(BF16),
    }

    tc_p = HGRN_ROWS // bp
    assert N_META == tc_p
    xm = to_time_major(x_prompt, meta_tokens.astype(x_prompt.dtype), tp_pad, tb_p // tc_p)
    zero = lambda ref, lead: jnp.zeros((lead, bp) + ref.shape[2:], ref.dtype)
    xp, p_new = _trunk(xm, tp, bp, tb_p, bp, zero(state_rglru_h, n_even), zero(state_rglru_conv, n_even),
                       zero(state_hgrn2, 1), zero(state_s5_re, n_odd), zero(state_s5_im, n_odd), w)
    nk_out = max(k for k in range(1, HGRN_ROWS // tc_p + 1) if tp0 % (k * tc_p) == 0)
    y_prompt = rmsnorm_batch_major(xp, norm_final, bp, N_META, tp0, tc_p, nk_out)

    xs = x_sample.transpose(1, 0, 2).reshape(ts * bs, d)
    xs, s_new = _trunk(xs, ts, bs, ts, nb_s, state_rglru_h, state_rglru_conv, state_hgrn2, state_s5_re, state_s5_im, w)
    y_sample = rmsnorm_rows(xs, norm_final, ts * bs).reshape(ts, bs, d).transpose(1, 0, 2)

    refs = (state_rglru_h, state_rglru_conv, state_hgrn2, state_s5_re, state_s5_im)
    cast = lambda new: tuple(a.astype(r.dtype) for a, r in zip(new, refs))
    return (y_prompt, y_sample) + cast(p_new) + cast(s_new)
```

```python
import functools

import jax
import jax.numpy as jnp
from jax import lax
from jax.experimental import pallas as pl
from jax.experimental.pallas import tpu as pltpu

F32 = jnp.float32
BF16 = jnp.bfloat16

EPS = 1e-6
N_META = 16
CONV_W = 4
RG_C = 8.0
H_A = 8
S5_P = 64

V7X_VMEM_LIMIT_BYTES = 56 * 1024 * 1024
SUBLANES = 8
LANES = 128
V7X_MXU_DIM = 256
MOE_TAIL_ROWS = (128, 160, 192, 224, V7X_MXU_DIM)
HGRN_ROWS = 128
S5_MAX_SUB_ROWS = 384
HGRN_EXP_CLAMP = 80.0
PROMPT_BLOCK_ROWS = 768
SCAN_CARRY_ELEMS = 16 * SUBLANES * LANES


def _params(*sem):
    return pltpu.CompilerParams(dimension_semantics=sem, vmem_limit_bytes=V7X_VMEM_LIMIT_BYTES)


def _rms(x, g):
    ms = jnp.mean(x * x, axis=-1, keepdims=True)
    return x * lax.rsqrt(ms + EPS) * g


def _dot(a, b):
    return jnp.dot(a, b, preferred_element_type=F32)


def _full(shape):
    return pl.BlockSpec(shape, lambda *_: (0,) * len(shape))


def _norm_matmul_kernel(x_ref, g_ref, w_ref, o_ref, xn_ref):
    @pl.when(pl.program_id(1) == 0)
    def _():
        xn_ref[...] = _rms(x_ref[...], g_ref[...]).astype(BF16)

    o_ref[...] = _dot(xn_ref[...], w_ref[...]).astype(o_ref.dtype)


def norm_matmul(x, g, w, li, tm, tn):
    n, d = x.shape
    nout = w.shape[2]
    return pl.pallas_call(
        _norm_matmul_kernel,
        out_shape=jax.ShapeDtypeStruct((n, nout), BF16),
        grid=(n // tm, nout // tn),
        in_specs=[
            pl.BlockSpec((tm, d), lambda i, j: (i, 0)),
            _full((1, d)),
            pl.BlockSpec((None, d, tn), lambda i, j: (li, 0, j)),
        ],
        out_specs=pl.BlockSpec((tm, tn), lambda i, j: (i, j)),
        scratch_shapes=[pltpu.VMEM((tm, d), BF16)],
        compiler_params=_params("parallel", "arbitrary"),
        name="norm_matmul",
    )(x, g.reshape(1, d), w)


def _mix_ffn_kernel(ya_ref, ob_ref, x_ref, wo_ref, g_ref, wg_ref, wu_ref, wdn_ref, o_ref, xn_ref):
    j = pl.program_id(1)
    d_a = ya_ref.shape[1]

    @pl.when(j == 0)
    def _():
        x1 = x_ref[...] + _dot(ya_ref[...], wo_ref[:d_a, :]) + _dot(ob_ref[...], wo_ref[d_a:, :])
        xn_ref[...] = _rms(x1, g_ref[...]).astype(BF16)
        o_ref[...] = x1

    xn = xn_ref[...]
    h = (jax.nn.silu(_dot(xn, wg_ref[...])) * _dot(xn, wu_ref[...])).astype(BF16)
    o_ref[...] += _dot(h, wdn_ref[...])


def mix_ffn(ya, ob, x, w_out, g, w_gu, w_down, li, tm, chunk):
    n, d = x.shape
    dff = w_down.shape[1]
    assert dff % chunk == 0 and chunk % LANES == 0
    nj = dff // chunk
    rows = lambda a: pl.BlockSpec((tm, a.shape[1]), lambda i, j: (i, 0))
    return pl.pallas_call(
        _mix_ffn_kernel,
        out_shape=jax.ShapeDtypeStruct((n, d), F32),
        grid=(n // tm, nj),
        in_specs=[
            rows(ya), rows(ob), rows(x),
            pl.BlockSpec((None,) + w_out.shape[1:], lambda i, j: (li, 0, 0)),
            _full((1, d)),
            pl.BlockSpec((None, d, chunk), lambda i, j: (li, 0, j)),
            pl.BlockSpec((None, d, chunk), lambda i, j: (li, 0, j + nj)),
            pl.BlockSpec((None, chunk, d), lambda i, j: (li, j, 0)),
        ],
        out_specs=pl.BlockSpec((tm, d), lambda i, j: (i, 0)),
        scratch_shapes=[pltpu.VMEM((tm, d), BF16)],
        compiler_params=_params("parallel", "arbitrary"),
        name="mix_ffn",
    )(ya, ob, x, w_out, g.reshape(1, d), w_gu, w_gu, w_down)


def _glu_residual_kernel(z_ref, wv_ref, wg_ref, r_ref, o_ref):
    z = z_ref[...]
    val = _dot(z, wv_ref[...])
    gate = _dot(z, wg_ref[...])
    o_ref[...] = r_ref[...] + val * jax.nn.sigmoid(gate)


def glu_residual(z, w_glu, li, res, tm, tn):
    n, d = z.shape
    dout = w_glu.shape[2] // 2
    nj = dout // tn
    return pl.pallas_call(
        _glu_residual_kernel,
        out_shape=jax.ShapeDtypeStruct((n, dout), F32),
        grid=(n // tm, nj),
        in_specs=[
            pl.BlockSpec((tm, d), lambda i, j: (i, 0)),
            pl.BlockSpec((None, d, tn), lambda i, j: (li, 0, j)),
            pl.BlockSpec((None, d, tn), lambda i, j: (li, 0, j + nj)),
            pl.BlockSpec((tm, tn), lambda i, j: (i, j)),
        ],
        out_specs=pl.BlockSpec((tm, tn), lambda i, j: (i, j)),
        compiler_params=_params("parallel", "arbitrary"),
        name="glu_residual",
    )(z, w_glu, w_glu, res)


def _rmsnorm_kernel(x_ref, g_ref, o_ref):
    o_ref[...] = _rms(x_ref[...], g_ref[...])


def rmsnorm_rows(x, g, tm):
    n, d = x.shape
    return pl.pallas_call(
        _rmsnorm_kernel,
        out_shape=jax.ShapeDtypeStruct((n, d), F32),
        grid=(n // tm,),
        in_specs=[pl.BlockSpec((tm, d), lambda i: (i, 0)), _full((1, d))],
        out_specs=pl.BlockSpec((tm, d), lambda i: (i, 0)),
        compiler_params=_params("parallel"),
        name="final_rmsnorm",
    )(x, g.reshape(1, d))


def _rmsnorm_batch_major_kernel(*refs, nk, tc, nb):
    x_refs, g_ref, o_ref = refs[:nk], refs[nk], refs[nk + 1]
    d = o_ref.shape[-1]
    for k in range(nk):
        y = _rms(x_refs[k][...], g_ref[...])
        o_ref[:, k * tc:(k + 1) * tc, :] = jnp.swapaxes(y.reshape(tc, nb, d), 0, 1)


def rmsnorm_batch_major(x, g, nb, t_skip, t_out, tc, nk):
    n, d = x.shape
    rows = tc * nb
    assert t_skip % tc == 0 and t_out % (nk * tc) == 0
    specs = [pl.BlockSpec((rows, d), lambda j, k=k: (nk * j + t_skip // tc + k, 0)) for k in range(nk)]
    return pl.pallas_call(
        functools.partial(_rmsnorm_batch_major_kernel, nk=nk, tc=tc, nb=nb),
        out_shape=jax.ShapeDtypeStruct((nb, t_out, d), F32),
        grid=(t_out // (nk * tc),),
        in_specs=specs + [_full((1, d))],
        out_specs=pl.BlockSpec((nb, nk * tc, d), lambda j: (0, j, 0)),
        compiler_params=_params("parallel"),
        name="final_rmsnorm_batch_major",
    )(*([x] * nk), g.reshape(1, d))


def _to_time_major_kernel(*refs, nk, n_chunks):
    x_refs, lead_ref, o_ref = refs[:nk], refs[nk], refs[nk + 1]
    i = pl.program_id(0)
    nb, tc, d = x_refs[0].shape
    rows = tc * nb
    for k in range(nk):
        chunk = i * nk + k - 1
        val = jnp.swapaxes(x_refs[k][...], 0, 1).reshape(rows, d)
        if k == 0:
            lead = jnp.broadcast_to(lead_ref[...][:, None, :], (tc, nb, d)).reshape(rows, d)
            val = jnp.where(i == 0, lead, val)
        o_ref[k * rows:(k + 1) * rows, :] = jnp.where(chunk < n_chunks, val, 0.0)


def to_time_major(x, lead, t_pad, nk):
    nb, t, d = x.shape
    tc = lead.shape[0]
    assert t % tc == 0 and t_pad % (nk * tc) == 0
    n_chunks = t // tc
    specs = [pl.BlockSpec((nb, tc, d), lambda i, k=k: (0, jnp.clip(i * nk + k - 1, 0, n_chunks - 1), 0))
             for k in range(nk)]
    return pl.pallas_call(
        functools.partial(_to_time_major_kernel, nk=nk, n_chunks=n_chunks),
        out_shape=jax.ShapeDtypeStruct((t_pad * nb, d), F32),
        grid=(t_pad // (nk * tc),),
        in_specs=specs + [_full((tc, d))],
        out_specs=pl.BlockSpec((nk * tc * nb, d), lambda i: (i, 0)),
        compiler_params=_params("parallel"),
        name="to_time_major",
    )(*([x] * nk), lead)


def _rglru_kernel(xa_ref, ga_ref, conv0_ref, h0_ref, cw_ref, cb_ref, wg_ref, bg_ref, lam_ref,
                  ya_ref, hlast_ref, convnew_ref, xpad_ref, a_ref, u_ref, *, n_t, tb, rb):
    i = pl.program_id(0)
    rows = tb * rb
    tail = (CONV_W - 1) * rb
    c = xa_ref.shape[-1]
    t_valid = jnp.minimum(n_t - i * tb, tb)

    @pl.when(i == 0)
    def _():
        xpad_ref[0:tail, :] = conv0_ref[...]
        hlast_ref[...] = h0_ref[...]

    @pl.when(i > 0)
    def _():
        xpad_ref[0:tail, :] = xpad_ref[rows:rows + tail, :]

    xpad_ref[tail:tail + rows, :] = xa_ref[...].astype(F32)
    xc = cb_ref[...]
    for k in range(CONV_W):
        xc = xc + cw_ref[k:k + 1, :] * xpad_ref[k * rb:k * rb + rows, :]

    gates = _dot(xc.astype(BF16), wg_ref[...]) + bg_ref[...]
    r = jax.nn.sigmoid(gates[:, :c])
    ig = jax.nn.sigmoid(gates[:, c:])
    log_a = (-RG_C) * r * jax.nn.softplus(-lam_ref[...])
    a = jnp.exp(log_a)
    mult = jnp.sqrt(1.0 - a * a)
    a_ref[...] = a
    u_ref[...] = mult * ig * xc

    lc = min(c, max(LANES, SCAN_CARRY_ELEMS // rb // LANES * LANES))
    for c0 in range(0, c, lc):
        def body(t, h, c0=c0):
            sl = pl.ds(pl.multiple_of(t * rb, rb), rb)
            h = a_ref[sl, c0:c0 + lc] * h + u_ref[sl, c0:c0 + lc]
            u_ref[sl, c0:c0 + lc] = h
            return h

        hlast_ref[:, c0:c0 + lc] = lax.fori_loop(0, t_valid, body, hlast_ref[:, c0:c0 + lc])

    live = lax.broadcasted_iota(jnp.int32, (rows, c), 0) < t_valid * rb
    ya_ref[...] = jnp.where(live, u_ref[...] * jax.nn.gelu(ga_ref[...].astype(F32)), 0.0).astype(ya_ref.dtype)
    convnew_ref[...] = xpad_ref[pl.ds(pl.multiple_of(t_valid * rb, rb), tail), :]


def rglru(proj, n_t, conv0, h0, cw, cb, wg, bg, lam, tb, rb):
    c = h0.shape[1]
    n_rows = proj.shape[0]
    rows = tb * rb
    tail = (CONV_W - 1) * rb
    return pl.pallas_call(
        functools.partial(_rglru_kernel, n_t=n_t, tb=tb, rb=rb),
        out_shape=(
            jax.ShapeDtypeStruct((n_rows, c), BF16),
            jax.ShapeDtypeStruct((rb, c), F32),
            jax.ShapeDtypeStruct((tail, c), F32),
        ),
        grid=(n_rows // rows,),
        in_specs=[
            pl.BlockSpec((rows, c), lambda i: (i, 0)),
            pl.BlockSpec((rows, c), lambda i: (i, 1)),
            _full((tail, c)), _full((rb, c)), _full((CONV_W, c)), _full((1, c)),
            _full((c, 2 * c)), _full((1, 2 * c)), _full((1, c)),
        ],
        out_specs=(
            pl.BlockSpec((rows, c), lambda i: (i, 0)),
            _full((rb, c)),
            _full((tail, c)),
        ),
        scratch_shapes=[
            pltpu.VMEM((rows + tail, c), F32),
            pltpu.VMEM((rows, c), F32),
            pltpu.VMEM((rows, c), F32),
        ],
        compiler_params=_params("arbitrary"),
        name="rglru",
    )(proj, proj, conv0, h0, cw, cb.reshape(1, c), wg, bg.reshape(1, 2 * c), lam.reshape(1, c))


def _hgrn2_kernel(q_ref, f_ref, v_ref, gb_ref, s0_ref, lbraw_ref, gn_ref, *rest, layer, n_t, tb, nb, nh):
    ob_ref, snew_ref, st_ref = rest[-3:]
    i = pl.program_id(1)
    m = HGRN_ROWS
    tc = m // nb
    d = q_ref.shape[-1]
    dk = d // nh
    n_chunks = jnp.minimum(n_t - i * tb, tb) // tc

    @pl.when(i == 0)
    def _():
        for b in range(nb):
            for h in range(nh):
                st_ref[h, :, b * dk:(b + 1) * dk] = s0_ref[b, h].T

    @pl.when(n_chunks < tb // tc)
    def _():
        ob_ref[...] = jnp.zeros_like(ob_ref)

    p = jax.nn.softmax(lbraw_ref[...], axis=0)
    cum = p[0:1, :]
    for r in range(1, layer + 1):
        cum = cum + p[r:r + 1, :]
    lb = cum - p[0:1, :]
    log_lb = jnp.log(lb)
    log_1mlb = jnp.log1p(-lb)

    row = lax.broadcasted_iota(jnp.int32, (m, m), 0)
    col = lax.broadcasted_iota(jnp.int32, (m, m), 1)
    same_seq_causal = jnp.where(((row & (nb - 1)) == (col & (nb - 1))) & (col <= row), 1.0, 0.0)
    bid = lax.broadcasted_iota(jnp.int32, (m, dk), 0) & (nb - 1)

    def chunk(c, carry):
        def rows_of(ref):
            if len(ref.shape) == 2:
                return ref[pl.ds(pl.multiple_of(c * m, m), m), :]
            return ref[pl.ds(c * tc, tc)].reshape(m, d)

        q = jax.nn.silu(rows_of(q_ref).astype(F32))
        fr = rows_of(f_ref).astype(F32)
        v = rows_of(v_ref).astype(BF16)
        gb = rows_of(gb_ref).astype(F32)
        logf = jnp.logaddexp(log_lb, log_1mlb + jax.nn.log_sigmoid(fr))
        k = 1.0 - jnp.exp(logf)
        slabs = [logf[0:nb]]
        for t in range(1, tc):
            slabs.append(slabs[-1] + logf[t * nb:(t + 1) * nb])
        g = jnp.concatenate(slabs, axis=0)
        g_last = jnp.concatenate([slabs[-1]] * tc, axis=0)
        qt = (q * jnp.exp(g)).astype(BF16)
        kt = (k * jnp.exp(jnp.minimum(-g, HGRN_EXP_CLAMP))).astype(BF16)
        ks = (k * jnp.exp(g_last - g)).astype(BF16)
        dec = jnp.exp(slabs[-1])

        outs = []
        for h in range(nh):
            hs = slice(h * dk, (h + 1) * dk)
            qt_h, kt_h, ks_h, v_h = qt[:, hs], kt[:, hs], ks[:, hs], v[:, hs]
            att = lax.dot_general(qt_h, kt_h, (((1,), (1,)), ((), ())), preferred_element_type=F32)
            att = (att * same_seq_causal).astype(BF16)
            o = _dot(att, v_h)
            zero = jnp.zeros_like(qt_h)
            expand = lambda x: jnp.concatenate([jnp.where(bid == b, x, zero) for b in range(nb)], axis=1)
            st_h = st_ref[h]
            o = o + lax.dot_general(expand(qt_h), st_h.astype(BF16), (((1,), (1,)), ((), ())),
                                    preferred_element_type=F32)
            dst = lax.dot_general(v_h, expand(ks_h), (((0,), (0,)), ((), ())), preferred_element_type=F32)
            dec_row = jnp.concatenate([dec[b:b + 1, hs] for b in range(nb)], axis=1)
            st_ref[h] = st_h * dec_row + dst
            outs.append(_rms(o, gn_ref[:, hs]))
        ob = jnp.concatenate(outs, axis=1) * jax.nn.silu(gb)
        ob_ref[pl.ds(pl.multiple_of(c * m, m), m), :] = ob.astype(ob_ref.dtype)
        return carry

    lax.fori_loop(0, n_chunks, chunk, 0)

    @pl.when(i == pl.num_programs(1) - 1)
    def _():
        for b in range(nb):
            for h in range(nh):
                snew_ref[b, h] = st_ref[h, :, b * dk:(b + 1) * dk].T


def hgrn2(proj, s0, s0_layer, s_stack, n_layers, lb_raw, gnorm, layer, n_t, tb, nb):
    _, bsz, nh, dk, _ = s0.shape
    d = nh * dk
    off = proj.shape[-1] // d - 4
    nbb = bsz // nb
    n_tpad = proj.shape[0] // bsz
    nt = n_tpad // tb
    if nb == bsz:
        col = lambda k: pl.BlockSpec((tb * nb, d), lambda j, i, k=k: (i, k + off))
    else:
        proj = proj.reshape(n_tpad, bsz, proj.shape[1])
        col = lambda k: pl.BlockSpec((tb, nb, d), lambda j, i, k=k: (i, j, k + off))
    in_specs = [
        col(0), col(1), col(2), col(3),
        pl.BlockSpec((None, nb, nh, dk, dk), lambda j, i: (s0_layer, j, 0, 0, 0)),
        _full(lb_raw.shape),
        _full((1, d)),
    ]
    args = [proj, proj, proj, proj, s0, lb_raw, gnorm.reshape(1, d)]
    aliases = {}
    if s_stack is not None:
        in_specs.append(pl.BlockSpec(memory_space=pl.ANY))
        args.append(s_stack)
        aliases = {len(args) - 1: 1}
    return pl.pallas_call(
        functools.partial(_hgrn2_kernel, layer=layer, n_t=n_t, tb=tb, nb=nb, nh=nh),
        out_shape=(
            jax.ShapeDtypeStruct((n_tpad * bsz, d), BF16),
            jax.ShapeDtypeStruct((n_layers, bsz, nh, dk, dk), F32),
        ),
        grid=(nbb, nt),
        in_specs=in_specs,
        out_specs=(
            pl.BlockSpec((tb * nb, d), lambda j, i: (j * nt + i, 0)),
            pl.BlockSpec((None, nb, nh, dk, dk), lambda j, i: (layer, j, 0, 0, 0)),
        ),
        scratch_shapes=[pltpu.VMEM((nh, dk, nb * dk), F32)],
        input_output_aliases=aliases,
        compiler_params=_params("arbitrary", "arbitrary"),
        name="hgrn2",
    )(*args)


def _s5_prep_kernel(lr_ref, li_ref, ldt_ref, bre_ref, bim_ref, ar_ref, ai_ref, ore_ref, oim_ref):
    lr = lr_ref[...]
    li = li_ref[...]
    dt = jnp.exp(ldt_ref[...])
    mag = jnp.exp(lr * dt)
    ar = mag * jnp.cos(li * dt)
    ai = mag * jnp.sin(li * dt)
    den = lr * lr + li * li
    cr = ((ar - 1.0) * lr + ai * li) / den
    ci = (ai * lr - (ar - 1.0) * li) / den
    ar_ref[...] = ar
    ai_ref[...] = ai
    ore_ref[...] = cr * bre_ref[...] - ci * bim_ref[...]
    oim_ref[...] = cr * bim_ref[...] + ci * bre_ref[...]


def s5_prep(lam_re, lam_im, log_dt, b_re, b_im):
    g, p, c = b_re.shape
    ns = g * p
    colv = jax.ShapeDtypeStruct((ns, 1), F32)
    mat = jax.ShapeDtypeStruct((ns, c), F32)
    ldt = jnp.broadcast_to(log_dt[:, None], (g, p)).reshape(ns, 1)
    return pl.pallas_call(
        _s5_prep_kernel, out_shape=(colv, colv, mat, mat), name="s5_prep",
    )(lam_re.reshape(ns, 1), lam_im.reshape(ns, 1), ldt, b_re.reshape(ns, c), b_im.reshape(ns, c))


def _s5_kernel(x_ref, g_ref, wb_ref, wc_ref, ar_ref, ai_ref, d_ref, h0r_ref, h0i_ref,
               z_ref, hr_ref, hi_ref, *, n_t, tb, rb, m):
    i = pl.program_id(0)
    nch = wb_ref.shape[0]
    cw = wb_ref.shape[1]
    sw = wb_ref.shape[2] // 2
    tsb = m // rb
    nsl = rb // SUBLANES
    n_sub = jnp.minimum(n_t - i * tb, tb) // tsb

    @pl.when(i == 0)
    def _():
        hr_ref[...] = h0r_ref[...]
        hi_ref[...] = h0i_ref[...]

    @pl.when(n_sub < tb // tsb)
    def _():
        z_ref[...] = jnp.zeros_like(z_ref)

    def sub_block(j, carry):
        r0 = pl.multiple_of(j * m, m)
        xn = _rms(x_ref[pl.ds(r0, m), :], g_ref[...])
        xnb = xn.astype(BF16)
        drive = [_dot(xnb[:, c * cw:(c + 1) * cw], wb_ref[c]) for c in range(nch)]
        for c in range(nch):
            cs = slice(c * cw, (c + 1) * cw)
            ss = slice(c * sw, (c + 1) * sw)
            u = xn[:, cs]
            bu = drive[c]
            ar = jnp.broadcast_to(ar_ref[:, ss], (SUBLANES, sw))
            ai = jnp.broadcast_to(ai_ref[:, ss], (SUBLANES, sw))
            out_r = [None] * (tsb * nsl)
            out_i = [None] * (tsb * nsl)
            for s in range(nsl):
                srow = slice(s * SUBLANES, (s + 1) * SUBLANES)
                hr, hi = hr_ref[srow, ss], hi_ref[srow, ss]
                for t in range(tsb):
                    lo = t * rb + s * SUBLANES
                    hr, hi = (ar * hr - ai * hi + bu[lo:lo + SUBLANES, :sw],
                              ar * hi + ai * hr + bu[lo:lo + SUBLANES, sw:])
                    out_r[t * nsl + s] = hr
                    out_i[t * nsl + s] = hi
                hr_ref[srow, ss] = hr
                hi_ref[srow, ss] = hi
            hcat = jnp.concatenate([jnp.concatenate(out_r, axis=0).astype(BF16),
                                    jnp.concatenate(out_i, axis=0).astype(BF16)], axis=1)
            y = _dot(hcat, wc_ref[c]) + d_ref[:, cs] * u
            z_ref[pl.ds(r0, m), cs] = jax.nn.gelu(y).astype(z_ref.dtype)
        return carry

    lax.fori_loop(0, n_sub, sub_block, 0)


def s5(x, n_t, g, wb, wc, li, a_re, a_im, dskip, h0r, h0i, tb, rb):
    n_rows, d = x.shape
    ns = h0r.shape[1]
    rows = tb * rb
    m = max(k for k in range(rb, S5_MAX_SUB_ROWS + 1, rb)
            if rows % k == 0 and n_t % (k // rb) == 0 and k % (2 * SUBLANES) == 0)
    return pl.pallas_call(
        functools.partial(_s5_kernel, n_t=n_t, tb=tb, rb=rb, m=m),
        out_shape=(
            jax.ShapeDtypeStruct((n_rows, d), BF16),
            jax.ShapeDtypeStruct((rb, ns), F32),
            jax.ShapeDtypeStruct((rb, ns), F32),
        ),
        grid=(n_rows // rows,),
        in_specs=[
            pl.BlockSpec((rows, d), lambda i: (i, 0)),
            _full((1, d)),
            pl.BlockSpec((None,) + wb.shape[1:], lambda i: (li, 0, 0, 0)),
            pl.BlockSpec((None,) + wc.shape[1:], lambda i: (li, 0, 0, 0)),
            _full((1, ns)), _full((1, ns)), _full((1, d)),
            _full((rb, ns)), _full((rb, ns)),
        ],
        out_specs=(
            pl.BlockSpec((rows, d), lambda i: (i, 0)),
            _full((rb, ns)), _full((rb, ns)),
        ),
        compiler_params=_params("arbitrary"),
        name="s5",
    )(x, g.reshape(1, d), wb, wc, a_re, a_im, dskip.reshape(1, d), h0r, h0i)


def _s5_block_diag_in(b):
    g, p, c = b.shape
    gpc = V7X_MXU_DIM // c
    bt = b.transpose(0, 2, 1).reshape(g // gpc, gpc, c, p)
    out = jnp.einsum('ngcp,gh->ngchp', bt, jnp.eye(gpc, dtype=b.dtype))
    return out.reshape(g // gpc, gpc * c, gpc * p)


def _s5_block_diag_out(cm):
    g, c, p = cm.shape
    gpc = V7X_MXU_DIM // c
    ct = cm.transpose(0, 2, 1).reshape(g // gpc, gpc, p, c)
    out = jnp.einsum('ngpc,gh->ngphc', ct, jnp.eye(gpc, dtype=cm.dtype))
    return out.reshape(g // gpc, gpc * p, gpc * c)


def _moe_kernel(x_ref, g_ref, wrt_ref, wgu_ref, wdn_ref, o_ref, xn_ref, pos_ref, gate_ref, *, sub):
    e = pl.program_id(1)
    tm = x_ref.shape[0]
    ne = wrt_ref.shape[0]
    dff = wdn_ref.shape[1]

    @pl.when(e == 0)
    def _():
        x = x_ref[...]
        xn = _rms(x, g_ref[...])
        xn_ref[...] = xn.astype(BF16)
        logits = lax.dot_general(wrt_ref[...], xn, (((1,), (1,)), ((), ())),
                                 precision=lax.Precision.HIGHEST, preferred_element_type=F32)
        ex = jnp.exp(logits - jnp.max(logits, axis=0, keepdims=True))
        probs = ex / jnp.sum(ex, axis=0, keepdims=True)
        eid = lax.broadcasted_iota(jnp.int32, (ne, tm), 0).astype(F32)
        m1 = jnp.max(probs, axis=0, keepdims=True)
        i1 = jnp.min(jnp.where(probs == m1, eid, float(ne)), axis=0, keepdims=True)
        sel1 = eid == i1
        rest = jnp.where(sel1, -1.0, probs)
        m2 = jnp.max(rest, axis=0, keepdims=True)
        i2 = jnp.min(jnp.where(rest == m2, eid, float(ne)), axis=0, keepdims=True)
        sel2 = eid == i2
        den = m1 + m2
        gate_ref[...] = jnp.where(sel1, m1 / den, 0.0) + jnp.where(sel2, m2 / den, 0.0)
        chosen = jnp.where(sel1, 1.0, jnp.where(sel2, 1.0, 0.0))
        r = lax.broadcasted_iota(jnp.int32, (tm, tm), 0)
        c = lax.broadcasted_iota(jnp.int32, (tm, tm), 1)
        before = jnp.where(r < c, 1.0, 0.0).astype(BF16)
        rank = _dot(chosen.astype(BF16), before)
        pos_ref[...] = jnp.where(chosen > 0.0, rank, -1.0)
        o_ref[...] = x

    pos_e = pos_ref[pl.ds(e, 1), :]
    gate_e = gate_ref[pl.ds(e, 1), :]
    cnt = jnp.sum(jnp.where(pos_e >= 0.0, 1.0, 0.0)).astype(jnp.int32)

    def run_block(base, rows):
        slot = lax.broadcasted_iota(jnp.int32, (rows, tm), 0) + base
        hit = pos_e == slot.astype(F32)
        onehot = jnp.where(hit, 1.0, 0.0).astype(BF16)
        xs = _dot(onehot, xn_ref[...]).astype(BF16)
        gs = jnp.sum(jnp.where(hit, gate_e, 0.0), axis=1, keepdims=True)
        hgu = _dot(xs, wgu_ref[0])
        act = (jax.nn.silu(hgu[:, :dff]) * hgu[:, dff:]).astype(BF16)
        yb = (_dot(act, wdn_ref[0]) * gs).astype(BF16)
        o_ref[...] += lax.dot_general(onehot, yb, (((0,), (0,)), ((), ())), preferred_element_type=F32)

    def body(s, carry):
        run_block(s * sub, sub)
        return carry

    n_full = cnt // sub
    lax.fori_loop(0, n_full, body, 0)
    rem = cnt - n_full * sub
    lo = 0
    for rows in MOE_TAIL_ROWS:
        pl.when((rem > lo) & (rem <= rows))(functools.partial(run_block, n_full * sub, rows))
        lo = rows


def moe_residual(x, g, w_router_t, w_gu, w_down, li, tm, sub):
    n, d = x.shape
    _, ne, _, dff2 = w_gu.shape
    assert MOE_TAIL_ROWS[-1] == sub
    return pl.pallas_call(
        functools.partial(_moe_kernel, sub=sub),
        out_shape=jax.ShapeDtypeStruct((n, d), F32),
        grid=(n // tm, ne),
        in_specs=[
            pl.BlockSpec((tm, d), lambda i, e: (i, 0)),
            _full((1, d)),
            _full((ne, d)),
            pl.BlockSpec((None, 1, d, dff2), lambda i, e: (li, e, 0, 0)),
            pl.BlockSpec((None, 1, dff2 // 2, d), lambda i, e: (li, e, 0, 0)),
        ],
        out_specs=pl.BlockSpec((tm, d), lambda i, e: (i, 0)),
        scratch_shapes=[
            pltpu.VMEM((tm, d), BF16),
            pltpu.VMEM((ne, tm), F32),
            pltpu.VMEM((ne, tm), F32),
        ],
        compiler_params=_params("parallel", "arbitrary"),
        name="moe",
    )(x, g.reshape(1, d), w_router_t, w_gu, w_down)


def _trunk(x, n_t, rb, tb, nb, h0, conv0, s0, re0, im0, w):
    n, d = x.shape
    depth = w['norm_mix'].shape[0]
    d_a = h0.shape[-1]
    nh, dk = s0.shape[2], s0.shape[3]
    d_b = nh * dk
    ns = re0.shape[-2] * re0.shape[-1]
    tm = tb * rb
    n_even = (depth + 1) // 2
    new = {k: [] for k in ('h', 'conv', 're', 'im')}
    s_stack = None
    for l in range(depth):
        li = l // 2
        if l % 2 == 0:
            proj = norm_matmul(x, w['norm_mix'][l], w['even_w_in'], li, tm, w['even_w_in'].shape[2])
            conv_tm = conv0[li].transpose(1, 0, 2).reshape((CONV_W - 1) * rb, d_a)
            ya, h_new, conv_new = rglru(proj, n_t, conv_tm, h0[li], w['rglru_conv_w'][li], w['rglru_conv_b'][li],
                                        w['rglru_wg'][li], w['rglru_bg'][li], w['rglru_lambda'][li], tb=tb, rb=rb)
            new['h'].append(h_new)
            new['conv'].append(conv_new.reshape(CONV_W - 1, rb, d_a).transpose(1, 0, 2))
            ob, s_stack = hgrn2(proj, s0, li % s0.shape[0], s_stack, n_even,
                                w['hgrn2_lb_raw'], w['hgrn2_gnorm'][li], li, n_t=n_t, tb=tb, nb=nb)
            if nb != rb:
                ob = ob.reshape(rb // nb, n // rb, nb, d_b).transpose(1, 0, 2, 3).reshape(n, d_b)
            x = mix_ffn(ya, ob, x, w['even_w_out'], w['norm_ffn'][l], w['ffn_w_gu'], w['ffn_w_down'], li, tm,
                        w['ffn_w_down'].shape[1] // 2)
        else:
            z, re_new, im_new = s5(x, n_t, w['norm_mix'][l], w['s5_wb'], w['s5_wc'], li, w['s5_a_re'][li],
                                   w['s5_a_im'][li], w['s5_d'][li], re0[li].reshape(rb, ns), im0[li].reshape(rb, ns),
                                   tb=tb, rb=rb)
            new['re'].append(re_new.reshape(re0.shape[1:]))
            new['im'].append(im_new.reshape(im0.shape[1:]))
            x = glu_residual(z, w['s5_w_glu'], li, x, tm, w['s5_w_glu'].shape[2] // 2)
            x = moe_residual(x, w['norm_ffn'][l], w['moe_w_router_t'][li], w['moe_w_gu'], w['moe_w_down'], li,
                             tm, V7X_MXU_DIM)
    stack = lambda k: jnp.stack(new[k])
    return x, (stack('h'), stack('conv'), s_stack, stack('re'), stack('im'))


def kernel(x_prompt, x_sample, state_rglru_h, state_rglru_conv, state_hgrn2, state_s5_re, state_s5_im,
           meta_tokens, norm_mix, norm_ffn, norm_final, even_w_in, even_w_out,
           rglru_conv_w, rglru_conv_b, rglru_w_a, rglru_b_a, rglru_w_x, rglru_b_x, rglru_lambda,
           hgrn2_lb_raw, hgrn2_gnorm, s5_lam_re, s5_lam_im, s5_log_dt, s5_b_re, s5_b_im, s5_c_re, s5_c_im,
           s5_d, s5_w_glu, ffn_w_gu, ffn_w_down, moe_w_router, moe_w_gu, moe_w_down):
    bp, tp0, d = x_prompt.shape
    bs, ts, _ = x_sample.shape
    tp = tp0 + N_META
    d_a = state_rglru_h.shape[-1]
    n_even, n_odd = state_rglru_h.shape[0], state_s5_re.shape[0]
    assert bp == SUBLANES and bs % SUBLANES == 0
    tb_p = PROMPT_BLOCK_ROWS // bp
    tp_pad = -(-tp // tb_p) * tb_p
    nb_s = HGRN_ROWS // ts
    assert (tp % (HGRN_ROWS // bp) == 0 and tb_p % (HGRN_ROWS // bp) == 0 and bs % nb_s == 0
            and nb_s * ts == HGRN_ROWS and nb_s % SUBLANES == 0)

    eye_a = jnp.eye(H_A, dtype=F32)
    block_diag = lambda m: jnp.einsum('lhij,hg->lhigj', m, eye_a).reshape(n_even, d_a, d_a)
    s5_g, s5_p, s5_c = s5_b_re.shape[1:]
    lg = n_odd * s5_g
    a_re, a_im, bt_re, bt_im = s5_prep(s5_lam_re.reshape(lg, s5_p), s5_lam_im.reshape(lg, s5_p), s5_log_dt.reshape(lg),
                                       s5_b_re.reshape(lg, s5_p, s5_c), s5_b_im.reshape(lg, s5_p, s5_c))
    s5_wb = jnp.concatenate([_s5_block_diag_in(bt_re.reshape(lg, s5_p, s5_c)),
                             _s5_block_diag_in(bt_im.reshape(lg, s5_p, s5_c))], axis=2).astype(BF16)
    s5_wc = jnp.concatenate([_s5_block_diag_out(s5_c_re.reshape(lg, s5_c, s5_p)),
                             -_s5_block_diag_out(s5_c_im.reshape(lg, s5_c, s5_p))], axis=1).astype(BF16)
    ns = s5_lam_re.shape[1] * s5_lam_re.shape[2]
    w = {
        'norm_mix': norm_mix, 'norm_ffn': norm_ffn, 'norm_final': norm_final,
        'even_w_in': even_w_in.astype(BF16), 'even_w_out': even_w_out.astype(BF16),
        'rglru_conv_w': rglru_conv_w, 'rglru_conv_b': rglru_conv_b,
        'rglru_wg': jnp.concatenate([block_diag(rglru_w_a), block_diag(rglru_w_x)], axis=2).astype(BF16),
        'rglru_bg': jnp.concatenate([rglru_b_a, rglru_b_x], axis=1),
        'rglru_lambda': rglru_lambda, 'hgrn2_lb_raw': hgrn2_lb_raw, 'hgrn2_gnorm': hgrn2_gnorm,
        's5_a_re': a_re.reshape(n_odd, 1, ns), 's5_a_im': a_im.reshape(n_odd, 1, ns),
        's5_wb': s5_wb.reshape((n_odd, -1) + s5_wb.shape[1:]), 's5_wc': s5_wc.reshape((n_odd, -1) + s5_wc.shape[1:]),
        's5_d': s5_d, 's5_w_glu': s5_w_glu.astype(BF16),
        'ffn_w_gu': ffn_w_gu.astype(BF16), 'ffn_w_down': ffn_w_down.astype(BF16),
        'moe_w_router_t': moe_w_router.transpose(0, 2, 1),
        'moe_w_gu': moe_w_gu.astype(BF16), 'moe_w_down': moe_w_down.astype(BF16),
    }

    tc_p = HGRN_ROWS // bp
    assert N_META == tc_p
    xm = to_time_major(x_prompt, meta_tokens.astype(x_prompt.dtype), tp_pad, tb_p // tc_p)
    zero = lambda ref, lead: jnp.zeros((lead, bp) + ref.shape[2:], ref.dtype)
    xp, p_new = _trunk(xm, tp, bp, tb_p, bp, zero(state_rglru_h, n_even), zero(state_rglru_conv, n_even),
                       zero(state_hgrn2, 1), zero(state_s5_re, n_odd), zero(state_s5_im, n_odd), w)
    nk_out = max(k for k in range(1, HGRN_ROWS // tc_p + 1) if tp0 % (k * tc_p) == 0)
    y_prompt = rmsnorm_batch_major(xp, norm_final, bp, N_META, tp0, tc_p, nk_out)

    xs = x_sample.transpose(1, 0, 2).reshape(ts * bs, d)
    xs, s_new = _trunk(xs, ts, bs, ts, nb_s, state_rglru_h, state_rglru_conv, state_hgrn2, state_s5_re, state_s5_im, w)
    y_sample = rmsnorm_rows(xs, norm_final, ts * bs).reshape(ts, bs, d).transpose(1, 0, 2)

    refs = (state_rglru_h, state_rglru_conv, state_hgrn2, state_s5_re, state_s5_im)
    cast = lambda new: tuple(a.astype(r.dtype) for a, r in zip(new, refs))
    return (y_prompt, y_sample) + cast(p_new) + cast(s_new)
```

```python
import functools

import jax
import jax.numpy as jnp
from jax import lax
from jax.experimental import pallas as pl
from jax.experimental.pallas import tpu as pltpu

F32 = jnp.float32
BF16 = jnp.bfloat16

EPS = 1e-6
N_META = 16
CONV_W = 4
RG_C = 8.0
H_A = 8
S5_P = 64

V7X_VMEM_LIMIT_BYTES = 56 * 1024 * 1024
SUBLANES = 8
LANES = 128
V7X_MXU_DIM = 256
MOE_TAIL_ROWS = (128, 160, 192, 224, V7X_MXU_DIM)
HGRN_ROWS = 128
S5_MAX_SUB_ROWS = 384
HGRN_EXP_CLAMP = 80.0
PROMPT_BLOCK_ROWS = 768
SCAN_CARRY_ELEMS = 16 * SUBLANES * LANES


def _params(*sem):
    return pltpu.CompilerParams(dimension_semantics=sem, vmem_limit_bytes=V7X_VMEM_LIMIT_BYTES)


def _rms(x, g):
    ms = jnp.mean(x * x, axis=-1, keepdims=True)
    return x * lax.rsqrt(ms + EPS) * g


def _dot(a, b):
    return jnp.dot(a, b, preferred_element_type=F32)


def _full(shape):
    return pl.BlockSpec(shape, lambda *_: (0,) * len(shape))


def _norm_matmul_kernel(x_ref, g_ref, w_ref, o_ref, xn_ref):
    @pl.when(pl.program_id(1) == 0)
    def _():
        xn_ref[...] = _rms(x_ref[...], g_ref[...]).astype(BF16)

    o_ref[...] = _dot(xn_ref[...], w_ref[...]).astype(o_ref.dtype)


def norm_matmul(x, g, w, li, tm, tn):
    n, d = x.shape
    nout = w.shape[2]
    return pl.pallas_call(
        _norm_matmul_kernel,
        out_shape=jax.ShapeDtypeStruct((n, nout), BF16),
        grid=(n // tm, nout // tn),
        in_specs=[
            pl.BlockSpec((tm, d), lambda i, j: (i, 0)),
            _full((1, d)),
            pl.BlockSpec((None, d, tn), lambda i, j: (li, 0, j)),
        ],
        out_specs=pl.BlockSpec((tm, tn), lambda i, j: (i, j)),
        scratch_shapes=[pltpu.VMEM((tm, d), BF16)],
        compiler_params=_params("parallel", "arbitrary"),
        name="norm_matmul",
    )(x, g.reshape(1, d), w)


def _mix_ffn_kernel(ya_ref, ob_ref, x_ref, wo_ref, g_ref, wg_ref, wu_ref, wdn_ref, o_ref, xn_ref):
    j = pl.program_id(1)
    d_a = ya_ref.shape[1]

    @pl.when(j == 0)
    def _():
        x1 = x_ref[...] + _dot(ya_ref[...], wo_ref[:d_a, :]) + _dot(ob_ref[...], wo_ref[d_a:, :])
        xn_ref[...] = _rms(x1, g_ref[...]).astype(BF16)
        o_ref[...] = x1

    xn = xn_ref[...]
    h = (jax.nn.silu(_dot(xn, wg_ref[...])) * _dot(xn, wu_ref[...])).astype(BF16)
    o_ref[...] += _dot(h, wdn_ref[...])


def mix_ffn(ya, ob, x, w_out, g, w_gu, w_down, li, tm, chunk):
    n, d = x.shape
    dff = w_down.shape[1]
    assert dff % chunk == 0 and chunk % LANES == 0
    nj = dff // chunk
    rows = lambda a: pl.BlockSpec((tm, a.shape[1]), lambda i, j: (i, 0))
    return pl.pallas_call(
        _mix_ffn_kernel,
        out_shape=jax.ShapeDtypeStruct((n, d), F32),
        grid=(n // tm, nj),
        in_specs=[
            rows(ya), rows(ob), rows(x),
            pl.BlockSpec((None,) + w_out.shape[1:], lambda i, j: (li, 0, 0)),
            _full((1, d)),
            pl.BlockSpec((None, d, chunk), lambda i, j: (li, 0, j)),
            pl.BlockSpec((None, d, chunk), lambda i, j: (li, 0, j + nj)),
            pl.BlockSpec((None, chunk, d), lambda i, j: (li, j, 0)),
        ],
        out_specs=pl.BlockSpec((tm, d), lambda i, j: (i, 0)),
        scratch_shapes=[pltpu.VMEM((tm, d), BF16)],
        compiler_params=_params("parallel", "arbitrary"),
        name="mix_ffn",
    )(ya, ob, x, w_out, g.reshape(1, d), w_gu, w_gu, w_down)


def _glu_residual_kernel(z_ref, wv_ref, wg_ref, r_ref, o_ref):
    z = z_ref[...]
    val = _dot(z, wv_ref[...])
    gate = _dot(z, wg_ref[...])
    o_ref[...] = r_ref[...] + val * jax.nn.sigmoid(gate)


def glu_residual(z, w_glu, li, res, tm, tn):
    n, d = z.shape
    dout = w_glu.shape[2] // 2
    nj = dout // tn
    return pl.pallas_call(
        _glu_residual_kernel,
        out_shape=jax.ShapeDtypeStruct((n, dout), F32),
        grid=(n // tm, nj),
        in_specs=[
            pl.BlockSpec((tm, d), lambda i, j: (i, 0)),
            pl.BlockSpec((None, d, tn), lambda i, j: (li, 0, j)),
            pl.BlockSpec((None, d, tn), lambda i, j: (li, 0, j + nj)),
            pl.BlockSpec((tm, tn), lambda i, j: (i, j)),
        ],
        out_specs=pl.BlockSpec((tm, tn), lambda i, j: (i, j)),
        compiler_params=_params("parallel", "arbitrary"),
        name="glu_residual",
    )(z, w_glu, w_glu, res)


def _rmsnorm_kernel(x_ref, g_ref, o_ref):
    o_ref[...] = _rms(x_ref[...], g_ref[...])


def rmsnorm_rows(x, g, tm):
    n, d = x.shape
    return pl.pallas_call(
        _rmsnorm_kernel,
        out_shape=jax.ShapeDtypeStruct((n, d), F32),
        grid=(n // tm,),
        in_specs=[pl.BlockSpec((tm, d), lambda i: (i, 0)), _full((1, d))],
        out_specs=pl.BlockSpec((tm, d), lambda i: (i, 0)),
        compiler_params=_params("parallel"),
        name="final_rmsnorm",
    )(x, g.reshape(1, d))


def _rmsnorm_batch_major_kernel(*refs, nk, tc, nb):
    x_refs, g_ref, o_ref = refs[:nk], refs[nk], refs[nk + 1]
    d = o_ref.shape[-1]
    for k in range(nk):
        y = _rms(x_refs[k][...], g_ref[...])
        o_ref[:, k * tc:(k + 1) * tc, :] = jnp.swapaxes(y.reshape(tc, nb, d), 0, 1)


def rmsnorm_batch_major(x, g, nb, t_skip, t_out, tc, nk):
    n, d = x.shape
    rows = tc * nb
    assert t_skip % tc == 0 and t_out % (nk * tc) == 0
    specs = [pl.BlockSpec((rows, d), lambda j, k=k: (nk * j + t_skip // tc + k, 0)) for k in range(nk)]
    return pl.pallas_call(
        functools.partial(_rmsnorm_batch_major_kernel, nk=nk, tc=tc, nb=nb),
        out_shape=jax.ShapeDtypeStruct((nb, t_out, d), F32),
        grid=(t_out // (nk * tc),),
        in_specs=specs + [_full((1, d))],
        out_specs=pl.BlockSpec((nb, nk * tc, d), lambda j: (0, j, 0)),
        compiler_params=_params("parallel"),
        name="final_rmsnorm_batch_major",
    )(*([x] * nk), g.reshape(1, d))


def _to_time_major_kernel(*refs, nk, n_chunks):
    x_refs, lead_ref, o_ref = refs[:nk], refs[nk], refs[nk + 1]
    i = pl.program_id(0)
    nb, tc, d = x_refs[0].shape
    rows = tc * nb
    for k in range(nk):
        chunk = i * nk + k - 1
        val = jnp.swapaxes(x_refs[k][...], 0, 1).reshape(rows, d)
        if k == 0:
            lead = jnp.broadcast_to(lead_ref[...][:, None, :], (tc, nb, d)).reshape(rows, d)
            val = jnp.where(i == 0, lead, val)
        o_ref[k * rows:(k + 1) * rows, :] = jnp.where(chunk < n_chunks, val, 0.0)


def to_time_major(x, lead, t_pad, nk):
    nb, t, d = x.shape
    tc = lead.shape[0]
    assert t % tc == 0 and t_pad % (nk * tc) == 0
    n_chunks = t // tc
    specs = [pl.BlockSpec((nb, tc, d), lambda i, k=k: (0, jnp.clip(i * nk + k - 1, 0, n_chunks - 1), 0))
             for k in range(nk)]
    return pl.pallas_call(
        functools.partial(_to_time_major_kernel, nk=nk, n_chunks=n_chunks),
        out_shape=jax.ShapeDtypeStruct((t_pad * nb, d), F32),
        grid=(t_pad // (nk * tc),),
        in_specs=specs + [_full((tc, d))],
        out_specs=pl.BlockSpec((nk * tc * nb, d), lambda i: (i, 0)),
        compiler_params=_params("parallel"),
        name="to_time_major",
    )(*([x] * nk), lead)


def _rglru_kernel(xa_ref, ga_ref, conv0_ref, h0_ref, cw_ref, cb_ref, wg_ref, bg_ref, lam_ref,
                  ya_ref, hlast_ref, convnew_ref, xpad_ref, a_ref, u_ref, *, n_t, tb, rb):
    i = pl.program_id(0)
    rows = tb * rb
    tail = (CONV_W - 1) * rb
    c = xa_ref.shape[-1]
    t_valid = jnp.minimum(n_t - i * tb, tb)

    @pl.when(i == 0)
    def _():
        xpad_ref[0:tail, :] = conv0_ref[...]
        hlast_ref[...] = h0_ref[...]

    @pl.when(i > 0)
    def _():
        xpad_ref[0:tail, :] = xpad_ref[rows:rows + tail, :]

    xpad_ref[tail:tail + rows, :] = xa_ref[...].astype(F32)
    xc = cb_ref[...]
    for k in range(CONV_W):
        xc = xc + cw_ref[k:k + 1, :] * xpad_ref[k * rb:k * rb + rows, :]

    gates = _dot(xc.astype(BF16), wg_ref[...]) + bg_ref[...]
    r = jax.nn.sigmoid(gates[:, :c])
    ig = jax.nn.sigmoid(gates[:, c:])
    log_a = (-RG_C) * r * jax.nn.softplus(-lam_ref[...])
    a = jnp.exp(log_a)
    mult = jnp.sqrt(1.0 - a * a)
    a_ref[...] = a
    u_ref[...] = mult * ig * xc

    lc = min(c, max(LANES, SCAN_CARRY_ELEMS // rb // LANES * LANES))
    for c0 in range(0, c, lc):
        def body(t, h, c0=c0):
            sl = pl.ds(pl.multiple_of(t * rb, rb), rb)
            h = a_ref[sl, c0:c0 + lc] * h + u_ref[sl, c0:c0 + lc]
            u_ref[sl, c0:c0 + lc] = h
            return h

        hlast_ref[:, c0:c0 + lc] = lax.fori_loop(0, t_valid, body, hlast_ref[:, c0:c0 + lc])

    live = lax.broadcasted_iota(jnp.int32, (rows, c), 0) < t_valid * rb
    ya_ref[...] = jnp.where(live, u_ref[...] * jax.nn.gelu(ga_ref[...].astype(F32)), 0.0).astype(ya_ref.dtype)
    convnew_ref[...] = xpad_ref[pl.ds(pl.multiple_of(t_valid * rb, rb), tail), :]


def rglru(proj, n_t, conv0, h0, cw, cb, wg, bg, lam, tb, rb):
    c = h0.shape[1]
    n_rows = proj.shape[0]
    rows = tb * rb
    tail = (CONV_W - 1) * rb
    return pl.pallas_call(
        functools.partial(_rglru_kernel, n_t=n_t, tb=tb, rb=rb),
        out_shape=(
            jax.ShapeDtypeStruct((n_rows, c), BF16),
            jax.ShapeDtypeStruct((rb, c), F32),
            jax.ShapeDtypeStruct((tail, c), F32),
        ),
        grid=(n_rows // rows,),
        in_specs=[
            pl.BlockSpec((rows, c), lambda i: (i, 0)),
            pl.BlockSpec((rows, c), lambda i: (i, 1)),
            _full((tail, c)), _full((rb, c)), _full((CONV_W, c)), _full((1, c)),
            _full((c, 2 * c)), _full((1, 2 * c)), _full((1, c)),
        ],
        out_specs=(
            pl.BlockSpec((rows, c), lambda i: (i, 0)),
            _full((rb, c)),
            _full((tail, c)),
        ),
        scratch_shapes=[
            pltpu.VMEM((rows + tail, c), F32),
            pltpu.VMEM((rows, c), F32),
            pltpu.VMEM((rows, c), F32),
        ],
        compiler_params=_params("arbitrary"),
        name="rglru",
    )(proj, proj, conv0, h0, cw, cb.reshape(1, c), wg, bg.reshape(1, 2 * c), lam.reshape(1, c))


def _hgrn2_kernel(q_ref, f_ref, v_ref, gb_ref, s0_ref, lbraw_ref, gn_ref, *rest, layer, n_t, tb, nb, nh):
    ob_ref, snew_ref, st_ref = rest[-3:]
    i = pl.program_id(1)
    m = HGRN_ROWS
    tc = m // nb
    d = q_ref.shape[-1]
    dk = d // nh
    n_chunks = jnp.minimum(n_t - i * tb, tb) // tc

    @pl.when(i == 0)
    def _():
        for b in range(nb):
            for h in range(nh):
                st_ref[h, :, b * dk:(b + 1) * dk] = s0_ref[b, h].T

    @pl.when(n_chunks < tb // tc)
    def _():
        ob_ref[...] = jnp.zeros_like(ob_ref)

    p = jax.nn.softmax(lbraw_ref[...], axis=0)
    cum = p[0:1, :]
    for r in range(1, layer + 1):
        cum = cum + p[r:r + 1, :]
    lb = cum - p[0:1, :]
    log_lb = jnp.log(lb)
    log_1mlb = jnp.log1p(-lb)

    row = lax.broadcasted_iota(jnp.int32, (m, m), 0)
    col = lax.broadcasted_iota(jnp.int32, (m, m), 1)
    same_seq_causal = jnp.where(((row & (nb - 1)) == (col & (nb - 1))) & (col <= row), 1.0, 0.0)
    bid = lax.broadcasted_iota(jnp.int32, (m, dk), 0) & (nb - 1)

    def chunk(c, carry):
        def rows_of(ref):
            if len(ref.shape) == 2:
                return ref[pl.ds(pl.multiple_of(c * m, m), m), :]
            return ref[pl.ds(c * tc, tc)].reshape(m, d)

        q = jax.nn.silu(rows_of(q_ref).astype(F32))
        fr = rows_of(f_ref).astype(F32)
        v = rows_of(v_ref).astype(BF16)
        gb = rows_of(gb_ref).astype(F32)
        logf = jnp.logaddexp(log_lb, log_1mlb + jax.nn.log_sigmoid(fr))
        k = 1.0 - jnp.exp(logf)
        slabs = [logf[0:nb]]
        for t in range(1, tc):
            slabs.append(slabs[-1] + logf[t * nb:(t + 1) * nb])
        g = jnp.concatenate(slabs, axis=0)
        g_last = jnp.concatenate([slabs[-1]] * tc, axis=0)
        qt = (q * jnp.exp(g)).astype(BF16)
        kt = (k * jnp.exp(jnp.minimum(-g, HGRN_EXP_CLAMP))).astype(BF16)
        ks = (k * jnp.exp(g_last - g)).astype(BF16)
        dec = jnp.exp(slabs[-1])

        outs = []
        for h in range(nh):
            hs = slice(h * dk, (h + 1) * dk)
            qt_h, kt_h, ks_h, v_h = qt[:, hs], kt[:, hs], ks[:, hs], v[:, hs]
            att = lax.dot_general(qt_h, kt_h, (((1,), (1,)), ((), ())), preferred_element_type=F32)
            att = (att * same_seq_causal).astype(BF16)
            o = _dot(att, v_h)
            zero = jnp.zeros_like(qt_h)
            expand = lambda x: jnp.concatenate([jnp.where(bid == b, x, zero) for b in range(nb)], axis=1)
            st_h = st_ref[h]
            o = o + lax.dot_general(expand(qt_h), st_h.astype(BF16), (((1,), (1,)), ((), ())),
                                    preferred_element_type=F32)
            dst = lax.dot_general(v_h, expand(ks_h), (((0,), (0,)), ((), ())), preferred_element_type=F32)
            dec_row = jnp.concatenate([dec[b:b + 1, hs] for b in range(nb)], axis=1)
            st_ref[h] = st_h * dec_row + dst
            outs.append(_rms(o, gn_ref[:, hs]))
        ob = jnp.concatenate(outs, axis=1) * jax.nn.silu(gb)
        ob_ref[pl.ds(pl.multiple_of(c * m, m), m), :] = ob.astype(ob_ref.dtype)
        return carry

    lax.fori_loop(0, n_chunks, chunk, 0)

    @pl.when(i == pl.num_programs(1) - 1)
    def _():
        for b in range(nb):
            for h in range(nh):
                snew_ref[b, h] = st_ref[h, :, b * dk:(b + 1) * dk].T


def hgrn2(proj, s0, s0_layer, s_stack, n_layers, lb_raw, gnorm, layer, n_t, tb, nb):
    _, bsz, nh, dk, _ = s0.shape
    d = nh * dk
    off = proj.shape[-1] // d - 4
    nbb = bsz // nb
    n_tpad = proj.shape[0] // bsz
    nt = n_tpad // tb
    if nb == bsz:
        col = lambda k: pl.BlockSpec((tb * nb, d), lambda j, i, k=k: (i, k + off))
    else:
        proj = proj.reshape(n_tpad, bsz, proj.shape[1])
        col = lambda k: pl.BlockSpec((tb, nb, d), lambda j, i, k=k: (i, j, k + off))
    in_specs = [
        col(0), col(1), col(2), col(3),
        pl.BlockSpec((None, nb, nh, dk, dk), lambda j, i: (s0_layer, j, 0, 0, 0)),
        _full(lb_raw.shape),
        _full((1, d)),
    ]
    args = [proj, proj, proj, proj, s0, lb_raw, gnorm.reshape(1, d)]
    aliases = {}
    if s_stack is not None:
        in_specs.append(pl.BlockSpec(memory_space=pl.ANY))
        args.append(s_stack)
        aliases = {len(args) - 1: 1}
    return pl.pallas_call(
        functools.partial(_hgrn2_kernel, layer=layer, n_t=n_t, tb=tb, nb=nb, nh=nh),
        out_shape=(
            jax.ShapeDtypeStruct((n_tpad * bsz, d), BF16),
            jax.ShapeDtypeStruct((n_layers, bsz, nh, dk, dk), F32),
        ),
        grid=(nbb, nt),
        in_specs=in_specs,
        out_specs=(
            pl.BlockSpec((tb * nb, d), lambda j, i: (j * nt + i, 0)),
            pl.BlockSpec((None, nb, nh, dk, dk), lambda j, i: (layer, j, 0, 0, 0)),
        ),
        scratch_shapes=[pltpu.VMEM((nh, dk, nb * dk), F32)],
        input_output_aliases=aliases,
        compiler_params=_params("arbitrary", "arbitrary"),
        name="hgrn2",
    )(*args)


def _s5_discretise(lr, li, ldt):
    dt = jnp.exp(ldt)
    mag = jnp.exp(lr * dt)
    return mag * jnp.cos(li * dt), mag * jnp.sin(li * dt)


def _s5_prep_kernel(lr_ref, li_ref, ldt_ref, lrc_ref, lic_ref, ldtc_ref, bre_ref, bim_ref,
                    ar_ref, ai_ref, ore_ref, oim_ref):
    ar_ref[...], ai_ref[...] = _s5_discretise(lr_ref[...], li_ref[...], ldt_ref[...])
    lr = lrc_ref[...]
    li = lic_ref[...]
    ar, ai = _s5_discretise(lr, li, ldtc_ref[...])
    den = lr * lr + li * li
    cr = ((ar - 1.0) * lr + ai * li) / den
    ci = (ai * lr - (ar - 1.0) * li) / den
    ore_ref[...] = cr * bre_ref[...] - ci * bim_ref[...]
    oim_ref[...] = cr * bim_ref[...] + ci * bre_ref[...]


def s5_prep(lam_re, lam_im, log_dt, b_re, b_im):
    g, p, c = b_re.shape
    dense = lambda a: a.reshape(-1, LANES)
    per_state = lambda a: dense(jnp.broadcast_to(a.reshape(g, -1, 1), (g, p, 1)))
    per_coef = lambda a: dense(jnp.broadcast_to(a.reshape(g, -1, 1), (g, p, c)))
    small = jax.ShapeDtypeStruct((g * p // LANES, LANES), F32)
    big = jax.ShapeDtypeStruct((g * p * c // LANES, LANES), F32)
    ar, ai, ore, oim = pl.pallas_call(
        _s5_prep_kernel, out_shape=(small, small, big, big), name="s5_prep",
    )(per_state(lam_re), per_state(lam_im), per_state(log_dt), per_coef(lam_re), per_coef(lam_im), per_coef(log_dt),
      dense(b_re), dense(b_im))
    return ar.reshape(g, p), ai.reshape(g, p), ore.reshape(g, p, c), oim.reshape(g, p, c)


def _s5_kernel(x_ref, g_ref, wb_ref, wc_ref, ar_ref, ai_ref, d_ref, h0r_ref, h0i_ref,
               z_ref, hr_ref, hi_ref, *, n_t, tb, rb, m):
    i = pl.program_id(0)
    nch = wb_ref.shape[0]
    cw = wb_ref.shape[1]
    sw = wb_ref.shape[2] // 2
    tsb = m // rb
    nsl = rb // SUBLANES
    n_sub = jnp.minimum(n_t - i * tb, tb) // tsb

    @pl.when(i == 0)
    def _():
        hr_ref[...] = h0r_ref[...]
        hi_ref[...] = h0i_ref[...]

    @pl.when(n_sub < tb // tsb)
    def _():
        z_ref[...] = jnp.zeros_like(z_ref)

    def sub_block(j, carry):
        r0 = pl.multiple_of(j * m, m)
        xn = _rms(x_ref[pl.ds(r0, m), :], g_ref[...])
        xnb = xn.astype(BF16)
        drive = [_dot(xnb[:, c * cw:(c + 1) * cw], wb_ref[c]) for c in range(nch)]
        for c in range(nch):
            cs = slice(c * cw, (c + 1) * cw)
            ss = slice(c * sw, (c + 1) * sw)
            u = xn[:, cs]
            bu = drive[c]
            ar = jnp.broadcast_to(ar_ref[:, ss], (SUBLANES, sw))
            ai = jnp.broadcast_to(ai_ref[:, ss], (SUBLANES, sw))
            out_r = [None] * (tsb * nsl)
            out_i = [None] * (tsb * nsl)
            for s in range(nsl):
                srow = slice(s * SUBLANES, (s + 1) * SUBLANES)
                hr, hi = hr_ref[srow, ss], hi_ref[srow, ss]
                for t in range(tsb):
                    lo = t * rb + s * SUBLANES
                    hr, hi = (ar * hr - ai * hi + bu[lo:lo + SUBLANES, :sw],
                              ar * hi + ai * hr + bu[lo:lo + SUBLANES, sw:])
                    out_r[t * nsl + s] = hr
                    out_i[t * nsl + s] = hi
                hr_ref[srow, ss] = hr
                hi_ref[srow, ss] = hi
            hcat = jnp.concatenate([jnp.concatenate(out_r, axis=0).astype(BF16),
                                    jnp.concatenate(out_i, axis=0).astype(BF16)], axis=1)
            y = _dot(hcat, wc_ref[c]) + d_ref[:, cs] * u
            z_ref[pl.ds(r0, m), cs] = jax.nn.gelu(y).astype(z_ref.dtype)
        return carry

    lax.fori_loop(0, n_sub, sub_block, 0)


def s5(x, n_t, g, wb, wc, li, a_re, a_im, dskip, h0r, h0i, tb, rb):
    n_rows, d = x.shape
    ns = h0r.shape[1]
    rows = tb * rb
    m = max(k for k in range(rb, S5_MAX_SUB_ROWS + 1, rb)
            if rows % k == 0 and n_t % (k // rb) == 0 and k % (2 * SUBLANES) == 0)
    return pl.pallas_call(
        functools.partial(_s5_kernel, n_t=n_t, tb=tb, rb=rb, m=m),
        out_shape=(
            jax.ShapeDtypeStruct((n_rows, d), BF16),
            jax.ShapeDtypeStruct((rb, ns), F32),
            jax.ShapeDtypeStruct((rb, ns), F32),
        ),
        grid=(n_rows // rows,),
        in_specs=[
            pl.BlockSpec((rows, d), lambda i: (i, 0)),
            _full((1, d)),
            pl.BlockSpec((None,) + wb.shape[1:], lambda i: (li, 0, 0, 0)),
            pl.BlockSpec((None,) + wc.shape[1:], lambda i: (li, 0, 0, 0)),
            _full((1, ns)), _full((1, ns)), _full((1, d)),
            _full((rb, ns)), _full((rb, ns)),
        ],
        out_specs=(
            pl.BlockSpec((rows, d), lambda i: (i, 0)),
            _full((rb, ns)), _full((rb, ns)),
        ),
        compiler_params=_params("arbitrary"),
        name="s5",
    )(x, g.reshape(1, d), wb, wc, a_re, a_im, dskip.reshape(1, d), h0r, h0i)


def _s5_block_diag_in(b):
    g, p, c = b.shape
    gpc = V7X_MXU_DIM // c
    bt = b.transpose(0, 2, 1).reshape(g // gpc, gpc, c, p)
    out = jnp.einsum('ngcp,gh->ngchp', bt, jnp.eye(gpc, dtype=b.dtype))
    return out.reshape(g // gpc, gpc * c, gpc * p)


def _s5_block_diag_out(cm):
    g, c, p = cm.shape
    gpc = V7X_MXU_DIM // c
    ct = cm.transpose(0, 2, 1).reshape(g // gpc, gpc, p, c)
    out = jnp.einsum('ngpc,gh->ngphc', ct, jnp.eye(gpc, dtype=cm.dtype))
    return out.reshape(g // gpc, gpc * p, gpc * c)


def _moe_kernel(x_ref, g_ref, wrt_ref, wgu_ref, wdn_ref, o_ref, xn_ref, pos_ref, gate_ref, *, sub):
    e = pl.program_id(1)
    tm = x_ref.shape[0]
    ne = wrt_ref.shape[0]
    dff = wdn_ref.shape[1]

    @pl.when(e == 0)
    def _():
        x = x_ref[...]
        xn = _rms(x, g_ref[...])
        xn_hi = xn.astype(BF16)
        xn_ref[...] = xn_hi
        xn_lo = (xn - xn_hi.astype(F32)).astype(BF16)
        wr = wrt_ref[...]
        wr_hi = wr.astype(BF16)
        wr_lo = (wr - wr_hi.astype(F32)).astype(BF16)
        nt = lambda a, b: lax.dot_general(a, b, (((1,), (1,)), ((), ())), preferred_element_type=F32)
        logits = nt(wr_hi, xn_hi) + (nt(wr_hi, xn_lo) + nt(wr_lo, xn_hi))
        ex = jnp.exp(logits - jnp.max(logits, axis=0, keepdims=True))
        probs = ex / jnp.sum(ex, axis=0, keepdims=True)
        eid = lax.broadcasted_iota(jnp.int32, (ne, tm), 0).astype(F32)
        m1 = jnp.max(probs, axis=0, keepdims=True)
        i1 = jnp.min(jnp.where(probs == m1, eid, float(ne)), axis=0, keepdims=True)
        sel1 = eid == i1
        rest = jnp.where(sel1, -1.0, probs)
        m2 = jnp.max(rest, axis=0, keepdims=True)
        i2 = jnp.min(jnp.where(rest == m2, eid, float(ne)), axis=0, keepdims=True)
        sel2 = eid == i2
        den = m1 + m2
        gate_ref[...] = jnp.where(sel1, m1 / den, 0.0) + jnp.where(sel2, m2 / den, 0.0)
        chosen = jnp.where(sel1, 1.0, jnp.where(sel2, 1.0, 0.0))
        r = lax.broadcasted_iota(jnp.int32, (tm, tm), 0)
        c = lax.broadcasted_iota(jnp.int32, (tm, tm), 1)
        before = jnp.where(r < c, 1.0, 0.0).astype(BF16)
        rank = _dot(chosen.astype(BF16), before)
        pos_ref[...] = jnp.where(chosen > 0.0, rank, -1.0)
        o_ref[...] = x

    pos_e = pos_ref[pl.ds(e, 1), :]
    gate_e = gate_ref[pl.ds(e, 1), :]
    cnt = jnp.sum(jnp.where(pos_e >= 0.0, 1.0, 0.0)).astype(jnp.int32)

    def run_block(base, rows):
        slot = lax.broadcasted_iota(jnp.int32, (rows, tm), 0) + base
        hit = pos_e == slot.astype(F32)
        onehot = jnp.where(hit, 1.0, 0.0).astype(BF16)
        xs = _dot(onehot, xn_ref[...]).astype(BF16)
        gs = jnp.sum(jnp.where(hit, gate_e, 0.0), axis=1, keepdims=True)
        hgu = _dot(xs, wgu_ref[0])
        act = (jax.nn.silu(hgu[:, :dff]) * hgu[:, dff:]).astype(BF16)
        yb = (_dot(act, wdn_ref[0]) * gs).astype(BF16)
        o_ref[...] += lax.dot_general(onehot, yb, (((0,), (0,)), ((), ())), preferred_element_type=F32)

    def body(s, carry):
        run_block(s * sub, sub)
        return carry

    n_full = cnt // sub
    lax.fori_loop(0, n_full, body, 0)
    rem = cnt - n_full * sub
    lo = 0
    for rows in MOE_TAIL_ROWS:
        pl.when((rem > lo) & (rem <= rows))(functools.partial(run_block, n_full * sub, rows))
        lo = rows


def moe_residual(x, g, w_router_t, w_gu, w_down, li, tm, sub):
    n, d = x.shape
    _, ne, _, dff2 = w_gu.shape
    assert MOE_TAIL_ROWS[-1] == sub
    return pl.pallas_call(
        functools.partial(_moe_kernel, sub=sub),
        out_shape=jax.ShapeDtypeStruct((n, d), F32),
        grid=(n // tm, ne),
        in_specs=[
            pl.BlockSpec((tm, d), lambda i, e: (i, 0)),
            _full((1, d)),
            _full((ne, d)),
            pl.BlockSpec((None, 1, d, dff2), lambda i, e: (li, e, 0, 0)),
            pl.BlockSpec((None, 1, dff2 // 2, d), lambda i, e: (li, e, 0, 0)),
        ],
        out_specs=pl.BlockSpec((tm, d), lambda i, e: (i, 0)),
        scratch_shapes=[
            pltpu.VMEM((tm, d), BF16),
            pltpu.VMEM((ne, tm), F32),
            pltpu.VMEM((ne, tm), F32),
        ],
        compiler_params=_params("parallel", "arbitrary"),
        name="moe",
    )(x, g.reshape(1, d), w_router_t, w_gu, w_down)


def _trunk(x, n_t, rb, tb, nb, h0, conv0, s0, re0, im0, w):
    n, d = x.shape
    depth = w['norm_mix'].shape[0]
    d_a = h0.shape[-1]
    nh, dk = s0.shape[2], s0.shape[3]
    d_b = nh * dk
    ns = re0.shape[-2] * re0.shape[-1]
    tm = tb * rb
    n_even = (depth + 1) // 2
    new = {k: [] for k in ('h', 'conv', 're', 'im')}
    s_stack = None
    for l in range(depth):
        li = l // 2
        if l % 2 == 0:
            proj = norm_matmul(x, w['norm_mix'][l], w['even_w_in'], li, tm, w['even_w_in'].shape[2])
            conv_tm = conv0[li].transpose(1, 0, 2).reshape((CONV_W - 1) * rb, d_a)
            ya, h_new, conv_new = rglru(proj, n_t, conv_tm, h0[li], w['rglru_conv_w'][li], w['rglru_conv_b'][li],
                                        w['rglru_wg'][li], w['rglru_bg'][li], w['rglru_lambda'][li], tb=tb, rb=rb)
            new['h'].append(h_new)
            new['conv'].append(conv_new.reshape(CONV_W - 1, rb, d_a).transpose(1, 0, 2))
            ob, s_stack = hgrn2(proj, s0, li % s0.shape[0], s_stack, n_even,
                                w['hgrn2_lb_raw'], w['hgrn2_gnorm'][li], li, n_t=n_t, tb=tb, nb=nb)
            if nb != rb:
                ob = ob.reshape(rb // nb, n // rb, nb, d_b).transpose(1, 0, 2, 3).reshape(n, d_b)
            x = mix_ffn(ya, ob, x, w['even_w_out'], w['norm_ffn'][l], w['ffn_w_gu'], w['ffn_w_down'], li, tm,
                        w['ffn_w_down'].shape[1] // 2)
        else:
            z, re_new, im_new = s5(x, n_t, w['norm_mix'][l], w['s5_wb'], w['s5_wc'], li, w['s5_a_re'][li],
                                   w['s5_a_im'][li], w['s5_d'][li], re0[li].reshape(rb, ns), im0[li].reshape(rb, ns),
                                   tb=tb, rb=rb)
            new['re'].append(re_new.reshape(re0.shape[1:]))
            new['im'].append(im_new.reshape(im0.shape[1:]))
            x = glu_residual(z, w['s5_w_glu'], li, x, tm, w['s5_w_glu'].shape[2] // 2)
            x = moe_residual(x, w['norm_ffn'][l], w['moe_w_router_t'][li], w['moe_w_gu'], w['moe_w_down'], li,
                             tm, V7X_MXU_DIM)
    stack = lambda k: jnp.stack(new[k])
    return x, (stack('h'), stack('conv'), s_stack, stack('re'), stack('im'))


def kernel(x_prompt, x_sample, state_rglru_h, state_rglru_conv, state_hgrn2, state_s5_re, state_s5_im,
           meta_tokens, norm_mix, norm_ffn, norm_final, even_w_in, even_w_out,
           rglru_conv_w, rglru_conv_b, rglru_w_a, rglru_b_a, rglru_w_x, rglru_b_x, rglru_lambda,
           hgrn2_lb_raw, hgrn2_gnorm, s5_lam_re, s5_lam_im, s5_log_dt, s5_b_re, s5_b_im, s5_c_re, s5_c_im,
           s5_d, s5_w_glu, ffn_w_gu, ffn_w_down, moe_w_router, moe_w_gu, moe_w_down):
    bp, tp0, d = x_prompt.shape
    bs, ts, _ = x_sample.shape
    tp = tp0 + N_META
    d_a = state_rglru_h.shape[-1]
    n_even, n_odd = state_rglru_h.shape[0], state_s5_re.shape[0]
    assert bp == SUBLANES and bs % SUBLANES == 0
    tb_p = PROMPT_BLOCK_ROWS // bp
    tp_pad = -(-tp // tb_p) * tb_p
    nb_s = HGRN_ROWS // ts
    assert (tp % (HGRN_ROWS // bp) == 0 and tb_p % (HGRN_ROWS // bp) == 0 and bs % nb_s == 0
            and nb_s * ts == HGRN_ROWS and nb_s % SUBLANES == 0)

    eye_a = jnp.eye(H_A, dtype=F32)
    block_diag = lambda m: jnp.einsum('lhij,hg->lhigj', m, eye_a).reshape(n_even, d_a, d_a)
    s5_g, s5_p, s5_c = s5_b_re.shape[1:]
    lg = n_odd * s5_g
    a_re, a_im, bt_re, bt_im = s5_prep(s5_lam_re.reshape(lg, s5_p), s5_lam_im.reshape(lg, s5_p), s5_log_dt.reshape(lg),
                                       s5_b_re.reshape(lg, s5_p, s5_c), s5_b_im.reshape(lg, s5_p, s5_c))
    s5_wb = jnp.concatenate([_s5_block_diag_in(bt_re.reshape(lg, s5_p, s5_c)),
                             _s5_block_diag_in(bt_im.reshape(lg, s5_p, s5_c))], axis=2).astype(BF16)
    s5_wc = jnp.concatenate([_s5_block_diag_out(s5_c_re.reshape(lg, s5_c, s5_p)),
                             -_s5_block_diag_out(s5_c_im.reshape(lg, s5_c, s5_p))], axis=1).astype(BF16)
    ns = s5_lam_re.shape[1] * s5_lam_re.shape[2]
    w = {
        'norm_mix': norm_mix, 'norm_ffn': norm_ffn, 'norm_final': norm_final,
        'even_w_in': even_w_in.astype(BF16), 'even_w_out': even_w_out.astype(BF16),
        'rglru_conv_w': rglru_conv_w, 'rglru_conv_b': rglru_conv_b,
        'rglru_wg': jnp.concatenate([block_diag(rglru_w_a), block_diag(rglru_w_x)], axis=2).astype(BF16),
        'rglru_bg': jnp.concatenate([rglru_b_a, rglru_b_x], axis=1),
        'rglru_lambda': rglru_lambda, 'hgrn2_lb_raw': hgrn2_lb_raw, 'hgrn2_gnorm': hgrn2_gnorm,
        's5_a_re': a_re.reshape(n_odd, 1, ns), 's5_a_im': a_im.reshape(n_odd, 1, ns),
        's5_wb': s5_wb.reshape((n_odd, -1) + s5_wb.shape[1:]), 's5_wc': s5_wc.reshape((n_odd, -1) + s5_wc.shape[1:]),
        's5_d': s5_d, 's5_w_glu': s5_w_glu.astype(BF16),
        'ffn_w_gu': ffn_w_gu.astype(BF16), 'ffn_w_down': ffn_w_down.astype(BF16),
        'moe_w_router_t': moe_w_router.transpose(0, 2, 1),
        'moe_w_gu': moe_w_gu.astype(BF16), 'moe_w_down': moe_w_down.astype(BF16),
    }

    tc_p = HGRN_ROWS // bp
    assert N_META == tc_p
    xm = to_time_major(x_prompt, meta_tokens.astype(x_prompt.dtype), tp_pad, tb_p // tc_p)
    zero = lambda ref, lead: jnp.zeros((lead, bp) + ref.shape[2:], ref.dtype)
    xp, p_new = _trunk(xm, tp, bp, tb_p, bp, zero(state_rglru_h, n_even), zero(state_rglru_conv, n_even),
                       zero(state_hgrn2, 1), zero(state_s5_re, n_odd), zero(state_s5_im, n_odd), w)
    nk_out = max(k for k in range(1, HGRN_ROWS // tc_p + 1) if tp0 % (k * tc_p) == 0)
    y_prompt = rmsnorm_batch_major(xp, norm_final, bp, N_META, tp0, tc_p, nk_out)

    xs = x_sample.transpose(1, 0, 2).reshape(ts * bs, d)
    xs, s_new = _trunk(xs, ts, bs, ts, nb_s, state_rglru_h, state_rglru_conv, state_hgrn2, state_s5_re, state_s5_im, w)
    y_sample = rmsnorm_rows(xs, norm_final, ts * bs).reshape(ts, bs, d).transpose(1, 0, 2)

    refs = (state_rglru_h, state_rglru_conv, state_hgrn2, state_s5_re, state_s5_im)
    cast = lambda new: tuple(a.astype(r.dtype) for a, r in zip(new, refs))
    return (y_prompt, y_sample) + cast(p_new) + cast(s_new)
```

```python
import functools

import jax
import jax.numpy as jnp
from jax import lax
from jax.experimental import pallas as pl
from jax.experimental.pallas import tpu as pltpu

F32 = jnp.float32
BF16 = jnp.bfloat16

EPS = 1e-6
N_META = 16
CONV_W = 4
RG_C = 8.0
H_A = 8
S5_P = 64

V7X_VMEM_LIMIT_BYTES = 56 * 1024 * 1024
SUBLANES = 8
LANES = 128
V7X_MXU_DIM = 256
MOE_TAIL_ROWS = (128, 160, 192, 224, 256, 288, 320)
HGRN_ROWS = 128
S5_MAX_SUB_ROWS = 384
HGRN_EXP_CLAMP = 80.0
PROMPT_BLOCK_ROWS = 768
SCAN_CARRY_ELEMS = 16 * SUBLANES * LANES


def _params(*sem):
    return pltpu.CompilerParams(dimension_semantics=sem, vmem_limit_bytes=V7X_VMEM_LIMIT_BYTES)


def _rms(x, g):
    ms = jnp.mean(x * x, axis=-1, keepdims=True)
    return x * lax.rsqrt(ms + EPS) * g


def _dot(a, b):
    return jnp.dot(a, b, preferred_element_type=F32)


def _full(shape):
    return pl.BlockSpec(shape, lambda *_: (0,) * len(shape))


def _norm_matmul_kernel(x_ref, g_ref, w_ref, o_ref, xn_ref):
    @pl.when(pl.program_id(1) == 0)
    def _():
        xn_ref[...] = _rms(x_ref[...], g_ref[...]).astype(BF16)

    o_ref[...] = _dot(xn_ref[...], w_ref[...]).astype(o_ref.dtype)


def norm_matmul(x, g, w, li, tm, tn):
    n, d = x.shape
    nout = w.shape[2]
    return pl.pallas_call(
        _norm_matmul_kernel,
        out_shape=jax.ShapeDtypeStruct((n, nout), BF16),
        grid=(n // tm, nout // tn),
        in_specs=[
            pl.BlockSpec((tm, d), lambda i, j: (i, 0)),
            _full((1, d)),
            pl.BlockSpec((None, d, tn), lambda i, j: (li, 0, j)),
        ],
        out_specs=pl.BlockSpec((tm, tn), lambda i, j: (i, j)),
        scratch_shapes=[pltpu.VMEM((tm, d), BF16)],
        compiler_params=_params("parallel", "arbitrary"),
        name="norm_matmul",
    )(x, g.reshape(1, d), w)


def _mix_ffn_kernel(ya_ref, ob_ref, x_ref, wo_ref, g_ref, wg_ref, wu_ref, wdn_ref, o_ref, xn_ref):
    j = pl.program_id(1)
    d_a = ya_ref.shape[1]

    @pl.when(j == 0)
    def _():
        x1 = x_ref[...] + _dot(ya_ref[...], wo_ref[:d_a, :]) + _dot(ob_ref[...], wo_ref[d_a:, :])
        xn_ref[...] = _rms(x1, g_ref[...]).astype(BF16)
        o_ref[...] = x1

    xn = xn_ref[...]
    h = (jax.nn.silu(_dot(xn, wg_ref[...])) * _dot(xn, wu_ref[...])).astype(BF16)
    o_ref[...] += _dot(h, wdn_ref[...])


def mix_ffn(ya, ob, x, w_out, g, w_gu, w_down, li, tm, chunk):
    n, d = x.shape
    dff = w_down.shape[1]
    assert dff % chunk == 0 and chunk % LANES == 0
    nj = dff // chunk
    rows = lambda a: pl.BlockSpec((tm, a.shape[1]), lambda i, j: (i, 0))
    return pl.pallas_call(
        _mix_ffn_kernel,
        out_shape=jax.ShapeDtypeStruct((n, d), F32),
        grid=(n // tm, nj),
        in_specs=[
            rows(ya), rows(ob), rows(x),
            pl.BlockSpec((None,) + w_out.shape[1:], lambda i, j: (li, 0, 0)),
            _full((1, d)),
            pl.BlockSpec((None, d, chunk), lambda i, j: (li, 0, j)),
            pl.BlockSpec((None, d, chunk), lambda i, j: (li, 0, j + nj)),
            pl.BlockSpec((None, chunk, d), lambda i, j: (li, j, 0)),
        ],
        out_specs=pl.BlockSpec((tm, d), lambda i, j: (i, 0)),
        scratch_shapes=[pltpu.VMEM((tm, d), BF16)],
        compiler_params=_params("parallel", "arbitrary"),
        name="mix_ffn",
    )(ya, ob, x, w_out, g.reshape(1, d), w_gu, w_gu, w_down)


def _glu_residual_kernel(z_ref, wv_ref, wg_ref, r_ref, o_ref):
    z = z_ref[...]
    val = _dot(z, wv_ref[...])
    gate = _dot(z, wg_ref[...])
    o_ref[...] = r_ref[...] + val * jax.nn.sigmoid(gate)


def glu_residual(z, w_glu, li, res, tm, tn):
    n, d = z.shape
    dout = w_glu.shape[2] // 2
    nj = dout // tn
    return pl.pallas_call(
        _glu_residual_kernel,
        out_shape=jax.ShapeDtypeStruct((n, dout), F32),
        grid=(n // tm, nj),
        in_specs=[
            pl.BlockSpec((tm, d), lambda i, j: (i, 0)),
            pl.BlockSpec((None, d, tn), lambda i, j: (li, 0, j)),
            pl.BlockSpec((None, d, tn), lambda i, j: (li, 0, j + nj)),
            pl.BlockSpec((tm, tn), lambda i, j: (i, j)),
        ],
        out_specs=pl.BlockSpec((tm, tn), lambda i, j: (i, j)),
        compiler_params=_params("parallel", "arbitrary"),
        name="glu_residual",
    )(z, w_glu, w_glu, res)


def _rmsnorm_kernel(x_ref, g_ref, o_ref):
    o_ref[...] = _rms(x_ref[...], g_ref[...])


def rmsnorm_rows(x, g, tm):
    n, d = x.shape
    return pl.pallas_call(
        _rmsnorm_kernel,
        out_shape=jax.ShapeDtypeStruct((n, d), F32),
        grid=(n // tm,),
        in_specs=[pl.BlockSpec((tm, d), lambda i: (i, 0)), _full((1, d))],
        out_specs=pl.BlockSpec((tm, d), lambda i: (i, 0)),
        compiler_params=_params("parallel"),
        name="final_rmsnorm",
    )(x, g.reshape(1, d))


def _rmsnorm_batch_major_kernel(*refs, nk, tc, nb):
    x_refs, g_ref, o_ref = refs[:nk], refs[nk], refs[nk + 1]
    d = o_ref.shape[-1]
    for k in range(nk):
        y = _rms(x_refs[k][...], g_ref[...])
        o_ref[:, k * tc:(k + 1) * tc, :] = jnp.swapaxes(y.reshape(tc, nb, d), 0, 1)


def rmsnorm_batch_major(x, g, nb, t_skip, t_out, tc, nk):
    n, d = x.shape
    rows = tc * nb
    assert t_skip % tc == 0 and t_out % (nk * tc) == 0
    specs = [pl.BlockSpec((rows, d), lambda j, k=k: (nk * j + t_skip // tc + k, 0)) for k in range(nk)]
    return pl.pallas_call(
        functools.partial(_rmsnorm_batch_major_kernel, nk=nk, tc=tc, nb=nb),
        out_shape=jax.ShapeDtypeStruct((nb, t_out, d), F32),
        grid=(t_out // (nk * tc),),
        in_specs=specs + [_full((1, d))],
        out_specs=pl.BlockSpec((nb, nk * tc, d), lambda j: (0, j, 0)),
        compiler_params=_params("parallel"),
        name="final_rmsnorm_batch_major",
    )(*([x] * nk), g.reshape(1, d))


def _to_time_major_kernel(*refs, nk, n_chunks):
    x_refs, lead_ref, o_ref = refs[:nk], refs[nk], refs[nk + 1]
    i = pl.program_id(0)
    nb, tc, d = x_refs[0].shape
    rows = tc * nb
    for k in range(nk):
        chunk = i * nk + k - 1
        val = jnp.swapaxes(x_refs[k][...], 0, 1).reshape(rows, d)
        if k == 0:
            lead = jnp.broadcast_to(lead_ref[...][:, None, :], (tc, nb, d)).reshape(rows, d)
            val = jnp.where(i == 0, lead, val)
        o_ref[k * rows:(k + 1) * rows, :] = jnp.where(chunk < n_chunks, val, 0.0)


def to_time_major(x, lead, t_pad, nk):
    nb, t, d = x.shape
    tc = lead.shape[0]
    assert t % tc == 0 and t_pad % (nk * tc) == 0
    n_chunks = t // tc
    specs = [pl.BlockSpec((nb, tc, d), lambda i, k=k: (0, jnp.clip(i * nk + k - 1, 0, n_chunks - 1), 0))
             for k in range(nk)]
    return pl.pallas_call(
        functools.partial(_to_time_major_kernel, nk=nk, n_chunks=n_chunks),
        out_shape=jax.ShapeDtypeStruct((t_pad * nb, d), F32),
        grid=(t_pad // (nk * tc),),
        in_specs=specs + [_full((tc, d))],
        out_specs=pl.BlockSpec((nk * tc * nb, d), lambda i: (i, 0)),
        compiler_params=_params("parallel"),
        name="to_time_major",
    )(*([x] * nk), lead)


def _rglru_kernel(xa_ref, ga_ref, conv0_ref, h0_ref, cw_ref, cb_ref, wg_ref, bg_ref, lam_ref,
                  ya_ref, hlast_ref, convnew_ref, xpad_ref, a_ref, u_ref, *, n_t, tb, rb):
    i = pl.program_id(0)
    rows = tb * rb
    tail = (CONV_W - 1) * rb
    c = xa_ref.shape[-1]
    t_valid = jnp.minimum(n_t - i * tb, tb)

    @pl.when(i == 0)
    def _():
        xpad_ref[0:tail, :] = conv0_ref[...]
        hlast_ref[...] = h0_ref[...]

    @pl.when(i > 0)
    def _():
        xpad_ref[0:tail, :] = xpad_ref[rows:rows + tail, :]

    xpad_ref[tail:tail + rows, :] = xa_ref[...].astype(F32)
    xc = cb_ref[...]
    for k in range(CONV_W):
        xc = xc + cw_ref[k:k + 1, :] * xpad_ref[k * rb:k * rb + rows, :]

    gates = _dot(xc.astype(BF16), wg_ref[...]) + bg_ref[...]
    r = jax.nn.sigmoid(gates[:, :c])
    ig = jax.nn.sigmoid(gates[:, c:])
    log_a = (-RG_C) * r * jax.nn.softplus(-lam_ref[...])
    a = jnp.exp(log_a)
    mult = jnp.sqrt(1.0 - a * a)
    a_ref[...] = a
    u_ref[...] = mult * ig * xc

    lc = min(c, max(LANES, SCAN_CARRY_ELEMS // rb // LANES * LANES))
    for c0 in range(0, c, lc):
        def body(t, h, c0=c0):
            sl = pl.ds(pl.multiple_of(t * rb, rb), rb)
            h = a_ref[sl, c0:c0 + lc] * h + u_ref[sl, c0:c0 + lc]
            u_ref[sl, c0:c0 + lc] = h
            return h

        hlast_ref[:, c0:c0 + lc] = lax.fori_loop(0, t_valid, body, hlast_ref[:, c0:c0 + lc])

    live = lax.broadcasted_iota(jnp.int32, (rows, c), 0) < t_valid * rb
    ya_ref[...] = jnp.where(live, u_ref[...] * jax.nn.gelu(ga_ref[...].astype(F32)), 0.0).astype(ya_ref.dtype)
    convnew_ref[...] = xpad_ref[pl.ds(pl.multiple_of(t_valid * rb, rb), tail), :]


def rglru(proj, n_t, conv0, h0, cw, cb, wg, bg, lam, tb, rb):
    c = h0.shape[1]
    n_rows = proj.shape[0]
    rows = tb * rb
    tail = (CONV_W - 1) * rb
    return pl.pallas_call(
        functools.partial(_rglru_kernel, n_t=n_t, tb=tb, rb=rb),
        out_shape=(
            jax.ShapeDtypeStruct((n_rows, c), BF16),
            jax.ShapeDtypeStruct((rb, c), F32),
            jax.ShapeDtypeStruct((tail, c), F32),
        ),
        grid=(n_rows // rows,),
        in_specs=[
            pl.BlockSpec((rows, c), lambda i: (i, 0)),
            pl.BlockSpec((rows, c), lambda i: (i, 1)),
            _full((tail, c)), _full((rb, c)), _full((CONV_W, c)), _full((1, c)),
            _full((c, 2 * c)), _full((1, 2 * c)), _full((1, c)),
        ],
        out_specs=(
            pl.BlockSpec((rows, c), lambda i: (i, 0)),
            _full((rb, c)),
            _full((tail, c)),
        ),
        scratch_shapes=[
            pltpu.VMEM((rows + tail, c), F32),
            pltpu.VMEM((rows, c), F32),
            pltpu.VMEM((rows, c), F32),
        ],
        compiler_params=_params("arbitrary"),
        name="rglru",
    )(proj, proj, conv0, h0, cw, cb.reshape(1, c), wg, bg.reshape(1, 2 * c), lam.reshape(1, c))


def _hgrn2_kernel(q_ref, f_ref, v_ref, gb_ref, s0_ref, lbraw_ref, gn_ref, *rest, layer, n_t, tb, nb, nh):
    ob_ref, snew_ref, st_ref = rest[-3:]
    i = pl.program_id(1)
    m = HGRN_ROWS
    tc = m // nb
    d = q_ref.shape[-1]
    dk = d // nh
    n_chunks = jnp.minimum(n_t - i * tb, tb) // tc

    @pl.when(i == 0)
    def _():
        for b in range(nb):
            for h in range(nh):
                st_ref[h, :, b * dk:(b + 1) * dk] = s0_ref[b, h].T

    @pl.when(n_chunks < tb // tc)
    def _():
        ob_ref[...] = jnp.zeros_like(ob_ref)

    p = jax.nn.softmax(lbraw_ref[...], axis=0)
    cum = p[0:1, :]
    for r in range(1, layer + 1):
        cum = cum + p[r:r + 1, :]
    lb = cum - p[0:1, :]
    log_lb = jnp.log(lb)
    log_1mlb = jnp.log1p(-lb)

    row = lax.broadcasted_iota(jnp.int32, (m, m), 0)
    col = lax.broadcasted_iota(jnp.int32, (m, m), 1)
    same_seq_causal = jnp.where(((row & (nb - 1)) == (col & (nb - 1))) & (col <= row), 1.0, 0.0)
    bid = lax.broadcasted_iota(jnp.int32, (m, dk), 0) & (nb - 1)

    def chunk(c, carry):
        def rows_of(ref):
            if len(ref.shape) == 2:
                return ref[pl.ds(pl.multiple_of(c * m, m), m), :]
            return ref[pl.ds(c * tc, tc)].reshape(m, d)

        q = jax.nn.silu(rows_of(q_ref).astype(F32))
        fr = rows_of(f_ref).astype(F32)
        v = rows_of(v_ref).astype(BF16)
        gb = rows_of(gb_ref).astype(F32)
        logf = jnp.logaddexp(log_lb, log_1mlb + jax.nn.log_sigmoid(fr))
        k = 1.0 - jnp.exp(logf)
        slabs = [logf[0:nb]]
        for t in range(1, tc):
            slabs.append(slabs[-1] + logf[t * nb:(t + 1) * nb])
        g = jnp.concatenate(slabs, axis=0)
        g_last = jnp.concatenate([slabs[-1]] * tc, axis=0)
        qt = (q * jnp.exp(g)).astype(BF16)
        kt = (k * jnp.exp(jnp.minimum(-g, HGRN_EXP_CLAMP))).astype(BF16)
        ks = (k * jnp.exp(g_last - g)).astype(BF16)
        dec = jnp.exp(slabs[-1])

        outs = []
        for h in range(nh):
            hs = slice(h * dk, (h + 1) * dk)
            qt_h, kt_h, ks_h, v_h = qt[:, hs], kt[:, hs], ks[:, hs], v[:, hs]
            att = lax.dot_general(qt_h, kt_h, (((1,), (1,)), ((), ())), preferred_element_type=F32)
            att = (att * same_seq_causal).astype(BF16)
            o = _dot(att, v_h)
            zero = jnp.zeros_like(qt_h)
            expand = lambda x: jnp.concatenate([jnp.where(bid == b, x, zero) for b in range(nb)], axis=1)
            st_h = st_ref[h]
            o = o + lax.dot_general(expand(qt_h), st_h.astype(BF16), (((1,), (1,)), ((), ())),
                                    preferred_element_type=F32)
            dst = lax.dot_general(v_h, expand(ks_h), (((0,), (0,)), ((), ())), preferred_element_type=F32)
            dec_row = jnp.concatenate([dec[b:b + 1, hs] for b in range(nb)], axis=1)
            st_ref[h] = st_h * dec_row + dst
            outs.append(_rms(o, gn_ref[:, hs]))
        ob = jnp.concatenate(outs, axis=1) * jax.nn.silu(gb)
        ob_ref[pl.ds(pl.multiple_of(c * m, m), m), :] = ob.astype(ob_ref.dtype)
        return carry

    lax.fori_loop(0, n_chunks, chunk, 0)

    @pl.when(i == pl.num_programs(1) - 1)
    def _():
        for b in range(nb):
            for h in range(nh):
                snew_ref[b, h] = st_ref[h, :, b * dk:(b + 1) * dk].T


def hgrn2(proj, s0, s0_layer, s_stack, n_layers, lb_raw, gnorm, layer, n_t, tb, nb):
    _, bsz, nh, dk, _ = s0.shape
    d = nh * dk
    off = proj.shape[-1] // d - 4
    nbb = bsz // nb
    n_tpad = proj.shape[0] // bsz
    nt = n_tpad // tb
    if nb == bsz:
        col = lambda k: pl.BlockSpec((tb * nb, d), lambda j, i, k=k: (i, k + off))
    else:
        proj = proj.reshape(n_tpad, bsz, proj.shape[1])
        col = lambda k: pl.BlockSpec((tb, nb, d), lambda j, i, k=k: (i, j, k + off))
    in_specs = [
        col(0), col(1), col(2), col(3),
        pl.BlockSpec((None, nb, nh, dk, dk), lambda j, i: (s0_layer, j, 0, 0, 0)),
        _full(lb_raw.shape),
        _full((1, d)),
    ]
    args = [proj, proj, proj, proj, s0, lb_raw, gnorm.reshape(1, d)]
    aliases = {}
    if s_stack is not None:
        in_specs.append(pl.BlockSpec(memory_space=pl.ANY))
        args.append(s_stack)
        aliases = {len(args) - 1: 1}
    return pl.pallas_call(
        functools.partial(_hgrn2_kernel, layer=layer, n_t=n_t, tb=tb, nb=nb, nh=nh),
        out_shape=(
            jax.ShapeDtypeStruct((n_tpad * bsz, d), BF16),
            jax.ShapeDtypeStruct((n_layers, bsz, nh, dk, dk), F32),
        ),
        grid=(nbb, nt),
        in_specs=in_specs,
        out_specs=(
            pl.BlockSpec((tb * nb, d), lambda j, i: (j * nt + i, 0)),
            pl.BlockSpec((None, nb, nh, dk, dk), lambda j, i: (layer, j, 0, 0, 0)),
        ),
        scratch_shapes=[pltpu.VMEM((nh, dk, nb * dk), F32)],
        input_output_aliases=aliases,
        compiler_params=_params("arbitrary", "arbitrary"),
        name="hgrn2",
    )(*args)


def _s5_discretise(lr, li, ldt):
    dt = jnp.exp(ldt)
    mag = jnp.exp(lr * dt)
    return mag * jnp.cos(li * dt), mag * jnp.sin(li * dt)


def _s5_prep_kernel(lr_ref, li_ref, ldt_ref, lrc_ref, lic_ref, ldtc_ref, bre_ref, bim_ref,
                    ar_ref, ai_ref, ore_ref, oim_ref):
    ar_ref[...], ai_ref[...] = _s5_discretise(lr_ref[...], li_ref[...], ldt_ref[...])
    lr = lrc_ref[...]
    li = lic_ref[...]
    ar, ai = _s5_discretise(lr, li, ldtc_ref[...])
    den = lr * lr + li * li
    cr = ((ar - 1.0) * lr + ai * li) / den
    ci = (ai * lr - (ar - 1.0) * li) / den
    ore_ref[...] = cr * bre_ref[...] - ci * bim_ref[...]
    oim_ref[...] = cr * bim_ref[...] + ci * bre_ref[...]


def s5_prep(lam_re, lam_im, log_dt, b_re, b_im):
    g, p, c = b_re.shape
    dense = lambda a: a.reshape(-1, LANES)
    per_state = lambda a: dense(jnp.broadcast_to(a.reshape(g, -1, 1), (g, p, 1)))
    per_coef = lambda a: dense(jnp.broadcast_to(a.reshape(g, -1, 1), (g, p, c)))
    small = jax.ShapeDtypeStruct((g * p // LANES, LANES), F32)
    big = jax.ShapeDtypeStruct((g * p * c // LANES, LANES), F32)
    ar, ai, ore, oim = pl.pallas_call(
        _s5_prep_kernel, out_shape=(small, small, big, big), name="s5_prep",
    )(per_state(lam_re), per_state(lam_im), per_state(log_dt), per_coef(lam_re), per_coef(lam_im), per_coef(log_dt),
      dense(b_re), dense(b_im))
    return ar.reshape(g, p), ai.reshape(g, p), ore.reshape(g, p, c), oim.reshape(g, p, c)


def _s5_kernel(x_ref, g_ref, wb_ref, wc_ref, ar_ref, ai_ref, d_ref, h0r_ref, h0i_ref,
               z_ref, hr_ref, hi_ref, *, n_t, tb, rb, m):
    i = pl.program_id(0)
    nch = wb_ref.shape[0]
    cw = wb_ref.shape[1]
    sw = wb_ref.shape[2] // 2
    tsb = m // rb
    nsl = rb // SUBLANES
    n_sub = jnp.minimum(n_t - i * tb, tb) // tsb

    @pl.when(i == 0)
    def _():
        hr_ref[...] = h0r_ref[...]
        hi_ref[...] = h0i_ref[...]

    @pl.when(n_sub < tb // tsb)
    def _():
        z_ref[...] = jnp.zeros_like(z_ref)

    def sub_block(j, carry):
        r0 = pl.multiple_of(j * m, m)
        xn = _rms(x_ref[pl.ds(r0, m), :], g_ref[...])
        xnb = xn.astype(BF16)
        drive = [_dot(xnb[:, c * cw:(c + 1) * cw], wb_ref[c]) for c in range(nch)]
        for c in range(nch):
            cs = slice(c * cw, (c + 1) * cw)
            ss = slice(c * sw, (c + 1) * sw)
            u = xn[:, cs]
            bu = drive[c]
            ar = jnp.broadcast_to(ar_ref[:, ss], (SUBLANES, sw))
            ai = jnp.broadcast_to(ai_ref[:, ss], (SUBLANES, sw))
            out_r = [None] * (tsb * nsl)
            out_i = [None] * (tsb * nsl)
            for s in range(nsl):
                srow = slice(s * SUBLANES, (s + 1) * SUBLANES)
                hr, hi = hr_ref[srow, ss], hi_ref[srow, ss]
                for t in range(tsb):
                    lo = t * rb + s * SUBLANES
                    hr, hi = (ar * hr - ai * hi + bu[lo:lo + SUBLANES, :sw],
                              ar * hi + ai * hr + bu[lo:lo + SUBLANES, sw:])
                    out_r[t * nsl + s] = hr
                    out_i[t * nsl + s] = hi
                hr_ref[srow, ss] = hr
                hi_ref[srow, ss] = hi
            hcat = jnp.concatenate([jnp.concatenate(out_r, axis=0).astype(BF16),
                                    jnp.concatenate(out_i, axis=0).astype(BF16)], axis=1)
            y = _dot(hcat, wc_ref[c]) + d_ref[:, cs] * u
            z_ref[pl.ds(r0, m), cs] = jax.nn.gelu(y).astype(z_ref.dtype)
        return carry

    lax.fori_loop(0, n_sub, sub_block, 0)


def s5(x, n_t, g, wb, wc, li, a_re, a_im, dskip, h0r, h0i, tb, rb):
    n_rows, d = x.shape
    ns = h0r.shape[1]
    rows = tb * rb
    m = max(k for k in range(rb, S5_MAX_SUB_ROWS + 1, rb)
            if rows % k == 0 and n_t % (k // rb) == 0 and k % (2 * SUBLANES) == 0)
    return pl.pallas_call(
        functools.partial(_s5_kernel, n_t=n_t, tb=tb, rb=rb, m=m),
        out_shape=(
            jax.ShapeDtypeStruct((n_rows, d), BF16),
            jax.ShapeDtypeStruct((rb, ns), F32),
            jax.ShapeDtypeStruct((rb, ns), F32),
        ),
        grid=(n_rows // rows,),
        in_specs=[
            pl.BlockSpec((rows, d), lambda i: (i, 0)),
            _full((1, d)),
            pl.BlockSpec((None,) + wb.shape[1:], lambda i: (li, 0, 0, 0)),
            pl.BlockSpec((None,) + wc.shape[1:], lambda i: (li, 0, 0, 0)),
            _full((1, ns)), _full((1, ns)), _full((1, d)),
            _full((rb, ns)), _full((rb, ns)),
        ],
        out_specs=(
            pl.BlockSpec((rows, d), lambda i: (i, 0)),
            _full((rb, ns)), _full((rb, ns)),
        ),
        compiler_params=_params("arbitrary"),
        name="s5",
    )(x, g.reshape(1, d), wb, wc, a_re, a_im, dskip.reshape(1, d), h0r, h0i)


def _s5_block_diag_in(b):
    g, p, c = b.shape
    gpc = V7X_MXU_DIM // c
    bt = b.transpose(0, 2, 1).reshape(g // gpc, gpc, c, p)
    out = jnp.einsum('ngcp,gh->ngchp', bt, jnp.eye(gpc, dtype=b.dtype))
    return out.reshape(g // gpc, gpc * c, gpc * p)


def _s5_block_diag_out(cm):
    g, c, p = cm.shape
    gpc = V7X_MXU_DIM // c
    ct = cm.transpose(0, 2, 1).reshape(g // gpc, gpc, p, c)
    out = jnp.einsum('ngpc,gh->ngphc', ct, jnp.eye(gpc, dtype=cm.dtype))
    return out.reshape(g // gpc, gpc * p, gpc * c)


def _moe_kernel(x_ref, g_ref, wrt_ref, wgu_ref, wdn_ref, o_ref, xn_ref, pos_ref, gate_ref, *, sub):
    e = pl.program_id(1)
    tm = x_ref.shape[0]
    ne = wrt_ref.shape[0]
    dff = wdn_ref.shape[1]

    @pl.when(e == 0)
    def _():
        x = x_ref[...]
        xn = _rms(x, g_ref[...])
        xn_hi = xn.astype(BF16)
        xn_ref[...] = xn_hi
        xn_lo = (xn - xn_hi.astype(F32)).astype(BF16)
        wr = wrt_ref[...]
        wr_hi = wr.astype(BF16)
        wr_lo = (wr - wr_hi.astype(F32)).astype(BF16)
        nt = lambda a, b: lax.dot_general(a, b, (((1,), (1,)), ((), ())), preferred_element_type=F32)
        logits = nt(wr_hi, xn_hi) + (nt(wr_hi, xn_lo) + nt(wr_lo, xn_hi))
        ex = jnp.exp(logits - jnp.max(logits, axis=0, keepdims=True))
        probs = ex / jnp.sum(ex, axis=0, keepdims=True)
        eid = lax.broadcasted_iota(jnp.int32, (ne, tm), 0).astype(F32)
        m1 = jnp.max(probs, axis=0, keepdims=True)
        i1 = jnp.min(jnp.where(probs == m1, eid, float(ne)), axis=0, keepdims=True)
        sel1 = eid == i1
        rest = jnp.where(sel1, -1.0, probs)
        m2 = jnp.max(rest, axis=0, keepdims=True)
        i2 = jnp.min(jnp.where(rest == m2, eid, float(ne)), axis=0, keepdims=True)
        sel2 = eid == i2
        den = m1 + m2
        gate_ref[...] = jnp.where(sel1, m1 / den, 0.0) + jnp.where(sel2, m2 / den, 0.0)
        chosen = jnp.where(sel1, 1.0, jnp.where(sel2, 1.0, 0.0))
        r = lax.broadcasted_iota(jnp.int32, (tm, tm), 0)
        c = lax.broadcasted_iota(jnp.int32, (tm, tm), 1)
        before = jnp.where(r < c, 1.0, 0.0).astype(BF16)
        rank = _dot(chosen.astype(BF16), before)
        pos_ref[...] = jnp.where(chosen > 0.0, rank, -1.0)
        o_ref[...] = x

    pos_e = pos_ref[pl.ds(e, 1), :]
    gate_e = gate_ref[pl.ds(e, 1), :]
    cnt = jnp.sum(jnp.where(pos_e >= 0.0, 1.0, 0.0)).astype(jnp.int32)

    def run_block(base, rows):
        slot = lax.broadcasted_iota(jnp.int32, (rows, tm), 0) + base
        hit = pos_e == slot.astype(F32)
        onehot = jnp.where(hit, 1.0, 0.0).astype(BF16)
        xs = _dot(onehot, xn_ref[...]).astype(BF16)
        gs = jnp.sum(jnp.where(hit, gate_e, 0.0), axis=1, keepdims=True)
        hgu = _dot(xs, wgu_ref[0])
        act = (jax.nn.silu(hgu[:, :dff]) * hgu[:, dff:]).astype(BF16)
        yb = (_dot(act, wdn_ref[0]) * gs).astype(BF16)
        o_ref[...] += lax.dot_general(onehot, yb, (((0,), (0,)), ((), ())), preferred_element_type=F32)

    def body(s, carry):
        run_block(s * sub, sub)
        return carry

    n_full = cnt // sub
    lax.fori_loop(0, n_full, body, 0)
    rem = cnt - n_full * sub
    lo = 0
    for rows in MOE_TAIL_ROWS:
        pl.when((rem > lo) & (rem <= rows))(functools.partial(run_block, n_full * sub, rows))
        lo = rows


def moe_residual(x, g, w_router_t, w_gu, w_down, li, tm, sub):
    n, d = x.shape
    _, ne, _, dff2 = w_gu.shape
    assert MOE_TAIL_ROWS[-1] == sub
    return pl.pallas_call(
        functools.partial(_moe_kernel, sub=sub),
        out_shape=jax.ShapeDtypeStruct((n, d), F32),
        grid=(n // tm, ne),
        in_specs=[
            pl.BlockSpec((tm, d), lambda i, e: (i, 0)),
            _full((1, d)),
            _full((ne, d)),
            pl.BlockSpec((None, 1, d, dff2), lambda i, e: (li, e, 0, 0)),
            pl.BlockSpec((None, 1, dff2 // 2, d), lambda i, e: (li, e, 0, 0)),
        ],
        out_specs=pl.BlockSpec((tm, d), lambda i, e: (i, 0)),
        scratch_shapes=[
            pltpu.VMEM((tm, d), BF16),
            pltpu.VMEM((ne, tm), F32),
            pltpu.VMEM((ne, tm), F32),
        ],
        compiler_params=_params("parallel", "arbitrary"),
        name="moe",
    )(x, g.reshape(1, d), w_router_t, w_gu, w_down)


def _trunk(x, n_t, rb, tb, nb, h0, conv0, s0, re0, im0, w):
    n, d = x.shape
    depth = w['norm_mix'].shape[0]
    d_a = h0.shape[-1]
    nh, dk = s0.shape[2], s0.shape[3]
    d_b = nh * dk
    ns = re0.shape[-2] * re0.shape[-1]
    tm = tb * rb
    n_even = (depth + 1) // 2
    new = {k: [] for k in ('h', 'conv', 're', 'im')}
    s_stack = None
    for l in range(depth):
        li = l // 2
        if l % 2 == 0:
            proj = norm_matmul(x, w['norm_mix'][l], w['even_w_in'], li, tm, w['even_w_in'].shape[2])
            conv_tm = conv0[li].transpose(1, 0, 2).reshape((CONV_W - 1) * rb, d_a)
            ya, h_new, conv_new = rglru(proj, n_t, conv_tm, h0[li], w['rglru_conv_w'][li], w['rglru_conv_b'][li],
                                        w['rglru_wg'][li], w['rglru_bg'][li], w['rglru_lambda'][li], tb=tb, rb=rb)
            new['h'].append(h_new)
            new['conv'].append(conv_new.reshape(CONV_W - 1, rb, d_a).transpose(1, 0, 2))
            ob, s_stack = hgrn2(proj, s0, li % s0.shape[0], s_stack, n_even,
                                w['hgrn2_lb_raw'], w['hgrn2_gnorm'][li], li, n_t=n_t, tb=tb, nb=nb)
            if nb != rb:
                ob = ob.reshape(rb // nb, n // rb, nb, d_b).transpose(1, 0, 2, 3).reshape(n, d_b)
            x = mix_ffn(ya, ob, x, w['even_w_out'], w['norm_ffn'][l], w['ffn_w_gu'], w['ffn_w_down'], li, tm,
                        w['ffn_w_down'].shape[1] // 2)
        else:
            z, re_new, im_new = s5(x, n_t, w['norm_mix'][l], w['s5_wb'], w['s5_wc'], li, w['s5_a_re'][li],
                                   w['s5_a_im'][li], w['s5_d'][li], re0[li].reshape(rb, ns), im0[li].reshape(rb, ns),
                                   tb=tb, rb=rb)
            new['re'].append(re_new.reshape(re0.shape[1:]))
            new['im'].append(im_new.reshape(im0.shape[1:]))
            x = glu_residual(z, w['s5_w_glu'], li, x, tm, w['s5_w_glu'].shape[2] // 2)
            x = moe_residual(x, w['norm_ffn'][l], w['moe_w_router_t'][li], w['moe_w_gu'], w['moe_w_down'], li,
                             tm, MOE_TAIL_ROWS[-1])
    stack = lambda k: jnp.stack(new[k])
    return x, (stack('h'), stack('conv'), s_stack, stack('re'), stack('im'))


def kernel(x_prompt, x_sample, state_rglru_h, state_rglru_conv, state_hgrn2, state_s5_re, state_s5_im,
           meta_tokens, norm_mix, norm_ffn, norm_final, even_w_in, even_w_out,
           rglru_conv_w, rglru_conv_b, rglru_w_a, rglru_b_a, rglru_w_x, rglru_b_x, rglru_lambda,
           hgrn2_lb_raw, hgrn2_gnorm, s5_lam_re, s5_lam_im, s5_log_dt, s5_b_re, s5_b_im, s5_c_re, s5_c_im,
           s5_d, s5_w_glu, ffn_w_gu, ffn_w_down, moe_w_router, moe_w_gu, moe_w_down):
    bp, tp0, d = x_prompt.shape
    bs, ts, _ = x_sample.shape
    tp = tp0 + N_META
    d_a = state_rglru_h.shape[-1]
    n_even, n_odd = state_rglru_h.shape[0], state_s5_re.shape[0]
    assert bp == SUBLANES and bs % SUBLANES == 0
    tb_p = PROMPT_BLOCK_ROWS // bp
    tp_pad = -(-tp // tb_p) * tb_p
    nb_s = HGRN_ROWS // ts
    assert (tp % (HGRN_ROWS // bp) == 0 and tb_p % (HGRN_ROWS // bp) == 0 and bs % nb_s == 0
            and nb_s * ts == HGRN_ROWS and nb_s % SUBLANES == 0)

    eye_a = jnp.eye(H_A, dtype=F32)
    block_diag = lambda m: jnp.einsum('lhij,hg->lhigj', m, eye_a).reshape(n_even, d_a, d_a)
    s5_g, s5_p, s5_c = s5_b_re.shape[1:]
    lg = n_odd * s5_g
    a_re, a_im, bt_re, bt_im = s5_prep(s5_lam_re.reshape(lg, s5_p), s5_lam_im.reshape(lg, s5_p), s5_log_dt.reshape(lg),
                                       s5_b_re.reshape(lg, s5_p, s5_c), s5_b_im.reshape(lg, s5_p, s5_c))
    s5_wb = jnp.concatenate([_s5_block_diag_in(bt_re.reshape(lg, s5_p, s5_c)),
                             _s5_block_diag_in(bt_im.reshape(lg, s5_p, s5_c))], axis=2).astype(BF16)
    s5_wc = jnp.concatenate([_s5_block_diag_out(s5_c_re.reshape(lg, s5_c, s5_p)),
                             -_s5_block_diag_out(s5_c_im.reshape(lg, s5_c, s5_p))], axis=1).astype(BF16)
    ns = s5_lam_re.shape[1] * s5_lam_re.shape[2]
    w = {
        'norm_mix': norm_mix, 'norm_ffn': norm_ffn, 'norm_final': norm_final,
        'even_w_in': even_w_in.astype(BF16), 'even_w_out': even_w_out.astype(BF16),
        'rglru_conv_w': rglru_conv_w, 'rglru_conv_b': rglru_conv_b,
        'rglru_wg': jnp.concatenate([block_diag(rglru_w_a), block_diag(rglru_w_x)], axis=2).astype(BF16),
        'rglru_bg': jnp.concatenate([rglru_b_a, rglru_b_x], axis=1),
        'rglru_lambda': rglru_lambda, 'hgrn2_lb_raw': hgrn2_lb_raw, 'hgrn2_gnorm': hgrn2_gnorm,
        's5_a_re': a_re.reshape(n_odd, 1, ns), 's5_a_im': a_im.reshape(n_odd, 1, ns),
        's5_wb': s5_wb.reshape((n_odd, -1) + s5_wb.shape[1:]), 's5_wc': s5_wc.reshape((n_odd, -1) + s5_wc.shape[1:]),
        's5_d': s5_d, 's5_w_glu': s5_w_glu.astype(BF16),
        'ffn_w_gu': ffn_w_gu.astype(BF16), 'ffn_w_down': ffn_w_down.astype(BF16),
        'moe_w_router_t': moe_w_router.transpose(0, 2, 1),
        'moe_w_gu': moe_w_gu.astype(BF16), 'moe_w_down': moe_w_down.astype(BF16),
    }

    tc_p = HGRN_ROWS // bp
    assert N_META == tc_p
    xm = to_time_major(x_prompt, meta_tokens.astype(x_prompt.dtype), tp_pad, tb_p // tc_p)
    zero = lambda ref, lead: jnp.zeros((lead, bp) + ref.shape[2:], ref.dtype)
    xp, p_new = _trunk(xm, tp, bp, tb_p, bp, zero(state_rglru_h, n_even), zero(state_rglru_conv, n_even),
                       zero(state_hgrn2, 1), zero(state_s5_re, n_odd), zero(state_s5_im, n_odd), w)
    nk_out = max(k for k in range(1, HGRN_ROWS // tc_p + 1) if tp0 % (k * tc_p) == 0)
    y_prompt = rmsnorm_batch_major(xp, norm_final, bp, N_META, tp0, tc_p, nk_out)

    xs = x_sample.transpose(1, 0, 2).reshape(ts * bs, d)
    xs, s_new = _trunk(xs, ts, bs, ts, nb_s, state_rglru_h, state_rglru_conv, state_hgrn2, state_s5_re, state_s5_im, w)
    y_sample = rmsnorm_rows(xs, norm_final, ts * bs).reshape(ts, bs, d).transpose(1, 0, 2)

    refs = (state_rglru_h, state_rglru_conv, state_hgrn2, state_s5_re, state_s5_im)
    cast = lambda new: tuple(a.astype(r.dtype) for a, r in zip(new, refs))
    return (y_prompt, y_sample) + cast(p_new) + cast(s_new)
```

```python
import functools

import jax
import jax.numpy as jnp
from jax import lax
from jax.experimental import pallas as pl
from jax.experimental.pallas import tpu as pltpu

F32 = jnp.float32
BF16 = jnp.bfloat16

EPS = 1e-6
N_META = 16
CONV_W = 4
RG_C = 8.0
H_A = 8
S5_P = 64

V7X_VMEM_LIMIT_BYTES = 56 * 1024 * 1024
SUBLANES = 8
LANES = 128
V7X_MXU_DIM = 256
MOE_TAIL_ROWS = (128, 160, 192, 224, 256, 288, 320)
TIME_CHUNK_ROWS = 128
HGRN_ROWS = 256
S5_MAX_SUB_ROWS = 384
HGRN_EXP_CLAMP = 80.0
PROMPT_BLOCK_ROWS = 768
SCAN_CARRY_ELEMS = 16 * SUBLANES * LANES


def _params(*sem):
    return pltpu.CompilerParams(dimension_semantics=sem, vmem_limit_bytes=V7X_VMEM_LIMIT_BYTES)


def _rms(x, g):
    ms = jnp.mean(x * x, axis=-1, keepdims=True)
    return x * lax.rsqrt(ms + EPS) * g


def _dot(a, b):
    return jnp.dot(a, b, preferred_element_type=F32)


def _full(shape):
    return pl.BlockSpec(shape, lambda *_: (0,) * len(shape))


def _norm_matmul_kernel(x_ref, g_ref, w_ref, o_ref, xn_ref):
    @pl.when(pl.program_id(1) == 0)
    def _():
        xn_ref[...] = _rms(x_ref[...], g_ref[...]).astype(BF16)

    o_ref[...] = _dot(xn_ref[...], w_ref[...]).astype(o_ref.dtype)


def norm_matmul(x, g, w, li, tm, tn):
    n, d = x.shape
    nout = w.shape[2]
    return pl.pallas_call(
        _norm_matmul_kernel,
        out_shape=jax.ShapeDtypeStruct((n, nout), BF16),
        grid=(n // tm, nout // tn),
        in_specs=[
            pl.BlockSpec((tm, d), lambda i, j: (i, 0)),
            _full((1, d)),
            pl.BlockSpec((None, d, tn), lambda i, j: (li, 0, j)),
        ],
        out_specs=pl.BlockSpec((tm, tn), lambda i, j: (i, j)),
        scratch_shapes=[pltpu.VMEM((tm, d), BF16)],
        compiler_params=_params("parallel", "arbitrary"),
        name="norm_matmul",
    )(x, g.reshape(1, d), w)


def _mix_ffn_kernel(ya_ref, ob_ref, x_ref, wo_ref, g_ref, wg_ref, wu_ref, wdn_ref, o_ref, xn_ref):
    j = pl.program_id(1)
    d_a = ya_ref.shape[1]

    @pl.when(j == 0)
    def _():
        x1 = x_ref[...] + _dot(ya_ref[...], wo_ref[:d_a, :]) + _dot(ob_ref[...], wo_ref[d_a:, :])
        xn_ref[...] = _rms(x1, g_ref[...]).astype(BF16)
        o_ref[...] = x1

    xn = xn_ref[...]
    h = (jax.nn.silu(_dot(xn, wg_ref[...])) * _dot(xn, wu_ref[...])).astype(BF16)
    o_ref[...] += _dot(h, wdn_ref[...])


def mix_ffn(ya, ob, x, w_out, g, w_gu, w_down, li, tm, chunk):
    n, d = x.shape
    dff = w_down.shape[1]
    assert dff % chunk == 0 and chunk % LANES == 0
    nj = dff // chunk
    rows = lambda a: pl.BlockSpec((tm, a.shape[1]), lambda i, j: (i, 0))
    return pl.pallas_call(
        _mix_ffn_kernel,
        out_shape=jax.ShapeDtypeStruct((n, d), F32),
        grid=(n // tm, nj),
        in_specs=[
            rows(ya), rows(ob), rows(x),
            pl.BlockSpec((None,) + w_out.shape[1:], lambda i, j: (li, 0, 0)),
            _full((1, d)),
            pl.BlockSpec((None, d, chunk), lambda i, j: (li, 0, j)),
            pl.BlockSpec((None, d, chunk), lambda i, j: (li, 0, j + nj)),
            pl.BlockSpec((None, chunk, d), lambda i, j: (li, j, 0)),
        ],
        out_specs=pl.BlockSpec((tm, d), lambda i, j: (i, 0)),
        scratch_shapes=[pltpu.VMEM((tm, d), BF16)],
        compiler_params=_params("parallel", "arbitrary"),
        name="mix_ffn",
    )(ya, ob, x, w_out, g.reshape(1, d), w_gu, w_gu, w_down)


def _glu_residual_kernel(z_ref, wv_ref, wg_ref, r_ref, o_ref):
    z = z_ref[...]
    val = _dot(z, wv_ref[...])
    gate = _dot(z, wg_ref[...])
    o_ref[...] = r_ref[...] + val * jax.nn.sigmoid(gate)


def glu_residual(z, w_glu, li, res, tm, tn):
    n, d = z.shape
    dout = w_glu.shape[2] // 2
    nj = dout // tn
    return pl.pallas_call(
        _glu_residual_kernel,
        out_shape=jax.ShapeDtypeStruct((n, dout), F32),
        grid=(n // tm, nj),
        in_specs=[
            pl.BlockSpec((tm, d), lambda i, j: (i, 0)),
            pl.BlockSpec((None, d, tn), lambda i, j: (li, 0, j)),
            pl.BlockSpec((None, d, tn), lambda i, j: (li, 0, j + nj)),
            pl.BlockSpec((tm, tn), lambda i, j: (i, j)),
        ],
        out_specs=pl.BlockSpec((tm, tn), lambda i, j: (i, j)),
        compiler_params=_params("parallel", "arbitrary"),
        name="glu_residual",
    )(z, w_glu, w_glu, res)


def _rmsnorm_kernel(x_ref, g_ref, o_ref):
    o_ref[...] = _rms(x_ref[...], g_ref[...])


def rmsnorm_rows(x, g, tm):
    n, d = x.shape
    return pl.pallas_call(
        _rmsnorm_kernel,
        out_shape=jax.ShapeDtypeStruct((n, d), F32),
        grid=(n // tm,),
        in_specs=[pl.BlockSpec((tm, d), lambda i: (i, 0)), _full((1, d))],
        out_specs=pl.BlockSpec((tm, d), lambda i: (i, 0)),
        compiler_params=_params("parallel"),
        name="final_rmsnorm",
    )(x, g.reshape(1, d))


def _rmsnorm_batch_major_kernel(*refs, nk, tc, nb):
    x_refs, g_ref, o_ref = refs[:nk], refs[nk], refs[nk + 1]
    d = o_ref.shape[-1]
    for k in range(nk):
        y = _rms(x_refs[k][...], g_ref[...])
        o_ref[:, k * tc:(k + 1) * tc, :] = jnp.swapaxes(y.reshape(tc, nb, d), 0, 1)


def rmsnorm_batch_major(x, g, nb, t_skip, t_out, tc, nk):
    n, d = x.shape
    rows = tc * nb
    assert t_skip % tc == 0 and t_out % (nk * tc) == 0
    specs = [pl.BlockSpec((rows, d), lambda j, k=k: (nk * j + t_skip // tc + k, 0)) for k in range(nk)]
    return pl.pallas_call(
        functools.partial(_rmsnorm_batch_major_kernel, nk=nk, tc=tc, nb=nb),
        out_shape=jax.ShapeDtypeStruct((nb, t_out, d), F32),
        grid=(t_out // (nk * tc),),
        in_specs=specs + [_full((1, d))],
        out_specs=pl.BlockSpec((nb, nk * tc, d), lambda j: (0, j, 0)),
        compiler_params=_params("parallel"),
        name="final_rmsnorm_batch_major",
    )(*([x] * nk), g.reshape(1, d))


def _to_time_major_kernel(*refs, nk, n_chunks):
    x_refs, lead_ref, o_ref = refs[:nk], refs[nk], refs[nk + 1]
    i = pl.program_id(0)
    nb, tc, d = x_refs[0].shape
    rows = tc * nb
    for k in range(nk):
        chunk = i * nk + k - 1
        val = jnp.swapaxes(x_refs[k][...], 0, 1).reshape(rows, d)
        if k == 0:
            lead = jnp.broadcast_to(lead_ref[...][:, None, :], (tc, nb, d)).reshape(rows, d)
            val = jnp.where(i == 0, lead, val)
        o_ref[k * rows:(k + 1) * rows, :] = jnp.where(chunk < n_chunks, val, 0.0)


def to_time_major(x, lead, t_pad, nk):
    nb, t, d = x.shape
    tc = lead.shape[0]
    assert t % tc == 0 and t_pad % (nk * tc) == 0
    n_chunks = t // tc
    specs = [pl.BlockSpec((nb, tc, d), lambda i, k=k: (0, jnp.clip(i * nk + k - 1, 0, n_chunks - 1), 0))
             for k in range(nk)]
    return pl.pallas_call(
        functools.partial(_to_time_major_kernel, nk=nk, n_chunks=n_chunks),
        out_shape=jax.ShapeDtypeStruct((t_pad * nb, d), F32),
        grid=(t_pad // (nk * tc),),
        in_specs=specs + [_full((tc, d))],
        out_specs=pl.BlockSpec((nk * tc * nb, d), lambda i: (i, 0)),
        compiler_params=_params("parallel"),
        name="to_time_major",
    )(*([x] * nk), lead)


def _rglru_kernel(xa_ref, ga_ref, conv0_ref, h0_ref, cw_ref, cb_ref, wg_ref, bg_ref, lam_ref,
                  ya_ref, hlast_ref, convnew_ref, xpad_ref, a_ref, u_ref, *, n_t, tb, rb):
    i = pl.program_id(0)
    rows = tb * rb
    tail = (CONV_W - 1) * rb
    c = xa_ref.shape[-1]
    t_valid = jnp.minimum(n_t - i * tb, tb)

    @pl.when(i == 0)
    def _():
        xpad_ref[0:tail, :] = conv0_ref[...]
        hlast_ref[...] = h0_ref[...]

    @pl.when(i > 0)
    def _():
        xpad_ref[0:tail, :] = xpad_ref[rows:rows + tail, :]

    xpad_ref[tail:tail + rows, :] = xa_ref[...].astype(F32)
    xc = cb_ref[...]
    for k in range(CONV_W):
        xc = xc + cw_ref[k:k + 1, :] * xpad_ref[k * rb:k * rb + rows, :]

    gates = _dot(xc.astype(BF16), wg_ref[...]) + bg_ref[...]
    r = jax.nn.sigmoid(gates[:, :c])
    ig = jax.nn.sigmoid(gates[:, c:])
    log_a = (-RG_C) * r * jax.nn.softplus(-lam_ref[...])
    a = jnp.exp(log_a)
    mult = jnp.sqrt(1.0 - a * a)
    a_ref[...] = a
    u_ref[...] = mult * ig * xc

    lc = min(c, max(LANES, SCAN_CARRY_ELEMS // rb // LANES * LANES))
    for c0 in range(0, c, lc):
        def body(t, h, c0=c0):
            sl = pl.ds(pl.multiple_of(t * rb, rb), rb)
            h = a_ref[sl, c0:c0 + lc] * h + u_ref[sl, c0:c0 + lc]
            u_ref[sl, c0:c0 + lc] = h
            return h

        hlast_ref[:, c0:c0 + lc] = lax.fori_loop(0, t_valid, body, hlast_ref[:, c0:c0 + lc])

    live = lax.broadcasted_iota(jnp.int32, (rows, c), 0) < t_valid * rb
    ya_ref[...] = jnp.where(live, u_ref[...] * jax.nn.gelu(ga_ref[...].astype(F32)), 0.0).astype(ya_ref.dtype)
    convnew_ref[...] = xpad_ref[pl.ds(pl.multiple_of(t_valid * rb, rb), tail), :]


def rglru(proj, n_t, conv0, h0, cw, cb, wg, bg, lam, tb, rb):
    c = h0.shape[1]
    n_rows = proj.shape[0]
    rows = tb * rb
    tail = (CONV_W - 1) * rb
    return pl.pallas_call(
        functools.partial(_rglru_kernel, n_t=n_t, tb=tb, rb=rb),
        out_shape=(
            jax.ShapeDtypeStruct((n_rows, c), BF16),
            jax.ShapeDtypeStruct((rb, c), F32),
            jax.ShapeDtypeStruct((tail, c), F32),
        ),
        grid=(n_rows // rows,),
        in_specs=[
            pl.BlockSpec((rows, c), lambda i: (i, 0)),
            pl.BlockSpec((rows, c), lambda i: (i, 1)),
            _full((tail, c)), _full((rb, c)), _full((CONV_W, c)), _full((1, c)),
            _full((c, 2 * c)), _full((1, 2 * c)), _full((1, c)),
        ],
        out_specs=(
            pl.BlockSpec((rows, c), lambda i: (i, 0)),
            _full((rb, c)),
            _full((tail, c)),
        ),
        scratch_shapes=[
            pltpu.VMEM((rows + tail, c), F32),
            pltpu.VMEM((rows, c), F32),
            pltpu.VMEM((rows, c), F32),
        ],
        compiler_params=_params("arbitrary"),
        name="rglru",
    )(proj, proj, conv0, h0, cw, cb.reshape(1, c), wg, bg.reshape(1, 2 * c), lam.reshape(1, c))


def _hgrn2_kernel(q_ref, f_ref, v_ref, gb_ref, s0_ref, lbraw_ref, gn_ref, stack_ref, ob_ref, snew_ref, st_ref,
                  *, layer, n_t, tb, nb, nh, m):
    del stack_ref
    i = pl.program_id(1)
    tc = m // nb
    half = tc // 2
    d = q_ref.shape[-1]
    dk = d // nh
    t_valid = jnp.minimum(n_t - i * tb, tb)
    n_chunks = (t_valid + tc - 1) // tc

    @pl.when(i == 0)
    def _():
        for b in range(nb):
            for h in range(nh):
                st_ref[h, :, b * dk:(b + 1) * dk] = s0_ref[b, h].T

    @pl.when(n_chunks < tb // tc)
    def _():
        ob_ref[...] = jnp.zeros_like(ob_ref)

    p = jax.nn.softmax(lbraw_ref[...], axis=0)
    cum = p[0:1, :]
    for r in range(1, layer + 1):
        cum = cum + p[r:r + 1, :]
    lb = cum - p[0:1, :]
    log_lb = jnp.log(lb)
    log_1mlb = jnp.log1p(-lb)

    assert tc % 2 == 0 and nb & (nb - 1) == 0
    row = lax.broadcasted_iota(jnp.int32, (m, m), 0)
    col = lax.broadcasted_iota(jnp.int32, (m, m), 1)
    same_seq_causal = jnp.where(((row & (nb - 1)) == (col & (nb - 1))) & (col <= row), 1.0, 0.0)
    bid = lax.broadcasted_iota(jnp.int32, (m, dk), 0) & (nb - 1)

    def chunk(c, carry):
        def rows_of(ref):
            if len(ref.shape) == 2:
                return ref[pl.ds(pl.multiple_of(c * m, m), m), :]
            return ref[pl.ds(c * tc, tc)].reshape(m, d)

        live = lax.broadcasted_iota(jnp.int32, (m, d), 0) < (t_valid - c * tc) * nb
        q = jax.nn.silu(rows_of(q_ref).astype(F32))
        fr = rows_of(f_ref).astype(F32)
        v = rows_of(v_ref).astype(BF16)
        gb = rows_of(gb_ref).astype(F32)
        logf = jnp.where(live, jnp.logaddexp(log_lb, log_1mlb + jax.nn.log_sigmoid(fr)), 0.0)
        k = 1.0 - jnp.exp(logf)
        slabs = [logf[0:nb]]
        for t in range(1, tc):
            slabs.append(slabs[-1] + logf[t * nb:(t + 1) * nb])
        g = jnp.concatenate(slabs, axis=0)
        g_last = jnp.concatenate([slabs[-1]] * tc, axis=0)
        g_rel = g - jnp.concatenate([slabs[half - 1]] * tc, axis=0)
        qa = (q * jnp.exp(jnp.minimum(g_rel, HGRN_EXP_CLAMP))).astype(BF16)
        kt = (k * jnp.exp(jnp.minimum(-g_rel, HGRN_EXP_CLAMP))).astype(BF16)
        qt = (q * jnp.exp(g)).astype(BF16)
        ks = (k * jnp.exp(g_last - g)).astype(BF16)
        dec = jnp.exp(slabs[-1])

        outs = []
        for h in range(nh):
            hs = slice(h * dk, (h + 1) * dk)
            qt_h, kt_h, ks_h, v_h = qt[:, hs], kt[:, hs], ks[:, hs], v[:, hs]
            att = lax.dot_general(qa[:, hs], kt_h, (((1,), (1,)), ((), ())), preferred_element_type=F32)
            att = (att * same_seq_causal).astype(BF16)
            o = _dot(att, v_h)
            zero = jnp.zeros_like(qt_h)
            expand = lambda x: jnp.concatenate([jnp.where(bid == b, x, zero) for b in range(nb)], axis=1)
            st_h = st_ref[h]
            o = o + lax.dot_general(expand(qt_h), st_h.astype(BF16), (((1,), (1,)), ((), ())),
                                    preferred_element_type=F32)
            dst = lax.dot_general(v_h, expand(ks_h), (((0,), (0,)), ((), ())), preferred_element_type=F32)
            dec_row = jnp.concatenate([dec[b:b + 1, hs] for b in range(nb)], axis=1)
            st_ref[h] = st_h * dec_row + dst
            outs.append(_rms(o, gn_ref[:, hs]))
        ob = jnp.where(live, jnp.concatenate(outs, axis=1) * jax.nn.silu(gb), 0.0)
        ob_ref[pl.ds(pl.multiple_of(c * m, m), m), :] = ob.astype(ob_ref.dtype)
        return carry

    lax.fori_loop(0, n_chunks, chunk, 0)

    @pl.when(i == pl.num_programs(1) - 1)
    def _():
        for b in range(nb):
            for h in range(nh):
                snew_ref[b, h] = st_ref[h, :, b * dk:(b + 1) * dk].T


def hgrn2(proj, s0, s0_layer, s_stack, n_layers, lb_raw, gnorm, layer, n_t, tb, nb):
    _, bsz, nh, dk, _ = s0.shape
    d = nh * dk
    off = proj.shape[-1] // d - 4
    nbb = bsz // nb
    n_tpad = proj.shape[0] // bsz
    nt = n_tpad // tb
    m = max(k for k in range(2 * nb, HGRN_ROWS + 1, 2 * nb) if (tb * nb) % k == 0)
    if nb == bsz:
        col = lambda k: pl.BlockSpec((tb * nb, d), lambda j, i, k=k: (i, k + off))
    else:
        proj = proj.reshape(n_tpad, bsz, proj.shape[1])
        col = lambda k: pl.BlockSpec((tb, nb, d), lambda j, i, k=k: (i, j, k + off))
    in_specs = [
        col(0), col(1), col(2), col(3),
        pl.BlockSpec((None, nb, nh, dk, dk), lambda j, i: (s0_layer, j, 0, 0, 0)),
        _full(lb_raw.shape),
        _full((1, d)),
        pl.BlockSpec(memory_space=pl.ANY),
    ]
    args = [proj, proj, proj, proj, s0, lb_raw, gnorm.reshape(1, d), s_stack]
    assert s_stack.shape == (n_layers, bsz, nh, dk, dk)
    return pl.pallas_call(
        functools.partial(_hgrn2_kernel, layer=layer, n_t=n_t, tb=tb, nb=nb, nh=nh, m=m),
        out_shape=(
            jax.ShapeDtypeStruct((n_tpad * bsz, d), BF16),
            jax.ShapeDtypeStruct((n_layers, bsz, nh, dk, dk), F32),
        ),
        grid=(nbb, nt),
        in_specs=in_specs,
        out_specs=(
            pl.BlockSpec((tb * nb, d), lambda j, i: (j * nt + i, 0)),
            pl.BlockSpec((None, nb, nh, dk, dk), lambda j, i: (layer, j, 0, 0, 0)),
        ),
        scratch_shapes=[pltpu.VMEM((nh, dk, nb * dk), F32)],
        input_output_aliases={len(args) - 1: 1},
        compiler_params=_params("arbitrary", "arbitrary"),
        name="hgrn2",
    )(*args)


def _s5_discretise(lr, li, ldt):
    dt = jnp.exp(ldt)
    mag = jnp.exp(lr * dt)
    return mag * jnp.cos(li * dt), mag * jnp.sin(li * dt)


def _s5_prep_kernel(lr_ref, li_ref, ldt_ref, lrc_ref, lic_ref, ldtc_ref, bre_ref, bim_ref,
                    ar_ref, ai_ref, ore_ref, oim_ref):
    ar_ref[...], ai_ref[...] = _s5_discretise(lr_ref[...], li_ref[...], ldt_ref[...])
    lr = lrc_ref[...]
    li = lic_ref[...]
    ar, ai = _s5_discretise(lr, li, ldtc_ref[...])
    den = lr * lr + li * li
    cr = ((ar - 1.0) * lr + ai * li) / den
    ci = (ai * lr - (ar - 1.0) * li) / den
    ore_ref[...] = cr * bre_ref[...] - ci * bim_ref[...]
    oim_ref[...] = cr * bim_ref[...] + ci * bre_ref[...]


def s5_prep(lam_re, lam_im, log_dt, b_re, b_im):
    g, p, c = b_re.shape
    dense = lambda a: a.reshape(-1, LANES)
    per_state = lambda a: dense(jnp.broadcast_to(a.reshape(g, -1, 1), (g, p, 1)))
    per_coef = lambda a: dense(jnp.broadcast_to(a.reshape(g, -1, 1), (g, p, c)))
    small = jax.ShapeDtypeStruct((g * p // LANES, LANES), F32)
    big = jax.ShapeDtypeStruct((g * p * c // LANES, LANES), F32)
    ar, ai, ore, oim = pl.pallas_call(
        _s5_prep_kernel, out_shape=(small, small, big, big), name="s5_prep",
    )(per_state(lam_re), per_state(lam_im), per_state(log_dt), per_coef(lam_re), per_coef(lam_im), per_coef(log_dt),
      dense(b_re), dense(b_im))
    return ar.reshape(g, p), ai.reshape(g, p), ore.reshape(g, p, c), oim.reshape(g, p, c)


def _s5_kernel(x_ref, g_ref, wb_ref, wc_ref, ar_ref, ai_ref, d_ref, h0r_ref, h0i_ref,
               z_ref, hr_ref, hi_ref, *, n_t, tb, rb, m):
    i = pl.program_id(0)
    nch = wb_ref.shape[0]
    cw = wb_ref.shape[1]
    sw = wb_ref.shape[2] // 2
    tsb = m // rb
    nsl = rb // SUBLANES
    n_sub = jnp.minimum(n_t - i * tb, tb) // tsb

    @pl.when(i == 0)
    def _():
        hr_ref[...] = h0r_ref[...]
        hi_ref[...] = h0i_ref[...]

    @pl.when(n_sub < tb // tsb)
    def _():
        z_ref[...] = jnp.zeros_like(z_ref)

    def sub_block(j, carry):
        r0 = pl.multiple_of(j * m, m)
        xn = _rms(x_ref[pl.ds(r0, m), :], g_ref[...])
        xnb = xn.astype(BF16)
        drive = [_dot(xnb[:, c * cw:(c + 1) * cw], wb_ref[c]) for c in range(nch)]
        for c in range(nch):
            cs = slice(c * cw, (c + 1) * cw)
            ss = slice(c * sw, (c + 1) * sw)
            u = xn[:, cs]
            bu = drive[c]
            ar = jnp.broadcast_to(ar_ref[:, ss], (SUBLANES, sw))
            ai = jnp.broadcast_to(ai_ref[:, ss], (SUBLANES, sw))
            out_r = [None] * (tsb * nsl)
            out_i = [None] * (tsb * nsl)
            for s in range(nsl):
                srow = slice(s * SUBLANES, (s + 1) * SUBLANES)
                hr, hi = hr_ref[srow, ss], hi_ref[srow, ss]
                for t in range(tsb):
                    lo = t * rb + s * SUBLANES
                    hr, hi = (ar * hr - ai * hi + bu[lo:lo + SUBLANES, :sw],
                              ar * hi + ai * hr + bu[lo:lo + SUBLANES, sw:])
                    out_r[t * nsl + s] = hr
                    out_i[t * nsl + s] = hi
                hr_ref[srow, ss] = hr
                hi_ref[srow, ss] = hi
            hcat = jnp.concatenate([jnp.concatenate(out_r, axis=0).astype(BF16),
                                    jnp.concatenate(out_i, axis=0).astype(BF16)], axis=1)
            y = _dot(hcat, wc_ref[c]) + d_ref[:, cs] * u
            z_ref[pl.ds(r0, m), cs] = jax.nn.gelu(y).astype(z_ref.dtype)
        return carry

    lax.fori_loop(0, n_sub, sub_block, 0)


def s5(x, n_t, g, wb, wc, li, a_re, a_im, dskip, h0r, h0i, tb, rb):
    n_rows, d = x.shape
    ns = h0r.shape[1]
    rows = tb * rb
    m = max(k for k in range(rb, S5_MAX_SUB_ROWS + 1, rb)
            if rows % k == 0 and n_t % (k // rb) == 0 and k % (2 * SUBLANES) == 0)
    return pl.pallas_call(
        functools.partial(_s5_kernel, n_t=n_t, tb=tb, rb=rb, m=m),
        out_shape=(
            jax.ShapeDtypeStruct((n_rows, d), BF16),
            jax.ShapeDtypeStruct((rb, ns), F32),
            jax.ShapeDtypeStruct((rb, ns), F32),
        ),
        grid=(n_rows // rows,),
        in_specs=[
            pl.BlockSpec((rows, d), lambda i: (i, 0)),
            _full((1, d)),
            pl.BlockSpec((None,) + wb.shape[1:], lambda i: (li, 0, 0, 0)),
            pl.BlockSpec((None,) + wc.shape[1:], lambda i: (li, 0, 0, 0)),
            _full((1, ns)), _full((1, ns)), _full((1, d)),
            _full((rb, ns)), _full((rb, ns)),
        ],
        out_specs=(
            pl.BlockSpec((rows, d), lambda i: (i, 0)),
            _full((rb, ns)), _full((rb, ns)),
        ),
        compiler_params=_params("arbitrary"),
        name="s5",
    )(x, g.reshape(1, d), wb, wc, a_re, a_im, dskip.reshape(1, d), h0r, h0i)


def _s5_block_diag_in(b):
    g, p, c = b.shape
    gpc = V7X_MXU_DIM // c
    bt = b.transpose(0, 2, 1).reshape(g // gpc, gpc, c, p)
    out = jnp.einsum('ngcp,gh->ngchp', bt, jnp.eye(gpc, dtype=b.dtype))
    return out.reshape(g // gpc, gpc * c, gpc * p)


def _s5_block_diag_out(cm):
    g, c, p = cm.shape
    gpc = V7X_MXU_DIM // c
    ct = cm.transpose(0, 2, 1).reshape(g // gpc, gpc, p, c)
    out = jnp.einsum('ngpc,gh->ngphc', ct, jnp.eye(gpc, dtype=cm.dtype))
    return out.reshape(g // gpc, gpc * p, gpc * c)


def _moe_kernel(x_ref, g_ref, wrt_ref, wgu_ref, wdn_ref, o_ref, xn_ref, pos_ref, gate_ref, *, sub):
    e = pl.program_id(1)
    tm = x_ref.shape[0]
    ne = wrt_ref.shape[0]
    dff = wdn_ref.shape[1]

    @pl.when(e == 0)
    def _():
        x = x_ref[...]
        xn = _rms(x, g_ref[...])
        xn_hi = xn.astype(BF16)
        xn_ref[...] = xn_hi
        xn_lo = (xn - xn_hi.astype(F32)).astype(BF16)
        wr = wrt_ref[...]
        wr_hi = wr.astype(BF16)
        wr_lo = (wr - wr_hi.astype(F32)).astype(BF16)
        nt = lambda a, b: lax.dot_general(a, b, (((1,), (1,)), ((), ())), preferred_element_type=F32)
        logits = nt(wr_hi, xn_hi) + (nt(wr_hi, xn_lo) + nt(wr_lo, xn_hi))
        ex = jnp.exp(logits - jnp.max(logits, axis=0, keepdims=True))
        probs = ex / jnp.sum(ex, axis=0, keepdims=True)
        eid = lax.broadcasted_iota(jnp.int32, (ne, tm), 0).astype(F32)
        m1 = jnp.max(probs, axis=0, keepdims=True)
        i1 = jnp.min(jnp.where(probs == m1, eid, float(ne)), axis=0, keepdims=True)
        sel1 = eid == i1
        rest = jnp.where(sel1, -1.0, probs)
        m2 = jnp.max(rest, axis=0, keepdims=True)
        i2 = jnp.min(jnp.where(rest == m2, eid, float(ne)), axis=0, keepdims=True)
        sel2 = eid == i2
        den = m1 + m2
        gate_ref[...] = jnp.where(sel1, m1 / den, 0.0) + jnp.where(sel2, m2 / den, 0.0)
        chosen = jnp.where(sel1, 1.0, jnp.where(sel2, 1.0, 0.0))
        r = lax.broadcasted_iota(jnp.int32, (tm, tm), 0)
        c = lax.broadcasted_iota(jnp.int32, (tm, tm), 1)
        before = jnp.where(r < c, 1.0, 0.0).astype(BF16)
        rank = _dot(chosen.astype(BF16), before)
        pos_ref[...] = jnp.where(chosen > 0.0, rank, -1.0)
        o_ref[...] = x

    pos_e = pos_ref[pl.ds(e, 1), :]
    gate_e = gate_ref[pl.ds(e, 1), :]
    cnt = jnp.sum(jnp.where(pos_e >= 0.0, 1.0, 0.0)).astype(jnp.int32)

    def run_block(base, rows):
        slot = lax.broadcasted_iota(jnp.int32, (rows, tm), 0) + base
        hit = pos_e == slot.astype(F32)
        onehot = jnp.where(hit, 1.0, 0.0).astype(BF16)
        xs = _dot(onehot, xn_ref[...]).astype(BF16)
        gs = jnp.sum(jnp.where(hit, gate_e, 0.0), axis=1, keepdims=True)
        hgu = _dot(xs, wgu_ref[0])
        act = (jax.nn.silu(hgu[:, :dff]) * hgu[:, dff:]).astype(BF16)
        yb = (_dot(act, wdn_ref[0]) * gs).astype(BF16)
        o_ref[...] += lax.dot_general(onehot, yb, (((0,), (0,)), ((), ())), preferred_element_type=F32)

    def body(s, carry):
        run_block(s * sub, sub)
        return carry

    n_full = cnt // sub
    lax.fori_loop(0, n_full, body, 0)
    rem = cnt - n_full * sub
    lo = 0
    for rows in MOE_TAIL_ROWS:
        pl.when((rem > lo) & (rem <= rows))(functools.partial(run_block, n_full * sub, rows))
        lo = rows


def moe_residual(x, g, w_router_t, w_gu, w_down, li, tm, sub):
    n, d = x.shape
    _, ne, _, dff2 = w_gu.shape
    assert MOE_TAIL_ROWS[-1] == sub
    return pl.pallas_call(
        functools.partial(_moe_kernel, sub=sub),
        out_shape=jax.ShapeDtypeStruct((n, d), F32),
        grid=(n // tm, ne),
        in_specs=[
            pl.BlockSpec((tm, d), lambda i, e: (i, 0)),
            _full((1, d)),
            _full((ne, d)),
            pl.BlockSpec((None, 1, d, dff2), lambda i, e: (li, e, 0, 0)),
            pl.BlockSpec((None, 1, dff2 // 2, d), lambda i, e: (li, e, 0, 0)),
        ],
        out_specs=pl.BlockSpec((tm, d), lambda i, e: (i, 0)),
        scratch_shapes=[
            pltpu.VMEM((tm, d), BF16),
            pltpu.VMEM((ne, tm), F32),
            pltpu.VMEM((ne, tm), F32),
        ],
        compiler_params=_params("parallel", "arbitrary"),
        name="moe",
    )(x, g.reshape(1, d), w_router_t, w_gu, w_down)


def _trunk(x, n_t, rb, tb, nb, h0, conv0, s0, re0, im0, w):
    n, d = x.shape
    depth = w['norm_mix'].shape[0]
    d_a = h0.shape[-1]
    nh, dk = s0.shape[2], s0.shape[3]
    d_b = nh * dk
    ns = re0.shape[-2] * re0.shape[-1]
    tm = tb * rb
    n_even = (depth + 1) // 2
    new = {k: [] for k in ('h', 'conv', 're', 'im')}
    s_stack = jnp.zeros((n_even,) + s0.shape[1:], F32)
    for l in range(depth):
        li = l // 2
        if l % 2 == 0:
            proj = norm_matmul(x, w['norm_mix'][l], w['even_w_in'], li, tm, w['even_w_in'].shape[2])
            conv_tm = conv0[li].transpose(1, 0, 2).reshape((CONV_W - 1) * rb, d_a)
            ya, h_new, conv_new = rglru(proj, n_t, conv_tm, h0[li], w['rglru_conv_w'][li], w['rglru_conv_b'][li],
                                        w['rglru_wg'][li], w['rglru_bg'][li], w['rglru_lambda'][li], tb=tb, rb=rb)
            new['h'].append(h_new)
            new['conv'].append(conv_new.reshape(CONV_W - 1, rb, d_a).transpose(1, 0, 2))
            ob, s_stack = hgrn2(proj, s0, li % s0.shape[0], s_stack, n_even,
                                w['hgrn2_lb_raw'], w['hgrn2_gnorm'][li], li, n_t=n_t, tb=tb, nb=nb)
            if nb != rb:
                ob = ob.reshape(rb // nb, n // rb, nb, d_b).transpose(1, 0, 2, 3).reshape(n, d_b)
            x = mix_ffn(ya, ob, x, w['even_w_out'], w['norm_ffn'][l], w['ffn_w_gu'], w['ffn_w_down'], li, tm,
                        w['ffn_w_down'].shape[1] // 2)
        else:
            z, re_new, im_new = s5(x, n_t, w['norm_mix'][l], w['s5_wb'], w['s5_wc'], li, w['s5_a_re'][li],
                                   w['s5_a_im'][li], w['s5_d'][li], re0[li].reshape(rb, ns), im0[li].reshape(rb, ns),
                                   tb=tb, rb=rb)
            new['re'].append(re_new.reshape(re0.shape[1:]))
            new['im'].append(im_new.reshape(im0.shape[1:]))
            x = glu_residual(z, w['s5_w_glu'], li, x, tm, w['s5_w_glu'].shape[2] // 2)
            x = moe_residual(x, w['norm_ffn'][l], w['moe_w_router_t'][li], w['moe_w_gu'], w['moe_w_down'], li,
                             tm, MOE_TAIL_ROWS[-1])
    stack = lambda k: jnp.stack(new[k])
    return x, (stack('h'), stack('conv'), s_stack, stack('re'), stack('im'))


def kernel(x_prompt, x_sample, state_rglru_h, state_rglru_conv, state_hgrn2, state_s5_re, state_s5_im,
           meta_tokens, norm_mix, norm_ffn, norm_final, even_w_in, even_w_out,
           rglru_conv_w, rglru_conv_b, rglru_w_a, rglru_b_a, rglru_w_x, rglru_b_x, rglru_lambda,
           hgrn2_lb_raw, hgrn2_gnorm, s5_lam_re, s5_lam_im, s5_log_dt, s5_b_re, s5_b_im, s5_c_re, s5_c_im,
           s5_d, s5_w_glu, ffn_w_gu, ffn_w_down, moe_w_router, moe_w_gu, moe_w_down):
    bp, tp0, d = x_prompt.shape
    bs, ts, _ = x_sample.shape
    tp = tp0 + N_META
    d_a = state_rglru_h.shape[-1]
    n_even, n_odd = state_rglru_h.shape[0], state_s5_re.shape[0]
    assert bp == SUBLANES and bs % SUBLANES == 0
    tb_p = PROMPT_BLOCK_ROWS // bp
    tp_pad = -(-tp // tb_p) * tb_p
    nb_s = TIME_CHUNK_ROWS // ts
    assert (tp % (TIME_CHUNK_ROWS // bp) == 0 and tb_p % (TIME_CHUNK_ROWS // bp) == 0 and bs % nb_s == 0
            and nb_s * ts == TIME_CHUNK_ROWS and nb_s % SUBLANES == 0)

    eye_a = jnp.eye(H_A, dtype=F32)
    block_diag = lambda m: jnp.einsum('lhij,hg->lhigj', m, eye_a).reshape(n_even, d_a, d_a)
    s5_g, s5_p, s5_c = s5_b_re.shape[1:]
    lg = n_odd * s5_g
    a_re, a_im, bt_re, bt_im = s5_prep(s5_lam_re.reshape(lg, s5_p), s5_lam_im.reshape(lg, s5_p), s5_log_dt.reshape(lg),
                                       s5_b_re.reshape(lg, s5_p, s5_c), s5_b_im.reshape(lg, s5_p, s5_c))
    s5_wb = jnp.concatenate([_s5_block_diag_in(bt_re.reshape(lg, s5_p, s5_c)),
                             _s5_block_diag_in(bt_im.reshape(lg, s5_p, s5_c))], axis=2).astype(BF16)
    s5_wc = jnp.concatenate([_s5_block_diag_out(s5_c_re.reshape(lg, s5_c, s5_p)),
                             -_s5_block_diag_out(s5_c_im.reshape(lg, s5_c, s5_p))], axis=1).astype(BF16)
    ns = s5_lam_re.shape[1] * s5_lam_re.shape[2]
    w = {
        'norm_mix': norm_mix, 'norm_ffn': norm_ffn, 'norm_final': norm_final,
        'even_w_in': even_w_in.astype(BF16), 'even_w_out': even_w_out.astype(BF16),
        'rglru_conv_w': rglru_conv_w, 'rglru_conv_b': rglru_conv_b,
        'rglru_wg': jnp.concatenate([block_diag(rglru_w_a), block_diag(rglru_w_x)], axis=2).astype(BF16),
        'rglru_bg': jnp.concatenate([rglru_b_a, rglru_b_x], axis=1),
        'rglru_lambda': rglru_lambda, 'hgrn2_lb_raw': hgrn2_lb_raw, 'hgrn2_gnorm': hgrn2_gnorm,
        's5_a_re': a_re.reshape(n_odd, 1, ns), 's5_a_im': a_im.reshape(n_odd, 1, ns),
        's5_wb': s5_wb.reshape((n_odd, -1) + s5_wb.shape[1:]), 's5_wc': s5_wc.reshape((n_odd, -1) + s5_wc.shape[1:]),
        's5_d': s5_d, 's5_w_glu': s5_w_glu.astype(BF16),
        'ffn_w_gu': ffn_w_gu.astype(BF16), 'ffn_w_down': ffn_w_down.astype(BF16),
        'moe_w_router_t': moe_w_router.transpose(0, 2, 1),
        'moe_w_gu': moe_w_gu.astype(BF16), 'moe_w_down': moe_w_down.astype(BF16),
    }

    tc_p = TIME_CHUNK_ROWS // bp
    assert N_META == tc_p
    xm = to_time_major(x_prompt, meta_tokens.astype(x_prompt.dtype), tp_pad, tb_p // tc_p)
    zero = lambda ref, lead: jnp.zeros((lead, bp) + ref.shape[2:], ref.dtype)
    xp, p_new = _trunk(xm, tp, bp, tb_p, bp, zero(state_rglru_h, n_even), zero(state_rglru_conv, n_even),
                       zero(state_hgrn2, 1), zero(state_s5_re, n_odd), zero(state_s5_im, n_odd), w)
    nk_out = max(k for k in range(1, TIME_CHUNK_ROWS // tc_p + 1) if tp0 % (k * tc_p) == 0)
    y_prompt = rmsnorm_batch_major(xp, norm_final, bp, N_META, tp0, tc_p, nk_out)

    xs = x_sample.transpose(1, 0, 2).reshape(ts * bs, d)
    xs, s_new = _trunk(xs, ts, bs, ts, nb_s, state_rglru_h, state_rglru_conv, state_hgrn2, state_s5_re, state_s5_im, w)
    y_sample = rmsnorm_rows(xs, norm_final, ts * bs).reshape(ts, bs, d).transpose(1, 0, 2)

    refs = (state_rglru_h, state_rglru_conv, state_hgrn2, state_s5_re, state_s5_im)
    cast = lambda new: tuple(a.astype(r.dtype) for a, r in zip(new, refs))
    return (y_prompt, y_sample) + cast(p_new) + cast(s_new)
```

```python
import functools

import jax
import jax.numpy as jnp
from jax import lax
from jax.experimental import pallas as pl
from jax.experimental.pallas import tpu as pltpu

F32 = jnp.float32
BF16 = jnp.bfloat16

EPS = 1e-6
N_META = 16
CONV_W = 4
RG_C = 8.0
H_A = 8
S5_P = 64

V7X_VMEM_LIMIT_BYTES = 56 * 1024 * 1024
SUBLANES = 8
LANES = 128
V7X_MXU_DIM = 256
MOE_TAIL_ROWS = (128, 192, 256, 320, 384, 448, 512)
TIME_CHUNK_ROWS = 128
HGRN_ROWS = 256
S5_MAX_SUB_ROWS = 384
HGRN_EXP_CLAMP = 80.0
PROMPT_BLOCK_ROWS = 768
SCAN_CARRY_ELEMS = 16 * SUBLANES * LANES


def _params(*sem):
    return pltpu.CompilerParams(dimension_semantics=sem, vmem_limit_bytes=V7X_VMEM_LIMIT_BYTES)


def _rms(x, g):
    ms = jnp.mean(x * x, axis=-1, keepdims=True)
    return x * lax.rsqrt(ms + EPS) * g


def _dot(a, b):
    return jnp.dot(a, b, preferred_element_type=F32)


def _full(shape):
    return pl.BlockSpec(shape, lambda *_: (0,) * len(shape))


def _norm_matmul_kernel(x_ref, g_ref, w_ref, o_ref, xn_ref):
    @pl.when(pl.program_id(1) == 0)
    def _():
        xn_ref[...] = _rms(x_ref[...], g_ref[...]).astype(BF16)

    o_ref[...] = _dot(xn_ref[...], w_ref[...]).astype(o_ref.dtype)


def norm_matmul(x, g, w, li, tm, tn):
    n, d = x.shape
    nout = w.shape[2]
    return pl.pallas_call(
        _norm_matmul_kernel,
        out_shape=jax.ShapeDtypeStruct((n, nout), BF16),
        grid=(n // tm, nout // tn),
        in_specs=[
            pl.BlockSpec((tm, d), lambda i, j: (i, 0)),
            _full((1, d)),
            pl.BlockSpec((None, d, tn), lambda i, j: (li, 0, j)),
        ],
        out_specs=pl.BlockSpec((tm, tn), lambda i, j: (i, j)),
        scratch_shapes=[pltpu.VMEM((tm, d), BF16)],
        compiler_params=_params("parallel", "arbitrary"),
        name="norm_matmul",
    )(x, g.reshape(1, d), w)


def _mix_ffn_kernel(ya_ref, ob_ref, x_ref, wo_ref, g_ref, wg_ref, wu_ref, wdn_ref, o_ref, xn_ref):
    j = pl.program_id(1)
    d_a = ya_ref.shape[1]

    @pl.when(j == 0)
    def _():
        x1 = x_ref[...] + _dot(ya_ref[...], wo_ref[:d_a, :]) + _dot(ob_ref[...], wo_ref[d_a:, :])
        xn_ref[...] = _rms(x1, g_ref[...]).astype(BF16)
        o_ref[...] = x1

    xn = xn_ref[...]
    h = (jax.nn.silu(_dot(xn, wg_ref[...])) * _dot(xn, wu_ref[...])).astype(BF16)
    o_ref[...] += _dot(h, wdn_ref[...])


def mix_ffn(ya, ob, x, w_out, g, w_gu, w_down, li, tm, chunk):
    n, d = x.shape
    dff = w_down.shape[1]
    assert dff % chunk == 0 and chunk % LANES == 0
    nj = dff // chunk
    rows = lambda a: pl.BlockSpec((tm, a.shape[1]), lambda i, j: (i, 0))
    return pl.pallas_call(
        _mix_ffn_kernel,
        out_shape=jax.ShapeDtypeStruct((n, d), F32),
        grid=(n // tm, nj),
        in_specs=[
            rows(ya), rows(ob), rows(x),
            pl.BlockSpec((None,) + w_out.shape[1:], lambda i, j: (li, 0, 0)),
            _full((1, d)),
            pl.BlockSpec((None, d, chunk), lambda i, j: (li, 0, j)),
            pl.BlockSpec((None, d, chunk), lambda i, j: (li, 0, j + nj)),
            pl.BlockSpec((None, chunk, d), lambda i, j: (li, j, 0)),
        ],
        out_specs=pl.BlockSpec((tm, d), lambda i, j: (i, 0)),
        scratch_shapes=[pltpu.VMEM((tm, d), BF16)],
        compiler_params=_params("parallel", "arbitrary"),
        name="mix_ffn",
    )(ya, ob, x, w_out, g.reshape(1, d), w_gu, w_gu, w_down)


def _glu_residual_kernel(z_ref, wv_ref, wg_ref, r_ref, o_ref):
    z = z_ref[...]
    val = _dot(z, wv_ref[...])
    gate = _dot(z, wg_ref[...])
    o_ref[...] = r_ref[...] + val * jax.nn.sigmoid(gate)


def glu_residual(z, w_glu, li, res, tm, tn):
    n, d = z.shape
    dout = w_glu.shape[2] // 2
    nj = dout // tn
    return pl.pallas_call(
        _glu_residual_kernel,
        out_shape=jax.ShapeDtypeStruct((n, dout), F32),
        grid=(n // tm, nj),
        in_specs=[
            pl.BlockSpec((tm, d), lambda i, j: (i, 0)),
            pl.BlockSpec((None, d, tn), lambda i, j: (li, 0, j)),
            pl.BlockSpec((None, d, tn), lambda i, j: (li, 0, j + nj)),
            pl.BlockSpec((tm, tn), lambda i, j: (i, j)),
        ],
        out_specs=pl.BlockSpec((tm, tn), lambda i, j: (i, j)),
        compiler_params=_params("parallel", "arbitrary"),
        name="glu_residual",
    )(z, w_glu, w_glu, res)


def _rmsnorm_kernel(x_ref, g_ref, o_ref):
    o_ref[...] = _rms(x_ref[...], g_ref[...])


def rmsnorm_rows(x, g, tm):
    n, d = x.shape
    return pl.pallas_call(
        _rmsnorm_kernel,
        out_shape=jax.ShapeDtypeStruct((n, d), F32),
        grid=(n // tm,),
        in_specs=[pl.BlockSpec((tm, d), lambda i: (i, 0)), _full((1, d))],
        out_specs=pl.BlockSpec((tm, d), lambda i: (i, 0)),
        compiler_params=_params("parallel"),
        name="final_rmsnorm",
    )(x, g.reshape(1, d))


def _rmsnorm_batch_major_kernel(*refs, nk, tc, nb):
    x_refs, g_ref, o_ref = refs[:nk], refs[nk], refs[nk + 1]
    d = o_ref.shape[-1]
    for k in range(nk):
        y = _rms(x_refs[k][...], g_ref[...])
        o_ref[:, k * tc:(k + 1) * tc, :] = jnp.swapaxes(y.reshape(tc, nb, d), 0, 1)


def rmsnorm_batch_major(x, g, nb, t_skip, t_out, tc, nk):
    n, d = x.shape
    rows = tc * nb
    assert t_skip % tc == 0 and t_out % (nk * tc) == 0
    specs = [pl.BlockSpec((rows, d), lambda j, k=k: (nk * j + t_skip // tc + k, 0)) for k in range(nk)]
    return pl.pallas_call(
        functools.partial(_rmsnorm_batch_major_kernel, nk=nk, tc=tc, nb=nb),
        out_shape=jax.ShapeDtypeStruct((nb, t_out, d), F32),
        grid=(t_out // (nk * tc),),
        in_specs=specs + [_full((1, d))],
        out_specs=pl.BlockSpec((nb, nk * tc, d), lambda j: (0, j, 0)),
        compiler_params=_params("parallel"),
        name="final_rmsnorm_batch_major",
    )(*([x] * nk), g.reshape(1, d))


def _to_time_major_kernel(*refs, nk, n_chunks):
    x_refs, lead_ref, o_ref = refs[:nk], refs[nk], refs[nk + 1]
    i = pl.program_id(0)
    nb, tc, d = x_refs[0].shape
    rows = tc * nb
    for k in range(nk):
        chunk = i * nk + k - 1
        val = jnp.swapaxes(x_refs[k][...], 0, 1).reshape(rows, d)
        if k == 0:
            lead = jnp.broadcast_to(lead_ref[...][:, None, :], (tc, nb, d)).reshape(rows, d)
            val = jnp.where(i == 0, lead, val)
        o_ref[k * rows:(k + 1) * rows, :] = jnp.where(chunk < n_chunks, val, 0.0)


def to_time_major(x, lead, t_pad, nk):
    nb, t, d = x.shape
    tc = lead.shape[0]
    assert t % tc == 0 and t_pad % (nk * tc) == 0
    n_chunks = t // tc
    specs = [pl.BlockSpec((nb, tc, d), lambda i, k=k: (0, jnp.clip(i * nk + k - 1, 0, n_chunks - 1), 0))
             for k in range(nk)]
    return pl.pallas_call(
        functools.partial(_to_time_major_kernel, nk=nk, n_chunks=n_chunks),
        out_shape=jax.ShapeDtypeStruct((t_pad * nb, d), F32),
        grid=(t_pad // (nk * tc),),
        in_specs=specs + [_full((tc, d))],
        out_specs=pl.BlockSpec((nk * tc * nb, d), lambda i: (i, 0)),
        compiler_params=_params("parallel"),
        name="to_time_major",
    )(*([x] * nk), lead)


def _rglru_kernel(xa_ref, ga_ref, conv0_ref, h0_ref, cw_ref, cb_ref, wg_ref, bg_ref, lam_ref,
                  ya_ref, hlast_ref, convnew_ref, xpad_ref, a_ref, u_ref, *, n_t, tb, rb):
    i = pl.program_id(0)
    rows = tb * rb
    tail = (CONV_W - 1) * rb
    c = xa_ref.shape[-1]
    t_valid = jnp.minimum(n_t - i * tb, tb)

    @pl.when(i == 0)
    def _():
        xpad_ref[0:tail, :] = conv0_ref[...]
        hlast_ref[...] = h0_ref[...]

    @pl.when(i > 0)
    def _():
        xpad_ref[0:tail, :] = xpad_ref[rows:rows + tail, :]

    xpad_ref[tail:tail + rows, :] = xa_ref[...].astype(F32)
    xc = cb_ref[...]
    for k in range(CONV_W):
        xc = xc + cw_ref[k:k + 1, :] * xpad_ref[k * rb:k * rb + rows, :]

    gates = _dot(xc.astype(BF16), wg_ref[...]) + bg_ref[...]
    r = jax.nn.sigmoid(gates[:, :c])
    ig = jax.nn.sigmoid(gates[:, c:])
    log_a = (-RG_C) * r * jax.nn.softplus(-lam_ref[...])
    a = jnp.exp(log_a)
    mult = jnp.sqrt(1.0 - a * a)
    a_ref[...] = a
    u_ref[...] = mult * ig * xc

    lc = min(c, max(LANES, SCAN_CARRY_ELEMS // rb // LANES * LANES))
    for c0 in range(0, c, lc):
        def body(t, h, c0=c0):
            sl = pl.ds(pl.multiple_of(t * rb, rb), rb)
            h = a_ref[sl, c0:c0 + lc] * h + u_ref[sl, c0:c0 + lc]
            u_ref[sl, c0:c0 + lc] = h
            return h

        hlast_ref[:, c0:c0 + lc] = lax.fori_loop(0, t_valid, body, hlast_ref[:, c0:c0 + lc])

    live = lax.broadcasted_iota(jnp.int32, (rows, c), 0) < t_valid * rb
    ya_ref[...] = jnp.where(live, u_ref[...] * jax.nn.gelu(ga_ref[...].astype(F32)), 0.0).astype(ya_ref.dtype)
    convnew_ref[...] = xpad_ref[pl.ds(pl.multiple_of(t_valid * rb, rb), tail), :]


def rglru(proj, n_t, conv0, h0, cw, cb, wg, bg, lam, tb, rb):
    c = h0.shape[1]
    n_rows = proj.shape[0]
    rows = tb * rb
    tail = (CONV_W - 1) * rb
    return pl.pallas_call(
        functools.partial(_rglru_kernel, n_t=n_t, tb=tb, rb=rb),
        out_shape=(
            jax.ShapeDtypeStruct((n_rows, c), BF16),
            jax.ShapeDtypeStruct((rb, c), F32),
            jax.ShapeDtypeStruct((tail, c), F32),
        ),
        grid=(n_rows // rows,),
        in_specs=[
            pl.BlockSpec((rows, c), lambda i: (i, 0)),
            pl.BlockSpec((rows, c), lambda i: (i, 1)),
            _full((tail, c)), _full((rb, c)), _full((CONV_W, c)), _full((1, c)),
            _full((c, 2 * c)), _full((1, 2 * c)), _full((1, c)),
        ],
        out_specs=(
            pl.BlockSpec((rows, c), lambda i: (i, 0)),
            _full((rb, c)),
            _full((tail, c)),
        ),
        scratch_shapes=[
            pltpu.VMEM((rows + tail, c), F32),
            pltpu.VMEM((rows, c), F32),
            pltpu.VMEM((rows, c), F32),
        ],
        compiler_params=_params("arbitrary"),
        name="rglru",
    )(proj, proj, conv0, h0, cw, cb.reshape(1, c), wg, bg.reshape(1, 2 * c), lam.reshape(1, c))


def _hgrn2_kernel(q_ref, f_ref, v_ref, gb_ref, s0_ref, lbraw_ref, gn_ref, stack_ref, ob_ref, snew_ref, st_ref,
                  *, layer, n_t, tb, nb, nh, m):
    del stack_ref
    i = pl.program_id(1)
    tc = m // nb
    half = tc // 2
    d = q_ref.shape[-1]
    dk = d // nh
    t_valid = jnp.minimum(n_t - i * tb, tb)
    n_chunks = (t_valid + tc - 1) // tc

    @pl.when(i == 0)
    def _():
        for b in range(nb):
            for h in range(nh):
                st_ref[h, :, b * dk:(b + 1) * dk] = s0_ref[b, h].T

    @pl.when(n_chunks < tb // tc)
    def _():
        ob_ref[...] = jnp.zeros_like(ob_ref)

    p = jax.nn.softmax(lbraw_ref[...], axis=0)
    cum = p[0:1, :]
    for r in range(1, layer + 1):
        cum = cum + p[r:r + 1, :]
    lb = cum - p[0:1, :]
    log_lb = jnp.log(lb)
    log_1mlb = jnp.log1p(-lb)

    assert tc % 2 == 0 and nb & (nb - 1) == 0
    row = lax.broadcasted_iota(jnp.int32, (m, m), 0)
    col = lax.broadcasted_iota(jnp.int32, (m, m), 1)
    same_seq_causal = jnp.where(((row & (nb - 1)) == (col & (nb - 1))) & (col <= row), 1.0, 0.0)
    bid = lax.broadcasted_iota(jnp.int32, (m, dk), 0) & (nb - 1)

    def chunk(c, carry):
        def rows_of(ref):
            if len(ref.shape) == 2:
                return ref[pl.ds(pl.multiple_of(c * m, m), m), :]
            return ref[pl.ds(c * tc, tc)].reshape(m, d)

        live = lax.broadcasted_iota(jnp.int32, (m, d), 0) < (t_valid - c * tc) * nb
        q = jax.nn.silu(rows_of(q_ref).astype(F32))
        fr = rows_of(f_ref).astype(F32)
        v = rows_of(v_ref).astype(BF16)
        gb = rows_of(gb_ref).astype(F32)
        logf = jnp.where(live, jnp.logaddexp(log_lb, log_1mlb + jax.nn.log_sigmoid(fr)), 0.0)
        k = 1.0 - jnp.exp(logf)
        slabs = [logf[0:nb]]
        for t in range(1, tc):
            slabs.append(slabs[-1] + logf[t * nb:(t + 1) * nb])
        g = jnp.concatenate(slabs, axis=0)
        g_last = jnp.concatenate([slabs[-1]] * tc, axis=0)
        g_rel = g - jnp.concatenate([slabs[half - 1]] * tc, axis=0)
        qa = (q * jnp.exp(jnp.minimum(g_rel, HGRN_EXP_CLAMP))).astype(BF16)
        kt = (k * jnp.exp(jnp.minimum(-g_rel, HGRN_EXP_CLAMP))).astype(BF16)
        qt = (q * jnp.exp(g)).astype(BF16)
        ks = (k * jnp.exp(g_last - g)).astype(BF16)
        dec = jnp.exp(slabs[-1])

        outs = []
        for h in range(nh):
            hs = slice(h * dk, (h + 1) * dk)
            qt_h, kt_h, ks_h, v_h = qt[:, hs], kt[:, hs], ks[:, hs], v[:, hs]
            att = lax.dot_general(qa[:, hs], kt_h, (((1,), (1,)), ((), ())), preferred_element_type=F32)
            att = (att * same_seq_causal).astype(BF16)
            o = _dot(att, v_h)
            zero = jnp.zeros_like(qt_h)
            expand = lambda x: jnp.concatenate([jnp.where(bid == b, x, zero) for b in range(nb)], axis=1)
            st_h = st_ref[h]
            o = o + lax.dot_general(expand(qt_h), st_h.astype(BF16), (((1,), (1,)), ((), ())),
                                    preferred_element_type=F32)
            dst = lax.dot_general(v_h, expand(ks_h), (((0,), (0,)), ((), ())), preferred_element_type=F32)
            dec_row = jnp.concatenate([dec[b:b + 1, hs] for b in range(nb)], axis=1)
            st_ref[h] = st_h * dec_row + dst
            outs.append(_rms(o, gn_ref[:, hs]))
        ob = jnp.where(live, jnp.concatenate(outs, axis=1) * jax.nn.silu(gb), 0.0)
        ob_ref[pl.ds(pl.multiple_of(c * m, m), m), :] = ob.astype(ob_ref.dtype)
        return carry

    lax.fori_loop(0, n_chunks, chunk, 0)

    @pl.when(i == pl.num_programs(1) - 1)
    def _():
        for b in range(nb):
            for h in range(nh):
                snew_ref[b, h] = st_ref[h, :, b * dk:(b + 1) * dk].T


def hgrn2(proj, s0, s0_layer, s_stack, n_layers, lb_raw, gnorm, layer, n_t, tb, nb):
    _, bsz, nh, dk, _ = s0.shape
    d = nh * dk
    off = proj.shape[-1] // d - 4
    nbb = bsz // nb
    n_tpad = proj.shape[0] // bsz
    nt = n_tpad // tb
    m = max(k for k in range(2 * nb, HGRN_ROWS + 1, 2 * nb) if (tb * nb) % k == 0)
    if nb == bsz:
        col = lambda k: pl.BlockSpec((tb * nb, d), lambda j, i, k=k: (i, k + off))
    else:
        proj = proj.reshape(n_tpad, bsz, proj.shape[1])
        col = lambda k: pl.BlockSpec((tb, nb, d), lambda j, i, k=k: (i, j, k + off))
    in_specs = [
        col(0), col(1), col(2), col(3),
        pl.BlockSpec((None, nb, nh, dk, dk), lambda j, i: (s0_layer, j, 0, 0, 0)),
        _full(lb_raw.shape),
        _full((1, d)),
        pl.BlockSpec(memory_space=pl.ANY),
    ]
    args = [proj, proj, proj, proj, s0, lb_raw, gnorm.reshape(1, d), s_stack]
    assert s_stack.shape == (n_layers, bsz, nh, dk, dk)
    return pl.pallas_call(
        functools.partial(_hgrn2_kernel, layer=layer, n_t=n_t, tb=tb, nb=nb, nh=nh, m=m),
        out_shape=(
            jax.ShapeDtypeStruct((n_tpad * bsz, d), BF16),
            jax.ShapeDtypeStruct((n_layers, bsz, nh, dk, dk), F32),
        ),
        grid=(nbb, nt),
        in_specs=in_specs,
        out_specs=(
            pl.BlockSpec((tb * nb, d), lambda j, i: (j * nt + i, 0)),
            pl.BlockSpec((None, nb, nh, dk, dk), lambda j, i: (layer, j, 0, 0, 0)),
        ),
        scratch_shapes=[pltpu.VMEM((nh, dk, nb * dk), F32)],
        input_output_aliases={len(args) - 1: 1},
        compiler_params=_params("arbitrary", "arbitrary"),
        name="hgrn2",
    )(*args)


def _s5_discretise(lr, li, ldt):
    dt = jnp.exp(ldt)
    mag = jnp.exp(lr * dt)
    return mag * jnp.cos(li * dt), mag * jnp.sin(li * dt)


def _s5_prep_kernel(lr_ref, li_ref, ldt_ref, lrc_ref, lic_ref, ldtc_ref, bre_ref, bim_ref,
                    ar_ref, ai_ref, ore_ref, oim_ref):
    ar_ref[...], ai_ref[...] = _s5_discretise(lr_ref[...], li_ref[...], ldt_ref[...])
    lr = lrc_ref[...]
    li = lic_ref[...]
    ar, ai = _s5_discretise(lr, li, ldtc_ref[...])
    den = lr * lr + li * li
    cr = ((ar - 1.0) * lr + ai * li) / den
    ci = (ai * lr - (ar - 1.0) * li) / den
    ore_ref[...] = cr * bre_ref[...] - ci * bim_ref[...]
    oim_ref[...] = cr * bim_ref[...] + ci * bre_ref[...]


def s5_prep(lam_re, lam_im, log_dt, b_re, b_im):
    g, p, c = b_re.shape
    dense = lambda a: a.reshape(-1, LANES)
    per_state = lambda a: dense(jnp.broadcast_to(a.reshape(g, -1, 1), (g, p, 1)))
    per_coef = lambda a: dense(jnp.broadcast_to(a.reshape(g, -1, 1), (g, p, c)))
    small = jax.ShapeDtypeStruct((g * p // LANES, LANES), F32)
    big = jax.ShapeDtypeStruct((g * p * c // LANES, LANES), F32)
    ar, ai, ore, oim = pl.pallas_call(
        _s5_prep_kernel, out_shape=(small, small, big, big), name="s5_prep",
    )(per_state(lam_re), per_state(lam_im), per_state(log_dt), per_coef(lam_re), per_coef(lam_im), per_coef(log_dt),
      dense(b_re), dense(b_im))
    return ar.reshape(g, p), ai.reshape(g, p), ore.reshape(g, p, c), oim.reshape(g, p, c)


def _s5_kernel(x_ref, g_ref, wb_ref, wc_ref, ar_ref, ai_ref, d_ref, h0r_ref, h0i_ref,
               z_ref, hr_ref, hi_ref, *, n_t, tb, rb, m):
    i = pl.program_id(0)
    nch = wb_ref.shape[0]
    cw = wb_ref.shape[1]
    sw = wb_ref.shape[2] // 2
    tsb = m // rb
    nsl = rb // SUBLANES
    n_sub = jnp.minimum(n_t - i * tb, tb) // tsb

    @pl.when(i == 0)
    def _():
        hr_ref[...] = h0r_ref[...]
        hi_ref[...] = h0i_ref[...]

    @pl.when(n_sub < tb // tsb)
    def _():
        z_ref[...] = jnp.zeros_like(z_ref)

    def sub_block(j, carry):
        r0 = pl.multiple_of(j * m, m)
        xn = _rms(x_ref[pl.ds(r0, m), :], g_ref[...])
        xnb = xn.astype(BF16)
        drive = [_dot(xnb[:, c * cw:(c + 1) * cw], wb_ref[c]) for c in range(nch)]
        for c in range(nch):
            cs = slice(c * cw, (c + 1) * cw)
            ss = slice(c * sw, (c + 1) * sw)
            u = xn[:, cs]
            bu = drive[c]
            ar = jnp.broadcast_to(ar_ref[:, ss], (SUBLANES, sw))
            ai = jnp.broadcast_to(ai_ref[:, ss], (SUBLANES, sw))
            out_r = [None] * (tsb * nsl)
            out_i = [None] * (tsb * nsl)
            for s in range(nsl):
                srow = slice(s * SUBLANES, (s + 1) * SUBLANES)
                hr, hi = hr_ref[srow, ss], hi_ref[srow, ss]
                for t in range(tsb):
                    lo = t * rb + s * SUBLANES
                    hr, hi = (ar * hr - ai * hi + bu[lo:lo + SUBLANES, :sw],
                              ar * hi + ai * hr + bu[lo:lo + SUBLANES, sw:])
                    out_r[t * nsl + s] = hr
                    out_i[t * nsl + s] = hi
                hr_ref[srow, ss] = hr
                hi_ref[srow, ss] = hi
            hcat = jnp.concatenate([jnp.concatenate(out_r, axis=0).astype(BF16),
                                    jnp.concatenate(out_i, axis=0).astype(BF16)], axis=1)
            y = _dot(hcat, wc_ref[c]) + d_ref[:, cs] * u
            z_ref[pl.ds(r0, m), cs] = jax.nn.gelu(y).astype(z_ref.dtype)
        return carry

    lax.fori_loop(0, n_sub, sub_block, 0)


def s5(x, n_t, g, wb, wc, li, a_re, a_im, dskip, h0r, h0i, tb, rb):
    n_rows, d = x.shape
    ns = h0r.shape[1]
    rows = tb * rb
    m = max(k for k in range(rb, S5_MAX_SUB_ROWS + 1, rb)
            if rows % k == 0 and n_t % (k // rb) == 0 and k % (2 * SUBLANES) == 0)
    return pl.pallas_call(
        functools.partial(_s5_kernel, n_t=n_t, tb=tb, rb=rb, m=m),
        out_shape=(
            jax.ShapeDtypeStruct((n_rows, d), BF16),
            jax.ShapeDtypeStruct((rb, ns), F32),
            jax.ShapeDtypeStruct((rb, ns), F32),
        ),
        grid=(n_rows // rows,),
        in_specs=[
            pl.BlockSpec((rows, d), lambda i: (i, 0)),
            _full((1, d)),
            pl.BlockSpec((None,) + wb.shape[1:], lambda i: (li, 0, 0, 0)),
            pl.BlockSpec((None,) + wc.shape[1:], lambda i: (li, 0, 0, 0)),
            _full((1, ns)), _full((1, ns)), _full((1, d)),
            _full((rb, ns)), _full((rb, ns)),
        ],
        out_specs=(
            pl.BlockSpec((rows, d), lambda i: (i, 0)),
            _full((rb, ns)), _full((rb, ns)),
        ),
        compiler_params=_params("arbitrary"),
        name="s5",
    )(x, g.reshape(1, d), wb, wc, a_re, a_im, dskip.reshape(1, d), h0r, h0i)


def _s5_block_diag_in(b):
    g, p, c = b.shape
    gpc = V7X_MXU_DIM // c
    bt = b.transpose(0, 2, 1).reshape(g // gpc, gpc, c, p)
    out = jnp.einsum('ngcp,gh->ngchp', bt, jnp.eye(gpc, dtype=b.dtype))
    return out.reshape(g // gpc, gpc * c, gpc * p)


def _s5_block_diag_out(cm):
    g, c, p = cm.shape
    gpc = V7X_MXU_DIM // c
    ct = cm.transpose(0, 2, 1).reshape(g // gpc, gpc, p, c)
    out = jnp.einsum('ngpc,gh->ngphc', ct, jnp.eye(gpc, dtype=cm.dtype))
    return out.reshape(g // gpc, gpc * p, gpc * c)


def _moe_kernel(x_ref, g_ref, wrt_ref, wgu_ref, wdn_ref, o_ref, xn_ref, pos_ref, gate_ref, *, sub):
    e = pl.program_id(1)
    tm = x_ref.shape[0]
    ne = wrt_ref.shape[0]
    dff = wdn_ref.shape[1]

    @pl.when(e == 0)
    def _():
        x = x_ref[...]
        xn = _rms(x, g_ref[...])
        xn_hi = xn.astype(BF16)
        xn_ref[...] = xn_hi
        xn_lo = (xn - xn_hi.astype(F32)).astype(BF16)
        wr = wrt_ref[...]
        wr_hi = wr.astype(BF16)
        wr_lo = (wr - wr_hi.astype(F32)).astype(BF16)
        nt = lambda a, b: lax.dot_general(a, b, (((1,), (1,)), ((), ())), preferred_element_type=F32)
        logits = nt(wr_hi, xn_hi) + (nt(wr_hi, xn_lo) + nt(wr_lo, xn_hi))
        ex = jnp.exp(logits - jnp.max(logits, axis=0, keepdims=True))
        probs = ex / jnp.sum(ex, axis=0, keepdims=True)
        eid = lax.broadcasted_iota(jnp.int32, (ne, tm), 0).astype(F32)
        m1 = jnp.max(probs, axis=0, keepdims=True)
        i1 = jnp.min(jnp.where(probs == m1, eid, float(ne)), axis=0, keepdims=True)
        sel1 = eid == i1
        rest = jnp.where(sel1, -1.0, probs)
        m2 = jnp.max(rest, axis=0, keepdims=True)
        i2 = jnp.min(jnp.where(rest == m2, eid, float(ne)), axis=0, keepdims=True)
        sel2 = eid == i2
        den = m1 + m2
        gate_ref[...] = jnp.where(sel1, m1 / den, 0.0) + jnp.where(sel2, m2 / den, 0.0)
        chosen = jnp.where(sel1, 1.0, jnp.where(sel2, 1.0, 0.0))
        r = lax.broadcasted_iota(jnp.int32, (tm, tm), 0)
        c = lax.broadcasted_iota(jnp.int32, (tm, tm), 1)
        before = jnp.where(r < c, 1.0, 0.0).astype(BF16)
        rank = _dot(chosen.astype(BF16), before)
        pos_ref[...] = jnp.where(chosen > 0.0, rank, -1.0)
        o_ref[...] = x

    pos_e = pos_ref[pl.ds(e, 1), :]
    gate_e = gate_ref[pl.ds(e, 1), :]
    cnt = jnp.sum(jnp.where(pos_e >= 0.0, 1.0, 0.0)).astype(jnp.int32)

    def run_block(base, rows):
        slot = lax.broadcasted_iota(jnp.int32, (rows, tm), 0) + base
        hit = pos_e == slot.astype(F32)
        onehot = jnp.where(hit, 1.0, 0.0).astype(BF16)
        xs = _dot(onehot, xn_ref[...]).astype(BF16)
        gs = jnp.sum(jnp.where(hit, gate_e, 0.0), axis=1, keepdims=True)
        hgu = _dot(xs, wgu_ref[0])
        act = (jax.nn.silu(hgu[:, :dff]) * hgu[:, dff:]).astype(BF16)
        yb = (_dot(act, wdn_ref[0]) * gs).astype(BF16)
        o_ref[...] += lax.dot_general(onehot, yb, (((0,), (0,)), ((), ())), preferred_element_type=F32)

    def body(s, carry):
        run_block(s * sub, sub)
        return carry

    n_full = cnt // sub
    lax.fori_loop(0, n_full, body, 0)
    rem = cnt - n_full * sub
    lo = 0
    for rows in MOE_TAIL_ROWS:
        pl.when((rem > lo) & (rem <= rows))(functools.partial(run_block, n_full * sub, rows))
        lo = rows


def moe_residual(x, g, w_router_t, w_gu, w_down, li, tm, sub):
    n, d = x.shape
    _, ne, _, dff2 = w_gu.shape
    assert MOE_TAIL_ROWS[-1] == sub
    return pl.pallas_call(
        functools.partial(_moe_kernel, sub=sub),
        out_shape=jax.ShapeDtypeStruct((n, d), F32),
        grid=(n // tm, ne),
        in_specs=[
            pl.BlockSpec((tm, d), lambda i, e: (i, 0)),
            _full((1, d)),
            _full((ne, d)),
            pl.BlockSpec((None, 1, d, dff2), lambda i, e: (li, e, 0, 0)),
            pl.BlockSpec((None, 1, dff2 // 2, d), lambda i, e: (li, e, 0, 0)),
        ],
        out_specs=pl.BlockSpec((tm, d), lambda i, e: (i, 0)),
        scratch_shapes=[
            pltpu.VMEM((tm, d), BF16),
            pltpu.VMEM((ne, tm), F32),
            pltpu.VMEM((ne, tm), F32),
        ],
        compiler_params=_params("parallel", "arbitrary"),
        name="moe",
    )(x, g.reshape(1, d), w_router_t, w_gu, w_down)


def _trunk(x, n_t, rb, tb, nb, h0, conv0, s0, re0, im0, w):
    n, d = x.shape
    depth = w['norm_mix'].shape[0]
    d_a = h0.shape[-1]
    nh, dk = s0.shape[2], s0.shape[3]
    d_b = nh * dk
    ns = re0.shape[-2] * re0.shape[-1]
    tm = tb * rb
    n_even = (depth + 1) // 2
    new = {k: [] for k in ('h', 'conv', 're', 'im')}
    s_stack = jnp.zeros((n_even,) + s0.shape[1:], F32)
    for l in range(depth):
        li = l // 2
        if l % 2 == 0:
            proj = norm_matmul(x, w['norm_mix'][l], w['even_w_in'], li, tm, w['even_w_in'].shape[2])
            conv_tm = conv0[li].transpose(1, 0, 2).reshape((CONV_W - 1) * rb, d_a)
            ya, h_new, conv_new = rglru(proj, n_t, conv_tm, h0[li], w['rglru_conv_w'][li], w['rglru_conv_b'][li],
                                        w['rglru_wg'][li], w['rglru_bg'][li], w['rglru_lambda'][li], tb=tb, rb=rb)
            new['h'].append(h_new)
            new['conv'].append(conv_new.reshape(CONV_W - 1, rb, d_a).transpose(1, 0, 2))
            ob, s_stack = hgrn2(proj, s0, li % s0.shape[0], s_stack, n_even,
                                w['hgrn2_lb_raw'], w['hgrn2_gnorm'][li], li, n_t=n_t, tb=tb, nb=nb)
            if nb != rb:
                ob = ob.reshape(rb // nb, n // rb, nb, d_b).transpose(1, 0, 2, 3).reshape(n, d_b)
            x = mix_ffn(ya, ob, x, w['even_w_out'], w['norm_ffn'][l], w['ffn_w_gu'], w['ffn_w_down'], li, tm,
                        w['ffn_w_down'].shape[1] // 2)
        else:
            z, re_new, im_new = s5(x, n_t, w['norm_mix'][l], w['s5_wb'], w['s5_wc'], li, w['s5_a_re'][li],
                                   w['s5_a_im'][li], w['s5_d'][li], re0[li].reshape(rb, ns), im0[li].reshape(rb, ns),
                                   tb=tb, rb=rb)
            new['re'].append(re_new.reshape(re0.shape[1:]))
            new['im'].append(im_new.reshape(im0.shape[1:]))
            x = glu_residual(z, w['s5_w_glu'], li, x, tm, w['s5_w_glu'].shape[2] // 2)
            tm_moe = 2 * tm if n % (2 * tm) == 0 else tm
            x = moe_residual(x, w['norm_ffn'][l], w['moe_w_router_t'][li], w['moe_w_gu'], w['moe_w_down'], li,
                             tm_moe, MOE_TAIL_ROWS[-1])
    stack = lambda k: jnp.stack(new[k])
    return x, (stack('h'), stack('conv'), s_stack, stack('re'), stack('im'))


def kernel(x_prompt, x_sample, state_rglru_h, state_rglru_conv, state_hgrn2, state_s5_re, state_s5_im,
           meta_tokens, norm_mix, norm_ffn, norm_final, even_w_in, even_w_out,
           rglru_conv_w, rglru_conv_b, rglru_w_a, rglru_b_a, rglru_w_x, rglru_b_x, rglru_lambda,
           hgrn2_lb_raw, hgrn2_gnorm, s5_lam_re, s5_lam_im, s5_log_dt, s5_b_re, s5_b_im, s5_c_re, s5_c_im,
           s5_d, s5_w_glu, ffn_w_gu, ffn_w_down, moe_w_router, moe_w_gu, moe_w_down):
    bp, tp0, d = x_prompt.shape
    bs, ts, _ = x_sample.shape
    tp = tp0 + N_META
    d_a = state_rglru_h.shape[-1]
    n_even, n_odd = state_rglru_h.shape[0], state_s5_re.shape[0]
    assert bp == SUBLANES and bs % SUBLANES == 0
    tb_p = PROMPT_BLOCK_ROWS // bp
    tp_pad = -(-tp // tb_p) * tb_p
    nb_s = TIME_CHUNK_ROWS // ts
    assert (tp % (TIME_CHUNK_ROWS // bp) == 0 and tb_p % (TIME_CHUNK_ROWS // bp) == 0 and bs % nb_s == 0
            and nb_s * ts == TIME_CHUNK_ROWS and nb_s % SUBLANES == 0)

    eye_a = jnp.eye(H_A, dtype=F32)
    block_diag = lambda m: jnp.einsum('lhij,hg->lhigj', m, eye_a).reshape(n_even, d_a, d_a)
    s5_g, s5_p, s5_c = s5_b_re.shape[1:]
    lg = n_odd * s5_g
    a_re, a_im, bt_re, bt_im = s5_prep(s5_lam_re.reshape(lg, s5_p), s5_lam_im.reshape(lg, s5_p), s5_log_dt.reshape(lg),
                                       s5_b_re.reshape(lg, s5_p, s5_c), s5_b_im.reshape(lg, s5_p, s5_c))
    s5_wb = jnp.concatenate([_s5_block_diag_in(bt_re.reshape(lg, s5_p, s5_c)),
                             _s5_block_diag_in(bt_im.reshape(lg, s5_p, s5_c))], axis=2).astype(BF16)
    s5_wc = jnp.concatenate([_s5_block_diag_out(s5_c_re.reshape(lg, s5_c, s5_p)),
                             -_s5_block_diag_out(s5_c_im.reshape(lg, s5_c, s5_p))], axis=1).astype(BF16)
    ns = s5_lam_re.shape[1] * s5_lam_re.shape[2]
    w = {
        'norm_mix': norm_mix, 'norm_ffn': norm_ffn, 'norm_final': norm_final,
        'even_w_in': even_w_in.astype(BF16), 'even_w_out': even_w_out.astype(BF16),
        'rglru_conv_w': rglru_conv_w, 'rglru_conv_b': rglru_conv_b,
        'rglru_wg': jnp.concatenate([block_diag(rglru_w_a), block_diag(rglru_w_x)], axis=2).astype(BF16),
        'rglru_bg': jnp.concatenate([rglru_b_a, rglru_b_x], axis=1),
        'rglru_lambda': rglru_lambda, 'hgrn2_lb_raw': hgrn2_lb_raw, 'hgrn2_gnorm': hgrn2_gnorm,
        's5_a_re': a_re.reshape(n_odd, 1, ns), 's5_a_im': a_im.reshape(n_odd, 1, ns),
        's5_wb': s5_wb.reshape((n_odd, -1) + s5_wb.shape[1:]), 's5_wc': s5_wc.reshape((n_odd, -1) + s5_wc.shape[1:]),
        's5_d': s5_d, 's5_w_glu': s5_w_glu.astype(BF16),
        'ffn_w_gu': ffn_w_gu.astype(BF16), 'ffn_w_down': ffn_w_down.astype(BF16),
        'moe_w_router_t': moe_w_router.transpose(0, 2, 1),
        'moe_w_gu': moe_w_gu.astype(BF16), 'moe_w_down': moe_w_down.astype(BF16),
    }

    tc_p = TIME_CHUNK_ROWS // bp
    assert N_META == tc_p
    xm = to_time_major(x_prompt, meta_tokens.astype(x_prompt.dtype), tp_pad, tb_p // tc_p)
    zero = lambda ref, lead: jnp.zeros((lead, bp) + ref.shape[2:], ref.dtype)
    xp, p_new = _trunk(xm, tp, bp, tb_p, bp, zero(state_rglru_h, n_even), zero(state_rglru_conv, n_even),
                       zero(state_hgrn2, 1), zero(state_s5_re, n_odd), zero(state_s5_im, n_odd), w)
    nk_out = max(k for k in range(1, TIME_CHUNK_ROWS // tc_p + 1) if tp0 % (k * tc_p) == 0)
    y_prompt = rmsnorm_batch_major(xp, norm_final, bp, N_META, tp0, tc_p, nk_out)

    xs = x_sample.transpose(1, 0, 2).reshape(ts * bs, d)
    xs, s_new = _trunk(xs, ts, bs, ts, nb_s, state_rglru_h, state_rglru_conv, state_hgrn2, state_s5_re, state_s5_im, w)
    y_sample = rmsnorm_rows(xs, norm_final, ts * bs).reshape(ts, bs, d).transpose(1, 0, 2)

    refs = (state_rglru_h, state_rglru_conv, state_hgrn2, state_s5_re, state_s5_im)
    cast = lambda new: tuple(a.astype(r.dtype) for a, r in zip(new, refs))
    return (y_prompt, y_sample) + cast(p_new) + cast(s_new)
```

```python
import functools

import jax
import jax.numpy as jnp
from jax import lax
from jax.experimental import pallas as pl
from jax.experimental.pallas import tpu as pltpu

F32 = jnp.float32
BF16 = jnp.bfloat16

EPS = 1e-6
N_META = 16
CONV_W = 4
RG_C = 8.0
H_A = 8

V7X_VMEM_LIMIT_BYTES = 56 * 1024 * 1024
SUBLANES = 8
LANES = 128
V7X_MXU_DIM = 256
MOE_TAIL_ROWS = (128, 160, 192, 224, 256, 288, 320)
TIME_CHUNK_ROWS = 128
HGRN_ROWS = 256
S5_MAX_SUB_ROWS = 384
HGRN_EXP_CLAMP = 80.0
PROMPT_BLOCK_ROWS = 768
SCAN_CARRY_ELEMS = 16 * SUBLANES * LANES


def _params(*sem):
    return pltpu.CompilerParams(dimension_semantics=sem, vmem_limit_bytes=V7X_VMEM_LIMIT_BYTES)


def _rms(x, g):
    ms = jnp.mean(x * x, axis=-1, keepdims=True)
    return x * lax.rsqrt(ms + EPS) * g


def _dot(a, b):
    return jnp.dot(a, b, preferred_element_type=F32)


def _full(shape):
    return pl.BlockSpec(shape, lambda *_: (0,) * len(shape))


def _norm_matmul_kernel(x_ref, g_ref, w_ref, o_ref, xn_ref):
    @pl.when(pl.program_id(1) == 0)
    def _():
        xn_ref[...] = _rms(x_ref[...], g_ref[...]).astype(BF16)

    o_ref[...] = _dot(xn_ref[...], w_ref[...]).astype(o_ref.dtype)


def norm_matmul(x, g, w, li, tm, tn):
    n, d = x.shape
    nout = w.shape[2]
    return pl.pallas_call(
        _norm_matmul_kernel,
        out_shape=jax.ShapeDtypeStruct((n, nout), BF16),
        grid=(n // tm, nout // tn),
        in_specs=[
            pl.BlockSpec((tm, d), lambda i, j: (i, 0)),
            _full((1, d)),
            pl.BlockSpec((None, d, tn), lambda i, j: (li, 0, j)),
        ],
        out_specs=pl.BlockSpec((tm, tn), lambda i, j: (i, j)),
        scratch_shapes=[pltpu.VMEM((tm, d), BF16)],
        compiler_params=_params("parallel", "arbitrary"),
        name="norm_matmul",
    )(x, g.reshape(1, d), w)


def _mix_ffn_kernel(ya_ref, ob_ref, x_ref, wo_ref, g_ref, wg_ref, wu_ref, wdn_ref, o_ref, xn_ref):
    j = pl.program_id(1)
    d_a = ya_ref.shape[1]

    @pl.when(j == 0)
    def _():
        x1 = x_ref[...] + _dot(ya_ref[...], wo_ref[:d_a, :]) + _dot(ob_ref[...], wo_ref[d_a:, :])
        xn_ref[...] = _rms(x1, g_ref[...]).astype(BF16)
        o_ref[...] = x1

    xn = xn_ref[...]
    h = (jax.nn.silu(_dot(xn, wg_ref[...])) * _dot(xn, wu_ref[...])).astype(BF16)
    o_ref[...] += _dot(h, wdn_ref[...])


def mix_ffn(ya, ob, x, w_out, g, w_gu, w_down, li, tm, chunk):
    n, d = x.shape
    dff = w_down.shape[1]
    assert dff % chunk == 0 and chunk % LANES == 0
    nj = dff // chunk
    rows = lambda a: pl.BlockSpec((tm, a.shape[1]), lambda i, j: (i, 0))
    return pl.pallas_call(
        _mix_ffn_kernel,
        out_shape=jax.ShapeDtypeStruct((n, d), F32),
        grid=(n // tm, nj),
        in_specs=[
            rows(ya), rows(ob), rows(x),
            pl.BlockSpec((None,) + w_out.shape[1:], lambda i, j: (li, 0, 0)),
            _full((1, d)),
            pl.BlockSpec((None, d, chunk), lambda i, j: (li, 0, j)),
            pl.BlockSpec((None, d, chunk), lambda i, j: (li, 0, j + nj)),
            pl.BlockSpec((None, chunk, d), lambda i, j: (li, j, 0)),
        ],
        out_specs=pl.BlockSpec((tm, d), lambda i, j: (i, 0)),
        scratch_shapes=[pltpu.VMEM((tm, d), BF16)],
        compiler_params=_params("parallel", "arbitrary"),
        name="mix_ffn",
    )(ya, ob, x, w_out, g.reshape(1, d), w_gu, w_gu, w_down)


def _glu_residual_kernel(z_ref, wv_ref, wg_ref, r_ref, o_ref):
    z = z_ref[...]
    val = _dot(z, wv_ref[...])
    gate = _dot(z, wg_ref[...])
    o_ref[...] = r_ref[...] + val * jax.nn.sigmoid(gate)


def glu_residual(z, w_glu, li, res, tm, tn):
    n, d = z.shape
    dout = w_glu.shape[2] // 2
    nj = dout // tn
    return pl.pallas_call(
        _glu_residual_kernel,
        out_shape=jax.ShapeDtypeStruct((n, dout), F32),
        grid=(n // tm, nj),
        in_specs=[
            pl.BlockSpec((tm, d), lambda i, j: (i, 0)),
            pl.BlockSpec((None, d, tn), lambda i, j: (li, 0, j)),
            pl.BlockSpec((None, d, tn), lambda i, j: (li, 0, j + nj)),
            pl.BlockSpec((tm, tn), lambda i, j: (i, j)),
        ],
        out_specs=pl.BlockSpec((tm, tn), lambda i, j: (i, j)),
        compiler_params=_params("parallel", "arbitrary"),
        name="glu_residual",
    )(z, w_glu, w_glu, res)


def _rmsnorm_kernel(x_ref, g_ref, o_ref):
    o_ref[...] = _rms(x_ref[...], g_ref[...])


def rmsnorm_rows(x, g, tm):
    n, d = x.shape
    return pl.pallas_call(
        _rmsnorm_kernel,
        out_shape=jax.ShapeDtypeStruct((n, d), F32),
        grid=(n // tm,),
        in_specs=[pl.BlockSpec((tm, d), lambda i: (i, 0)), _full((1, d))],
        out_specs=pl.BlockSpec((tm, d), lambda i: (i, 0)),
        compiler_params=_params("parallel"),
        name="final_rmsnorm",
    )(x, g.reshape(1, d))


def _rmsnorm_batch_major_kernel(*refs, nk, tc, nb):
    x_refs, g_ref, o_ref = refs[:nk], refs[nk], refs[nk + 1]
    d = o_ref.shape[-1]
    for k in range(nk):
        y = _rms(x_refs[k][...], g_ref[...])
        o_ref[:, k * tc:(k + 1) * tc, :] = jnp.swapaxes(y.reshape(tc, nb, d), 0, 1)


def rmsnorm_batch_major(x, g, nb, t_skip, t_out, tc, nk):
    n, d = x.shape
    rows = tc * nb
    assert t_skip % tc == 0 and t_out % (nk * tc) == 0
    specs = [pl.BlockSpec((rows, d), lambda j, k=k: (nk * j + t_skip // tc + k, 0)) for k in range(nk)]
    return pl.pallas_call(
        functools.partial(_rmsnorm_batch_major_kernel, nk=nk, tc=tc, nb=nb),
        out_shape=jax.ShapeDtypeStruct((nb, t_out, d), F32),
        grid=(t_out // (nk * tc),),
        in_specs=specs + [_full((1, d))],
        out_specs=pl.BlockSpec((nb, nk * tc, d), lambda j: (0, j, 0)),
        compiler_params=_params("parallel"),
        name="final_rmsnorm_batch_major",
    )(*([x] * nk), g.reshape(1, d))


def _to_time_major_kernel(*refs, nk, n_chunks):
    x_refs, lead_ref, o_ref = refs[:nk], refs[nk], refs[nk + 1]
    i = pl.program_id(0)
    nb, tc, d = x_refs[0].shape
    rows = tc * nb
    for k in range(nk):
        chunk = i * nk + k - 1
        val = jnp.swapaxes(x_refs[k][...], 0, 1).reshape(rows, d)
        if k == 0:
            lead = jnp.broadcast_to(lead_ref[...][:, None, :], (tc, nb, d)).reshape(rows, d)
            val = jnp.where(i == 0, lead, val)
        o_ref[k * rows:(k + 1) * rows, :] = jnp.where(chunk < n_chunks, val, 0.0)


def to_time_major(x, lead, t_pad, nk):
    nb, t, d = x.shape
    tc = lead.shape[0]
    assert t % tc == 0 and t_pad % (nk * tc) == 0
    n_chunks = t // tc
    specs = [pl.BlockSpec((nb, tc, d), lambda i, k=k: (0, jnp.clip(i * nk + k - 1, 0, n_chunks - 1), 0))
             for k in range(nk)]
    return pl.pallas_call(
        functools.partial(_to_time_major_kernel, nk=nk, n_chunks=n_chunks),
        out_shape=jax.ShapeDtypeStruct((t_pad * nb, d), F32),
        grid=(t_pad // (nk * tc),),
        in_specs=specs + [_full((tc, d))],
        out_specs=pl.BlockSpec((nk * tc * nb, d), lambda i: (i, 0)),
        compiler_params=_params("parallel"),
        name="to_time_major",
    )(*([x] * nk), lead)


def _rglru_kernel(xa_ref, ga_ref, conv0_ref, h0_ref, cw_ref, cb_ref, wg_ref, bg_ref, lam_ref,
                  ya_ref, hlast_ref, convnew_ref, xpad_ref, a_ref, u_ref, *, n_t, tb, rb):
    i = pl.program_id(0)
    rows = tb * rb
    tail = (CONV_W - 1) * rb
    c = xa_ref.shape[-1]
    t_valid = jnp.minimum(n_t - i * tb, tb)

    @pl.when(i == 0)
    def _():
        xpad_ref[0:tail, :] = conv0_ref[...]
        hlast_ref[...] = h0_ref[...]

    @pl.when(i > 0)
    def _():
        xpad_ref[0:tail, :] = xpad_ref[rows:rows + tail, :]

    xpad_ref[tail:tail + rows, :] = xa_ref[...].astype(F32)
    xc = cb_ref[...]
    for k in range(CONV_W):
        xc = xc + cw_ref[k:k + 1, :] * xpad_ref[k * rb:k * rb + rows, :]

    gates = _dot(xc.astype(BF16), wg_ref[...]) + bg_ref[...]
    r = jax.nn.sigmoid(gates[:, :c])
    ig = jax.nn.sigmoid(gates[:, c:])
    log_a = (-RG_C) * r * jax.nn.softplus(-lam_ref[...])
    a = jnp.exp(log_a)
    y = 1.0 - a * a
    mult = y * lax.rsqrt(jnp.maximum(y, 1e-30))
    a_ref[...] = a
    u_ref[...] = mult * ig * xc

    lc = min(c, max(LANES, SCAN_CARRY_ELEMS // rb // LANES * LANES))
    for c0 in range(0, c, lc):
        def body(t, h, c0=c0):
            sl = pl.ds(pl.multiple_of(t * rb, rb), rb)
            h = a_ref[sl, c0:c0 + lc] * h + u_ref[sl, c0:c0 + lc]
            u_ref[sl, c0:c0 + lc] = h
            return h

        hlast_ref[:, c0:c0 + lc] = lax.fori_loop(0, t_valid, body, hlast_ref[:, c0:c0 + lc])

    live = lax.broadcasted_iota(jnp.int32, (rows, c), 0) < t_valid * rb
    ya_ref[...] = jnp.where(live, u_ref[...] * jax.nn.gelu(ga_ref[...].astype(F32)), 0.0).astype(ya_ref.dtype)
    convnew_ref[...] = xpad_ref[pl.ds(pl.multiple_of(t_valid * rb, rb), tail), :]


def rglru(proj, n_t, conv0, h0, cw, cb, wg, bg, lam, tb, rb):
    c = h0.shape[1]
    n_rows = proj.shape[0]
    rows = tb * rb
    tail = (CONV_W - 1) * rb
    return pl.pallas_call(
        functools.partial(_rglru_kernel, n_t=n_t, tb=tb, rb=rb),
        out_shape=(
            jax.ShapeDtypeStruct((n_rows, c), BF16),
            jax.ShapeDtypeStruct((rb, c), F32),
            jax.ShapeDtypeStruct((tail, c), F32),
        ),
        grid=(n_rows // rows,),
        in_specs=[
            pl.BlockSpec((rows, c), lambda i: (i, 0)),
            pl.BlockSpec((rows, c), lambda i: (i, 1)),
            _full((tail, c)), _full((rb, c)), _full((CONV_W, c)), _full((1, c)),
            _full((c, 2 * c)), _full((1, 2 * c)), _full((1, c)),
        ],
        out_specs=(
            pl.BlockSpec((rows, c), lambda i: (i, 0)),
            _full((rb, c)),
            _full((tail, c)),
        ),
        scratch_shapes=[
            pltpu.VMEM((rows + tail, c), F32),
            pltpu.VMEM((rows, c), F32),
            pltpu.VMEM((rows, c), F32),
        ],
        compiler_params=_params("arbitrary"),
        name="rglru",
    )(proj, proj, conv0, h0, cw, cb.reshape(1, c), wg, bg.reshape(1, 2 * c), lam.reshape(1, c))


def _hgrn2_kernel(q_ref, f_ref, v_ref, gb_ref, s0_ref, lbraw_ref, gn_ref, stack_ref, ob_ref, snew_ref, st_ref,
                  *, layer, n_t, tb, nb, nh, m):
    del stack_ref
    i = pl.program_id(1)
    tc = m // nb
    half = tc // 2
    d = q_ref.shape[-1]
    dk = d // nh
    t_valid = jnp.minimum(n_t - i * tb, tb)
    n_chunks = (t_valid + tc - 1) // tc

    @pl.when(i == 0)
    def _():
        for b in range(nb):
            for h in range(nh):
                st_ref[h, :, b * dk:(b + 1) * dk] = s0_ref[b, h].T

    @pl.when(n_chunks < tb // tc)
    def _():
        ob_ref[...] = jnp.zeros_like(ob_ref)

    p = jax.nn.softmax(lbraw_ref[...], axis=0)
    cum = p[0:1, :]
    for r in range(1, layer + 1):
        cum = cum + p[r:r + 1, :]
    lb = cum - p[0:1, :]
    log_lb = jnp.log(lb)
    log_1mlb = jnp.log1p(-lb)

    assert tc % 2 == 0 and nb & (nb - 1) == 0
    row = lax.broadcasted_iota(jnp.int32, (m, m), 0)
    col = lax.broadcasted_iota(jnp.int32, (m, m), 1)
    same_seq_causal = jnp.where(((row & (nb - 1)) == (col & (nb - 1))) & (col <= row), 1.0, 0.0)
    bid = lax.broadcasted_iota(jnp.int32, (m, dk), 0) & (nb - 1)

    def chunk(c, carry):
        def rows_of(ref):
            if len(ref.shape) == 2:
                return ref[pl.ds(pl.multiple_of(c * m, m), m), :]
            return ref[pl.ds(c * tc, tc)].reshape(m, d)

        live = lax.broadcasted_iota(jnp.int32, (m, d), 0) < (t_valid - c * tc) * nb
        q = jax.nn.silu(rows_of(q_ref).astype(F32))
        fr = rows_of(f_ref).astype(F32)
        v = rows_of(v_ref).astype(BF16)
        gb = rows_of(gb_ref).astype(F32)
        logf = jnp.where(live, jnp.logaddexp(log_lb, log_1mlb + jax.nn.log_sigmoid(fr)), 0.0)
        k = 1.0 - jnp.exp(logf)
        slabs = [logf[0:nb]]
        for t in range(1, tc):
            slabs.append(slabs[-1] + logf[t * nb:(t + 1) * nb])
        g = jnp.concatenate(slabs, axis=0)
        g_last = jnp.concatenate([slabs[-1]] * tc, axis=0)
        g_rel = g - jnp.concatenate([slabs[half - 1]] * tc, axis=0)
        qa = (q * jnp.exp(jnp.minimum(g_rel, HGRN_EXP_CLAMP))).astype(BF16)
        kt = (k * jnp.exp(jnp.minimum(-g_rel, HGRN_EXP_CLAMP))).astype(BF16)
        qt = (q * jnp.exp(g)).astype(BF16)
        ks = (k * jnp.exp(g_last - g)).astype(BF16)
        dec = jnp.exp(slabs[-1])

        outs = []
        for h in range(nh):
            hs = slice(h * dk, (h + 1) * dk)
            qt_h, kt_h, ks_h, v_h = qt[:, hs], kt[:, hs], ks[:, hs], v[:, hs]
            att = lax.dot_general(qa[:, hs], kt_h, (((1,), (1,)), ((), ())), preferred_element_type=F32)
            att = (att * same_seq_causal).astype(BF16)
            o = _dot(att, v_h)
            zero = jnp.zeros_like(qt_h)
            expand = lambda x: jnp.concatenate([jnp.where(bid == b, x, zero) for b in range(nb)], axis=1)
            st_h = st_ref[h]
            o = o + lax.dot_general(expand(qt_h), st_h.astype(BF16), (((1,), (1,)), ((), ())),
                                    preferred_element_type=F32)
            dst = lax.dot_general(v_h, expand(ks_h), (((0,), (0,)), ((), ())), preferred_element_type=F32)
            dec_row = jnp.concatenate([dec[b:b + 1, hs] for b in range(nb)], axis=1)
            st_ref[h] = st_h * dec_row + dst
            outs.append(_rms(o, gn_ref[:, hs]))
        ob = jnp.where(live, jnp.concatenate(outs, axis=1) * jax.nn.silu(gb), 0.0)
        ob_ref[pl.ds(pl.multiple_of(c * m, m), m), :] = ob.astype(ob_ref.dtype)
        return carry

    lax.fori_loop(0, n_chunks, chunk, 0)

    @pl.when(i == pl.num_programs(1) - 1)
    def _():
        for b in range(nb):
            for h in range(nh):
                snew_ref[b, h] = st_ref[h, :, b * dk:(b + 1) * dk].T


def hgrn2(proj, s0, s0_layer, s_stack, n_layers, lb_raw, gnorm, layer, n_t, tb, nb):
    _, bsz, nh, dk, _ = s0.shape
    d = nh * dk
    off = proj.shape[-1] // d - 4
    nbb = bsz // nb
    n_tpad = proj.shape[0] // bsz
    nt = n_tpad // tb
    m = max(k for k in range(2 * nb, HGRN_ROWS + 1, 2 * nb) if (tb * nb) % k == 0)
    if nb == bsz:
        col = lambda k: pl.BlockSpec((tb * nb, d), lambda j, i, k=k: (i, k + off))
    else:
        proj = proj.reshape(n_tpad, bsz, proj.shape[1])
        col = lambda k: pl.BlockSpec((tb, nb, d), lambda j, i, k=k: (i, j, k + off))
    in_specs = [
        col(0), col(1), col(2), col(3),
        pl.BlockSpec((None, nb, nh, dk, dk), lambda j, i: (s0_layer, j, 0, 0, 0)),
        _full(lb_raw.shape),
        _full((1, d)),
        pl.BlockSpec(memory_space=pl.ANY),
    ]
    args = [proj, proj, proj, proj, s0, lb_raw, gnorm.reshape(1, d), s_stack]
    assert s_stack.shape == (n_layers, bsz, nh, dk, dk)
    return pl.pallas_call(
        functools.partial(_hgrn2_kernel, layer=layer, n_t=n_t, tb=tb, nb=nb, nh=nh, m=m),
        out_shape=(
            jax.ShapeDtypeStruct((n_tpad * bsz, d), BF16),
            jax.ShapeDtypeStruct((n_layers, bsz, nh, dk, dk), F32),
        ),
        grid=(nbb, nt),
        in_specs=in_specs,
        out_specs=(
            pl.BlockSpec((tb * nb, d), lambda j, i: (j * nt + i, 0)),
            pl.BlockSpec((None, nb, nh, dk, dk), lambda j, i: (layer, j, 0, 0, 0)),
        ),
        scratch_shapes=[pltpu.VMEM((nh, dk, nb * dk), F32)],
        input_output_aliases={len(args) - 1: 1},
        compiler_params=_params("arbitrary", "arbitrary"),
        name="hgrn2",
    )(*args)


def _s5_discretise(lr, li, ldt):
    dt = jnp.exp(ldt)
    mag = jnp.exp(lr * dt)
    return mag * jnp.cos(li * dt), mag * jnp.sin(li * dt)


def _s5_prep_kernel(lr_ref, li_ref, ldt_ref, lrc_ref, lic_ref, ldtc_ref, bre_ref, bim_ref,
                    ar_ref, ai_ref, ore_ref, oim_ref):
    ar_ref[...], ai_ref[...] = _s5_discretise(lr_ref[...], li_ref[...], ldt_ref[...])
    lr = lrc_ref[...]
    li = lic_ref[...]
    ar, ai = _s5_discretise(lr, li, ldtc_ref[...])
    den = lr * lr + li * li
    cr = ((ar - 1.0) * lr + ai * li) / den
    ci = (ai * lr - (ar - 1.0) * li) / den
    ore_ref[...] = cr * bre_ref[...] - ci * bim_ref[...]
    oim_ref[...] = cr * bim_ref[...] + ci * bre_ref[...]


def s5_prep(lam_re, lam_im, log_dt, b_re, b_im):
    g, p, c = b_re.shape
    dense = lambda a: a.reshape(-1, LANES)
    per_state = lambda a: dense(jnp.broadcast_to(a.reshape(g, -1, 1), (g, p, 1)))
    per_coef = lambda a: dense(jnp.broadcast_to(a.reshape(g, -1, 1), (g, p, c)))
    small = jax.ShapeDtypeStruct((g * p // LANES, LANES), F32)
    big = jax.ShapeDtypeStruct((g * p * c // LANES, LANES), F32)
    ar, ai, ore, oim = pl.pallas_call(
        _s5_prep_kernel, out_shape=(small, small, big, big), name="s5_prep",
    )(per_state(lam_re), per_state(lam_im), per_state(log_dt), per_coef(lam_re), per_coef(lam_im), per_coef(log_dt),
      dense(b_re), dense(b_im))
    return ar.reshape(g, p), ai.reshape(g, p), ore.reshape(g, p, c), oim.reshape(g, p, c)


def _s5_kernel(x_ref, g_ref, wb_ref, wc_ref, ar_ref, ai_ref, d_ref, h0r_ref, h0i_ref,
               z_ref, hr_ref, hi_ref, *, n_t, tb, rb, m):
    i = pl.program_id(0)
    nch = wb_ref.shape[0]
    cw = wb_ref.shape[1]
    sw = wb_ref.shape[2] // 2
    tsb = m // rb
    nsl = rb // SUBLANES
    n_sub = jnp.minimum(n_t - i * tb, tb) // tsb

    @pl.when(i == 0)
    def _():
        hr_ref[...] = h0r_ref[...]
        hi_ref[...] = h0i_ref[...]

    @pl.when(n_sub < tb // tsb)
    def _():
        z_ref[...] = jnp.zeros_like(z_ref)

    def sub_block(j, carry):
        r0 = pl.multiple_of(j * m, m)
        xn = _rms(x_ref[pl.ds(r0, m), :], g_ref[...])
        xnb = xn.astype(BF16)
        drive = [_dot(xnb[:, c * cw:(c + 1) * cw], wb_ref[c]) for c in range(nch)]
        for c in range(nch):
            cs = slice(c * cw, (c + 1) * cw)
            ss = slice(c * sw, (c + 1) * sw)
            u = xn[:, cs]
            bu = drive[c]
            ar = jnp.broadcast_to(ar_ref[:, ss], (SUBLANES, sw))
            ai = jnp.broadcast_to(ai_ref[:, ss], (SUBLANES, sw))
            out_r = [None] * (tsb * nsl)
            out_i = [None] * (tsb * nsl)
            for s in range(nsl):
                srow = slice(s * SUBLANES, (s + 1) * SUBLANES)
                hr, hi = hr_ref[srow, ss], hi_ref[srow, ss]
                for t in range(tsb):
                    lo = t * rb + s * SUBLANES
                    hr, hi = (ar * hr - ai * hi + bu[lo:lo + SUBLANES, :sw],
                              ar * hi + ai * hr + bu[lo:lo + SUBLANES, sw:])
                    out_r[t * nsl + s] = hr
                    out_i[t * nsl + s] = hi
                hr_ref[srow, ss] = hr
                hi_ref[srow, ss] = hi
            hcat = jnp.concatenate([jnp.concatenate(out_r, axis=0).astype(BF16),
                                    jnp.concatenate(out_i, axis=0).astype(BF16)], axis=1)
            y = _dot(hcat, wc_ref[c]) + d_ref[:, cs] * u
            z_ref[pl.ds(r0, m), cs] = jax.nn.gelu(y).astype(z_ref.dtype)
        return carry

    lax.fori_loop(0, n_sub, sub_block, 0)


def s5(x, n_t, g, wb, wc, li, a_re, a_im, dskip, h0r, h0i, tb, rb):
    n_rows, d = x.shape
    ns = h0r.shape[1]
    rows = tb * rb
    m = max(k for k in range(rb, S5_MAX_SUB_ROWS + 1, rb)
            if rows % k == 0 and n_t % (k // rb) == 0 and k % (2 * SUBLANES) == 0)
    return pl.pallas_call(
        functools.partial(_s5_kernel, n_t=n_t, tb=tb, rb=rb, m=m),
        out_shape=(
            jax.ShapeDtypeStruct((n_rows, d), BF16),
            jax.ShapeDtypeStruct((rb, ns), F32),
            jax.ShapeDtypeStruct((rb, ns), F32),
        ),
        grid=(n_rows // rows,),
        in_specs=[
            pl.BlockSpec((rows, d), lambda i: (i, 0)),
            _full((1, d)),
            pl.BlockSpec((None,) + wb.shape[1:], lambda i: (li, 0, 0, 0)),
            pl.BlockSpec((None,) + wc.shape[1:], lambda i: (li, 0, 0, 0)),
            _full((1, ns)), _full((1, ns)), _full((1, d)),
            _full((rb, ns)), _full((rb, ns)),
        ],
        out_specs=(
            pl.BlockSpec((rows, d), lambda i: (i, 0)),
            _full((rb, ns)), _full((rb, ns)),
        ),
        compiler_params=_params("arbitrary"),
        name="s5",
    )(x, g.reshape(1, d), wb, wc, a_re, a_im, dskip.reshape(1, d), h0r, h0i)


def _s5_block_diag_in(b):
    g, p, c = b.shape
    gpc = V7X_MXU_DIM // c
    bt = b.transpose(0, 2, 1).reshape(g // gpc, gpc, c, p)
    out = jnp.einsum('ngcp,gh->ngchp', bt, jnp.eye(gpc, dtype=b.dtype))
    return out.reshape(g // gpc, gpc * c, gpc * p)


def _s5_block_diag_out(cm):
    g, c, p = cm.shape
    gpc = V7X_MXU_DIM // c
    ct = cm.transpose(0, 2, 1).reshape(g // gpc, gpc, p, c)
    out = jnp.einsum('ngpc,gh->ngphc', ct, jnp.eye(gpc, dtype=cm.dtype))
    return out.reshape(g // gpc, gpc * p, gpc * c)


def _moe_kernel(x_ref, g_ref, wrt_ref, wgu_ref, wdn_ref, o_ref, xn_ref, pos_ref, gate_ref, *, sub):
    e = pl.program_id(1)
    tm = x_ref.shape[0]
    ne = wrt_ref.shape[0]
    dff = wdn_ref.shape[1]

    @pl.when(e == 0)
    def _():
        x = x_ref[...]
        xn = _rms(x, g_ref[...])
        xn_hi = xn.astype(BF16)
        xn_ref[...] = xn_hi
        xn_lo = (xn - xn_hi.astype(F32)).astype(BF16)
        wr = wrt_ref[...]
        wr_hi = wr.astype(BF16)
        wr_lo = (wr - wr_hi.astype(F32)).astype(BF16)
        nt = lambda a, b: lax.dot_general(a, b, (((1,), (1,)), ((), ())), preferred_element_type=F32)
        logits = nt(wr_hi, xn_hi) + (nt(wr_hi, xn_lo) + nt(wr_lo, xn_hi))
        ex = jnp.exp(logits - jnp.max(logits, axis=0, keepdims=True))
        probs = ex / jnp.sum(ex, axis=0, keepdims=True)
        eid = lax.broadcasted_iota(jnp.int32, (ne, tm), 0).astype(F32)
        m1 = jnp.max(probs, axis=0, keepdims=True)
        i1 = jnp.min(jnp.where(probs == m1, eid, float(ne)), axis=0, keepdims=True)
        sel1 = eid == i1
        rest = jnp.where(sel1, -1.0, probs)
        m2 = jnp.max(rest, axis=0, keepdims=True)
        i2 = jnp.min(jnp.where(rest == m2, eid, float(ne)), axis=0, keepdims=True)
        sel2 = eid == i2
        den = m1 + m2
        gate_ref[...] = jnp.where(sel1, m1 / den, 0.0) + jnp.where(sel2, m2 / den, 0.0)
        chosen = jnp.where(sel1, 1.0, jnp.where(sel2, 1.0, 0.0))
        r = lax.broadcasted_iota(jnp.int32, (tm, tm), 0)
        c = lax.broadcasted_iota(jnp.int32, (tm, tm), 1)
        before = jnp.where(r < c, 1.0, 0.0).astype(BF16)
        rank = _dot(chosen.astype(BF16), before)
        pos_ref[...] = jnp.where(chosen > 0.0, rank, -1.0)
        o_ref[...] = x

    pos_e = pos_ref[pl.ds(e, 1), :]
    gate_e = gate_ref[pl.ds(e, 1), :]
    cnt = jnp.sum(jnp.where(pos_e >= 0.0, 1.0, 0.0)).astype(jnp.int32)

    def run_block(base, rows):
        slot = lax.broadcasted_iota(jnp.int32, (rows, tm), 0) + base
        hit = pos_e == slot.astype(F32)
        onehot = jnp.where(hit, 1.0, 0.0).astype(BF16)
        xs = _dot(onehot, xn_ref[...]).astype(BF16)
        gs = jnp.sum(jnp.where(hit, gate_e, 0.0), axis=1, keepdims=True)
        hgu = _dot(xs, wgu_ref[0])
        act = (jax.nn.silu(hgu[:, :dff]) * hgu[:, dff:]).astype(BF16)
        yb = (_dot(act, wdn_ref[0]) * gs).astype(BF16)
        o_ref[...] += lax.dot_general(onehot, yb, (((0,), (0,)), ((), ())), preferred_element_type=F32)

    def body(s, carry):
        run_block(s * sub, sub)
        return carry

    n_full = cnt // sub
    lax.fori_loop(0, n_full, body, 0)
    rem = cnt - n_full * sub
    lo = 0
    for rows in MOE_TAIL_ROWS:
        pl.when((rem > lo) & (rem <= rows))(functools.partial(run_block, n_full * sub, rows))
        lo = rows


def moe_residual(x, g, w_router_t, w_gu, w_down, li, tm, sub):
    n, d = x.shape
    _, ne, _, dff2 = w_gu.shape
    assert MOE_TAIL_ROWS[-1] == sub
    return pl.pallas_call(
        functools.partial(_moe_kernel, sub=sub),
        out_shape=jax.ShapeDtypeStruct((n, d), F32),
        grid=(n // tm, ne),
        in_specs=[
            pl.BlockSpec((tm, d), lambda i, e: (i, 0)),
            _full((1, d)),
            _full((ne, d)),
            pl.BlockSpec((None, 1, d, dff2), lambda i, e: (li, e, 0, 0)),
            pl.BlockSpec((None, 1, dff2 // 2, d), lambda i, e: (li, e, 0, 0)),
        ],
        out_specs=pl.BlockSpec((tm, d), lambda i, e: (i, 0)),
        scratch_shapes=[
            pltpu.VMEM((tm, d), BF16),
            pltpu.VMEM((ne, tm), F32),
            pltpu.VMEM((ne, tm), F32),
        ],
        compiler_params=_params("parallel", "arbitrary"),
        name="moe",
    )(x, g.reshape(1, d), w_router_t, w_gu, w_down)


def _trunk(x, n_t, rb, tb, nb, h0, conv0, s0, re0, im0, w):
    n, d = x.shape
    depth = w['norm_mix'].shape[0]
    d_a = h0.shape[-1]
    nh, dk = s0.shape[2], s0.shape[3]
    d_b = nh * dk
    ns = re0.shape[-2] * re0.shape[-1]
    tm = tb * rb
    n_even = (depth + 1) // 2
    new = {k: [] for k in ('h', 'conv', 're', 'im')}
    s_stack = jnp.zeros((n_even,) + s0.shape[1:], F32)
    for l in range(depth):
        li = l // 2
        if l % 2 == 0:
            proj = norm_matmul(x, w['norm_mix'][l], w['even_w_in'], li, tm, w['even_w_in'].shape[2])
            conv_tm = conv0[li].transpose(1, 0, 2).reshape((CONV_W - 1) * rb, d_a)
            ya, h_new, conv_new = rglru(proj, n_t, conv_tm, h0[li], w['rglru_conv_w'][li], w['rglru_conv_b'][li],
                                        w['rglru_wg'][li], w['rglru_bg'][li], w['rglru_lambda'][li], tb=tb, rb=rb)
            new['h'].append(h_new)
            new['conv'].append(conv_new.reshape(CONV_W - 1, rb, d_a).transpose(1, 0, 2))
            ob, s_stack = hgrn2(proj, s0, li % s0.shape[0], s_stack, n_even,
                                w['hgrn2_lb_raw'], w['hgrn2_gnorm'][li], li, n_t=n_t, tb=tb, nb=nb)
            if nb != rb:
                ob = ob.reshape(rb // nb, n // rb, nb, d_b).transpose(1, 0, 2, 3).reshape(n, d_b)
            x = mix_ffn(ya, ob, x, w['even_w_out'], w['norm_ffn'][l], w['ffn_w_gu'], w['ffn_w_down'], li, tm,
                        w['ffn_w_down'].shape[1] // 2)
        else:
            z, re_new, im_new = s5(x, n_t, w['norm_mix'][l], w['s5_wb'], w['s5_wc'], li, w['s5_a_re'][li],
                                   w['s5_a_im'][li], w['s5_d'][li], re0[li].reshape(rb, ns), im0[li].reshape(rb, ns),
                                   tb=tb, rb=rb)
            new['re'].append(re_new.reshape(re0.shape[1:]))
            new['im'].append(im_new.reshape(im0.shape[1:]))
            x = glu_residual(z, w['s5_w_glu'], li, x, tm, w['s5_w_glu'].shape[2] // 2)
            x = moe_residual(x, w['norm_ffn'][l], w['moe_w_router_t'][li], w['moe_w_gu'], w['moe_w_down'], li,
                             tm, MOE_TAIL_ROWS[-1])
    stack = lambda k: jnp.stack(new[k])
    return x, (stack('h'), stack('conv'), s_stack, stack('re'), stack('im'))


def kernel(x_prompt, x_sample, state_rglru_h, state_rglru_conv, state_hgrn2, state_s5_re, state_s5_im,
           meta_tokens, norm_mix, norm_ffn, norm_final, even_w_in, even_w_out,
           rglru_conv_w, rglru_conv_b, rglru_w_a, rglru_b_a, rglru_w_x, rglru_b_x, rglru_lambda,
           hgrn2_lb_raw, hgrn2_gnorm, s5_lam_re, s5_lam_im, s5_log_dt, s5_b_re, s5_b_im, s5_c_re, s5_c_im,
           s5_d, s5_w_glu, ffn_w_gu, ffn_w_down, moe_w_router, moe_w_gu, moe_w_down):
    bp, tp0, d = x_prompt.shape
    bs, ts, _ = x_sample.shape
    tp = tp0 + N_META
    d_a = state_rglru_h.shape[-1]
    n_even, n_odd = state_rglru_h.shape[0], state_s5_re.shape[0]
    assert bp == SUBLANES and bs % SUBLANES == 0
    tb_p = PROMPT_BLOCK_ROWS // bp
    tp_pad = -(-tp // tb_p) * tb_p
    nb_s = TIME_CHUNK_ROWS // ts
    assert (tp % (TIME_CHUNK_ROWS // bp) == 0 and tb_p % (TIME_CHUNK_ROWS // bp) == 0 and bs % nb_s == 0
            and nb_s * ts == TIME_CHUNK_ROWS and nb_s % SUBLANES == 0)

    eye_a = jnp.eye(H_A, dtype=F32)
    block_diag = lambda m: jnp.einsum('lhij,hg->lhigj', m, eye_a).reshape(n_even, d_a, d_a)
    s5_g, s5_p, s5_c = s5_b_re.shape[1:]
    lg = n_odd * s5_g
    a_re, a_im, bt_re, bt_im = s5_prep(s5_lam_re.reshape(lg, s5_p), s5_lam_im.reshape(lg, s5_p), s5_log_dt.reshape(lg),
                                       s5_b_re.reshape(lg, s5_p, s5_c), s5_b_im.reshape(lg, s5_p, s5_c))
    s5_wb = jnp.concatenate([_s5_block_diag_in(bt_re.reshape(lg, s5_p, s5_c)),
                             _s5_block_diag_in(bt_im.reshape(lg, s5_p, s5_c))], axis=2).astype(BF16)
    s5_wc = jnp.concatenate([_s5_block_diag_out(s5_c_re.reshape(lg, s5_c, s5_p)),
                             -_s5_block_diag_out(s5_c_im.reshape(lg, s5_c, s5_p))], axis=1).astype(BF16)
    ns = s5_lam_re.shape[1] * s5_lam_re.shape[2]
    w = {
        'norm_mix': norm_mix, 'norm_ffn': norm_ffn, 'norm_final': norm_final,
        'even_w_in': even_w_in.astype(BF16), 'even_w_out': even_w_out.astype(BF16),
        'rglru_conv_w': rglru_conv_w, 'rglru_conv_b': rglru_conv_b,
        'rglru_wg': jnp.concatenate([block_diag(rglru_w_a), block_diag(rglru_w_x)], axis=2).astype(BF16),
        'rglru_bg': jnp.concatenate([rglru_b_a, rglru_b_x], axis=1),
        'rglru_lambda': rglru_lambda, 'hgrn2_lb_raw': hgrn2_lb_raw, 'hgrn2_gnorm': hgrn2_gnorm,
        's5_a_re': a_re.reshape(n_odd, 1, ns), 's5_a_im': a_im.reshape(n_odd, 1, ns),
        's5_wb': s5_wb.reshape((n_odd, -1) + s5_wb.shape[1:]), 's5_wc': s5_wc.reshape((n_odd, -1) + s5_wc.shape[1:]),
        's5_d': s5_d, 's5_w_glu': s5_w_glu.astype(BF16),
        'ffn_w_gu': ffn_w_gu.astype(BF16), 'ffn_w_down': ffn_w_down.astype(BF16),
        'moe_w_router_t': moe_w_router.transpose(0, 2, 1),
        'moe_w_gu': moe_w_gu.astype(BF16), 'moe_w_down': moe_w_down.astype(BF16),
    }

    tc_p = TIME_CHUNK_ROWS // bp
    assert N_META == tc_p
    xm = to_time_major(x_prompt, meta_tokens.astype(x_prompt.dtype), tp_pad, tb_p // tc_p)
    zero = lambda ref, lead: jnp.zeros((lead, bp) + ref.shape[2:], ref.dtype)
    xp, p_new = _trunk(xm, tp, bp, tb_p, bp, zero(state_rglru_h, n_even), zero(state_rglru_conv, n_even),
                       zero(state_hgrn2, 1), zero(state_s5_re, n_odd), zero(state_s5_im, n_odd), w)
    nk_out = max(k for k in range(1, TIME_CHUNK_ROWS // tc_p + 1) if tp0 % (k * tc_p) == 0)
    y_prompt = rmsnorm_batch_major(xp, norm_final, bp, N_META, tp0, tc_p, nk_out)

    xs = x_sample.transpose(1, 0, 2).reshape(ts * bs, d)
    xs, s_new = _trunk(xs, ts, bs, ts, nb_s, state_rglru_h, state_rglru_conv, state_hgrn2, state_s5_re, state_s5_im, w)
    y_sample = rmsnorm_rows(xs, norm_final, ts * bs).reshape(ts, bs, d).transpose(1, 0, 2)

    refs = (state_rglru_h, state_rglru_conv, state_hgrn2, state_s5_re, state_s5_im)
    cast = lambda new: tuple(a.astype(r.dtype) for a, r in zip(new, refs))
    return (y_prompt, y_sample) + cast(p_new) + cast(s_new)
```

```python
import functools

import jax
import jax.numpy as jnp
from jax import lax
from jax.experimental import pallas as pl
from jax.experimental.pallas import tpu as pltpu

F32 = jnp.float32
BF16 = jnp.bfloat16

EPS = 1e-6
N_META = 16
CONV_W = 4
RG_C = 8.0
H_A = 8

V7X_VMEM_LIMIT_BYTES = 56 * 1024 * 1024
SUBLANES = 8
LANES = 128
V7X_MXU_DIM = 256
MOE_BLOCK_ROWS = (128, 160, 192, 224, 256, 288, 320)
TIME_CHUNK_ROWS = 128
HGRN_ROWS = 256
S5_MAX_SUB_ROWS = 384
HGRN_EXP_CLAMP = 80.0
PROMPT_BLOCK_ROWS = 768
SCAN_CARRY_ELEMS = 16 * SUBLANES * LANES


def _params(*sem):
    return pltpu.CompilerParams(dimension_semantics=sem, vmem_limit_bytes=V7X_VMEM_LIMIT_BYTES)


def _rms(x, g):
    ms = jnp.mean(x * x, axis=-1, keepdims=True)
    return x * lax.rsqrt(ms + EPS) * g


def _dot(a, b):
    return jnp.dot(a, b, preferred_element_type=F32)


def _full(shape):
    return pl.BlockSpec(shape, lambda *_: (0,) * len(shape))


def _norm_matmul_kernel(x_ref, g_ref, w_ref, o_ref, xn_ref):
    @pl.when(pl.program_id(1) == 0)
    def _():
        xn_ref[...] = _rms(x_ref[...], g_ref[...]).astype(BF16)

    o_ref[...] = _dot(xn_ref[...], w_ref[...]).astype(o_ref.dtype)


def norm_matmul(x, g, w, li, tm, tn):
    n, d = x.shape
    nout = w.shape[2]
    return pl.pallas_call(
        _norm_matmul_kernel,
        out_shape=jax.ShapeDtypeStruct((n, nout), BF16),
        grid=(n // tm, nout // tn),
        in_specs=[
            pl.BlockSpec((tm, d), lambda i, j: (i, 0)),
            _full((1, d)),
            pl.BlockSpec((None, d, tn), lambda i, j: (li, 0, j)),
        ],
        out_specs=pl.BlockSpec((tm, tn), lambda i, j: (i, j)),
        scratch_shapes=[pltpu.VMEM((tm, d), BF16)],
        compiler_params=_params("parallel", "arbitrary"),
        name="norm_matmul",
    )(x, g.reshape(1, d), w)


def _mix_ffn_kernel(ya_ref, ob_ref, x_ref, wo_ref, g_ref, wg_ref, wu_ref, wdn_ref, o_ref, xn_ref):
    j = pl.program_id(1)
    d_a = ya_ref.shape[1]

    @pl.when(j == 0)
    def _():
        x1 = x_ref[...] + _dot(ya_ref[...], wo_ref[:d_a, :]) + _dot(ob_ref[...], wo_ref[d_a:, :])
        xn_ref[...] = _rms(x1, g_ref[...]).astype(BF16)
        o_ref[...] = x1

    xn = xn_ref[...]
    h = (jax.nn.silu(_dot(xn, wg_ref[...])) * _dot(xn, wu_ref[...])).astype(BF16)
    o_ref[...] += _dot(h, wdn_ref[...])


def mix_ffn(ya, ob, x, w_out, g, w_gu, w_down, li, tm, chunk):
    n, d = x.shape
    dff = w_down.shape[1]
    assert dff % chunk == 0 and chunk % LANES == 0
    nj = dff // chunk
    rows = lambda a: pl.BlockSpec((tm, a.shape[1]), lambda i, j: (i, 0))
    return pl.pallas_call(
        _mix_ffn_kernel,
        out_shape=jax.ShapeDtypeStruct((n, d), F32),
        grid=(n // tm, nj),
        in_specs=[
            rows(ya), rows(ob), rows(x),
            pl.BlockSpec((None,) + w_out.shape[1:], lambda i, j: (li, 0, 0)),
            _full((1, d)),
            pl.BlockSpec((None, d, chunk), lambda i, j: (li, 0, j)),
            pl.BlockSpec((None, d, chunk), lambda i, j: (li, 0, j + nj)),
            pl.BlockSpec((None, chunk, d), lambda i, j: (li, j, 0)),
        ],
        out_specs=pl.BlockSpec((tm, d), lambda i, j: (i, 0)),
        scratch_shapes=[pltpu.VMEM((tm, d), BF16)],
        compiler_params=_params("parallel", "arbitrary"),
        name="mix_ffn",
    )(ya, ob, x, w_out, g.reshape(1, d), w_gu, w_gu, w_down)


def _glu_residual_kernel(z_ref, wv_ref, wg_ref, r_ref, o_ref):
    z = z_ref[...]
    val = _dot(z, wv_ref[...])
    gate = _dot(z, wg_ref[...])
    o_ref[...] = r_ref[...] + val * jax.nn.sigmoid(gate)


def glu_residual(z, w_glu, li, res, tm, tn):
    n, d = z.shape
    dout = w_glu.shape[2] // 2
    nj = dout // tn
    return pl.pallas_call(
        _glu_residual_kernel,
        out_shape=jax.ShapeDtypeStruct((n, dout), F32),
        grid=(n // tm, nj),
        in_specs=[
            pl.BlockSpec((tm, d), lambda i, j: (i, 0)),
            pl.BlockSpec((None, d, tn), lambda i, j: (li, 0, j)),
            pl.BlockSpec((None, d, tn), lambda i, j: (li, 0, j + nj)),
            pl.BlockSpec((tm, tn), lambda i, j: (i, j)),
        ],
        out_specs=pl.BlockSpec((tm, tn), lambda i, j: (i, j)),
        compiler_params=_params("parallel", "arbitrary"),
        name="glu_residual",
    )(z, w_glu, w_glu, res)


def _rmsnorm_kernel(x_ref, g_ref, o_ref):
    o_ref[...] = _rms(x_ref[...], g_ref[...])


def rmsnorm_rows(x, g, tm):
    n, d = x.shape
    return pl.pallas_call(
        _rmsnorm_kernel,
        out_shape=jax.ShapeDtypeStruct((n, d), F32),
        grid=(n // tm,),
        in_specs=[pl.BlockSpec((tm, d), lambda i: (i, 0)), _full((1, d))],
        out_specs=pl.BlockSpec((tm, d), lambda i: (i, 0)),
        compiler_params=_params("parallel"),
        name="final_rmsnorm",
    )(x, g.reshape(1, d))


def _rmsnorm_batch_major_kernel(*refs, nk, tc, nb):
    x_refs, g_ref, o_ref = refs[:nk], refs[nk], refs[nk + 1]
    d = o_ref.shape[-1]
    for k in range(nk):
        y = _rms(x_refs[k][...], g_ref[...])
        o_ref[:, k * tc:(k + 1) * tc, :] = jnp.swapaxes(y.reshape(tc, nb, d), 0, 1)


def rmsnorm_batch_major(x, g, nb, t_skip, t_out, tc, nk):
    n, d = x.shape
    rows = tc * nb
    assert t_skip % tc == 0 and t_out % (nk * tc) == 0
    specs = [pl.BlockSpec((rows, d), lambda j, k=k: (nk * j + t_skip // tc + k, 0)) for k in range(nk)]
    return pl.pallas_call(
        functools.partial(_rmsnorm_batch_major_kernel, nk=nk, tc=tc, nb=nb),
        out_shape=jax.ShapeDtypeStruct((nb, t_out, d), F32),
        grid=(t_out // (nk * tc),),
        in_specs=specs + [_full((1, d))],
        out_specs=pl.BlockSpec((nb, nk * tc, d), lambda j: (0, j, 0)),
        compiler_params=_params("parallel"),
        name="final_rmsnorm_batch_major",
    )(*([x] * nk), g.reshape(1, d))


def _to_time_major_kernel(*refs, nk, n_chunks):
    x_refs, lead_ref, o_ref = refs[:nk], refs[nk], refs[nk + 1]
    i = pl.program_id(0)
    nb, tc, d = x_refs[0].shape
    rows = tc * nb
    for k in range(nk):
        chunk = i * nk + k - 1
        val = jnp.swapaxes(x_refs[k][...], 0, 1).reshape(rows, d)
        if k == 0:
            lead = jnp.broadcast_to(lead_ref[...][:, None, :], (tc, nb, d)).reshape(rows, d)
            val = jnp.where(i == 0, lead, val)
        o_ref[k * rows:(k + 1) * rows, :] = jnp.where(chunk < n_chunks, val, 0.0)


def to_time_major(x, lead, t_pad, nk):
    nb, t, d = x.shape
    tc = lead.shape[0]
    assert t % tc == 0 and t_pad % (nk * tc) == 0
    n_chunks = t // tc
    specs = [pl.BlockSpec((nb, tc, d), lambda i, k=k: (0, jnp.clip(i * nk + k - 1, 0, n_chunks - 1), 0))
             for k in range(nk)]
    return pl.pallas_call(
        functools.partial(_to_time_major_kernel, nk=nk, n_chunks=n_chunks),
        out_shape=jax.ShapeDtypeStruct((t_pad * nb, d), F32),
        grid=(t_pad // (nk * tc),),
        in_specs=specs + [_full((tc, d))],
        out_specs=pl.BlockSpec((nk * tc * nb, d), lambda i: (i, 0)),
        compiler_params=_params("parallel"),
        name="to_time_major",
    )(*([x] * nk), lead)


def _rglru_kernel(xa_ref, ga_ref, conv0_ref, h0_ref, cw_ref, cb_ref, wg_ref, bg_ref, lam_ref,
                  ya_ref, hlast_ref, convnew_ref, xpad_ref, a_ref, u_ref, *, n_t, tb, rb):
    i = pl.program_id(0)
    rows = tb * rb
    tail = (CONV_W - 1) * rb
    c = xa_ref.shape[-1]
    t_valid = jnp.minimum(n_t - i * tb, tb)

    @pl.when(i == 0)
    def _():
        xpad_ref[0:tail, :] = conv0_ref[...]
        hlast_ref[...] = h0_ref[...]

    @pl.when(i > 0)
    def _():
        xpad_ref[0:tail, :] = xpad_ref[rows:rows + tail, :]

    xpad_ref[tail:tail + rows, :] = xa_ref[...].astype(F32)
    xc = cb_ref[...]
    for k in range(CONV_W):
        xc = xc + cw_ref[k:k + 1, :] * xpad_ref[k * rb:k * rb + rows, :]

    gates = _dot(xc.astype(BF16), wg_ref[...]) + bg_ref[...]
    r = jax.nn.sigmoid(gates[:, :c])
    ig = jax.nn.sigmoid(gates[:, c:])
    log_a = (-RG_C) * r * jax.nn.softplus(-lam_ref[...])
    a = jnp.exp(log_a)
    y = 1.0 - a * a
    mult = y * lax.rsqrt(jnp.maximum(y, 1e-30))
    a_ref[...] = a
    u_ref[...] = mult * ig * xc

    lc = min(c, max(LANES, SCAN_CARRY_ELEMS // rb // LANES * LANES))
    for c0 in range(0, c, lc):
        def body(t, h, c0=c0):
            sl = pl.ds(pl.multiple_of(t * rb, rb), rb)
            h = a_ref[sl, c0:c0 + lc] * h + u_ref[sl, c0:c0 + lc]
            u_ref[sl, c0:c0 + lc] = h
            return h

        hlast_ref[:, c0:c0 + lc] = lax.fori_loop(0, t_valid, body, hlast_ref[:, c0:c0 + lc])

    live = lax.broadcasted_iota(jnp.int32, (rows, c), 0) < t_valid * rb
    ya_ref[...] = jnp.where(live, u_ref[...] * jax.nn.gelu(ga_ref[...].astype(F32)), 0.0).astype(ya_ref.dtype)
    convnew_ref[...] = xpad_ref[pl.ds(pl.multiple_of(t_valid * rb, rb), tail), :]


def rglru(proj, n_t, conv0, h0, cw, cb, wg, bg, lam, tb, rb):
    c = h0.shape[1]
    n_rows = proj.shape[0]
    rows = tb * rb
    tail = (CONV_W - 1) * rb
    return pl.pallas_call(
        functools.partial(_rglru_kernel, n_t=n_t, tb=tb, rb=rb),
        out_shape=(
            jax.ShapeDtypeStruct((n_rows, c), BF16),
            jax.ShapeDtypeStruct((rb, c), F32),
            jax.ShapeDtypeStruct((tail, c), F32),
        ),
        grid=(n_rows // rows,),
        in_specs=[
            pl.BlockSpec((rows, c), lambda i: (i, 0)),
            pl.BlockSpec((rows, c), lambda i: (i, 1)),
            _full((tail, c)), _full((rb, c)), _full((CONV_W, c)), _full((1, c)),
            _full((c, 2 * c)), _full((1, 2 * c)), _full((1, c)),
        ],
        out_specs=(
            pl.BlockSpec((rows, c), lambda i: (i, 0)),
            _full((rb, c)),
            _full((tail, c)),
        ),
        scratch_shapes=[
            pltpu.VMEM((rows + tail, c), F32),
            pltpu.VMEM((rows, c), F32),
            pltpu.VMEM((rows, c), F32),
        ],
        compiler_params=_params("arbitrary"),
        name="rglru",
    )(proj, proj, conv0, h0, cw, cb.reshape(1, c), wg, bg.reshape(1, 2 * c), lam.reshape(1, c))


def _hgrn2_kernel(q_ref, f_ref, v_ref, gb_ref, s0_ref, lbraw_ref, gn_ref, stack_ref, ob_ref, snew_ref, st_ref,
                  *, layer, n_t, tb, nb, nh, m):
    del stack_ref
    i = pl.program_id(1)
    tc = m // nb
    half = tc // 2
    d = q_ref.shape[-1]
    dk = d // nh
    t_valid = jnp.minimum(n_t - i * tb, tb)
    n_chunks = (t_valid + tc - 1) // tc

    @pl.when(i == 0)
    def _():
        for b in range(nb):
            for h in range(nh):
                st_ref[h, :, b * dk:(b + 1) * dk] = s0_ref[b, h].T

    @pl.when(n_chunks < tb // tc)
    def _():
        ob_ref[...] = jnp.zeros_like(ob_ref)

    p = jax.nn.softmax(lbraw_ref[...], axis=0)
    cum = p[0:1, :]
    for r in range(1, layer + 1):
        cum = cum + p[r:r + 1, :]
    lb = cum - p[0:1, :]
    log_lb = jnp.log(lb)
    log_1mlb = jnp.log1p(-lb)

    assert tc % 2 == 0 and nb & (nb - 1) == 0
    row = lax.broadcasted_iota(jnp.int32, (m, m), 0)
    col = lax.broadcasted_iota(jnp.int32, (m, m), 1)
    same_seq_causal = jnp.where(((row & (nb - 1)) == (col & (nb - 1))) & (col <= row), 1.0, 0.0)
    bid = lax.broadcasted_iota(jnp.int32, (m, dk), 0) & (nb - 1)

    def chunk(c, carry):
        def rows_of(ref):
            if len(ref.shape) == 2:
                return ref[pl.ds(pl.multiple_of(c * m, m), m), :]
            return ref[pl.ds(c * tc, tc)].reshape(m, d)

        live = lax.broadcasted_iota(jnp.int32, (m, d), 0) < (t_valid - c * tc) * nb
        q = jax.nn.silu(rows_of(q_ref).astype(F32))
        fr = rows_of(f_ref).astype(F32)
        v = rows_of(v_ref).astype(BF16)
        gb = rows_of(gb_ref).astype(F32)
        logf = jnp.where(live, jnp.logaddexp(log_lb, log_1mlb + jax.nn.log_sigmoid(fr)), 0.0)
        k = 1.0 - jnp.exp(logf)
        slabs = [logf[0:nb]]
        for t in range(1, tc):
            slabs.append(slabs[-1] + logf[t * nb:(t + 1) * nb])
        g = jnp.concatenate(slabs, axis=0)
        g_last = jnp.concatenate([slabs[-1]] * tc, axis=0)
        g_rel = g - jnp.concatenate([slabs[half - 1]] * tc, axis=0)
        qa = (q * jnp.exp(jnp.minimum(g_rel, HGRN_EXP_CLAMP))).astype(BF16)
        kt = (k * jnp.exp(jnp.minimum(-g_rel, HGRN_EXP_CLAMP))).astype(BF16)
        qt = (q * jnp.exp(g)).astype(BF16)
        ks = (k * jnp.exp(g_last - g)).astype(BF16)
        dec = jnp.exp(slabs[-1])

        outs = []
        for h in range(nh):
            hs = slice(h * dk, (h + 1) * dk)
            qt_h, kt_h, ks_h, v_h = qt[:, hs], kt[:, hs], ks[:, hs], v[:, hs]
            att = lax.dot_general(qa[:, hs], kt_h, (((1,), (1,)), ((), ())), preferred_element_type=F32)
            att = (att * same_seq_causal).astype(BF16)
            o = _dot(att, v_h)
            zero = jnp.zeros_like(qt_h)
            expand = lambda x: jnp.concatenate([jnp.where(bid == b, x, zero) for b in range(nb)], axis=1)
            st_h = st_ref[h]
            o = o + lax.dot_general(expand(qt_h), st_h.astype(BF16), (((1,), (1,)), ((), ())),
                                    preferred_element_type=F32)
            dst = lax.dot_general(v_h, expand(ks_h), (((0,), (0,)), ((), ())), preferred_element_type=F32)
            dec_row = jnp.concatenate([dec[b:b + 1, hs] for b in range(nb)], axis=1)
            st_ref[h] = st_h * dec_row + dst
            outs.append(_rms(o, gn_ref[:, hs]))
        ob = jnp.where(live, jnp.concatenate(outs, axis=1) * jax.nn.silu(gb), 0.0)
        ob_ref[pl.ds(pl.multiple_of(c * m, m), m), :] = ob.astype(ob_ref.dtype)
        return carry

    lax.fori_loop(0, n_chunks, chunk, 0)

    @pl.when(i == pl.num_programs(1) - 1)
    def _():
        for b in range(nb):
            for h in range(nh):
                snew_ref[b, h] = st_ref[h, :, b * dk:(b + 1) * dk].T


def hgrn2(proj, s0, s0_layer, s_stack, n_layers, lb_raw, gnorm, layer, n_t, tb, nb):
    _, bsz, nh, dk, _ = s0.shape
    d = nh * dk
    off = proj.shape[-1] // d - 4
    nbb = bsz // nb
    n_tpad = proj.shape[0] // bsz
    nt = n_tpad // tb
    m = max(k for k in range(2 * nb, HGRN_ROWS + 1, 2 * nb) if (tb * nb) % k == 0)
    if nb == bsz:
        col = lambda k: pl.BlockSpec((tb * nb, d), lambda j, i, k=k: (i, k + off))
    else:
        proj = proj.reshape(n_tpad, bsz, proj.shape[1])
        col = lambda k: pl.BlockSpec((tb, nb, d), lambda j, i, k=k: (i, j, k + off))
    in_specs = [
        col(0), col(1), col(2), col(3),
        pl.BlockSpec((None, nb, nh, dk, dk), lambda j, i: (s0_layer, j, 0, 0, 0)),
        _full(lb_raw.shape),
        _full((1, d)),
        pl.BlockSpec(memory_space=pl.ANY),
    ]
    args = [proj, proj, proj, proj, s0, lb_raw, gnorm.reshape(1, d), s_stack]
    assert s_stack.shape == (n_layers, bsz, nh, dk, dk)
    return pl.pallas_call(
        functools.partial(_hgrn2_kernel, layer=layer, n_t=n_t, tb=tb, nb=nb, nh=nh, m=m),
        out_shape=(
            jax.ShapeDtypeStruct((n_tpad * bsz, d), BF16),
            jax.ShapeDtypeStruct((n_layers, bsz, nh, dk, dk), F32),
        ),
        grid=(nbb, nt),
        in_specs=in_specs,
        out_specs=(
            pl.BlockSpec((tb * nb, d), lambda j, i: (j * nt + i, 0)),
            pl.BlockSpec((None, nb, nh, dk, dk), lambda j, i: (layer, j, 0, 0, 0)),
        ),
        scratch_shapes=[pltpu.VMEM((nh, dk, nb * dk), F32)],
        input_output_aliases={len(args) - 1: 1},
        compiler_params=_params("arbitrary", "arbitrary"),
        name="hgrn2",
    )(*args)


def _s5_discretise(lr, li, ldt):
    dt = jnp.exp(ldt)
    mag = jnp.exp(lr * dt)
    return mag * jnp.cos(li * dt), mag * jnp.sin(li * dt)


def _s5_prep_kernel(lr_ref, li_ref, ldt_ref, lrc_ref, lic_ref, ldtc_ref, bre_ref, bim_ref,
                    ar_ref, ai_ref, ore_ref, oim_ref):
    ar_ref[...], ai_ref[...] = _s5_discretise(lr_ref[...], li_ref[...], ldt_ref[...])
    lr = lrc_ref[...]
    li = lic_ref[...]
    ar, ai = _s5_discretise(lr, li, ldtc_ref[...])
    den = lr * lr + li * li
    cr = ((ar - 1.0) * lr + ai * li) / den
    ci = (ai * lr - (ar - 1.0) * li) / den
    ore_ref[...] = cr * bre_ref[...] - ci * bim_ref[...]
    oim_ref[...] = cr * bim_ref[...] + ci * bre_ref[...]


def s5_prep(lam_re, lam_im, log_dt, b_re, b_im):
    g, p, c = b_re.shape
    dense = lambda a: a.reshape(-1, LANES)
    per_state = lambda a: dense(jnp.broadcast_to(a.reshape(g, -1, 1), (g, p, 1)))
    per_coef = lambda a: dense(jnp.broadcast_to(a.reshape(g, -1, 1), (g, p, c)))
    small = jax.ShapeDtypeStruct((g * p // LANES, LANES), F32)
    big = jax.ShapeDtypeStruct((g * p * c // LANES, LANES), F32)
    ar, ai, ore, oim = pl.pallas_call(
        _s5_prep_kernel, out_shape=(small, small, big, big), name="s5_prep",
    )(per_state(lam_re), per_state(lam_im), per_state(log_dt), per_coef(lam_re), per_coef(lam_im), per_coef(log_dt),
      dense(b_re), dense(b_im))
    return ar.reshape(g, p), ai.reshape(g, p), ore.reshape(g, p, c), oim.reshape(g, p, c)


def _s5_kernel(x_ref, g_ref, wb_ref, wc_ref, ar_ref, ai_ref, d_ref, h0r_ref, h0i_ref,
               z_ref, hr_ref, hi_ref, *, n_t, tb, rb, m):
    i = pl.program_id(0)
    nch = wb_ref.shape[0]
    cw = wb_ref.shape[1]
    sw = wb_ref.shape[2] // 2
    tsb = m // rb
    nsl = rb // SUBLANES
    n_sub = jnp.minimum(n_t - i * tb, tb) // tsb

    @pl.when(i == 0)
    def _():
        hr_ref[...] = h0r_ref[...]
        hi_ref[...] = h0i_ref[...]

    @pl.when(n_sub < tb // tsb)
    def _():
        z_ref[...] = jnp.zeros_like(z_ref)

    def sub_block(j, carry):
        r0 = pl.multiple_of(j * m, m)
        xn = _rms(x_ref[pl.ds(r0, m), :], g_ref[...])
        xnb = xn.astype(BF16)
        drive = [_dot(xnb[:, c * cw:(c + 1) * cw], wb_ref[c]) for c in range(nch)]
        for c in range(nch):
            cs = slice(c * cw, (c + 1) * cw)
            ss = slice(c * sw, (c + 1) * sw)
            u = xn[:, cs]
            bu = drive[c]
            ar = jnp.broadcast_to(ar_ref[:, ss], (SUBLANES, sw))
            ai = jnp.broadcast_to(ai_ref[:, ss], (SUBLANES, sw))
            out_r = [None] * (tsb * nsl)
            out_i = [None] * (tsb * nsl)
            for s in range(nsl):
                srow = slice(s * SUBLANES, (s + 1) * SUBLANES)
                hr, hi = hr_ref[srow, ss], hi_ref[srow, ss]
                for t in range(tsb):
                    lo = t * rb + s * SUBLANES
                    hr, hi = (ar * hr - ai * hi + bu[lo:lo + SUBLANES, :sw],
                              ar * hi + ai * hr + bu[lo:lo + SUBLANES, sw:])
                    out_r[t * nsl + s] = hr
                    out_i[t * nsl + s] = hi
                hr_ref[srow, ss] = hr
                hi_ref[srow, ss] = hi
            hcat = jnp.concatenate([jnp.concatenate(out_r, axis=0).astype(BF16),
                                    jnp.concatenate(out_i, axis=0).astype(BF16)], axis=1)
            y = _dot(hcat, wc_ref[c]) + d_ref[:, cs] * u
            z_ref[pl.ds(r0, m), cs] = jax.nn.gelu(y).astype(z_ref.dtype)
        return carry

    lax.fori_loop(0, n_sub, sub_block, 0)


def s5(x, n_t, g, wb, wc, li, a_re, a_im, dskip, h0r, h0i, tb, rb):
    n_rows, d = x.shape
    ns = h0r.shape[1]
    rows = tb * rb
    m = max(k for k in range(rb, S5_MAX_SUB_ROWS + 1, rb)
            if rows % k == 0 and n_t % (k // rb) == 0 and k % (2 * SUBLANES) == 0)
    return pl.pallas_call(
        functools.partial(_s5_kernel, n_t=n_t, tb=tb, rb=rb, m=m),
        out_shape=(
            jax.ShapeDtypeStruct((n_rows, d), BF16),
            jax.ShapeDtypeStruct((rb, ns), F32),
            jax.ShapeDtypeStruct((rb, ns), F32),
        ),
        grid=(n_rows // rows,),
        in_specs=[
            pl.BlockSpec((rows, d), lambda i: (i, 0)),
            _full((1, d)),
            pl.BlockSpec((None,) + wb.shape[1:], lambda i: (li, 0, 0, 0)),
            pl.BlockSpec((None,) + wc.shape[1:], lambda i: (li, 0, 0, 0)),
            _full((1, ns)), _full((1, ns)), _full((1, d)),
            _full((rb, ns)), _full((rb, ns)),
        ],
        out_specs=(
            pl.BlockSpec((rows, d), lambda i: (i, 0)),
            _full((rb, ns)), _full((rb, ns)),
        ),
        compiler_params=_params("arbitrary"),
        name="s5",
    )(x, g.reshape(1, d), wb, wc, a_re, a_im, dskip.reshape(1, d), h0r, h0i)


def _s5_block_diag_in(b):
    g, p, c = b.shape
    gpc = V7X_MXU_DIM // c
    bt = b.transpose(0, 2, 1).reshape(g // gpc, gpc, c, p)
    out = jnp.einsum('ngcp,gh->ngchp', bt, jnp.eye(gpc, dtype=b.dtype))
    return out.reshape(g // gpc, gpc * c, gpc * p)


def _s5_block_diag_out(cm):
    g, c, p = cm.shape
    gpc = V7X_MXU_DIM // c
    ct = cm.transpose(0, 2, 1).reshape(g // gpc, gpc, p, c)
    out = jnp.einsum('ngpc,gh->ngphc', ct, jnp.eye(gpc, dtype=cm.dtype))
    return out.reshape(g // gpc, gpc * p, gpc * c)


def _moe_kernel(x_ref, g_ref, wrt_ref, before_ref, wgu_ref, wdn_ref, o_ref, xn_ref, pos_ref, gate_ref, *, sub):
    e = pl.program_id(1)
    tm = x_ref.shape[0]
    ne = wrt_ref.shape[0]
    dff = wdn_ref.shape[1]

    @pl.when(e == 0)
    def _():
        x = x_ref[...]
        xn = _rms(x, g_ref[...])
        xn_hi = xn.astype(BF16)
        xn_ref[...] = xn_hi
        xn_lo = (xn - xn_hi.astype(F32)).astype(BF16)
        wr = wrt_ref[...]
        wr_hi = wr.astype(BF16)
        wr_lo = (wr - wr_hi.astype(F32)).astype(BF16)
        nt = lambda a, b: lax.dot_general(a, b, (((1,), (1,)), ((), ())), preferred_element_type=F32)
        logits = nt(wr_hi, xn_hi) + (nt(wr_hi, xn_lo) + nt(wr_lo, xn_hi))
        ex = jnp.exp(logits - jnp.max(logits, axis=0, keepdims=True))
        probs = ex / jnp.sum(ex, axis=0, keepdims=True)
        eid = lax.broadcasted_iota(jnp.int32, (ne, tm), 0).astype(F32)
        m1 = jnp.max(probs, axis=0, keepdims=True)
        i1 = jnp.min(jnp.where(probs == m1, eid, float(ne)), axis=0, keepdims=True)
        sel1 = eid == i1
        rest = jnp.where(sel1, -1.0, probs)
        m2 = jnp.max(rest, axis=0, keepdims=True)
        i2 = jnp.min(jnp.where(rest == m2, eid, float(ne)), axis=0, keepdims=True)
        sel2 = eid == i2
        den = m1 + m2
        gate_ref[...] = jnp.where(sel1, m1 / den, 0.0) + jnp.where(sel2, m2 / den, 0.0)
        chosen = jnp.where(sel1, 1.0, jnp.where(sel2, 1.0, 0.0))
        rank = _dot(chosen.astype(BF16), before_ref[...])
        pos_ref[...] = jnp.where(chosen > 0.0, rank, -1.0)
        o_ref[...] = x

    pos_e = pos_ref[pl.ds(e, 1), :]
    gate_e = gate_ref[pl.ds(e, 1), :]
    cnt = jnp.sum(jnp.where(pos_e >= 0.0, 1.0, 0.0)).astype(jnp.int32)

    def run_block(base, rows):
        slot = lax.broadcasted_iota(jnp.int32, (rows, tm), 0) + base
        hit = pos_e == slot.astype(F32)
        onehot = jnp.where(hit, 1.0, 0.0).astype(BF16)
        xs = _dot(onehot, xn_ref[...]).astype(BF16)
        gs = jnp.sum(jnp.where(hit, gate_e, 0.0), axis=1, keepdims=True)
        hgu = _dot(xs, wgu_ref[0])
        act = (jax.nn.silu(hgu[:, :dff]) * hgu[:, dff:]).astype(BF16)
        yb = (_dot(act, wdn_ref[0]) * gs).astype(BF16)
        o_ref[...] += lax.dot_general(onehot, yb, (((0,), (0,)), ((), ())), preferred_element_type=F32)

    def body(s, carry):
        run_block(s * sub, sub)
        return carry

    n_full = cnt // sub
    lax.fori_loop(0, n_full, body, 0)
    rem = cnt - n_full * sub
    lo = 0
    for rows in MOE_BLOCK_ROWS:
        pl.when((rem > lo) & (rem <= rows))(functools.partial(run_block, n_full * sub, rows))
        lo = rows


def moe_residual(x, g, w_router_t, w_gu, w_down, li, tm, sub):
    n, d = x.shape
    _, ne, _, dff2 = w_gu.shape
    assert MOE_BLOCK_ROWS[-1] == sub
    before = jnp.triu(jnp.ones((tm, tm), BF16), k=1)
    return pl.pallas_call(
        functools.partial(_moe_kernel, sub=sub),
        out_shape=jax.ShapeDtypeStruct((n, d), F32),
        grid=(n // tm, ne),
        in_specs=[
            pl.BlockSpec((tm, d), lambda i, e: (i, 0)),
            _full((1, d)),
            _full((ne, d)),
            _full((tm, tm)),
            pl.BlockSpec((None, 1, d, dff2), lambda i, e: (li, e, 0, 0)),
            pl.BlockSpec((None, 1, dff2 // 2, d), lambda i, e: (li, e, 0, 0)),
        ],
        out_specs=pl.BlockSpec((tm, d), lambda i, e: (i, 0)),
        scratch_shapes=[
            pltpu.VMEM((tm, d), BF16),
            pltpu.VMEM((ne, tm), F32),
            pltpu.VMEM((ne, tm), F32),
        ],
        compiler_params=_params("parallel", "arbitrary"),
        name="moe",
    )(x, g.reshape(1, d), w_router_t, before, w_gu, w_down)


def _trunk(x, n_t, rb, tb, nb, h0, conv0, s0, re0, im0, w):
    n, d = x.shape
    depth = w['norm_mix'].shape[0]
    d_a = h0.shape[-1]
    nh, dk = s0.shape[2], s0.shape[3]
    d_b = nh * dk
    ns = re0.shape[-2] * re0.shape[-1]
    tm = tb * rb
    n_even = (depth + 1) // 2
    new = {k: [] for k in ('h', 'conv', 're', 'im')}
    s_stack = jnp.zeros((n_even,) + s0.shape[1:], F32)
    for l in range(depth):
        li = l // 2
        if l % 2 == 0:
            proj = norm_matmul(x, w['norm_mix'][l], w['even_w_in'], li, tm, w['even_w_in'].shape[2])
            conv_tm = conv0[li].transpose(1, 0, 2).reshape((CONV_W - 1) * rb, d_a)
            ya, h_new, conv_new = rglru(proj, n_t, conv_tm, h0[li], w['rglru_conv_w'][li], w['rglru_conv_b'][li],
                                        w['rglru_wg'][li], w['rglru_bg'][li], w['rglru_lambda'][li], tb=tb, rb=rb)
            new['h'].append(h_new)
            new['conv'].append(conv_new.reshape(CONV_W - 1, rb, d_a).transpose(1, 0, 2))
            ob, s_stack = hgrn2(proj, s0, li % s0.shape[0], s_stack, n_even,
                                w['hgrn2_lb_raw'], w['hgrn2_gnorm'][li], li, n_t=n_t, tb=tb, nb=nb)
            if nb != rb:
                ob = ob.reshape(rb // nb, n // rb, nb, d_b).transpose(1, 0, 2, 3).reshape(n, d_b)
            x = mix_ffn(ya, ob, x, w['even_w_out'], w['norm_ffn'][l], w['ffn_w_gu'], w['ffn_w_down'], li, tm,
                        w['ffn_w_down'].shape[1] // 2)
        else:
            z, re_new, im_new = s5(x, n_t, w['norm_mix'][l], w['s5_wb'], w['s5_wc'], li, w['s5_a_re'][li],
                                   w['s5_a_im'][li], w['s5_d'][li], re0[li].reshape(rb, ns), im0[li].reshape(rb, ns),
                                   tb=tb, rb=rb)
            new['re'].append(re_new.reshape(re0.shape[1:]))
            new['im'].append(im_new.reshape(im0.shape[1:]))
            x = glu_residual(z, w['s5_w_glu'], li, x, tm, w['s5_w_glu'].shape[2] // 2)
            x = moe_residual(x, w['norm_ffn'][l], w['moe_w_router_t'][li], w['moe_w_gu'], w['moe_w_down'], li,
                             tm, MOE_BLOCK_ROWS[-1])
    stack = lambda k: jnp.stack(new[k])
    return x, (stack('h'), stack('conv'), s_stack, stack('re'), stack('im'))


def kernel(x_prompt, x_sample, state_rglru_h, state_rglru_conv, state_hgrn2, state_s5_re, state_s5_im,
           meta_tokens, norm_mix, norm_ffn, norm_final, even_w_in, even_w_out,
           rglru_conv_w, rglru_conv_b, rglru_w_a, rglru_b_a, rglru_w_x, rglru_b_x, rglru_lambda,
           hgrn2_lb_raw, hgrn2_gnorm, s5_lam_re, s5_lam_im, s5_log_dt, s5_b_re, s5_b_im, s5_c_re, s5_c_im,
           s5_d, s5_w_glu, ffn_w_gu, ffn_w_down, moe_w_router, moe_w_gu, moe_w_down):
    bp, tp0, d = x_prompt.shape
    bs, ts, _ = x_sample.shape
    tp = tp0 + N_META
    d_a = state_rglru_h.shape[-1]
    n_even, n_odd = state_rglru_h.shape[0], state_s5_re.shape[0]
    assert bp == SUBLANES and bs % SUBLANES == 0
    tb_p = PROMPT_BLOCK_ROWS // bp
    tp_pad = -(-tp // tb_p) * tb_p
    nb_s = TIME_CHUNK_ROWS // ts
    assert (tp % (TIME_CHUNK_ROWS // bp) == 0 and tb_p % (TIME_CHUNK_ROWS // bp) == 0 and bs % nb_s == 0
            and nb_s * ts == TIME_CHUNK_ROWS and nb_s % SUBLANES == 0)

    eye_a = jnp.eye(H_A, dtype=F32)
    block_diag = lambda m: jnp.einsum('lhij,hg->lhigj', m, eye_a).reshape(n_even, d_a, d_a)
    s5_g, s5_p, s5_c = s5_b_re.shape[1:]
    lg = n_odd * s5_g
    a_re, a_im, bt_re, bt_im = s5_prep(s5_lam_re.reshape(lg, s5_p), s5_lam_im.reshape(lg, s5_p), s5_log_dt.reshape(lg),
                                       s5_b_re.reshape(lg, s5_p, s5_c), s5_b_im.reshape(lg, s5_p, s5_c))
    s5_wb = jnp.concatenate([_s5_block_diag_in(bt_re.reshape(lg, s5_p, s5_c)),
                             _s5_block_diag_in(bt_im.reshape(lg, s5_p, s5_c))], axis=2).astype(BF16)
    s5_wc = jnp.concatenate([_s5_block_diag_out(s5_c_re.reshape(lg, s5_c, s5_p)),
                             -_s5_block_diag_out(s5_c_im.reshape(lg, s5_c, s5_p))], axis=1).astype(BF16)
    ns = s5_lam_re.shape[1] * s5_lam_re.shape[2]
    w = {
        'norm_mix': norm_mix, 'norm_ffn': norm_ffn, 'norm_final': norm_final,
        'even_w_in': even_w_in.astype(BF16), 'even_w_out': even_w_out.astype(BF16),
        'rglru_conv_w': rglru_conv_w, 'rglru_conv_b': rglru_conv_b,
        'rglru_wg': jnp.concatenate([block_diag(rglru_w_a), block_diag(rglru_w_x)], axis=2).astype(BF16),
        'rglru_bg': jnp.concatenate([rglru_b_a, rglru_b_x], axis=1),
        'rglru_lambda': rglru_lambda, 'hgrn2_lb_raw': hgrn2_lb_raw, 'hgrn2_gnorm': hgrn2_gnorm,
        's5_a_re': a_re.reshape(n_odd, 1, ns), 's5_a_im': a_im.reshape(n_odd, 1, ns),
        's5_wb': s5_wb.reshape((n_odd, -1) + s5_wb.shape[1:]), 's5_wc': s5_wc.reshape((n_odd, -1) + s5_wc.shape[1:]),
        's5_d': s5_d, 's5_w_glu': s5_w_glu.astype(BF16),
        'ffn_w_gu': ffn_w_gu.astype(BF16), 'ffn_w_down': ffn_w_down.astype(BF16),
        'moe_w_router_t': moe_w_router.transpose(0, 2, 1),
        'moe_w_gu': moe_w_gu.astype(BF16), 'moe_w_down': moe_w_down.astype(BF16),
    }

    tc_p = TIME_CHUNK_ROWS // bp
    assert N_META == tc_p
    xm = to_time_major(x_prompt, meta_tokens.astype(x_prompt.dtype), tp_pad, tb_p // tc_p)
    zero = lambda ref, lead: jnp.zeros((lead, bp) + ref.shape[2:], ref.dtype)
    xp, p_new = _trunk(xm, tp, bp, tb_p, bp, zero(state_rglru_h, n_even), zero(state_rglru_conv, n_even),
                       zero(state_hgrn2, 1), zero(state_s5_re, n_odd), zero(state_s5_im, n_odd), w)
    nk_out = max(k for k in range(1, TIME_CHUNK_ROWS // tc_p + 1) if tp0 % (k * tc_p) == 0)
    y_prompt = rmsnorm_batch_major(xp, norm_final, bp, N_META, tp0, tc_p, nk_out)

    xs = x_sample.transpose(1, 0, 2).reshape(ts * bs, d)
    xs, s_new = _trunk(xs, ts, bs, ts, nb_s, state_rglru_h, state_rglru_conv, state_hgrn2, state_s5_re, state_s5_im, w)
    y_sample = rmsnorm_rows(xs, norm_final, ts * bs).reshape(ts, bs, d).transpose(1, 0, 2)

    refs = (state_rglru_h, state_rglru_conv, state_hgrn2, state_s5_re, state_s5_im)
    cast = lambda new: tuple(a.astype(r.dtype) for a, r in zip(new, refs))
    return (y_prompt, y_sample) + cast(p_new) + cast(s_new)
```

```python
import functools

import jax
import jax.numpy as jnp
from jax import lax
from jax.experimental import pallas as pl
from jax.experimental.pallas import tpu as pltpu

F32 = jnp.float32
BF16 = jnp.bfloat16

EPS = 1e-6
N_META = 16
CONV_W = 4
RG_C = 8.0
H_A = 8

V7X_VMEM_LIMIT_BYTES = 56 * 1024 * 1024
SUBLANES = 8
LANES = 128
V7X_MXU_DIM = 256
MOE_BLOCK_ROWS = (128, 160, 192, 224, 256, 288, 320)
TIME_CHUNK_ROWS = 128
HGRN_ROWS = 256
S5_MAX_SUB_ROWS = 384
HGRN_EXP_CLAMP = 80.0
PROMPT_BLOCK_ROWS = 768
SCAN_CARRY_ELEMS = 16 * SUBLANES * LANES


def _params(*sem):
    return pltpu.CompilerParams(dimension_semantics=sem, vmem_limit_bytes=V7X_VMEM_LIMIT_BYTES)


def _rms(x, g):
    ms = jnp.mean(x * x, axis=-1, keepdims=True)
    return x * lax.rsqrt(ms + EPS) * g


def _dot(a, b):
    return jnp.dot(a, b, preferred_element_type=F32)


def _full(shape):
    return pl.BlockSpec(shape, lambda *_: (0,) * len(shape))


def _norm_matmul_kernel(x_ref, g_ref, w_ref, o_ref, xn_ref):
    @pl.when(pl.program_id(1) == 0)
    def _():
        xn_ref[...] = _rms(x_ref[...], g_ref[...]).astype(BF16)

    o_ref[...] = _dot(xn_ref[...], w_ref[...]).astype(o_ref.dtype)


def norm_matmul(x, g, w, li, tm, tn):
    n, d = x.shape
    nout = w.shape[2]
    return pl.pallas_call(
        _norm_matmul_kernel,
        out_shape=jax.ShapeDtypeStruct((n, nout), BF16),
        grid=(n // tm, nout // tn),
        in_specs=[
            pl.BlockSpec((tm, d), lambda i, j: (i, 0)),
            _full((1, d)),
            pl.BlockSpec((None, d, tn), lambda i, j: (li, 0, j)),
        ],
        out_specs=pl.BlockSpec((tm, tn), lambda i, j: (i, j)),
        scratch_shapes=[pltpu.VMEM((tm, d), BF16)],
        compiler_params=_params("parallel", "arbitrary"),
        name="norm_matmul",
    )(x, g.reshape(1, d), w)


def _mix_ffn_kernel(ya_ref, ob_ref, x_ref, wo_ref, g_ref, wg_ref, wu_ref, wdn_ref, o_ref, xn_ref):
    j = pl.program_id(1)
    d_a = ya_ref.shape[1]

    @pl.when(j == 0)
    def _():
        x1 = x_ref[...] + _dot(ya_ref[...], wo_ref[:d_a, :]) + _dot(ob_ref[...], wo_ref[d_a:, :])
        xn_ref[...] = _rms(x1, g_ref[...]).astype(BF16)
        o_ref[...] = x1

    xn = xn_ref[...]
    h = (jax.nn.silu(_dot(xn, wg_ref[...])) * _dot(xn, wu_ref[...])).astype(BF16)
    o_ref[...] += _dot(h, wdn_ref[...])


def mix_ffn(ya, ob, x, w_out, g, w_gu, w_down, li, tm, chunk):
    n, d = x.shape
    dff = w_down.shape[1]
    assert dff % chunk == 0 and chunk % LANES == 0
    nj = dff // chunk
    rows = lambda a: pl.BlockSpec((tm, a.shape[1]), lambda i, j: (i, 0))
    return pl.pallas_call(
        _mix_ffn_kernel,
        out_shape=jax.ShapeDtypeStruct((n, d), F32),
        grid=(n // tm, nj),
        in_specs=[
            rows(ya), rows(ob), rows(x),
            pl.BlockSpec((None,) + w_out.shape[1:], lambda i, j: (li, 0, 0)),
            _full((1, d)),
            pl.BlockSpec((None, d, chunk), lambda i, j: (li, 0, j)),
            pl.BlockSpec((None, d, chunk), lambda i, j: (li, 0, j + nj)),
            pl.BlockSpec((None, chunk, d), lambda i, j: (li, j, 0)),
        ],
        out_specs=pl.BlockSpec((tm, d), lambda i, j: (i, 0)),
        scratch_shapes=[pltpu.VMEM((tm, d), BF16)],
        compiler_params=_params("parallel", "arbitrary"),
        name="mix_ffn",
    )(ya, ob, x, w_out, g.reshape(1, d), w_gu, w_gu, w_down)


def _glu_residual_kernel(z_ref, wv_ref, wg_ref, r_ref, o_ref):
    z = z_ref[...]
    val = _dot(z, wv_ref[...])
    gate = _dot(z, wg_ref[...])
    o_ref[...] = r_ref[...] + val * jax.nn.sigmoid(gate)


def glu_residual(z, w_glu, li, res, tm, tn):
    n, d = z.shape
    dout = w_glu.shape[2] // 2
    nj = dout // tn
    return pl.pallas_call(
        _glu_residual_kernel,
        out_shape=jax.ShapeDtypeStruct((n, dout), F32),
        grid=(n // tm, nj),
        in_specs=[
            pl.BlockSpec((tm, d), lambda i, j: (i, 0)),
            pl.BlockSpec((None, d, tn), lambda i, j: (li, 0, j)),
            pl.BlockSpec((None, d, tn), lambda i, j: (li, 0, j + nj)),
            pl.BlockSpec((tm, tn), lambda i, j: (i, j)),
        ],
        out_specs=pl.BlockSpec((tm, tn), lambda i, j: (i, j)),
        compiler_params=_params("parallel", "arbitrary"),
        name="glu_residual",
    )(z, w_glu, w_glu, res)


def _rmsnorm_kernel(x_ref, g_ref, o_ref):
    o_ref[...] = _rms(x_ref[...], g_ref[...])


def rmsnorm_rows(x, g, tm):
    n, d = x.shape
    return pl.pallas_call(
        _rmsnorm_kernel,
        out_shape=jax.ShapeDtypeStruct((n, d), F32),
        grid=(n // tm,),
        in_specs=[pl.BlockSpec((tm, d), lambda i: (i, 0)), _full((1, d))],
        out_specs=pl.BlockSpec((tm, d), lambda i: (i, 0)),
        compiler_params=_params("parallel"),
        name="final_rmsnorm",
    )(x, g.reshape(1, d))


def _rmsnorm_batch_major_kernel(*refs, nk, tc, nb):
    x_refs, g_ref, o_ref = refs[:nk], refs[nk], refs[nk + 1]
    d = o_ref.shape[-1]
    for k in range(nk):
        y = _rms(x_refs[k][...], g_ref[...])
        o_ref[:, k * tc:(k + 1) * tc, :] = jnp.swapaxes(y.reshape(tc, nb, d), 0, 1)


def rmsnorm_batch_major(x, g, nb, t_skip, t_out, tc, nk):
    n, d = x.shape
    rows = tc * nb
    assert t_skip % tc == 0 and t_out % (nk * tc) == 0
    specs = [pl.BlockSpec((rows, d), lambda j, k=k: (nk * j + t_skip // tc + k, 0)) for k in range(nk)]
    return pl.pallas_call(
        functools.partial(_rmsnorm_batch_major_kernel, nk=nk, tc=tc, nb=nb),
        out_shape=jax.ShapeDtypeStruct((nb, t_out, d), F32),
        grid=(t_out // (nk * tc),),
        in_specs=specs + [_full((1, d))],
        out_specs=pl.BlockSpec((nb, nk * tc, d), lambda j: (0, j, 0)),
        compiler_params=_params("parallel"),
        name="final_rmsnorm_batch_major",
    )(*([x] * nk), g.reshape(1, d))


def _to_time_major_kernel(*refs, nk, n_chunks):
    x_refs, lead_ref, o_ref = refs[:nk], refs[nk], refs[nk + 1]
    i = pl.program_id(0)
    nb, tc, d = x_refs[0].shape
    rows = tc * nb
    for k in range(nk):
        chunk = i * nk + k - 1
        val = jnp.swapaxes(x_refs[k][...], 0, 1).reshape(rows, d)
        if k == 0:
            lead = jnp.broadcast_to(lead_ref[...][:, None, :], (tc, nb, d)).reshape(rows, d)
            val = jnp.where(i == 0, lead, val)
        o_ref[k * rows:(k + 1) * rows, :] = jnp.where(chunk < n_chunks, val, 0.0)


def to_time_major(x, lead, t_pad, nk):
    nb, t, d = x.shape
    tc = lead.shape[0]
    assert t % tc == 0 and t_pad % (nk * tc) == 0
    n_chunks = t // tc
    specs = [pl.BlockSpec((nb, tc, d), lambda i, k=k: (0, jnp.clip(i * nk + k - 1, 0, n_chunks - 1), 0))
             for k in range(nk)]
    return pl.pallas_call(
        functools.partial(_to_time_major_kernel, nk=nk, n_chunks=n_chunks),
        out_shape=jax.ShapeDtypeStruct((t_pad * nb, d), F32),
        grid=(t_pad // (nk * tc),),
        in_specs=specs + [_full((tc, d))],
        out_specs=pl.BlockSpec((nk * tc * nb, d), lambda i: (i, 0)),
        compiler_params=_params("parallel"),
        name="to_time_major",
    )(*([x] * nk), lead)


def _rglru_kernel(xa_ref, ga_ref, conv0_ref, h0_ref, cw_ref, cb_ref, wg_ref, bg_ref, lam_ref,
                  ya_ref, hlast_ref, convnew_ref, xpad_ref, a_ref, u_ref, *, n_t, tb, rb):
    i = pl.program_id(0)
    rows = tb * rb
    tail = (CONV_W - 1) * rb
    c = xa_ref.shape[-1]
    t_valid = jnp.minimum(n_t - i * tb, tb)

    @pl.when(i == 0)
    def _():
        xpad_ref[0:tail, :] = conv0_ref[...]
        hlast_ref[...] = h0_ref[...]

    @pl.when(i > 0)
    def _():
        xpad_ref[0:tail, :] = xpad_ref[rows:rows + tail, :]

    xpad_ref[tail:tail + rows, :] = xa_ref[...].astype(F32)
    xc = cb_ref[...]
    for k in range(CONV_W):
        xc = xc + cw_ref[k:k + 1, :] * xpad_ref[k * rb:k * rb + rows, :]

    gates = _dot(xc.astype(BF16), wg_ref[...]) + bg_ref[...]
    r = jax.nn.sigmoid(gates[:, :c])
    ig = jax.nn.sigmoid(gates[:, c:])
    log_a = (-RG_C) * r * jax.nn.softplus(-lam_ref[...])
    a = jnp.exp(log_a)
    y = 1.0 - a * a
    mult = y * lax.rsqrt(jnp.maximum(y, 1e-30))
    a_ref[...] = a
    u_ref[...] = mult * ig * xc

    lc = min(c, max(LANES, SCAN_CARRY_ELEMS // rb // LANES * LANES))
    for c0 in range(0, c, lc):
        def body(t, h, c0=c0):
            sl = pl.ds(pl.multiple_of(t * rb, rb), rb)
            h = a_ref[sl, c0:c0 + lc] * h + u_ref[sl, c0:c0 + lc]
            u_ref[sl, c0:c0 + lc] = h
            return h

        hlast_ref[:, c0:c0 + lc] = lax.fori_loop(0, t_valid, body, hlast_ref[:, c0:c0 + lc])

    live = lax.broadcasted_iota(jnp.int32, (rows, c), 0) < t_valid * rb
    ya_ref[...] = jnp.where(live, u_ref[...] * jax.nn.gelu(ga_ref[...].astype(F32)), 0.0).astype(ya_ref.dtype)
    convnew_ref[...] = xpad_ref[pl.ds(pl.multiple_of(t_valid * rb, rb), tail), :]


def rglru(proj, n_t, conv0, h0, cw, cb, wg, bg, lam, tb, rb):
    c = h0.shape[1]
    n_rows = proj.shape[0]
    rows = tb * rb
    tail = (CONV_W - 1) * rb
    return pl.pallas_call(
        functools.partial(_rglru_kernel, n_t=n_t, tb=tb, rb=rb),
        out_shape=(
            jax.ShapeDtypeStruct((n_rows, c), BF16),
            jax.ShapeDtypeStruct((rb, c), F32),
            jax.ShapeDtypeStruct((tail, c), F32),
        ),
        grid=(n_rows // rows,),
        in_specs=[
            pl.BlockSpec((rows, c), lambda i: (i, 0)),
            pl.BlockSpec((rows, c), lambda i: (i, 1)),
            _full((tail, c)), _full((rb, c)), _full((CONV_W, c)), _full((1, c)),
            _full((c, 2 * c)), _full((1, 2 * c)), _full((1, c)),
        ],
        out_specs=(
            pl.BlockSpec((rows, c), lambda i: (i, 0)),
            _full((rb, c)),
            _full((tail, c)),
        ),
        scratch_shapes=[
            pltpu.VMEM((rows + tail, c), F32),
            pltpu.VMEM((rows, c), F32),
            pltpu.VMEM((rows, c), F32),
        ],
        compiler_params=_params("arbitrary"),
        name="rglru",
    )(proj, proj, conv0, h0, cw, cb.reshape(1, c), wg, bg.reshape(1, 2 * c), lam.reshape(1, c))


def _hgrn2_kernel(q_ref, f_ref, v_ref, gb_ref, s0_ref, lbraw_ref, gn_ref, stack_ref, ob_ref, snew_ref, st_ref,
                  *, layer, n_t, tb, nb, nh, m):
    del stack_ref
    i = pl.program_id(1)
    tc = m // nb
    half = tc // 2
    d = q_ref.shape[-1]
    dk = d // nh
    t_valid = jnp.minimum(n_t - i * tb, tb)
    n_chunks = (t_valid + tc - 1) // tc

    @pl.when(i == 0)
    def _():
        for b in range(nb):
            for h in range(nh):
                st_ref[h, :, b * dk:(b + 1) * dk] = s0_ref[b, h].T

    @pl.when(n_chunks < tb // tc)
    def _():
        ob_ref[...] = jnp.zeros_like(ob_ref)

    p = jax.nn.softmax(lbraw_ref[...], axis=0)
    cum = p[0:1, :]
    for r in range(1, layer + 1):
        cum = cum + p[r:r + 1, :]
    lb = cum - p[0:1, :]
    log_lb = jnp.log(lb)
    log_1mlb = jnp.log1p(-lb)

    assert tc % 2 == 0 and nb & (nb - 1) == 0
    row = lax.broadcasted_iota(jnp.int32, (m, m), 0)
    col = lax.broadcasted_iota(jnp.int32, (m, m), 1)
    same_seq_causal = jnp.where(((row & (nb - 1)) == (col & (nb - 1))) & (col <= row), 1.0, 0.0)
    bid = lax.broadcasted_iota(jnp.int32, (m, dk), 0) & (nb - 1)

    def chunk(c, carry):
        def rows_of(ref):
            if len(ref.shape) == 2:
                return ref[pl.ds(pl.multiple_of(c * m, m), m), :]
            return ref[pl.ds(c * tc, tc)].reshape(m, d)

        live = lax.broadcasted_iota(jnp.int32, (m, d), 0) < (t_valid - c * tc) * nb
        q = jax.nn.silu(rows_of(q_ref).astype(F32))
        fr = rows_of(f_ref).astype(F32)
        v = rows_of(v_ref).astype(BF16)
        gb = rows_of(gb_ref).astype(F32)
        logf = jnp.where(live, jnp.logaddexp(log_lb, log_1mlb + jax.nn.log_sigmoid(fr)), 0.0)
        k = 1.0 - jnp.exp(logf)
        slabs = [logf[0:nb]]
        for t in range(1, tc):
            slabs.append(slabs[-1] + logf[t * nb:(t + 1) * nb])
        g = jnp.concatenate(slabs, axis=0)
        g_last = jnp.concatenate([slabs[-1]] * tc, axis=0)
        g_rel = g - jnp.concatenate([slabs[half - 1]] * tc, axis=0)
        qa = (q * jnp.exp(jnp.minimum(g_rel, HGRN_EXP_CLAMP))).astype(BF16)
        kt = (k * jnp.exp(jnp.minimum(-g_rel, HGRN_EXP_CLAMP))).astype(BF16)
        qt = (q * jnp.exp(g)).astype(BF16)
        ks = (k * jnp.exp(g_last - g)).astype(BF16)
        dec = jnp.exp(slabs[-1])

        outs = []
        for h in range(nh):
            hs = slice(h * dk, (h + 1) * dk)
            qt_h, kt_h, ks_h, v_h = qt[:, hs], kt[:, hs], ks[:, hs], v[:, hs]
            att = lax.dot_general(qa[:, hs], kt_h, (((1,), (1,)), ((), ())), preferred_element_type=F32)
            att = (att * same_seq_causal).astype(BF16)
            o = _dot(att, v_h)
            zero = jnp.zeros_like(qt_h)
            expand = lambda x: jnp.concatenate([jnp.where(bid == b, x, zero) for b in range(nb)], axis=1)
            st_h = st_ref[h]
            o = o + lax.dot_general(expand(qt_h), st_h.astype(BF16), (((1,), (1,)), ((), ())),
                                    preferred_element_type=F32)
            dst = lax.dot_general(v_h, expand(ks_h), (((0,), (0,)), ((), ())), preferred_element_type=F32)
            dec_row = jnp.concatenate([dec[b:b + 1, hs] for b in range(nb)], axis=1)
            st_ref[h] = st_h * dec_row + dst
            outs.append(_rms(o, gn_ref[:, hs]))
        ob = jnp.where(live, jnp.concatenate(outs, axis=1) * jax.nn.silu(gb), 0.0)
        ob_ref[pl.ds(pl.multiple_of(c * m, m), m), :] = ob.astype(ob_ref.dtype)
        return carry

    lax.fori_loop(0, n_chunks, chunk, 0)

    @pl.when(i == pl.num_programs(1) - 1)
    def _():
        for b in range(nb):
            for h in range(nh):
                snew_ref[b, h] = st_ref[h, :, b * dk:(b + 1) * dk].T


def hgrn2(proj, s0, s0_layer, s_stack, n_layers, lb_raw, gnorm, layer, n_t, tb, nb):
    _, bsz, nh, dk, _ = s0.shape
    d = nh * dk
    off = proj.shape[-1] // d - 4
    nbb = bsz // nb
    n_tpad = proj.shape[0] // bsz
    nt = n_tpad // tb
    m = max(k for k in range(2 * nb, HGRN_ROWS + 1, 2 * nb) if (tb * nb) % k == 0)
    if nb == bsz:
        col = lambda k: pl.BlockSpec((tb * nb, d), lambda j, i, k=k: (i, k + off))
    else:
        proj = proj.reshape(n_tpad, bsz, proj.shape[1])
        col = lambda k: pl.BlockSpec((tb, nb, d), lambda j, i, k=k: (i, j, k + off))
    in_specs = [
        col(0), col(1), col(2), col(3),
        pl.BlockSpec((None, nb, nh, dk, dk), lambda j, i: (s0_layer, j, 0, 0, 0)),
        _full(lb_raw.shape),
        _full((1, d)),
        pl.BlockSpec(memory_space=pl.ANY),
    ]
    args = [proj, proj, proj, proj, s0, lb_raw, gnorm.reshape(1, d), s_stack]
    assert s_stack.shape == (n_layers, bsz, nh, dk, dk)
    return pl.pallas_call(
        functools.partial(_hgrn2_kernel, layer=layer, n_t=n_t, tb=tb, nb=nb, nh=nh, m=m),
        out_shape=(
            jax.ShapeDtypeStruct((n_tpad * bsz, d), BF16),
            jax.ShapeDtypeStruct((n_layers, bsz, nh, dk, dk), F32),
        ),
        grid=(nbb, nt),
        in_specs=in_specs,
        out_specs=(
            pl.BlockSpec((tb * nb, d), lambda j, i: (j * nt + i, 0)),
            pl.BlockSpec((None, nb, nh, dk, dk), lambda j, i: (layer, j, 0, 0, 0)),
        ),
        scratch_shapes=[pltpu.VMEM((nh, dk, nb * dk), F32)],
        input_output_aliases={len(args) - 1: 1},
        compiler_params=_params("arbitrary", "arbitrary"),
        name="hgrn2",
    )(*args)


def _s5_discretise(lr, li, ldt):
    dt = jnp.exp(ldt)
    mag = jnp.exp(lr * dt)
    return mag * jnp.cos(li * dt), mag * jnp.sin(li * dt)


def _s5_prep_kernel(lr_ref, li_ref, ldt_ref, lrc_ref, lic_ref, ldtc_ref, bre_ref, bim_ref,
                    ar_ref, ai_ref, ore_ref, oim_ref):
    ar_ref[...], ai_ref[...] = _s5_discretise(lr_ref[...], li_ref[...], ldt_ref[...])
    lr = lrc_ref[...]
    li = lic_ref[...]
    ar, ai = _s5_discretise(lr, li, ldtc_ref[...])
    den = lr * lr + li * li
    cr = ((ar - 1.0) * lr + ai * li) / den
    ci = (ai * lr - (ar - 1.0) * li) / den
    ore_ref[...] = cr * bre_ref[...] - ci * bim_ref[...]
    oim_ref[...] = cr * bim_ref[...] + ci * bre_ref[...]


def s5_prep(lam_re, lam_im, log_dt, b_re, b_im):
    g, p, c = b_re.shape
    dense = lambda a: a.reshape(-1, LANES)
    per_state = lambda a: dense(jnp.broadcast_to(a.reshape(g, -1, 1), (g, p, 1)))
    per_coef = lambda a: dense(jnp.broadcast_to(a.reshape(g, -1, 1), (g, p, c)))
    small = jax.ShapeDtypeStruct((g * p // LANES, LANES), F32)
    big = jax.ShapeDtypeStruct((g * p * c // LANES, LANES), F32)
    ar, ai, ore, oim = pl.pallas_call(
        _s5_prep_kernel, out_shape=(small, small, big, big), name="s5_prep",
    )(per_state(lam_re), per_state(lam_im), per_state(log_dt), per_coef(lam_re), per_coef(lam_im), per_coef(log_dt),
      dense(b_re), dense(b_im))
    return ar.reshape(g, p), ai.reshape(g, p), ore.reshape(g, p, c), oim.reshape(g, p, c)


def _s5_kernel(x_ref, g_ref, wb_ref, wc_ref, ar_ref, ai_ref, d_ref, h0r_ref, h0i_ref,
               z_ref, hr_ref, hi_ref, *, n_t, tb, rb, m):
    i = pl.program_id(0)
    nch = wb_ref.shape[0]
    cw = wb_ref.shape[1]
    sw = wb_ref.shape[2] // 2
    tsb = m // rb
    nsl = rb // SUBLANES
    n_sub = jnp.minimum(n_t - i * tb, tb) // tsb

    @pl.when(i == 0)
    def _():
        hr_ref[...] = h0r_ref[...]
        hi_ref[...] = h0i_ref[...]

    @pl.when(n_sub < tb // tsb)
    def _():
        z_ref[...] = jnp.zeros_like(z_ref)

    def sub_block(j, carry):
        r0 = pl.multiple_of(j * m, m)
        xn = _rms(x_ref[pl.ds(r0, m), :], g_ref[...])
        xnb = xn.astype(BF16)
        drive = [_dot(xnb[:, c * cw:(c + 1) * cw], wb_ref[c]) for c in range(nch)]
        for c in range(nch):
            cs = slice(c * cw, (c + 1) * cw)
            ss = slice(c * sw, (c + 1) * sw)
            u = xn[:, cs]
            bu = drive[c]
            ar = jnp.broadcast_to(ar_ref[:, ss], (SUBLANES, sw))
            ai = jnp.broadcast_to(ai_ref[:, ss], (SUBLANES, sw))
            out_r = [None] * (tsb * nsl)
            out_i = [None] * (tsb * nsl)
            for s in range(nsl):
                srow = slice(s * SUBLANES, (s + 1) * SUBLANES)
                hr, hi = hr_ref[srow, ss], hi_ref[srow, ss]
                for t in range(tsb):
                    lo = t * rb + s * SUBLANES
                    hr, hi = (ar * hr - ai * hi + bu[lo:lo + SUBLANES, :sw],
                              ar * hi + ai * hr + bu[lo:lo + SUBLANES, sw:])
                    out_r[t * nsl + s] = hr
                    out_i[t * nsl + s] = hi
                hr_ref[srow, ss] = hr
                hi_ref[srow, ss] = hi
            hcat = jnp.concatenate([jnp.concatenate(out_r, axis=0).astype(BF16),
                                    jnp.concatenate(out_i, axis=0).astype(BF16)], axis=1)
            y = _dot(hcat, wc_ref[c]) + d_ref[:, cs] * u
            z_ref[pl.ds(r0, m), cs] = jax.nn.gelu(y).astype(z_ref.dtype)
        return carry

    lax.fori_loop(0, n_sub, sub_block, 0)


def s5(x, n_t, g, wb, wc, li, a_re, a_im, dskip, h0r, h0i, tb, rb):
    n_rows, d = x.shape
    ns = h0r.shape[1]
    rows = tb * rb
    m = max(k for k in range(rb, S5_MAX_SUB_ROWS + 1, rb)
            if rows % k == 0 and n_t % (k // rb) == 0 and k % (2 * SUBLANES) == 0)
    return pl.pallas_call(
        functools.partial(_s5_kernel, n_t=n_t, tb=tb, rb=rb, m=m),
        out_shape=(
            jax.ShapeDtypeStruct((n_rows, d), BF16),
            jax.ShapeDtypeStruct((rb, ns), F32),
            jax.ShapeDtypeStruct((rb, ns), F32),
        ),
        grid=(n_rows // rows,),
        in_specs=[
            pl.BlockSpec((rows, d), lambda i: (i, 0)),
            _full((1, d)),
            pl.BlockSpec((None,) + wb.shape[1:], lambda i: (li, 0, 0, 0)),
            pl.BlockSpec((None,) + wc.shape[1:], lambda i: (li, 0, 0, 0)),
            _full((1, ns)), _full((1, ns)), _full((1, d)),
            _full((rb, ns)), _full((rb, ns)),
        ],
        out_specs=(
            pl.BlockSpec((rows, d), lambda i: (i, 0)),
            _full((rb, ns)), _full((rb, ns)),
        ),
        compiler_params=_params("arbitrary"),
        name="s5",
    )(x, g.reshape(1, d), wb, wc, a_re, a_im, dskip.reshape(1, d), h0r, h0i)


def _s5_block_diag_in(b):
    g, p, c = b.shape
    gpc = V7X_MXU_DIM // c
    bt = b.transpose(0, 2, 1).reshape(g // gpc, gpc, c, p)
    out = jnp.einsum('ngcp,gh->ngchp', bt, jnp.eye(gpc, dtype=b.dtype))
    return out.reshape(g // gpc, gpc * c, gpc * p)


def _s5_block_diag_out(cm):
    g, c, p = cm.shape
    gpc = V7X_MXU_DIM // c
    ct = cm.transpose(0, 2, 1).reshape(g // gpc, gpc, p, c)
    out = jnp.einsum('ngpc,gh->ngphc', ct, jnp.eye(gpc, dtype=cm.dtype))
    return out.reshape(g // gpc, gpc * p, gpc * c)


def _moe_kernel(x_ref, g_ref, wrt_ref, wgu_ref, wdn_ref, o_ref, xn_ref, pos_ref, gate_ref, *, sub):
    e = pl.program_id(1)
    tm = x_ref.shape[0]
    ne = wrt_ref.shape[0]
    dff = wdn_ref.shape[1]

    @pl.when(e == 0)
    def _():
        x = x_ref[...]
        xn = _rms(x, g_ref[...])
        xn_hi = xn.astype(BF16)
        xn_ref[...] = xn_hi
        xn_lo = (xn - xn_hi.astype(F32)).astype(BF16)
        wr = wrt_ref[...]
        wr_hi = wr.astype(BF16)
        wr_lo = (wr - wr_hi.astype(F32)).astype(BF16)
        nt = lambda a, b: lax.dot_general(a, b, (((1,), (1,)), ((), ())), preferred_element_type=F32)
        logits = nt(wr_hi, xn_hi) + (nt(wr_hi, xn_lo) + nt(wr_lo, xn_hi))
        ex = jnp.exp(logits - jnp.max(logits, axis=0, keepdims=True))
        probs = ex / jnp.sum(ex, axis=0, keepdims=True)
        eid = lax.broadcasted_iota(jnp.int32, (ne, tm), 0).astype(F32)
        m1 = jnp.max(probs, axis=0, keepdims=True)
        i1 = jnp.min(jnp.where(probs == m1, eid, float(ne)), axis=0, keepdims=True)
        sel1 = eid == i1
        rest = jnp.where(sel1, -1.0, probs)
        m2 = jnp.max(rest, axis=0, keepdims=True)
        i2 = jnp.min(jnp.where(rest == m2, eid, float(ne)), axis=0, keepdims=True)
        sel2 = eid == i2
        den = m1 + m2
        gate_ref[...] = jnp.where(sel1, m1 / den, 0.0) + jnp.where(sel2, m2 / den, 0.0)
        chosen = jnp.where(sel1, 1.0, jnp.where(sel2, 1.0, 0.0))
        r = lax.broadcasted_iota(jnp.int32, (tm, tm), 0)
        c = lax.broadcasted_iota(jnp.int32, (tm, tm), 1)
        before = jnp.where(r < c, 1.0, 0.0).astype(BF16)
        rank = _dot(chosen.astype(BF16), before)
        pos_ref[...] = jnp.where(chosen > 0.0, rank, -1.0)
        o_ref[...] = x

    pos_e = pos_ref[pl.ds(e, 1), :]
    gate_e = gate_ref[pl.ds(e, 1), :]
    cnt = jnp.sum(jnp.where(pos_e >= 0.0, 1.0, 0.0)).astype(jnp.int32)

    def run_block(base, rows):
        slot = lax.broadcasted_iota(jnp.int32, (rows, tm), 0) + base
        hit = pos_e == slot.astype(F32)
        onehot = jnp.where(hit, 1.0, 0.0).astype(BF16)
        xs = _dot(onehot, xn_ref[...]).astype(BF16)
        gs = jnp.sum(jnp.where(hit, gate_e, 0.0), axis=1, keepdims=True)
        hgu = _dot(xs, wgu_ref[0])
        act = (jax.nn.silu(hgu[:, :dff]) * hgu[:, dff:]).astype(BF16)
        yb = (_dot(act, wdn_ref[0]) * gs).astype(BF16)
        o_ref[...] += lax.dot_general(onehot, yb, (((0,), (0,)), ((), ())), preferred_element_type=F32)

    def body(s, carry):
        run_block(s * sub, sub)
        return carry

    n_full = cnt // sub
    lax.fori_loop(0, n_full, body, 0)
    rem = cnt - n_full * sub
    lo = 0
    for rows in MOE_BLOCK_ROWS:
        pl.when((rem > lo) & (rem <= rows))(functools.partial(run_block, n_full * sub, rows))
        lo = rows


def moe_residual(x, g, w_router_t, w_gu, w_down, li, tm, sub):
    n, d = x.shape
    _, ne, _, dff2 = w_gu.shape
    assert MOE_BLOCK_ROWS[-1] == sub
    return pl.pallas_call(
        functools.partial(_moe_kernel, sub=sub),
        out_shape=jax.ShapeDtypeStruct((n, d), F32),
        grid=(n // tm, ne),
        in_specs=[
            pl.BlockSpec((tm, d), lambda i, e: (i, 0)),
            _full((1, d)),
            _full((ne, d)),
            pl.BlockSpec((None, 1, d, dff2), lambda i, e: (li, e, 0, 0)),
            pl.BlockSpec((None, 1, dff2 // 2, d), lambda i, e: (li, e, 0, 0)),
        ],
        out_specs=pl.BlockSpec((tm, d), lambda i, e: (i, 0)),
        scratch_shapes=[
            pltpu.VMEM((tm, d), BF16),
            pltpu.VMEM((ne, tm), F32),
            pltpu.VMEM((ne, tm), F32),
        ],
        compiler_params=_params("parallel", "arbitrary"),
        name="moe",
    )(x, g.reshape(1, d), w_router_t, w_gu, w_down)


def _trunk(x, n_t, rb, tb, nb, h0, conv0, s0, re0, im0, w):
    n, d = x.shape
    depth = w['norm_mix'].shape[0]
    d_a = h0.shape[-1]
    nh, dk = s0.shape[2], s0.shape[3]
    d_b = nh * dk
    ns = re0.shape[-2] * re0.shape[-1]
    tm = tb * rb
    n_even = (depth + 1) // 2
    new = {k: [] for k in ('h', 'conv', 're', 'im')}
    s_stack = jnp.zeros((n_even,) + s0.shape[1:], F32)
    for l in range(depth):
        li = l // 2
        if l % 2 == 0:
            proj = norm_matmul(x, w['norm_mix'][l], w['even_w_in'], li, tm, w['even_w_in'].shape[2])
            conv_tm = conv0[li].transpose(1, 0, 2).reshape((CONV_W - 1) * rb, d_a)
            ya, h_new, conv_new = rglru(proj, n_t, conv_tm, h0[li], w['rglru_conv_w'][li], w['rglru_conv_b'][li],
                                        w['rglru_wg'][li], w['rglru_bg'][li], w['rglru_lambda'][li], tb=tb, rb=rb)
            new['h'].append(h_new)
            new['conv'].append(conv_new.reshape(CONV_W - 1, rb, d_a).transpose(1, 0, 2))
            ob, s_stack = hgrn2(proj, s0, li % s0.shape[0], s_stack, n_even,
                                w['hgrn2_lb_raw'], w['hgrn2_gnorm'][li], li, n_t=n_t, tb=tb, nb=nb)
            if nb != rb:
                ob = ob.reshape(rb // nb, n // rb, nb, d_b).transpose(1, 0, 2, 3).reshape(n, d_b)
            x = mix_ffn(ya, ob, x, w['even_w_out'], w['norm_ffn'][l], w['ffn_w_gu'], w['ffn_w_down'], li, tm,
                        w['ffn_w_down'].shape[1] // 2)
        else:
            z, re_new, im_new = s5(x, n_t, w['norm_mix'][l], w['s5_wb'], w['s5_wc'], li, w['s5_a_re'][li],
                                   w['s5_a_im'][li], w['s5_d'][li], re0[li].reshape(rb, ns), im0[li].reshape(rb, ns),
                                   tb=tb, rb=rb)
            new['re'].append(re_new.reshape(re0.shape[1:]))
            new['im'].append(im_new.reshape(im0.shape[1:]))
            x = glu_residual(z, w['s5_w_glu'], li, x, tm, w['s5_w_glu'].shape[2] // 2)
            x = moe_residual(x, w['norm_ffn'][l], w['moe_w_router_t'][li], w['moe_w_gu'], w['moe_w_down'], li,
                             tm, MOE_BLOCK_ROWS[-1])
    stack = lambda k: jnp.stack(new[k])
    return x, (stack('h'), stack('conv'), s_stack, stack('re'), stack('im'))


def kernel(x_prompt, x_sample, state_rglru_h, state_rglru_conv, state_hgrn2, state_s5_re, state_s5_im,
           meta_tokens, norm_mix, norm_ffn, norm_final, even_w_in, even_w_out,
           rglru_conv_w, rglru_conv_b, rglru_w_a, rglru_b_a, rglru_w_x, rglru_b_x, rglru_lambda,
           hgrn2_lb_raw, hgrn2_gnorm, s5_lam_re, s5_lam_im, s5_log_dt, s5_b_re, s5_b_im, s5_c_re, s5_c_im,
           s5_d, s5_w_glu, ffn_w_gu, ffn_w_down, moe_w_router, moe_w_gu, moe_w_down):
    bp, tp0, d = x_prompt.shape
    bs, ts, _ = x_sample.shape
    tp = tp0 + N_META
    d_a = state_rglru_h.shape[-1]
    n_even, n_odd = state_rglru_h.shape[0], state_s5_re.shape[0]
    assert bp == SUBLANES and bs % SUBLANES == 0
    tb_p = PROMPT_BLOCK_ROWS // bp
    tp_pad = -(-tp // tb_p) * tb_p
    nb_s = TIME_CHUNK_ROWS // ts
    assert (tp % (TIME_CHUNK_ROWS // bp) == 0 and tb_p % (TIME_CHUNK_ROWS // bp) == 0 and bs % nb_s == 0
            and nb_s * ts == TIME_CHUNK_ROWS and nb_s % SUBLANES == 0)

    eye_a = jnp.eye(H_A, dtype=F32)
    block_diag = lambda m: jnp.einsum('lhij,hg->lhigj', m, eye_a).reshape(n_even, d_a, d_a)
    s5_g, s5_p, s5_c = s5_b_re.shape[1:]
    lg = n_odd * s5_g
    a_re, a_im, bt_re, bt_im = s5_prep(s5_lam_re.reshape(lg, s5_p), s5_lam_im.reshape(lg, s5_p), s5_log_dt.reshape(lg),
                                       s5_b_re.reshape(lg, s5_p, s5_c), s5_b_im.reshape(lg, s5_p, s5_c))
    s5_wb = jnp.concatenate([_s5_block_diag_in(bt_re.reshape(lg, s5_p, s5_c)),
                             _s5_block_diag_in(bt_im.reshape(lg, s5_p, s5_c))], axis=2).astype(BF16)
    s5_wc = jnp.concatenate([_s5_block_diag_out(s5_c_re.reshape(lg, s5_c, s5_p)),
                             -_s5_block_diag_out(s5_c_im.reshape(lg, s5_c, s5_p))], axis=1).astype(BF16)
    ns = s5_lam_re.shape[1] * s5_lam_re.shape[2]
    w = {
        'norm_mix': norm_mix, 'norm_ffn': norm_ffn, 'norm_final': norm_final,
        'even_w_in': even_w_in.astype(BF16), 'even_w_out': even_w_out.astype(BF16),
        'rglru_conv_w': rglru_conv_w, 'rglru_conv_b': rglru_conv_b,
        'rglru_wg': jnp.concatenate([block_diag(rglru_w_a), block_diag(rglru_w_x)], axis=2).astype(BF16),
        'rglru_bg': jnp.concatenate([rglru_b_a, rglru_b_x], axis=1),
        'rglru_lambda': rglru_lambda, 'hgrn2_lb_raw': hgrn2_lb_raw, 'hgrn2_gnorm': hgrn2_gnorm,
        's5_a_re': a_re.reshape(n_odd, 1, ns), 's5_a_im': a_im.reshape(n_odd, 1, ns),
        's5_wb': s5_wb.reshape((n_odd, -1) + s5_wb.shape[1:]), 's5_wc': s5_wc.reshape((n_odd, -1) + s5_wc.shape[1:]),
        's5_d': s5_d, 's5_w_glu': s5_w_glu.astype(BF16),
        'ffn_w_gu': ffn_w_gu.astype(BF16), 'ffn_w_down': ffn_w_down.astype(BF16),
        'moe_w_router_t': moe_w_router.transpose(0, 2, 1),
        'moe_w_gu': moe_w_gu.astype(BF16), 'moe_w_down': moe_w_down.astype(BF16),
    }

    tc_p = TIME_CHUNK_ROWS // bp
    assert N_META == tc_p
    xm = to_time_major(x_prompt, meta_tokens.astype(x_prompt.dtype), tp_pad, tb_p // tc_p)
    zero = lambda ref, lead: jnp.zeros((lead, bp) + ref.shape[2:], ref.dtype)
    xp, p_new = _trunk(xm, tp, bp, tb_p, bp, zero(state_rglru_h, n_even), zero(state_rglru_conv, n_even),
                       zero(state_hgrn2, 1), zero(state_s5_re, n_odd), zero(state_s5_im, n_odd), w)
    nk_out = max(k for k in range(1, TIME_CHUNK_ROWS // tc_p + 1) if tp0 % (k * tc_p) == 0)
    y_prompt = rmsnorm_batch_major(xp, norm_final, bp, N_META, tp0, tc_p, nk_out)

    xs = x_sample.transpose(1, 0, 2).reshape(ts * bs, d)
    xs, s_new = _trunk(xs, ts, bs, ts, nb_s, state_rglru_h, state_rglru_conv, state_hgrn2, state_s5_re, state_s5_im, w)
    y_sample = rmsnorm_rows(xs, norm_final, ts * bs).reshape(ts, bs, d).transpose(1, 0, 2)

    refs = (state_rglru_h, state_rglru_conv, state_hgrn2, state_s5_re, state_s5_im)
    cast = lambda new: tuple(a.astype(r.dtype) for a, r in zip(new, refs))
    return (y_prompt, y_sample) + cast(p_new) + cast(s_new)
```

```python
import functools

import jax
import jax.numpy as jnp
from jax import lax
from jax.experimental import pallas as pl
from jax.experimental.pallas import tpu as pltpu

F32 = jnp.float32
BF16 = jnp.bfloat16

EPS = 1e-6
N_META = 16
CONV_W = 4
RG_C = 8.0
H_A = 8

V7X_VMEM_LIMIT_BYTES = 56 * 1024 * 1024
SUBLANES = 8
LANES = 128
V7X_MXU_DIM = 256
MOE_BLOCK_ROWS = (128, 160, 192, 224, 256, 288, 320)
TIME_CHUNK_ROWS = 128
HGRN_ROWS = 256
S5_MAX_SUB_ROWS = 384
HGRN_EXP_CLAMP = 80.0
PROMPT_BLOCK_ROWS = 768
SCAN_CARRY_ELEMS = 16 * SUBLANES * LANES


def _params(*sem):
    return pltpu.CompilerParams(dimension_semantics=sem, vmem_limit_bytes=V7X_VMEM_LIMIT_BYTES)


def _rms(x, g):
    ms = jnp.mean(x * x, axis=-1, keepdims=True)
    return x * lax.rsqrt(ms + EPS) * g


def _dot(a, b):
    return jnp.dot(a, b, preferred_element_type=F32)


def _full(shape):
    return pl.BlockSpec(shape, lambda *_: (0,) * len(shape))


def _norm_matmul_kernel(x_ref, g_ref, w_ref, o_ref, xn_ref):
    @pl.when(pl.program_id(1) == 0)
    def _():
        xn_ref[...] = _rms(x_ref[...], g_ref[...]).astype(BF16)

    o_ref[...] = _dot(xn_ref[...], w_ref[...]).astype(o_ref.dtype)


def norm_matmul(x, g, w, li, tm, tn):
    n, d = x.shape
    nout = w.shape[2]
    return pl.pallas_call(
        _norm_matmul_kernel,
        out_shape=jax.ShapeDtypeStruct((n, nout), BF16),
        grid=(n // tm, nout // tn),
        in_specs=[
            pl.BlockSpec((tm, d), lambda i, j: (i, 0)),
            _full((1, d)),
            pl.BlockSpec((None, d, tn), lambda i, j: (li, 0, j)),
        ],
        out_specs=pl.BlockSpec((tm, tn), lambda i, j: (i, j)),
        scratch_shapes=[pltpu.VMEM((tm, d), BF16)],
        compiler_params=_params("parallel", "arbitrary"),
        name="norm_matmul",
    )(x, g.reshape(1, d), w)


def _mix_ffn_kernel(ya_ref, ob_ref, x_ref, wo_ref, g_ref, wg_ref, wu_ref, wdn_ref, o_ref, xn_ref):
    j = pl.program_id(1)
    d_a = ya_ref.shape[1]

    @pl.when(j == 0)
    def _():
        x1 = x_ref[...] + _dot(ya_ref[...], wo_ref[:d_a, :]) + _dot(ob_ref[...], wo_ref[d_a:, :])
        xn_ref[...] = _rms(x1, g_ref[...]).astype(BF16)
        o_ref[...] = x1

    xn = xn_ref[...]
    h = (jax.nn.silu(_dot(xn, wg_ref[...])) * _dot(xn, wu_ref[...])).astype(BF16)
    o_ref[...] += _dot(h, wdn_ref[...])


def mix_ffn(ya, ob, x, w_out, g, w_gu, w_down, li, tm, chunk):
    n, d = x.shape
    dff = w_down.shape[1]
    assert dff % chunk == 0 and chunk % LANES == 0
    nj = dff // chunk
    rows = lambda a: pl.BlockSpec((tm, a.shape[1]), lambda i, j: (i, 0))
    return pl.pallas_call(
        _mix_ffn_kernel,
        out_shape=jax.ShapeDtypeStruct((n, d), F32),
        grid=(n // tm, nj),
        in_specs=[
            rows(ya), rows(ob), rows(x),
            pl.BlockSpec((None,) + w_out.shape[1:], lambda i, j: (li, 0, 0)),
            _full((1, d)),
            pl.BlockSpec((None, d, chunk), lambda i, j: (li, 0, j)),
            pl.BlockSpec((None, d, chunk), lambda i, j: (li, 0, j + nj)),
            pl.BlockSpec((None, chunk, d), lambda i, j: (li, j, 0)),
        ],
        out_specs=pl.BlockSpec((tm, d), lambda i, j: (i, 0)),
        scratch_shapes=[pltpu.VMEM((tm, d), BF16)],
        compiler_params=_params("parallel", "arbitrary"),
        name="mix_ffn",
    )(ya, ob, x, w_out, g.reshape(1, d), w_gu, w_gu, w_down)


def _glu_residual_kernel(z_ref, wv_ref, wg_ref, r_ref, o_ref):
    z = z_ref[...]
    val = _dot(z, wv_ref[...])
    gate = _dot(z, wg_ref[...])
    o_ref[...] = r_ref[...] + val * jax.nn.sigmoid(gate)


def glu_residual(z, w_glu, li, res, tm, tn):
    n, d = z.shape
    dout = w_glu.shape[2] // 2
    nj = dout // tn
    return pl.pallas_call(
        _glu_residual_kernel,
        out_shape=jax.ShapeDtypeStruct((n, dout), F32),
        grid=(n // tm, nj),
        in_specs=[
            pl.BlockSpec((tm, d), lambda i, j: (i, 0)),
            pl.BlockSpec((None, d, tn), lambda i, j: (li, 0, j)),
            pl.BlockSpec((None, d, tn), lambda i, j: (li, 0, j + nj)),
            pl.BlockSpec((tm, tn), lambda i, j: (i, j)),
        ],
        out_specs=pl.BlockSpec((tm, tn), lambda i, j: (i, j)),
        compiler_params=_params("parallel", "arbitrary"),
        name="glu_residual",
    )(z, w_glu, w_glu, res)


def _rmsnorm_kernel(x_ref, g_ref, o_ref):
    o_ref[...] = _rms(x_ref[...], g_ref[...])


def rmsnorm_rows(x, g, tm):
    n, d = x.shape
    return pl.pallas_call(
        _rmsnorm_kernel,
        out_shape=jax.ShapeDtypeStruct((n, d), F32),
        grid=(n // tm,),
        in_specs=[pl.BlockSpec((tm, d), lambda i: (i, 0)), _full((1, d))],
        out_specs=pl.BlockSpec((tm, d), lambda i: (i, 0)),
        compiler_params=_params("parallel"),
        name="final_rmsnorm",
    )(x, g.reshape(1, d))


def _rmsnorm_batch_major_kernel(*refs, nk, tc, nb):
    x_refs, g_ref, o_ref = refs[:nk], refs[nk], refs[nk + 1]
    d = o_ref.shape[-1]
    for k in range(nk):
        y = _rms(x_refs[k][...], g_ref[...])
        o_ref[:, k * tc:(k + 1) * tc, :] = jnp.swapaxes(y.reshape(tc, nb, d), 0, 1)


def rmsnorm_batch_major(x, g, nb, t_skip, t_out, tc, nk):
    n, d = x.shape
    rows = tc * nb
    assert t_skip % tc == 0 and t_out % (nk * tc) == 0
    specs = [pl.BlockSpec((rows, d), lambda j, k=k: (nk * j + t_skip // tc + k, 0)) for k in range(nk)]
    return pl.pallas_call(
        functools.partial(_rmsnorm_batch_major_kernel, nk=nk, tc=tc, nb=nb),
        out_shape=jax.ShapeDtypeStruct((nb, t_out, d), F32),
        grid=(t_out // (nk * tc),),
        in_specs=specs + [_full((1, d))],
        out_specs=pl.BlockSpec((nb, nk * tc, d), lambda j: (0, j, 0)),
        compiler_params=_params("parallel"),
        name="final_rmsnorm_batch_major",
    )(*([x] * nk), g.reshape(1, d))


def _to_time_major_kernel(*refs, nk, n_chunks):
    x_refs, lead_ref, o_ref = refs[:nk], refs[nk], refs[nk + 1]
    i = pl.program_id(0)
    nb, tc, d = x_refs[0].shape
    rows = tc * nb
    for k in range(nk):
        chunk = i * nk + k - 1
        val = jnp.swapaxes(x_refs[k][...], 0, 1).reshape(rows, d)
        if k == 0:
            lead = jnp.broadcast_to(lead_ref[...][:, None, :], (tc, nb, d)).reshape(rows, d)
            val = jnp.where(i == 0, lead, val)
        o_ref[k * rows:(k + 1) * rows, :] = jnp.where(chunk < n_chunks, val, 0.0)


def to_time_major(x, lead, t_pad, nk):
    nb, t, d = x.shape
    tc = lead.shape[0]
    assert t % tc == 0 and t_pad % (nk * tc) == 0
    n_chunks = t // tc
    specs = [pl.BlockSpec((nb, tc, d), lambda i, k=k: (0, jnp.clip(i * nk + k - 1, 0, n_chunks - 1), 0))
             for k in range(nk)]
    return pl.pallas_call(
        functools.partial(_to_time_major_kernel, nk=nk, n_chunks=n_chunks),
        out_shape=jax.ShapeDtypeStruct((t_pad * nb, d), F32),
        grid=(t_pad // (nk * tc),),
        in_specs=specs + [_full((tc, d))],
        out_specs=pl.BlockSpec((nk * tc * nb, d), lambda i: (i, 0)),
        compiler_params=_params("parallel"),
        name="to_time_major",
    )(*([x] * nk), lead)


def _rglru_kernel(xa_ref, ga_ref, conv0_ref, h0_ref, cw_ref, cb_ref, wg_ref, bg_ref, lam_ref,
                  ya_ref, hlast_ref, convnew_ref, xpad_ref, a_ref, u_ref, *, n_t, tb, rb):
    i = pl.program_id(0)
    rows = tb * rb
    tail = (CONV_W - 1) * rb
    c = xa_ref.shape[-1]
    t_valid = jnp.minimum(n_t - i * tb, tb)

    @pl.when(i == 0)
    def _():
        xpad_ref[0:tail, :] = conv0_ref[...]
        hlast_ref[...] = h0_ref[...]

    @pl.when(i > 0)
    def _():
        xpad_ref[0:tail, :] = xpad_ref[rows:rows + tail, :]

    xpad_ref[tail:tail + rows, :] = xa_ref[...].astype(F32)
    xc = cb_ref[...]
    for k in range(CONV_W):
        xc = xc + cw_ref[k:k + 1, :] * xpad_ref[k * rb:k * rb + rows, :]

    gates = _dot(xc.astype(BF16), wg_ref[...]) + bg_ref[...]
    r = jax.nn.sigmoid(gates[:, :c])
    ig = jax.nn.sigmoid(gates[:, c:])
    log_a = (-RG_C) * r * jax.nn.softplus(-lam_ref[...])
    a = jnp.exp(log_a)
    y = 1.0 - a * a
    mult = y * lax.rsqrt(jnp.maximum(y, 1e-30))
    a_ref[...] = a
    u_ref[...] = mult * ig * xc

    lc = min(c, max(LANES, SCAN_CARRY_ELEMS // rb // LANES * LANES))
    for c0 in range(0, c, lc):
        def body(t, h, c0=c0):
            sl = pl.ds(pl.multiple_of(t * rb, rb), rb)
            h = a_ref[sl, c0:c0 + lc] * h + u_ref[sl, c0:c0 + lc]
            u_ref[sl, c0:c0 + lc] = h
            return h

        hlast_ref[:, c0:c0 + lc] = lax.fori_loop(0, t_valid, body, hlast_ref[:, c0:c0 + lc])

    live = lax.broadcasted_iota(jnp.int32, (rows, c), 0) < t_valid * rb
    ya_ref[...] = jnp.where(live, u_ref[...] * jax.nn.gelu(ga_ref[...].astype(F32)), 0.0).astype(ya_ref.dtype)
    convnew_ref[...] = xpad_ref[pl.ds(pl.multiple_of(t_valid * rb, rb), tail), :]


def rglru(proj, n_t, conv0, h0, cw, cb, wg, bg, lam, tb, rb):
    c = h0.shape[1]
    n_rows = proj.shape[0]
    rows = tb * rb
    tail = (CONV_W - 1) * rb
    return pl.pallas_call(
        functools.partial(_rglru_kernel, n_t=n_t, tb=tb, rb=rb),
        out_shape=(
            jax.ShapeDtypeStruct((n_rows, c), BF16),
            jax.ShapeDtypeStruct((rb, c), F32),
            jax.ShapeDtypeStruct((tail, c), F32),
        ),
        grid=(n_rows // rows,),
        in_specs=[
            pl.BlockSpec((rows, c), lambda i: (i, 0)),
            pl.BlockSpec((rows, c), lambda i: (i, 1)),
            _full((tail, c)), _full((rb, c)), _full((CONV_W, c)), _full((1, c)),
            _full((c, 2 * c)), _full((1, 2 * c)), _full((1, c)),
        ],
        out_specs=(
            pl.BlockSpec((rows, c), lambda i: (i, 0)),
            _full((rb, c)),
            _full((tail, c)),
        ),
        scratch_shapes=[
            pltpu.VMEM((rows + tail, c), F32),
            pltpu.VMEM((rows, c), F32),
            pltpu.VMEM((rows, c), F32),
        ],
        compiler_params=_params("arbitrary"),
        name="rglru",
    )(proj, proj, conv0, h0, cw, cb.reshape(1, c), wg, bg.reshape(1, 2 * c), lam.reshape(1, c))


def _hgrn2_kernel(q_ref, f_ref, v_ref, gb_ref, s0_ref, lbraw_ref, gn_ref, stack_ref, ob_ref, snew_ref, st_ref,
                  *, layer, n_t, tb, nb, nh, m):
    del stack_ref
    i = pl.program_id(1)
    tc = m // nb
    half = tc // 2
    d = q_ref.shape[-1]
    dk = d // nh
    t_valid = jnp.minimum(n_t - i * tb, tb)
    n_chunks = (t_valid + tc - 1) // tc

    @pl.when(i == 0)
    def _():
        for b in range(nb):
            for h in range(nh):
                st_ref[h, :, b * dk:(b + 1) * dk] = s0_ref[b, h].T

    @pl.when(n_chunks < tb // tc)
    def _():
        ob_ref[...] = jnp.zeros_like(ob_ref)

    p = jax.nn.softmax(lbraw_ref[...], axis=0)
    cum = p[0:1, :]
    for r in range(1, layer + 1):
        cum = cum + p[r:r + 1, :]
    lb = cum - p[0:1, :]
    log_lb = jnp.log(lb)
    log_1mlb = jnp.log1p(-lb)

    assert tc % 2 == 0 and nb & (nb - 1) == 0
    row = lax.broadcasted_iota(jnp.int32, (m, m), 0)
    col = lax.broadcasted_iota(jnp.int32, (m, m), 1)
    same_seq_causal = jnp.where(((row & (nb - 1)) == (col & (nb - 1))) & (col <= row), 1.0, 0.0)
    bid = lax.broadcasted_iota(jnp.int32, (m, dk), 0) & (nb - 1)

    def chunk(c, carry):
        def rows_of(ref):
            if len(ref.shape) == 2:
                return ref[pl.ds(pl.multiple_of(c * m, m), m), :]
            return ref[pl.ds(c * tc, tc)].reshape(m, d)

        live = lax.broadcasted_iota(jnp.int32, (m, d), 0) < (t_valid - c * tc) * nb
        q = jax.nn.silu(rows_of(q_ref).astype(F32))
        fr = rows_of(f_ref).astype(F32)
        v = rows_of(v_ref).astype(BF16)
        gb = rows_of(gb_ref).astype(F32)
        logf = jnp.where(live, jnp.logaddexp(log_lb, log_1mlb + jax.nn.log_sigmoid(fr)), 0.0)
        k = 1.0 - jnp.exp(logf)
        slabs = [logf[0:nb]]
        for t in range(1, tc):
            slabs.append(slabs[-1] + logf[t * nb:(t + 1) * nb])
        g = jnp.concatenate(slabs, axis=0)
        g_last = jnp.concatenate([slabs[-1]] * tc, axis=0)
        g_rel = g - jnp.concatenate([slabs[half - 1]] * tc, axis=0)
        qa = (q * jnp.exp(jnp.minimum(g_rel, HGRN_EXP_CLAMP))).astype(BF16)
        kt = (k * jnp.exp(jnp.minimum(-g_rel, HGRN_EXP_CLAMP))).astype(BF16)
        qt = (q * jnp.exp(g)).astype(BF16)
        ks = (k * jnp.exp(g_last - g)).astype(BF16)
        dec = jnp.exp(slabs[-1])

        outs = []
        for h in range(nh):
            hs = slice(h * dk, (h + 1) * dk)
            qt_h, kt_h, ks_h, v_h = qt[:, hs], kt[:, hs], ks[:, hs], v[:, hs]
            att = lax.dot_general(qa[:, hs], kt_h, (((1,), (1,)), ((), ())), preferred_element_type=F32)
            att = (att * same_seq_causal).astype(BF16)
            o = _dot(att, v_h)
            zero = jnp.zeros_like(qt_h)
            expand = lambda x: jnp.concatenate([jnp.where(bid == b, x, zero) for b in range(nb)], axis=1)
            st_h = st_ref[h]
            o = o + lax.dot_general(expand(qt_h), st_h.astype(BF16), (((1,), (1,)), ((), ())),
                                    preferred_element_type=F32)
            dst = lax.dot_general(v_h, expand(ks_h), (((0,), (0,)), ((), ())), preferred_element_type=F32)
            dec_row = jnp.concatenate([dec[b:b + 1, hs] for b in range(nb)], axis=1)
            st_ref[h] = st_h * dec_row + dst
            outs.append(_rms(o, gn_ref[:, hs]))
        ob = jnp.where(live, jnp.concatenate(outs, axis=1) * jax.nn.silu(gb), 0.0)
        ob_ref[pl.ds(pl.multiple_of(c * m, m), m), :] = ob.astype(ob_ref.dtype)
        return carry

    lax.fori_loop(0, n_chunks, chunk, 0)

    @pl.when(i == pl.num_programs(1) - 1)
    def _():
        for b in range(nb):
            for h in range(nh):
                snew_ref[b, h] = st_ref[h, :, b * dk:(b + 1) * dk].T


def hgrn2(proj, s0, s0_layer, s_stack, n_layers, lb_raw, gnorm, layer, n_t, tb, nb):
    _, bsz, nh, dk, _ = s0.shape
    d = nh * dk
    off = proj.shape[-1] // d - 4
    nbb = bsz // nb
    n_tpad = proj.shape[0] // bsz
    nt = n_tpad // tb
    m = max(k for k in range(2 * nb, HGRN_ROWS + 1, 2 * nb) if (tb * nb) % k == 0)
    if nb == bsz:
        col = lambda k: pl.BlockSpec((tb * nb, d), lambda j, i, k=k: (i, k + off))
    else:
        proj = proj.reshape(n_tpad, bsz, proj.shape[1])
        col = lambda k: pl.BlockSpec((tb, nb, d), lambda j, i, k=k: (i, j, k + off))
    in_specs = [
        col(0), col(1), col(2), col(3),
        pl.BlockSpec((None, nb, nh, dk, dk), lambda j, i: (s0_layer, j, 0, 0, 0)),
        _full(lb_raw.shape),
        _full((1, d)),
        pl.BlockSpec(memory_space=pl.ANY),
    ]
    args = [proj, proj, proj, proj, s0, lb_raw, gnorm.reshape(1, d), s_stack]
    assert s_stack.shape == (n_layers, bsz, nh, dk, dk)
    return pl.pallas_call(
        functools.partial(_hgrn2_kernel, layer=layer, n_t=n_t, tb=tb, nb=nb, nh=nh, m=m),
        out_shape=(
            jax.ShapeDtypeStruct((n_tpad * bsz, d), BF16),
            jax.ShapeDtypeStruct((n_layers, bsz, nh, dk, dk), F32),
        ),
        grid=(nbb, nt),
        in_specs=in_specs,
        out_specs=(
            pl.BlockSpec((tb * nb, d), lambda j, i: (j * nt + i, 0)),
            pl.BlockSpec((None, nb, nh, dk, dk), lambda j, i: (layer, j, 0, 0, 0)),
        ),
        scratch_shapes=[pltpu.VMEM((nh, dk, nb * dk), F32)],
        input_output_aliases={len(args) - 1: 1},
        compiler_params=_params("arbitrary", "arbitrary"),
        name="hgrn2",
    )(*args)


def _s5_discretise(lr, li, ldt):
    dt = jnp.exp(ldt)
    mag = jnp.exp(lr * dt)
    return mag * jnp.cos(li * dt), mag * jnp.sin(li * dt)


def _s5_prep_kernel(lr_ref, li_ref, ldt_ref, lrc_ref, lic_ref, ldtc_ref, bre_ref, bim_ref,
                    ar_ref, ai_ref, ore_ref, oim_ref):
    ar_ref[...], ai_ref[...] = _s5_discretise(lr_ref[...], li_ref[...], ldt_ref[...])
    lr = lrc_ref[...]
    li = lic_ref[...]
    ar, ai = _s5_discretise(lr, li, ldtc_ref[...])
    den = lr * lr + li * li
    cr = ((ar - 1.0) * lr + ai * li) / den
    ci = (ai * lr - (ar - 1.0) * li) / den
    ore_ref[...] = cr * bre_ref[...] - ci * bim_ref[...]
    oim_ref[...] = cr * bim_ref[...] + ci * bre_ref[...]


def s5_prep(lam_re, lam_im, log_dt, b_re, b_im):
    g, p, c = b_re.shape
    dense = lambda a: a.reshape(-1, LANES)
    per_state = lambda a: dense(jnp.broadcast_to(a.reshape(g, -1, 1), (g, p, 1)))
    per_coef = lambda a: dense(jnp.broadcast_to(a.reshape(g, -1, 1), (g, p, c)))
    small = jax.ShapeDtypeStruct((g * p // LANES, LANES), F32)
    big = jax.ShapeDtypeStruct((g * p * c // LANES, LANES), F32)
    ar, ai, ore, oim = pl.pallas_call(
        _s5_prep_kernel, out_shape=(small, small, big, big), name="s5_prep",
    )(per_state(lam_re), per_state(lam_im), per_state(log_dt), per_coef(lam_re), per_coef(lam_im), per_coef(log_dt),
      dense(b_re), dense(b_im))
    return ar.reshape(g, p), ai.reshape(g, p), ore.reshape(g, p, c), oim.reshape(g, p, c)


def _s5_kernel(x_ref, g_ref, wb_ref, wc_ref, ar_ref, ai_ref, d_ref, h0r_ref, h0i_ref, *rest, n_t, tb, rb, m):
    n_cast = (len(rest) - 3) // 2
    z_ref, hr_ref, hi_ref = rest[n_cast:n_cast + 3]
    for src_ref, dst_ref in zip(rest[:n_cast], rest[n_cast + 3:]):
        dst_ref[...] = src_ref[...].astype(dst_ref.dtype)
    i = pl.program_id(0)
    nch = wb_ref.shape[0]
    cw = wb_ref.shape[1]
    sw = wb_ref.shape[2] // 2
    tsb = m // rb
    nsl = rb // SUBLANES
    n_sub = jnp.minimum(n_t - i * tb, tb) // tsb

    @pl.when(i == 0)
    def _():
        hr_ref[...] = h0r_ref[...]
        hi_ref[...] = h0i_ref[...]

    @pl.when(n_sub < tb // tsb)
    def _():
        z_ref[...] = jnp.zeros_like(z_ref)

    def sub_block(j, carry):
        r0 = pl.multiple_of(j * m, m)
        xn = _rms(x_ref[pl.ds(r0, m), :], g_ref[...])
        xnb = xn.astype(BF16)
        drive = [_dot(xnb[:, c * cw:(c + 1) * cw], wb_ref[c]) for c in range(nch)]
        for c in range(nch):
            cs = slice(c * cw, (c + 1) * cw)
            ss = slice(c * sw, (c + 1) * sw)
            u = xn[:, cs]
            bu = drive[c]
            ar = jnp.broadcast_to(ar_ref[:, ss], (SUBLANES, sw))
            ai = jnp.broadcast_to(ai_ref[:, ss], (SUBLANES, sw))
            out_r = [None] * (tsb * nsl)
            out_i = [None] * (tsb * nsl)
            for s in range(nsl):
                srow = slice(s * SUBLANES, (s + 1) * SUBLANES)
                hr, hi = hr_ref[srow, ss], hi_ref[srow, ss]
                for t in range(tsb):
                    lo = t * rb + s * SUBLANES
                    hr, hi = (ar * hr - ai * hi + bu[lo:lo + SUBLANES, :sw],
                              ar * hi + ai * hr + bu[lo:lo + SUBLANES, sw:])
                    out_r[t * nsl + s] = hr
                    out_i[t * nsl + s] = hi
                hr_ref[srow, ss] = hr
                hi_ref[srow, ss] = hi
            hcat = jnp.concatenate([jnp.concatenate(out_r, axis=0).astype(BF16),
                                    jnp.concatenate(out_i, axis=0).astype(BF16)], axis=1)
            y = _dot(hcat, wc_ref[c]) + d_ref[:, cs] * u
            z_ref[pl.ds(r0, m), cs] = jax.nn.gelu(y).astype(z_ref.dtype)
        return carry

    lax.fori_loop(0, n_sub, sub_block, 0)


def s5(x, n_t, g, wb, wc, li, a_re, a_im, dskip, h0r, h0i, tb, rb, cast=()):
    n_rows, d = x.shape
    ns = h0r.shape[1]
    rows = tb * rb
    steps = n_rows // rows
    cast_in, cast_out, cast_shapes = [], [], []
    for a in cast:
        _, r, c = a.shape
        blk = min(k for k in range(2 * SUBLANES, r + 1, 2 * SUBLANES) if r % k == 0 and k * steps >= r)
        last = r // blk - 1
        cast_in.append(pl.BlockSpec((None, blk, c), lambda i, last=last: (li, jnp.minimum(i, last), 0)))
        cast_out.append(pl.BlockSpec((blk, c), lambda i, last=last: (jnp.minimum(i, last), 0)))
        cast_shapes.append(jax.ShapeDtypeStruct((r, c), BF16))
    m = max(k for k in range(rb, S5_MAX_SUB_ROWS + 1, rb)
            if rows % k == 0 and n_t % (k // rb) == 0 and k % (2 * SUBLANES) == 0)
    return pl.pallas_call(
        functools.partial(_s5_kernel, n_t=n_t, tb=tb, rb=rb, m=m),
        out_shape=(
            jax.ShapeDtypeStruct((n_rows, d), BF16),
            jax.ShapeDtypeStruct((rb, ns), F32),
            jax.ShapeDtypeStruct((rb, ns), F32),
            *cast_shapes,
        ),
        grid=(steps,),
        in_specs=[
            pl.BlockSpec((rows, d), lambda i: (i, 0)),
            _full((1, d)),
            pl.BlockSpec((None,) + wb.shape[1:], lambda i: (li, 0, 0, 0)),
            pl.BlockSpec((None,) + wc.shape[1:], lambda i: (li, 0, 0, 0)),
            _full((1, ns)), _full((1, ns)), _full((1, d)),
            _full((rb, ns)), _full((rb, ns)),
            *cast_in,
        ],
        out_specs=(
            pl.BlockSpec((rows, d), lambda i: (i, 0)),
            _full((rb, ns)), _full((rb, ns)),
            *cast_out,
        ),
        compiler_params=_params("arbitrary"),
        name="s5",
    )(x, g.reshape(1, d), wb, wc, a_re, a_im, dskip.reshape(1, d), h0r, h0i, *cast)


def _s5_block_diag_in(b):
    g, p, c = b.shape
    gpc = V7X_MXU_DIM // c
    bt = b.transpose(0, 2, 1).reshape(g // gpc, gpc, c, p)
    out = jnp.einsum('ngcp,gh->ngchp', bt, jnp.eye(gpc, dtype=b.dtype))
    return out.reshape(g // gpc, gpc * c, gpc * p)


def _s5_block_diag_out(cm):
    g, c, p = cm.shape
    gpc = V7X_MXU_DIM // c
    ct = cm.transpose(0, 2, 1).reshape(g // gpc, gpc, p, c)
    out = jnp.einsum('ngpc,gh->ngphc', ct, jnp.eye(gpc, dtype=cm.dtype))
    return out.reshape(g // gpc, gpc * p, gpc * c)


def _moe_kernel(x_ref, g_ref, wrt_ref, wgu_ref, wdn_ref, o_ref, xn_ref, pos_ref, gate_ref, *, sub):
    e = pl.program_id(1)
    tm = x_ref.shape[0]
    ne = wrt_ref.shape[0]
    dff = wdn_ref.shape[1]

    @pl.when(e == 0)
    def _():
        x = x_ref[...]
        xn = _rms(x, g_ref[...])
        xn_hi = xn.astype(BF16)
        xn_ref[...] = xn_hi
        xn_lo = (xn - xn_hi.astype(F32)).astype(BF16)
        wr = wrt_ref[...]
        wr_hi = wr.astype(BF16)
        wr_lo = (wr - wr_hi.astype(F32)).astype(BF16)
        nt = lambda a, b: lax.dot_general(a, b, (((1,), (1,)), ((), ())), preferred_element_type=F32)
        logits = nt(wr_hi, xn_hi) + (nt(wr_hi, xn_lo) + nt(wr_lo, xn_hi))
        ex = jnp.exp(logits - jnp.max(logits, axis=0, keepdims=True))
        probs = ex / jnp.sum(ex, axis=0, keepdims=True)
        eid = lax.broadcasted_iota(jnp.int32, (ne, tm), 0).astype(F32)
        m1 = jnp.max(probs, axis=0, keepdims=True)
        i1 = jnp.min(jnp.where(probs == m1, eid, float(ne)), axis=0, keepdims=True)
        sel1 = eid == i1
        rest = jnp.where(sel1, -1.0, probs)
        m2 = jnp.max(rest, axis=0, keepdims=True)
        i2 = jnp.min(jnp.where(rest == m2, eid, float(ne)), axis=0, keepdims=True)
        sel2 = eid == i2
        den = m1 + m2
        gate_ref[...] = jnp.where(sel1, m1 / den, 0.0) + jnp.where(sel2, m2 / den, 0.0)
        chosen = jnp.where(sel1, 1.0, jnp.where(sel2, 1.0, 0.0))
        r = lax.broadcasted_iota(jnp.int32, (tm, tm), 0)
        c = lax.broadcasted_iota(jnp.int32, (tm, tm), 1)
        before = jnp.where(r < c, 1.0, 0.0).astype(BF16)
        rank = _dot(chosen.astype(BF16), before)
        pos_ref[...] = jnp.where(chosen > 0.0, rank, -1.0)
        o_ref[...] = x

    pos_e = pos_ref[pl.ds(e, 1), :]
    gate_e = gate_ref[pl.ds(e, 1), :]
    cnt = jnp.sum(jnp.where(pos_e >= 0.0, 1.0, 0.0)).astype(jnp.int32)

    def run_block(base, rows):
        slot = lax.broadcasted_iota(jnp.int32, (rows, tm), 0) + base
        hit = pos_e == slot.astype(F32)
        onehot = jnp.where(hit, 1.0, 0.0).astype(BF16)
        xs = _dot(onehot, xn_ref[...]).astype(BF16)
        gs = jnp.sum(jnp.where(hit, gate_e, 0.0), axis=1, keepdims=True)
        hgu = _dot(xs, wgu_ref[0])
        act = (jax.nn.silu(hgu[:, :dff]) * hgu[:, dff:]).astype(BF16)
        yb = (_dot(act, wdn_ref[0]) * gs).astype(BF16)
        o_ref[...] += lax.dot_general(onehot, yb, (((0,), (0,)), ((), ())), preferred_element_type=F32)

    def body(s, carry):
        run_block(s * sub, sub)
        return carry

    n_full = cnt // sub
    lax.fori_loop(0, n_full, body, 0)
    rem = cnt - n_full * sub
    lo = 0
    for rows in MOE_BLOCK_ROWS:
        pl.when((rem > lo) & (rem <= rows))(functools.partial(run_block, n_full * sub, rows))
        lo = rows


def moe_residual(x, g, w_router_t, w_gu, w_down, li, tm, sub):
    n, d = x.shape
    _, ne, _, dff2 = w_gu.shape
    assert MOE_BLOCK_ROWS[-1] == sub
    return pl.pallas_call(
        functools.partial(_moe_kernel, sub=sub),
        out_shape=jax.ShapeDtypeStruct((n, d), F32),
        grid=(n // tm, ne),
        in_specs=[
            pl.BlockSpec((tm, d), lambda i, e: (i, 0)),
            _full((1, d)),
            _full((ne, d)),
            pl.BlockSpec((None, 1, d, dff2), lambda i, e: (li, e, 0, 0)),
            pl.BlockSpec((None, 1, dff2 // 2, d), lambda i, e: (li, e, 0, 0)),
        ],
        out_specs=pl.BlockSpec((tm, d), lambda i, e: (i, 0)),
        scratch_shapes=[
            pltpu.VMEM((tm, d), BF16),
            pltpu.VMEM((ne, tm), F32),
            pltpu.VMEM((ne, tm), F32),
        ],
        compiler_params=_params("parallel", "arbitrary"),
        name="moe",
    )(x, g.reshape(1, d), w_router_t, w_gu, w_down)


def _trunk(x, n_t, rb, tb, nb, h0, conv0, s0, re0, im0, w, moe_bf16):
    n, d = x.shape
    depth = w['norm_mix'].shape[0]
    d_a = h0.shape[-1]
    nh, dk = s0.shape[2], s0.shape[3]
    d_b = nh * dk
    ns = re0.shape[-2] * re0.shape[-1]
    tm = tb * rb
    n_even, n_odd = (depth + 1) // 2, depth // 2
    new = {k: [] for k in ('h', 'conv', 're', 'im')}
    s_stack = jnp.zeros((n_even,) + s0.shape[1:], F32)
    for l in range(depth):
        li = l // 2
        if l % 2 == 0:
            proj = norm_matmul(x, w['norm_mix'][l], w['even_w_in'], li, tm, w['even_w_in'].shape[2])
            conv_tm = conv0[li].transpose(1, 0, 2).reshape((CONV_W - 1) * rb, d_a)
            ya, h_new, conv_new = rglru(proj, n_t, conv_tm, h0[li], w['rglru_conv_w'][li], w['rglru_conv_b'][li],
                                        w['rglru_wg'][li], w['rglru_bg'][li], w['rglru_lambda'][li], tb=tb, rb=rb)
            new['h'].append(h_new)
            new['conv'].append(conv_new.reshape(CONV_W - 1, rb, d_a).transpose(1, 0, 2))
            ob, s_stack = hgrn2(proj, s0, li % s0.shape[0], s_stack, n_even,
                                w['hgrn2_lb_raw'], w['hgrn2_gnorm'][li], li, n_t=n_t, tb=tb, nb=nb)
            if nb != rb:
                ob = ob.reshape(rb // nb, n // rb, nb, d_b).transpose(1, 0, 2, 3).reshape(n, d_b)
            x = mix_ffn(ya, ob, x, w['even_w_out'], w['norm_ffn'][l], w['ffn_w_gu'], w['ffn_w_down'], li, tm,
                        w['ffn_w_down'].shape[1] // 2)
        else:
            cast = () if li in moe_bf16 else (w['moe_w_gu'].reshape(n_odd, -1, w['moe_w_gu'].shape[-1]),
                                              w['moe_w_down'].reshape(n_odd, -1, d))
            z, re_new, im_new, *cast_out = s5(x, n_t, w['norm_mix'][l], w['s5_wb'], w['s5_wc'], li, w['s5_a_re'][li],
                                              w['s5_a_im'][li], w['s5_d'][li], re0[li].reshape(rb, ns),
                                              im0[li].reshape(rb, ns), tb=tb, rb=rb, cast=cast)
            if cast_out:
                moe_bf16[li] = tuple(a.reshape((1,) + f.shape[1:]) for a, f in
                                     zip(cast_out, (w['moe_w_gu'], w['moe_w_down'])))
            w_gu_bf, w_down_bf = moe_bf16[li]
            new['re'].append(re_new.reshape(re0.shape[1:]))
            new['im'].append(im_new.reshape(im0.shape[1:]))
            x = glu_residual(z, w['s5_w_glu'], li, x, tm, w['s5_w_glu'].shape[2] // 2)
            x = moe_residual(x, w['norm_ffn'][l], w['moe_w_router_t'][li], w_gu_bf, w_down_bf, 0,
                             tm, MOE_BLOCK_ROWS[-1])
    stack = lambda k: jnp.stack(new[k])
    return x, (stack('h'), stack('conv'), s_stack, stack('re'), stack('im'))


def kernel(x_prompt, x_sample, state_rglru_h, state_rglru_conv, state_hgrn2, state_s5_re, state_s5_im,
           meta_tokens, norm_mix, norm_ffn, norm_final, even_w_in, even_w_out,
           rglru_conv_w, rglru_conv_b, rglru_w_a, rglru_b_a, rglru_w_x, rglru_b_x, rglru_lambda,
           hgrn2_lb_raw, hgrn2_gnorm, s5_lam_re, s5_lam_im, s5_log_dt, s5_b_re, s5_b_im, s5_c_re, s5_c_im,
           s5_d, s5_w_glu, ffn_w_gu, ffn_w_down, moe_w_router, moe_w_gu, moe_w_down):
    bp, tp0, d = x_prompt.shape
    bs, ts, _ = x_sample.shape
    tp = tp0 + N_META
    d_a = state_rglru_h.shape[-1]
    n_even, n_odd = state_rglru_h.shape[0], state_s5_re.shape[0]
    assert bp == SUBLANES and bs % SUBLANES == 0
    tb_p = PROMPT_BLOCK_ROWS // bp
    tp_pad = -(-tp // tb_p) * tb_p
    nb_s = TIME_CHUNK_ROWS // ts
    assert (tp % (TIME_CHUNK_ROWS // bp) == 0 and tb_p % (TIME_CHUNK_ROWS // bp) == 0 and bs % nb_s == 0
            and nb_s * ts == TIME_CHUNK_ROWS and nb_s % SUBLANES == 0)

    eye_a = jnp.eye(H_A, dtype=F32)
    block_diag = lambda m: jnp.einsum('lhij,hg->lhigj', m, eye_a).reshape(n_even, d_a, d_a)
    s5_g, s5_p, s5_c = s5_b_re.shape[1:]
    lg = n_odd * s5_g
    a_re, a_im, bt_re, bt_im = s5_prep(s5_lam_re.reshape(lg, s5_p), s5_lam_im.reshape(lg, s5_p), s5_log_dt.reshape(lg),
                                       s5_b_re.reshape(lg, s5_p, s5_c), s5_b_im.reshape(lg, s5_p, s5_c))
    s5_wb = jnp.concatenate([_s5_block_diag_in(bt_re.reshape(lg, s5_p, s5_c)),
                             _s5_block_diag_in(bt_im.reshape(lg, s5_p, s5_c))], axis=2).astype(BF16)
    s5_wc = jnp.concatenate([_s5_block_diag_out(s5_c_re.reshape(lg, s5_c, s5_p)),
                             -_s5_block_diag_out(s5_c_im.reshape(lg, s5_c, s5_p))], axis=1).astype(BF16)
    ns = s5_lam_re.shape[1] * s5_lam_re.shape[2]
    w = {
        'norm_mix': norm_mix, 'norm_ffn': norm_ffn, 'norm_final': norm_final,
        'even_w_in': even_w_in.astype(BF16), 'even_w_out': even_w_out.astype(BF16),
        'rglru_conv_w': rglru_conv_w, 'rglru_conv_b': rglru_conv_b,
        'rglru_wg': jnp.concatenate([block_diag(rglru_w_a), block_diag(rglru_w_x)], axis=2).astype(BF16),
        'rglru_bg': jnp.concatenate([rglru_b_a, rglru_b_x], axis=1),
        'rglru_lambda': rglru_lambda, 'hgrn2_lb_raw': hgrn2_lb_raw, 'hgrn2_gnorm': hgrn2_gnorm,
        's5_a_re': a_re.reshape(n_odd, 1, ns), 's5_a_im': a_im.reshape(n_odd, 1, ns),
        's5_wb': s5_wb.reshape((n_odd, -1) + s5_wb.shape[1:]), 's5_wc': s5_wc.reshape((n_odd, -1) + s5_wc.shape[1:]),
        's5_d': s5_d, 's5_w_glu': s5_w_glu.astype(BF16),
        'ffn_w_gu': ffn_w_gu.astype(BF16), 'ffn_w_down': ffn_w_down.astype(BF16),
        'moe_w_router_t': moe_w_router.transpose(0, 2, 1),
        'moe_w_gu': moe_w_gu, 'moe_w_down': moe_w_down,
    }

    moe_bf16 = {}
    tc_p = TIME_CHUNK_ROWS // bp
    assert N_META == tc_p
    xm = to_time_major(x_prompt, meta_tokens.astype(x_prompt.dtype), tp_pad, tb_p // tc_p)
    zero = lambda ref, lead: jnp.zeros((lead, bp) + ref.shape[2:], ref.dtype)
    xp, p_new = _trunk(xm, tp, bp, tb_p, bp, zero(state_rglru_h, n_even), zero(state_rglru_conv, n_even),
                       zero(state_hgrn2, 1), zero(state_s5_re, n_odd), zero(state_s5_im, n_odd), w, moe_bf16)
    nk_out = max(k for k in range(1, TIME_CHUNK_ROWS // tc_p + 1) if tp0 % (k * tc_p) == 0)
    y_prompt = rmsnorm_batch_major(xp, norm_final, bp, N_META, tp0, tc_p, nk_out)

    xs = x_sample.transpose(1, 0, 2).reshape(ts * bs, d)
    xs, s_new = _trunk(xs, ts, bs, ts, nb_s, state_rglru_h, state_rglru_conv, state_hgrn2, state_s5_re, state_s5_im, w, moe_bf16)
    y_sample = rmsnorm_rows(xs, norm_final, ts * bs).reshape(ts, bs, d).transpose(1, 0, 2)

    refs = (state_rglru_h, state_rglru_conv, state_hgrn2, state_s5_re, state_s5_im)
    cast = lambda new: tuple(a.astype(r.dtype) for a, r in zip(new, refs))
    return (y_prompt, y_sample) + cast(p_new) + cast(s_new)
```

```python
import functools

import jax
import jax.numpy as jnp
from jax import lax
from jax.experimental import pallas as pl
from jax.experimental.pallas import tpu as pltpu

F32 = jnp.float32
BF16 = jnp.bfloat16

EPS = 1e-6
N_META = 16
CONV_W = 4
RG_C = 8.0
H_A = 8

V7X_VMEM_LIMIT_BYTES = 56 * 1024 * 1024
SUBLANES = 8
LANES = 128
V7X_MXU_DIM = 256
MOE_BLOCK_ROWS = (128, 160, 192, 224, 256, 288, 320)
TIME_CHUNK_ROWS = 128
HGRN_ROWS = 256
S5_MAX_SUB_ROWS = 384
HGRN_EXP_CLAMP = 80.0
PROMPT_BLOCK_ROWS = 768
SCAN_CARRY_ELEMS = 16 * SUBLANES * LANES


def _params(*sem):
    return pltpu.CompilerParams(dimension_semantics=sem, vmem_limit_bytes=V7X_VMEM_LIMIT_BYTES)


def _rms(x, g):
    ms = jnp.mean(x * x, axis=-1, keepdims=True)
    return x * lax.rsqrt(ms + EPS) * g


def _dot(a, b):
    return jnp.dot(a, b, preferred_element_type=F32)


def _full(shape):
    return pl.BlockSpec(shape, lambda *_: (0,) * len(shape))


def _ride_along_cast_specs(cast, layer, steps, step_axis):
    ins, outs, shapes = [], [], []
    for a in cast:
        _, r, c = a.shape
        blk = min(k for k in range(2 * SUBLANES, r + 1, 2 * SUBLANES) if r % k == 0 and k * steps >= r)
        last = r // blk - 1
        ins.append(pl.BlockSpec((None, blk, c), lambda *g, last=last: (layer, jnp.minimum(g[step_axis], last), 0)))
        outs.append(pl.BlockSpec((blk, c), lambda *g, last=last: (jnp.minimum(g[step_axis], last), 0)))
        shapes.append(jax.ShapeDtypeStruct((r, c), BF16))
    return ins, outs, shapes


def _ride_along_cast(src_refs, dst_refs):
    for src_ref, dst_ref in zip(src_refs, dst_refs):
        dst_ref[...] = src_ref[...].astype(dst_ref.dtype)


def _norm_matmul_kernel(x_ref, g_ref, w_ref, o_ref, xn_ref):
    @pl.when(pl.program_id(1) == 0)
    def _():
        xn_ref[...] = _rms(x_ref[...], g_ref[...]).astype(BF16)

    o_ref[...] = _dot(xn_ref[...], w_ref[...]).astype(o_ref.dtype)


def norm_matmul(x, g, w, li, tm, tn):
    n, d = x.shape
    nout = w.shape[2]
    return pl.pallas_call(
        _norm_matmul_kernel,
        out_shape=jax.ShapeDtypeStruct((n, nout), BF16),
        grid=(n // tm, nout // tn),
        in_specs=[
            pl.BlockSpec((tm, d), lambda i, j: (i, 0)),
            _full((1, d)),
            pl.BlockSpec((None, d, tn), lambda i, j: (li, 0, j)),
        ],
        out_specs=pl.BlockSpec((tm, tn), lambda i, j: (i, j)),
        scratch_shapes=[pltpu.VMEM((tm, d), BF16)],
        compiler_params=_params("parallel", "arbitrary"),
        name="norm_matmul",
    )(x, g.reshape(1, d), w)


def _mix_ffn_kernel(ya_ref, ob_ref, x_ref, wo_ref, g_ref, wg_ref, wu_ref, wdn_ref, o_ref, xn_ref):
    j = pl.program_id(1)
    d_a = ya_ref.shape[1]

    @pl.when(j == 0)
    def _():
        x1 = x_ref[...] + _dot(ya_ref[...], wo_ref[:d_a, :]) + _dot(ob_ref[...], wo_ref[d_a:, :])
        xn_ref[...] = _rms(x1, g_ref[...]).astype(BF16)
        o_ref[...] = x1

    xn = xn_ref[...]
    h = (jax.nn.silu(_dot(xn, wg_ref[...])) * _dot(xn, wu_ref[...])).astype(BF16)
    o_ref[...] += _dot(h, wdn_ref[...])


def mix_ffn(ya, ob, x, w_out, g, w_gu, w_down, li, tm, chunk):
    n, d = x.shape
    dff = w_down.shape[1]
    assert dff % chunk == 0 and chunk % LANES == 0
    nj = dff // chunk
    rows = lambda a: pl.BlockSpec((tm, a.shape[1]), lambda i, j: (i, 0))
    return pl.pallas_call(
        _mix_ffn_kernel,
        out_shape=jax.ShapeDtypeStruct((n, d), F32),
        grid=(n // tm, nj),
        in_specs=[
            rows(ya), rows(ob), rows(x),
            pl.BlockSpec((None,) + w_out.shape[1:], lambda i, j: (li, 0, 0)),
            _full((1, d)),
            pl.BlockSpec((None, d, chunk), lambda i, j: (li, 0, j)),
            pl.BlockSpec((None, d, chunk), lambda i, j: (li, 0, j + nj)),
            pl.BlockSpec((None, chunk, d), lambda i, j: (li, j, 0)),
        ],
        out_specs=pl.BlockSpec((tm, d), lambda i, j: (i, 0)),
        scratch_shapes=[pltpu.VMEM((tm, d), BF16)],
        compiler_params=_params("parallel", "arbitrary"),
        name="mix_ffn",
    )(ya, ob, x, w_out, g.reshape(1, d), w_gu, w_gu, w_down)


def _glu_residual_kernel(z_ref, wv_ref, wg_ref, r_ref, o_ref):
    z = z_ref[...]
    val = _dot(z, wv_ref[...])
    gate = _dot(z, wg_ref[...])
    o_ref[...] = r_ref[...] + val * jax.nn.sigmoid(gate)


def glu_residual(z, w_glu, li, res, tm, tn):
    n, d = z.shape
    dout = w_glu.shape[2] // 2
    nj = dout // tn
    return pl.pallas_call(
        _glu_residual_kernel,
        out_shape=jax.ShapeDtypeStruct((n, dout), F32),
        grid=(n // tm, nj),
        in_specs=[
            pl.BlockSpec((tm, d), lambda i, j: (i, 0)),
            pl.BlockSpec((None, d, tn), lambda i, j: (li, 0, j)),
            pl.BlockSpec((None, d, tn), lambda i, j: (li, 0, j + nj)),
            pl.BlockSpec((tm, tn), lambda i, j: (i, j)),
        ],
        out_specs=pl.BlockSpec((tm, tn), lambda i, j: (i, j)),
        compiler_params=_params("parallel", "arbitrary"),
        name="glu_residual",
    )(z, w_glu, w_glu, res)


def _rmsnorm_kernel(x_ref, g_ref, o_ref):
    o_ref[...] = _rms(x_ref[...], g_ref[...])


def rmsnorm_rows(x, g, tm):
    n, d = x.shape
    return pl.pallas_call(
        _rmsnorm_kernel,
        out_shape=jax.ShapeDtypeStruct((n, d), F32),
        grid=(n // tm,),
        in_specs=[pl.BlockSpec((tm, d), lambda i: (i, 0)), _full((1, d))],
        out_specs=pl.BlockSpec((tm, d), lambda i: (i, 0)),
        compiler_params=_params("parallel"),
        name="final_rmsnorm",
    )(x, g.reshape(1, d))


def _rmsnorm_batch_major_kernel(*refs, nk, tc, nb):
    x_refs, g_ref, o_ref = refs[:nk], refs[nk], refs[nk + 1]
    d = o_ref.shape[-1]
    for k in range(nk):
        y = _rms(x_refs[k][...], g_ref[...])
        o_ref[:, k * tc:(k + 1) * tc, :] = jnp.swapaxes(y.reshape(tc, nb, d), 0, 1)


def rmsnorm_batch_major(x, g, nb, t_skip, t_out, tc, nk):
    n, d = x.shape
    rows = tc * nb
    assert t_skip % tc == 0 and t_out % (nk * tc) == 0
    specs = [pl.BlockSpec((rows, d), lambda j, k=k: (nk * j + t_skip // tc + k, 0)) for k in range(nk)]
    return pl.pallas_call(
        functools.partial(_rmsnorm_batch_major_kernel, nk=nk, tc=tc, nb=nb),
        out_shape=jax.ShapeDtypeStruct((nb, t_out, d), F32),
        grid=(t_out // (nk * tc),),
        in_specs=specs + [_full((1, d))],
        out_specs=pl.BlockSpec((nb, nk * tc, d), lambda j: (0, j, 0)),
        compiler_params=_params("parallel"),
        name="final_rmsnorm_batch_major",
    )(*([x] * nk), g.reshape(1, d))


def _to_time_major_kernel(*refs, nk, n_chunks):
    x_refs, lead_ref, o_ref = refs[:nk], refs[nk], refs[nk + 1]
    i = pl.program_id(0)
    nb, tc, d = x_refs[0].shape
    rows = tc * nb
    for k in range(nk):
        chunk = i * nk + k - 1
        val = jnp.swapaxes(x_refs[k][...], 0, 1).reshape(rows, d)
        if k == 0:
            lead = jnp.broadcast_to(lead_ref[...][:, None, :], (tc, nb, d)).reshape(rows, d)
            val = jnp.where(i == 0, lead, val)
        o_ref[k * rows:(k + 1) * rows, :] = jnp.where(chunk < n_chunks, val, 0.0)


def to_time_major(x, lead, t_pad, nk):
    nb, t, d = x.shape
    tc = lead.shape[0]
    assert t % tc == 0 and t_pad % (nk * tc) == 0
    n_chunks = t // tc
    specs = [pl.BlockSpec((nb, tc, d), lambda i, k=k: (0, jnp.clip(i * nk + k - 1, 0, n_chunks - 1), 0))
             for k in range(nk)]
    return pl.pallas_call(
        functools.partial(_to_time_major_kernel, nk=nk, n_chunks=n_chunks),
        out_shape=jax.ShapeDtypeStruct((t_pad * nb, d), F32),
        grid=(t_pad // (nk * tc),),
        in_specs=specs + [_full((tc, d))],
        out_specs=pl.BlockSpec((nk * tc * nb, d), lambda i: (i, 0)),
        compiler_params=_params("parallel"),
        name="to_time_major",
    )(*([x] * nk), lead)


def _rglru_kernel(xa_ref, ga_ref, conv0_ref, h0_ref, cw_ref, cb_ref, wg_ref, bg_ref, lam_ref,
                  ya_ref, hlast_ref, convnew_ref, xpad_ref, a_ref, u_ref, *, n_t, tb, rb):
    i = pl.program_id(0)
    rows = tb * rb
    tail = (CONV_W - 1) * rb
    c = xa_ref.shape[-1]
    t_valid = jnp.minimum(n_t - i * tb, tb)

    @pl.when(i == 0)
    def _():
        xpad_ref[0:tail, :] = conv0_ref[...]
        hlast_ref[...] = h0_ref[...]

    @pl.when(i > 0)
    def _():
        xpad_ref[0:tail, :] = xpad_ref[rows:rows + tail, :]

    xpad_ref[tail:tail + rows, :] = xa_ref[...].astype(F32)
    xc = cb_ref[...]
    for k in range(CONV_W):
        xc = xc + cw_ref[k:k + 1, :] * xpad_ref[k * rb:k * rb + rows, :]

    gates = _dot(xc.astype(BF16), wg_ref[...]) + bg_ref[...]
    r = jax.nn.sigmoid(gates[:, :c])
    ig = jax.nn.sigmoid(gates[:, c:])
    log_a = (-RG_C) * r * jax.nn.softplus(-lam_ref[...])
    a = jnp.exp(log_a)
    y = 1.0 - a * a
    mult = y * lax.rsqrt(jnp.maximum(y, 1e-30))
    a_ref[...] = a
    u_ref[...] = mult * ig * xc

    lc = min(c, max(LANES, SCAN_CARRY_ELEMS // rb // LANES * LANES))
    for c0 in range(0, c, lc):
        def body(t, h, c0=c0):
            sl = pl.ds(pl.multiple_of(t * rb, rb), rb)
            h = a_ref[sl, c0:c0 + lc] * h + u_ref[sl, c0:c0 + lc]
            u_ref[sl, c0:c0 + lc] = h
            return h

        hlast_ref[:, c0:c0 + lc] = lax.fori_loop(0, t_valid, body, hlast_ref[:, c0:c0 + lc])

    live = lax.broadcasted_iota(jnp.int32, (rows, c), 0) < t_valid * rb
    ya_ref[...] = jnp.where(live, u_ref[...] * jax.nn.gelu(ga_ref[...].astype(F32)), 0.0).astype(ya_ref.dtype)
    convnew_ref[...] = xpad_ref[pl.ds(pl.multiple_of(t_valid * rb, rb), tail), :]


def rglru(proj, n_t, conv0, h0, cw, cb, wg, bg, lam, tb, rb):
    c = h0.shape[1]
    n_rows = proj.shape[0]
    rows = tb * rb
    tail = (CONV_W - 1) * rb
    return pl.pallas_call(
        functools.partial(_rglru_kernel, n_t=n_t, tb=tb, rb=rb),
        out_shape=(
            jax.ShapeDtypeStruct((n_rows, c), BF16),
            jax.ShapeDtypeStruct((rb, c), F32),
            jax.ShapeDtypeStruct((tail, c), F32),
        ),
        grid=(n_rows // rows,),
        in_specs=[
            pl.BlockSpec((rows, c), lambda i: (i, 0)),
            pl.BlockSpec((rows, c), lambda i: (i, 1)),
            _full((tail, c)), _full((rb, c)), _full((CONV_W, c)), _full((1, c)),
            _full((c, 2 * c)), _full((1, 2 * c)), _full((1, c)),
        ],
        out_specs=(
            pl.BlockSpec((rows, c), lambda i: (i, 0)),
            _full((rb, c)),
            _full((tail, c)),
        ),
        scratch_shapes=[
            pltpu.VMEM((rows + tail, c), F32),
            pltpu.VMEM((rows, c), F32),
            pltpu.VMEM((rows, c), F32),
        ],
        compiler_params=_params("arbitrary"),
        name="rglru",
    )(proj, proj, conv0, h0, cw, cb.reshape(1, c), wg, bg.reshape(1, 2 * c), lam.reshape(1, c))


def _hgrn2_kernel(q_ref, f_ref, v_ref, gb_ref, s0_ref, lbraw_ref, gn_ref, stack_ref, *rest,
                  layer, n_t, tb, nb, nh, m):
    del stack_ref
    n_cast = (len(rest) - 3) // 2
    ob_ref, snew_ref = rest[n_cast:n_cast + 2]
    st_ref = rest[-1]
    _ride_along_cast(rest[:n_cast], rest[n_cast + 2:-1])
    i = pl.program_id(1)
    tc = m // nb
    half = tc // 2
    d = q_ref.shape[-1]
    dk = d // nh
    t_valid = jnp.minimum(n_t - i * tb, tb)
    n_chunks = (t_valid + tc - 1) // tc

    @pl.when(i == 0)
    def _():
        for b in range(nb):
            for h in range(nh):
                st_ref[h, :, b * dk:(b + 1) * dk] = s0_ref[b, h].T

    @pl.when(n_chunks < tb // tc)
    def _():
        ob_ref[...] = jnp.zeros_like(ob_ref)

    p = jax.nn.softmax(lbraw_ref[...], axis=0)
    cum = p[0:1, :]
    for r in range(1, layer + 1):
        cum = cum + p[r:r + 1, :]
    lb = cum - p[0:1, :]
    log_lb = jnp.log(lb)
    log_1mlb = jnp.log1p(-lb)

    assert tc % 2 == 0 and nb & (nb - 1) == 0
    row = lax.broadcasted_iota(jnp.int32, (m, m), 0)
    col = lax.broadcasted_iota(jnp.int32, (m, m), 1)
    same_seq_causal = jnp.where(((row & (nb - 1)) == (col & (nb - 1))) & (col <= row), 1.0, 0.0)
    bid = lax.broadcasted_iota(jnp.int32, (m, dk), 0) & (nb - 1)

    def chunk(c, carry):
        def rows_of(ref):
            if len(ref.shape) == 2:
                return ref[pl.ds(pl.multiple_of(c * m, m), m), :]
            return ref[pl.ds(c * tc, tc)].reshape(m, d)

        live = lax.broadcasted_iota(jnp.int32, (m, d), 0) < (t_valid - c * tc) * nb
        q = jax.nn.silu(rows_of(q_ref).astype(F32))
        fr = rows_of(f_ref).astype(F32)
        v = rows_of(v_ref).astype(BF16)
        gb = rows_of(gb_ref).astype(F32)
        logf = jnp.where(live, jnp.logaddexp(log_lb, log_1mlb + jax.nn.log_sigmoid(fr)), 0.0)
        k = 1.0 - jnp.exp(logf)
        slabs = [logf[0:nb]]
        for t in range(1, tc):
            slabs.append(slabs[-1] + logf[t * nb:(t + 1) * nb])
        g = jnp.concatenate(slabs, axis=0)
        g_last = jnp.concatenate([slabs[-1]] * tc, axis=0)
        g_rel = g - jnp.concatenate([slabs[half - 1]] * tc, axis=0)
        qa = (q * jnp.exp(jnp.minimum(g_rel, HGRN_EXP_CLAMP))).astype(BF16)
        kt = (k * jnp.exp(jnp.minimum(-g_rel, HGRN_EXP_CLAMP))).astype(BF16)
        qt = (q * jnp.exp(g)).astype(BF16)
        ks = (k * jnp.exp(g_last - g)).astype(BF16)
        dec = jnp.exp(slabs[-1])

        outs = []
        for h in range(nh):
            hs = slice(h * dk, (h + 1) * dk)
            qt_h, kt_h, ks_h, v_h = qt[:, hs], kt[:, hs], ks[:, hs], v[:, hs]
            att = lax.dot_general(qa[:, hs], kt_h, (((1,), (1,)), ((), ())), preferred_element_type=F32)
            att = (att * same_seq_causal).astype(BF16)
            o = _dot(att, v_h)
            zero = jnp.zeros_like(qt_h)
            expand = lambda x: jnp.concatenate([jnp.where(bid == b, x, zero) for b in range(nb)], axis=1)
            st_h = st_ref[h]
            o = o + lax.dot_general(expand(qt_h), st_h.astype(BF16), (((1,), (1,)), ((), ())),
                                    preferred_element_type=F32)
            dst = lax.dot_general(v_h, expand(ks_h), (((0,), (0,)), ((), ())), preferred_element_type=F32)
            dec_row = jnp.concatenate([dec[b:b + 1, hs] for b in range(nb)], axis=1)
            st_ref[h] = st_h * dec_row + dst
            outs.append(_rms(o, gn_ref[:, hs]))
        ob = jnp.where(live, jnp.concatenate(outs, axis=1) * jax.nn.silu(gb), 0.0)
        ob_ref[pl.ds(pl.multiple_of(c * m, m), m), :] = ob.astype(ob_ref.dtype)
        return carry

    lax.fori_loop(0, n_chunks, chunk, 0)

    @pl.when(i == pl.num_programs(1) - 1)
    def _():
        for b in range(nb):
            for h in range(nh):
                snew_ref[b, h] = st_ref[h, :, b * dk:(b + 1) * dk].T


def hgrn2(proj, s0, s0_layer, s_stack, n_layers, lb_raw, gnorm, layer, n_t, tb, nb, cast=()):
    _, bsz, nh, dk, _ = s0.shape
    d = nh * dk
    off = proj.shape[-1] // d - 4
    nbb = bsz // nb
    n_tpad = proj.shape[0] // bsz
    nt = n_tpad // tb
    m = max(k for k in range(2 * nb, HGRN_ROWS + 1, 2 * nb) if (tb * nb) % k == 0)
    if nb == bsz:
        col = lambda k: pl.BlockSpec((tb * nb, d), lambda j, i, k=k: (i, k + off))
    else:
        proj = proj.reshape(n_tpad, bsz, proj.shape[1])
        col = lambda k: pl.BlockSpec((tb, nb, d), lambda j, i, k=k: (i, j, k + off))
    in_specs = [
        col(0), col(1), col(2), col(3),
        pl.BlockSpec((None, nb, nh, dk, dk), lambda j, i: (s0_layer, j, 0, 0, 0)),
        _full(lb_raw.shape),
        _full((1, d)),
        pl.BlockSpec(memory_space=pl.ANY),
    ]
    args = [proj, proj, proj, proj, s0, lb_raw, gnorm.reshape(1, d), s_stack]
    assert s_stack.shape == (n_layers, bsz, nh, dk, dk) and (not cast or nbb == 1)
    stack_arg = len(args) - 1
    cast_in, cast_out, cast_shapes = _ride_along_cast_specs(cast, layer, nt, 1)
    return pl.pallas_call(
        functools.partial(_hgrn2_kernel, layer=layer, n_t=n_t, tb=tb, nb=nb, nh=nh, m=m),
        out_shape=(
            jax.ShapeDtypeStruct((n_tpad * bsz, d), BF16),
            jax.ShapeDtypeStruct((n_layers, bsz, nh, dk, dk), F32),
            *cast_shapes,
        ),
        grid=(nbb, nt),
        in_specs=in_specs + cast_in,
        out_specs=(
            pl.BlockSpec((tb * nb, d), lambda j, i: (j * nt + i, 0)),
            pl.BlockSpec((None, nb, nh, dk, dk), lambda j, i: (layer, j, 0, 0, 0)),
            *cast_out,
        ),
        scratch_shapes=[pltpu.VMEM((nh, dk, nb * dk), F32)],
        input_output_aliases={stack_arg: 1},
        compiler_params=_params("arbitrary", "arbitrary"),
        name="hgrn2",
    )(*args, *cast)


def _s5_discretise(lr, li, ldt):
    dt = jnp.exp(ldt)
    mag = jnp.exp(lr * dt)
    return mag * jnp.cos(li * dt), mag * jnp.sin(li * dt)


def _s5_prep_kernel(lr_ref, li_ref, ldt_ref, lrc_ref, lic_ref, ldtc_ref, bre_ref, bim_ref,
                    ar_ref, ai_ref, ore_ref, oim_ref):
    ar_ref[...], ai_ref[...] = _s5_discretise(lr_ref[...], li_ref[...], ldt_ref[...])
    lr = lrc_ref[...]
    li = lic_ref[...]
    ar, ai = _s5_discretise(lr, li, ldtc_ref[...])
    den = lr * lr + li * li
    cr = ((ar - 1.0) * lr + ai * li) / den
    ci = (ai * lr - (ar - 1.0) * li) / den
    ore_ref[...] = cr * bre_ref[...] - ci * bim_ref[...]
    oim_ref[...] = cr * bim_ref[...] + ci * bre_ref[...]


def s5_prep(lam_re, lam_im, log_dt, b_re, b_im):
    g, p, c = b_re.shape
    dense = lambda a: a.reshape(-1, LANES)
    per_state = lambda a: dense(jnp.broadcast_to(a.reshape(g, -1, 1), (g, p, 1)))
    per_coef = lambda a: dense(jnp.broadcast_to(a.reshape(g, -1, 1), (g, p, c)))
    small = jax.ShapeDtypeStruct((g * p // LANES, LANES), F32)
    big = jax.ShapeDtypeStruct((g * p * c // LANES, LANES), F32)
    ar, ai, ore, oim = pl.pallas_call(
        _s5_prep_kernel, out_shape=(small, small, big, big), name="s5_prep",
    )(per_state(lam_re), per_state(lam_im), per_state(log_dt), per_coef(lam_re), per_coef(lam_im), per_coef(log_dt),
      dense(b_re), dense(b_im))
    return ar.reshape(g, p), ai.reshape(g, p), ore.reshape(g, p, c), oim.reshape(g, p, c)


def _s5_kernel(x_ref, g_ref, wb_ref, wc_ref, ar_ref, ai_ref, d_ref, h0r_ref, h0i_ref, *rest, n_t, tb, rb, m):
    n_cast = (len(rest) - 3) // 2
    z_ref, hr_ref, hi_ref = rest[n_cast:n_cast + 3]
    _ride_along_cast(rest[:n_cast], rest[n_cast + 3:])
    i = pl.program_id(0)
    nch = wb_ref.shape[0]
    cw = wb_ref.shape[1]
    sw = wb_ref.shape[2] // 2
    tsb = m // rb
    nsl = rb // SUBLANES
    n_sub = jnp.minimum(n_t - i * tb, tb) // tsb

    @pl.when(i == 0)
    def _():
        hr_ref[...] = h0r_ref[...]
        hi_ref[...] = h0i_ref[...]

    @pl.when(n_sub < tb // tsb)
    def _():
        z_ref[...] = jnp.zeros_like(z_ref)

    def sub_block(j, carry):
        r0 = pl.multiple_of(j * m, m)
        xn = _rms(x_ref[pl.ds(r0, m), :], g_ref[...])
        xnb = xn.astype(BF16)
        drive = [_dot(xnb[:, c * cw:(c + 1) * cw], wb_ref[c]) for c in range(nch)]
        for c in range(nch):
            cs = slice(c * cw, (c + 1) * cw)
            ss = slice(c * sw, (c + 1) * sw)
            u = xn[:, cs]
            bu = drive[c]
            ar = jnp.broadcast_to(ar_ref[:, ss], (SUBLANES, sw))
            ai = jnp.broadcast_to(ai_ref[:, ss], (SUBLANES, sw))
            out_r = [None] * (tsb * nsl)
            out_i = [None] * (tsb * nsl)
            for s in range(nsl):
                srow = slice(s * SUBLANES, (s + 1) * SUBLANES)
                hr, hi = hr_ref[srow, ss], hi_ref[srow, ss]
                for t in range(tsb):
                    lo = t * rb + s * SUBLANES
                    hr, hi = (ar * hr - ai * hi + bu[lo:lo + SUBLANES, :sw],
                              ar * hi + ai * hr + bu[lo:lo + SUBLANES, sw:])
                    out_r[t * nsl + s] = hr
                    out_i[t * nsl + s] = hi
                hr_ref[srow, ss] = hr
                hi_ref[srow, ss] = hi
            hcat = jnp.concatenate([jnp.concatenate(out_r, axis=0).astype(BF16),
                                    jnp.concatenate(out_i, axis=0).astype(BF16)], axis=1)
            y = _dot(hcat, wc_ref[c]) + d_ref[:, cs] * u
            z_ref[pl.ds(r0, m), cs] = jax.nn.gelu(y).astype(z_ref.dtype)
        return carry

    lax.fori_loop(0, n_sub, sub_block, 0)


def s5(x, n_t, g, wb, wc, li, a_re, a_im, dskip, h0r, h0i, tb, rb, cast=()):
    n_rows, d = x.shape
    ns = h0r.shape[1]
    rows = tb * rb
    steps = n_rows // rows
    cast_in, cast_out, cast_shapes = _ride_along_cast_specs(cast, li, steps, 0)
    m = max(k for k in range(rb, S5_MAX_SUB_ROWS + 1, rb)
            if rows % k == 0 and n_t % (k // rb) == 0 and k % (2 * SUBLANES) == 0)
    return pl.pallas_call(
        functools.partial(_s5_kernel, n_t=n_t, tb=tb, rb=rb, m=m),
        out_shape=(
            jax.ShapeDtypeStruct((n_rows, d), BF16),
            jax.ShapeDtypeStruct((rb, ns), F32),
            jax.ShapeDtypeStruct((rb, ns), F32),
            *cast_shapes,
        ),
        grid=(steps,),
        in_specs=[
            pl.BlockSpec((rows, d), lambda i: (i, 0)),
            _full((1, d)),
            pl.BlockSpec((None,) + wb.shape[1:], lambda i: (li, 0, 0, 0)),
            pl.BlockSpec((None,) + wc.shape[1:], lambda i: (li, 0, 0, 0)),
            _full((1, ns)), _full((1, ns)), _full((1, d)),
            _full((rb, ns)), _full((rb, ns)),
            *cast_in,
        ],
        out_specs=(
            pl.BlockSpec((rows, d), lambda i: (i, 0)),
            _full((rb, ns)), _full((rb, ns)),
            *cast_out,
        ),
        compiler_params=_params("arbitrary"),
        name="s5",
    )(x, g.reshape(1, d), wb, wc, a_re, a_im, dskip.reshape(1, d), h0r, h0i, *cast)


def _s5_block_diag_in(b):
    g, p, c = b.shape
    gpc = V7X_MXU_DIM // c
    bt = b.transpose(0, 2, 1).reshape(g // gpc, gpc, c, p)
    out = jnp.einsum('ngcp,gh->ngchp', bt, jnp.eye(gpc, dtype=b.dtype))
    return out.reshape(g // gpc, gpc * c, gpc * p)


def _s5_block_diag_out(cm):
    g, c, p = cm.shape
    gpc = V7X_MXU_DIM // c
    ct = cm.transpose(0, 2, 1).reshape(g // gpc, gpc, p, c)
    out = jnp.einsum('ngpc,gh->ngphc', ct, jnp.eye(gpc, dtype=cm.dtype))
    return out.reshape(g // gpc, gpc * p, gpc * c)


def _moe_kernel(x_ref, g_ref, wrt_ref, wgu_ref, wdn_ref, o_ref, xn_ref, pos_ref, gate_ref, *, sub):
    e = pl.program_id(1)
    tm = x_ref.shape[0]
    ne = wrt_ref.shape[0]
    dff = wdn_ref.shape[1]

    @pl.when(e == 0)
    def _():
        x = x_ref[...]
        xn = _rms(x, g_ref[...])
        xn_hi = xn.astype(BF16)
        xn_ref[...] = xn_hi
        xn_lo = (xn - xn_hi.astype(F32)).astype(BF16)
        wr = wrt_ref[...]
        wr_hi = wr.astype(BF16)
        wr_lo = (wr - wr_hi.astype(F32)).astype(BF16)
        nt = lambda a, b: lax.dot_general(a, b, (((1,), (1,)), ((), ())), preferred_element_type=F32)
        logits = nt(wr_hi, xn_hi) + (nt(wr_hi, xn_lo) + nt(wr_lo, xn_hi))
        ex = jnp.exp(logits - jnp.max(logits, axis=0, keepdims=True))
        probs = ex / jnp.sum(ex, axis=0, keepdims=True)
        eid = lax.broadcasted_iota(jnp.int32, (ne, tm), 0).astype(F32)
        m1 = jnp.max(probs, axis=0, keepdims=True)
        i1 = jnp.min(jnp.where(probs == m1, eid, float(ne)), axis=0, keepdims=True)
        sel1 = eid == i1
        rest = jnp.where(sel1, -1.0, probs)
        m2 = jnp.max(rest, axis=0, keepdims=True)
        i2 = jnp.min(jnp.where(rest == m2, eid, float(ne)), axis=0, keepdims=True)
        sel2 = eid == i2
        den = m1 + m2
        gate_ref[...] = jnp.where(sel1, m1 / den, 0.0) + jnp.where(sel2, m2 / den, 0.0)
        chosen = jnp.where(sel1, 1.0, jnp.where(sel2, 1.0, 0.0))
        r = lax.broadcasted_iota(jnp.int32, (tm, tm), 0)
        c = lax.broadcasted_iota(jnp.int32, (tm, tm), 1)
        before = jnp.where(r < c, 1.0, 0.0).astype(BF16)
        rank = _dot(chosen.astype(BF16), before)
        pos_ref[...] = jnp.where(chosen > 0.0, rank, -1.0)
        o_ref[...] = x

    pos_e = pos_ref[pl.ds(e, 1), :]
    gate_e = gate_ref[pl.ds(e, 1), :]
    cnt = jnp.sum(jnp.where(pos_e >= 0.0, 1.0, 0.0)).astype(jnp.int32)

    def run_block(base, rows):
        slot = lax.broadcasted_iota(jnp.int32, (rows, tm), 0) + base
        hit = pos_e == slot.astype(F32)
        onehot = jnp.where(hit, 1.0, 0.0).astype(BF16)
        xs = _dot(onehot, xn_ref[...]).astype(BF16)
        gs = jnp.sum(jnp.where(hit, gate_e, 0.0), axis=1, keepdims=True)
        hgu = _dot(xs, wgu_ref[0])
        act = (jax.nn.silu(hgu[:, :dff]) * hgu[:, dff:]).astype(BF16)
        yb = (_dot(act, wdn_ref[0]) * gs).astype(BF16)
        o_ref[...] += lax.dot_general(onehot, yb, (((0,), (0,)), ((), ())), preferred_element_type=F32)

    def body(s, carry):
        run_block(s * sub, sub)
        return carry

    n_full = cnt // sub
    lax.fori_loop(0, n_full, body, 0)
    rem = cnt - n_full * sub
    lo = 0
    for rows in MOE_BLOCK_ROWS:
        pl.when((rem > lo) & (rem <= rows))(functools.partial(run_block, n_full * sub, rows))
        lo = rows


def moe_residual(x, g, w_router_t, w_gu, w_down, li, tm, sub):
    n, d = x.shape
    _, ne, _, dff2 = w_gu.shape
    assert MOE_BLOCK_ROWS[-1] == sub
    return pl.pallas_call(
        functools.partial(_moe_kernel, sub=sub),
        out_shape=jax.ShapeDtypeStruct((n, d), F32),
        grid=(n // tm, ne),
        in_specs=[
            pl.BlockSpec((tm, d), lambda i, e: (i, 0)),
            _full((1, d)),
            _full((ne, d)),
            pl.BlockSpec((None, 1, d, dff2), lambda i, e: (li, e, 0, 0)),
            pl.BlockSpec((None, 1, dff2 // 2, d), lambda i, e: (li, e, 0, 0)),
        ],
        out_specs=pl.BlockSpec((tm, d), lambda i, e: (i, 0)),
        scratch_shapes=[
            pltpu.VMEM((tm, d), BF16),
            pltpu.VMEM((ne, tm), F32),
            pltpu.VMEM((ne, tm), F32),
        ],
        compiler_params=_params("parallel", "arbitrary"),
        name="moe",
    )(x, g.reshape(1, d), w_router_t, w_gu, w_down)


def _trunk(x, n_t, rb, tb, nb, h0, conv0, s0, re0, im0, w, moe_bf16):
    n, d = x.shape
    depth = w['norm_mix'].shape[0]
    d_a = h0.shape[-1]
    nh, dk = s0.shape[2], s0.shape[3]
    d_b = nh * dk
    ns = re0.shape[-2] * re0.shape[-1]
    tm = tb * rb
    n_even, n_odd = (depth + 1) // 2, depth // 2
    new = {k: [] for k in ('h', 'conv', 're', 'im')}
    s_stack = jnp.zeros((n_even,) + s0.shape[1:], F32)
    for l in range(depth):
        li = l // 2
        if l % 2 == 0:
            proj = norm_matmul(x, w['norm_mix'][l], w['even_w_in'], li, tm, w['even_w_in'].shape[2])
            conv_tm = conv0[li].transpose(1, 0, 2).reshape((CONV_W - 1) * rb, d_a)
            ya, h_new, conv_new = rglru(proj, n_t, conv_tm, h0[li], w['rglru_conv_w'][li], w['rglru_conv_b'][li],
                                        w['rglru_wg'][li], w['rglru_bg'][li], w['rglru_lambda'][li], tb=tb, rb=rb)
            new['h'].append(h_new)
            new['conv'].append(conv_new.reshape(CONV_W - 1, rb, d_a).transpose(1, 0, 2))
            cast = () if ('even', li) in moe_bf16 else (w['even_w_out'], w['ffn_w_gu'], w['ffn_w_down'])
            ob, s_stack, *cast_out = hgrn2(proj, s0, li % s0.shape[0], s_stack, n_even, w['hgrn2_lb_raw'],
                                           w['hgrn2_gnorm'][li], li, n_t=n_t, tb=tb, nb=nb, cast=cast)
            if cast_out:
                moe_bf16[('even', li)] = tuple(a[None] for a in cast_out)
            w_out_bf, w_gu_bf, w_down_bf = moe_bf16[('even', li)]
            if nb != rb:
                ob = ob.reshape(rb // nb, n // rb, nb, d_b).transpose(1, 0, 2, 3).reshape(n, d_b)
            x = mix_ffn(ya, ob, x, w_out_bf, w['norm_ffn'][l], w_gu_bf, w_down_bf, 0, tm, w_down_bf.shape[1] // 2)
        else:
            cast = () if li in moe_bf16 else (w['moe_w_gu'].reshape(n_odd, -1, w['moe_w_gu'].shape[-1]),
                                              w['moe_w_down'].reshape(n_odd, -1, d))
            z, re_new, im_new, *cast_out = s5(x, n_t, w['norm_mix'][l], w['s5_wb'], w['s5_wc'], li, w['s5_a_re'][li],
                                              w['s5_a_im'][li], w['s5_d'][li], re0[li].reshape(rb, ns),
                                              im0[li].reshape(rb, ns), tb=tb, rb=rb, cast=cast)
            if cast_out:
                moe_bf16[li] = tuple(a.reshape((1,) + f.shape[1:]) for a, f in
                                     zip(cast_out, (w['moe_w_gu'], w['moe_w_down'])))
            w_gu_bf, w_down_bf = moe_bf16[li]
            new['re'].append(re_new.reshape(re0.shape[1:]))
            new['im'].append(im_new.reshape(im0.shape[1:]))
            x = glu_residual(z, w['s5_w_glu'], li, x, tm, w['s5_w_glu'].shape[2] // 2)
            x = moe_residual(x, w['norm_ffn'][l], w['moe_w_router_t'][li], w_gu_bf, w_down_bf, 0,
                             tm, MOE_BLOCK_ROWS[-1])
    stack = lambda k: jnp.stack(new[k])
    return x, (stack('h'), stack('conv'), s_stack, stack('re'), stack('im'))


def kernel(x_prompt, x_sample, state_rglru_h, state_rglru_conv, state_hgrn2, state_s5_re, state_s5_im,
           meta_tokens, norm_mix, norm_ffn, norm_final, even_w_in, even_w_out,
           rglru_conv_w, rglru_conv_b, rglru_w_a, rglru_b_a, rglru_w_x, rglru_b_x, rglru_lambda,
           hgrn2_lb_raw, hgrn2_gnorm, s5_lam_re, s5_lam_im, s5_log_dt, s5_b_re, s5_b_im, s5_c_re, s5_c_im,
           s5_d, s5_w_glu, ffn_w_gu, ffn_w_down, moe_w_router, moe_w_gu, moe_w_down):
    bp, tp0, d = x_prompt.shape
    bs, ts, _ = x_sample.shape
    tp = tp0 + N_META
    d_a = state_rglru_h.shape[-1]
    n_even, n_odd = state_rglru_h.shape[0], state_s5_re.shape[0]
    assert bp == SUBLANES and bs % SUBLANES == 0
    tb_p = PROMPT_BLOCK_ROWS // bp
    tp_pad = -(-tp // tb_p) * tb_p
    nb_s = TIME_CHUNK_ROWS // ts
    assert (tp % (TIME_CHUNK_ROWS // bp) == 0 and tb_p % (TIME_CHUNK_ROWS // bp) == 0 and bs % nb_s == 0
            and nb_s * ts == TIME_CHUNK_ROWS and nb_s % SUBLANES == 0)

    eye_a = jnp.eye(H_A, dtype=F32)
    block_diag = lambda m: jnp.einsum('lhij,hg->lhigj', m, eye_a).reshape(n_even, d_a, d_a)
    s5_g, s5_p, s5_c = s5_b_re.shape[1:]
    lg = n_odd * s5_g
    a_re, a_im, bt_re, bt_im = s5_prep(s5_lam_re.reshape(lg, s5_p), s5_lam_im.reshape(lg, s5_p), s5_log_dt.reshape(lg),
                                       s5_b_re.reshape(lg, s5_p, s5_c), s5_b_im.reshape(lg, s5_p, s5_c))
    s5_wb = jnp.concatenate([_s5_block_diag_in(bt_re.reshape(lg, s5_p, s5_c)),
                             _s5_block_diag_in(bt_im.reshape(lg, s5_p, s5_c))], axis=2).astype(BF16)
    s5_wc = jnp.concatenate([_s5_block_diag_out(s5_c_re.reshape(lg, s5_c, s5_p)),
                             -_s5_block_diag_out(s5_c_im.reshape(lg, s5_c, s5_p))], axis=1).astype(BF16)
    ns = s5_lam_re.shape[1] * s5_lam_re.shape[2]
    w = {
        'norm_mix': norm_mix, 'norm_ffn': norm_ffn, 'norm_final': norm_final,
        'even_w_in': even_w_in.astype(BF16), 'even_w_out': even_w_out,
        'rglru_conv_w': rglru_conv_w, 'rglru_conv_b': rglru_conv_b,
        'rglru_wg': jnp.concatenate([block_diag(rglru_w_a), block_diag(rglru_w_x)], axis=2).astype(BF16),
        'rglru_bg': jnp.concatenate([rglru_b_a, rglru_b_x], axis=1),
        'rglru_lambda': rglru_lambda, 'hgrn2_lb_raw': hgrn2_lb_raw, 'hgrn2_gnorm': hgrn2_gnorm,
        's5_a_re': a_re.reshape(n_odd, 1, ns), 's5_a_im': a_im.reshape(n_odd, 1, ns),
        's5_wb': s5_wb.reshape((n_odd, -1) + s5_wb.shape[1:]), 's5_wc': s5_wc.reshape((n_odd, -1) + s5_wc.shape[1:]),
        's5_d': s5_d, 's5_w_glu': s5_w_glu.astype(BF16),
        'ffn_w_gu': ffn_w_gu, 'ffn_w_down': ffn_w_down,
        'moe_w_router_t': moe_w_router.transpose(0, 2, 1),
        'moe_w_gu': moe_w_gu, 'moe_w_down': moe_w_down,
    }

    moe_bf16 = {}
    tc_p = TIME_CHUNK_ROWS // bp
    assert N_META == tc_p
    xm = to_time_major(x_prompt, meta_tokens.astype(x_prompt.dtype), tp_pad, tb_p // tc_p)
    zero = lambda ref, lead: jnp.zeros((lead, bp) + ref.shape[2:], ref.dtype)
    xp, p_new = _trunk(xm, tp, bp, tb_p, bp, zero(state_rglru_h, n_even), zero(state_rglru_conv, n_even),
                       zero(state_hgrn2, 1), zero(state_s5_re, n_odd), zero(state_s5_im, n_odd), w, moe_bf16)
    nk_out = max(k for k in range(1, TIME_CHUNK_ROWS // tc_p + 1) if tp0 % (k * tc_p) == 0)
    y_prompt = rmsnorm_batch_major(xp, norm_final, bp, N_META, tp0, tc_p, nk_out)

    xs = x_sample.transpose(1, 0, 2).reshape(ts * bs, d)
    xs, s_new = _trunk(xs, ts, bs, ts, nb_s, state_rglru_h, state_rglru_conv, state_hgrn2, state_s5_re, state_s5_im, w, moe_bf16)
    y_sample = rmsnorm_rows(xs, norm_final, ts * bs).reshape(ts, bs, d).transpose(1, 0, 2)

    refs = (state_rglru_h, state_rglru_conv, state_hgrn2, state_s5_re, state_s5_im)
    cast = lambda new: tuple(a.astype(r.dtype) for a, r in zip(new, refs))
    return (y_prompt, y_sample) + cast(p_new) + cast(s_new)
```

```python
import functools

import jax
import jax.numpy as jnp
from jax import lax
from jax.experimental import pallas as pl
from jax.experimental.pallas import tpu as pltpu

F32 = jnp.float32
BF16 = jnp.bfloat16

EPS = 1e-6
N_META = 16
CONV_W = 4
RG_C = 8.0
H_A = 8

V7X_VMEM_LIMIT_BYTES = 56 * 1024 * 1024
SUBLANES = 8
LANES = 128
V7X_MXU_DIM = 256
MOE_BLOCK_ROWS = (128, 160, 192, 224, 256, 288, 320)
TIME_CHUNK_ROWS = 128
HGRN_ROWS = 256
S5_MAX_SUB_ROWS = 384
HGRN_EXP_CLAMP = 80.0
PROMPT_BLOCK_ROWS = 768
SCAN_CARRY_ELEMS = 16 * SUBLANES * LANES


def _params(*sem):
    return pltpu.CompilerParams(dimension_semantics=sem, vmem_limit_bytes=V7X_VMEM_LIMIT_BYTES)


def _rms(x, g):
    ms = jnp.mean(x * x, axis=-1, keepdims=True)
    return x * lax.rsqrt(ms + EPS) * g


def _dot(a, b):
    return jnp.dot(a, b, preferred_element_type=F32)


def _full(shape):
    return pl.BlockSpec(shape, lambda *_: (0,) * len(shape))


def _ride_along_cast_specs(cast, layer, steps, step_axis):
    ins, outs, shapes = [], [], []
    for a in cast:
        _, r, c = a.shape
        blk = min(k for k in range(2 * SUBLANES, r + 1, 2 * SUBLANES) if r % k == 0 and k * steps >= r)
        last = r // blk - 1
        ins.append(pl.BlockSpec((None, blk, c), lambda *g, last=last: (layer, jnp.minimum(g[step_axis], last), 0)))
        outs.append(pl.BlockSpec((blk, c), lambda *g, last=last: (jnp.minimum(g[step_axis], last), 0)))
        shapes.append(jax.ShapeDtypeStruct((r, c), BF16))
    return ins, outs, shapes


def _ride_along_cast(src_refs, dst_refs):
    for src_ref, dst_ref in zip(src_refs, dst_refs):
        dst_ref[...] = src_ref[...].astype(dst_ref.dtype)


def _norm_matmul_kernel(x_ref, g_ref, w_ref, o_ref, xn_ref):
    @pl.when(pl.program_id(1) == 0)
    def _():
        xn_ref[...] = _rms(x_ref[...], g_ref[...]).astype(BF16)

    o_ref[...] = _dot(xn_ref[...], w_ref[...]).astype(o_ref.dtype)


def norm_matmul(x, g, w, li, tm, tn):
    n, d = x.shape
    nout = w.shape[2]
    return pl.pallas_call(
        _norm_matmul_kernel,
        out_shape=jax.ShapeDtypeStruct((n, nout), BF16),
        grid=(n // tm, nout // tn),
        in_specs=[
            pl.BlockSpec((tm, d), lambda i, j: (i, 0)),
            _full((1, d)),
            pl.BlockSpec((None, d, tn), lambda i, j: (li, 0, j)),
        ],
        out_specs=pl.BlockSpec((tm, tn), lambda i, j: (i, j)),
        scratch_shapes=[pltpu.VMEM((tm, d), BF16)],
        compiler_params=_params("parallel", "arbitrary"),
        name="norm_matmul",
    )(x, g.reshape(1, d), w)


def _mix_ffn_kernel(ya_ref, ob_ref, x_ref, wo_ref, g_ref, wg_ref, wu_ref, wdn_ref, o_ref, xn_ref):
    j = pl.program_id(1)
    d_a = ya_ref.shape[1]

    @pl.when(j == 0)
    def _():
        x1 = x_ref[...] + _dot(ya_ref[...], wo_ref[:d_a, :]) + _dot(ob_ref[...], wo_ref[d_a:, :])
        xn_ref[...] = _rms(x1, g_ref[...]).astype(BF16)
        o_ref[...] = x1

    xn = xn_ref[...]
    h = (jax.nn.silu(_dot(xn, wg_ref[...])) * _dot(xn, wu_ref[...])).astype(BF16)
    o_ref[...] += _dot(h, wdn_ref[...])


def mix_ffn(ya, ob, x, w_out, g, w_gu, w_down, li, tm, chunk):
    n, d = x.shape
    dff = w_down.shape[1]
    assert dff % chunk == 0 and chunk % LANES == 0
    nj = dff // chunk
    rows = lambda a: pl.BlockSpec((tm, a.shape[1]), lambda i, j: (i, 0))
    return pl.pallas_call(
        _mix_ffn_kernel,
        out_shape=jax.ShapeDtypeStruct((n, d), F32),
        grid=(n // tm, nj),
        in_specs=[
            rows(ya), rows(ob), rows(x),
            pl.BlockSpec((None,) + w_out.shape[1:], lambda i, j: (li, 0, 0)),
            _full((1, d)),
            pl.BlockSpec((None, d, chunk), lambda i, j: (li, 0, j)),
            pl.BlockSpec((None, d, chunk), lambda i, j: (li, 0, j + nj)),
            pl.BlockSpec((None, chunk, d), lambda i, j: (li, j, 0)),
        ],
        out_specs=pl.BlockSpec((tm, d), lambda i, j: (i, 0)),
        scratch_shapes=[pltpu.VMEM((tm, d), BF16)],
        compiler_params=_params("parallel", "arbitrary"),
        name="mix_ffn",
    )(ya, ob, x, w_out, g.reshape(1, d), w_gu, w_gu, w_down)


def _glu_residual_kernel(z_ref, wv_ref, wg_ref, r_ref, o_ref):
    z = z_ref[...]
    val = _dot(z, wv_ref[...])
    gate = _dot(z, wg_ref[...])
    o_ref[...] = r_ref[...] + val * jax.nn.sigmoid(gate)


def glu_residual(z, w_glu, li, res, tm, tn):
    n, d = z.shape
    dout = w_glu.shape[2] // 2
    nj = dout // tn
    return pl.pallas_call(
        _glu_residual_kernel,
        out_shape=jax.ShapeDtypeStruct((n, dout), F32),
        grid=(n // tm, nj),
        in_specs=[
            pl.BlockSpec((tm, d), lambda i, j: (i, 0)),
            pl.BlockSpec((None, d, tn), lambda i, j: (li, 0, j)),
            pl.BlockSpec((None, d, tn), lambda i, j: (li, 0, j + nj)),
            pl.BlockSpec((tm, tn), lambda i, j: (i, j)),
        ],
        out_specs=pl.BlockSpec((tm, tn), lambda i, j: (i, j)),
        compiler_params=_params("parallel", "arbitrary"),
        name="glu_residual",
    )(z, w_glu, w_glu, res)


def _rmsnorm_kernel(x_ref, g_ref, o_ref):
    o_ref[...] = _rms(x_ref[...], g_ref[...])


def rmsnorm_rows(x, g, tm):
    n, d = x.shape
    return pl.pallas_call(
        _rmsnorm_kernel,
        out_shape=jax.ShapeDtypeStruct((n, d), F32),
        grid=(n // tm,),
        in_specs=[pl.BlockSpec((tm, d), lambda i: (i, 0)), _full((1, d))],
        out_specs=pl.BlockSpec((tm, d), lambda i: (i, 0)),
        compiler_params=_params("parallel"),
        name="final_rmsnorm",
    )(x, g.reshape(1, d))


def _rmsnorm_batch_major_kernel(*refs, nk, tc, nb):
    x_refs, g_ref, o_ref = refs[:nk], refs[nk], refs[nk + 1]
    d = o_ref.shape[-1]
    for k in range(nk):
        y = _rms(x_refs[k][...], g_ref[...])
        o_ref[:, k * tc:(k + 1) * tc, :] = jnp.swapaxes(y.reshape(tc, nb, d), 0, 1)


def rmsnorm_batch_major(x, g, nb, t_skip, t_out, tc, nk):
    n, d = x.shape
    rows = tc * nb
    assert t_skip % tc == 0 and t_out % (nk * tc) == 0
    specs = [pl.BlockSpec((rows, d), lambda j, k=k: (nk * j + t_skip // tc + k, 0)) for k in range(nk)]
    return pl.pallas_call(
        functools.partial(_rmsnorm_batch_major_kernel, nk=nk, tc=tc, nb=nb),
        out_shape=jax.ShapeDtypeStruct((nb, t_out, d), F32),
        grid=(t_out // (nk * tc),),
        in_specs=specs + [_full((1, d))],
        out_specs=pl.BlockSpec((nb, nk * tc, d), lambda j: (0, j, 0)),
        compiler_params=_params("parallel"),
        name="final_rmsnorm_batch_major",
    )(*([x] * nk), g.reshape(1, d))


def _to_time_major_kernel(*refs, nk, n_chunks):
    x_refs, lead_ref, o_ref = refs[:nk], refs[nk], refs[nk + 1]
    i = pl.program_id(0)
    nb, tc, d = x_refs[0].shape
    rows = tc * nb
    for k in range(nk):
        chunk = i * nk + k - 1
        val = jnp.swapaxes(x_refs[k][...], 0, 1).reshape(rows, d)
        if k == 0:
            lead = jnp.broadcast_to(lead_ref[...][:, None, :], (tc, nb, d)).reshape(rows, d)
            val = jnp.where(i == 0, lead, val)
        o_ref[k * rows:(k + 1) * rows, :] = jnp.where(chunk < n_chunks, val, 0.0)


def to_time_major(x, lead, t_pad, nk):
    nb, t, d = x.shape
    tc = lead.shape[0]
    assert t % tc == 0 and t_pad % (nk * tc) == 0
    n_chunks = t // tc
    specs = [pl.BlockSpec((nb, tc, d), lambda i, k=k: (0, jnp.clip(i * nk + k - 1, 0, n_chunks - 1), 0))
             for k in range(nk)]
    return pl.pallas_call(
        functools.partial(_to_time_major_kernel, nk=nk, n_chunks=n_chunks),
        out_shape=jax.ShapeDtypeStruct((t_pad * nb, d), F32),
        grid=(t_pad // (nk * tc),),
        in_specs=specs + [_full((tc, d))],
        out_specs=pl.BlockSpec((nk * tc * nb, d), lambda i: (i, 0)),
        compiler_params=_params("parallel"),
        name="to_time_major",
    )(*([x] * nk), lead)


def _rglru_kernel(xa_ref, ga_ref, conv0_ref, h0_ref, cw_ref, cb_ref, wg_ref, bg_ref, lam_ref,
                  ya_ref, hlast_ref, convnew_ref, xpad_ref, a_ref, u_ref, *, n_t, tb, rb):
    i = pl.program_id(0)
    rows = tb * rb
    tail = (CONV_W - 1) * rb
    c = xa_ref.shape[-1]
    t_valid = jnp.minimum(n_t - i * tb, tb)

    @pl.when(i == 0)
    def _():
        xpad_ref[0:tail, :] = conv0_ref[...]
        hlast_ref[...] = h0_ref[...]

    @pl.when(i > 0)
    def _():
        xpad_ref[0:tail, :] = xpad_ref[rows:rows + tail, :]

    xpad_ref[tail:tail + rows, :] = xa_ref[...].astype(F32)
    xc = cb_ref[...]
    for k in range(CONV_W):
        xc = xc + cw_ref[k:k + 1, :] * xpad_ref[k * rb:k * rb + rows, :]

    gates = _dot(xc.astype(BF16), wg_ref[...]) + bg_ref[...]
    r = jax.nn.sigmoid(gates[:, :c])
    ig = jax.nn.sigmoid(gates[:, c:])
    log_a = (-RG_C) * r * jax.nn.softplus(-lam_ref[...])
    a = jnp.exp(log_a)
    y = 1.0 - a * a
    mult = y * lax.rsqrt(jnp.maximum(y, 1e-30))
    a_ref[...] = a
    u_ref[...] = mult * ig * xc

    lc = min(c, max(LANES, SCAN_CARRY_ELEMS // rb // LANES * LANES))
    for c0 in range(0, c, lc):
        def body(t, h, c0=c0):
            sl = pl.ds(pl.multiple_of(t * rb, rb), rb)
            h = a_ref[sl, c0:c0 + lc] * h + u_ref[sl, c0:c0 + lc]
            u_ref[sl, c0:c0 + lc] = h
            return h

        hlast_ref[:, c0:c0 + lc] = lax.fori_loop(0, t_valid, body, hlast_ref[:, c0:c0 + lc])

    live = lax.broadcasted_iota(jnp.int32, (rows, c), 0) < t_valid * rb
    ya_ref[...] = jnp.where(live, u_ref[...] * jax.nn.gelu(ga_ref[...].astype(F32)), 0.0).astype(ya_ref.dtype)
    convnew_ref[...] = xpad_ref[pl.ds(pl.multiple_of(t_valid * rb, rb), tail), :]


def rglru(proj, n_t, conv0, h0, cw, cb, wg, bg, lam, tb, rb):
    c = h0.shape[1]
    n_rows = proj.shape[0]
    rows = tb * rb
    tail = (CONV_W - 1) * rb
    return pl.pallas_call(
        functools.partial(_rglru_kernel, n_t=n_t, tb=tb, rb=rb),
        out_shape=(
            jax.ShapeDtypeStruct((n_rows, c), BF16),
            jax.ShapeDtypeStruct((rb, c), F32),
            jax.ShapeDtypeStruct((tail, c), F32),
        ),
        grid=(n_rows // rows,),
        in_specs=[
            pl.BlockSpec((rows, c), lambda i: (i, 0)),
            pl.BlockSpec((rows, c), lambda i: (i, 1)),
            _full((tail, c)), _full((rb, c)), _full((CONV_W, c)), _full((1, c)),
            _full((c, 2 * c)), _full((1, 2 * c)), _full((1, c)),
        ],
        out_specs=(
            pl.BlockSpec((rows, c), lambda i: (i, 0)),
            _full((rb, c)),
            _full((tail, c)),
        ),
        scratch_shapes=[
            pltpu.VMEM((rows + tail, c), F32),
            pltpu.VMEM((rows, c), F32),
            pltpu.VMEM((rows, c), F32),
        ],
        compiler_params=_params("arbitrary"),
        name="rglru",
    )(proj, proj, conv0, h0, cw, cb.reshape(1, c), wg, bg.reshape(1, 2 * c), lam.reshape(1, c))


def _hgrn2_kernel(q_ref, f_ref, v_ref, gb_ref, s0_ref, lbraw_ref, gn_ref, stack_ref, *rest,
                  layer, n_t, tb, nb, nh, m):
    del stack_ref
    n_cast = (len(rest) - 3) // 2
    ob_ref, snew_ref = rest[n_cast:n_cast + 2]
    st_ref = rest[-1]
    _ride_along_cast(rest[:n_cast], rest[n_cast + 2:-1])
    i = pl.program_id(1)
    tc = m // nb
    half = tc // 2
    d = q_ref.shape[-1]
    dk = d // nh
    t_valid = jnp.minimum(n_t - i * tb, tb)
    n_chunks = (t_valid + tc - 1) // tc

    @pl.when(i == 0)
    def _():
        for b in range(nb):
            for h in range(nh):
                st_ref[h, :, b * dk:(b + 1) * dk] = s0_ref[b, h].T

    @pl.when(n_chunks < tb // tc)
    def _():
        ob_ref[...] = jnp.zeros_like(ob_ref)

    p = jax.nn.softmax(lbraw_ref[...], axis=0)
    cum = p[0:1, :]
    for r in range(1, layer + 1):
        cum = cum + p[r:r + 1, :]
    lb = cum - p[0:1, :]
    log_lb = jnp.log(lb)
    log_1mlb = jnp.log1p(-lb)

    assert tc % 2 == 0 and nb & (nb - 1) == 0
    row = lax.broadcasted_iota(jnp.int32, (m, m), 0)
    col = lax.broadcasted_iota(jnp.int32, (m, m), 1)
    same_seq_causal = jnp.where(((row & (nb - 1)) == (col & (nb - 1))) & (col <= row), 1.0, 0.0)
    bid = lax.broadcasted_iota(jnp.int32, (m, dk), 0) & (nb - 1)

    def chunk(c, carry):
        def rows_of(ref):
            if len(ref.shape) == 2:
                return ref[pl.ds(pl.multiple_of(c * m, m), m), :]
            return ref[pl.ds(c * tc, tc)].reshape(m, d)

        live = lax.broadcasted_iota(jnp.int32, (m, d), 0) < (t_valid - c * tc) * nb
        q = jax.nn.silu(rows_of(q_ref).astype(F32))
        fr = rows_of(f_ref).astype(F32)
        v = rows_of(v_ref).astype(BF16)
        gb = rows_of(gb_ref).astype(F32)
        logf = jnp.where(live, jnp.logaddexp(log_lb, log_1mlb + jax.nn.log_sigmoid(fr)), 0.0)
        k = 1.0 - jnp.exp(logf)
        slabs = [logf[0:nb]]
        for t in range(1, tc):
            slabs.append(slabs[-1] + logf[t * nb:(t + 1) * nb])
        g = jnp.concatenate(slabs, axis=0)
        g_last = jnp.concatenate([slabs[-1]] * tc, axis=0)
        g_rel = g - jnp.concatenate([slabs[half - 1]] * tc, axis=0)
        qa = (q * jnp.exp(jnp.minimum(g_rel, HGRN_EXP_CLAMP))).astype(BF16)
        kt = (k * jnp.exp(jnp.minimum(-g_rel, HGRN_EXP_CLAMP))).astype(BF16)
        qt = (q * jnp.exp(g)).astype(BF16)
        ks = (k * jnp.exp(g_last - g)).astype(BF16)
        dec = jnp.exp(slabs[-1])

        outs = []
        for h in range(nh):
            hs = slice(h * dk, (h + 1) * dk)
            qt_h, kt_h, ks_h, v_h = qt[:, hs], kt[:, hs], ks[:, hs], v[:, hs]
            att = lax.dot_general(qa[:, hs], kt_h, (((1,), (1,)), ((), ())), preferred_element_type=F32)
            att = (att * same_seq_causal).astype(BF16)
            o = _dot(att, v_h)
            zero = jnp.zeros_like(qt_h)
            expand = lambda x: jnp.concatenate([jnp.where(bid == b, x, zero) for b in range(nb)], axis=1)
            st_h = st_ref[h]
            o = o + lax.dot_general(expand(qt_h), st_h.astype(BF16), (((1,), (1,)), ((), ())),
                                    preferred_element_type=F32)
            dst = lax.dot_general(v_h, expand(ks_h), (((0,), (0,)), ((), ())), preferred_element_type=F32)
            dec_row = jnp.concatenate([dec[b:b + 1, hs] for b in range(nb)], axis=1)
            st_ref[h] = st_h * dec_row + dst
            outs.append(_rms(o, gn_ref[:, hs]))
        ob = jnp.where(live, jnp.concatenate(outs, axis=1) * jax.nn.silu(gb), 0.0)
        ob_ref[pl.ds(pl.multiple_of(c * m, m), m), :] = ob.astype(ob_ref.dtype)
        return carry

    lax.fori_loop(0, n_chunks, chunk, 0)

    @pl.when(i == pl.num_programs(1) - 1)
    def _():
        for b in range(nb):
            for h in range(nh):
                snew_ref[b, h] = st_ref[h, :, b * dk:(b + 1) * dk].T


def hgrn2(proj, s0, s0_layer, s_stack, n_layers, lb_raw, gnorm, layer, n_t, tb, nb, cast=()):
    _, bsz, nh, dk, _ = s0.shape
    d = nh * dk
    off = proj.shape[-1] // d - 4
    nbb = bsz // nb
    n_tpad = proj.shape[0] // bsz
    nt = n_tpad // tb
    m = max(k for k in range(2 * nb, HGRN_ROWS + 1, 2 * nb) if (tb * nb) % k == 0)
    if nb == bsz:
        col = lambda k: pl.BlockSpec((tb * nb, d), lambda j, i, k=k: (i, k + off))
    else:
        proj = proj.reshape(n_tpad, bsz, proj.shape[1])
        col = lambda k: pl.BlockSpec((tb, nb, d), lambda j, i, k=k: (i, j, k + off))
    in_specs = [
        col(0), col(1), col(2), col(3),
        pl.BlockSpec((None, nb, nh, dk, dk), lambda j, i: (s0_layer, j, 0, 0, 0)),
        _full(lb_raw.shape),
        _full((1, d)),
        pl.BlockSpec(memory_space=pl.ANY),
    ]
    args = [proj, proj, proj, proj, s0, lb_raw, gnorm.reshape(1, d), s_stack]
    assert s_stack.shape == (n_layers, bsz, nh, dk, dk) and (not cast or nbb == 1)
    stack_arg = len(args) - 1
    cast_in, cast_out, cast_shapes = _ride_along_cast_specs(cast, layer, nt, 1)
    return pl.pallas_call(
        functools.partial(_hgrn2_kernel, layer=layer, n_t=n_t, tb=tb, nb=nb, nh=nh, m=m),
        out_shape=(
            jax.ShapeDtypeStruct((n_tpad * bsz, d), BF16),
            jax.ShapeDtypeStruct((n_layers, bsz, nh, dk, dk), F32),
            *cast_shapes,
        ),
        grid=(nbb, nt),
        in_specs=in_specs + cast_in,
        out_specs=(
            pl.BlockSpec((tb * nb, d), lambda j, i: (j * nt + i, 0)),
            pl.BlockSpec((None, nb, nh, dk, dk), lambda j, i: (layer, j, 0, 0, 0)),
            *cast_out,
        ),
        scratch_shapes=[pltpu.VMEM((nh, dk, nb * dk), F32)],
        input_output_aliases={stack_arg: 1},
        compiler_params=_params("arbitrary", "arbitrary"),
        name="hgrn2",
    )(*args, *cast)


def _s5_discretise(lr, li, ldt):
    dt = jnp.exp(ldt)
    mag = jnp.exp(lr * dt)
    return mag * jnp.cos(li * dt), mag * jnp.sin(li * dt)


def _s5_prep_kernel(lr_ref, li_ref, ldt_ref, lrc_ref, lic_ref, ldtc_ref, bre_ref, bim_ref,
                    ar_ref, ai_ref, ore_ref, oim_ref):
    ar_ref[...], ai_ref[...] = _s5_discretise(lr_ref[...], li_ref[...], ldt_ref[...])
    lr = lrc_ref[...]
    li = lic_ref[...]
    ar, ai = _s5_discretise(lr, li, ldtc_ref[...])
    den = lr * lr + li * li
    cr = ((ar - 1.0) * lr + ai * li) / den
    ci = (ai * lr - (ar - 1.0) * li) / den
    ore_ref[...] = cr * bre_ref[...] - ci * bim_ref[...]
    oim_ref[...] = cr * bim_ref[...] + ci * bre_ref[...]


def s5_prep(lam_re, lam_im, log_dt, b_re, b_im):
    g, p, c = b_re.shape
    dense = lambda a: a.reshape(-1, LANES)
    per_state = lambda a: dense(jnp.broadcast_to(a.reshape(g, -1, 1), (g, p, 1)))
    per_coef = lambda a: dense(jnp.broadcast_to(a.reshape(g, -1, 1), (g, p, c)))
    small = jax.ShapeDtypeStruct((g * p // LANES, LANES), F32)
    big = jax.ShapeDtypeStruct((g * p * c // LANES, LANES), F32)
    ar, ai, ore, oim = pl.pallas_call(
        _s5_prep_kernel, out_shape=(small, small, big, big), name="s5_prep",
    )(per_state(lam_re), per_state(lam_im), per_state(log_dt), per_coef(lam_re), per_coef(lam_im), per_coef(log_dt),
      dense(b_re), dense(b_im))
    return ar.reshape(g, p), ai.reshape(g, p), ore.reshape(g, p, c), oim.reshape(g, p, c)


def _s5_kernel(x_ref, g_ref, wb_ref, wc_ref, ar_ref, ai_ref, d_ref, h0r_ref, h0i_ref, *rest, n_t, tb, rb, m):
    n_cast = (len(rest) - 3) // 2
    z_ref, hr_ref, hi_ref = rest[n_cast:n_cast + 3]
    _ride_along_cast(rest[:n_cast], rest[n_cast + 3:])
    i = pl.program_id(0)
    nch = wb_ref.shape[0]
    cw = wb_ref.shape[1]
    sw = wb_ref.shape[2] // 2
    tsb = m // rb
    nsl = rb // SUBLANES
    n_sub = jnp.minimum(n_t - i * tb, tb) // tsb

    @pl.when(i == 0)
    def _():
        hr_ref[...] = h0r_ref[...]
        hi_ref[...] = h0i_ref[...]

    @pl.when(n_sub < tb // tsb)
    def _():
        z_ref[...] = jnp.zeros_like(z_ref)

    def sub_block(j, carry):
        r0 = pl.multiple_of(j * m, m)
        xn = _rms(x_ref[pl.ds(r0, m), :], g_ref[...])
        xnb = xn.astype(BF16)
        drive = [_dot(xnb[:, c * cw:(c + 1) * cw], wb_ref[c]) for c in range(nch)]
        for c in range(nch):
            cs = slice(c * cw, (c + 1) * cw)
            ss = slice(c * sw, (c + 1) * sw)
            u = xn[:, cs]
            bu = drive[c]
            ar = jnp.broadcast_to(ar_ref[:, ss], (SUBLANES, sw))
            ai = jnp.broadcast_to(ai_ref[:, ss], (SUBLANES, sw))
            out_r = [None] * (tsb * nsl)
            out_i = [None] * (tsb * nsl)
            for s in range(nsl):
                srow = slice(s * SUBLANES, (s + 1) * SUBLANES)
                hr, hi = hr_ref[srow, ss], hi_ref[srow, ss]
                for t in range(tsb):
                    lo = t * rb + s * SUBLANES
                    hr, hi = (ar * hr - ai * hi + bu[lo:lo + SUBLANES, :sw],
                              ar * hi + ai * hr + bu[lo:lo + SUBLANES, sw:])
                    out_r[t * nsl + s] = hr
                    out_i[t * nsl + s] = hi
                hr_ref[srow, ss] = hr
                hi_ref[srow, ss] = hi
            hcat = jnp.concatenate([jnp.concatenate(out_r, axis=0).astype(BF16),
                                    jnp.concatenate(out_i, axis=0).astype(BF16)], axis=1)
            y = _dot(hcat, wc_ref[c]) + d_ref[:, cs] * u
            z_ref[pl.ds(r0, m), cs] = jax.nn.gelu(y).astype(z_ref.dtype)
        return carry

    lax.fori_loop(0, n_sub, sub_block, 0)


def s5(x, n_t, g, wb, wc, li, a_re, a_im, dskip, h0r, h0i, tb, rb, cast=()):
    n_rows, d = x.shape
    ns = h0r.shape[1]
    rows = tb * rb
    steps = n_rows // rows
    cast_in, cast_out, cast_shapes = _ride_along_cast_specs(cast, li, steps, 0)
    m = max(k for k in range(rb, S5_MAX_SUB_ROWS + 1, rb)
            if rows % k == 0 and n_t % (k // rb) == 0 and k % (2 * SUBLANES) == 0)
    return pl.pallas_call(
        functools.partial(_s5_kernel, n_t=n_t, tb=tb, rb=rb, m=m),
        out_shape=(
            jax.ShapeDtypeStruct((n_rows, d), BF16),
            jax.ShapeDtypeStruct((rb, ns), F32),
            jax.ShapeDtypeStruct((rb, ns), F32),
            *cast_shapes,
        ),
        grid=(steps,),
        in_specs=[
            pl.BlockSpec((rows, d), lambda i: (i, 0)),
            _full((1, d)),
            pl.BlockSpec((None,) + wb.shape[1:], lambda i: (li, 0, 0, 0)),
            pl.BlockSpec((None,) + wc.shape[1:], lambda i: (li, 0, 0, 0)),
            _full((1, ns)), _full((1, ns)), _full((1, d)),
            _full((rb, ns)), _full((rb, ns)),
            *cast_in,
        ],
        out_specs=(
            pl.BlockSpec((rows, d), lambda i: (i, 0)),
            _full((rb, ns)), _full((rb, ns)),
            *cast_out,
        ),
        compiler_params=_params("arbitrary"),
        name="s5",
    )(x, g.reshape(1, d), wb, wc, a_re, a_im, dskip.reshape(1, d), h0r, h0i, *cast)


def _s5_block_diag_in(b):
    g, p, c = b.shape
    gpc = V7X_MXU_DIM // c
    bt = b.transpose(0, 2, 1).reshape(g // gpc, gpc, c, p)
    out = jnp.einsum('ngcp,gh->ngchp', bt, jnp.eye(gpc, dtype=b.dtype))
    return out.reshape(g // gpc, gpc * c, gpc * p)


def _s5_block_diag_out(cm):
    g, c, p = cm.shape
    gpc = V7X_MXU_DIM // c
    ct = cm.transpose(0, 2, 1).reshape(g // gpc, gpc, p, c)
    out = jnp.einsum('ngpc,gh->ngphc', ct, jnp.eye(gpc, dtype=cm.dtype))
    return out.reshape(g // gpc, gpc * p, gpc * c)


def _moe_kernel(x_ref, g_ref, wrt_ref, wgu_ref, wdn_ref, o_ref, xn_ref, pos_ref, gate_ref, *, sub):
    e = pl.program_id(1)
    tm = x_ref.shape[0]
    ne = wrt_ref.shape[0]
    dff = wdn_ref.shape[1]

    @pl.when(e == 0)
    def _():
        x = x_ref[...]
        xn = _rms(x, g_ref[...])
        xn_hi = xn.astype(BF16)
        xn_ref[...] = xn_hi
        xn_lo = (xn - xn_hi.astype(F32)).astype(BF16)
        wr = wrt_ref[...]
        wr_hi = wr.astype(BF16)
        wr_lo = (wr - wr_hi.astype(F32)).astype(BF16)
        nt = lambda a, b: lax.dot_general(a, b, (((1,), (1,)), ((), ())), preferred_element_type=F32)
        logits = nt(wr_hi, xn_hi) + (nt(wr_hi, xn_lo) + nt(wr_lo, xn_hi))
        ex = jnp.exp(logits - jnp.max(logits, axis=0, keepdims=True))
        probs = ex / jnp.sum(ex, axis=0, keepdims=True)
        eid = lax.broadcasted_iota(jnp.int32, (ne, tm), 0).astype(F32)
        m1 = jnp.max(probs, axis=0, keepdims=True)
        i1 = jnp.min(jnp.where(probs == m1, eid, float(ne)), axis=0, keepdims=True)
        sel1 = eid == i1
        rest = jnp.where(sel1, -1.0, probs)
        m2 = jnp.max(rest, axis=0, keepdims=True)
        i2 = jnp.min(jnp.where(rest == m2, eid, float(ne)), axis=0, keepdims=True)
        sel2 = eid == i2
        den = m1 + m2
        gate_ref[...] = jnp.where(sel1, m1 / den, 0.0) + jnp.where(sel2, m2 / den, 0.0)
        chosen = jnp.where(sel1, 1.0, jnp.where(sel2, 1.0, 0.0))
        r = lax.broadcasted_iota(jnp.int32, (tm, tm), 0)
        c = lax.broadcasted_iota(jnp.int32, (tm, tm), 1)
        before = jnp.where(r < c, 1.0, 0.0).astype(BF16)
        rank = _dot(chosen.astype(BF16), before)
        pos_ref[...] = jnp.where(chosen > 0.0, rank, -1.0)
        o_ref[...] = x

    pos_e = pos_ref[pl.ds(e, 1), :]
    gate_e = gate_ref[pl.ds(e, 1), :]
    cnt = jnp.sum(jnp.where(pos_e >= 0.0, 1.0, 0.0)).astype(jnp.int32)

    def run_block(base, rows):
        slot = lax.broadcasted_iota(jnp.int32, (rows, tm), 0) + base
        hit = pos_e == slot.astype(F32)
        onehot = jnp.where(hit, 1.0, 0.0).astype(BF16)
        xs = _dot(onehot, xn_ref[...]).astype(BF16)
        gs = jnp.sum(jnp.where(hit, gate_e, 0.0), axis=1, keepdims=True)
        hgu = _dot(xs, wgu_ref[0])
        act = (jax.nn.silu(hgu[:, :dff]) * hgu[:, dff:]).astype(BF16)
        yb = (_dot(act, wdn_ref[0]) * gs).astype(BF16)
        o_ref[...] += lax.dot_general(onehot, yb, (((0,), (0,)), ((), ())), preferred_element_type=F32)

    def body(s, carry):
        run_block(s * sub, sub)
        return carry

    n_full = cnt // sub
    lax.fori_loop(0, n_full, body, 0)
    rem = cnt - n_full * sub
    lo = 0
    for rows in MOE_BLOCK_ROWS:
        pl.when((rem > lo) & (rem <= rows))(functools.partial(run_block, n_full * sub, rows))
        lo = rows


def moe_residual(x, g, w_router_t, w_gu, w_down, li, tm, sub):
    n, d = x.shape
    _, ne, _, dff2 = w_gu.shape
    assert MOE_BLOCK_ROWS[-1] == sub
    return pl.pallas_call(
        functools.partial(_moe_kernel, sub=sub),
        out_shape=jax.ShapeDtypeStruct((n, d), F32),
        grid=(n // tm, ne),
        in_specs=[
            pl.BlockSpec((tm, d), lambda i, e: (i, 0)),
            _full((1, d)),
            _full((ne, d)),
            pl.BlockSpec((None, 1, d, dff2), lambda i, e: (li, e, 0, 0)),
            pl.BlockSpec((None, 1, dff2 // 2, d), lambda i, e: (li, e, 0, 0)),
        ],
        out_specs=pl.BlockSpec((tm, d), lambda i, e: (i, 0)),
        scratch_shapes=[
            pltpu.VMEM((tm, d), BF16),
            pltpu.VMEM((ne, tm), F32),
            pltpu.VMEM((ne, tm), F32),
        ],
        compiler_params=_params("parallel", "arbitrary"),
        name="moe",
    )(x, g.reshape(1, d), w_router_t, w_gu, w_down)


def _trunk(x, n_t, rb, tb, nb, h0, conv0, s0, re0, im0, w, moe_bf16):
    n, d = x.shape
    depth = w['norm_mix'].shape[0]
    d_a = h0.shape[-1]
    nh, dk = s0.shape[2], s0.shape[3]
    d_b = nh * dk
    ns = re0.shape[-2] * re0.shape[-1]
    tm = tb * rb
    n_even, n_odd = (depth + 1) // 2, depth // 2
    new = {k: [] for k in ('h', 'conv', 're', 'im')}
    s_stack = jnp.zeros((n_even,) + s0.shape[1:], F32)
    for l in range(depth):
        li = l // 2
        if l % 2 == 0:
            proj = norm_matmul(x, w['norm_mix'][l], w['even_w_in'], li, tm, w['even_w_in'].shape[2])
            conv_tm = conv0[li].transpose(1, 0, 2).reshape((CONV_W - 1) * rb, d_a)
            ya, h_new, conv_new = rglru(proj, n_t, conv_tm, h0[li], w['rglru_conv_w'][li], w['rglru_conv_b'][li],
                                        w['rglru_wg'][li], w['rglru_bg'][li], w['rglru_lambda'][li], tb=tb, rb=rb)
            new['h'].append(h_new)
            new['conv'].append(conv_new.reshape(CONV_W - 1, rb, d_a).transpose(1, 0, 2))
            cast = () if ('even', li) in moe_bf16 else (w['even_w_out'], w['ffn_w_gu'], w['ffn_w_down'])
            ob, s_stack, *cast_out = hgrn2(proj, s0, li % s0.shape[0], s_stack, n_even, w['hgrn2_lb_raw'],
                                           w['hgrn2_gnorm'][li], li, n_t=n_t, tb=tb, nb=nb, cast=cast)
            if cast_out:
                moe_bf16[('even', li)] = tuple(a[None] for a in cast_out)
            w_out_bf, w_gu_bf, w_down_bf = moe_bf16[('even', li)]
            if nb != rb:
                ob = ob.reshape(rb // nb, n // rb, nb, d_b).transpose(1, 0, 2, 3).reshape(n, d_b)
            x = mix_ffn(ya, ob, x, w_out_bf, w['norm_ffn'][l], w_gu_bf, w_down_bf, 0, tm, w_down_bf.shape[1] // 2)
        else:
            cast = () if li in moe_bf16 else (w['moe_w_gu'].reshape(n_odd, -1, w['moe_w_gu'].shape[-1]),
                                              w['moe_w_down'].reshape(n_odd, -1, d), w['s5_w_glu'])
            z, re_new, im_new, *cast_out = s5(x, n_t, w['norm_mix'][l], w['s5_wb'], w['s5_wc'], li, w['s5_a_re'][li],
                                              w['s5_a_im'][li], w['s5_d'][li], re0[li].reshape(rb, ns),
                                              im0[li].reshape(rb, ns), tb=tb, rb=rb, cast=cast)
            if cast_out:
                moe_bf16[li] = tuple(a.reshape((1,) + f.shape[1:]) for a, f in
                                     zip(cast_out, (w['moe_w_gu'], w['moe_w_down'], w['s5_w_glu'])))
            w_gu_bf, w_down_bf, w_glu_bf = moe_bf16[li]
            new['re'].append(re_new.reshape(re0.shape[1:]))
            new['im'].append(im_new.reshape(im0.shape[1:]))
            x = glu_residual(z, w_glu_bf, 0, x, tm, w_glu_bf.shape[2] // 2)
            x = moe_residual(x, w['norm_ffn'][l], w['moe_w_router_t'][li], w_gu_bf, w_down_bf, 0,
                             tm, MOE_BLOCK_ROWS[-1])
    stack = lambda k: jnp.stack(new[k])
    return x, (stack('h'), stack('conv'), s_stack, stack('re'), stack('im'))


def kernel(x_prompt, x_sample, state_rglru_h, state_rglru_conv, state_hgrn2, state_s5_re, state_s5_im,
           meta_tokens, norm_mix, norm_ffn, norm_final, even_w_in, even_w_out,
           rglru_conv_w, rglru_conv_b, rglru_w_a, rglru_b_a, rglru_w_x, rglru_b_x, rglru_lambda,
           hgrn2_lb_raw, hgrn2_gnorm, s5_lam_re, s5_lam_im, s5_log_dt, s5_b_re, s5_b_im, s5_c_re, s5_c_im,
           s5_d, s5_w_glu, ffn_w_gu, ffn_w_down, moe_w_router, moe_w_gu, moe_w_down):
    bp, tp0, d = x_prompt.shape
    bs, ts, _ = x_sample.shape
    tp = tp0 + N_META
    d_a = state_rglru_h.shape[-1]
    n_even, n_odd = state_rglru_h.shape[0], state_s5_re.shape[0]
    assert bp == SUBLANES and bs % SUBLANES == 0
    tb_p = PROMPT_BLOCK_ROWS // bp
    tp_pad = -(-tp // tb_p) * tb_p
    nb_s = TIME_CHUNK_ROWS // ts
    assert (tp % (TIME_CHUNK_ROWS // bp) == 0 and tb_p % (TIME_CHUNK_ROWS // bp) == 0 and bs % nb_s == 0
            and nb_s * ts == TIME_CHUNK_ROWS and nb_s % SUBLANES == 0)

    eye_a = jnp.eye(H_A, dtype=F32)
    block_diag = lambda m: jnp.einsum('lhij,hg->lhigj', m, eye_a).reshape(n_even, d_a, d_a)
    s5_g, s5_p, s5_c = s5_b_re.shape[1:]
    lg = n_odd * s5_g
    a_re, a_im, bt_re, bt_im = s5_prep(s5_lam_re.reshape(lg, s5_p), s5_lam_im.reshape(lg, s5_p), s5_log_dt.reshape(lg),
                                       s5_b_re.reshape(lg, s5_p, s5_c), s5_b_im.reshape(lg, s5_p, s5_c))
    s5_wb = jnp.concatenate([_s5_block_diag_in(bt_re.reshape(lg, s5_p, s5_c)),
                             _s5_block_diag_in(bt_im.reshape(lg, s5_p, s5_c))], axis=2).astype(BF16)
    s5_wc = jnp.concatenate([_s5_block_diag_out(s5_c_re.reshape(lg, s5_c, s5_p)),
                             -_s5_block_diag_out(s5_c_im.reshape(lg, s5_c, s5_p))], axis=1).astype(BF16)
    ns = s5_lam_re.shape[1] * s5_lam_re.shape[2]
    w = {
        'norm_mix': norm_mix, 'norm_ffn': norm_ffn, 'norm_final': norm_final,
        'even_w_in': even_w_in.astype(BF16), 'even_w_out': even_w_out,
        'rglru_conv_w': rglru_conv_w, 'rglru_conv_b': rglru_conv_b,
        'rglru_wg': jnp.concatenate([block_diag(rglru_w_a), block_diag(rglru_w_x)], axis=2).astype(BF16),
        'rglru_bg': jnp.concatenate([rglru_b_a, rglru_b_x], axis=1),
        'rglru_lambda': rglru_lambda, 'hgrn2_lb_raw': hgrn2_lb_raw, 'hgrn2_gnorm': hgrn2_gnorm,
        's5_a_re': a_re.reshape(n_odd, 1, ns), 's5_a_im': a_im.reshape(n_odd, 1, ns),
        's5_wb': s5_wb.reshape((n_odd, -1) + s5_wb.shape[1:]), 's5_wc': s5_wc.reshape((n_odd, -1) + s5_wc.shape[1:]),
        's5_d': s5_d, 's5_w_glu': s5_w_glu,
        'ffn_w_gu': ffn_w_gu, 'ffn_w_down': ffn_w_down,
        'moe_w_router_t': moe_w_router.transpose(0, 2, 1),
        'moe_w_gu': moe_w_gu, 'moe_w_down': moe_w_down,
    }

    moe_bf16 = {}
    tc_p = TIME_CHUNK_ROWS // bp
    assert N_META == tc_p
    xm = to_time_major(x_prompt, meta_tokens.astype(x_prompt.dtype), tp_pad, tb_p // tc_p)
    zero = lambda ref, lead: jnp.zeros((lead, bp) + ref.shape[2:], ref.dtype)
    xp, p_new = _trunk(xm, tp, bp, tb_p, bp, zero(state_rglru_h, n_even), zero(state_rglru_conv, n_even),
                       zero(state_hgrn2, 1), zero(state_s5_re, n_odd), zero(state_s5_im, n_odd), w, moe_bf16)
    nk_out = max(k for k in range(1, TIME_CHUNK_ROWS // tc_p + 1) if tp0 % (k * tc_p) == 0)
    y_prompt = rmsnorm_batch_major(xp, norm_final, bp, N_META, tp0, tc_p, nk_out)

    xs = x_sample.transpose(1, 0, 2).reshape(ts * bs, d)
    xs, s_new = _trunk(xs, ts, bs, ts, nb_s, state_rglru_h, state_rglru_conv, state_hgrn2, state_s5_re, state_s5_im, w, moe_bf16)
    y_sample = rmsnorm_rows(xs, norm_final, ts * bs).reshape(ts, bs, d).transpose(1, 0, 2)

    refs = (state_rglru_h, state_rglru_conv, state_hgrn2, state_s5_re, state_s5_im)
    cast = lambda new: tuple(a.astype(r.dtype) for a, r in zip(new, refs))
    return (y_prompt, y_sample) + cast(p_new) + cast(s_new)
```

```python
import functools

import jax
import jax.numpy as jnp
from jax import lax
from jax.experimental import pallas as pl
from jax.experimental.pallas import tpu as pltpu

F32 = jnp.float32
BF16 = jnp.bfloat16

EPS = 1e-6
N_META = 16
CONV_W = 4
RG_C = 8.0
H_A = 8

V7X_VMEM_LIMIT_BYTES = 56 * 1024 * 1024
SUBLANES = 8
LANES = 128
V7X_MXU_DIM = 256
MOE_BLOCK_ROWS = (128, 160, 192, 224, 256, 288, 320)
TIME_CHUNK_ROWS = 128
HGRN_ROWS = 256
S5_MAX_SUB_ROWS = 384
HGRN_EXP_CLAMP = 80.0
PROMPT_BLOCK_ROWS = 768
SCAN_CARRY_ELEMS = 16 * SUBLANES * LANES


def _params(*sem):
    return pltpu.CompilerParams(dimension_semantics=sem, vmem_limit_bytes=V7X_VMEM_LIMIT_BYTES)


def _rms(x, g):
    ms = jnp.mean(x * x, axis=-1, keepdims=True)
    return x * lax.rsqrt(ms + EPS) * g


def _dot(a, b):
    return jnp.dot(a, b, preferred_element_type=F32)


def _full(shape):
    return pl.BlockSpec(shape, lambda *_: (0,) * len(shape))


def _ride_along_cast_specs(cast, layer, steps, step_axis):
    ins, outs, shapes = [], [], []
    for a in cast:
        _, r, c = a.shape
        blk = min(k for k in range(2 * SUBLANES, r + 1, 2 * SUBLANES) if r % k == 0 and k * steps >= r)
        last = r // blk - 1
        ins.append(pl.BlockSpec((None, blk, c), lambda *g, last=last: (layer, jnp.minimum(g[step_axis], last), 0)))
        outs.append(pl.BlockSpec((blk, c), lambda *g, last=last: (jnp.minimum(g[step_axis], last), 0)))
        shapes.append(jax.ShapeDtypeStruct((r, c), BF16))
    return ins, outs, shapes


def _ride_along_cast(src_refs, dst_refs):
    for src_ref, dst_ref in zip(src_refs, dst_refs):
        dst_ref[...] = src_ref[...].astype(dst_ref.dtype)


def _norm_matmul_kernel(x_ref, g_ref, w_ref, o_ref, xn_ref, *, single_step):
    if single_step:
        o_ref[...] = _dot(_rms(x_ref[...], g_ref[...]).astype(BF16), w_ref[...]).astype(o_ref.dtype)
        return

    @pl.when(pl.program_id(1) == 0)
    def _():
        xn_ref[...] = _rms(x_ref[...], g_ref[...]).astype(BF16)

    o_ref[...] = _dot(xn_ref[...], w_ref[...]).astype(o_ref.dtype)


def norm_matmul(x, g, w, li, tm, tn):
    n, d = x.shape
    nout = w.shape[2]
    return pl.pallas_call(
        functools.partial(_norm_matmul_kernel, single_step=nout == tn),
        out_shape=jax.ShapeDtypeStruct((n, nout), BF16),
        grid=(n // tm, nout // tn),
        in_specs=[
            pl.BlockSpec((tm, d), lambda i, j: (i, 0)),
            _full((1, d)),
            pl.BlockSpec((None, d, tn), lambda i, j: (li, 0, j)),
        ],
        out_specs=pl.BlockSpec((tm, tn), lambda i, j: (i, j)),
        scratch_shapes=[pltpu.VMEM((tm, d), BF16)],
        compiler_params=_params("parallel", "arbitrary"),
        name="norm_matmul",
    )(x, g.reshape(1, d), w)


def _mix_ffn_kernel(ya_ref, ob_ref, x_ref, wo_ref, g_ref, wg_ref, wu_ref, wdn_ref, o_ref, xn_ref):
    j = pl.program_id(1)
    d_a = ya_ref.shape[1]

    @pl.when(j == 0)
    def _():
        x1 = x_ref[...] + _dot(ya_ref[...], wo_ref[:d_a, :]) + _dot(ob_ref[...], wo_ref[d_a:, :])
        xn_ref[...] = _rms(x1, g_ref[...]).astype(BF16)
        o_ref[...] = x1

    xn = xn_ref[...]
    h = (jax.nn.silu(_dot(xn, wg_ref[...])) * _dot(xn, wu_ref[...])).astype(BF16)
    o_ref[...] += _dot(h, wdn_ref[...])


def mix_ffn(ya, ob, x, w_out, g, w_gu, w_down, li, tm, chunk):
    n, d = x.shape
    dff = w_down.shape[1]
    assert dff % chunk == 0 and chunk % LANES == 0
    nj = dff // chunk
    rows = lambda a: pl.BlockSpec((tm, a.shape[1]), lambda i, j: (i, 0))
    return pl.pallas_call(
        _mix_ffn_kernel,
        out_shape=jax.ShapeDtypeStruct((n, d), F32),
        grid=(n // tm, nj),
        in_specs=[
            rows(ya), rows(ob), rows(x),
            pl.BlockSpec((None,) + w_out.shape[1:], lambda i, j: (li, 0, 0)),
            _full((1, d)),
            pl.BlockSpec((None, d, chunk), lambda i, j: (li, 0, j)),
            pl.BlockSpec((None, d, chunk), lambda i, j: (li, 0, j + nj)),
            pl.BlockSpec((None, chunk, d), lambda i, j: (li, j, 0)),
        ],
        out_specs=pl.BlockSpec((tm, d), lambda i, j: (i, 0)),
        scratch_shapes=[pltpu.VMEM((tm, d), BF16)],
        compiler_params=_params("parallel", "arbitrary"),
        name="mix_ffn",
    )(ya, ob, x, w_out, g.reshape(1, d), w_gu, w_gu, w_down)


def _glu_residual_kernel(z_ref, wv_ref, wg_ref, r_ref, o_ref):
    z = z_ref[...]
    val = _dot(z, wv_ref[...])
    gate = _dot(z, wg_ref[...])
    o_ref[...] = r_ref[...] + val * jax.nn.sigmoid(gate)


def glu_residual(z, w_glu, li, res, tm, tn):
    n, d = z.shape
    dout = w_glu.shape[2] // 2
    nj = dout // tn
    return pl.pallas_call(
        _glu_residual_kernel,
        out_shape=jax.ShapeDtypeStruct((n, dout), F32),
        grid=(n // tm, nj),
        in_specs=[
            pl.BlockSpec((tm, d), lambda i, j: (i, 0)),
            pl.BlockSpec((None, d, tn), lambda i, j: (li, 0, j)),
            pl.BlockSpec((None, d, tn), lambda i, j: (li, 0, j + nj)),
            pl.BlockSpec((tm, tn), lambda i, j: (i, j)),
        ],
        out_specs=pl.BlockSpec((tm, tn), lambda i, j: (i, j)),
        compiler_params=_params("parallel", "arbitrary"),
        name="glu_residual",
    )(z, w_glu, w_glu, res)


def _rmsnorm_kernel(x_ref, g_ref, o_ref):
    o_ref[...] = _rms(x_ref[...], g_ref[...])


def rmsnorm_rows(x, g, tm):
    n, d = x.shape
    return pl.pallas_call(
        _rmsnorm_kernel,
        out_shape=jax.ShapeDtypeStruct((n, d), F32),
        grid=(n // tm,),
        in_specs=[pl.BlockSpec((tm, d), lambda i: (i, 0)), _full((1, d))],
        out_specs=pl.BlockSpec((tm, d), lambda i: (i, 0)),
        compiler_params=_params("parallel"),
        name="final_rmsnorm",
    )(x, g.reshape(1, d))


def _rmsnorm_batch_major_kernel(*refs, nk, tc, nb):
    x_refs, g_ref, o_ref = refs[:nk], refs[nk], refs[nk + 1]
    d = o_ref.shape[-1]
    for k in range(nk):
        y = _rms(x_refs[k][...], g_ref[...])
        o_ref[:, k * tc:(k + 1) * tc, :] = jnp.swapaxes(y.reshape(tc, nb, d), 0, 1)


def rmsnorm_batch_major(x, g, nb, t_skip, t_out, tc, nk):
    n, d = x.shape
    rows = tc * nb
    assert t_skip % tc == 0 and t_out % (nk * tc) == 0
    specs = [pl.BlockSpec((rows, d), lambda j, k=k: (nk * j + t_skip // tc + k, 0)) for k in range(nk)]
    return pl.pallas_call(
        functools.partial(_rmsnorm_batch_major_kernel, nk=nk, tc=tc, nb=nb),
        out_shape=jax.ShapeDtypeStruct((nb, t_out, d), F32),
        grid=(t_out // (nk * tc),),
        in_specs=specs + [_full((1, d))],
        out_specs=pl.BlockSpec((nb, nk * tc, d), lambda j: (0, j, 0)),
        compiler_params=_params("parallel"),
        name="final_rmsnorm_batch_major",
    )(*([x] * nk), g.reshape(1, d))


def _to_time_major_kernel(*refs, nk, n_chunks):
    x_refs, lead_ref, o_ref = refs[:nk], refs[nk], refs[nk + 1]
    i = pl.program_id(0)
    nb, tc, d = x_refs[0].shape
    rows = tc * nb
    for k in range(nk):
        chunk = i * nk + k - 1
        val = jnp.swapaxes(x_refs[k][...], 0, 1).reshape(rows, d)
        if k == 0:
            lead = jnp.broadcast_to(lead_ref[...][:, None, :], (tc, nb, d)).reshape(rows, d)
            val = jnp.where(i == 0, lead, val)
        o_ref[k * rows:(k + 1) * rows, :] = jnp.where(chunk < n_chunks, val, 0.0)


def to_time_major(x, lead, t_pad, nk):
    nb, t, d = x.shape
    tc = lead.shape[0]
    assert t % tc == 0 and t_pad % (nk * tc) == 0
    n_chunks = t // tc
    specs = [pl.BlockSpec((nb, tc, d), lambda i, k=k: (0, jnp.clip(i * nk + k - 1, 0, n_chunks - 1), 0))
             for k in range(nk)]
    return pl.pallas_call(
        functools.partial(_to_time_major_kernel, nk=nk, n_chunks=n_chunks),
        out_shape=jax.ShapeDtypeStruct((t_pad * nb, d), F32),
        grid=(t_pad // (nk * tc),),
        in_specs=specs + [_full((tc, d))],
        out_specs=pl.BlockSpec((nk * tc * nb, d), lambda i: (i, 0)),
        compiler_params=_params("parallel"),
        name="to_time_major",
    )(*([x] * nk), lead)


def _rglru_kernel(xa_ref, ga_ref, conv0_ref, h0_ref, cw_ref, cb_ref, wg_ref, bg_ref, lam_ref,
                  ya_ref, hlast_ref, convnew_ref, xpad_ref, a_ref, u_ref, *, n_t, tb, rb):
    i = pl.program_id(0)
    rows = tb * rb
    tail = (CONV_W - 1) * rb
    c = xa_ref.shape[-1]
    t_valid = jnp.minimum(n_t - i * tb, tb)

    @pl.when(i == 0)
    def _():
        xpad_ref[0:tail, :] = conv0_ref[...]
        hlast_ref[...] = h0_ref[...]

    @pl.when(i > 0)
    def _():
        xpad_ref[0:tail, :] = xpad_ref[rows:rows + tail, :]

    xpad_ref[tail:tail + rows, :] = xa_ref[...].astype(F32)
    xc = cb_ref[...]
    for k in range(CONV_W):
        xc = xc + cw_ref[k:k + 1, :] * xpad_ref[k * rb:k * rb + rows, :]

    gates = _dot(xc.astype(BF16), wg_ref[...]) + bg_ref[...]
    r = jax.nn.sigmoid(gates[:, :c])
    ig = jax.nn.sigmoid(gates[:, c:])
    log_a = (-RG_C) * r * jax.nn.softplus(-lam_ref[...])
    a = jnp.exp(log_a)
    y = 1.0 - a * a
    mult = y * lax.rsqrt(jnp.maximum(y, 1e-30))
    a_ref[...] = a
    u_ref[...] = mult * ig * xc

    lc = min(c, max(LANES, SCAN_CARRY_ELEMS // rb // LANES * LANES))
    for c0 in range(0, c, lc):
        def body(t, h, c0=c0):
            sl = pl.ds(pl.multiple_of(t * rb, rb), rb)
            h = a_ref[sl, c0:c0 + lc] * h + u_ref[sl, c0:c0 + lc]
            u_ref[sl, c0:c0 + lc] = h
            return h

        hlast_ref[:, c0:c0 + lc] = lax.fori_loop(0, t_valid, body, hlast_ref[:, c0:c0 + lc])

    live = lax.broadcasted_iota(jnp.int32, (rows, c), 0) < t_valid * rb
    ya_ref[...] = jnp.where(live, u_ref[...] * jax.nn.gelu(ga_ref[...].astype(F32)), 0.0).astype(ya_ref.dtype)
    convnew_ref[...] = xpad_ref[pl.ds(pl.multiple_of(t_valid * rb, rb), tail), :]


def rglru(proj, n_t, conv0, h0, cw, cb, wg, bg, lam, tb, rb):
    c = h0.shape[1]
    n_rows = proj.shape[0]
    rows = tb * rb
    tail = (CONV_W - 1) * rb
    return pl.pallas_call(
        functools.partial(_rglru_kernel, n_t=n_t, tb=tb, rb=rb),
        out_shape=(
            jax.ShapeDtypeStruct((n_rows, c), BF16),
            jax.ShapeDtypeStruct((rb, c), F32),
            jax.ShapeDtypeStruct((tail, c), F32),
        ),
        grid=(n_rows // rows,),
        in_specs=[
            pl.BlockSpec((rows, c), lambda i: (i, 0)),
            pl.BlockSpec((rows, c), lambda i: (i, 1)),
            _full((tail, c)), _full((rb, c)), _full((CONV_W, c)), _full((1, c)),
            _full((c, 2 * c)), _full((1, 2 * c)), _full((1, c)),
        ],
        out_specs=(
            pl.BlockSpec((rows, c), lambda i: (i, 0)),
            _full((rb, c)),
            _full((tail, c)),
        ),
        scratch_shapes=[
            pltpu.VMEM((rows + tail, c), F32),
            pltpu.VMEM((rows, c), F32),
            pltpu.VMEM((rows, c), F32),
        ],
        compiler_params=_params("arbitrary"),
        name="rglru",
    )(proj, proj, conv0, h0, cw, cb.reshape(1, c), wg, bg.reshape(1, 2 * c), lam.reshape(1, c))


def _hgrn2_kernel(q_ref, f_ref, v_ref, gb_ref, s0_ref, lbraw_ref, gn_ref, stack_ref, *rest,
                  layer, n_t, tb, nb, nh, m):
    del stack_ref
    n_cast = (len(rest) - 3) // 2
    ob_ref, snew_ref = rest[n_cast:n_cast + 2]
    st_ref = rest[-1]
    _ride_along_cast(rest[:n_cast], rest[n_cast + 2:-1])
    i = pl.program_id(1)
    tc = m // nb
    half = tc // 2
    d = q_ref.shape[-1]
    dk = d // nh
    t_valid = jnp.minimum(n_t - i * tb, tb)
    n_chunks = (t_valid + tc - 1) // tc

    @pl.when(i == 0)
    def _():
        for b in range(nb):
            for h in range(nh):
                st_ref[h, :, b * dk:(b + 1) * dk] = s0_ref[b, h].T

    @pl.when(n_chunks < tb // tc)
    def _():
        ob_ref[...] = jnp.zeros_like(ob_ref)

    p = jax.nn.softmax(lbraw_ref[...], axis=0)
    cum = p[0:1, :]
    for r in range(1, layer + 1):
        cum = cum + p[r:r + 1, :]
    lb = cum - p[0:1, :]
    log_lb = jnp.log(lb)
    log_1mlb = jnp.log1p(-lb)

    assert tc % 2 == 0 and nb & (nb - 1) == 0
    row = lax.broadcasted_iota(jnp.int32, (m, m), 0)
    col = lax.broadcasted_iota(jnp.int32, (m, m), 1)
    same_seq_causal = jnp.where(((row & (nb - 1)) == (col & (nb - 1))) & (col <= row), 1.0, 0.0)
    bid = lax.broadcasted_iota(jnp.int32, (m, dk), 0) & (nb - 1)

    def chunk(c, carry):
        def rows_of(ref):
            if len(ref.shape) == 2:
                return ref[pl.ds(pl.multiple_of(c * m, m), m), :]
            return ref[pl.ds(c * tc, tc)].reshape(m, d)

        live = lax.broadcasted_iota(jnp.int32, (m, d), 0) < (t_valid - c * tc) * nb
        q = jax.nn.silu(rows_of(q_ref).astype(F32))
        fr = rows_of(f_ref).astype(F32)
        v = rows_of(v_ref).astype(BF16)
        gb = rows_of(gb_ref).astype(F32)
        logf = jnp.where(live, jnp.logaddexp(log_lb, log_1mlb + jax.nn.log_sigmoid(fr)), 0.0)
        k = 1.0 - jnp.exp(logf)
        slabs = [logf[0:nb]]
        for t in range(1, tc):
            slabs.append(slabs[-1] + logf[t * nb:(t + 1) * nb])
        g = jnp.concatenate(slabs, axis=0)
        g_last = jnp.concatenate([slabs[-1]] * tc, axis=0)
        g_rel = g - jnp.concatenate([slabs[half - 1]] * tc, axis=0)
        qa = (q * jnp.exp(jnp.minimum(g_rel, HGRN_EXP_CLAMP))).astype(BF16)
        kt = (k * jnp.exp(jnp.minimum(-g_rel, HGRN_EXP_CLAMP))).astype(BF16)
        qt = (q * jnp.exp(g)).astype(BF16)
        ks = (k * jnp.exp(g_last - g)).astype(BF16)
        dec = jnp.exp(slabs[-1])

        outs = []
        for h in range(nh):
            hs = slice(h * dk, (h + 1) * dk)
            qt_h, kt_h, ks_h, v_h = qt[:, hs], kt[:, hs], ks[:, hs], v[:, hs]
            att = lax.dot_general(qa[:, hs], kt_h, (((1,), (1,)), ((), ())), preferred_element_type=F32)
            att = (att * same_seq_causal).astype(BF16)
            o = _dot(att, v_h)
            zero = jnp.zeros_like(qt_h)
            expand = lambda x: jnp.concatenate([jnp.where(bid == b, x, zero) for b in range(nb)], axis=1)
            st_h = st_ref[h]
            o = o + lax.dot_general(expand(qt_h), st_h.astype(BF16), (((1,), (1,)), ((), ())),
                                    preferred_element_type=F32)
            dst = lax.dot_general(v_h, expand(ks_h), (((0,), (0,)), ((), ())), preferred_element_type=F32)
            dec_row = jnp.concatenate([dec[b:b + 1, hs] for b in range(nb)], axis=1)
            st_ref[h] = st_h * dec_row + dst
            outs.append(_rms(o, gn_ref[:, hs]))
        ob = jnp.where(live, jnp.concatenate(outs, axis=1) * jax.nn.silu(gb), 0.0)
        ob_ref[pl.ds(pl.multiple_of(c * m, m), m), :] = ob.astype(ob_ref.dtype)
        return carry

    lax.fori_loop(0, n_chunks, chunk, 0)

    @pl.when(i == pl.num_programs(1) - 1)
    def _():
        for b in range(nb):
            for h in range(nh):
                snew_ref[b, h] = st_ref[h, :, b * dk:(b + 1) * dk].T


def hgrn2(proj, s0, s0_layer, s_stack, n_layers, lb_raw, gnorm, layer, n_t, tb, nb, cast=()):
    _, bsz, nh, dk, _ = s0.shape
    d = nh * dk
    off = proj.shape[-1] // d - 4
    nbb = bsz // nb
    n_tpad = proj.shape[0] // bsz
    nt = n_tpad // tb
    m = max(k for k in range(2 * nb, HGRN_ROWS + 1, 2 * nb) if (tb * nb) % k == 0)
    if nb == bsz:
        col = lambda k: pl.BlockSpec((tb * nb, d), lambda j, i, k=k: (i, k + off))
    else:
        proj = proj.reshape(n_tpad, bsz, proj.shape[1])
        col = lambda k: pl.BlockSpec((tb, nb, d), lambda j, i, k=k: (i, j, k + off))
    in_specs = [
        col(0), col(1), col(2), col(3),
        pl.BlockSpec((None, nb, nh, dk, dk), lambda j, i: (s0_layer, j, 0, 0, 0)),
        _full(lb_raw.shape),
        _full((1, d)),
        pl.BlockSpec(memory_space=pl.ANY),
    ]
    args = [proj, proj, proj, proj, s0, lb_raw, gnorm.reshape(1, d), s_stack]
    assert s_stack.shape == (n_layers, bsz, nh, dk, dk) and (not cast or nbb == 1)
    stack_arg = len(args) - 1
    cast_in, cast_out, cast_shapes = _ride_along_cast_specs(cast, layer, nt, 1)
    return pl.pallas_call(
        functools.partial(_hgrn2_kernel, layer=layer, n_t=n_t, tb=tb, nb=nb, nh=nh, m=m),
        out_shape=(
            jax.ShapeDtypeStruct((n_tpad * bsz, d), BF16),
            jax.ShapeDtypeStruct((n_layers, bsz, nh, dk, dk), F32),
            *cast_shapes,
        ),
        grid=(nbb, nt),
        in_specs=in_specs + cast_in,
        out_specs=(
            pl.BlockSpec((tb * nb, d), lambda j, i: (j * nt + i, 0)),
            pl.BlockSpec((None, nb, nh, dk, dk), lambda j, i: (layer, j, 0, 0, 0)),
            *cast_out,
        ),
        scratch_shapes=[pltpu.VMEM((nh, dk, nb * dk), F32)],
        input_output_aliases={stack_arg: 1},
        compiler_params=_params("arbitrary", "arbitrary"),
        name="hgrn2",
    )(*args, *cast)


def _s5_discretise(lr, li, ldt):
    dt = jnp.exp(ldt)
    mag = jnp.exp(lr * dt)
    return mag * jnp.cos(li * dt), mag * jnp.sin(li * dt)


def _s5_prep_kernel(lr_ref, li_ref, ldt_ref, lrc_ref, lic_ref, ldtc_ref, bre_ref, bim_ref,
                    ar_ref, ai_ref, ore_ref, oim_ref):
    ar_ref[...], ai_ref[...] = _s5_discretise(lr_ref[...], li_ref[...], ldt_ref[...])
    lr = lrc_ref[...]
    li = lic_ref[...]
    ar, ai = _s5_discretise(lr, li, ldtc_ref[...])
    den = lr * lr + li * li
    cr = ((ar - 1.0) * lr + ai * li) / den
    ci = (ai * lr - (ar - 1.0) * li) / den
    ore_ref[...] = cr * bre_ref[...] - ci * bim_ref[...]
    oim_ref[...] = cr * bim_ref[...] + ci * bre_ref[...]


def s5_prep(lam_re, lam_im, log_dt, b_re, b_im):
    g, p, c = b_re.shape
    dense = lambda a: a.reshape(-1, LANES)
    per_state = lambda a: dense(jnp.broadcast_to(a.reshape(g, -1, 1), (g, p, 1)))
    per_coef = lambda a: dense(jnp.broadcast_to(a.reshape(g, -1, 1), (g, p, c)))
    small = jax.ShapeDtypeStruct((g * p // LANES, LANES), F32)
    big = jax.ShapeDtypeStruct((g * p * c // LANES, LANES), F32)
    ar, ai, ore, oim = pl.pallas_call(
        _s5_prep_kernel, out_shape=(small, small, big, big), name="s5_prep",
    )(per_state(lam_re), per_state(lam_im), per_state(log_dt), per_coef(lam_re), per_coef(lam_im), per_coef(log_dt),
      dense(b_re), dense(b_im))
    return ar.reshape(g, p), ai.reshape(g, p), ore.reshape(g, p, c), oim.reshape(g, p, c)


def _s5_kernel(x_ref, g_ref, wb_ref, wc_ref, ar_ref, ai_ref, d_ref, h0r_ref, h0i_ref, *rest, n_t, tb, rb, m):
    n_cast = (len(rest) - 3) // 2
    z_ref, hr_ref, hi_ref = rest[n_cast:n_cast + 3]
    _ride_along_cast(rest[:n_cast], rest[n_cast + 3:])
    i = pl.program_id(0)
    nch = wb_ref.shape[0]
    cw = wb_ref.shape[1]
    sw = wb_ref.shape[2] // 2
    tsb = m // rb
    nsl = rb // SUBLANES
    n_sub = jnp.minimum(n_t - i * tb, tb) // tsb

    @pl.when(i == 0)
    def _():
        hr_ref[...] = h0r_ref[...]
        hi_ref[...] = h0i_ref[...]

    @pl.when(n_sub < tb // tsb)
    def _():
        z_ref[...] = jnp.zeros_like(z_ref)

    def sub_block(j, carry):
        r0 = pl.multiple_of(j * m, m)
        xn = _rms(x_ref[pl.ds(r0, m), :], g_ref[...])
        xnb = xn.astype(BF16)
        drive = [_dot(xnb[:, c * cw:(c + 1) * cw], wb_ref[c]) for c in range(nch)]
        for c in range(nch):
            cs = slice(c * cw, (c + 1) * cw)
            ss = slice(c * sw, (c + 1) * sw)
            u = xn[:, cs]
            bu = drive[c]
            ar = jnp.broadcast_to(ar_ref[:, ss], (SUBLANES, sw))
            ai = jnp.broadcast_to(ai_ref[:, ss], (SUBLANES, sw))
            out_r = [None] * (tsb * nsl)
            out_i = [None] * (tsb * nsl)
            for s in range(nsl):
                srow = slice(s * SUBLANES, (s + 1) * SUBLANES)
                hr, hi = hr_ref[srow, ss], hi_ref[srow, ss]
                for t in range(tsb):
                    lo = t * rb + s * SUBLANES
                    hr, hi = (ar * hr - ai * hi + bu[lo:lo + SUBLANES, :sw],
                              ar * hi + ai * hr + bu[lo:lo + SUBLANES, sw:])
                    out_r[t * nsl + s] = hr
                    out_i[t * nsl + s] = hi
                hr_ref[srow, ss] = hr
                hi_ref[srow, ss] = hi
            hcat = jnp.concatenate([jnp.concatenate(out_r, axis=0).astype(BF16),
                                    jnp.concatenate(out_i, axis=0).astype(BF16)], axis=1)
            y = _dot(hcat, wc_ref[c]) + d_ref[:, cs] * u
            z_ref[pl.ds(r0, m), cs] = jax.nn.gelu(y).astype(z_ref.dtype)
        return carry

    lax.fori_loop(0, n_sub, sub_block, 0)


def s5(x, n_t, g, wb, wc, li, a_re, a_im, dskip, h0r, h0i, tb, rb, cast=()):
    n_rows, d = x.shape
    ns = h0r.shape[1]
    rows = tb * rb
    steps = n_rows // rows
    cast_in, cast_out, cast_shapes = _ride_along_cast_specs(cast, li, steps, 0)
    m = max(k for k in range(rb, S5_MAX_SUB_ROWS + 1, rb)
            if rows % k == 0 and n_t % (k // rb) == 0 and k % (2 * SUBLANES) == 0)
    return pl.pallas_call(
        functools.partial(_s5_kernel, n_t=n_t, tb=tb, rb=rb, m=m),
        out_shape=(
            jax.ShapeDtypeStruct((n_rows, d), BF16),
            jax.ShapeDtypeStruct((rb, ns), F32),
            jax.ShapeDtypeStruct((rb, ns), F32),
            *cast_shapes,
        ),
        grid=(steps,),
        in_specs=[
            pl.BlockSpec((rows, d), lambda i: (i, 0)),
            _full((1, d)),
            pl.BlockSpec((None,) + wb.shape[1:], lambda i: (li, 0, 0, 0)),
            pl.BlockSpec((None,) + wc.shape[1:], lambda i: (li, 0, 0, 0)),
            _full((1, ns)), _full((1, ns)), _full((1, d)),
            _full((rb, ns)), _full((rb, ns)),
            *cast_in,
        ],
        out_specs=(
            pl.BlockSpec((rows, d), lambda i: (i, 0)),
            _full((rb, ns)), _full((rb, ns)),
            *cast_out,
        ),
        compiler_params=_params("arbitrary"),
        name="s5",
    )(x, g.reshape(1, d), wb, wc, a_re, a_im, dskip.reshape(1, d), h0r, h0i, *cast)


def _s5_block_diag_in(b):
    g, p, c = b.shape
    gpc = V7X_MXU_DIM // c
    bt = b.transpose(0, 2, 1).reshape(g // gpc, gpc, c, p)
    out = jnp.einsum('ngcp,gh->ngchp', bt, jnp.eye(gpc, dtype=b.dtype))
    return out.reshape(g // gpc, gpc * c, gpc * p)


def _s5_block_diag_out(cm):
    g, c, p = cm.shape
    gpc = V7X_MXU_DIM // c
    ct = cm.transpose(0, 2, 1).reshape(g // gpc, gpc, p, c)
    out = jnp.einsum('ngpc,gh->ngphc', ct, jnp.eye(gpc, dtype=cm.dtype))
    return out.reshape(g // gpc, gpc * p, gpc * c)


def _moe_kernel(x_ref, g_ref, wrt_ref, wgu_ref, wdn_ref, o_ref, xn_ref, pos_ref, gate_ref, *, sub):
    e = pl.program_id(1)
    tm = x_ref.shape[0]
    ne = wrt_ref.shape[0]
    dff = wdn_ref.shape[1]

    @pl.when(e == 0)
    def _():
        x = x_ref[...]
        xn = _rms(x, g_ref[...])
        xn_hi = xn.astype(BF16)
        xn_ref[...] = xn_hi
        xn_lo = (xn - xn_hi.astype(F32)).astype(BF16)
        wr = wrt_ref[...]
        wr_hi = wr.astype(BF16)
        wr_lo = (wr - wr_hi.astype(F32)).astype(BF16)
        nt = lambda a, b: lax.dot_general(a, b, (((1,), (1,)), ((), ())), preferred_element_type=F32)
        logits = nt(wr_hi, xn_hi) + (nt(wr_hi, xn_lo) + nt(wr_lo, xn_hi))
        ex = jnp.exp(logits - jnp.max(logits, axis=0, keepdims=True))
        probs = ex / jnp.sum(ex, axis=0, keepdims=True)
        eid = lax.broadcasted_iota(jnp.int32, (ne, tm), 0).astype(F32)
        m1 = jnp.max(probs, axis=0, keepdims=True)
        i1 = jnp.min(jnp.where(probs == m1, eid, float(ne)), axis=0, keepdims=True)
        sel1 = eid == i1
        rest = jnp.where(sel1, -1.0, probs)
        m2 = jnp.max(rest, axis=0, keepdims=True)
        i2 = jnp.min(jnp.where(rest == m2, eid, float(ne)), axis=0, keepdims=True)
        sel2 = eid == i2
        den = m1 + m2
        gate_ref[...] = jnp.where(sel1, m1 / den, 0.0) + jnp.where(sel2, m2 / den, 0.0)
        chosen = jnp.where(sel1, 1.0, jnp.where(sel2, 1.0, 0.0))
        r = lax.broadcasted_iota(jnp.int32, (tm, tm), 0)
        c = lax.broadcasted_iota(jnp.int32, (tm, tm), 1)
        before = jnp.where(r < c, 1.0, 0.0).astype(BF16)
        rank = _dot(chosen.astype(BF16), before)
        pos_ref[...] = jnp.where(chosen > 0.0, rank, -1.0)
        o_ref[...] = x

    pos_e = pos_ref[pl.ds(e, 1), :]
    gate_e = gate_ref[pl.ds(e, 1), :]
    cnt = jnp.sum(jnp.where(pos_e >= 0.0, 1.0, 0.0)).astype(jnp.int32)

    def run_block(base, rows):
        slot = lax.broadcasted_iota(jnp.int32, (rows, tm), 0) + base
        hit = pos_e == slot.astype(F32)
        onehot = jnp.where(hit, 1.0, 0.0).astype(BF16)
        xs = _dot(onehot, xn_ref[...]).astype(BF16)
        gs = jnp.sum(jnp.where(hit, gate_e, 0.0), axis=1, keepdims=True)
        hgu = _dot(xs, wgu_ref[0])
        act = (jax.nn.silu(hgu[:, :dff]) * hgu[:, dff:]).astype(BF16)
        yb = (_dot(act, wdn_ref[0]) * gs).astype(BF16)
        o_ref[...] += lax.dot_general(onehot, yb, (((0,), (0,)), ((), ())), preferred_element_type=F32)

    def body(s, carry):
        run_block(s * sub, sub)
        return carry

    n_full = cnt // sub
    lax.fori_loop(0, n_full, body, 0)
    rem = cnt - n_full * sub
    lo = 0
    for rows in MOE_BLOCK_ROWS:
        pl.when((rem > lo) & (rem <= rows))(functools.partial(run_block, n_full * sub, rows))
        lo = rows


def moe_residual(x, g, w_router_t, w_gu, w_down, li, tm, sub):
    n, d = x.shape
    _, ne, _, dff2 = w_gu.shape
    assert MOE_BLOCK_ROWS[-1] == sub
    return pl.pallas_call(
        functools.partial(_moe_kernel, sub=sub),
        out_shape=jax.ShapeDtypeStruct((n, d), F32),
        grid=(n // tm, ne),
        in_specs=[
            pl.BlockSpec((tm, d), lambda i, e: (i, 0)),
            _full((1, d)),
            _full((ne, d)),
            pl.BlockSpec((None, 1, d, dff2), lambda i, e: (li, e, 0, 0)),
            pl.BlockSpec((None, 1, dff2 // 2, d), lambda i, e: (li, e, 0, 0)),
        ],
        out_specs=pl.BlockSpec((tm, d), lambda i, e: (i, 0)),
        scratch_shapes=[
            pltpu.VMEM((tm, d), BF16),
            pltpu.VMEM((ne, tm), F32),
            pltpu.VMEM((ne, tm), F32),
        ],
        compiler_params=_params("parallel", "arbitrary"),
        name="moe",
    )(x, g.reshape(1, d), w_router_t, w_gu, w_down)


def _trunk(x, n_t, rb, tb, nb, h0, conv0, s0, re0, im0, w, moe_bf16):
    n, d = x.shape
    depth = w['norm_mix'].shape[0]
    d_a = h0.shape[-1]
    nh, dk = s0.shape[2], s0.shape[3]
    d_b = nh * dk
    ns = re0.shape[-2] * re0.shape[-1]
    tm = tb * rb
    n_even, n_odd = (depth + 1) // 2, depth // 2
    new = {k: [] for k in ('h', 'conv', 're', 'im')}
    s_stack = jnp.zeros((n_even,) + s0.shape[1:], F32)
    for l in range(depth):
        li = l // 2
        if l % 2 == 0:
            proj = norm_matmul(x, w['norm_mix'][l], w['even_w_in'], li, tm, w['even_w_in'].shape[2])
            conv_tm = conv0[li].transpose(1, 0, 2).reshape((CONV_W - 1) * rb, d_a)
            ya, h_new, conv_new = rglru(proj, n_t, conv_tm, h0[li], w['rglru_conv_w'][li], w['rglru_conv_b'][li],
                                        w['rglru_wg'][li], w['rglru_bg'][li], w['rglru_lambda'][li], tb=tb, rb=rb)
            new['h'].append(h_new)
            new['conv'].append(conv_new.reshape(CONV_W - 1, rb, d_a).transpose(1, 0, 2))
            cast = () if ('even', li) in moe_bf16 else (w['even_w_out'], w['ffn_w_gu'], w['ffn_w_down'])
            ob, s_stack, *cast_out = hgrn2(proj, s0, li % s0.shape[0], s_stack, n_even, w['hgrn2_lb_raw'],
                                           w['hgrn2_gnorm'][li], li, n_t=n_t, tb=tb, nb=nb, cast=cast)
            if cast_out:
                moe_bf16[('even', li)] = tuple(a[None] for a in cast_out)
            w_out_bf, w_gu_bf, w_down_bf = moe_bf16[('even', li)]
            if nb != rb:
                ob = ob.reshape(rb // nb, n // rb, nb, d_b).transpose(1, 0, 2, 3).reshape(n, d_b)
            x = mix_ffn(ya, ob, x, w_out_bf, w['norm_ffn'][l], w_gu_bf, w_down_bf, 0, tm, w_down_bf.shape[1] // 2)
        else:
            cast = () if li in moe_bf16 else (w['moe_w_gu'].reshape(n_odd, -1, w['moe_w_gu'].shape[-1]),
                                              w['moe_w_down'].reshape(n_odd, -1, d), w['s5_w_glu'])
            z, re_new, im_new, *cast_out = s5(x, n_t, w['norm_mix'][l], w['s5_wb'], w['s5_wc'], li, w['s5_a_re'][li],
                                              w['s5_a_im'][li], w['s5_d'][li], re0[li].reshape(rb, ns),
                                              im0[li].reshape(rb, ns), tb=tb, rb=rb, cast=cast)
            if cast_out:
                moe_bf16[li] = tuple(a.reshape((1,) + f.shape[1:]) for a, f in
                                     zip(cast_out, (w['moe_w_gu'], w['moe_w_down'], w['s5_w_glu'])))
            w_gu_bf, w_down_bf, w_glu_bf = moe_bf16[li]
            new['re'].append(re_new.reshape(re0.shape[1:]))
            new['im'].append(im_new.reshape(im0.shape[1:]))
            x = glu_residual(z, w_glu_bf, 0, x, tm, w_glu_bf.shape[2] // 2)
            x = moe_residual(x, w['norm_ffn'][l], w['moe_w_router_t'][li], w_gu_bf, w_down_bf, 0,
                             tm, MOE_BLOCK_ROWS[-1])
    stack = lambda k: jnp.stack(new[k])
    return x, (stack('h'), stack('conv'), s_stack, stack('re'), stack('im'))


def kernel(x_prompt, x_sample, state_rglru_h, state_rglru_conv, state_hgrn2, state_s5_re, state_s5_im,
           meta_tokens, norm_mix, norm_ffn, norm_final, even_w_in, even_w_out,
           rglru_conv_w, rglru_conv_b, rglru_w_a, rglru_b_a, rglru_w_x, rglru_b_x, rglru_lambda,
           hgrn2_lb_raw, hgrn2_gnorm, s5_lam_re, s5_lam_im, s5_log_dt, s5_b_re, s5_b_im, s5_c_re, s5_c_im,
           s5_d, s5_w_glu, ffn_w_gu, ffn_w_down, moe_w_router, moe_w_gu, moe_w_down):
    bp, tp0, d = x_prompt.shape
    bs, ts, _ = x_sample.shape
    tp = tp0 + N_META
    d_a = state_rglru_h.shape[-1]
    n_even, n_odd = state_rglru_h.shape[0], state_s5_re.shape[0]
    assert bp == SUBLANES and bs % SUBLANES == 0
    tb_p = PROMPT_BLOCK_ROWS // bp
    tp_pad = -(-tp // tb_p) * tb_p
    nb_s = TIME_CHUNK_ROWS // ts
    assert (tp % (TIME_CHUNK_ROWS // bp) == 0 and tb_p % (TIME_CHUNK_ROWS // bp) == 0 and bs % nb_s == 0
            and nb_s * ts == TIME_CHUNK_ROWS and nb_s % SUBLANES == 0)

    eye_a = jnp.eye(H_A, dtype=F32)
    block_diag = lambda m: jnp.einsum('lhij,hg->lhigj', m, eye_a).reshape(n_even, d_a, d_a)
    s5_g, s5_p, s5_c = s5_b_re.shape[1:]
    lg = n_odd * s5_g
    a_re, a_im, bt_re, bt_im = s5_prep(s5_lam_re.reshape(lg, s5_p), s5_lam_im.reshape(lg, s5_p), s5_log_dt.reshape(lg),
                                       s5_b_re.reshape(lg, s5_p, s5_c), s5_b_im.reshape(lg, s5_p, s5_c))
    s5_wb = jnp.concatenate([_s5_block_diag_in(bt_re.reshape(lg, s5_p, s5_c)),
                             _s5_block_diag_in(bt_im.reshape(lg, s5_p, s5_c))], axis=2).astype(BF16)
    s5_wc = jnp.concatenate([_s5_block_diag_out(s5_c_re.reshape(lg, s5_c, s5_p)),
                             -_s5_block_diag_out(s5_c_im.reshape(lg, s5_c, s5_p))], axis=1).astype(BF16)
    ns = s5_lam_re.shape[1] * s5_lam_re.shape[2]
    w = {
        'norm_mix': norm_mix, 'norm_ffn': norm_ffn, 'norm_final': norm_final,
        'even_w_in': even_w_in.astype(BF16), 'even_w_out': even_w_out,
        'rglru_conv_w': rglru_conv_w, 'rglru_conv_b': rglru_conv_b,
        'rglru_wg': jnp.concatenate([block_diag(rglru_w_a), block_diag(rglru_w_x)], axis=2).astype(BF16),
        'rglru_bg': jnp.concatenate([rglru_b_a, rglru_b_x], axis=1),
        'rglru_lambda': rglru_lambda, 'hgrn2_lb_raw': hgrn2_lb_raw, 'hgrn2_gnorm': hgrn2_gnorm,
        's5_a_re': a_re.reshape(n_odd, 1, ns), 's5_a_im': a_im.reshape(n_odd, 1, ns),
        's5_wb': s5_wb.reshape((n_odd, -1) + s5_wb.shape[1:]), 's5_wc': s5_wc.reshape((n_odd, -1) + s5_wc.shape[1:]),
        's5_d': s5_d, 's5_w_glu': s5_w_glu,
        'ffn_w_gu': ffn_w_gu, 'ffn_w_down': ffn_w_down,
        'moe_w_router_t': moe_w_router.transpose(0, 2, 1),
        'moe_w_gu': moe_w_gu, 'moe_w_down': moe_w_down,
    }

    moe_bf16 = {}
    tc_p = TIME_CHUNK_ROWS // bp
    assert N_META == tc_p
    xm = to_time_major(x_prompt, meta_tokens.astype(x_prompt.dtype), tp_pad, tb_p // tc_p)
    zero = lambda ref, lead: jnp.zeros((lead, bp) + ref.shape[2:], ref.dtype)
    xp, p_new = _trunk(xm, tp, bp, tb_p, bp, zero(state_rglru_h, n_even), zero(state_rglru_conv, n_even),
                       zero(state_hgrn2, 1), zero(state_s5_re, n_odd), zero(state_s5_im, n_odd), w, moe_bf16)
    nk_out = max(k for k in range(1, TIME_CHUNK_ROWS // tc_p + 1) if tp0 % (k * tc_p) == 0)
    y_prompt = rmsnorm_batch_major(xp, norm_final, bp, N_META, tp0, tc_p, nk_out)

    xs = x_sample.transpose(1, 0, 2).reshape(ts * bs, d)
    xs, s_new = _trunk(xs, ts, bs, ts, nb_s, state_rglru_h, state_rglru_conv, state_hgrn2, state_s5_re, state_s5_im, w, moe_bf16)
    y_sample = rmsnorm_rows(xs, norm_final, ts * bs).reshape(ts, bs, d).transpose(1, 0, 2)

    refs = (state_rglru_h, state_rglru_conv, state_hgrn2, state_s5_re, state_s5_im)
    cast = lambda new: tuple(a.astype(r.dtype) for a, r in zip(new, refs))
    return (y_prompt, y_sample) + cast(p_new) + cast(s_new)
```
